```python
import jax, jax.numpy as jnp
from jax import lax
import numpy as np

D_MODEL = 1024
BATCH = 8
SEQ = 8192
DEPTH = 2

N_MIXERS = 2
N_POOL_LAYERS = (DEPTH + N_MIXERS - 1) // N_MIXERS
N_MLA_LAYERS = DEPTH // N_MIXERS
EXPAND = 2
POOL_WIDTH = EXPAND * D_MODEL
POOL_WINDOWS = (2, 4, 8, 16)
N_POOL_GROUPS = len(POOL_WINDOWS)
POOL_GROUP = POOL_WIDTH // N_POOL_GROUPS
N_HEADS = 16
QK_NOPE_DIM = 128
QK_ROPE_DIM = 64
QK_HEAD_DIM = QK_NOPE_DIM + QK_ROPE_DIM
V_HEAD_DIM = 128
Q_LORA_RANK = 384
KV_LORA_RANK = 256
MLA_WIDTH = N_HEADS * V_HEAD_DIM
MLA_IN_WIDTH = Q_LORA_RANK + KV_LORA_RANK + QK_ROPE_DIM + MLA_WIDTH
ROPE_THETA = 10000.0
Q_BLOCK = 128
EPS = 1e-6

kernel_name = "interleaved_pool_mla_gated_trunk"


def rmsnorm(x, g):
    xf = x.astype(jnp.float32)
    inv = lax.rsqrt(jnp.mean(xf * xf, axis=-1, keepdims=True) + EPS)
    return (xf * inv).astype(x.dtype) * g


def causal_window_mean(u, w):
    S = u.shape[1]
    cs = jnp.cumsum(u.astype(jnp.float32), axis=1)
    shifted = jnp.pad(cs, ((0, 0), (w, 0), (0, 0)))[:, :S]
    count = jnp.minimum(jnp.arange(1, S + 1), w).astype(jnp.float32)
    return ((cs - shifted) / count[None, :, None]).astype(u.dtype)


def pool_layer(x, norm_g, w_in, w_group, scale, w_out):
    B, S, _ = x.shape
    h = rmsnorm(x, norm_g)
    uz = h @ w_in
    u, z = uz[..., :POOL_WIDTH], uz[..., POOL_WIDTH:]
    ug = u.reshape(B, S, N_POOL_GROUPS, POOL_GROUP)
    pooled = jnp.stack([causal_window_mean(ug[:, :, g], w)
                        for g, w in enumerate(POOL_WINDOWS)], axis=2) - ug
    mixed = jnp.einsum('bsgc,gcd->bsgd', pooled, w_group).reshape(B, S, POOL_WIDTH) * scale
    y = mixed * jax.nn.silu(z)
    return x + y @ w_out


def rope_tables(positions, dtype):
    inv_freq = 1.0 / (ROPE_THETA ** (jnp.arange(0, QK_ROPE_DIM, 2, dtype=jnp.float32) / QK_ROPE_DIM))
    ang = positions.astype(jnp.float32)[..., None] * inv_freq
    return jnp.cos(ang).astype(dtype), jnp.sin(ang).astype(dtype)


def apply_rope(t, cos, sin):
    half = QK_ROPE_DIM // 2
    t1, t2 = t[..., :half], t[..., half:]
    return jnp.concatenate([t1 * cos - t2 * sin, t2 * cos + t1 * sin], axis=-1)


def causal_mla_attention(q_nope, q_rope, k_nope, k_rope, v):
    B, S, H, _ = q_nope.shape
    nb = S // Q_BLOCK
    scale = QK_HEAD_DIM ** -0.5
    key_pos = jnp.arange(S)

    def to_blocks(t):
        return t.reshape(B, nb, Q_BLOCK, *t.shape[2:]).swapaxes(0, 1)

    def one_block(args):
        qn, qr, start = args
        s = (jnp.einsum('bqhd,bkhd->bhqk', qn, k_nope)
             + jnp.einsum('bqhr,bkr->bhqk', qr, k_rope)).astype(jnp.float32) * scale
        q_pos = start + jnp.arange(Q_BLOCK)
        mask = q_pos[:, None] >= key_pos[None, :]
        s = jnp.where(mask, s, jnp.finfo(jnp.float32).min)
        p = jax.nn.softmax(s, axis=-1).astype(v.dtype)
        return jnp.einsum('bhqk,bkhd->bqhd', p, v)

    starts = jnp.arange(nb, dtype=jnp.int32) * Q_BLOCK
    out = lax.map(one_block, (to_blocks(q_nope), to_blocks(q_rope), starts))
    return out.swapaxes(0, 1).reshape(B, S, H, V_HEAD_DIM)


def mla_layer(x, cos, sin, norm_g, w_in, q_norm_g, w_q_b, kv_norm_g, w_kv_b, w_out):
    B, S, _ = x.shape
    h = rmsnorm(x, norm_g)
    proj = h @ w_in
    o1 = Q_LORA_RANK
    o2 = o1 + KV_LORA_RANK
    o3 = o2 + QK_ROPE_DIM
    q_lat, kv_lat, k_rope, z = proj[..., :o1], proj[..., o1:o2], proj[..., o2:o3], proj[..., o3:]
    q = (rmsnorm(q_lat, q_norm_g) @ w_q_b).reshape(B, S, N_HEADS, QK_HEAD_DIM)
    q_nope, q_rope = q[..., :QK_NOPE_DIM], q[..., QK_NOPE_DIM:]
    kv = (rmsnorm(kv_lat, kv_norm_g) @ w_kv_b).reshape(B, S, N_HEADS, QK_NOPE_DIM + V_HEAD_DIM)
    k_nope, v = kv[..., :QK_NOPE_DIM], kv[..., QK_NOPE_DIM:]
    q_rope = apply_rope(q_rope, cos[:, :, None, :], sin[:, :, None, :])
    k_rope = apply_rope(k_rope, cos, sin)
    o = causal_mla_attention(q_nope, q_rope, k_nope, k_rope, v)
    y = o.reshape(B, S, MLA_WIDTH) * jax.nn.silu(z)
    return x + y @ w_out


def _fwd_setup_inputs(seed: int = 0) -> dict:
    key = jax.random.key(seed)
    ks = jax.random.split(key, 16)
    nrm = jax.random.normal
    Lp, Lm = N_POOL_LAYERS, N_MLA_LAYERS
    x = nrm(ks[0], (BATCH, SEQ, D_MODEL), jnp.float32)
    positions = jnp.broadcast_to(jnp.arange(SEQ, dtype=jnp.int32)[None, :], (BATCH, SEQ))
    return {
        "x": x,
        "positions": positions,
        "pool_norm": 1.0 + 0.02 * nrm(ks[1], (Lp, D_MODEL), jnp.float32),
        "pool_w_in": nrm(ks[2], (Lp, D_MODEL, 2 * POOL_WIDTH), jnp.float32) * D_MODEL ** -0.5,
        "pool_w_group": nrm(ks[3], (Lp, N_POOL_GROUPS, POOL_GROUP, POOL_GROUP), jnp.float32) * POOL_GROUP ** -0.5,
        "pool_scale": 1.0 + 0.02 * nrm(ks[4], (Lp, POOL_WIDTH), jnp.float32),
        "pool_w_out": nrm(ks[5], (Lp, POOL_WIDTH, D_MODEL), jnp.float32) * POOL_WIDTH ** -0.5,
        "mla_norm": 1.0 + 0.02 * nrm(ks[6], (Lm, D_MODEL), jnp.float32),
        "mla_w_in": nrm(ks[7], (Lm, D_MODEL, MLA_IN_WIDTH), jnp.float32) * D_MODEL ** -0.5,
        "mla_q_norm": 1.0 + 0.02 * nrm(ks[8], (Lm, Q_LORA_RANK), jnp.float32),
        "mla_w_q_b": nrm(ks[9], (Lm, Q_LORA_RANK, N_HEADS * QK_HEAD_DIM), jnp.float32) * Q_LORA_RANK ** -0.5,
        "mla_kv_norm": 1.0 + 0.02 * nrm(ks[10], (Lm, KV_LORA_RANK), jnp.float32),
        "mla_w_kv_b": nrm(ks[11], (Lm, KV_LORA_RANK, N_HEADS * (QK_NOPE_DIM + V_HEAD_DIM)), jnp.float32) * KV_LORA_RANK ** -0.5,
        "mla_w_out": nrm(ks[12], (Lm, MLA_WIDTH, D_MODEL), jnp.float32) * MLA_WIDTH ** -0.5,
        "final_norm": 1.0 + 0.02 * nrm(ks[13], (D_MODEL,), jnp.float32),
    }


def _fwd_reference(x, positions, pool_norm, pool_w_in, pool_w_group, pool_scale, pool_w_out,
              mla_norm, mla_w_in, mla_q_norm, mla_w_q_b, mla_kv_norm, mla_w_kv_b, mla_w_out,
              final_norm):
    cos, sin = rope_tables(positions, x.dtype)
    for i in range(DEPTH):
        j = i // N_MIXERS
        if i % N_MIXERS == 0:
            x = pool_layer(x, pool_norm[j], pool_w_in[j], pool_w_group[j], pool_scale[j], pool_w_out[j])
        else:
            x = mla_layer(x, cos, sin, mla_norm[j], mla_w_in[j], mla_q_norm[j], mla_w_q_b[j],
                          mla_kv_norm[j], mla_w_kv_b[j], mla_w_out[j])
    return rmsnorm(x, final_norm)


import jax as _jax
import jax.numpy as _jnp

TWIN_FORMAT = 'train_step'
FWD_PARAMS = ['x', 'positions', 'pool_norm', 'pool_w_in', 'pool_w_group', 'pool_scale', 'pool_w_out', 'mla_norm', 'mla_w_in', 'mla_q_norm', 'mla_w_q_b', 'mla_kv_norm', 'mla_w_kv_b', 'mla_w_out', 'final_norm']
TWIN_WEIGHTS = ['pool_norm', 'pool_w_in', 'pool_w_group', 'pool_scale', 'pool_w_out', 'mla_norm', 'mla_w_in', 'mla_q_norm', 'mla_w_q_b', 'mla_kv_norm', 'mla_w_kv_b', 'mla_w_out', 'final_norm']
TWIN_DIFF_INPUT = 'x'
TWIN_INPUTS = ['x', 'positions', 'pool_norm', 'pool_w_in', 'pool_w_group', 'pool_scale', 'pool_w_out', 'mla_norm', 'mla_w_in', 'mla_q_norm', 'mla_w_q_b', 'mla_kv_norm', 'mla_w_kv_b', 'mla_w_out', 'final_norm', 'loss_target', 'm_pool_norm', 'm_pool_w_in', 'm_pool_w_group', 'm_pool_scale', 'm_pool_w_out', 'm_mla_norm', 'm_mla_w_in', 'm_mla_q_norm', 'm_mla_w_q_b', 'm_mla_kv_norm', 'm_mla_w_kv_b', 'm_mla_w_out', 'm_final_norm', 'v_pool_norm', 'v_pool_w_in', 'v_pool_w_group', 'v_pool_scale', 'v_pool_w_out', 'v_mla_norm', 'v_mla_w_in', 'v_mla_q_norm', 'v_mla_w_q_b', 'v_mla_kv_norm', 'v_mla_w_kv_b', 'v_mla_w_out', 'v_final_norm']
TWIN_OUTPUTS = ['loss', 'grad_x', 'grad_pool_norm', 'grad_pool_w_in', 'grad_pool_w_group', 'grad_pool_scale', 'grad_pool_w_out', 'grad_mla_norm', 'grad_mla_w_in', 'grad_mla_q_norm', 'grad_mla_w_q_b', 'grad_mla_kv_norm', 'grad_mla_w_kv_b', 'grad_mla_w_out', 'grad_final_norm', 'delta_pool_norm', 'delta_pool_w_in', 'delta_pool_w_group', 'delta_pool_scale', 'delta_pool_w_out', 'delta_mla_norm', 'delta_mla_w_in', 'delta_mla_q_norm', 'delta_mla_w_q_b', 'delta_mla_kv_norm', 'delta_mla_w_kv_b', 'delta_mla_w_out', 'delta_final_norm', 'new_m_pool_norm', 'new_m_pool_w_in', 'new_m_pool_w_group', 'new_m_pool_scale', 'new_m_pool_w_out', 'new_m_mla_norm', 'new_m_mla_w_in', 'new_m_mla_q_norm', 'new_m_mla_w_q_b', 'new_m_mla_kv_norm', 'new_m_mla_w_kv_b', 'new_m_mla_w_out', 'new_m_final_norm', 'new_v_pool_norm', 'new_v_pool_w_in', 'new_v_pool_w_group', 'new_v_pool_scale', 'new_v_pool_w_out', 'new_v_mla_norm', 'new_v_mla_w_in', 'new_v_mla_q_norm', 'new_v_mla_w_q_b', 'new_v_mla_kv_norm', 'new_v_mla_w_kv_b', 'new_v_mla_w_out', 'new_v_final_norm']
TWIN_LEAF_KINDS = {'loss': 'loss', 'grad_x': 'grad_x', 'grad_pool_norm': 'grad_w', 'grad_pool_w_in': 'grad_w', 'grad_pool_w_group': 'grad_w', 'grad_pool_scale': 'grad_w', 'grad_pool_w_out': 'grad_w', 'grad_mla_norm': 'grad_w', 'grad_mla_w_in': 'grad_w', 'grad_mla_q_norm': 'grad_w', 'grad_mla_w_q_b': 'grad_w', 'grad_mla_kv_norm': 'grad_w', 'grad_mla_w_kv_b': 'grad_w', 'grad_mla_w_out': 'grad_w', 'grad_final_norm': 'grad_w', 'delta_pool_norm': 'delta_w', 'delta_pool_w_in': 'delta_w', 'delta_pool_w_group': 'delta_w', 'delta_pool_scale': 'delta_w', 'delta_pool_w_out': 'delta_w', 'delta_mla_norm': 'delta_w', 'delta_mla_w_in': 'delta_w', 'delta_mla_q_norm': 'delta_w', 'delta_mla_w_q_b': 'delta_w', 'delta_mla_kv_norm': 'delta_w', 'delta_mla_w_kv_b': 'delta_w', 'delta_mla_w_out': 'delta_w', 'delta_final_norm': 'delta_w', 'new_m_pool_norm': 'new_m', 'new_m_pool_w_in': 'new_m', 'new_m_pool_w_group': 'new_m', 'new_m_pool_scale': 'new_m', 'new_m_pool_w_out': 'new_m', 'new_m_mla_norm': 'new_m', 'new_m_mla_w_in': 'new_m', 'new_m_mla_q_norm': 'new_m', 'new_m_mla_w_q_b': 'new_m', 'new_m_mla_kv_norm': 'new_m', 'new_m_mla_w_kv_b': 'new_m', 'new_m_mla_w_out': 'new_m', 'new_m_final_norm': 'new_m', 'new_v_pool_norm': 'new_v', 'new_v_pool_w_in': 'new_v', 'new_v_pool_w_group': 'new_v', 'new_v_pool_scale': 'new_v', 'new_v_pool_w_out': 'new_v', 'new_v_mla_norm': 'new_v', 'new_v_mla_w_in': 'new_v', 'new_v_mla_q_norm': 'new_v', 'new_v_mla_w_q_b': 'new_v', 'new_v_mla_kv_norm': 'new_v', 'new_v_mla_w_kv_b': 'new_v', 'new_v_mla_w_out': 'new_v', 'new_v_final_norm': 'new_v'}


def _forward(args):
    return _fwd_reference(*[args[k] for k in FWD_PARAMS])


def _output_shape():
    def fwd():
        inp = _fwd_setup_inputs(0)
        return _fwd_reference(*[inp[k] for k in FWD_PARAMS])
    out = _jax.eval_shape(fwd)
    return out.shape, out.dtype

N_MICROBATCH = 1
ADAM_LR = 0.001
ADAM_B1 = 0.9
ADAM_B2 = 0.999
ADAM_EPS = 1e-08
ADAM_WD = 0.01
ADAM_STEP = 10
PER_EXAMPLE_BATCH_AXIS = {'x': 0, 'positions': 0, 'loss_target': 0}
SHARED_INPUTS = []
_WEIGHT_DTYPES = {'pool_norm': _jnp.float32, 'pool_w_in': _jnp.float32, 'pool_w_group': _jnp.float32, 'pool_scale': _jnp.float32, 'pool_w_out': _jnp.float32, 'mla_norm': _jnp.float32, 'mla_w_in': _jnp.float32, 'mla_q_norm': _jnp.float32, 'mla_w_q_b': _jnp.float32, 'mla_kv_norm': _jnp.float32, 'mla_w_kv_b': _jnp.float32, 'mla_w_out': _jnp.float32, 'final_norm': _jnp.float32}
MOMENT_SCALE = {'pool_norm': 1.688121e-01, 'pool_w_in': 8.398002e-02, 'pool_w_group': 8.253068e-02, 'pool_scale': 8.310726e-02, 'pool_w_out': 1.169534e-01, 'mla_norm': 5.371680e-02, 'mla_w_in': 3.205710e-02, 'mla_q_norm': 4.016795e-02, 'mla_w_q_b': 1.375701e-02, 'mla_kv_norm': 7.178226e-02, 'mla_w_kv_b': 1.741696e-02, 'mla_w_out': 2.854677e-02, 'final_norm': 6.402559e+01}


def _to_microbatches(a, axis):
    t = _jnp.moveaxis(a, axis, 0)
    t = t.reshape((N_MICROBATCH, t.shape[0] // N_MICROBATCH) + t.shape[1:])
    return _jnp.moveaxis(t, 1, axis + 1)


def setup_inputs(seed: int = 0) -> dict:
    inp = _fwd_setup_inputs(seed)
    key = _jax.random.fold_in(_jax.random.key(seed), 7919)
    shape, _ = _output_shape()
    out = dict(inp)
    out["loss_target"] = _jax.random.normal(_jax.random.fold_in(key, 0), shape, _jnp.float32)
    for i, name in enumerate(TWIN_WEIGHTS):
        w = inp[name].astype(_jnp.float32)
        if MOMENT_SCALE is None:
            s = _jnp.sqrt(_jnp.mean(_jnp.square(w)) + 1e-30)
        else:
            s = MOMENT_SCALE[name]
        km, kv = _jax.random.split(_jax.random.fold_in(key, i + 1))
        out[name] = w
        out["m_" + name] = s * _jax.random.normal(km, w.shape, _jnp.float32)
        out["v_" + name] = (s * s) * _jax.random.uniform(kv, w.shape, _jnp.float32, 0.5, 1.5)
    if N_MICROBATCH > 1:
        for name, axis in PER_EXAMPLE_BATCH_AXIS.items():
            out[name] = _to_microbatches(out[name], axis)
    return {'x': out['x'], 'positions': out['positions'], 'pool_norm': out['pool_norm'], 'pool_w_in': out['pool_w_in'], 'pool_w_group': out['pool_w_group'], 'pool_scale': out['pool_scale'], 'pool_w_out': out['pool_w_out'], 'mla_norm': out['mla_norm'], 'mla_w_in': out['mla_w_in'], 'mla_q_norm': out['mla_q_norm'], 'mla_w_q_b': out['mla_w_q_b'], 'mla_kv_norm': out['mla_kv_norm'], 'mla_w_kv_b': out['mla_w_kv_b'], 'mla_w_out': out['mla_w_out'], 'final_norm': out['final_norm'], 'loss_target': out['loss_target'], 'm_pool_norm': out['m_pool_norm'], 'm_pool_w_in': out['m_pool_w_in'], 'm_pool_w_group': out['m_pool_w_group'], 'm_pool_scale': out['m_pool_scale'], 'm_pool_w_out': out['m_pool_w_out'], 'm_mla_norm': out['m_mla_norm'], 'm_mla_w_in': out['m_mla_w_in'], 'm_mla_q_norm': out['m_mla_q_norm'], 'm_mla_w_q_b': out['m_mla_w_q_b'], 'm_mla_kv_norm': out['m_mla_kv_norm'], 'm_mla_w_kv_b': out['m_mla_w_kv_b'], 'm_mla_w_out': out['m_mla_w_out'], 'm_final_norm': out['m_final_norm'], 'v_pool_norm': out['v_pool_norm'], 'v_pool_w_in': out['v_pool_w_in'], 'v_pool_w_group': out['v_pool_w_group'], 'v_pool_scale': out['v_pool_scale'], 'v_pool_w_out': out['v_pool_w_out'], 'v_mla_norm': out['v_mla_norm'], 'v_mla_w_in': out['v_mla_w_in'], 'v_mla_q_norm': out['v_mla_q_norm'], 'v_mla_w_q_b': out['v_mla_w_q_b'], 'v_mla_kv_norm': out['v_mla_kv_norm'], 'v_mla_w_kv_b': out['v_mla_w_kv_b'], 'v_mla_w_out': out['v_mla_w_out'], 'v_final_norm': out['v_final_norm']}


def _loss(weights, diff, rest, loss_target):
    with _jax.named_scope("forward"):
        args = {**rest, TWIN_DIFF_INPUT: diff, **{k: w.astype(_WEIGHT_DTYPES[k]) for k, w in weights.items()}}
        y = _forward(args)
    with _jax.named_scope("loss_head"):
        err = _jnp.square(y.astype(_jnp.float32) - loss_target)
        return 0.5 * _jnp.sum(_jnp.mean(err, axis=-1)) if err.ndim else 0.5 * err


def _adamw(w, g, m, v):
    m = ADAM_B1 * m + (1.0 - ADAM_B1) * g
    v = ADAM_B2 * v + (1.0 - ADAM_B2) * _jnp.square(g)
    m_hat = m / (1.0 - ADAM_B1 ** ADAM_STEP)
    v_hat = v / (1.0 - ADAM_B2 ** ADAM_STEP)
    delta = -ADAM_LR * (m_hat / (_jnp.sqrt(v_hat) + ADAM_EPS) + ADAM_WD * w)
    return delta, m, v


def reference(x, positions, pool_norm, pool_w_in, pool_w_group, pool_scale, pool_w_out, mla_norm, mla_w_in, mla_q_norm, mla_w_q_b, mla_kv_norm, mla_w_kv_b, mla_w_out, final_norm, loss_target, m_pool_norm, m_pool_w_in, m_pool_w_group, m_pool_scale, m_pool_w_out, m_mla_norm, m_mla_w_in, m_mla_q_norm, m_mla_w_q_b, m_mla_kv_norm, m_mla_w_kv_b, m_mla_w_out, m_final_norm, v_pool_norm, v_pool_w_in, v_pool_w_group, v_pool_scale, v_pool_w_out, v_mla_norm, v_mla_w_in, v_mla_q_norm, v_mla_w_q_b, v_mla_kv_norm, v_mla_w_kv_b, v_mla_w_out, v_final_norm):
    given = dict(x=x, positions=positions, pool_norm=pool_norm, pool_w_in=pool_w_in, pool_w_group=pool_w_group, pool_scale=pool_scale, pool_w_out=pool_w_out, mla_norm=mla_norm, mla_w_in=mla_w_in, mla_q_norm=mla_q_norm, mla_w_q_b=mla_w_q_b, mla_kv_norm=mla_kv_norm, mla_w_kv_b=mla_w_kv_b, mla_w_out=mla_w_out, final_norm=final_norm, loss_target=loss_target, m_pool_norm=m_pool_norm, m_pool_w_in=m_pool_w_in, m_pool_w_group=m_pool_w_group, m_pool_scale=m_pool_scale, m_pool_w_out=m_pool_w_out, m_mla_norm=m_mla_norm, m_mla_w_in=m_mla_w_in, m_mla_q_norm=m_mla_q_norm, m_mla_w_q_b=m_mla_w_q_b, m_mla_kv_norm=m_mla_kv_norm, m_mla_w_kv_b=m_mla_w_kv_b, m_mla_w_out=m_mla_w_out, m_final_norm=m_final_norm, v_pool_norm=v_pool_norm, v_pool_w_in=v_pool_w_in, v_pool_w_group=v_pool_w_group, v_pool_scale=v_pool_scale, v_pool_w_out=v_pool_w_out, v_mla_norm=v_mla_norm, v_mla_w_in=v_mla_w_in, v_mla_q_norm=v_mla_q_norm, v_mla_w_q_b=v_mla_w_q_b, v_mla_kv_norm=v_mla_kv_norm, v_mla_w_kv_b=v_mla_w_kv_b, v_mla_w_out=v_mla_w_out, v_final_norm=v_final_norm)
    weights = {n: given[n] for n in TWIN_WEIGHTS}
    shared = {n: given[n] for n in SHARED_INPUTS}
    per_example = {n: given[n] for n in ['x', 'positions']}
    grad_fn = _jax.value_and_grad(_loss, argnums=(0, 1))

    def one_microbatch(ex, loss_target):
        ex = dict(ex)
        diff = ex.pop(TWIN_DIFF_INPUT)
        return grad_fn(weights, diff, {**shared, **ex}, loss_target)

    if N_MICROBATCH == 1:
        loss, (grad_w, grad_x) = one_microbatch(per_example, given["loss_target"])
    else:
        def body(carry, xs):
            loss_sum, grad_sum = carry
            l_k, (gw_k, gx_k) = one_microbatch(xs[0], xs[1])
            with _jax.named_scope("update"):
                return (loss_sum + l_k, _jax.tree.map(_jnp.add, grad_sum, gw_k)), gx_k

        init = (_jnp.zeros((), _jnp.float32), _jax.tree.map(_jnp.zeros_like, weights))
        (loss, grad_w), grad_x = _jax.lax.scan(body, init, (per_example, given["loss_target"]))
    with _jax.named_scope("update"):
        delta_w, new_m, new_v = {}, {}, {}
        for n in TWIN_WEIGHTS:
            delta_w[n], new_m[n], new_v[n] = _adamw(weights[n], grad_w[n], given["m_" + n], given["v_" + n])
    return (loss, grad_x, *[grad_w[n] for n in TWIN_WEIGHTS], *[delta_w[n] for n in TWIN_WEIGHTS],
            *[new_m[n] for n in TWIN_WEIGHTS], *[new_v[n] for n in TWIN_WEIGHTS])
```

```python
import functools

import jax
import jax.numpy as jnp
from jax import lax
from jax.experimental import pallas as pl
from jax.experimental.pallas import tpu as pltpu

F32 = jnp.float32
BF = jnp.bfloat16
MESH = pl.DeviceIdType.MESH

D_MODEL = 1024
POOL_WIDTH = 2048
POOL_WINDOWS = (2, 4, 8, 16)
POOL_GROUP = 512
HALO = 16
N_HEADS = 16
QK_NOPE = 128
QK_ROPE = 64
V_DIM = 128
HEAD_PAD = 256
Q_LORA = 384
KV_LORA = 256
MLA_WIDTH = 2048
ROPE_THETA = 10000.0
EPS = 1e-6
SCALE = (QK_NOPE + QK_ROPE) ** -0.5
NEG = -1e30

P_KV, P_KR, P_Q, P_Z = 0, 256, 384, 768
P_SMALL = 768
P_WIDTH = 2816

ADAM_LR = 0.001
ADAM_B1 = 0.9
ADAM_B2 = 0.999
ADAM_EPS = 1e-08
ADAM_WD = 0.01
ADAM_STEP = 10

NN = (((1,), (0,)), ((), ()))
NT = (((1,), (1,)), ((), ()))
TN = (((0,), (0,)), ((), ()))

PACK_ROWS = (1024, 256, 512, 688, 288, 256, 512)
PACK_PAD = 16
PACK_R = sum(PACK_ROWS) + PACK_PAD
PACK_C = 1024
SV_OFF = dict(pool_norm=0, pool_scale=1024, final_norm=3072, mla_norm=4096, q_norm=5120, kv_norm=5504, loss=5760)
SV_ROWS, SV_COLS = 8, 768

VMEM_LIMIT = 56 * 1024 * 1024


def _params(n_axes, vmem=None):
    return pltpu.CompilerParams(dimension_semantics=("arbitrary",) * n_axes,
                                vmem_limit_bytes=VMEM_LIMIT if vmem is None else vmem)


def _sigmoid(z):
    return 1.0 / (1.0 + jnp.exp(-z))


def _mm(a, b, *, dims, grid, a_spec, b_spec, o_spec, out_shape, out_dtype, acc_shape, name,
        add=None, add_spec=None):
    nk = grid[-1]
    kax = len(grid) - 1

    def body(*refs):
        if add is None:
            a_ref, b_ref, o_ref = refs[:3]
            add_ref = None
            rest = refs[3:]
        else:
            a_ref, b_ref, add_ref, o_ref = refs[:4]
            rest = refs[4:]
        part = lax.dot_general(a_ref[...].astype(BF), b_ref[...].astype(BF), dims,
                               preferred_element_type=F32)

        def finish(r):
            if add_ref is not None:
                r = r + add_ref[...]
            o_ref[...] = r.astype(o_ref.dtype)

        if nk == 1:
            finish(part)
        else:
            acc = rest[0]
            k = pl.program_id(kax)

            @pl.when(k == 0)
            def _():
                acc[...] = part

            @pl.when(k > 0)
            def _():
                acc[...] += part

            @pl.when(k == nk - 1)
            def _():
                finish(acc[...])

    in_specs = [a_spec, b_spec]
    args = [a, b]
    if add is not None:
        in_specs.append(add_spec)
        args.append(add)
    scratch = [] if nk == 1 else [pltpu.VMEM(acc_shape, F32)]
    return pl.pallas_call(
        body, name=name, grid=grid, in_specs=in_specs, out_specs=o_spec,
        out_shape=jax.ShapeDtypeStruct(out_shape, out_dtype), scratch_shapes=scratch,
        compiler_params=_params(len(grid)))(*args)


def _pick(n, t):
    t = min(n, t)
    assert n % t == 0, (n, t)
    return t


def mm_nn(a, b, *, name, out_dtype, a_col=0, k_size=None, add=None, tm=512, tn=1024, tk=1024):
    m = a.shape[0]
    kk, n = b.shape
    assert k_size is None or k_size == kk
    tm, tn, tk = _pick(m, tm), _pick(n, tn), _pick(kk, tk)
    assert a_col % tk == 0
    ko = a_col // tk
    return _mm(a, b, dims=NN, grid=(m // tm, n // tn, kk // tk),
               a_spec=pl.BlockSpec((tm, tk), lambda i, j, k: (i, ko + k)),
               b_spec=pl.BlockSpec((tk, tn), lambda i, j, k: (k, j)),
               o_spec=pl.BlockSpec((tm, tn), lambda i, j, k: (i, j)),
               add=add, add_spec=pl.BlockSpec((tm, tn), lambda i, j, k: (i, j)),
               out_shape=(m, n), out_dtype=out_dtype, acc_shape=(tm, tn), name=name)


def mm_nt(a, b, *, name, out_dtype, b_col=0, add=None, tm=512, tn=1024, tk=1024):
    m, kk = a.shape
    n = b.shape[0]
    tm, tn, tk = _pick(m, tm), _pick(n, tn), _pick(kk, tk)
    assert b_col % tk == 0
    ko = b_col // tk
    return _mm(a, b, dims=NT, grid=(m // tm, n // tn, kk // tk),
               a_spec=pl.BlockSpec((tm, tk), lambda i, j, k: (i, k)),
               b_spec=pl.BlockSpec((tn, tk), lambda i, j, k: (j, ko + k)),
               o_spec=pl.BlockSpec((tm, tn), lambda i, j, k: (i, j)),
               add=add, add_spec=pl.BlockSpec((tm, tn), lambda i, j, k: (i, j)),
               out_shape=(m, n), out_dtype=out_dtype, acc_shape=(tm, tn), name=name)


def mm_tn(a, b, *, name, a_col=0, m_size=None, b_col=0, n_size=None, tm=1024, tn=1024, tk=512):
    s = a.shape[0]
    m = a.shape[1] if m_size is None else m_size
    n = b.shape[1] if n_size is None else n_size
    tm, tn, tk = _pick(m, tm), _pick(n, tn), _pick(s, tk)
    assert a_col % tm == 0 and b_col % tn == 0
    ao, bo = a_col // tm, b_col // tn
    return _mm(a, b, dims=TN, grid=(m // tm, n // tn, s // tk),
               a_spec=pl.BlockSpec((tk, tm), lambda i, j, k: (k, ao + i)),
               b_spec=pl.BlockSpec((tk, tn), lambda i, j, k: (k, bo + j)),
               o_spec=pl.BlockSpec((tm, tn), lambda i, j, k: (i, j)),
               out_shape=(m, n), out_dtype=F32, acc_shape=(tm, tn), name=name)


def gmm_nn(a, w, *, name, tm=512):
    s = a.shape[0]
    g, kk, n = w.shape
    tm = _pick(s, tm)
    return _mm(a, w, dims=NN, grid=(s // tm, g, 1),
               a_spec=pl.BlockSpec((tm, kk), lambda i, gi, k: (i, gi)),
               b_spec=pl.BlockSpec((None, kk, n), lambda i, gi, k: (gi, 0, 0)),
               o_spec=pl.BlockSpec((tm, n), lambda i, gi, k: (i, gi)),
               out_shape=(s, g * n), out_dtype=F32, acc_shape=(tm, n), name=name)


def gmm_nt(a, w, *, name, tm=512):
    s = a.shape[0]
    g, kk, n = w.shape
    tm = _pick(s, tm)
    return _mm(a, w, dims=NT, grid=(s // tm, g, 1),
               a_spec=pl.BlockSpec((tm, n), lambda i, gi, k: (i, gi)),
               b_spec=pl.BlockSpec((None, kk, n), lambda i, gi, k: (gi, 0, 0)),
               o_spec=pl.BlockSpec((tm, kk), lambda i, gi, k: (i, gi)),
               out_shape=(s, g * kk), out_dtype=F32, acc_shape=(tm, kk), name=name)


def gmm_tn(a, b, g, *, name, tk=512):
    s = a.shape[0]
    kk, n = a.shape[1] // g, b.shape[1] // g
    tk = _pick(s, tk)
    return _mm(a, b, dims=TN, grid=(g, s // tk),
               a_spec=pl.BlockSpec((tk, kk), lambda gi, k: (k, gi)),
               b_spec=pl.BlockSpec((tk, n), lambda gi, k: (k, gi)),
               o_spec=pl.BlockSpec((None, kk, n), lambda gi, k: (gi, 0, 0)),
               out_shape=(g, kk, n), out_dtype=F32, acc_shape=(kk, n), name=name)


def norm_fwd(x, g, *, col, width, name, t=512):
    s = x.shape[0]
    t = _pick(s, t)
    cb = col // width
    assert col % width == 0

    def body(x_ref, g_ref, o_ref):
        xv = x_ref[...]
        inv = lax.rsqrt(jnp.mean(xv * xv, axis=-1, keepdims=True) + EPS)
        o_ref[...] = ((xv * inv) * g_ref[...]).astype(o_ref.dtype)

    return pl.pallas_call(
        body, name=name, grid=(s // t,),
        in_specs=[pl.BlockSpec((t, width), lambda i: (i, cb)), pl.BlockSpec((1, width), lambda i: (0, 0))],
        out_specs=pl.BlockSpec((t, width), lambda i: (i, 0)),
        out_shape=jax.ShapeDtypeStruct((s, width), BF), compiler_params=_params(1))(x, g)


def norm_bwd(x, g, dh, *, col, width, name, res=None, out_dtype=F32, t=512):
    s = x.shape[0]
    t = _pick(s, t)
    cb = col // width
    assert col % width == 0

    def body(*refs):
        if res is None:
            x_ref, g_ref, dh_ref, dx_ref, dg_ref = refs
        else:
            x_ref, g_ref, dh_ref, res_ref, dx_ref, dg_ref = refs
        xv = x_ref[...]
        inv = lax.rsqrt(jnp.mean(xv * xv, axis=-1, keepdims=True) + EPS)
        xhat = xv * inv
        dh_v = dh_ref[...]
        part = jnp.sum(dh_v * xhat, axis=0, keepdims=True)

        @pl.when(pl.program_id(0) == 0)
        def _():
            dg_ref[...] = part

        @pl.when(pl.program_id(0) > 0)
        def _():
            dg_ref[...] += part

        dxhat = dh_v * g_ref[...]
        dx = inv * (dxhat - xhat * jnp.mean(dxhat * xhat, axis=-1, keepdims=True))
        if res is not None:
            dx = dx + res_ref[...]
        dx_ref[...] = dx.astype(dx_ref.dtype)

    row = pl.BlockSpec((t, width), lambda i: (i, 0))
    vec = pl.BlockSpec((1, width), lambda i: (0, 0))
    in_specs = [pl.BlockSpec((t, width), lambda i: (i, cb)), vec, row]
    args = [x, g, dh]
    if res is not None:
        in_specs.append(row)
        args.append(res)
    return pl.pallas_call(
        body, name=name, grid=(s // t,), in_specs=in_specs, out_specs=[row, vec],
        out_shape=[jax.ShapeDtypeStruct((s, width), out_dtype), jax.ShapeDtypeStruct((1, width), F32)],
        compiler_params=_params(1))(*args)


def final_loss(x2, gf, tgt, *, name, t=512):
    s, d = x2.shape
    t = _pick(s, t)

    def body(x_ref, g_ref, t_ref, dx_ref, dg_ref, loss_ref):
        xv = x_ref[...]
        inv = lax.rsqrt(jnp.mean(xv * xv, axis=-1, keepdims=True) + EPS)
        xhat = xv * inv
        gv = g_ref[...]
        diff = xhat * gv - t_ref[...]
        row_err = jnp.mean(diff * diff, axis=-1, keepdims=True)
        lpart = jnp.broadcast_to(0.5 * jnp.sum(row_err, axis=0, keepdims=True), (1, 128))
        dout = diff * (1.0 / d)
        gpart = jnp.sum(dout * xhat, axis=0, keepdims=True)

        @pl.when(pl.program_id(0) == 0)
        def _():
            dg_ref[...] = gpart
            loss_ref[...] = lpart

        @pl.when(pl.program_id(0) > 0)
        def _():
            dg_ref[...] += gpart
            loss_ref[...] += lpart

        dxhat = dout * gv
        dx_ref[...] = inv * (dxhat - xhat * jnp.mean(dxhat * xhat, axis=-1, keepdims=True))

    row = pl.BlockSpec((t, d), lambda i: (i, 0))
    vec = pl.BlockSpec((1, d), lambda i: (0, 0))
    return pl.pallas_call(
        body, name=name, grid=(s // t,), in_specs=[row, vec, row],
        out_specs=[row, vec, pl.BlockSpec((1, 128), lambda i: (0, 0))],
        out_shape=[jax.ShapeDtypeStruct((s, d), F32), jax.ShapeDtypeStruct((1, d), F32),
                   jax.ShapeDtypeStruct((1, 128), F32)],
        compiler_params=_params(1))(x2, gf, tgt)


def pool_prep(uz, *, name, t=256):
    s = uz.shape[0]
    t = _pick(s, t)
    hb = t // HALO

    def body(u_ref, halo_ref, o_ref, buf):
        i = pl.program_id(0)
        buf[pl.ds(HALO, t), :] = u_ref[...]

        @pl.when(i == 0)
        def _():
            buf[pl.ds(0, HALO), :] = jnp.zeros((HALO, POOL_WIDTH), F32)

        @pl.when(i > 0)
        def _():
            buf[pl.ds(0, HALO), :] = halo_ref[...]

        pos = i * t + lax.broadcasted_iota(jnp.int32, (t, POOL_GROUP), 0)
        for g, w in enumerate(POOL_WINDOWS):
            cols = pl.ds(g * POOL_GROUP, POOL_GROUP)
            cur = buf[pl.ds(HALO, t), cols]
            acc = cur
            for k in range(1, w):
                acc = acc + buf[pl.ds(HALO - k, t), cols]
            cnt = jnp.minimum(pos + 1, w).astype(F32)
            o_ref[:, cols] = (acc / cnt - cur).astype(o_ref.dtype)

    return pl.pallas_call(
        body, name=name, grid=(s // t,),
        in_specs=[pl.BlockSpec((t, POOL_WIDTH), lambda i: (i, 0)),
                  pl.BlockSpec((HALO, POOL_WIDTH), lambda i: (jnp.maximum(i * hb - 1, 0), 0))],
        out_specs=pl.BlockSpec((t, POOL_WIDTH), lambda i: (i, 0)),
        out_shape=jax.ShapeDtypeStruct((s, POOL_WIDTH), BF),
        scratch_shapes=[pltpu.VMEM((t + HALO, POOL_WIDTH), F32)],
        compiler_params=_params(1))(uz, uz)


def pool_prep_bwd(dpd, *, name, t=256):
    s = dpd.shape[0]
    t = _pick(s, t)
    hb = t // HALO
    n = s // t

    def body(d_ref, halo_ref, o_ref, buf):
        i = pl.program_id(0)
        pos = i * t + lax.broadcasted_iota(jnp.int32, (t, POOL_GROUP), 0)
        for g, w in enumerate(POOL_WINDOWS):
            cols = pl.ds(g * POOL_GROUP, POOL_GROUP)
            cnt = jnp.minimum(pos + 1, w).astype(F32)
            buf[pl.ds(0, t), cols] = d_ref[:, cols] / cnt

            @pl.when(i < n - 1)
            def _():
                buf[pl.ds(t, HALO), cols] = halo_ref[:, cols] / float(w)

            @pl.when(i == n - 1)
            def _():
                buf[pl.ds(t, HALO), cols] = jnp.zeros((HALO, POOL_GROUP), F32)

        for g, w in enumerate(POOL_WINDOWS):
            cols = pl.ds(g * POOL_GROUP, POOL_GROUP)
            acc = buf[pl.ds(0, t), cols]
            for k in range(1, w):
                acc = acc + buf[pl.ds(k, t), cols]
            o_ref[:, cols] = (acc - d_ref[:, cols]).astype(o_ref.dtype)

    return pl.pallas_call(
        body, name=name, grid=(n,),
        in_specs=[pl.BlockSpec((t, POOL_WIDTH), lambda i: (i, 0)),
                  pl.BlockSpec((HALO, POOL_WIDTH), lambda i: (jnp.minimum((i + 1) * hb, n * hb - 1), 0))],
        out_specs=pl.BlockSpec((t, POOL_WIDTH), lambda i: (i, 0)),
        out_shape=jax.ShapeDtypeStruct((s, POOL_WIDTH), BF),
        scratch_shapes=[pltpu.VMEM((t + HALO, POOL_WIDTH), F32)],
        compiler_params=_params(1))(dpd, dpd)


def pool_gate(mm, uz, scale, *, name, t=512, cw=512):
    s = mm.shape[0]
    t = _pick(s, t)
    zo = POOL_WIDTH // cw

    def body(mm_ref, z_ref, sc_ref, y_ref):
        z = z_ref[...]
        y_ref[...] = ((mm_ref[...] * sc_ref[...]) * (z * _sigmoid(z))).astype(y_ref.dtype)

    blk = pl.BlockSpec((t, cw), lambda i, j: (i, j))
    return pl.pallas_call(
        body, name=name, grid=(s // t, POOL_WIDTH // cw),
        in_specs=[blk, pl.BlockSpec((t, cw), lambda i, j: (i, zo + j)), pl.BlockSpec((1, cw), lambda i, j: (0, j))],
        out_specs=blk, out_shape=jax.ShapeDtypeStruct((s, POOL_WIDTH), BF),
        compiler_params=_params(2))(mm, uz, scale)


def pool_gate_bwd(dy, mm, uz, scale, *, name, t=512, cw=512):
    s = mm.shape[0]
    t = _pick(s, t)
    zo = POOL_WIDTH // cw

    def body(dy_ref, mm_ref, z_ref, sc_ref, dmm_ref, dz_ref, dsc_ref):
        z = z_ref[...]
        sig = _sigmoid(z)
        silu = z * sig
        dyv = dy_ref[...]
        mmv = mm_ref[...]
        scv = sc_ref[...]
        dmixed = dyv * silu
        dmm_ref[...] = (dmixed * scv).astype(dmm_ref.dtype)
        dz_ref[...] = (dyv * (mmv * scv) * (sig * (1.0 + z * (1.0 - sig)))).astype(dz_ref.dtype)
        part = jnp.sum(dmixed * mmv, axis=0, keepdims=True)

        @pl.when(pl.program_id(1) == 0)
        def _():
            dsc_ref[...] = part

        @pl.when(pl.program_id(1) > 0)
        def _():
            dsc_ref[...] += part

    blk = pl.BlockSpec((t, cw), lambda j, i: (i, j))
    vec = pl.BlockSpec((1, cw), lambda j, i: (0, j))
    return pl.pallas_call(
        body, name=name, grid=(POOL_WIDTH // cw, s // t),
        in_specs=[blk, blk, pl.BlockSpec((t, cw), lambda j, i: (i, zo + j)), vec],
        out_specs=[blk, blk, vec],
        out_shape=[jax.ShapeDtypeStruct((s, POOL_WIDTH), BF), jax.ShapeDtypeStruct((s, POOL_WIDTH), BF),
                   jax.ShapeDtypeStruct((1, POOL_WIDTH), F32)],
        compiler_params=_params(2))(dy, mm, uz, scale)


def _rope(a, cc, sa, sb):
    return a * cc + pltpu.roll(a, 96, 1) * sa + pltpu.roll(a, 32, 1) * sb


def _unrope(d, cc, sa, sb):
    return d * cc + pltpu.roll(d * sa, 32, 1) + pltpu.roll(d * sb, 96, 1)


def rope_q(q_pre, cc, sa, sb, *, name, t=512):
    s = q_pre.shape[0]
    t = _pick(s, t)

    def body(q_ref, cc_ref, sa_ref, sb_ref, o_ref):
        o_ref[:, :QK_NOPE] = q_ref[:, :QK_NOPE].astype(o_ref.dtype)
        o_ref[:, QK_NOPE:] = _rope(q_ref[:, QK_NOPE:], cc_ref[...], sa_ref[...], sb_ref[...]).astype(o_ref.dtype)

    tab = pl.BlockSpec((t, 128), lambda i, h: (i, 0))
    blk = pl.BlockSpec((t, HEAD_PAD), lambda i, h: (i, h))
    return pl.pallas_call(
        body, name=name, grid=(s // t, N_HEADS), in_specs=[blk, tab, tab, tab], out_specs=blk,
        out_shape=jax.ShapeDtypeStruct((s, N_HEADS * HEAD_PAD), BF), compiler_params=_params(2))(q_pre, cc, sa, sb)


def pack_k(kv, proj, cc, sa, sb, *, name, t=512):
    s = kv.shape[0]
    t = _pick(s, t)
    kr_blk = P_KR // 128

    def body(kn_ref, kr_ref, cc_ref, sa_ref, sb_ref, o_ref):
        o_ref[:, :QK_NOPE] = kn_ref[...]
        o_ref[:, QK_NOPE:] = _rope(kr_ref[...], cc_ref[...], sa_ref[...], sb_ref[...]).astype(o_ref.dtype)

    tab = pl.BlockSpec((t, 128), lambda i, h: (i, 0))
    return pl.pallas_call(
        body, name=name, grid=(s // t, N_HEADS),
        in_specs=[pl.BlockSpec((t, 128), lambda i, h: (i, 2 * h)),
                  pl.BlockSpec((t, 128), lambda i, h: (i, kr_blk)), tab, tab, tab],
        out_specs=pl.BlockSpec((t, HEAD_PAD), lambda i, h: (i, h)),
        out_shape=jax.ShapeDtypeStruct((s, N_HEADS * HEAD_PAD), BF), compiler_params=_params(2))(kv, proj, cc, sa, sb)


def unrope_q(dqr, cc, sa, sb, *, name, t=512):
    s = dqr.shape[0]
    t = _pick(s, t)

    def body(d_ref, cc_ref, sa_ref, sb_ref, o_ref):
        o_ref[:, :QK_NOPE] = (d_ref[:, :QK_NOPE] * SCALE).astype(o_ref.dtype)
        o_ref[:, QK_NOPE:] = _unrope(d_ref[:, QK_NOPE:] * SCALE, cc_ref[...], sa_ref[...], sb_ref[...]).astype(o_ref.dtype)

    tab = pl.BlockSpec((t, 128), lambda i, h: (i, 0))
    blk = pl.BlockSpec((t, HEAD_PAD), lambda i, h: (i, h))
    return pl.pallas_call(
        body, name=name, grid=(s // t, N_HEADS), in_specs=[blk, tab, tab, tab], out_specs=blk,
        out_shape=jax.ShapeDtypeStruct((s, N_HEADS * HEAD_PAD), BF), compiler_params=_params(2))(dqr, cc, sa, sb)


def unrope_k(dkr, cc, sa, sb, *, name, t=512):
    s = dkr.shape[0]
    t = _pick(s, t)

    def body(d_ref, cc_ref, sa_ref, sb_ref, o_ref):
        acc = d_ref[:, pl.ds(0, 128)]
        for h in range(1, N_HEADS):
            acc = acc + d_ref[:, pl.ds(h * 128, 128)]
        o_ref[...] = _unrope(acc, cc_ref[...], sa_ref[...], sb_ref[...]).astype(o_ref.dtype)

    tab = pl.BlockSpec((t, 128), lambda i: (i, 0))
    return pl.pallas_call(
        body, name=name, grid=(s // t,),
        in_specs=[pl.BlockSpec((t, N_HEADS * 128), lambda i: (i, 0)), tab, tab, tab], out_specs=tab,
        out_shape=jax.ShapeDtypeStruct((s, 128), BF), compiler_params=_params(1))(dkr, cc, sa, sb)


def mla_gate(o, proj, *, name, t=512, cw=256):
    s = o.shape[0]
    t = _pick(s, t)
    zo = P_Z // cw

    def body(o_ref, z_ref, y_ref):
        z = z_ref[...]
        y_ref[...] = (o_ref[...] * (z * _sigmoid(z))).astype(y_ref.dtype)

    blk = pl.BlockSpec((t, cw), lambda i, j: (i, j))
    return pl.pallas_call(
        body, name=name, grid=(s // t, MLA_WIDTH // cw),
        in_specs=[blk, pl.BlockSpec((t, cw), lambda i, j: (i, zo + j))], out_specs=blk,
        out_shape=jax.ShapeDtypeStruct((s, MLA_WIDTH), BF), compiler_params=_params(2))(o, proj)


def mla_gate_bwd(dy, o, proj, *, name, tq):
    s = o.shape[0]
    nq = s // tq
    zo = P_Z // 128

    def body(dy_ref, o_ref, z_ref, do_ref, dz_ref, dl_ref):
        z = z_ref[...]
        sig = _sigmoid(z)
        dyv = dy_ref[...]
        ov = o_ref[...]
        dov = dyv * (z * sig)
        do_ref[...] = dov.astype(do_ref.dtype)
        dz_ref[...] = (dyv * ov * (sig * (1.0 + z * (1.0 - sig)))).astype(dz_ref.dtype)
        delta = jnp.sum(dov * ov, axis=-1, keepdims=True)
        dl_ref[...] = jnp.broadcast_to(delta, (tq, 128)).T[:8, :]

    blk = pl.BlockSpec((tq, 128), lambda i, h: (i, h))
    return pl.pallas_call(
        body, name=name, grid=(nq, N_HEADS),
        in_specs=[blk, blk, pl.BlockSpec((tq, 128), lambda i, h: (i, zo + h))],
        out_specs=[blk, blk, pl.BlockSpec((None, None, 8, tq), lambda i, h: (h, i, 0, 0))],
        out_shape=[jax.ShapeDtypeStruct((s, MLA_WIDTH), BF), jax.ShapeDtypeStruct((s, MLA_WIDTH), BF),
                   jax.ShapeDtypeStruct((N_HEADS, nq, 8, tq), F32)],
        compiler_params=_params(2))(dy, o, proj)


def attn_fwd(qr, kc, kv, *, name, tq):
    s = qr.shape[0]
    nq = s // tq

    def body(q_ref, k_ref, v_ref, o_ref, lse_ref, m_sc, l_sc, acc_sc):
        i = pl.program_id(1)
        q = q_ref[...]
        m_sc[...] = jnp.full((tq, 1), NEG, F32)
        l_sc[...] = jnp.zeros((tq, 1), F32)
        acc_sc[...] = jnp.zeros((tq, V_DIM), F32)

        def step(j, masked):
            r0 = pl.multiple_of(j * tq, tq)
            k = k_ref[pl.ds(r0, tq), :]
            v = v_ref[pl.ds(r0, tq), :]
            sc = lax.dot_general(q, k, NT, preferred_element_type=F32) * SCALE
            if masked:
                row = lax.broadcasted_iota(jnp.int32, (tq, tq), 0)
                col = lax.broadcasted_iota(jnp.int32, (tq, tq), 1)
                sc = jnp.where(col <= row, sc, NEG)
            m_prev = m_sc[...]
            m_new = jnp.maximum(m_prev, jnp.max(sc, axis=-1, keepdims=True))
            alpha = jnp.exp(m_prev - m_new)
            p = jnp.exp(sc - m_new)
            l_sc[...] = alpha * l_sc[...] + jnp.sum(p, axis=-1, keepdims=True)
            acc_sc[...] = alpha * acc_sc[...] + jnp.dot(p.astype(BF), v, preferred_element_type=F32)
            m_sc[...] = m_new

        def unmasked(j, carry):
            step(j, False)
            return carry

        lax.fori_loop(0, i, unmasked, 0)
        step(i, True)
        l = l_sc[...]
        o_ref[...] = acc_sc[...] / l
        lse = m_sc[...] + jnp.log(l)
        lse_ref[...] = jnp.broadcast_to(lse, (tq, 128)).T[:8, :]

    return pl.pallas_call(
        body, name=name, grid=(N_HEADS, nq),
        in_specs=[pl.BlockSpec((tq, HEAD_PAD), lambda h, i: (i, h)),
                  pl.BlockSpec((s, HEAD_PAD), lambda h, i: (0, h)),
                  pl.BlockSpec((s, V_DIM), lambda h, i: (0, 2 * h + 1))],
        out_specs=[pl.BlockSpec((tq, V_DIM), lambda h, i: (i, h)),
                   pl.BlockSpec((None, None, 8, tq), lambda h, i: (h, i, 0, 0))],
        out_shape=[jax.ShapeDtypeStruct((s, N_HEADS * V_DIM), F32),
                   jax.ShapeDtypeStruct((N_HEADS, nq, 8, tq), F32)],
        scratch_shapes=[pltpu.VMEM((tq, 1), F32), pltpu.VMEM((tq, 1), F32), pltpu.VMEM((tq, V_DIM), F32)],
        compiler_params=_params(2))(qr, kc, kv)


def attn_bwd(qr, kc, kv, do, lse, delta, *, name, tq):
    s = qr.shape[0]
    nq = s // tq

    def body(k_ref, v_ref, q_ref, do_ref, lse_ref, dl_ref, dkv_ref, dkr_ref, dq_ref, dk_sc, dv_sc):
        j = pl.program_id(1)

        @pl.when(j == 0)
        def _():
            dq_ref[...] = jnp.zeros((s, HEAD_PAD), F32)

        dk_sc[...] = jnp.zeros((tq, HEAD_PAD), F32)
        dv_sc[...] = jnp.zeros((tq, V_DIM), F32)
        k = k_ref[...]
        v = v_ref[...]

        def step(i, masked):
            r0 = pl.multiple_of(i * tq, tq)
            q = q_ref[pl.ds(r0, tq), :]
            dov = do_ref[pl.ds(r0, tq), :]
            lse_row = lse_ref[i, pl.ds(0, 1), :]
            dl_row = dl_ref[i, pl.ds(0, 1), :]
            st = lax.dot_general(k, q, NT, preferred_element_type=F32) * SCALE
            if masked:
                krow = lax.broadcasted_iota(jnp.int32, (tq, tq), 0)
                qcol = lax.broadcasted_iota(jnp.int32, (tq, tq), 1)
                st = jnp.where(qcol >= krow, st, NEG)
            pt = jnp.exp(st - lse_row)
            dpt = lax.dot_general(v, dov, NT, preferred_element_type=F32)
            dst = (pt * (dpt - dl_row)).astype(BF)
            dv_sc[...] += jnp.dot(pt.astype(BF), dov, preferred_element_type=F32)
            dk_sc[...] += jnp.dot(dst, q, preferred_element_type=F32)
            dq_ref[pl.ds(r0, tq), :] += lax.dot_general(dst, k, TN, preferred_element_type=F32)

        def unmasked(i, carry):
            step(i, False)
            return carry

        step(j, True)
        lax.fori_loop(j + 1, nq, unmasked, 0)
        dkv_ref[:, :QK_NOPE] = (dk_sc[:, :QK_NOPE] * SCALE).astype(dkv_ref.dtype)
        dkv_ref[:, QK_NOPE:] = dv_sc[...].astype(dkv_ref.dtype)
        dkr_ref[...] = dk_sc[:, QK_NOPE:] * SCALE

    rows = pl.BlockSpec((None, nq, 8, tq), lambda h, j: (h, 0, 0, 0))
    return pl.pallas_call(
        body, name=name, grid=(N_HEADS, nq),
        in_specs=[pl.BlockSpec((tq, HEAD_PAD), lambda h, j: (j, h)),
                  pl.BlockSpec((tq, V_DIM), lambda h, j: (j, 2 * h + 1)),
                  pl.BlockSpec((s, HEAD_PAD), lambda h, j: (0, h)),
                  pl.BlockSpec((s, V_DIM), lambda h, j: (0, h)), rows, rows],
        out_specs=[pl.BlockSpec((tq, 256), lambda h, j: (j, h)),
                   pl.BlockSpec((tq, 128), lambda h, j: (j, h)),
                   pl.BlockSpec((s, HEAD_PAD), lambda h, j: (0, h))],
        out_shape=[jax.ShapeDtypeStruct((s, N_HEADS * 256), BF),
                   jax.ShapeDtypeStruct((s, N_HEADS * 128), F32),
                   jax.ShapeDtypeStruct((s, N_HEADS * HEAD_PAD), F32)],
        scratch_shapes=[pltpu.VMEM((tq, HEAD_PAD), F32), pltpu.VMEM((tq, V_DIM), F32)],
        compiler_params=_params(2))(kc, kv, qr, do, lse, delta)


def adamw(w, g, m, v, *, name, t=256):
    r, c = w.shape
    t = r if r % t else t
    c1 = 1.0 - ADAM_B1 ** ADAM_STEP
    c2 = 1.0 - ADAM_B2 ** ADAM_STEP

    def body(w_ref, g_ref, m_ref, v_ref, d_ref, nm_ref, nv_ref):
        gv = g_ref[...]
        nm = ADAM_B1 * m_ref[...] + (1.0 - ADAM_B1) * gv
        nv = ADAM_B2 * v_ref[...] + (1.0 - ADAM_B2) * (gv * gv)
        nm_ref[...] = nm
        nv_ref[...] = nv
        d_ref[...] = -ADAM_LR * ((nm / c1) / (jnp.sqrt(nv / c2) + ADAM_EPS) + ADAM_WD * w_ref[...])

    blk = pl.BlockSpec((t, c), lambda i: (i, 0))
    return pl.pallas_call(
        body, name=name, grid=(r // t,), in_specs=[blk] * 4, out_specs=[blk] * 3,
        out_shape=[jax.ShapeDtypeStruct((r, c), F32)] * 3, compiler_params=_params(1))(w, g, m, v)


def sum_devices(parts, *, name):
    def body(p_ref, o_ref):
        acc = p_ref[pl.ds(0, SV_ROWS), :]
        for d in range(1, 8):
            acc = acc + p_ref[pl.ds(d * SV_ROWS, SV_ROWS), :]
        o_ref[...] = acc

    return pl.pallas_call(body, name=name, out_shape=jax.ShapeDtypeStruct((SV_ROWS, SV_COLS), F32))(parts)


def add_halves(g, rb, c_idx, *, name, rows):
    nq, r2, cc = rb.shape
    nb = r2 // rows

    def body(c_ref, g_ref, r_ref, o_ref):
        o_ref[...] = g_ref[...] + r_ref[...]

    grid_spec = pltpu.PrefetchScalarGridSpec(
        num_scalar_prefetch=1, grid=(nq, nb),
        in_specs=[pl.BlockSpec((None, rows, cc), lambda q, i, c: (q, c[0] * nb + i, 0)),
                  pl.BlockSpec((None, rows, cc), lambda q, i, c: (q, i, 0))],
        out_specs=pl.BlockSpec((None, rows, cc), lambda q, i, c: (q, i, 0)))
    return pl.pallas_call(body, name=name, grid_spec=grid_spec,
                          out_shape=jax.ShapeDtypeStruct((nq, r2, cc), F32),
                          compiler_params=_params(2))(c_idx, g, rb)


def sum_chips(rc, *, name, rows):
    nq, r2, cc = rc.shape

    def body(r_ref, o_ref):
        o_ref[...] = ((r_ref[0] + r_ref[1]) + r_ref[2]) + r_ref[3]

    return pl.pallas_call(
        body, name=name, grid=(r2 // rows,),
        in_specs=[pl.BlockSpec((nq, rows, cc), lambda i: (0, i, 0))],
        out_specs=pl.BlockSpec((rows, cc), lambda i: (i, 0)),
        out_shape=jax.ShapeDtypeStruct((r2, cc), F32), compiler_params=_params(1))(rc)


def _place():
    return lax.axis_index("x"), lax.axis_index("y"), lax.axis_index("c")


def all_gather8(xs, *, name, own_half):
    m = xs.shape[0] // 2 if own_half else xs.shape[0]
    n = xs.shape[1]

    def body(x_ref, out_ref, send_sems, recv_sems, local_sem):
        x, y, c = _place()
        me, sibling = (x, y, c), (x, y, 1 - c)
        chips = [(1 - x, y), (x, 1 - y), (1 - x, 1 - y)]
        src_own = x_ref.at[pl.ds(c * m, m), :] if own_half else x_ref

        def rows(px, py, pc):
            return out_ref.at[pl.ds((4 * px + 2 * py + pc) * m, m), :]

        def copy(k, block, to, src=None):
            return pltpu.make_async_remote_copy(
                src_ref=rows(*block) if src is None else src, dst_ref=rows(*block),
                send_sem=send_sems.at[k], recv_sem=recv_sems.at[k], device_id=to, device_id_type=MESH)

        mine = pltpu.make_async_copy(src_own, rows(*me), local_sem)
        mine.start()
        first = [copy(0, me, sibling, src=src_own)]
        first += [copy(1 + j, me, (*chip, c), src=src_own) for j, chip in enumerate(chips)]
        for cp in first:
            cp.start()
        passed = [copy(4 + j, (*chip, c), sibling) for j, chip in enumerate(chips)]
        for j, chip in enumerate(chips):
            copy(1 + j, (*chip, c), me).wait_recv()
            passed[j].start()
        copy(0, sibling, me).wait_recv()
        for j, chip in enumerate(chips):
            copy(4 + j, (*chip, 1 - c), me).wait_recv()
        for cp in first + passed:
            cp.wait_send()
        mine.wait()

    return pl.pallas_call(
        body, name=name, out_shape=jax.ShapeDtypeStruct((8 * m, n), xs.dtype),
        in_specs=[pl.BlockSpec(memory_space=pl.ANY)], out_specs=pl.BlockSpec(memory_space=pl.ANY),
        scratch_shapes=[pltpu.SemaphoreType.DMA((7,)), pltpu.SemaphoreType.DMA((7,)), pltpu.SemaphoreType.DMA],
    )(xs)


def swap_halves(g, *, name):
    nq, r, cc = g.shape
    r2 = r // 2

    def body(g_ref, out_ref, send_sem, recv_sem):
        x, y, c = _place()
        cp = pltpu.make_async_remote_copy(
            src_ref=g_ref.at[:, pl.ds((1 - c) * r2, r2), :], dst_ref=out_ref,
            send_sem=send_sem, recv_sem=recv_sem, device_id=(x, y, 1 - c), device_id_type=MESH)
        cp.start()
        cp.wait()

    return pl.pallas_call(
        body, name=name, out_shape=jax.ShapeDtypeStruct((nq, r2, cc), g.dtype),
        in_specs=[pl.BlockSpec(memory_space=pl.ANY)], out_specs=pl.BlockSpec(memory_space=pl.ANY),
        scratch_shapes=[pltpu.SemaphoreType.DMA, pltpu.SemaphoreType.DMA],
    )(g)


def exchange_chips(p, *, name):
    nq, r2, cc = p.shape

    def body(p_ref, out_ref, send_sems, recv_sems, local_sem):
        x, y, c = _place()
        q0 = 2 * x + y
        chips = [(1 - x, y), (x, 1 - y), (1 - x, 1 - y)]
        mine = pltpu.make_async_copy(p_ref.at[q0], out_ref.at[q0], local_sem)
        mine.start()
        sends = []
        for j, (cx, cy) in enumerate(chips):
            sends.append(pltpu.make_async_remote_copy(
                src_ref=p_ref.at[2 * cx + cy], dst_ref=out_ref.at[q0],
                send_sem=send_sems.at[j], recv_sem=recv_sems.at[j],
                device_id=(cx, cy, c), device_id_type=MESH))
        for cp in sends:
            cp.start()
        for j, (cx, cy) in enumerate(chips):
            qj = 2 * cx + cy
            pltpu.make_async_remote_copy(
                src_ref=p_ref.at[qj], dst_ref=out_ref.at[qj],
                send_sem=send_sems.at[j], recv_sem=recv_sems.at[j],
                device_id=(cx, cy, c), device_id_type=MESH).wait_recv()
        for cp in sends:
            cp.wait_send()
        mine.wait()

    return pl.pallas_call(
        body, name=name, out_shape=jax.ShapeDtypeStruct((nq, r2, cc), p.dtype),
        in_specs=[pl.BlockSpec(memory_space=pl.ANY)], out_specs=pl.BlockSpec(memory_space=pl.ANY),
        scratch_shapes=[pltpu.SemaphoreType.DMA((3,)), pltpu.SemaphoreType.DMA((3,)), pltpu.SemaphoreType.DMA],
    )(p)


def join_halves(tot, *, name):
    r2, cc = tot.shape

    def body(t_ref, out_ref, send_sem, recv_sem, local_sem):
        x, y, c = _place()
        mine = pltpu.make_async_copy(t_ref, out_ref.at[pl.ds(c * r2, r2), :], local_sem)
        mine.start()
        cp = pltpu.make_async_remote_copy(
            src_ref=t_ref, dst_ref=out_ref.at[pl.ds(c * r2, r2), :],
            send_sem=send_sem, recv_sem=recv_sem, device_id=(x, y, 1 - c), device_id_type=MESH)
        cp.start()
        pltpu.make_async_remote_copy(
            src_ref=t_ref, dst_ref=out_ref.at[pl.ds((1 - c) * r2, r2), :],
            send_sem=send_sem, recv_sem=recv_sem, device_id=(x, y, 1 - c), device_id_type=MESH).wait_recv()
        cp.wait_send()
        mine.wait()

    return pl.pallas_call(
        body, name=name, out_shape=jax.ShapeDtypeStruct((2 * r2, cc), tot.dtype),
        in_specs=[pl.BlockSpec(memory_space=pl.ANY)], out_specs=pl.BlockSpec(memory_space=pl.ANY),
        scratch_shapes=[pltpu.SemaphoreType.DMA, pltpu.SemaphoreType.DMA, pltpu.SemaphoreType.DMA],
    )(tot)


def _pack_local_shard(big, small_vec):
    parts = [w.reshape(-1, PACK_C).astype(BF) for w in big]
    srow = lax.bitcast_convert_type(small_vec, BF).reshape(1, PACK_C)
    parts.append(jnp.concatenate([srow, jnp.zeros((PACK_PAD - 1, PACK_C), BF)], axis=0))
    return jnp.concatenate(parts, axis=0)


def _split_rows(a, axis):
    out, off = [], 0
    for n in PACK_ROWS:
        out.append(lax.slice_in_dim(a, off, off + n, axis=axis))
        off += n
    return out, off


def _unpack_gathered(gw):
    (p_in, p_grp, p_out, m_in, m_qb, m_kvb, m_out), off = _split_rows(gw, 1)
    w = {}
    w["pool_w_in"] = p_in.reshape(4, D_MODEL, 1024).transpose(1, 0, 2).reshape(D_MODEL, 2 * POOL_WIDTH)
    w["pool_w_group"] = p_grp.reshape(4, 4, 128, POOL_GROUP).transpose(1, 0, 2, 3).reshape(4, POOL_GROUP, POOL_GROUP)
    w["pool_w_out"] = p_out.reshape(POOL_WIDTH, D_MODEL)
    win = m_in.reshape(4, D_MODEL, 688).transpose(1, 0, 2).reshape(D_MODEL, 2752)
    w["mla_w_in"] = jnp.concatenate(
        [win[:, 384:640], win[:, 640:704], jnp.zeros((D_MODEL, 64), BF), win[:, 0:384], win[:, 704:]], axis=1)
    wq = m_qb.reshape(4, Q_LORA, 768).transpose(1, 0, 2).reshape(Q_LORA, N_HEADS, QK_NOPE + QK_ROPE)
    w["mla_w_q_b"] = jnp.pad(wq, ((0, 0), (0, 0), (0, HEAD_PAD - QK_NOPE - QK_ROPE))).reshape(Q_LORA, N_HEADS * HEAD_PAD)
    w["mla_w_kv_b"] = m_kvb.reshape(4, KV_LORA, 1024).transpose(1, 0, 2).reshape(KV_LORA, 4096)
    w["mla_w_out"] = m_out.reshape(MLA_WIDTH, D_MODEL)
    small = lax.bitcast_convert_type(gw[:, off, :].reshape(4, 512, 2), F32)
    w["mla_norm"] = small[:, :256].reshape(1, D_MODEL)
    w["mla_q_norm"] = small[:, 256:352].reshape(1, Q_LORA)
    w["mla_kv_norm"] = small[:, 352:416].reshape(1, KV_LORA)
    return w


def _pack_grads(g):
    parts = [
        g["pool_w_in"].reshape(D_MODEL, 4, 1024).transpose(1, 0, 2),
        g["pool_w_group"].reshape(4, 4, 128, POOL_GROUP).transpose(1, 0, 2, 3).reshape(4, 256, PACK_C),
        g["pool_w_out"].reshape(4, 512, PACK_C),
        g["mla_w_in"].reshape(D_MODEL, 4, 688).transpose(1, 0, 2).reshape(4, 688, PACK_C),
        g["mla_w_q_b"].reshape(Q_LORA, 4, 768).transpose(1, 0, 2).reshape(4, 288, PACK_C),
        g["mla_w_kv_b"].reshape(KV_LORA, 4, 1024).transpose(1, 0, 2),
        g["mla_w_out"].reshape(4, 512, PACK_C),
        jnp.zeros((4, PACK_PAD, PACK_C), F32),
    ]
    return jnp.concatenate(parts, axis=1)


def kernel(x, positions, pool_norm, pool_w_in, pool_w_group, pool_scale, pool_w_out, mla_norm, mla_w_in, mla_q_norm, mla_w_q_b, mla_kv_norm, mla_w_kv_b, mla_w_out, final_norm, loss_target, m_pool_norm, m_pool_w_in, m_pool_w_group, m_pool_scale, m_pool_w_out, m_mla_norm, m_mla_w_in, m_mla_q_norm, m_mla_w_q_b, m_mla_kv_norm, m_mla_w_kv_b, m_mla_w_out, m_final_norm, v_pool_norm, v_pool_w_in, v_pool_w_group, v_pool_scale, v_pool_w_out, v_mla_norm, v_mla_w_in, v_mla_q_norm, v_mla_w_q_b, v_mla_kv_norm, v_mla_w_kv_b, v_mla_w_out, v_final_norm):
    s = x.shape[1]
    tq = min(512, s)
    x0 = x.reshape(s, D_MODEL)
    tgt = loss_target.reshape(s, D_MODEL)
    cx, cy, cc_idx = _place()
    chip = 2 * cx + cy

    big_names = ("pool_w_in", "pool_w_group", "pool_w_out", "mla_w_in", "mla_w_q_b", "mla_w_kv_b", "mla_w_out")
    big_w = dict(zip(big_names, (pool_w_in, pool_w_group, pool_w_out, mla_w_in, mla_w_q_b, mla_w_kv_b, mla_w_out)))
    big_m = dict(zip(big_names, (m_pool_w_in, m_pool_w_group, m_pool_w_out, m_mla_w_in, m_mla_w_q_b, m_mla_w_kv_b, m_mla_w_out)))
    big_v = dict(zip(big_names, (v_pool_w_in, v_pool_w_group, v_pool_w_out, v_mla_w_in, v_mla_w_q_b, v_mla_w_kv_b, v_mla_w_out)))

    small_vec = jnp.concatenate([mla_norm.reshape(-1), mla_q_norm.reshape(-1), mla_kv_norm.reshape(-1),
                                 jnp.zeros((96,), F32)])
    packed = _pack_local_shard([big_w[n] for n in big_names], small_vec)
    gathered = all_gather8(packed, name="gather_weights", own_half=True).reshape(4, PACK_R, PACK_C)
    w = _unpack_gathered(gathered)
    g_pool = pool_norm.reshape(1, D_MODEL)
    g_final = final_norm.reshape(1, D_MODEL)
    sc_pool = pool_scale.reshape(1, POOL_WIDTH)

    inv_freq = 1.0 / (ROPE_THETA ** (jnp.arange(0, QK_ROPE, 2, dtype=F32) / QK_ROPE))
    ang = positions.reshape(s).astype(F32)[:, None] * inv_freq
    cos, sin = jnp.cos(ang), jnp.sin(ang)
    z32, z64, z96 = (jnp.zeros((s, n), F32) for n in (32, 64, 96))
    t_cc = jnp.concatenate([cos, cos, z64], axis=1)
    t_sa = jnp.concatenate([-sin, z96], axis=1)
    t_sb = jnp.concatenate([z32, sin, z64], axis=1)

    h0 = norm_fwd(x0, g_pool, col=0, width=D_MODEL, name="pool_norm_fwd")
    uz = mm_nn(h0, w["pool_w_in"], name="pool_in_proj", out_dtype=F32)
    pd = pool_prep(uz, name="pool_window")
    mm = gmm_nn(pd, w["pool_w_group"], name="pool_group_mix")
    y1 = pool_gate(mm, uz, sc_pool, name="pool_gate")
    x1 = mm_nn(y1, w["pool_w_out"], name="pool_out_proj", out_dtype=F32, add=x0)

    h1 = norm_fwd(x1, w["mla_norm"], col=0, width=D_MODEL, name="mla_norm_fwd")
    proj = mm_nn(h1, w["mla_w_in"], name="mla_in_proj", out_dtype=F32, tn=P_WIDTH // 2)
    qn = norm_fwd(proj, w["mla_q_norm"], col=P_Q, width=Q_LORA, name="mla_q_norm_fwd")
    kvn = norm_fwd(proj, w["mla_kv_norm"], col=P_KV, width=KV_LORA, name="mla_kv_norm_fwd")
    q_pre = mm_nn(qn, w["mla_w_q_b"], name="mla_q_proj", out_dtype=F32, tk=Q_LORA)
    kv = mm_nn(kvn, w["mla_w_kv_b"], name="mla_kv_proj", out_dtype=BF, tk=KV_LORA)
    qr = rope_q(q_pre, t_cc, t_sa, t_sb, name="mla_rope_q")
    kc = pack_k(kv, proj, t_cc, t_sa, t_sb, name="mla_pack_k")
    o, lse = attn_fwd(qr, kc, kv, name="mla_attn_fwd", tq=tq)
    y2 = mla_gate(o, proj, name="mla_gate")
    x2 = mm_nn(y2, w["mla_w_out"], name="mla_out_proj", out_dtype=F32, add=x1)

    dx2, d_final, loss_part = final_loss(x2, g_final, tgt, name="final_norm_loss")

    grads = {}
    dy2 = mm_nt(dx2, w["mla_w_out"], name="mla_out_proj_dx", out_dtype=F32)
    grads["mla_w_out"] = mm_tn(y2, dx2, name="mla_out_proj_dw")
    do, dz2, delta = mla_gate_bwd(dy2, o, proj, name="mla_gate_bwd", tq=tq)
    dkv, dkr, dqr = attn_bwd(qr, kc, kv, do, lse, delta, name="mla_attn_bwd", tq=tq)
    dq_pre = unrope_q(dqr, t_cc, t_sa, t_sb, name="mla_unrope_q")
    dkr_pre = unrope_k(dkr, t_cc, t_sa, t_sb, name="mla_unrope_k")
    dqn = mm_nt(dq_pre, w["mla_w_q_b"], name="mla_q_proj_dx", out_dtype=F32, tn=Q_LORA)
    g_qb = mm_tn(qn, dq_pre, name="mla_q_proj_dw", tm=Q_LORA, tn=2048)
    dkvn = mm_nt(dkv, w["mla_w_kv_b"], name="mla_kv_proj_dx", out_dtype=F32, tn=KV_LORA)
    grads["mla_w_kv_b"] = mm_tn(kvn, dkv, name="mla_kv_proj_dw", tm=KV_LORA, tn=2048)
    dq_lat, d_qnorm = norm_bwd(proj, w["mla_q_norm"], dqn, col=P_Q, width=Q_LORA, name="mla_q_norm_bwd", out_dtype=BF)
    dkv_lat, d_kvnorm = norm_bwd(proj, w["mla_kv_norm"], dkvn, col=P_KV, width=KV_LORA, name="mla_kv_norm_bwd", out_dtype=BF)
    dsmall = jnp.concatenate([dkv_lat, dkr_pre, dq_lat], axis=1)
    dh1 = mm_nt(dsmall, w["mla_w_in"], name="mla_in_proj_dx_a", out_dtype=F32, tk=P_SMALL)
    dh1 = mm_nt(dz2, w["mla_w_in"], name="mla_in_proj_dx_b", out_dtype=F32, b_col=P_Z, add=dh1, tk=256)
    g_in_a = mm_tn(h1, dsmall, name="mla_in_proj_dw_a", tn=P_SMALL)
    g_in_b = mm_tn(h1, dz2, name="mla_in_proj_dw_b", tn=2048)
    dx1, d_mnorm = norm_bwd(x1, w["mla_norm"], dh1, col=0, width=D_MODEL, name="mla_norm_bwd", res=dx2)

    dy1 = mm_nt(dx1, w["pool_w_out"], name="pool_out_proj_dx", out_dtype=F32)
    grads["pool_w_out"] = mm_tn(y1, dx1, name="pool_out_proj_dw")
    dmm, dz1, d_scale = pool_gate_bwd(dy1, mm, uz, sc_pool, name="pool_gate_bwd")
    dpd = gmm_nt(dmm, w["pool_w_group"], name="pool_group_mix_dx")
    grads["pool_w_group"] = gmm_tn(pd, dmm, 4, name="pool_group_mix_dw")
    du = pool_prep_bwd(dpd, name="pool_window_bwd")
    dh0 = mm_nt(du, w["pool_w_in"], name="pool_in_proj_dx_u", out_dtype=F32, tk=1024)
    dh0 = mm_nt(dz1, w["pool_w_in"], name="pool_in_proj_dx_z", out_dtype=F32, b_col=POOL_WIDTH, add=dh0, tk=1024)
    g_pin_u = mm_tn(h0, du, name="pool_in_proj_dw_u", tn=2048)
    g_pin_z = mm_tn(h0, dz1, name="pool_in_proj_dw_z", tn=2048)
    grad_x, d_pnorm = norm_bwd(x0, g_pool, dh0, col=0, width=D_MODEL, name="pool_norm_bwd", res=dx1)

    grads["pool_w_in"] = jnp.concatenate([g_pin_u, g_pin_z], axis=1)
    g_in = jnp.concatenate([g_in_a, g_in_b], axis=1)
    grads["mla_w_in"] = jnp.concatenate([g_in[:, P_Q:P_Z], g_in[:, P_KV:P_KV + KV_LORA],
                                         g_in[:, P_KR:P_KR + QK_ROPE], g_in[:, P_Z:]], axis=1)
    grads["mla_w_q_b"] = g_qb.reshape(Q_LORA, N_HEADS, HEAD_PAD)[:, :, :QK_NOPE + QK_ROPE].reshape(Q_LORA, -1)

    gp = _pack_grads(grads)
    half_rows = PACK_R // 2 // 3
    sib = swap_halves(gp, name="grad_swap_halves")
    pre = add_halves(gp, sib, cc_idx.reshape(1).astype(jnp.int32), name="grad_add_halves", rows=half_rows)
    got = exchange_chips(pre, name="grad_exchange_chips")
    tot = sum_chips(got, name="grad_sum_chips", rows=half_rows)
    red = join_halves(tot, name="grad_join_halves")
    red_parts, _ = _split_rows(red, 0)

    sv = jnp.concatenate([d_pnorm.reshape(-1), d_scale.reshape(-1), d_final.reshape(-1), d_mnorm.reshape(-1),
                          d_qnorm.reshape(-1), d_kvnorm.reshape(-1), loss_part[0, :1],
                          jnp.zeros((SV_ROWS * SV_COLS - SV_OFF["loss"] - 1,), F32)]).reshape(SV_ROWS, SV_COLS)
    sv_all = all_gather8(sv, name="gather_small_grads", own_half=False)
    sv_sum = sum_devices(sv_all, name="sum_small_grads").reshape(-1)
    loss = sv_sum[SV_OFF["loss"]]

    def sv_take(key, n):
        return lax.slice_in_dim(sv_sum, SV_OFF[key], SV_OFF[key] + n)

    out_g, out_d, out_m, out_v = {}, {}, {}, {}
    for name, part in zip(big_names, red_parts):
        shp = big_w[name].shape
        g2 = part.reshape(shp)
        two_d = (-1, shp[-1])
        d_, m_, v_ = adamw(big_w[name].reshape(two_d), g2.reshape(two_d), big_m[name].reshape(two_d),
                           big_v[name].reshape(two_d), name="adamw_" + name)
        out_g[name], out_d[name], out_m[name], out_v[name] = g2, d_.reshape(shp), m_.reshape(shp), v_.reshape(shp)

    small = [
        ("pool_norm", pool_norm, m_pool_norm, v_pool_norm, sv_take("pool_norm", 1024)),
        ("pool_scale", pool_scale, m_pool_scale, v_pool_scale, sv_take("pool_scale", 2048)),
        ("final_norm", final_norm, m_final_norm, v_final_norm, sv_take("final_norm", 1024)),
        ("mla_norm", mla_norm, m_mla_norm, v_mla_norm,
         lax.dynamic_slice_in_dim(sv_take("mla_norm", 1024), chip * 256, 256)),
        ("mla_q_norm", mla_q_norm, m_mla_q_norm, v_mla_q_norm,
         lax.dynamic_slice_in_dim(sv_take("q_norm", 384), chip * 96, 96)),
        ("mla_kv_norm", mla_kv_norm, m_mla_kv_norm, v_mla_kv_norm,
         lax.dynamic_slice_in_dim(sv_take("kv_norm", 256), chip * 64, 64)),
    ]
    sw = jnp.concatenate([t[1].reshape(-1) for t in small] + [jnp.zeros((96,), F32)]).reshape(1, -1)
    sm = jnp.concatenate([t[2].reshape(-1) for t in small] + [jnp.zeros((96,), F32)]).reshape(1, -1)
    s_v = jnp.concatenate([t[3].reshape(-1) for t in small] + [jnp.ones((96,), F32)]).reshape(1, -1)
    sg = jnp.concatenate([t[4].reshape(-1) for t in small] + [jnp.zeros((96,), F32)]).reshape(1, -1)
    sd_, sm_, sv_ = adamw(sw, sg, sm, s_v, name="adamw_vectors")
    off = 0
    for name, wt, _, _, gvec in small:
        n = gvec.shape[0]
        shp = wt.shape
        out_g[name] = gvec.reshape(shp)
        out_d[name] = sd_[0, off:off + n].reshape(shp)
        out_m[name] = sm_[0, off:off + n].reshape(shp)
        out_v[name] = sv_[0, off:off + n].reshape(shp)
        off += n

    order = ("pool_norm", "pool_w_in", "pool_w_group", "pool_scale", "pool_w_out", "mla_norm", "mla_w_in",
             "mla_q_norm", "mla_w_q_b", "mla_kv_norm", "mla_w_kv_b", "mla_w_out", "final_norm")
    return (loss, grad_x.reshape(x.shape), *[out_g[n] for n in order], *[out_d[n] for n in order],
            *[out_m[n] for n in order], *[out_v[n] for n in order])
```

```python
import functools

import jax
import jax.numpy as jnp
from jax import lax
from jax.experimental import pallas as pl
from jax.experimental.pallas import tpu as pltpu

F32 = jnp.float32
BF = jnp.bfloat16
MESH = pl.DeviceIdType.MESH

D_MODEL = 1024
POOL_WIDTH = 2048
POOL_WINDOWS = (2, 4, 8, 16)
POOL_GROUP = 512
HALO = 16
N_HEADS = 16
QK_NOPE = 128
QK_ROPE = 64
V_DIM = 128
HEAD_PAD = 256
Q_LORA = 384
KV_LORA = 256
MLA_WIDTH = 2048
ROPE_THETA = 10000.0
EPS = 1e-6
SCALE = (QK_NOPE + QK_ROPE) ** -0.5
SCALE_LOG2E = SCALE * 1.4426950408889634
NEG = -1e30

P_KV, P_KR, P_Q, P_Z = 0, 256, 384, 768
P_SMALL = 768
P_WIDTH = 2816

ADAM_LR = 0.001
ADAM_B1 = 0.9
ADAM_B2 = 0.999
ADAM_EPS = 1e-08
ADAM_WD = 0.01
ADAM_STEP = 10

NN = (((1,), (0,)), ((), ()))
NT = (((1,), (1,)), ((), ()))
TN = (((0,), (0,)), ((), ()))

PACK_ROWS = (1024, 256, 512, 688, 288, 256, 512)
PACK_PAD = 16
PACK_R = sum(PACK_ROWS) + PACK_PAD
PACK_C = 1024
SV_OFF = dict(pool_norm=0, pool_scale=1024, final_norm=3072, mla_norm=4096, q_norm=5120, kv_norm=5504, loss=5760)
SV_ROWS, SV_COLS = 8, 768

VMEM_LIMIT = 56 * 1024 * 1024


def _params(n_axes, vmem=None):
    return pltpu.CompilerParams(dimension_semantics=("arbitrary",) * n_axes,
                                vmem_limit_bytes=VMEM_LIMIT if vmem is None else vmem)


def _sigmoid(z):
    return 1.0 / (1.0 + jnp.exp(-z))


def _mm(a, b, *, dims, grid, a_spec, b_spec, o_spec, out_shape, out_dtype, acc_shape, name,
        add=None, add_spec=None):
    nk = grid[-1]
    kax = len(grid) - 1

    def body(*refs):
        if add is None:
            a_ref, b_ref, o_ref = refs[:3]
            add_ref = None
            rest = refs[3:]
        else:
            a_ref, b_ref, add_ref, o_ref = refs[:4]
            rest = refs[4:]
        part = lax.dot_general(a_ref[...].astype(BF), b_ref[...].astype(BF), dims,
                               preferred_element_type=F32)

        def finish(r):
            if add_ref is not None:
                r = r + add_ref[...]
            o_ref[...] = r.astype(o_ref.dtype)

        if nk == 1:
            finish(part)
        else:
            acc = rest[0]
            k = pl.program_id(kax)

            @pl.when(k == 0)
            def _():
                acc[...] = part

            @pl.when(k > 0)
            def _():
                acc[...] += part

            @pl.when(k == nk - 1)
            def _():
                finish(acc[...])

    in_specs = [a_spec, b_spec]
    args = [a, b]
    if add is not None:
        in_specs.append(add_spec)
        args.append(add)
    scratch = [] if nk == 1 else [pltpu.VMEM(acc_shape, F32)]
    return pl.pallas_call(
        body, name=name, grid=grid, in_specs=in_specs, out_specs=o_spec,
        out_shape=jax.ShapeDtypeStruct(out_shape, out_dtype), scratch_shapes=scratch,
        compiler_params=_params(len(grid)))(*args)


def _pick(n, t):
    t = min(n, t)
    assert n % t == 0, (n, t)
    return t


def mm_nn(a, b, *, name, out_dtype, a_col=0, k_size=None, add=None, tm=512, tn=1024, tk=1024):
    m = a.shape[0]
    kk, n = b.shape
    assert k_size is None or k_size == kk
    tm, tn, tk = _pick(m, tm), _pick(n, tn), _pick(kk, tk)
    assert a_col % tk == 0
    ko = a_col // tk
    return _mm(a, b, dims=NN, grid=(m // tm, n // tn, kk // tk),
               a_spec=pl.BlockSpec((tm, tk), lambda i, j, k: (i, ko + k)),
               b_spec=pl.BlockSpec((tk, tn), lambda i, j, k: (k, j)),
               o_spec=pl.BlockSpec((tm, tn), lambda i, j, k: (i, j)),
               add=add, add_spec=pl.BlockSpec((tm, tn), lambda i, j, k: (i, j)),
               out_shape=(m, n), out_dtype=out_dtype, acc_shape=(tm, tn), name=name)


def mm_nt(a, b, *, name, out_dtype, b_col=0, add=None, tm=512, tn=1024, tk=1024):
    m, kk = a.shape
    n = b.shape[0]
    tm, tn, tk = _pick(m, tm), _pick(n, tn), _pick(kk, tk)
    assert b_col % tk == 0
    ko = b_col // tk
    return _mm(a, b, dims=NT, grid=(m // tm, n // tn, kk // tk),
               a_spec=pl.BlockSpec((tm, tk), lambda i, j, k: (i, k)),
               b_spec=pl.BlockSpec((tn, tk), lambda i, j, k: (j, ko + k)),
               o_spec=pl.BlockSpec((tm, tn), lambda i, j, k: (i, j)),
               add=add, add_spec=pl.BlockSpec((tm, tn), lambda i, j, k: (i, j)),
               out_shape=(m, n), out_dtype=out_dtype, acc_shape=(tm, tn), name=name)


def mm_tn(a, b, *, name, a_col=0, m_size=None, b_col=0, n_size=None, tm=1024, tn=1024, tk=512):
    s = a.shape[0]
    m = a.shape[1] if m_size is None else m_size
    n = b.shape[1] if n_size is None else n_size
    tm, tn, tk = _pick(m, tm), _pick(n, tn), _pick(s, tk)
    assert a_col % tm == 0 and b_col % tn == 0
    ao, bo = a_col // tm, b_col // tn
    return _mm(a, b, dims=TN, grid=(m // tm, n // tn, s // tk),
               a_spec=pl.BlockSpec((tk, tm), lambda i, j, k: (k, ao + i)),
               b_spec=pl.BlockSpec((tk, tn), lambda i, j, k: (k, bo + j)),
               o_spec=pl.BlockSpec((tm, tn), lambda i, j, k: (i, j)),
               out_shape=(m, n), out_dtype=F32, acc_shape=(tm, tn), name=name)


def gmm_nn(a, w, *, name, tm=512):
    s = a.shape[0]
    g, kk, n = w.shape
    tm = _pick(s, tm)
    return _mm(a, w, dims=NN, grid=(s // tm, g, 1),
               a_spec=pl.BlockSpec((tm, kk), lambda i, gi, k: (i, gi)),
               b_spec=pl.BlockSpec((None, kk, n), lambda i, gi, k: (gi, 0, 0)),
               o_spec=pl.BlockSpec((tm, n), lambda i, gi, k: (i, gi)),
               out_shape=(s, g * n), out_dtype=F32, acc_shape=(tm, n), name=name)


def gmm_nt(a, w, *, name, tm=512):
    s = a.shape[0]
    g, kk, n = w.shape
    tm = _pick(s, tm)
    return _mm(a, w, dims=NT, grid=(s // tm, g, 1),
               a_spec=pl.BlockSpec((tm, n), lambda i, gi, k: (i, gi)),
               b_spec=pl.BlockSpec((None, kk, n), lambda i, gi, k: (gi, 0, 0)),
               o_spec=pl.BlockSpec((tm, kk), lambda i, gi, k: (i, gi)),
               out_shape=(s, g * kk), out_dtype=F32, acc_shape=(tm, kk), name=name)


def gmm_tn(a, b, g, *, name, tk=512):
    s = a.shape[0]
    kk, n = a.shape[1] // g, b.shape[1] // g
    tk = _pick(s, tk)
    return _mm(a, b, dims=TN, grid=(g, s // tk),
               a_spec=pl.BlockSpec((tk, kk), lambda gi, k: (k, gi)),
               b_spec=pl.BlockSpec((tk, n), lambda gi, k: (k, gi)),
               o_spec=pl.BlockSpec((None, kk, n), lambda gi, k: (gi, 0, 0)),
               out_shape=(g, kk, n), out_dtype=F32, acc_shape=(kk, n), name=name)


def norm_fwd(x, g, *, col, width, name, t=512):
    s = x.shape[0]
    t = _pick(s, t)
    cb = col // width
    assert col % width == 0

    def body(x_ref, g_ref, o_ref):
        xv = x_ref[...]
        inv = lax.rsqrt(jnp.mean(xv * xv, axis=-1, keepdims=True) + EPS)
        o_ref[...] = ((xv * inv) * g_ref[...]).astype(o_ref.dtype)

    return pl.pallas_call(
        body, name=name, grid=(s // t,),
        in_specs=[pl.BlockSpec((t, width), lambda i: (i, cb)), pl.BlockSpec((1, width), lambda i: (0, 0))],
        out_specs=pl.BlockSpec((t, width), lambda i: (i, 0)),
        out_shape=jax.ShapeDtypeStruct((s, width), BF), compiler_params=_params(1))(x, g)


def norm_bwd(x, g, dh, *, col, width, name, res=None, out_dtype=F32, t=512):
    s = x.shape[0]
    t = _pick(s, t)
    cb = col // width
    assert col % width == 0

    def body(*refs):
        if res is None:
            x_ref, g_ref, dh_ref, dx_ref, dg_ref = refs
        else:
            x_ref, g_ref, dh_ref, res_ref, dx_ref, dg_ref = refs
        xv = x_ref[...]
        inv = lax.rsqrt(jnp.mean(xv * xv, axis=-1, keepdims=True) + EPS)
        xhat = xv * inv
        dh_v = dh_ref[...]
        part = jnp.sum(dh_v * xhat, axis=0, keepdims=True)

        @pl.when(pl.program_id(0) == 0)
        def _():
            dg_ref[...] = part

        @pl.when(pl.program_id(0) > 0)
        def _():
            dg_ref[...] += part

        dxhat = dh_v * g_ref[...]
        dx = inv * (dxhat - xhat * jnp.mean(dxhat * xhat, axis=-1, keepdims=True))
        if res is not None:
            dx = dx + res_ref[...]
        dx_ref[...] = dx.astype(dx_ref.dtype)

    row = pl.BlockSpec((t, width), lambda i: (i, 0))
    vec = pl.BlockSpec((1, width), lambda i: (0, 0))
    in_specs = [pl.BlockSpec((t, width), lambda i: (i, cb)), vec, row]
    args = [x, g, dh]
    if res is not None:
        in_specs.append(row)
        args.append(res)
    return pl.pallas_call(
        body, name=name, grid=(s // t,), in_specs=in_specs, out_specs=[row, vec],
        out_shape=[jax.ShapeDtypeStruct((s, width), out_dtype), jax.ShapeDtypeStruct((1, width), F32)],
        compiler_params=_params(1))(*args)


def final_loss(x2, gf, tgt, *, name, t=512):
    s, d = x2.shape
    t = _pick(s, t)

    def body(x_ref, g_ref, t_ref, dx_ref, dg_ref, loss_ref):
        xv = x_ref[...]
        inv = lax.rsqrt(jnp.mean(xv * xv, axis=-1, keepdims=True) + EPS)
        xhat = xv * inv
        gv = g_ref[...]
        diff = xhat * gv - t_ref[...]
        row_err = jnp.mean(diff * diff, axis=-1, keepdims=True)
        lpart = jnp.broadcast_to(0.5 * jnp.sum(row_err, axis=0, keepdims=True), (1, 128))
        dout = diff * (1.0 / d)
        gpart = jnp.sum(dout * xhat, axis=0, keepdims=True)

        @pl.when(pl.program_id(0) == 0)
        def _():
            dg_ref[...] = gpart
            loss_ref[...] = lpart

        @pl.when(pl.program_id(0) > 0)
        def _():
            dg_ref[...] += gpart
            loss_ref[...] += lpart

        dxhat = dout * gv
        dx_ref[...] = inv * (dxhat - xhat * jnp.mean(dxhat * xhat, axis=-1, keepdims=True))

    row = pl.BlockSpec((t, d), lambda i: (i, 0))
    vec = pl.BlockSpec((1, d), lambda i: (0, 0))
    return pl.pallas_call(
        body, name=name, grid=(s // t,), in_specs=[row, vec, row],
        out_specs=[row, vec, pl.BlockSpec((1, 128), lambda i: (0, 0))],
        out_shape=[jax.ShapeDtypeStruct((s, d), F32), jax.ShapeDtypeStruct((1, d), F32),
                   jax.ShapeDtypeStruct((1, 128), F32)],
        compiler_params=_params(1))(x2, gf, tgt)


def pool_prep(uz, *, name, t=256):
    s = uz.shape[0]
    t = _pick(s, t)
    hb = t // HALO

    def body(u_ref, halo_ref, o_ref, buf):
        i = pl.program_id(0)
        buf[pl.ds(HALO, t), :] = u_ref[...]

        @pl.when(i == 0)
        def _():
            buf[pl.ds(0, HALO), :] = jnp.zeros((HALO, POOL_WIDTH), F32)

        @pl.when(i > 0)
        def _():
            buf[pl.ds(0, HALO), :] = halo_ref[...]

        pos = i * t + lax.broadcasted_iota(jnp.int32, (t, POOL_GROUP), 0)
        for g, w in enumerate(POOL_WINDOWS):
            cols = pl.ds(g * POOL_GROUP, POOL_GROUP)
            cur = buf[pl.ds(HALO, t), cols]
            acc = cur
            for k in range(1, w):
                acc = acc + buf[pl.ds(HALO - k, t), cols]
            cnt = jnp.minimum(pos + 1, w).astype(F32)
            o_ref[:, cols] = (acc / cnt - cur).astype(o_ref.dtype)

    return pl.pallas_call(
        body, name=name, grid=(s // t,),
        in_specs=[pl.BlockSpec((t, POOL_WIDTH), lambda i: (i, 0)),
                  pl.BlockSpec((HALO, POOL_WIDTH), lambda i: (jnp.maximum(i * hb - 1, 0), 0))],
        out_specs=pl.BlockSpec((t, POOL_WIDTH), lambda i: (i, 0)),
        out_shape=jax.ShapeDtypeStruct((s, POOL_WIDTH), BF),
        scratch_shapes=[pltpu.VMEM((t + HALO, POOL_WIDTH), F32)],
        compiler_params=_params(1))(uz, uz)


def pool_prep_bwd(dpd, *, name, t=256):
    s = dpd.shape[0]
    t = _pick(s, t)
    hb = t // HALO
    n = s // t

    def body(d_ref, halo_ref, o_ref, buf):
        i = pl.program_id(0)
        pos = i * t + lax.broadcasted_iota(jnp.int32, (t, POOL_GROUP), 0)
        for g, w in enumerate(POOL_WINDOWS):
            cols = pl.ds(g * POOL_GROUP, POOL_GROUP)
            cnt = jnp.minimum(pos + 1, w).astype(F32)
            buf[pl.ds(0, t), cols] = d_ref[:, cols] / cnt

            @pl.when(i < n - 1)
            def _():
                buf[pl.ds(t, HALO), cols] = halo_ref[:, cols] / float(w)

            @pl.when(i == n - 1)
            def _():
                buf[pl.ds(t, HALO), cols] = jnp.zeros((HALO, POOL_GROUP), F32)

        for g, w in enumerate(POOL_WINDOWS):
            cols = pl.ds(g * POOL_GROUP, POOL_GROUP)
            acc = buf[pl.ds(0, t), cols]
            for k in range(1, w):
                acc = acc + buf[pl.ds(k, t), cols]
            o_ref[:, cols] = (acc - d_ref[:, cols]).astype(o_ref.dtype)

    return pl.pallas_call(
        body, name=name, grid=(n,),
        in_specs=[pl.BlockSpec((t, POOL_WIDTH), lambda i: (i, 0)),
                  pl.BlockSpec((HALO, POOL_WIDTH), lambda i: (jnp.minimum((i + 1) * hb, n * hb - 1), 0))],
        out_specs=pl.BlockSpec((t, POOL_WIDTH), lambda i: (i, 0)),
        out_shape=jax.ShapeDtypeStruct((s, POOL_WIDTH), BF),
        scratch_shapes=[pltpu.VMEM((t + HALO, POOL_WIDTH), F32)],
        compiler_params=_params(1))(dpd, dpd)


def pool_gate(mm, uz, scale, *, name, t=512, cw=512):
    s = mm.shape[0]
    t = _pick(s, t)
    zo = POOL_WIDTH // cw

    def body(mm_ref, z_ref, sc_ref, y_ref):
        z = z_ref[...]
        y_ref[...] = ((mm_ref[...] * sc_ref[...]) * (z * _sigmoid(z))).astype(y_ref.dtype)

    blk = pl.BlockSpec((t, cw), lambda i, j: (i, j))
    return pl.pallas_call(
        body, name=name, grid=(s // t, POOL_WIDTH // cw),
        in_specs=[blk, pl.BlockSpec((t, cw), lambda i, j: (i, zo + j)), pl.BlockSpec((1, cw), lambda i, j: (0, j))],
        out_specs=blk, out_shape=jax.ShapeDtypeStruct((s, POOL_WIDTH), BF),
        compiler_params=_params(2))(mm, uz, scale)


def pool_gate_bwd(dy, mm, uz, scale, *, name, t=512, cw=512):
    s = mm.shape[0]
    t = _pick(s, t)
    zo = POOL_WIDTH // cw

    def body(dy_ref, mm_ref, z_ref, sc_ref, dmm_ref, dz_ref, dsc_ref):
        z = z_ref[...]
        sig = _sigmoid(z)
        silu = z * sig
        dyv = dy_ref[...]
        mmv = mm_ref[...]
        scv = sc_ref[...]
        dmixed = dyv * silu
        dmm_ref[...] = (dmixed * scv).astype(dmm_ref.dtype)
        dz_ref[...] = (dyv * (mmv * scv) * (sig * (1.0 + z * (1.0 - sig)))).astype(dz_ref.dtype)
        part = jnp.sum(dmixed * mmv, axis=0, keepdims=True)

        @pl.when(pl.program_id(1) == 0)
        def _():
            dsc_ref[...] = part

        @pl.when(pl.program_id(1) > 0)
        def _():
            dsc_ref[...] += part

    blk = pl.BlockSpec((t, cw), lambda j, i: (i, j))
    vec = pl.BlockSpec((1, cw), lambda j, i: (0, j))
    return pl.pallas_call(
        body, name=name, grid=(POOL_WIDTH // cw, s // t),
        in_specs=[blk, blk, pl.BlockSpec((t, cw), lambda j, i: (i, zo + j)), vec],
        out_specs=[blk, blk, vec],
        out_shape=[jax.ShapeDtypeStruct((s, POOL_WIDTH), BF), jax.ShapeDtypeStruct((s, POOL_WIDTH), BF),
                   jax.ShapeDtypeStruct((1, POOL_WIDTH), F32)],
        compiler_params=_params(2))(dy, mm, uz, scale)


def _rope(a, cc, sa, sb):
    return a * cc + pltpu.roll(a, 96, 1) * sa + pltpu.roll(a, 32, 1) * sb


def _unrope(d, cc, sa, sb):
    return d * cc + pltpu.roll(d * sa, 32, 1) + pltpu.roll(d * sb, 96, 1)


def rope_q(q_pre, cc, sa, sb, *, name, t=512):
    s = q_pre.shape[0]
    t = _pick(s, t)

    def body(q_ref, cc_ref, sa_ref, sb_ref, o_ref):
        o_ref[:, :QK_NOPE] = q_ref[:, :QK_NOPE].astype(o_ref.dtype)
        o_ref[:, QK_NOPE:] = _rope(q_ref[:, QK_NOPE:], cc_ref[...], sa_ref[...], sb_ref[...]).astype(o_ref.dtype)

    tab = pl.BlockSpec((t, 128), lambda i, h: (i, 0))
    blk = pl.BlockSpec((t, HEAD_PAD), lambda i, h: (i, h))
    return pl.pallas_call(
        body, name=name, grid=(s // t, N_HEADS), in_specs=[blk, tab, tab, tab], out_specs=blk,
        out_shape=jax.ShapeDtypeStruct((s, N_HEADS * HEAD_PAD), BF), compiler_params=_params(2))(q_pre, cc, sa, sb)


def pack_k(kv, proj, cc, sa, sb, *, name, t=512):
    s = kv.shape[0]
    t = _pick(s, t)
    kr_blk = P_KR // 128

    def body(kn_ref, kr_ref, cc_ref, sa_ref, sb_ref, o_ref):
        o_ref[:, :QK_NOPE] = kn_ref[...]
        o_ref[:, QK_NOPE:] = _rope(kr_ref[...], cc_ref[...], sa_ref[...], sb_ref[...]).astype(o_ref.dtype)

    tab = pl.BlockSpec((t, 128), lambda i, h: (i, 0))
    return pl.pallas_call(
        body, name=name, grid=(s // t, N_HEADS),
        in_specs=[pl.BlockSpec((t, 128), lambda i, h: (i, 2 * h)),
                  pl.BlockSpec((t, 128), lambda i, h: (i, kr_blk)), tab, tab, tab],
        out_specs=pl.BlockSpec((t, HEAD_PAD), lambda i, h: (i, h)),
        out_shape=jax.ShapeDtypeStruct((s, N_HEADS * HEAD_PAD), BF), compiler_params=_params(2))(kv, proj, cc, sa, sb)


def unrope_q(dqr, cc, sa, sb, *, name, t=512):
    s = dqr.shape[0]
    t = _pick(s, t)

    def body(d_ref, cc_ref, sa_ref, sb_ref, o_ref):
        o_ref[:, :QK_NOPE] = (d_ref[:, :QK_NOPE] * SCALE).astype(o_ref.dtype)
        o_ref[:, QK_NOPE:] = _unrope(d_ref[:, QK_NOPE:] * SCALE, cc_ref[...], sa_ref[...], sb_ref[...]).astype(o_ref.dtype)

    tab = pl.BlockSpec((t, 128), lambda i, h: (i, 0))
    blk = pl.BlockSpec((t, HEAD_PAD), lambda i, h: (i, h))
    return pl.pallas_call(
        body, name=name, grid=(s // t, N_HEADS), in_specs=[blk, tab, tab, tab], out_specs=blk,
        out_shape=jax.ShapeDtypeStruct((s, N_HEADS * HEAD_PAD), BF), compiler_params=_params(2))(dqr, cc, sa, sb)


def unrope_k(dkr, cc, sa, sb, *, name, t=512):
    s = dkr.shape[0]
    t = _pick(s, t)

    def body(d_ref, cc_ref, sa_ref, sb_ref, o_ref):
        acc = d_ref[:, pl.ds(0, 128)]
        for h in range(1, N_HEADS):
            acc = acc + d_ref[:, pl.ds(h * 128, 128)]
        o_ref[...] = _unrope(acc, cc_ref[...], sa_ref[...], sb_ref[...]).astype(o_ref.dtype)

    tab = pl.BlockSpec((t, 128), lambda i: (i, 0))
    return pl.pallas_call(
        body, name=name, grid=(s // t,),
        in_specs=[pl.BlockSpec((t, N_HEADS * 128), lambda i: (i, 0)), tab, tab, tab], out_specs=tab,
        out_shape=jax.ShapeDtypeStruct((s, 128), BF), compiler_params=_params(1))(dkr, cc, sa, sb)


def mla_gate(o, proj, *, name, t=512, cw=256):
    s = o.shape[0]
    t = _pick(s, t)
    zo = P_Z // cw

    def body(o_ref, z_ref, y_ref):
        z = z_ref[...]
        y_ref[...] = (o_ref[...] * (z * _sigmoid(z))).astype(y_ref.dtype)

    blk = pl.BlockSpec((t, cw), lambda i, j: (i, j))
    return pl.pallas_call(
        body, name=name, grid=(s // t, MLA_WIDTH // cw),
        in_specs=[blk, pl.BlockSpec((t, cw), lambda i, j: (i, zo + j))], out_specs=blk,
        out_shape=jax.ShapeDtypeStruct((s, MLA_WIDTH), BF), compiler_params=_params(2))(o, proj)


def mla_gate_bwd(dy, o, proj, *, name, tq):
    s = o.shape[0]
    nq = s // tq
    zo = P_Z // 128

    def body(dy_ref, o_ref, z_ref, do_ref, dz_ref, dl_ref):
        z = z_ref[...]
        sig = _sigmoid(z)
        dyv = dy_ref[...]
        ov = o_ref[...]
        dov = dyv * (z * sig)
        do_ref[...] = dov.astype(do_ref.dtype)
        dz_ref[...] = (dyv * ov * (sig * (1.0 + z * (1.0 - sig)))).astype(dz_ref.dtype)
        delta = jnp.sum(dov * ov, axis=-1, keepdims=True)
        dl_ref[...] = jnp.broadcast_to(delta, (tq, 128)).T[:8, :]

    blk = pl.BlockSpec((tq, 128), lambda i, h: (i, h))
    return pl.pallas_call(
        body, name=name, grid=(nq, N_HEADS),
        in_specs=[blk, blk, pl.BlockSpec((tq, 128), lambda i, h: (i, zo + h))],
        out_specs=[blk, blk, pl.BlockSpec((None, None, 8, tq), lambda i, h: (h, i, 0, 0))],
        out_shape=[jax.ShapeDtypeStruct((s, MLA_WIDTH), BF), jax.ShapeDtypeStruct((s, MLA_WIDTH), BF),
                   jax.ShapeDtypeStruct((N_HEADS, nq, 8, tq), F32)],
        compiler_params=_params(2))(dy, o, proj)


def attn_fwd(qr, kc, kv, *, name, tq):
    s = qr.shape[0]
    nq = s // tq

    def body(q_ref, k_ref, v_ref, o_ref, lse_ref, acc_sc, vx_sc):
        i = pl.program_id(1)

        @pl.when(i == 0)
        def _():
            vx_sc[:, :V_DIM] = v_ref[...]
            vx_sc[:, V_DIM:] = jnp.ones((s, V_DIM), BF)

        q = q_ref[...]
        acc_sc[...] = jnp.zeros((tq, 2 * V_DIM), F32)

        def scores(j):
            k = k_ref[pl.ds(pl.multiple_of(j * tq, tq), tq), :]
            return lax.dot_general(k, q, NT, preferred_element_type=F32)

        def tile(j, st, m_prev, masked):
            st = st * SCALE_LOG2E
            if masked:
                krow = lax.broadcasted_iota(jnp.int32, (tq, tq), 0)
                qcol = lax.broadcasted_iota(jnp.int32, (tq, tq), 1)
                st = jnp.where(qcol >= krow, st, NEG)
            m_new = jnp.maximum(m_prev, jnp.max(st, axis=0, keepdims=True))
            alpha_c = jnp.broadcast_to(jnp.exp2(m_prev - m_new), (128, tq)).T
            pt = jnp.exp2(st - m_new).astype(BF)
            vx = vx_sc[pl.ds(pl.multiple_of(j * tq, tq), tq), :]
            pv = lax.dot_general(pt, vx, TN, preferred_element_type=F32)
            for cols in (slice(0, V_DIM), slice(V_DIM, 2 * V_DIM)):
                acc_sc[:, cols] = alpha_c * acc_sc[:, cols] + pv[:, cols]
            return m_new

        def pair(j0, m, mask_second):
            st_a, st_b = scores(j0), scores(j0 + 1)
            return tile(j0 + 1, st_b, tile(j0, st_a, m, False), mask_second)

        m = lax.fori_loop(0, i // 2, lambda jj, m: pair(2 * jj, m, False), jnp.full((1, tq), NEG, F32))
        m = lax.cond(i % 2 == 1, lambda m: pair(i - 1, m, True), lambda m: tile(i, scores(i), m, True), m)
        l = acc_sc[:, V_DIM:]
        o_ref[...] = acc_sc[:, :V_DIM] / l
        lse_ref[...] = jnp.broadcast_to(m, (8, tq)) + jnp.log2(l).T[:8, :]

    return pl.pallas_call(
        body, name=name, grid=(N_HEADS, nq),
        in_specs=[pl.BlockSpec((tq, HEAD_PAD), lambda h, i: (i, h)),
                  pl.BlockSpec((s, HEAD_PAD), lambda h, i: (0, h)),
                  pl.BlockSpec((s, V_DIM), lambda h, i: (0, 2 * h + 1))],
        out_specs=[pl.BlockSpec((tq, V_DIM), lambda h, i: (i, h)),
                   pl.BlockSpec((None, None, 8, tq), lambda h, i: (h, i, 0, 0))],
        out_shape=[jax.ShapeDtypeStruct((s, N_HEADS * V_DIM), F32),
                   jax.ShapeDtypeStruct((N_HEADS, nq, 8, tq), F32)],
        scratch_shapes=[pltpu.VMEM((tq, 2 * V_DIM), F32), pltpu.VMEM((s, 2 * V_DIM), BF)],
        compiler_params=_params(2))(qr, kc, kv)


def attn_bwd(qr, kc, kv, do, lse, delta, *, name, tq):
    s = qr.shape[0]
    nq = s // tq

    def body(k_ref, v_ref, q_ref, do_ref, lse_ref, dl_ref, dkv_ref, dkr_ref, dq_ref, dk_sc, dv_sc):
        j = pl.program_id(1)

        @pl.when(j == 0)
        def _():
            dq_ref[...] = jnp.zeros((s, HEAD_PAD), F32)

        dk_sc[...] = jnp.zeros((tq, HEAD_PAD), F32)
        dv_sc[...] = jnp.zeros((tq, V_DIM), F32)
        k = k_ref[...]
        v = v_ref[...]

        def step(i, masked):
            r0 = pl.multiple_of(i * tq, tq)
            q = q_ref[pl.ds(r0, tq), :]
            dov = do_ref[pl.ds(r0, tq), :]
            lse_row = lse_ref[i, pl.ds(0, 1), :]
            dl_row = dl_ref[i, pl.ds(0, 1), :]
            st = lax.dot_general(k, q, NT, preferred_element_type=F32) * SCALE_LOG2E
            if masked:
                krow = lax.broadcasted_iota(jnp.int32, (tq, tq), 0)
                qcol = lax.broadcasted_iota(jnp.int32, (tq, tq), 1)
                st = jnp.where(qcol >= krow, st, NEG)
            pt = jnp.exp2(st - lse_row)
            dpt = lax.dot_general(v, dov, NT, preferred_element_type=F32)
            dst = (pt * (dpt - dl_row)).astype(BF)
            dv_sc[...] += jnp.dot(pt.astype(BF), dov, preferred_element_type=F32)
            dk_sc[...] += jnp.dot(dst, q, preferred_element_type=F32)
            dq_ref[pl.ds(r0, tq), :] += lax.dot_general(dst, k, TN, preferred_element_type=F32)

        def unmasked(i, carry):
            step(i, False)
            return carry

        step(j, True)
        lax.fori_loop(j + 1, nq, unmasked, 0)
        dkv_ref[:, :QK_NOPE] = (dk_sc[:, :QK_NOPE] * SCALE).astype(dkv_ref.dtype)
        dkv_ref[:, QK_NOPE:] = dv_sc[...].astype(dkv_ref.dtype)
        dkr_ref[...] = dk_sc[:, QK_NOPE:] * SCALE

    rows = pl.BlockSpec((None, nq, 8, tq), lambda h, j: (h, 0, 0, 0))
    return pl.pallas_call(
        body, name=name, grid=(N_HEADS, nq),
        in_specs=[pl.BlockSpec((tq, HEAD_PAD), lambda h, j: (j, h)),
                  pl.BlockSpec((tq, V_DIM), lambda h, j: (j, 2 * h + 1)),
                  pl.BlockSpec((s, HEAD_PAD), lambda h, j: (0, h)),
                  pl.BlockSpec((s, V_DIM), lambda h, j: (0, h)), rows, rows],
        out_specs=[pl.BlockSpec((tq, 256), lambda h, j: (j, h)),
                   pl.BlockSpec((tq, 128), lambda h, j: (j, h)),
                   pl.BlockSpec((s, HEAD_PAD), lambda h, j: (0, h))],
        out_shape=[jax.ShapeDtypeStruct((s, N_HEADS * 256), BF),
                   jax.ShapeDtypeStruct((s, N_HEADS * 128), F32),
                   jax.ShapeDtypeStruct((s, N_HEADS * HEAD_PAD), F32)],
        scratch_shapes=[pltpu.VMEM((tq, HEAD_PAD), F32), pltpu.VMEM((tq, V_DIM), F32)],
        compiler_params=_params(2))(kc, kv, qr, do, lse, delta)


def adamw(w, g, m, v, *, name, t=256):
    r, c = w.shape
    t = r if r % t else t
    c1 = 1.0 - ADAM_B1 ** ADAM_STEP
    c2 = 1.0 - ADAM_B2 ** ADAM_STEP

    def body(w_ref, g_ref, m_ref, v_ref, d_ref, nm_ref, nv_ref):
        gv = g_ref[...]
        nm = ADAM_B1 * m_ref[...] + (1.0 - ADAM_B1) * gv
        nv = ADAM_B2 * v_ref[...] + (1.0 - ADAM_B2) * (gv * gv)
        nm_ref[...] = nm
        nv_ref[...] = nv
        d_ref[...] = -ADAM_LR * ((nm / c1) / (jnp.sqrt(nv / c2) + ADAM_EPS) + ADAM_WD * w_ref[...])

    blk = pl.BlockSpec((t, c), lambda i: (i, 0))
    return pl.pallas_call(
        body, name=name, grid=(r // t,), in_specs=[blk] * 4, out_specs=[blk] * 3,
        out_shape=[jax.ShapeDtypeStruct((r, c), F32)] * 3, compiler_params=_params(1))(w, g, m, v)


def sum_devices(parts, *, name):
    def body(p_ref, o_ref):
        acc = p_ref[pl.ds(0, SV_ROWS), :]
        for d in range(1, 8):
            acc = acc + p_ref[pl.ds(d * SV_ROWS, SV_ROWS), :]
        o_ref[...] = acc

    return pl.pallas_call(body, name=name, out_shape=jax.ShapeDtypeStruct((SV_ROWS, SV_COLS), F32))(parts)


def add_halves(g, rb, c_idx, *, name, rows):
    nq, r2, cc = rb.shape
    nb = r2 // rows

    def body(c_ref, g_ref, r_ref, o_ref):
        o_ref[...] = g_ref[...] + r_ref[...]

    grid_spec = pltpu.PrefetchScalarGridSpec(
        num_scalar_prefetch=1, grid=(nq, nb),
        in_specs=[pl.BlockSpec((None, rows, cc), lambda q, i, c: (q, c[0] * nb + i, 0)),
                  pl.BlockSpec((None, rows, cc), lambda q, i, c: (q, i, 0))],
        out_specs=pl.BlockSpec((None, rows, cc), lambda q, i, c: (q, i, 0)))
    return pl.pallas_call(body, name=name, grid_spec=grid_spec,
                          out_shape=jax.ShapeDtypeStruct((nq, r2, cc), F32),
                          compiler_params=_params(2))(c_idx, g, rb)


def sum_chips(rc, *, name, rows):
    nq, r2, cc = rc.shape

    def body(r_ref, o_ref):
        o_ref[...] = ((r_ref[0] + r_ref[1]) + r_ref[2]) + r_ref[3]

    return pl.pallas_call(
        body, name=name, grid=(r2 // rows,),
        in_specs=[pl.BlockSpec((nq, rows, cc), lambda i: (0, i, 0))],
        out_specs=pl.BlockSpec((rows, cc), lambda i: (i, 0)),
        out_shape=jax.ShapeDtypeStruct((r2, cc), F32), compiler_params=_params(1))(rc)


def _place():
    return lax.axis_index("x"), lax.axis_index("y"), lax.axis_index("c")


def all_gather8(xs, *, name, own_half):
    m = xs.shape[0] // 2 if own_half else xs.shape[0]
    n = xs.shape[1]

    def body(x_ref, out_ref, send_sems, recv_sems, local_sem):
        x, y, c = _place()
        me, sibling = (x, y, c), (x, y, 1 - c)
        chips = [(1 - x, y), (x, 1 - y), (1 - x, 1 - y)]
        src_own = x_ref.at[pl.ds(c * m, m), :] if own_half else x_ref

        def rows(px, py, pc):
            return out_ref.at[pl.ds((4 * px + 2 * py + pc) * m, m), :]

        def copy(k, block, to, src=None):
            return pltpu.make_async_remote_copy(
                src_ref=rows(*block) if src is None else src, dst_ref=rows(*block),
                send_sem=send_sems.at[k], recv_sem=recv_sems.at[k], device_id=to, device_id_type=MESH)

        mine = pltpu.make_async_copy(src_own, rows(*me), local_sem)
        mine.start()
        first = [copy(0, me, sibling, src=src_own)]
        first += [copy(1 + j, me, (*chip, c), src=src_own) for j, chip in enumerate(chips)]
        for cp in first:
            cp.start()
        passed = [copy(4 + j, (*chip, c), sibling) for j, chip in enumerate(chips)]
        for j, chip in enumerate(chips):
            copy(1 + j, (*chip, c), me).wait_recv()
            passed[j].start()
        copy(0, sibling, me).wait_recv()
        for j, chip in enumerate(chips):
            copy(4 + j, (*chip, 1 - c), me).wait_recv()
        for cp in first + passed:
            cp.wait_send()
        mine.wait()

    return pl.pallas_call(
        body, name=name, out_shape=jax.ShapeDtypeStruct((8 * m, n), xs.dtype),
        in_specs=[pl.BlockSpec(memory_space=pl.ANY)], out_specs=pl.BlockSpec(memory_space=pl.ANY),
        scratch_shapes=[pltpu.SemaphoreType.DMA((7,)), pltpu.SemaphoreType.DMA((7,)), pltpu.SemaphoreType.DMA],
    )(xs)


def swap_halves(g, *, name):
    nq, r, cc = g.shape
    r2 = r // 2

    def body(g_ref, out_ref, send_sem, recv_sem):
        x, y, c = _place()
        cp = pltpu.make_async_remote_copy(
            src_ref=g_ref.at[:, pl.ds((1 - c) * r2, r2), :], dst_ref=out_ref,
            send_sem=send_sem, recv_sem=recv_sem, device_id=(x, y, 1 - c), device_id_type=MESH)
        cp.start()
        cp.wait()

    return pl.pallas_call(
        body, name=name, out_shape=jax.ShapeDtypeStruct((nq, r2, cc), g.dtype),
        in_specs=[pl.BlockSpec(memory_space=pl.ANY)], out_specs=pl.BlockSpec(memory_space=pl.ANY),
        scratch_shapes=[pltpu.SemaphoreType.DMA, pltpu.SemaphoreType.DMA],
    )(g)


def exchange_chips(p, *, name):
    nq, r2, cc = p.shape

    def body(p_ref, out_ref, send_sems, recv_sems, local_sem):
        x, y, c = _place()
        q0 = 2 * x + y
        chips = [(1 - x, y), (x, 1 - y), (1 - x, 1 - y)]
        mine = pltpu.make_async_copy(p_ref.at[q0], out_ref.at[q0], local_sem)
        mine.start()
        sends = []
        for j, (cx, cy) in enumerate(chips):
            sends.append(pltpu.make_async_remote_copy(
                src_ref=p_ref.at[2 * cx + cy], dst_ref=out_ref.at[q0],
                send_sem=send_sems.at[j], recv_sem=recv_sems.at[j],
                device_id=(cx, cy, c), device_id_type=MESH))
        for cp in sends:
            cp.start()
        for j, (cx, cy) in enumerate(chips):
            qj = 2 * cx + cy
            pltpu.make_async_remote_copy(
                src_ref=p_ref.at[qj], dst_ref=out_ref.at[qj],
                send_sem=send_sems.at[j], recv_sem=recv_sems.at[j],
                device_id=(cx, cy, c), device_id_type=MESH).wait_recv()
        for cp in sends:
            cp.wait_send()
        mine.wait()

    return pl.pallas_call(
        body, name=name, out_shape=jax.ShapeDtypeStruct((nq, r2, cc), p.dtype),
        in_specs=[pl.BlockSpec(memory_space=pl.ANY)], out_specs=pl.BlockSpec(memory_space=pl.ANY),
        scratch_shapes=[pltpu.SemaphoreType.DMA((3,)), pltpu.SemaphoreType.DMA((3,)), pltpu.SemaphoreType.DMA],
    )(p)


def join_halves(tot, *, name):
    r2, cc = tot.shape

    def body(t_ref, out_ref, send_sem, recv_sem, local_sem):
        x, y, c = _place()
        mine = pltpu.make_async_copy(t_ref, out_ref.at[pl.ds(c * r2, r2), :], local_sem)
        mine.start()
        cp = pltpu.make_async_remote_copy(
            src_ref=t_ref, dst_ref=out_ref.at[pl.ds(c * r2, r2), :],
            send_sem=send_sem, recv_sem=recv_sem, device_id=(x, y, 1 - c), device_id_type=MESH)
        cp.start()
        pltpu.make_async_remote_copy(
            src_ref=t_ref, dst_ref=out_ref.at[pl.ds((1 - c) * r2, r2), :],
            send_sem=send_sem, recv_sem=recv_sem, device_id=(x, y, 1 - c), device_id_type=MESH).wait_recv()
        cp.wait_send()
        mine.wait()

    return pl.pallas_call(
        body, name=name, out_shape=jax.ShapeDtypeStruct((2 * r2, cc), tot.dtype),
        in_specs=[pl.BlockSpec(memory_space=pl.ANY)], out_specs=pl.BlockSpec(memory_space=pl.ANY),
        scratch_shapes=[pltpu.SemaphoreType.DMA, pltpu.SemaphoreType.DMA, pltpu.SemaphoreType.DMA],
    )(tot)


def _pack_local_shard(big, small_vec):
    parts = [w.reshape(-1, PACK_C).astype(BF) for w in big]
    srow = lax.bitcast_convert_type(small_vec, BF).reshape(1, PACK_C)
    parts.append(jnp.pad(srow, ((0, PACK_PAD - 1), (0, 0))))
    return jnp.concatenate(parts, axis=0)


def _split_rows(a, axis):
    out, off = [], 0
    for n in PACK_ROWS:
        out.append(lax.slice_in_dim(a, off, off + n, axis=axis))
        off += n
    return out, off


def _unpack_gathered(gw):
    (p_in, p_grp, p_out, m_in, m_qb, m_kvb, m_out), off = _split_rows(gw, 1)
    w = {}
    w["pool_w_in"] = p_in.reshape(4, D_MODEL, 1024).transpose(1, 0, 2).reshape(D_MODEL, 2 * POOL_WIDTH)
    w["pool_w_group"] = p_grp.reshape(4, 4, 128, POOL_GROUP).transpose(1, 0, 2, 3).reshape(4, POOL_GROUP, POOL_GROUP)
    w["pool_w_out"] = p_out.reshape(POOL_WIDTH, D_MODEL)
    win = m_in.reshape(4, D_MODEL, 688).transpose(1, 0, 2).reshape(D_MODEL, 2752)
    w["mla_w_in"] = jnp.concatenate(
        [win[:, 384:640], win[:, 640:704], jnp.zeros((D_MODEL, 64), BF), win[:, 0:384], win[:, 704:]], axis=1)
    wq = m_qb.reshape(4, Q_LORA, 768).transpose(1, 0, 2).reshape(Q_LORA, N_HEADS, QK_NOPE + QK_ROPE)
    w["mla_w_q_b"] = jnp.pad(wq, ((0, 0), (0, 0), (0, HEAD_PAD - QK_NOPE - QK_ROPE))).reshape(Q_LORA, N_HEADS * HEAD_PAD)
    w["mla_w_kv_b"] = m_kvb.reshape(4, KV_LORA, 1024).transpose(1, 0, 2).reshape(KV_LORA, 4096)
    w["mla_w_out"] = m_out.reshape(MLA_WIDTH, D_MODEL)
    small = lax.bitcast_convert_type(gw[:, off, :].reshape(4, 512, 2), F32)
    w["mla_norm"] = small[:, :256].reshape(1, D_MODEL)
    w["mla_q_norm"] = small[:, 256:352].reshape(1, Q_LORA)
    w["mla_kv_norm"] = small[:, 352:416].reshape(1, KV_LORA)
    return w


def _pack_grads(g):
    parts = [
        g["pool_w_in"].reshape(D_MODEL, 4, 1024).transpose(1, 0, 2),
        g["pool_w_group"].reshape(4, 4, 128, POOL_GROUP).transpose(1, 0, 2, 3).reshape(4, 256, PACK_C),
        g["pool_w_out"].reshape(4, 512, PACK_C),
        g["mla_w_in"].reshape(D_MODEL, 4, 688).transpose(1, 0, 2).reshape(4, 688, PACK_C),
        g["mla_w_q_b"].reshape(Q_LORA, 4, 768).transpose(1, 0, 2).reshape(4, 288, PACK_C),
        g["mla_w_kv_b"].reshape(KV_LORA, 4, 1024).transpose(1, 0, 2),
        g["mla_w_out"].reshape(4, 512, PACK_C),
        jnp.zeros((4, PACK_PAD, PACK_C), F32),
    ]
    return jnp.concatenate(parts, axis=1)


def kernel(x, positions, pool_norm, pool_w_in, pool_w_group, pool_scale, pool_w_out, mla_norm, mla_w_in, mla_q_norm, mla_w_q_b, mla_kv_norm, mla_w_kv_b, mla_w_out, final_norm, loss_target, m_pool_norm, m_pool_w_in, m_pool_w_group, m_pool_scale, m_pool_w_out, m_mla_norm, m_mla_w_in, m_mla_q_norm, m_mla_w_q_b, m_mla_kv_norm, m_mla_w_kv_b, m_mla_w_out, m_final_norm, v_pool_norm, v_pool_w_in, v_pool_w_group, v_pool_scale, v_pool_w_out, v_mla_norm, v_mla_w_in, v_mla_q_norm, v_mla_w_q_b, v_mla_kv_norm, v_mla_w_kv_b, v_mla_w_out, v_final_norm):
    s = x.shape[1]
    tq = min(512, s)
    x0 = x.reshape(s, D_MODEL)
    tgt = loss_target.reshape(s, D_MODEL)
    cx, cy, cc_idx = _place()
    chip = 2 * cx + cy

    big_names = ("pool_w_in", "pool_w_group", "pool_w_out", "mla_w_in", "mla_w_q_b", "mla_w_kv_b", "mla_w_out")
    big_w = dict(zip(big_names, (pool_w_in, pool_w_group, pool_w_out, mla_w_in, mla_w_q_b, mla_w_kv_b, mla_w_out)))
    big_m = dict(zip(big_names, (m_pool_w_in, m_pool_w_group, m_pool_w_out, m_mla_w_in, m_mla_w_q_b, m_mla_w_kv_b, m_mla_w_out)))
    big_v = dict(zip(big_names, (v_pool_w_in, v_pool_w_group, v_pool_w_out, v_mla_w_in, v_mla_w_q_b, v_mla_w_kv_b, v_mla_w_out)))

    small_vec = jnp.concatenate([mla_norm.reshape(-1), mla_q_norm.reshape(-1), mla_kv_norm.reshape(-1),
                                 jnp.zeros((96,), F32)])
    packed = _pack_local_shard([big_w[n] for n in big_names], small_vec)
    gathered = all_gather8(packed, name="gather_weights", own_half=True).reshape(4, PACK_R, PACK_C)
    w = _unpack_gathered(gathered)
    g_pool = pool_norm.reshape(1, D_MODEL)
    g_final = final_norm.reshape(1, D_MODEL)
    sc_pool = pool_scale.reshape(1, POOL_WIDTH)

    inv_freq = 1.0 / (ROPE_THETA ** (jnp.arange(0, QK_ROPE, 2, dtype=F32) / QK_ROPE))
    ang = positions.reshape(s).astype(F32)[:, None] * inv_freq
    cos, sin = jnp.cos(ang), jnp.sin(ang)
    z32, z64, z96 = (jnp.zeros((s, n), F32) for n in (32, 64, 96))
    t_cc = jnp.concatenate([cos, cos, z64], axis=1)
    t_sa = jnp.concatenate([-sin, z96], axis=1)
    t_sb = jnp.concatenate([z32, sin, z64], axis=1)

    h0 = norm_fwd(x0, g_pool, col=0, width=D_MODEL, name="pool_norm_fwd")
    uz = mm_nn(h0, w["pool_w_in"], name="pool_in_proj", out_dtype=F32)
    pd = pool_prep(uz, name="pool_window")
    mm = gmm_nn(pd, w["pool_w_group"], name="pool_group_mix")
    y1 = pool_gate(mm, uz, sc_pool, name="pool_gate")
    x1 = mm_nn(y1, w["pool_w_out"], name="pool_out_proj", out_dtype=F32, add=x0)

    h1 = norm_fwd(x1, w["mla_norm"], col=0, width=D_MODEL, name="mla_norm_fwd")
    proj = mm_nn(h1, w["mla_w_in"], name="mla_in_proj", out_dtype=F32, tn=P_WIDTH // 2)
    qn = norm_fwd(proj, w["mla_q_norm"], col=P_Q, width=Q_LORA, name="mla_q_norm_fwd")
    kvn = norm_fwd(proj, w["mla_kv_norm"], col=P_KV, width=KV_LORA, name="mla_kv_norm_fwd")
    q_pre = mm_nn(qn, w["mla_w_q_b"], name="mla_q_proj", out_dtype=F32, tk=Q_LORA)
    kv = mm_nn(kvn, w["mla_w_kv_b"], name="mla_kv_proj", out_dtype=BF, tk=KV_LORA)
    qr = rope_q(q_pre, t_cc, t_sa, t_sb, name="mla_rope_q")
    kc = pack_k(kv, proj, t_cc, t_sa, t_sb, name="mla_pack_k")
    o, lse = attn_fwd(qr, kc, kv, name="mla_attn_fwd", tq=tq)
    y2 = mla_gate(o, proj, name="mla_gate")
    x2 = mm_nn(y2, w["mla_w_out"], name="mla_out_proj", out_dtype=F32, add=x1)

    dx2, d_final, loss_part = final_loss(x2, g_final, tgt, name="final_norm_loss")

    grads = {}
    dy2 = mm_nt(dx2, w["mla_w_out"], name="mla_out_proj_dx", out_dtype=F32)
    grads["mla_w_out"] = mm_tn(y2, dx2, name="mla_out_proj_dw")
    do, dz2, delta = mla_gate_bwd(dy2, o, proj, name="mla_gate_bwd", tq=tq)
    dkv, dkr, dqr = attn_bwd(qr, kc, kv, do, lse, delta, name="mla_attn_bwd", tq=tq)
    dq_pre = unrope_q(dqr, t_cc, t_sa, t_sb, name="mla_unrope_q")
    dkr_pre = unrope_k(dkr, t_cc, t_sa, t_sb, name="mla_unrope_k")
    dqn = mm_nt(dq_pre, w["mla_w_q_b"], name="mla_q_proj_dx", out_dtype=F32, tn=Q_LORA)
    g_qb = mm_tn(qn, dq_pre, name="mla_q_proj_dw", tm=Q_LORA, tn=2048)
    dkvn = mm_nt(dkv, w["mla_w_kv_b"], name="mla_kv_proj_dx", out_dtype=F32, tn=KV_LORA)
    grads["mla_w_kv_b"] = mm_tn(kvn, dkv, name="mla_kv_proj_dw", tm=KV_LORA, tn=2048)
    dq_lat, d_qnorm = norm_bwd(proj, w["mla_q_norm"], dqn, col=P_Q, width=Q_LORA, name="mla_q_norm_bwd", out_dtype=BF)
    dkv_lat, d_kvnorm = norm_bwd(proj, w["mla_kv_norm"], dkvn, col=P_KV, width=KV_LORA, name="mla_kv_norm_bwd", out_dtype=BF)
    dsmall = jnp.concatenate([dkv_lat, dkr_pre, dq_lat], axis=1)
    dh1 = mm_nt(dsmall, w["mla_w_in"], name="mla_in_proj_dx_a", out_dtype=F32, tk=P_SMALL)
    dh1 = mm_nt(dz2, w["mla_w_in"], name="mla_in_proj_dx_b", out_dtype=F32, b_col=P_Z, add=dh1, tk=256)
    g_in_a = mm_tn(h1, dsmall, name="mla_in_proj_dw_a", tn=P_SMALL)
    g_in_b = mm_tn(h1, dz2, name="mla_in_proj_dw_b", tn=2048)
    dx1, d_mnorm = norm_bwd(x1, w["mla_norm"], dh1, col=0, width=D_MODEL, name="mla_norm_bwd", res=dx2)

    dy1 = mm_nt(dx1, w["pool_w_out"], name="pool_out_proj_dx", out_dtype=F32)
    grads["pool_w_out"] = mm_tn(y1, dx1, name="pool_out_proj_dw")
    dmm, dz1, d_scale = pool_gate_bwd(dy1, mm, uz, sc_pool, name="pool_gate_bwd")
    dpd = gmm_nt(dmm, w["pool_w_group"], name="pool_group_mix_dx")
    grads["pool_w_group"] = gmm_tn(pd, dmm, 4, name="pool_group_mix_dw")
    du = pool_prep_bwd(dpd, name="pool_window_bwd")
    dh0 = mm_nt(du, w["pool_w_in"], name="pool_in_proj_dx_u", out_dtype=F32, tk=1024)
    dh0 = mm_nt(dz1, w["pool_w_in"], name="pool_in_proj_dx_z", out_dtype=F32, b_col=POOL_WIDTH, add=dh0, tk=1024)
    g_pin_u = mm_tn(h0, du, name="pool_in_proj_dw_u", tn=2048)
    g_pin_z = mm_tn(h0, dz1, name="pool_in_proj_dw_z", tn=2048)
    grad_x, d_pnorm = norm_bwd(x0, g_pool, dh0, col=0, width=D_MODEL, name="pool_norm_bwd", res=dx1)

    grads["pool_w_in"] = jnp.concatenate([g_pin_u, g_pin_z], axis=1)
    g_in = jnp.concatenate([g_in_a, g_in_b], axis=1)
    grads["mla_w_in"] = jnp.concatenate([g_in[:, P_Q:P_Z], g_in[:, P_KV:P_KV + KV_LORA],
                                         g_in[:, P_KR:P_KR + QK_ROPE], g_in[:, P_Z:]], axis=1)
    grads["mla_w_q_b"] = g_qb.reshape(Q_LORA, N_HEADS, HEAD_PAD)[:, :, :QK_NOPE + QK_ROPE].reshape(Q_LORA, -1)

    gp = _pack_grads(grads)
    half_rows = PACK_R // 2 // 3
    sib = swap_halves(gp, name="grad_swap_halves")
    pre = add_halves(gp, sib, cc_idx.reshape(1).astype(jnp.int32), name="grad_add_halves", rows=half_rows)
    got = exchange_chips(pre, name="grad_exchange_chips")
    tot = sum_chips(got, name="grad_sum_chips", rows=half_rows)
    red = join_halves(tot, name="grad_join_halves")
    red_parts, _ = _split_rows(red, 0)

    sv = jnp.concatenate([d_pnorm.reshape(-1), d_scale.reshape(-1), d_final.reshape(-1), d_mnorm.reshape(-1),
                          d_qnorm.reshape(-1), d_kvnorm.reshape(-1), loss_part[0, :1],
                          jnp.zeros((SV_ROWS * SV_COLS - SV_OFF["loss"] - 1,), F32)]).reshape(SV_ROWS, SV_COLS)
    sv_all = all_gather8(sv, name="gather_small_grads", own_half=False)
    sv_sum = sum_devices(sv_all, name="sum_small_grads").reshape(-1)
    loss = sv_sum[SV_OFF["loss"]]

    def sv_take(key, n):
        return lax.slice_in_dim(sv_sum, SV_OFF[key], SV_OFF[key] + n)

    out_g, out_d, out_m, out_v = {}, {}, {}, {}
    for name, part in zip(big_names, red_parts):
        shp = big_w[name].shape
        g2 = part.reshape(shp)
        two_d = (-1, shp[-1])
        d_, m_, v_ = adamw(big_w[name].reshape(two_d), g2.reshape(two_d), big_m[name].reshape(two_d),
                           big_v[name].reshape(two_d), name="adamw_" + name)
        out_g[name], out_d[name], out_m[name], out_v[name] = g2, d_.reshape(shp), m_.reshape(shp), v_.reshape(shp)

    small = [
        ("pool_norm", pool_norm, m_pool_norm, v_pool_norm, sv_take("pool_norm", 1024)),
        ("pool_scale", pool_scale, m_pool_scale, v_pool_scale, sv_take("pool_scale", 2048)),
        ("final_norm", final_norm, m_final_norm, v_final_norm, sv_take("final_norm", 1024)),
        ("mla_norm", mla_norm, m_mla_norm, v_mla_norm,
         lax.dynamic_slice_in_dim(sv_take("mla_norm", 1024), chip * 256, 256)),
        ("mla_q_norm", mla_q_norm, m_mla_q_norm, v_mla_q_norm,
         lax.dynamic_slice_in_dim(sv_take("q_norm", 384), chip * 96, 96)),
        ("mla_kv_norm", mla_kv_norm, m_mla_kv_norm, v_mla_kv_norm,
         lax.dynamic_slice_in_dim(sv_take("kv_norm", 256), chip * 64, 64)),
    ]
    sw = jnp.concatenate([t[1].reshape(-1) for t in small] + [jnp.zeros((96,), F32)]).reshape(1, -1)
    sm = jnp.concatenate([t[2].reshape(-1) for t in small] + [jnp.zeros((96,), F32)]).reshape(1, -1)
    s_v = jnp.concatenate([t[3].reshape(-1) for t in small] + [jnp.ones((96,), F32)]).reshape(1, -1)
    sg = jnp.concatenate([t[4].reshape(-1) for t in small] + [jnp.zeros((96,), F32)]).reshape(1, -1)
    sd_, sm_, sv_ = adamw(sw, sg, sm, s_v, name="adamw_vectors")
    off = 0
    for name, wt, _, _, gvec in small:
        n = gvec.shape[0]
        shp = wt.shape
        out_g[name] = gvec.reshape(shp)
        out_d[name] = sd_[0, off:off + n].reshape(shp)
        out_m[name] = sm_[0, off:off + n].reshape(shp)
        out_v[name] = sv_[0, off:off + n].reshape(shp)
        off += n

    order = ("pool_norm", "pool_w_in", "pool_w_group", "pool_scale", "pool_w_out", "mla_norm", "mla_w_in",
             "mla_q_norm", "mla_w_q_b", "mla_kv_norm", "mla_w_kv_b", "mla_w_out", "final_norm")
    return (loss, grad_x.reshape(x.shape), *[out_g[n] for n in order], *[out_d[n] for n in order],
            *[out_m[n] for n in order], *[out_v[n] for n in order])
```

```python
import functools

import jax
import jax.numpy as jnp
from jax import lax
from jax.experimental import pallas as pl
from jax.experimental.pallas import tpu as pltpu

F32 = jnp.float32
BF = jnp.bfloat16
MESH = pl.DeviceIdType.MESH

D_MODEL = 1024
POOL_WIDTH = 2048
POOL_WINDOWS = (2, 4, 8, 16)
POOL_GROUP = 512
HALO = 16
N_HEADS = 16
QK_NOPE = 128
QK_ROPE = 64
V_DIM = 128
HEAD_PAD = 256
Q_LORA = 384
KV_LORA = 256
MLA_WIDTH = 2048
ROPE_THETA = 10000.0
EPS = 1e-6
SCALE = (QK_NOPE + QK_ROPE) ** -0.5
SCALE_LOG2E = SCALE * 1.4426950408889634
NEG = -1e30

P_KV, P_KR, P_Q, P_Z = 0, 256, 384, 768
P_SMALL = 768
P_WIDTH = 2816

ADAM_LR = 0.001
ADAM_B1 = 0.9
ADAM_B2 = 0.999
ADAM_EPS = 1e-08
ADAM_WD = 0.01
ADAM_STEP = 10

NN = (((1,), (0,)), ((), ()))
NT = (((1,), (1,)), ((), ()))
TN = (((0,), (0,)), ((), ()))

PACK_ROWS = (1024, 256, 512, 688, 288, 256, 512)
PACK_PAD = 16
PACK_R = sum(PACK_ROWS) + PACK_PAD
PACK_C = 1024
SV_OFF = dict(pool_norm=0, pool_scale=1024, final_norm=3072, mla_norm=4096, q_norm=5120, kv_norm=5504, loss=5760)
SV_ROWS, SV_COLS = 8, 768

VMEM_LIMIT = 56 * 1024 * 1024


def _params(n_axes, vmem=None):
    return pltpu.CompilerParams(dimension_semantics=("arbitrary",) * n_axes,
                                vmem_limit_bytes=VMEM_LIMIT if vmem is None else vmem)


def _sigmoid(z):
    return 1.0 / (1.0 + jnp.exp(-z))


def _mm(a, b, *, dims, grid, a_spec, b_spec, o_spec, out_shape, out_dtype, acc_shape, name,
        add=None, add_spec=None):
    nk = grid[-1]
    kax = len(grid) - 1

    def body(*refs):
        if add is None:
            a_ref, b_ref, o_ref = refs[:3]
            add_ref = None
            rest = refs[3:]
        else:
            a_ref, b_ref, add_ref, o_ref = refs[:4]
            rest = refs[4:]
        part = lax.dot_general(a_ref[...].astype(BF), b_ref[...].astype(BF), dims,
                               preferred_element_type=F32)

        def finish(r):
            if add_ref is not None:
                r = r + add_ref[...]
            o_ref[...] = r.astype(o_ref.dtype)

        if nk == 1:
            finish(part)
        else:
            acc = rest[0]
            k = pl.program_id(kax)

            @pl.when(k == 0)
            def _():
                acc[...] = part

            @pl.when(k > 0)
            def _():
                acc[...] += part

            @pl.when(k == nk - 1)
            def _():
                finish(acc[...])

    in_specs = [a_spec, b_spec]
    args = [a, b]
    if add is not None:
        in_specs.append(add_spec)
        args.append(add)
    scratch = [] if nk == 1 else [pltpu.VMEM(acc_shape, F32)]
    return pl.pallas_call(
        body, name=name, grid=grid, in_specs=in_specs, out_specs=o_spec,
        out_shape=jax.ShapeDtypeStruct(out_shape, out_dtype), scratch_shapes=scratch,
        compiler_params=_params(len(grid)))(*args)


def _pick(n, t):
    t = min(n, t)
    assert n % t == 0, (n, t)
    return t


def mm_nn(a, b, *, name, out_dtype, a_col=0, k_size=None, add=None, tm=1024, tn=1024, tk=1024):
    m = a.shape[0]
    kk, n = b.shape
    assert k_size is None or k_size == kk
    tm, tn, tk = _pick(m, tm), _pick(n, tn), _pick(kk, tk)
    assert a_col % tk == 0
    ko = a_col // tk
    return _mm(a, b, dims=NN, grid=(m // tm, n // tn, kk // tk),
               a_spec=pl.BlockSpec((tm, tk), lambda i, j, k: (i, ko + k)),
               b_spec=pl.BlockSpec((tk, tn), lambda i, j, k: (k, j)),
               o_spec=pl.BlockSpec((tm, tn), lambda i, j, k: (i, j)),
               add=add, add_spec=pl.BlockSpec((tm, tn), lambda i, j, k: (i, j)),
               out_shape=(m, n), out_dtype=out_dtype, acc_shape=(tm, tn), name=name)


def mm_nt(a, b, *, name, out_dtype, b_col=0, add=None, tm=1024, tn=1024, tk=1024):
    m, kk = a.shape
    n = b.shape[0]
    tm, tn, tk = _pick(m, tm), _pick(n, tn), _pick(kk, tk)
    assert b_col % tk == 0
    ko = b_col // tk
    return _mm(a, b, dims=NT, grid=(m // tm, n // tn, kk // tk),
               a_spec=pl.BlockSpec((tm, tk), lambda i, j, k: (i, k)),
               b_spec=pl.BlockSpec((tn, tk), lambda i, j, k: (j, ko + k)),
               o_spec=pl.BlockSpec((tm, tn), lambda i, j, k: (i, j)),
               add=add, add_spec=pl.BlockSpec((tm, tn), lambda i, j, k: (i, j)),
               out_shape=(m, n), out_dtype=out_dtype, acc_shape=(tm, tn), name=name)


def mm_tn(a, b, *, name, a_col=0, m_size=None, b_col=0, n_size=None, tm=1024, tn=1024, tk=1024):
    s = a.shape[0]
    m = a.shape[1] if m_size is None else m_size
    n = b.shape[1] if n_size is None else n_size
    tm, tn, tk = _pick(m, tm), _pick(n, tn), _pick(s, tk)
    assert a_col % tm == 0 and b_col % tn == 0
    ao, bo = a_col // tm, b_col // tn
    return _mm(a, b, dims=TN, grid=(m // tm, n // tn, s // tk),
               a_spec=pl.BlockSpec((tk, tm), lambda i, j, k: (k, ao + i)),
               b_spec=pl.BlockSpec((tk, tn), lambda i, j, k: (k, bo + j)),
               o_spec=pl.BlockSpec((tm, tn), lambda i, j, k: (i, j)),
               out_shape=(m, n), out_dtype=F32, acc_shape=(tm, tn), name=name)


def gmm_nn(a, w, *, name, tm=1024):
    s = a.shape[0]
    g, kk, n = w.shape
    tm = _pick(s, tm)
    return _mm(a, w, dims=NN, grid=(s // tm, g, 1),
               a_spec=pl.BlockSpec((tm, kk), lambda i, gi, k: (i, gi)),
               b_spec=pl.BlockSpec((None, kk, n), lambda i, gi, k: (gi, 0, 0)),
               o_spec=pl.BlockSpec((tm, n), lambda i, gi, k: (i, gi)),
               out_shape=(s, g * n), out_dtype=F32, acc_shape=(tm, n), name=name)


def gmm_nt(a, w, *, name, tm=1024):
    s = a.shape[0]
    g, kk, n = w.shape
    tm = _pick(s, tm)
    return _mm(a, w, dims=NT, grid=(s // tm, g, 1),
               a_spec=pl.BlockSpec((tm, n), lambda i, gi, k: (i, gi)),
               b_spec=pl.BlockSpec((None, kk, n), lambda i, gi, k: (gi, 0, 0)),
               o_spec=pl.BlockSpec((tm, kk), lambda i, gi, k: (i, gi)),
               out_shape=(s, g * kk), out_dtype=F32, acc_shape=(tm, kk), name=name)


def gmm_tn(a, b, g, *, name, tk=1024):
    s = a.shape[0]
    kk, n = a.shape[1] // g, b.shape[1] // g
    tk = _pick(s, tk)
    return _mm(a, b, dims=TN, grid=(g, s // tk),
               a_spec=pl.BlockSpec((tk, kk), lambda gi, k: (k, gi)),
               b_spec=pl.BlockSpec((tk, n), lambda gi, k: (k, gi)),
               o_spec=pl.BlockSpec((None, kk, n), lambda gi, k: (gi, 0, 0)),
               out_shape=(g, kk, n), out_dtype=F32, acc_shape=(kk, n), name=name)


def norm_fwd(x, g, *, col, width, name, t=512):
    s = x.shape[0]
    t = _pick(s, t)
    cb = col // width
    assert col % width == 0

    def body(x_ref, g_ref, o_ref):
        xv = x_ref[...]
        inv = lax.rsqrt(jnp.mean(xv * xv, axis=-1, keepdims=True) + EPS)
        o_ref[...] = ((xv * inv) * g_ref[...]).astype(o_ref.dtype)

    return pl.pallas_call(
        body, name=name, grid=(s // t,),
        in_specs=[pl.BlockSpec((t, width), lambda i: (i, cb)), pl.BlockSpec((1, width), lambda i: (0, 0))],
        out_specs=pl.BlockSpec((t, width), lambda i: (i, 0)),
        out_shape=jax.ShapeDtypeStruct((s, width), BF), compiler_params=_params(1))(x, g)


def norm_bwd(x, g, dh, *, col, width, name, res=None, out_dtype=F32, t=512):
    s = x.shape[0]
    t = _pick(s, t)
    cb = col // width
    assert col % width == 0

    def body(*refs):
        if res is None:
            x_ref, g_ref, dh_ref, dx_ref, dg_ref = refs
        else:
            x_ref, g_ref, dh_ref, res_ref, dx_ref, dg_ref = refs
        xv = x_ref[...]
        inv = lax.rsqrt(jnp.mean(xv * xv, axis=-1, keepdims=True) + EPS)
        xhat = xv * inv
        dh_v = dh_ref[...]
        part = jnp.sum(dh_v * xhat, axis=0, keepdims=True)

        @pl.when(pl.program_id(0) == 0)
        def _():
            dg_ref[...] = part

        @pl.when(pl.program_id(0) > 0)
        def _():
            dg_ref[...] += part

        dxhat = dh_v * g_ref[...]
        dx = inv * (dxhat - xhat * jnp.mean(dxhat * xhat, axis=-1, keepdims=True))
        if res is not None:
            dx = dx + res_ref[...]
        dx_ref[...] = dx.astype(dx_ref.dtype)

    row = pl.BlockSpec((t, width), lambda i: (i, 0))
    vec = pl.BlockSpec((1, width), lambda i: (0, 0))
    in_specs = [pl.BlockSpec((t, width), lambda i: (i, cb)), vec, row]
    args = [x, g, dh]
    if res is not None:
        in_specs.append(row)
        args.append(res)
    return pl.pallas_call(
        body, name=name, grid=(s // t,), in_specs=in_specs, out_specs=[row, vec],
        out_shape=[jax.ShapeDtypeStruct((s, width), out_dtype), jax.ShapeDtypeStruct((1, width), F32)],
        compiler_params=_params(1))(*args)


def final_loss(x2, gf, tgt, *, name, t=512):
    s, d = x2.shape
    t = _pick(s, t)

    def body(x_ref, g_ref, t_ref, dx_ref, dg_ref, loss_ref):
        xv = x_ref[...]
        inv = lax.rsqrt(jnp.mean(xv * xv, axis=-1, keepdims=True) + EPS)
        xhat = xv * inv
        gv = g_ref[...]
        diff = xhat * gv - t_ref[...]
        row_err = jnp.mean(diff * diff, axis=-1, keepdims=True)
        lpart = jnp.broadcast_to(0.5 * jnp.sum(row_err, axis=0, keepdims=True), (1, 128))
        dout = diff * (1.0 / d)
        gpart = jnp.sum(dout * xhat, axis=0, keepdims=True)

        @pl.when(pl.program_id(0) == 0)
        def _():
            dg_ref[...] = gpart
            loss_ref[...] = lpart

        @pl.when(pl.program_id(0) > 0)
        def _():
            dg_ref[...] += gpart
            loss_ref[...] += lpart

        dxhat = dout * gv
        dx_ref[...] = inv * (dxhat - xhat * jnp.mean(dxhat * xhat, axis=-1, keepdims=True))

    row = pl.BlockSpec((t, d), lambda i: (i, 0))
    vec = pl.BlockSpec((1, d), lambda i: (0, 0))
    return pl.pallas_call(
        body, name=name, grid=(s // t,), in_specs=[row, vec, row],
        out_specs=[row, vec, pl.BlockSpec((1, 128), lambda i: (0, 0))],
        out_shape=[jax.ShapeDtypeStruct((s, d), F32), jax.ShapeDtypeStruct((1, d), F32),
                   jax.ShapeDtypeStruct((1, 128), F32)],
        compiler_params=_params(1))(x2, gf, tgt)


def pool_prep(uz, *, name, t=256):
    s = uz.shape[0]
    t = _pick(s, t)
    hb = t // HALO

    def body(u_ref, halo_ref, o_ref, buf):
        i = pl.program_id(0)
        buf[pl.ds(HALO, t), :] = u_ref[...]

        @pl.when(i == 0)
        def _():
            buf[pl.ds(0, HALO), :] = jnp.zeros((HALO, POOL_WIDTH), F32)

        @pl.when(i > 0)
        def _():
            buf[pl.ds(0, HALO), :] = halo_ref[...]

        pos = i * t + lax.broadcasted_iota(jnp.int32, (t, POOL_GROUP), 0)
        for g, w in enumerate(POOL_WINDOWS):
            cols = pl.ds(g * POOL_GROUP, POOL_GROUP)
            cur = buf[pl.ds(HALO, t), cols]
            acc = cur
            for k in range(1, w):
                acc = acc + buf[pl.ds(HALO - k, t), cols]
            cnt = jnp.minimum(pos + 1, w).astype(F32)
            o_ref[:, cols] = (acc / cnt - cur).astype(o_ref.dtype)

    return pl.pallas_call(
        body, name=name, grid=(s // t,),
        in_specs=[pl.BlockSpec((t, POOL_WIDTH), lambda i: (i, 0)),
                  pl.BlockSpec((HALO, POOL_WIDTH), lambda i: (jnp.maximum(i * hb - 1, 0), 0))],
        out_specs=pl.BlockSpec((t, POOL_WIDTH), lambda i: (i, 0)),
        out_shape=jax.ShapeDtypeStruct((s, POOL_WIDTH), BF),
        scratch_shapes=[pltpu.VMEM((t + HALO, POOL_WIDTH), F32)],
        compiler_params=_params(1))(uz, uz)


def pool_prep_bwd(dpd, *, name, t=256):
    s = dpd.shape[0]
    t = _pick(s, t)
    hb = t // HALO
    n = s // t

    def body(d_ref, halo_ref, o_ref, buf):
        i = pl.program_id(0)
        pos = i * t + lax.broadcasted_iota(jnp.int32, (t, POOL_GROUP), 0)
        for g, w in enumerate(POOL_WINDOWS):
            cols = pl.ds(g * POOL_GROUP, POOL_GROUP)
            cnt = jnp.minimum(pos + 1, w).astype(F32)
            buf[pl.ds(0, t), cols] = d_ref[:, cols] / cnt

            @pl.when(i < n - 1)
            def _():
                buf[pl.ds(t, HALO), cols] = halo_ref[:, cols] / float(w)

            @pl.when(i == n - 1)
            def _():
                buf[pl.ds(t, HALO), cols] = jnp.zeros((HALO, POOL_GROUP), F32)

        for g, w in enumerate(POOL_WINDOWS):
            cols = pl.ds(g * POOL_GROUP, POOL_GROUP)
            acc = buf[pl.ds(0, t), cols]
            for k in range(1, w):
                acc = acc + buf[pl.ds(k, t), cols]
            o_ref[:, cols] = (acc - d_ref[:, cols]).astype(o_ref.dtype)

    return pl.pallas_call(
        body, name=name, grid=(n,),
        in_specs=[pl.BlockSpec((t, POOL_WIDTH), lambda i: (i, 0)),
                  pl.BlockSpec((HALO, POOL_WIDTH), lambda i: (jnp.minimum((i + 1) * hb, n * hb - 1), 0))],
        out_specs=pl.BlockSpec((t, POOL_WIDTH), lambda i: (i, 0)),
        out_shape=jax.ShapeDtypeStruct((s, POOL_WIDTH), BF),
        scratch_shapes=[pltpu.VMEM((t + HALO, POOL_WIDTH), F32)],
        compiler_params=_params(1))(dpd, dpd)


CHUNK = 512


def _chunks(width, step=CHUNK):
    return [slice(c, c + step) for c in range(0, width, step)]


def pool_gate(mm, uz, scale, *, name, t=256):
    s = mm.shape[0]
    t = _pick(s, t)

    def body(mm_ref, uz_ref, sc_ref, y_ref):
        for c in _chunks(POOL_WIDTH):
            z = uz_ref[:, slice(POOL_WIDTH + c.start, POOL_WIDTH + c.stop)]
            y_ref[:, c] = ((mm_ref[:, c] * sc_ref[:, c]) * (z * _sigmoid(z))).astype(y_ref.dtype)

    blk = pl.BlockSpec((t, POOL_WIDTH), lambda i: (i, 0))
    return pl.pallas_call(
        body, name=name, grid=(s // t,),
        in_specs=[blk, pl.BlockSpec((t, 2 * POOL_WIDTH), lambda i: (i, 0)), pl.BlockSpec((1, POOL_WIDTH), lambda i: (0, 0))],
        out_specs=blk, out_shape=jax.ShapeDtypeStruct((s, POOL_WIDTH), BF),
        compiler_params=_params(1))(mm, uz, scale)


def pool_gate_bwd(dy, mm, uz, scale, *, name, t=256):
    s = mm.shape[0]
    t = _pick(s, t)

    def body(dy_ref, mm_ref, uz_ref, sc_ref, dmm_ref, dz_ref, dsc_ref):
        parts = []
        for c in _chunks(POOL_WIDTH):
            z = uz_ref[:, slice(POOL_WIDTH + c.start, POOL_WIDTH + c.stop)]
            sig = _sigmoid(z)
            dyv = dy_ref[:, c]
            mmv = mm_ref[:, c]
            scv = sc_ref[:, c]
            dmixed = dyv * (z * sig)
            dmm_ref[:, c] = (dmixed * scv).astype(dmm_ref.dtype)
            dz_ref[:, c] = (dyv * (mmv * scv) * (sig * (1.0 + z * (1.0 - sig)))).astype(dz_ref.dtype)
            parts.append(jnp.sum(dmixed * mmv, axis=0, keepdims=True))

        @pl.when(pl.program_id(0) == 0)
        def _():
            for c, part in zip(_chunks(POOL_WIDTH), parts):
                dsc_ref[:, c] = part

        @pl.when(pl.program_id(0) > 0)
        def _():
            for c, part in zip(_chunks(POOL_WIDTH), parts):
                dsc_ref[:, c] += part

    blk = pl.BlockSpec((t, POOL_WIDTH), lambda i: (i, 0))
    vec = pl.BlockSpec((1, POOL_WIDTH), lambda i: (0, 0))
    return pl.pallas_call(
        body, name=name, grid=(s // t,),
        in_specs=[blk, blk, pl.BlockSpec((t, 2 * POOL_WIDTH), lambda i: (i, 0)), vec],
        out_specs=[blk, blk, vec],
        out_shape=[jax.ShapeDtypeStruct((s, POOL_WIDTH), BF), jax.ShapeDtypeStruct((s, POOL_WIDTH), BF),
                   jax.ShapeDtypeStruct((1, POOL_WIDTH), F32)],
        compiler_params=_params(1))(dy, mm, uz, scale)


def _rope(a, cc, sa, sb):
    return a * cc + pltpu.roll(a, 96, 1) * sa + pltpu.roll(a, 32, 1) * sb


def _unrope(d, cc, sa, sb):
    return d * cc + pltpu.roll(d * sa, 32, 1) + pltpu.roll(d * sb, 96, 1)


def rope_q(q_pre, cc, sa, sb, *, name, t=256):
    s = q_pre.shape[0]
    t = _pick(s, t)

    def body(q_ref, cc_ref, sa_ref, sb_ref, o_ref):
        for h in range(N_HEADS):
            nope = slice(h * HEAD_PAD, h * HEAD_PAD + QK_NOPE)
            rope = slice(h * HEAD_PAD + QK_NOPE, (h + 1) * HEAD_PAD)
            o_ref[:, nope] = q_ref[:, nope].astype(o_ref.dtype)
            o_ref[:, rope] = _rope(q_ref[:, rope], cc_ref[...], sa_ref[...], sb_ref[...]).astype(o_ref.dtype)

    tab = pl.BlockSpec((t, 128), lambda i: (i, 0))
    blk = pl.BlockSpec((t, N_HEADS * HEAD_PAD), lambda i: (i, 0))
    return pl.pallas_call(
        body, name=name, grid=(s // t,), in_specs=[blk, tab, tab, tab], out_specs=blk,
        out_shape=jax.ShapeDtypeStruct((s, N_HEADS * HEAD_PAD), BF), compiler_params=_params(1))(q_pre, cc, sa, sb)


def pack_k(kv, proj, cc, sa, sb, *, name, t=256):
    s = kv.shape[0]
    t = _pick(s, t)
    kr_blk = P_KR // 128

    def body(kv_ref, kr_ref, cc_ref, sa_ref, sb_ref, o_ref):
        kr = _rope(kr_ref[...], cc_ref[...], sa_ref[...], sb_ref[...]).astype(o_ref.dtype)
        for h in range(N_HEADS):
            nope = slice(h * HEAD_PAD, h * HEAD_PAD + QK_NOPE)
            o_ref[:, nope] = kv_ref[:, nope]
            o_ref[:, slice(h * HEAD_PAD + QK_NOPE, (h + 1) * HEAD_PAD)] = kr

    tab = pl.BlockSpec((t, 128), lambda i: (i, 0))
    blk = pl.BlockSpec((t, N_HEADS * HEAD_PAD), lambda i: (i, 0))
    return pl.pallas_call(
        body, name=name, grid=(s // t,),
        in_specs=[blk, pl.BlockSpec((t, 128), lambda i: (i, kr_blk)), tab, tab, tab], out_specs=blk,
        out_shape=jax.ShapeDtypeStruct((s, N_HEADS * HEAD_PAD), BF), compiler_params=_params(1))(kv, proj, cc, sa, sb)


def unrope_q(dqr, cc, sa, sb, *, name, t=256):
    s = dqr.shape[0]
    t = _pick(s, t)

    def body(d_ref, cc_ref, sa_ref, sb_ref, o_ref):
        for h in range(N_HEADS):
            nope = slice(h * HEAD_PAD, h * HEAD_PAD + QK_NOPE)
            rope = slice(h * HEAD_PAD + QK_NOPE, (h + 1) * HEAD_PAD)
            o_ref[:, nope] = (d_ref[:, nope] * SCALE).astype(o_ref.dtype)
            o_ref[:, rope] = _unrope(d_ref[:, rope] * SCALE, cc_ref[...], sa_ref[...], sb_ref[...]).astype(o_ref.dtype)

    tab = pl.BlockSpec((t, 128), lambda i: (i, 0))
    blk = pl.BlockSpec((t, N_HEADS * HEAD_PAD), lambda i: (i, 0))
    return pl.pallas_call(
        body, name=name, grid=(s // t,), in_specs=[blk, tab, tab, tab], out_specs=blk,
        out_shape=jax.ShapeDtypeStruct((s, N_HEADS * HEAD_PAD), BF), compiler_params=_params(1))(dqr, cc, sa, sb)


def unrope_k(dkr, cc, sa, sb, *, name, t=512):
    s = dkr.shape[0]
    t = _pick(s, t)

    def body(d_ref, cc_ref, sa_ref, sb_ref, o_ref):
        acc = d_ref[:, pl.ds(0, 128)]
        for h in range(1, N_HEADS):
            acc = acc + d_ref[:, pl.ds(h * 128, 128)]
        o_ref[...] = _unrope(acc, cc_ref[...], sa_ref[...], sb_ref[...]).astype(o_ref.dtype)

    tab = pl.BlockSpec((t, 128), lambda i: (i, 0))
    return pl.pallas_call(
        body, name=name, grid=(s // t,),
        in_specs=[pl.BlockSpec((t, N_HEADS * 128), lambda i: (i, 0)), tab, tab, tab], out_specs=tab,
        out_shape=jax.ShapeDtypeStruct((s, 128), BF), compiler_params=_params(1))(dkr, cc, sa, sb)


def mla_gate(o, proj, *, name, t=256):
    s = o.shape[0]
    t = _pick(s, t)

    def body(o_ref, p_ref, y_ref):
        for c in _chunks(MLA_WIDTH):
            z = p_ref[:, slice(P_Z + c.start, P_Z + c.stop)]
            y_ref[:, c] = (o_ref[:, c] * (z * _sigmoid(z))).astype(y_ref.dtype)

    blk = pl.BlockSpec((t, MLA_WIDTH), lambda i: (i, 0))
    return pl.pallas_call(
        body, name=name, grid=(s // t,),
        in_specs=[blk, pl.BlockSpec((t, P_WIDTH), lambda i: (i, 0))], out_specs=blk,
        out_shape=jax.ShapeDtypeStruct((s, MLA_WIDTH), BF), compiler_params=_params(1))(o, proj)


def mla_gate_bwd(dy, o, proj, *, name, tq):
    s = o.shape[0]
    nq = s // tq

    def body(dy_ref, o_ref, p_ref, do_ref, dz_ref, dl_ref):
        for h in range(N_HEADS):
            c = slice(h * V_DIM, (h + 1) * V_DIM)
            z = p_ref[:, slice(P_Z + c.start, P_Z + c.stop)]
            sig = _sigmoid(z)
            dyv = dy_ref[:, c]
            ov = o_ref[:, c]
            dov = dyv * (z * sig)
            do_ref[:, c] = dov.astype(do_ref.dtype)
            dz_ref[:, c] = (dyv * ov * (sig * (1.0 + z * (1.0 - sig)))).astype(dz_ref.dtype)
            delta = jnp.sum(dov * ov, axis=-1, keepdims=True)
            dl_ref[h] = jnp.broadcast_to(delta, (tq, 128)).T[:8, :]

    blk = pl.BlockSpec((tq, MLA_WIDTH), lambda i: (i, 0))
    return pl.pallas_call(
        body, name=name, grid=(nq,),
        in_specs=[blk, blk, pl.BlockSpec((tq, P_WIDTH), lambda i: (i, 0))],
        out_specs=[blk, blk, pl.BlockSpec((N_HEADS, None, 8, tq), lambda i: (0, i, 0, 0))],
        out_shape=[jax.ShapeDtypeStruct((s, MLA_WIDTH), BF), jax.ShapeDtypeStruct((s, MLA_WIDTH), BF),
                   jax.ShapeDtypeStruct((N_HEADS, nq, 8, tq), F32)],
        compiler_params=_params(1))(dy, o, proj)


def attn_fwd(qr, kc, kv, *, name, tq):
    s = qr.shape[0]
    nq = s // tq

    def body(q_ref, k_ref, v_ref, o_ref, lse_ref, acc_sc, vx_sc):
        i = pl.program_id(1)

        @pl.when(i == 0)
        def _():
            vx_sc[:, :V_DIM] = v_ref[...]
            vx_sc[:, V_DIM:] = jnp.ones((s, V_DIM), BF)

        q = q_ref[...]
        acc_sc[...] = jnp.zeros((tq, 2 * V_DIM), F32)

        def scores(j):
            k = k_ref[pl.ds(pl.multiple_of(j * tq, tq), tq), :]
            return lax.dot_general(k, q, NT, preferred_element_type=F32)

        def tile(j, st, m_prev, masked):
            st = st * SCALE_LOG2E
            if masked:
                krow = lax.broadcasted_iota(jnp.int32, (tq, tq), 0)
                qcol = lax.broadcasted_iota(jnp.int32, (tq, tq), 1)
                st = jnp.where(qcol >= krow, st, NEG)
            m_new = jnp.maximum(m_prev, jnp.max(st, axis=0, keepdims=True))
            alpha_c = jnp.broadcast_to(jnp.exp2(m_prev - m_new), (128, tq)).T
            pt = jnp.exp2(st - m_new).astype(BF)
            vx = vx_sc[pl.ds(pl.multiple_of(j * tq, tq), tq), :]
            pv = lax.dot_general(pt, vx, TN, preferred_element_type=F32)
            for cols in (slice(0, V_DIM), slice(V_DIM, 2 * V_DIM)):
                acc_sc[:, cols] = alpha_c * acc_sc[:, cols] + pv[:, cols]
            return m_new

        def pair(j0, m, mask_second):
            st_a, st_b = scores(j0), scores(j0 + 1)
            return tile(j0 + 1, st_b, tile(j0, st_a, m, False), mask_second)

        m = lax.fori_loop(0, i // 2, lambda jj, m: pair(2 * jj, m, False), jnp.full((1, tq), NEG, F32))
        m = lax.cond(i % 2 == 1, lambda m: pair(i - 1, m, True), lambda m: tile(i, scores(i), m, True), m)
        l = acc_sc[:, V_DIM:]
        o_ref[...] = acc_sc[:, :V_DIM] / l
        lse_ref[...] = jnp.broadcast_to(m, (8, tq)) + jnp.log2(l).T[:8, :]

    return pl.pallas_call(
        body, name=name, grid=(N_HEADS, nq),
        in_specs=[pl.BlockSpec((tq, HEAD_PAD), lambda h, i: (i, h)),
                  pl.BlockSpec((s, HEAD_PAD), lambda h, i: (0, h)),
                  pl.BlockSpec((s, V_DIM), lambda h, i: (0, 2 * h + 1))],
        out_specs=[pl.BlockSpec((tq, V_DIM), lambda h, i: (i, h)),
                   pl.BlockSpec((None, None, 8, tq), lambda h, i: (h, i, 0, 0))],
        out_shape=[jax.ShapeDtypeStruct((s, N_HEADS * V_DIM), F32),
                   jax.ShapeDtypeStruct((N_HEADS, nq, 8, tq), F32)],
        scratch_shapes=[pltpu.VMEM((tq, 2 * V_DIM), F32), pltpu.VMEM((s, 2 * V_DIM), BF)],
        compiler_params=_params(2))(qr, kc, kv)


def attn_bwd(qr, kc, kv, do, lse, delta, *, name, tq):
    s = qr.shape[0]
    nq = s // tq

    def body(k_ref, v_ref, q_ref, do_ref, lse_ref, dl_ref, dkv_ref, dkr_ref, dq_ref, dk_sc, dv_sc):
        j = pl.program_id(1)

        @pl.when(j == 0)
        def _():
            dq_ref[...] = jnp.zeros((s, HEAD_PAD), F32)

        dk_sc[...] = jnp.zeros((tq, HEAD_PAD), F32)
        dv_sc[...] = jnp.zeros((tq, V_DIM), F32)
        k = k_ref[...]
        v = v_ref[...]

        def step(i, n_tiles, masked):
            r0 = pl.multiple_of(i * tq, tq)
            rows = pl.ds(r0, n_tiles * tq)
            q = q_ref[rows, :]
            dov = do_ref[rows, :]
            lse_row = jnp.concatenate([lse_ref[i + n, pl.ds(0, 1), :] for n in range(n_tiles)], axis=1)
            dl_row = jnp.concatenate([dl_ref[i + n, pl.ds(0, 1), :] for n in range(n_tiles)], axis=1)
            st = lax.dot_general(k, q, NT, preferred_element_type=F32) * SCALE_LOG2E
            if masked:
                krow = lax.broadcasted_iota(jnp.int32, (tq, tq), 0)
                qcol = lax.broadcasted_iota(jnp.int32, (tq, tq), 1)
                st = jnp.where(qcol >= krow, st, NEG)
            pt = jnp.exp2(st - lse_row)
            dpt = lax.dot_general(v, dov, NT, preferred_element_type=F32)
            dst = (pt * (dpt - dl_row)).astype(BF)
            dv_sc[...] += jnp.dot(pt.astype(BF), dov, preferred_element_type=F32)
            dk_sc[...] += jnp.dot(dst, q, preferred_element_type=F32)
            dq_ref[rows, :] += lax.dot_general(dst, k, TN, preferred_element_type=F32)

        step(j, 1, True)
        n_after = nq - 1 - j

        def two(p, carry):
            step(j + 1 + 2 * p, 2, False)
            return carry

        lax.fori_loop(0, n_after // 2, two, 0)

        @pl.when(n_after % 2 == 1)
        def _():
            step(nq - 1, 1, False)
        dkv_ref[:, :QK_NOPE] = (dk_sc[:, :QK_NOPE] * SCALE).astype(dkv_ref.dtype)
        dkv_ref[:, QK_NOPE:] = dv_sc[...].astype(dkv_ref.dtype)
        dkr_ref[...] = dk_sc[:, QK_NOPE:] * SCALE

    rows = pl.BlockSpec((None, nq, 8, tq), lambda h, j: (h, 0, 0, 0))
    return pl.pallas_call(
        body, name=name, grid=(N_HEADS, nq),
        in_specs=[pl.BlockSpec((tq, HEAD_PAD), lambda h, j: (j, h)),
                  pl.BlockSpec((tq, V_DIM), lambda h, j: (j, 2 * h + 1)),
                  pl.BlockSpec((s, HEAD_PAD), lambda h, j: (0, h)),
                  pl.BlockSpec((s, V_DIM), lambda h, j: (0, h)), rows, rows],
        out_specs=[pl.BlockSpec((tq, 256), lambda h, j: (j, h)),
                   pl.BlockSpec((tq, 128), lambda h, j: (j, h)),
                   pl.BlockSpec((s, HEAD_PAD), lambda h, j: (0, h))],
        out_shape=[jax.ShapeDtypeStruct((s, N_HEADS * 256), BF),
                   jax.ShapeDtypeStruct((s, N_HEADS * 128), F32),
                   jax.ShapeDtypeStruct((s, N_HEADS * HEAD_PAD), F32)],
        scratch_shapes=[pltpu.VMEM((tq, HEAD_PAD), F32), pltpu.VMEM((tq, V_DIM), F32)],
        compiler_params=_params(2))(kc, kv, qr, do, lse, delta)


def adamw(w, g, m, v, *, name, t=256):
    r, c = w.shape
    t = r if r % t else t
    c1 = 1.0 - ADAM_B1 ** ADAM_STEP
    c2 = 1.0 - ADAM_B2 ** ADAM_STEP

    def body(w_ref, g_ref, m_ref, v_ref, d_ref, nm_ref, nv_ref):
        gv = g_ref[...]
        nm = ADAM_B1 * m_ref[...] + (1.0 - ADAM_B1) * gv
        nv = ADAM_B2 * v_ref[...] + (1.0 - ADAM_B2) * (gv * gv)
        nm_ref[...] = nm
        nv_ref[...] = nv
        d_ref[...] = -ADAM_LR * ((nm / c1) / (jnp.sqrt(nv / c2) + ADAM_EPS) + ADAM_WD * w_ref[...])

    blk = pl.BlockSpec((t, c), lambda i: (i, 0))
    return pl.pallas_call(
        body, name=name, grid=(r // t,), in_specs=[blk] * 4, out_specs=[blk] * 3,
        out_shape=[jax.ShapeDtypeStruct((r, c), F32)] * 3, compiler_params=_params(1))(w, g, m, v)


def sum_devices(parts, *, name):
    def body(p_ref, o_ref):
        acc = p_ref[pl.ds(0, SV_ROWS), :]
        for d in range(1, 8):
            acc = acc + p_ref[pl.ds(d * SV_ROWS, SV_ROWS), :]
        o_ref[...] = acc

    return pl.pallas_call(body, name=name, out_shape=jax.ShapeDtypeStruct((SV_ROWS, SV_COLS), F32))(parts)


def add_halves(g, rb, c_idx, *, name, rows):
    nq, r2, cc = rb.shape
    nb = r2 // rows

    def body(c_ref, g_ref, r_ref, o_ref):
        o_ref[...] = (g_ref[...] + r_ref[...]).astype(o_ref.dtype)

    grid_spec = pltpu.PrefetchScalarGridSpec(
        num_scalar_prefetch=1, grid=(nq, nb),
        in_specs=[pl.BlockSpec((None, rows, cc), lambda q, i, c: (q, c[0] * nb + i, 0)),
                  pl.BlockSpec((None, rows, cc), lambda q, i, c: (q, i, 0))],
        out_specs=pl.BlockSpec((None, rows, cc), lambda q, i, c: (q, i, 0)))
    return pl.pallas_call(body, name=name, grid_spec=grid_spec,
                          out_shape=jax.ShapeDtypeStruct((nq, r2, cc), BF),
                          compiler_params=_params(2))(c_idx, g, rb)


def sum_chips(rc, *, name, rows):
    nq, r2, cc = rc.shape

    def body(r_ref, o_ref):
        parts = [r_ref[q].astype(F32) for q in range(4)]
        o_ref[...] = ((parts[0] + parts[1]) + parts[2]) + parts[3]

    return pl.pallas_call(
        body, name=name, grid=(r2 // rows,),
        in_specs=[pl.BlockSpec((nq, rows, cc), lambda i: (0, i, 0))],
        out_specs=pl.BlockSpec((rows, cc), lambda i: (i, 0)),
        out_shape=jax.ShapeDtypeStruct((r2, cc), F32), compiler_params=_params(1))(rc)


def _place():
    return lax.axis_index("x"), lax.axis_index("y"), lax.axis_index("c")


def all_gather8(xs, *, name, own_half):
    m = xs.shape[0] // 2 if own_half else xs.shape[0]
    n = xs.shape[1]

    def body(x_ref, out_ref, send_sems, recv_sems, local_sem):
        x, y, c = _place()
        me, sibling = (x, y, c), (x, y, 1 - c)
        chips = [(1 - x, y), (x, 1 - y), (1 - x, 1 - y)]
        src_own = x_ref.at[pl.ds(c * m, m), :] if own_half else x_ref

        def rows(px, py, pc):
            return out_ref.at[pl.ds((4 * px + 2 * py + pc) * m, m), :]

        def copy(k, block, to, src=None):
            return pltpu.make_async_remote_copy(
                src_ref=rows(*block) if src is None else src, dst_ref=rows(*block),
                send_sem=send_sems.at[k], recv_sem=recv_sems.at[k], device_id=to, device_id_type=MESH)

        mine = pltpu.make_async_copy(src_own, rows(*me), local_sem)
        mine.start()
        first = [copy(0, me, sibling, src=src_own)]
        first += [copy(1 + j, me, (*chip, c), src=src_own) for j, chip in enumerate(chips)]
        for cp in first:
            cp.start()
        passed = [copy(4 + j, (*chip, c), sibling) for j, chip in enumerate(chips)]
        for j, chip in enumerate(chips):
            copy(1 + j, (*chip, c), me).wait_recv()
            passed[j].start()
        copy(0, sibling, me).wait_recv()
        for j, chip in enumerate(chips):
            copy(4 + j, (*chip, 1 - c), me).wait_recv()
        for cp in first + passed:
            cp.wait_send()
        mine.wait()

    return pl.pallas_call(
        body, name=name, out_shape=jax.ShapeDtypeStruct((8 * m, n), xs.dtype),
        in_specs=[pl.BlockSpec(memory_space=pl.ANY)], out_specs=pl.BlockSpec(memory_space=pl.ANY),
        scratch_shapes=[pltpu.SemaphoreType.DMA((7,)), pltpu.SemaphoreType.DMA((7,)), pltpu.SemaphoreType.DMA],
    )(xs)


def swap_halves(g, *, name):
    nq, r, cc = g.shape
    r2 = r // 2

    def body(g_ref, out_ref, send_sem, recv_sem):
        x, y, c = _place()
        cp = pltpu.make_async_remote_copy(
            src_ref=g_ref.at[:, pl.ds((1 - c) * r2, r2), :], dst_ref=out_ref,
            send_sem=send_sem, recv_sem=recv_sem, device_id=(x, y, 1 - c), device_id_type=MESH)
        cp.start()
        cp.wait()

    return pl.pallas_call(
        body, name=name, out_shape=jax.ShapeDtypeStruct((nq, r2, cc), g.dtype),
        in_specs=[pl.BlockSpec(memory_space=pl.ANY)], out_specs=pl.BlockSpec(memory_space=pl.ANY),
        scratch_shapes=[pltpu.SemaphoreType.DMA, pltpu.SemaphoreType.DMA],
    )(g)


def exchange_chips(p, *, name):
    nq, r2, cc = p.shape

    def body(p_ref, out_ref, send_sems, recv_sems, local_sem):
        x, y, c = _place()
        q0 = 2 * x + y
        chips = [(1 - x, y), (x, 1 - y), (1 - x, 1 - y)]
        mine = pltpu.make_async_copy(p_ref.at[q0], out_ref.at[q0], local_sem)
        mine.start()
        sends = []
        for j, (cx, cy) in enumerate(chips):
            sends.append(pltpu.make_async_remote_copy(
                src_ref=p_ref.at[2 * cx + cy], dst_ref=out_ref.at[q0],
                send_sem=send_sems.at[j], recv_sem=recv_sems.at[j],
                device_id=(cx, cy, c), device_id_type=MESH))
        for cp in sends:
            cp.start()
        for j, (cx, cy) in enumerate(chips):
            qj = 2 * cx + cy
            pltpu.make_async_remote_copy(
                src_ref=p_ref.at[qj], dst_ref=out_ref.at[qj],
                send_sem=send_sems.at[j], recv_sem=recv_sems.at[j],
                device_id=(cx, cy, c), device_id_type=MESH).wait_recv()
        for cp in sends:
            cp.wait_send()
        mine.wait()

    return pl.pallas_call(
        body, name=name, out_shape=jax.ShapeDtypeStruct((nq, r2, cc), p.dtype),
        in_specs=[pl.BlockSpec(memory_space=pl.ANY)], out_specs=pl.BlockSpec(memory_space=pl.ANY),
        scratch_shapes=[pltpu.SemaphoreType.DMA((3,)), pltpu.SemaphoreType.DMA((3,)), pltpu.SemaphoreType.DMA],
    )(p)


def join_halves(tot, *, name):
    r2, cc = tot.shape

    def body(t_ref, out_ref, send_sem, recv_sem, local_sem):
        x, y, c = _place()
        mine = pltpu.make_async_copy(t_ref, out_ref.at[pl.ds(c * r2, r2), :], local_sem)
        mine.start()
        cp = pltpu.make_async_remote_copy(
            src_ref=t_ref, dst_ref=out_ref.at[pl.ds(c * r2, r2), :],
            send_sem=send_sem, recv_sem=recv_sem, device_id=(x, y, 1 - c), device_id_type=MESH)
        cp.start()
        pltpu.make_async_remote_copy(
            src_ref=t_ref, dst_ref=out_ref.at[pl.ds((1 - c) * r2, r2), :],
            send_sem=send_sem, recv_sem=recv_sem, device_id=(x, y, 1 - c), device_id_type=MESH).wait_recv()
        cp.wait_send()
        mine.wait()

    return pl.pallas_call(
        body, name=name, out_shape=jax.ShapeDtypeStruct((2 * r2, cc), tot.dtype),
        in_specs=[pl.BlockSpec(memory_space=pl.ANY)], out_specs=pl.BlockSpec(memory_space=pl.ANY),
        scratch_shapes=[pltpu.SemaphoreType.DMA, pltpu.SemaphoreType.DMA, pltpu.SemaphoreType.DMA],
    )(tot)


def _pack_local_shard(big, small_vec):
    parts = [w.reshape(-1, PACK_C).astype(BF) for w in big]
    srow = lax.bitcast_convert_type(small_vec, BF).reshape(1, PACK_C)
    parts.append(jnp.pad(srow, ((0, PACK_PAD - 1), (0, 0))))
    return jnp.concatenate(parts, axis=0)


def _split_rows(a, axis):
    out, off = [], 0
    for n in PACK_ROWS:
        out.append(lax.slice_in_dim(a, off, off + n, axis=axis))
        off += n
    return out, off


def _unpack_gathered(gw):
    (p_in, p_grp, p_out, m_in, m_qb, m_kvb, m_out), off = _split_rows(gw, 1)
    w = {}
    w["pool_w_in"] = p_in.reshape(4, D_MODEL, 1024).transpose(1, 0, 2).reshape(D_MODEL, 2 * POOL_WIDTH)
    w["pool_w_group"] = p_grp.reshape(4, 4, 128, POOL_GROUP).transpose(1, 0, 2, 3).reshape(4, POOL_GROUP, POOL_GROUP)
    w["pool_w_out"] = p_out.reshape(POOL_WIDTH, D_MODEL)
    win = m_in.reshape(4, D_MODEL, 688).transpose(1, 0, 2).reshape(D_MODEL, 2752)
    w["mla_w_in"] = jnp.concatenate(
        [win[:, 384:640], win[:, 640:704], jnp.zeros((D_MODEL, 64), BF), win[:, 0:384], win[:, 704:]], axis=1)
    wq = m_qb.reshape(4, Q_LORA, 768).transpose(1, 0, 2).reshape(Q_LORA, N_HEADS, QK_NOPE + QK_ROPE)
    w["mla_w_q_b"] = jnp.pad(wq, ((0, 0), (0, 0), (0, HEAD_PAD - QK_NOPE - QK_ROPE))).reshape(Q_LORA, N_HEADS * HEAD_PAD)
    w["mla_w_kv_b"] = m_kvb.reshape(4, KV_LORA, 1024).transpose(1, 0, 2).reshape(KV_LORA, 4096)
    w["mla_w_out"] = m_out.reshape(MLA_WIDTH, D_MODEL)
    small = lax.bitcast_convert_type(gw[:, off, :].reshape(4, 512, 2), F32)
    w["mla_norm"] = small[:, :256].reshape(1, D_MODEL)
    w["mla_q_norm"] = small[:, 256:352].reshape(1, Q_LORA)
    w["mla_kv_norm"] = small[:, 352:416].reshape(1, KV_LORA)
    return w


def _pack_grads(g):
    parts = [
        g["pool_w_in"].reshape(D_MODEL, 4, 1024).transpose(1, 0, 2),
        g["pool_w_group"].reshape(4, 4, 128, POOL_GROUP).transpose(1, 0, 2, 3).reshape(4, 256, PACK_C),
        g["pool_w_out"].reshape(4, 512, PACK_C),
        g["mla_w_in"].reshape(D_MODEL, 4, 688).transpose(1, 0, 2).reshape(4, 688, PACK_C),
        g["mla_w_q_b"].reshape(Q_LORA, 4, 768).transpose(1, 0, 2).reshape(4, 288, PACK_C),
        g["mla_w_kv_b"].reshape(KV_LORA, 4, 1024).transpose(1, 0, 2),
        g["mla_w_out"].reshape(4, 512, PACK_C),
        jnp.zeros((4, PACK_PAD, PACK_C), F32),
    ]
    return jnp.concatenate(parts, axis=1)


def kernel(x, positions, pool_norm, pool_w_in, pool_w_group, pool_scale, pool_w_out, mla_norm, mla_w_in, mla_q_norm, mla_w_q_b, mla_kv_norm, mla_w_kv_b, mla_w_out, final_norm, loss_target, m_pool_norm, m_pool_w_in, m_pool_w_group, m_pool_scale, m_pool_w_out, m_mla_norm, m_mla_w_in, m_mla_q_norm, m_mla_w_q_b, m_mla_kv_norm, m_mla_w_kv_b, m_mla_w_out, m_final_norm, v_pool_norm, v_pool_w_in, v_pool_w_group, v_pool_scale, v_pool_w_out, v_mla_norm, v_mla_w_in, v_mla_q_norm, v_mla_w_q_b, v_mla_kv_norm, v_mla_w_kv_b, v_mla_w_out, v_final_norm):
    s = x.shape[1]
    tq = min(512, s)
    x0 = x.reshape(s, D_MODEL)
    tgt = loss_target.reshape(s, D_MODEL)
    cx, cy, cc_idx = _place()
    chip = 2 * cx + cy

    big_names = ("pool_w_in", "pool_w_group", "pool_w_out", "mla_w_in", "mla_w_q_b", "mla_w_kv_b", "mla_w_out")
    big_w = dict(zip(big_names, (pool_w_in, pool_w_group, pool_w_out, mla_w_in, mla_w_q_b, mla_w_kv_b, mla_w_out)))
    big_m = dict(zip(big_names, (m_pool_w_in, m_pool_w_group, m_pool_w_out, m_mla_w_in, m_mla_w_q_b, m_mla_w_kv_b, m_mla_w_out)))
    big_v = dict(zip(big_names, (v_pool_w_in, v_pool_w_group, v_pool_w_out, v_mla_w_in, v_mla_w_q_b, v_mla_w_kv_b, v_mla_w_out)))

    small_vec = jnp.concatenate([mla_norm.reshape(-1), mla_q_norm.reshape(-1), mla_kv_norm.reshape(-1),
                                 jnp.zeros((96,), F32)])
    packed = _pack_local_shard([big_w[n] for n in big_names], small_vec)
    gathered = all_gather8(packed, name="gather_weights", own_half=True).reshape(4, PACK_R, PACK_C)
    w = _unpack_gathered(gathered)
    g_pool = pool_norm.reshape(1, D_MODEL)
    g_final = final_norm.reshape(1, D_MODEL)
    sc_pool = pool_scale.reshape(1, POOL_WIDTH)

    inv_freq = 1.0 / (ROPE_THETA ** (jnp.arange(0, QK_ROPE, 2, dtype=F32) / QK_ROPE))
    ang = positions.reshape(s).astype(F32)[:, None] * inv_freq
    cos, sin = jnp.cos(ang), jnp.sin(ang)
    z32, z64, z96 = (jnp.zeros((s, n), F32) for n in (32, 64, 96))
    t_cc = jnp.concatenate([cos, cos, z64], axis=1)
    t_sa = jnp.concatenate([-sin, z96], axis=1)
    t_sb = jnp.concatenate([z32, sin, z64], axis=1)

    h0 = norm_fwd(x0, g_pool, col=0, width=D_MODEL, name="pool_norm_fwd")
    uz = mm_nn(h0, w["pool_w_in"], name="pool_in_proj", out_dtype=F32)
    pd = pool_prep(uz, name="pool_window")
    mm = gmm_nn(pd, w["pool_w_group"], name="pool_group_mix")
    y1 = pool_gate(mm, uz, sc_pool, name="pool_gate")
    x1 = mm_nn(y1, w["pool_w_out"], name="pool_out_proj", out_dtype=F32, add=x0)

    h1 = norm_fwd(x1, w["mla_norm"], col=0, width=D_MODEL, name="mla_norm_fwd")
    proj = mm_nn(h1, w["mla_w_in"], name="mla_in_proj", out_dtype=F32, tn=P_WIDTH // 2)
    qn = norm_fwd(proj, w["mla_q_norm"], col=P_Q, width=Q_LORA, name="mla_q_norm_fwd")
    kvn = norm_fwd(proj, w["mla_kv_norm"], col=P_KV, width=KV_LORA, name="mla_kv_norm_fwd")
    q_pre = mm_nn(qn, w["mla_w_q_b"], name="mla_q_proj", out_dtype=F32, tk=Q_LORA)
    kv = mm_nn(kvn, w["mla_w_kv_b"], name="mla_kv_proj", out_dtype=BF, tk=KV_LORA)
    qr = rope_q(q_pre, t_cc, t_sa, t_sb, name="mla_rope_q")
    kc = pack_k(kv, proj, t_cc, t_sa, t_sb, name="mla_pack_k")
    o, lse = attn_fwd(qr, kc, kv, name="mla_attn_fwd", tq=tq)
    y2 = mla_gate(o, proj, name="mla_gate")
    x2 = mm_nn(y2, w["mla_w_out"], name="mla_out_proj", out_dtype=F32, add=x1)

    dx2, d_final, loss_part = final_loss(x2, g_final, tgt, name="final_norm_loss")

    grads = {}
    dy2 = mm_nt(dx2, w["mla_w_out"], name="mla_out_proj_dx", out_dtype=F32)
    grads["mla_w_out"] = mm_tn(y2, dx2, name="mla_out_proj_dw")
    do, dz2, delta = mla_gate_bwd(dy2, o, proj, name="mla_gate_bwd", tq=tq)
    dkv, dkr, dqr = attn_bwd(qr, kc, kv, do, lse, delta, name="mla_attn_bwd", tq=tq)
    dq_pre = unrope_q(dqr, t_cc, t_sa, t_sb, name="mla_unrope_q")
    dkr_pre = unrope_k(dkr, t_cc, t_sa, t_sb, name="mla_unrope_k")
    dqn = mm_nt(dq_pre, w["mla_w_q_b"], name="mla_q_proj_dx", out_dtype=F32, tn=Q_LORA, tk=2048)
    g_qb = mm_tn(qn, dq_pre, name="mla_q_proj_dw", tm=Q_LORA, tn=2048)
    dkvn = mm_nt(dkv, w["mla_w_kv_b"], name="mla_kv_proj_dx", out_dtype=F32, tn=KV_LORA, tk=2048)
    grads["mla_w_kv_b"] = mm_tn(kvn, dkv, name="mla_kv_proj_dw", tm=KV_LORA, tn=2048)
    dq_lat, d_qnorm = norm_bwd(proj, w["mla_q_norm"], dqn, col=P_Q, width=Q_LORA, name="mla_q_norm_bwd", out_dtype=BF)
    dkv_lat, d_kvnorm = norm_bwd(proj, w["mla_kv_norm"], dkvn, col=P_KV, width=KV_LORA, name="mla_kv_norm_bwd", out_dtype=BF)
    dsmall = jnp.concatenate([dkv_lat, dkr_pre, dq_lat], axis=1)
    dh1 = mm_nt(dsmall, w["mla_w_in"], name="mla_in_proj_dx_a", out_dtype=F32, tk=P_SMALL)
    dh1 = mm_nt(dz2, w["mla_w_in"][:, P_Z:], name="mla_in_proj_dx_b", out_dtype=F32, add=dh1)
    g_in_a = mm_tn(h1, dsmall, name="mla_in_proj_dw_a", tn=P_SMALL)
    g_in_b = mm_tn(h1, dz2, name="mla_in_proj_dw_b", tn=2048)
    dx1, d_mnorm = norm_bwd(x1, w["mla_norm"], dh1, col=0, width=D_MODEL, name="mla_norm_bwd", res=dx2)

    dy1 = mm_nt(dx1, w["pool_w_out"], name="pool_out_proj_dx", out_dtype=F32)
    grads["pool_w_out"] = mm_tn(y1, dx1, name="pool_out_proj_dw")
    dmm, dz1, d_scale = pool_gate_bwd(dy1, mm, uz, sc_pool, name="pool_gate_bwd")
    dpd = gmm_nt(dmm, w["pool_w_group"], name="pool_group_mix_dx")
    grads["pool_w_group"] = gmm_tn(pd, dmm, 4, name="pool_group_mix_dw")
    du = pool_prep_bwd(dpd, name="pool_window_bwd")
    dh0 = mm_nt(du, w["pool_w_in"], name="pool_in_proj_dx_u", out_dtype=F32, tk=1024)
    dh0 = mm_nt(dz1, w["pool_w_in"], name="pool_in_proj_dx_z", out_dtype=F32, b_col=POOL_WIDTH, add=dh0, tk=1024)
    g_pin_u = mm_tn(h0, du, name="pool_in_proj_dw_u", tn=2048)
    g_pin_z = mm_tn(h0, dz1, name="pool_in_proj_dw_z", tn=2048)
    grad_x, d_pnorm = norm_bwd(x0, g_pool, dh0, col=0, width=D_MODEL, name="pool_norm_bwd", res=dx1)

    grads["pool_w_in"] = jnp.concatenate([g_pin_u, g_pin_z], axis=1)
    g_in = jnp.concatenate([g_in_a, g_in_b], axis=1)
    grads["mla_w_in"] = jnp.concatenate([g_in[:, P_Q:P_Z], g_in[:, P_KV:P_KV + KV_LORA],
                                         g_in[:, P_KR:P_KR + QK_ROPE], g_in[:, P_Z:]], axis=1)
    grads["mla_w_q_b"] = g_qb.reshape(Q_LORA, N_HEADS, HEAD_PAD)[:, :, :QK_NOPE + QK_ROPE].reshape(Q_LORA, -1)

    gp = _pack_grads(grads)
    half_rows = PACK_R // 2 // 3
    sib = swap_halves(gp, name="grad_swap_halves")
    pre = add_halves(gp, sib, cc_idx.reshape(1).astype(jnp.int32), name="grad_add_halves", rows=half_rows)
    got = exchange_chips(pre, name="grad_exchange_chips")
    tot = sum_chips(got, name="grad_sum_chips", rows=half_rows)
    red = join_halves(tot, name="grad_join_halves")
    red_parts, _ = _split_rows(red, 0)

    sv = jnp.concatenate([d_pnorm.reshape(-1), d_scale.reshape(-1), d_final.reshape(-1), d_mnorm.reshape(-1),
                          d_qnorm.reshape(-1), d_kvnorm.reshape(-1), loss_part[0, :1],
                          jnp.zeros((SV_ROWS * SV_COLS - SV_OFF["loss"] - 1,), F32)]).reshape(SV_ROWS, SV_COLS)
    sv_all = all_gather8(sv, name="gather_small_grads", own_half=False)
    sv_sum = sum_devices(sv_all, name="sum_small_grads").reshape(-1)
    loss = sv_sum[SV_OFF["loss"]]

    def sv_take(key, n):
        return lax.slice_in_dim(sv_sum, SV_OFF[key], SV_OFF[key] + n)

    out_g, out_d, out_m, out_v = {}, {}, {}, {}
    for name, part in zip(big_names, red_parts):
        shp = big_w[name].shape
        g2 = part.reshape(shp)
        two_d = (-1, shp[-1])
        d_, m_, v_ = adamw(big_w[name].reshape(two_d), g2.reshape(two_d), big_m[name].reshape(two_d),
                           big_v[name].reshape(two_d), name="adamw_" + name)
        out_g[name], out_d[name], out_m[name], out_v[name] = g2, d_.reshape(shp), m_.reshape(shp), v_.reshape(shp)

    small = [
        ("pool_norm", pool_norm, m_pool_norm, v_pool_norm, sv_take("pool_norm", 1024)),
        ("pool_scale", pool_scale, m_pool_scale, v_pool_scale, sv_take("pool_scale", 2048)),
        ("final_norm", final_norm, m_final_norm, v_final_norm, sv_take("final_norm", 1024)),
        ("mla_norm", mla_norm, m_mla_norm, v_mla_norm,
         lax.dynamic_slice_in_dim(sv_take("mla_norm", 1024), chip * 256, 256)),
        ("mla_q_norm", mla_q_norm, m_mla_q_norm, v_mla_q_norm,
         lax.dynamic_slice_in_dim(sv_take("q_norm", 384), chip * 96, 96)),
        ("mla_kv_norm", mla_kv_norm, m_mla_kv_norm, v_mla_kv_norm,
         lax.dynamic_slice_in_dim(sv_take("kv_norm", 256), chip * 64, 64)),
    ]
    sw = jnp.concatenate([t[1].reshape(-1) for t in small] + [jnp.zeros((96,), F32)]).reshape(1, -1)
    sm = jnp.concatenate([t[2].reshape(-1) for t in small] + [jnp.zeros((96,), F32)]).reshape(1, -1)
    s_v = jnp.concatenate([t[3].reshape(-1) for t in small] + [jnp.ones((96,), F32)]).reshape(1, -1)
    sg = jnp.concatenate([t[4].reshape(-1) for t in small] + [jnp.zeros((96,), F32)]).reshape(1, -1)
    sd_, sm_, sv_ = adamw(sw, sg, sm, s_v, name="adamw_vectors")
    off = 0
    for name, wt, _, _, gvec in small:
        n = gvec.shape[0]
        shp = wt.shape
        out_g[name] = gvec.reshape(shp)
        out_d[name] = sd_[0, off:off + n].reshape(shp)
        out_m[name] = sm_[0, off:off + n].reshape(shp)
        out_v[name] = sv_[0, off:off + n].reshape(shp)
        off += n

    order = ("pool_norm", "pool_w_in", "pool_w_group", "pool_scale", "pool_w_out", "mla_norm", "mla_w_in",
             "mla_q_norm", "mla_w_q_b", "mla_kv_norm", "mla_w_kv_b", "mla_w_out", "final_norm")
    return (loss, grad_x.reshape(x.shape), *[out_g[n] for n in order], *[out_d[n] for n in order],
            *[out_m[n] for n in order], *[out_v[n] for n in order])
```

```python
import functools

import jax
import jax.numpy as jnp
from jax import lax
from jax.experimental import pallas as pl
from jax.experimental.pallas import tpu as pltpu

F32 = jnp.float32
BF = jnp.bfloat16
MESH = pl.DeviceIdType.MESH

D_MODEL = 1024
POOL_WIDTH = 2048
POOL_WINDOWS = (2, 4, 8, 16)
POOL_GROUP = 512
HALO = 16
N_HEADS = 16
QK_NOPE = 128
QK_ROPE = 64
V_DIM = 128
HEAD_PAD = 256
Q_LORA = 384
KV_LORA = 256
MLA_WIDTH = 2048
ROPE_THETA = 10000.0
EPS = 1e-6
SCALE = (QK_NOPE + QK_ROPE) ** -0.5
SCALE_LOG2E = SCALE * 1.4426950408889634
NEG = -1e30

P_KV, P_KR, P_Q, P_Z = 0, 256, 384, 768
P_SMALL = 768
P_WIDTH = 2816

ADAM_LR = 0.001
ADAM_B1 = 0.9
ADAM_B2 = 0.999
ADAM_EPS = 1e-08
ADAM_WD = 0.01
ADAM_STEP = 10

NN = (((1,), (0,)), ((), ()))
NT = (((1,), (1,)), ((), ()))
TN = (((0,), (0,)), ((), ()))

PACK_ROWS = (1024, 256, 512, 688, 288, 256, 512)
PACK_PAD = 16
PACK_R = sum(PACK_ROWS) + PACK_PAD
PACK_C = 1024
SV_OFF = dict(pool_norm=0, pool_scale=1024, final_norm=3072, mla_norm=4096, q_norm=5120, kv_norm=5504, loss=5760)
SV_ROWS, SV_COLS = 8, 768

VMEM_LIMIT = 56 * 1024 * 1024


def _params(n_axes, vmem=None):
    return pltpu.CompilerParams(dimension_semantics=("arbitrary",) * n_axes,
                                vmem_limit_bytes=VMEM_LIMIT if vmem is None else vmem)


def _sigmoid(z):
    return 1.0 / (1.0 + jnp.exp(-z))


def _mm(a, b, *, dims, grid, a_spec, b_spec, o_spec, out_shape, out_dtype, acc_shape, name,
        add=None, add_spec=None):
    nk = grid[-1]
    kax = len(grid) - 1

    def body(*refs):
        if add is None:
            a_ref, b_ref, o_ref = refs[:3]
            add_ref = None
            rest = refs[3:]
        else:
            a_ref, b_ref, add_ref, o_ref = refs[:4]
            rest = refs[4:]
        part = lax.dot_general(a_ref[...].astype(BF), b_ref[...].astype(BF), dims,
                               preferred_element_type=F32)

        def finish(r):
            if add_ref is not None:
                r = r + add_ref[...]
            o_ref[...] = r.astype(o_ref.dtype)

        if nk == 1:
            finish(part)
        else:
            acc = rest[0]
            k = pl.program_id(kax)

            @pl.when(k == 0)
            def _():
                acc[...] = part

            @pl.when(k > 0)
            def _():
                acc[...] += part

            @pl.when(k == nk - 1)
            def _():
                finish(acc[...])

    in_specs = [a_spec, b_spec]
    args = [a, b]
    if add is not None:
        in_specs.append(add_spec)
        args.append(add)
    scratch = [] if nk == 1 else [pltpu.VMEM(acc_shape, F32)]
    return pl.pallas_call(
        body, name=name, grid=grid, in_specs=in_specs, out_specs=o_spec,
        out_shape=jax.ShapeDtypeStruct(out_shape, out_dtype), scratch_shapes=scratch,
        compiler_params=_params(len(grid)))(*args)


def _pick(n, t):
    t = min(n, t)
    assert n % t == 0, (n, t)
    return t


def mm_nn(a, b, *, name, out_dtype, a_col=0, k_size=None, add=None, tm=1024, tn=1024, tk=1024):
    m = a.shape[0]
    kk, n = b.shape
    assert k_size is None or k_size == kk
    tm, tn, tk = _pick(m, tm), _pick(n, tn), _pick(kk, tk)
    assert a_col % tk == 0
    ko = a_col // tk
    return _mm(a, b, dims=NN, grid=(m // tm, n // tn, kk // tk),
               a_spec=pl.BlockSpec((tm, tk), lambda i, j, k: (i, ko + k)),
               b_spec=pl.BlockSpec((tk, tn), lambda i, j, k: (k, j)),
               o_spec=pl.BlockSpec((tm, tn), lambda i, j, k: (i, j)),
               add=add, add_spec=pl.BlockSpec((tm, tn), lambda i, j, k: (i, j)),
               out_shape=(m, n), out_dtype=out_dtype, acc_shape=(tm, tn), name=name)


def mm_nt(a, b, *, name, out_dtype, b_col=0, add=None, tm=1024, tn=1024, tk=1024):
    m, kk = a.shape
    n = b.shape[0]
    tm, tn, tk = _pick(m, tm), _pick(n, tn), _pick(kk, tk)
    assert b_col % tk == 0
    ko = b_col // tk
    return _mm(a, b, dims=NT, grid=(m // tm, n // tn, kk // tk),
               a_spec=pl.BlockSpec((tm, tk), lambda i, j, k: (i, k)),
               b_spec=pl.BlockSpec((tn, tk), lambda i, j, k: (j, ko + k)),
               o_spec=pl.BlockSpec((tm, tn), lambda i, j, k: (i, j)),
               add=add, add_spec=pl.BlockSpec((tm, tn), lambda i, j, k: (i, j)),
               out_shape=(m, n), out_dtype=out_dtype, acc_shape=(tm, tn), name=name)


def mm_tn(a, b, *, name, a_col=0, m_size=None, b_col=0, n_size=None, tm=1024, tn=1024, tk=1024):
    s = a.shape[0]
    m = a.shape[1] if m_size is None else m_size
    n = b.shape[1] if n_size is None else n_size
    tm, tn, tk = _pick(m, tm), _pick(n, tn), _pick(s, tk)
    assert a_col % tm == 0 and b_col % tn == 0
    ao, bo = a_col // tm, b_col // tn
    return _mm(a, b, dims=TN, grid=(m // tm, n // tn, s // tk),
               a_spec=pl.BlockSpec((tk, tm), lambda i, j, k: (k, ao + i)),
               b_spec=pl.BlockSpec((tk, tn), lambda i, j, k: (k, bo + j)),
               o_spec=pl.BlockSpec((tm, tn), lambda i, j, k: (i, j)),
               out_shape=(m, n), out_dtype=F32, acc_shape=(tm, tn), name=name)


def gmm_nn(a, w, *, name, tm=1024):
    s = a.shape[0]
    g, kk, n = w.shape
    tm = _pick(s, tm)
    return _mm(a, w, dims=NN, grid=(s // tm, g, 1),
               a_spec=pl.BlockSpec((tm, kk), lambda i, gi, k: (i, gi)),
               b_spec=pl.BlockSpec((None, kk, n), lambda i, gi, k: (gi, 0, 0)),
               o_spec=pl.BlockSpec((tm, n), lambda i, gi, k: (i, gi)),
               out_shape=(s, g * n), out_dtype=F32, acc_shape=(tm, n), name=name)


def gmm_nt(a, w, *, name, tm=1024):
    s = a.shape[0]
    g, kk, n = w.shape
    tm = _pick(s, tm)
    return _mm(a, w, dims=NT, grid=(s // tm, g, 1),
               a_spec=pl.BlockSpec((tm, n), lambda i, gi, k: (i, gi)),
               b_spec=pl.BlockSpec((None, kk, n), lambda i, gi, k: (gi, 0, 0)),
               o_spec=pl.BlockSpec((tm, kk), lambda i, gi, k: (i, gi)),
               out_shape=(s, g * kk), out_dtype=F32, acc_shape=(tm, kk), name=name)


def gmm_tn(a, b, g, *, name, tk=1024):
    s = a.shape[0]
    kk, n = a.shape[1] // g, b.shape[1] // g
    tk = _pick(s, tk)
    return _mm(a, b, dims=TN, grid=(g, s // tk),
               a_spec=pl.BlockSpec((tk, kk), lambda gi, k: (k, gi)),
               b_spec=pl.BlockSpec((tk, n), lambda gi, k: (k, gi)),
               o_spec=pl.BlockSpec((None, kk, n), lambda gi, k: (gi, 0, 0)),
               out_shape=(g, kk, n), out_dtype=F32, acc_shape=(kk, n), name=name)


def norm_fwd(x, g, *, col, width, name, t=512):
    s = x.shape[0]
    t = _pick(s, t)
    cb = col // width
    assert col % width == 0

    def body(x_ref, g_ref, o_ref):
        xv = x_ref[...]
        inv = lax.rsqrt(jnp.mean(xv * xv, axis=-1, keepdims=True) + EPS)
        o_ref[...] = ((xv * inv) * g_ref[...]).astype(o_ref.dtype)

    return pl.pallas_call(
        body, name=name, grid=(s // t,),
        in_specs=[pl.BlockSpec((t, width), lambda i: (i, cb)), pl.BlockSpec((1, width), lambda i: (0, 0))],
        out_specs=pl.BlockSpec((t, width), lambda i: (i, 0)),
        out_shape=jax.ShapeDtypeStruct((s, width), BF), compiler_params=_params(1))(x, g)


def norm_bwd(x, g, dh, *, col, width, name, res=None, out_dtype=F32, t=512):
    s = x.shape[0]
    t = _pick(s, t)
    cb = col // width
    assert col % width == 0

    def body(*refs):
        if res is None:
            x_ref, g_ref, dh_ref, dx_ref, dg_ref = refs
        else:
            x_ref, g_ref, dh_ref, res_ref, dx_ref, dg_ref = refs
        xv = x_ref[...]
        inv = lax.rsqrt(jnp.mean(xv * xv, axis=-1, keepdims=True) + EPS)
        xhat = xv * inv
        dh_v = dh_ref[...]
        part = jnp.sum(dh_v * xhat, axis=0, keepdims=True)

        @pl.when(pl.program_id(0) == 0)
        def _():
            dg_ref[...] = part

        @pl.when(pl.program_id(0) > 0)
        def _():
            dg_ref[...] += part

        dxhat = dh_v * g_ref[...]
        dx = inv * (dxhat - xhat * jnp.mean(dxhat * xhat, axis=-1, keepdims=True))
        if res is not None:
            dx = dx + res_ref[...]
        dx_ref[...] = dx.astype(dx_ref.dtype)

    row = pl.BlockSpec((t, width), lambda i: (i, 0))
    vec = pl.BlockSpec((1, width), lambda i: (0, 0))
    in_specs = [pl.BlockSpec((t, width), lambda i: (i, cb)), vec, row]
    args = [x, g, dh]
    if res is not None:
        in_specs.append(row)
        args.append(res)
    return pl.pallas_call(
        body, name=name, grid=(s // t,), in_specs=in_specs, out_specs=[row, vec],
        out_shape=[jax.ShapeDtypeStruct((s, width), out_dtype), jax.ShapeDtypeStruct((1, width), F32)],
        compiler_params=_params(1))(*args)


def final_loss(x2, gf, tgt, *, name, t=512):
    s, d = x2.shape
    t = _pick(s, t)

    def body(x_ref, g_ref, t_ref, dx_ref, dg_ref, loss_ref):
        xv = x_ref[...]
        inv = lax.rsqrt(jnp.mean(xv * xv, axis=-1, keepdims=True) + EPS)
        xhat = xv * inv
        gv = g_ref[...]
        diff = xhat * gv - t_ref[...]
        row_err = jnp.mean(diff * diff, axis=-1, keepdims=True)
        lpart = jnp.broadcast_to(0.5 * jnp.sum(row_err, axis=0, keepdims=True), (1, 128))
        dout = diff * (1.0 / d)
        gpart = jnp.sum(dout * xhat, axis=0, keepdims=True)

        @pl.when(pl.program_id(0) == 0)
        def _():
            dg_ref[...] = gpart
            loss_ref[...] = lpart

        @pl.when(pl.program_id(0) > 0)
        def _():
            dg_ref[...] += gpart
            loss_ref[...] += lpart

        dxhat = dout * gv
        dx_ref[...] = inv * (dxhat - xhat * jnp.mean(dxhat * xhat, axis=-1, keepdims=True))

    row = pl.BlockSpec((t, d), lambda i: (i, 0))
    vec = pl.BlockSpec((1, d), lambda i: (0, 0))
    return pl.pallas_call(
        body, name=name, grid=(s // t,), in_specs=[row, vec, row],
        out_specs=[row, vec, pl.BlockSpec((1, 128), lambda i: (0, 0))],
        out_shape=[jax.ShapeDtypeStruct((s, d), F32), jax.ShapeDtypeStruct((1, d), F32),
                   jax.ShapeDtypeStruct((1, 128), F32)],
        compiler_params=_params(1))(x2, gf, tgt)


def pool_prep(uz, *, name, t=256):
    s = uz.shape[0]
    t = _pick(s, t)
    hb = t // HALO

    def body(u_ref, halo_ref, o_ref, buf):
        i = pl.program_id(0)
        buf[pl.ds(HALO, t), :] = u_ref[...]

        @pl.when(i == 0)
        def _():
            buf[pl.ds(0, HALO), :] = jnp.zeros((HALO, POOL_WIDTH), F32)

        @pl.when(i > 0)
        def _():
            buf[pl.ds(0, HALO), :] = halo_ref[...]

        pos = i * t + lax.broadcasted_iota(jnp.int32, (t, POOL_GROUP), 0)
        for g, w in enumerate(POOL_WINDOWS):
            cols = pl.ds(g * POOL_GROUP, POOL_GROUP)
            cur = buf[pl.ds(HALO, t), cols]
            acc = cur
            for k in range(1, w):
                acc = acc + buf[pl.ds(HALO - k, t), cols]
            cnt = jnp.minimum(pos + 1, w).astype(F32)
            o_ref[:, cols] = (acc / cnt - cur).astype(o_ref.dtype)

    return pl.pallas_call(
        body, name=name, grid=(s // t,),
        in_specs=[pl.BlockSpec((t, POOL_WIDTH), lambda i: (i, 0)),
                  pl.BlockSpec((HALO, POOL_WIDTH), lambda i: (jnp.maximum(i * hb - 1, 0), 0))],
        out_specs=pl.BlockSpec((t, POOL_WIDTH), lambda i: (i, 0)),
        out_shape=jax.ShapeDtypeStruct((s, POOL_WIDTH), BF),
        scratch_shapes=[pltpu.VMEM((t + HALO, POOL_WIDTH), F32)],
        compiler_params=_params(1))(uz, uz)


def pool_prep_bwd(dpd, *, name, t=256):
    s = dpd.shape[0]
    t = _pick(s, t)
    hb = t // HALO
    n = s // t

    def body(d_ref, halo_ref, o_ref, buf):
        i = pl.program_id(0)
        pos = i * t + lax.broadcasted_iota(jnp.int32, (t, POOL_GROUP), 0)
        for g, w in enumerate(POOL_WINDOWS):
            cols = pl.ds(g * POOL_GROUP, POOL_GROUP)
            cnt = jnp.minimum(pos + 1, w).astype(F32)
            buf[pl.ds(0, t), cols] = d_ref[:, cols] / cnt

            @pl.when(i < n - 1)
            def _():
                buf[pl.ds(t, HALO), cols] = halo_ref[:, cols] / float(w)

            @pl.when(i == n - 1)
            def _():
                buf[pl.ds(t, HALO), cols] = jnp.zeros((HALO, POOL_GROUP), F32)

        for g, w in enumerate(POOL_WINDOWS):
            cols = pl.ds(g * POOL_GROUP, POOL_GROUP)
            acc = buf[pl.ds(0, t), cols]
            for k in range(1, w):
                acc = acc + buf[pl.ds(k, t), cols]
            o_ref[:, cols] = (acc - d_ref[:, cols]).astype(o_ref.dtype)

    return pl.pallas_call(
        body, name=name, grid=(n,),
        in_specs=[pl.BlockSpec((t, POOL_WIDTH), lambda i: (i, 0)),
                  pl.BlockSpec((HALO, POOL_WIDTH), lambda i: (jnp.minimum((i + 1) * hb, n * hb - 1), 0))],
        out_specs=pl.BlockSpec((t, POOL_WIDTH), lambda i: (i, 0)),
        out_shape=jax.ShapeDtypeStruct((s, POOL_WIDTH), BF),
        scratch_shapes=[pltpu.VMEM((t + HALO, POOL_WIDTH), F32)],
        compiler_params=_params(1))(dpd, dpd)


CHUNK = 512


def _chunks(width, step=CHUNK):
    return [slice(c, c + step) for c in range(0, width, step)]


def pool_gate(mm, uz, scale, *, name, t=256):
    s = mm.shape[0]
    t = _pick(s, t)

    def body(mm_ref, uz_ref, sc_ref, y_ref):
        for c in _chunks(POOL_WIDTH):
            z = uz_ref[:, slice(POOL_WIDTH + c.start, POOL_WIDTH + c.stop)]
            y_ref[:, c] = ((mm_ref[:, c] * sc_ref[:, c]) * (z * _sigmoid(z))).astype(y_ref.dtype)

    blk = pl.BlockSpec((t, POOL_WIDTH), lambda i: (i, 0))
    return pl.pallas_call(
        body, name=name, grid=(s // t,),
        in_specs=[blk, pl.BlockSpec((t, 2 * POOL_WIDTH), lambda i: (i, 0)), pl.BlockSpec((1, POOL_WIDTH), lambda i: (0, 0))],
        out_specs=blk, out_shape=jax.ShapeDtypeStruct((s, POOL_WIDTH), BF),
        compiler_params=_params(1))(mm, uz, scale)


def pool_gate_bwd(dy, mm, uz, scale, *, name, t=256):
    s = mm.shape[0]
    t = _pick(s, t)

    def body(dy_ref, mm_ref, uz_ref, sc_ref, dmm_ref, dz_ref, dsc_ref):
        parts = []
        for c in _chunks(POOL_WIDTH):
            z = uz_ref[:, slice(POOL_WIDTH + c.start, POOL_WIDTH + c.stop)]
            sig = _sigmoid(z)
            dyv = dy_ref[:, c]
            mmv = mm_ref[:, c]
            scv = sc_ref[:, c]
            dmixed = dyv * (z * sig)
            dmm_ref[:, c] = (dmixed * scv).astype(dmm_ref.dtype)
            dz_ref[:, c] = (dyv * (mmv * scv) * (sig * (1.0 + z * (1.0 - sig)))).astype(dz_ref.dtype)
            parts.append(jnp.sum(dmixed * mmv, axis=0, keepdims=True))

        @pl.when(pl.program_id(0) == 0)
        def _():
            for c, part in zip(_chunks(POOL_WIDTH), parts):
                dsc_ref[:, c] = part

        @pl.when(pl.program_id(0) > 0)
        def _():
            for c, part in zip(_chunks(POOL_WIDTH), parts):
                dsc_ref[:, c] += part

    blk = pl.BlockSpec((t, POOL_WIDTH), lambda i: (i, 0))
    vec = pl.BlockSpec((1, POOL_WIDTH), lambda i: (0, 0))
    return pl.pallas_call(
        body, name=name, grid=(s // t,),
        in_specs=[blk, blk, pl.BlockSpec((t, 2 * POOL_WIDTH), lambda i: (i, 0)), vec],
        out_specs=[blk, blk, vec],
        out_shape=[jax.ShapeDtypeStruct((s, POOL_WIDTH), BF), jax.ShapeDtypeStruct((s, POOL_WIDTH), BF),
                   jax.ShapeDtypeStruct((1, POOL_WIDTH), F32)],
        compiler_params=_params(1))(dy, mm, uz, scale)


def _rope(a, cc, sa, sb):
    return a * cc + pltpu.roll(a, 96, 1) * sa + pltpu.roll(a, 32, 1) * sb


def _unrope(d, cc, sa, sb):
    return d * cc + pltpu.roll(d * sa, 32, 1) + pltpu.roll(d * sb, 96, 1)


def rope_q(q_pre, cc, sa, sb, *, name, t=256):
    s = q_pre.shape[0]
    t = _pick(s, t)

    def body(q_ref, cc_ref, sa_ref, sb_ref, o_ref):
        for h in range(N_HEADS):
            nope = slice(h * HEAD_PAD, h * HEAD_PAD + QK_NOPE)
            rope = slice(h * HEAD_PAD + QK_NOPE, (h + 1) * HEAD_PAD)
            o_ref[:, nope] = q_ref[:, nope].astype(o_ref.dtype)
            o_ref[:, rope] = _rope(q_ref[:, rope], cc_ref[...], sa_ref[...], sb_ref[...]).astype(o_ref.dtype)

    tab = pl.BlockSpec((t, 128), lambda i: (i, 0))
    blk = pl.BlockSpec((t, N_HEADS * HEAD_PAD), lambda i: (i, 0))
    return pl.pallas_call(
        body, name=name, grid=(s // t,), in_specs=[blk, tab, tab, tab], out_specs=blk,
        out_shape=jax.ShapeDtypeStruct((s, N_HEADS * HEAD_PAD), BF), compiler_params=_params(1))(q_pre, cc, sa, sb)


def pack_k(kv, proj, cc, sa, sb, *, name, t=256):
    s = kv.shape[0]
    t = _pick(s, t)
    kr_blk = P_KR // 128

    def body(kv_ref, kr_ref, cc_ref, sa_ref, sb_ref, o_ref):
        kr = _rope(kr_ref[...], cc_ref[...], sa_ref[...], sb_ref[...]).astype(o_ref.dtype)
        for h in range(N_HEADS):
            nope = slice(h * HEAD_PAD, h * HEAD_PAD + QK_NOPE)
            o_ref[:, nope] = kv_ref[:, nope]
            o_ref[:, slice(h * HEAD_PAD + QK_NOPE, (h + 1) * HEAD_PAD)] = kr

    tab = pl.BlockSpec((t, 128), lambda i: (i, 0))
    blk = pl.BlockSpec((t, N_HEADS * HEAD_PAD), lambda i: (i, 0))
    return pl.pallas_call(
        body, name=name, grid=(s // t,),
        in_specs=[blk, pl.BlockSpec((t, 128), lambda i: (i, kr_blk)), tab, tab, tab], out_specs=blk,
        out_shape=jax.ShapeDtypeStruct((s, N_HEADS * HEAD_PAD), BF), compiler_params=_params(1))(kv, proj, cc, sa, sb)


def unrope_q(dqr, cc, sa, sb, *, name, t=256):
    s = dqr.shape[0]
    t = _pick(s, t)

    def body(d_ref, cc_ref, sa_ref, sb_ref, o_ref):
        for h in range(N_HEADS):
            nope = slice(h * HEAD_PAD, h * HEAD_PAD + QK_NOPE)
            rope = slice(h * HEAD_PAD + QK_NOPE, (h + 1) * HEAD_PAD)
            o_ref[:, nope] = (d_ref[:, nope] * SCALE).astype(o_ref.dtype)
            o_ref[:, rope] = _unrope(d_ref[:, rope] * SCALE, cc_ref[...], sa_ref[...], sb_ref[...]).astype(o_ref.dtype)

    tab = pl.BlockSpec((t, 128), lambda i: (i, 0))
    blk = pl.BlockSpec((t, N_HEADS * HEAD_PAD), lambda i: (i, 0))
    return pl.pallas_call(
        body, name=name, grid=(s // t,), in_specs=[blk, tab, tab, tab], out_specs=blk,
        out_shape=jax.ShapeDtypeStruct((s, N_HEADS * HEAD_PAD), BF), compiler_params=_params(1))(dqr, cc, sa, sb)


def unrope_k(dkr, cc, sa, sb, *, name, t=512):
    s = dkr.shape[0]
    t = _pick(s, t)

    def body(d_ref, cc_ref, sa_ref, sb_ref, o_ref):
        acc = d_ref[:, pl.ds(0, 128)]
        for h in range(1, N_HEADS):
            acc = acc + d_ref[:, pl.ds(h * 128, 128)]
        o_ref[...] = _unrope(acc, cc_ref[...], sa_ref[...], sb_ref[...]).astype(o_ref.dtype)

    tab = pl.BlockSpec((t, 128), lambda i: (i, 0))
    return pl.pallas_call(
        body, name=name, grid=(s // t,),
        in_specs=[pl.BlockSpec((t, N_HEADS * 128), lambda i: (i, 0)), tab, tab, tab], out_specs=tab,
        out_shape=jax.ShapeDtypeStruct((s, 128), BF), compiler_params=_params(1))(dkr, cc, sa, sb)


def mla_gate(o, proj, *, name, t=256):
    s = o.shape[0]
    t = _pick(s, t)

    def body(o_ref, p_ref, y_ref):
        for c in _chunks(MLA_WIDTH):
            z = p_ref[:, slice(P_Z + c.start, P_Z + c.stop)]
            y_ref[:, c] = (o_ref[:, c] * (z * _sigmoid(z))).astype(y_ref.dtype)

    blk = pl.BlockSpec((t, MLA_WIDTH), lambda i: (i, 0))
    return pl.pallas_call(
        body, name=name, grid=(s // t,),
        in_specs=[blk, pl.BlockSpec((t, P_WIDTH), lambda i: (i, 0))], out_specs=blk,
        out_shape=jax.ShapeDtypeStruct((s, MLA_WIDTH), BF), compiler_params=_params(1))(o, proj)


def mla_gate_bwd(dy, o, proj, *, name, tq):
    s = o.shape[0]
    nq = s // tq

    def body(dy_ref, o_ref, p_ref, do_ref, dz_ref, dl_ref):
        for h in range(N_HEADS):
            c = slice(h * V_DIM, (h + 1) * V_DIM)
            z = p_ref[:, slice(P_Z + c.start, P_Z + c.stop)]
            sig = _sigmoid(z)
            dyv = dy_ref[:, c]
            ov = o_ref[:, c]
            dov = dyv * (z * sig)
            do_ref[:, c] = dov.astype(do_ref.dtype)
            dz_ref[:, c] = (dyv * ov * (sig * (1.0 + z * (1.0 - sig)))).astype(dz_ref.dtype)
            delta = jnp.sum(dov * ov, axis=-1, keepdims=True)
            dl_ref[h] = jnp.broadcast_to(delta, (tq, 128)).T[:8, :]

    blk = pl.BlockSpec((tq, MLA_WIDTH), lambda i: (i, 0))
    return pl.pallas_call(
        body, name=name, grid=(nq,),
        in_specs=[blk, blk, pl.BlockSpec((tq, P_WIDTH), lambda i: (i, 0))],
        out_specs=[blk, blk, pl.BlockSpec((N_HEADS, None, 8, tq), lambda i: (0, i, 0, 0))],
        out_shape=[jax.ShapeDtypeStruct((s, MLA_WIDTH), BF), jax.ShapeDtypeStruct((s, MLA_WIDTH), BF),
                   jax.ShapeDtypeStruct((N_HEADS, nq, 8, tq), F32)],
        compiler_params=_params(1))(dy, o, proj)


FWD_GROUPS = (4, 2, 1)
BWD_GROUPS = (4, 2, 1)


def _for_groups(first, count, groups, fn):
    lead = groups[-1]
    for g in groups[:-1][::-1]:
        lead = jnp.where(count >= g, g, lead)
    for g in groups:
        @pl.when(lead == g)
        def _(g=g):
            fn(first, g, True)
    first = first + lead
    count = count - lead
    for g in groups:
        n = count // g

        def one(p, carry, g=g, first=first):
            fn(first + p * g, g, False)
            return carry

        lax.fori_loop(0, n, one, 0)
        first = first + n * g
        count = count - n * g


def attn_fwd(qr, kc, kv, *, name, tq):
    s = qr.shape[0]
    nq = s // tq

    def body(k_ref, v_ref, q_ref, o_ref, lse_ref, acc_sc, m_sc):
        j = pl.program_id(1)

        @pl.when(j == 0)
        def _():
            acc_sc[...] = jnp.zeros((s, 2 * V_DIM), F32)
            m_sc[...] = jnp.full((nq, 8, tq), NEG, F32)

        k = k_ref[...]
        vx = jnp.concatenate([v_ref[...], jnp.ones((tq, V_DIM), BF)], axis=1)

        def update(i, n_tiles, masked):
            rows = pl.ds(pl.multiple_of(i * tq, tq), n_tiles * tq)
            st = lax.dot_general(k, q_ref[rows, :], NT, preferred_element_type=F32) * SCALE_LOG2E
            if masked:
                krow = lax.broadcasted_iota(jnp.int32, (tq, n_tiles * tq), 0)
                qcol = lax.broadcasted_iota(jnp.int32, (tq, n_tiles * tq), 1)
                st = jnp.where(qcol >= krow, st, NEG)
            m_prev = jnp.concatenate([m_sc[i + n, pl.ds(0, 1), :] for n in range(n_tiles)], axis=1)
            m_new = jnp.maximum(m_prev, jnp.max(st, axis=0, keepdims=True))
            alpha_c = jnp.broadcast_to(jnp.exp2(m_prev - m_new), (128, n_tiles * tq)).T
            pt = jnp.exp2(st - m_new).astype(BF)
            pv = lax.dot_general(pt, vx, TN, preferred_element_type=F32)
            for cols in (slice(0, V_DIM), slice(V_DIM, 2 * V_DIM)):
                acc_sc[rows, cols] = alpha_c * acc_sc[rows, cols] + pv[:, cols]
            for n in range(n_tiles):
                m_sc[i + n, pl.ds(0, 1), :] = m_new[:, n * tq:(n + 1) * tq]

        _for_groups(j, nq - j, FWD_GROUPS, update)
        mine = pl.ds(pl.multiple_of(j * tq, tq), tq)
        l = acc_sc[mine, V_DIM:]
        o_ref[...] = acc_sc[mine, :V_DIM] / l
        lse_ref[...] = jnp.broadcast_to(m_sc[j, pl.ds(0, 1), :], (8, tq)) + jnp.log2(l).T[:8, :]

    return pl.pallas_call(
        body, name=name, grid=(N_HEADS, nq),
        in_specs=[pl.BlockSpec((tq, HEAD_PAD), lambda h, j: (j, h)),
                  pl.BlockSpec((tq, V_DIM), lambda h, j: (j, 2 * h + 1)),
                  pl.BlockSpec((s, HEAD_PAD), lambda h, j: (0, h))],
        out_specs=[pl.BlockSpec((tq, V_DIM), lambda h, j: (j, h)),
                   pl.BlockSpec((None, None, 8, tq), lambda h, j: (h, j, 0, 0))],
        out_shape=[jax.ShapeDtypeStruct((s, N_HEADS * V_DIM), F32),
                   jax.ShapeDtypeStruct((N_HEADS, nq, 8, tq), F32)],
        scratch_shapes=[pltpu.VMEM((s, 2 * V_DIM), F32), pltpu.VMEM((nq, 8, tq), F32)],
        compiler_params=_params(2))(kc, kv, qr)


def attn_bwd(qr, kc, kv, do, lse, delta, *, name, tq):
    s = qr.shape[0]
    nq = s // tq

    def body(k_ref, v_ref, q_ref, do_ref, lse_ref, dl_ref, dkv_ref, dkr_ref, dq_ref, dk_sc, dv_sc):
        j = pl.program_id(1)

        @pl.when(j == 0)
        def _():
            dq_ref[...] = jnp.zeros((s, HEAD_PAD), F32)

        dk_sc[...] = jnp.zeros((tq, HEAD_PAD), F32)
        dv_sc[...] = jnp.zeros((tq, V_DIM), F32)
        k = k_ref[...]
        v = v_ref[...]

        def step(i, n_tiles, masked):
            r0 = pl.multiple_of(i * tq, tq)
            rows = pl.ds(r0, n_tiles * tq)
            q = q_ref[rows, :]
            dov = do_ref[rows, :]
            lse_row = jnp.concatenate([lse_ref[i + n, pl.ds(0, 1), :] for n in range(n_tiles)], axis=1)
            dl_row = jnp.concatenate([dl_ref[i + n, pl.ds(0, 1), :] for n in range(n_tiles)], axis=1)
            st = lax.dot_general(k, q, NT, preferred_element_type=F32) * SCALE_LOG2E
            if masked:
                krow = lax.broadcasted_iota(jnp.int32, (tq, n_tiles * tq), 0)
                qcol = lax.broadcasted_iota(jnp.int32, (tq, n_tiles * tq), 1)
                st = jnp.where(qcol >= krow, st, NEG)
            pt = jnp.exp2(st - lse_row)
            dpt = lax.dot_general(v, dov, NT, preferred_element_type=F32)
            dst = (pt * (dpt - dl_row)).astype(BF)
            dv_sc[...] += jnp.dot(pt.astype(BF), dov, preferred_element_type=F32)
            dk_sc[...] += jnp.dot(dst, q, preferred_element_type=F32)
            dq_ref[rows, :] += lax.dot_general(dst, k, TN, preferred_element_type=F32)

        _for_groups(j, nq - j, BWD_GROUPS, step)
        dkv_ref[:, :QK_NOPE] = (dk_sc[:, :QK_NOPE] * SCALE).astype(dkv_ref.dtype)
        dkv_ref[:, QK_NOPE:] = dv_sc[...].astype(dkv_ref.dtype)
        dkr_ref[...] = dk_sc[:, QK_NOPE:] * SCALE

    rows = pl.BlockSpec((None, nq, 8, tq), lambda h, j: (h, 0, 0, 0))
    return pl.pallas_call(
        body, name=name, grid=(N_HEADS, nq),
        in_specs=[pl.BlockSpec((tq, HEAD_PAD), lambda h, j: (j, h)),
                  pl.BlockSpec((tq, V_DIM), lambda h, j: (j, 2 * h + 1)),
                  pl.BlockSpec((s, HEAD_PAD), lambda h, j: (0, h)),
                  pl.BlockSpec((s, V_DIM), lambda h, j: (0, h)), rows, rows],
        out_specs=[pl.BlockSpec((tq, 256), lambda h, j: (j, h)),
                   pl.BlockSpec((tq, 128), lambda h, j: (j, h)),
                   pl.BlockSpec((s, HEAD_PAD), lambda h, j: (0, h))],
        out_shape=[jax.ShapeDtypeStruct((s, N_HEADS * 256), BF),
                   jax.ShapeDtypeStruct((s, N_HEADS * 128), F32),
                   jax.ShapeDtypeStruct((s, N_HEADS * HEAD_PAD), F32)],
        scratch_shapes=[pltpu.VMEM((tq, HEAD_PAD), F32), pltpu.VMEM((tq, V_DIM), F32)],
        compiler_params=_params(2))(kc, kv, qr, do, lse, delta)


def adamw(w, g, m, v, *, name, t=256):
    r, c = w.shape
    t = r if r % t else t
    c1 = 1.0 - ADAM_B1 ** ADAM_STEP
    c2 = 1.0 - ADAM_B2 ** ADAM_STEP

    def body(w_ref, g_ref, m_ref, v_ref, d_ref, nm_ref, nv_ref):
        gv = g_ref[...]
        nm = ADAM_B1 * m_ref[...] + (1.0 - ADAM_B1) * gv
        nv = ADAM_B2 * v_ref[...] + (1.0 - ADAM_B2) * (gv * gv)
        nm_ref[...] = nm
        nv_ref[...] = nv
        d_ref[...] = -ADAM_LR * ((nm / c1) / (jnp.sqrt(nv / c2) + ADAM_EPS) + ADAM_WD * w_ref[...])

    blk = pl.BlockSpec((t, c), lambda i: (i, 0))
    return pl.pallas_call(
        body, name=name, grid=(r // t,), in_specs=[blk] * 4, out_specs=[blk] * 3,
        out_shape=[jax.ShapeDtypeStruct((r, c), F32)] * 3, compiler_params=_params(1))(w, g, m, v)


def sum_devices(parts, *, name):
    def body(p_ref, o_ref):
        acc = p_ref[pl.ds(0, SV_ROWS), :]
        for d in range(1, 8):
            acc = acc + p_ref[pl.ds(d * SV_ROWS, SV_ROWS), :]
        o_ref[...] = acc

    return pl.pallas_call(body, name=name, out_shape=jax.ShapeDtypeStruct((SV_ROWS, SV_COLS), F32))(parts)


def add_halves(g, rb, c_idx, *, name, rows):
    nq, r2, cc = rb.shape
    nb = r2 // rows

    def body(c_ref, g_ref, r_ref, o_ref):
        o_ref[...] = (g_ref[...] + r_ref[...]).astype(o_ref.dtype)

    grid_spec = pltpu.PrefetchScalarGridSpec(
        num_scalar_prefetch=1, grid=(nq, nb),
        in_specs=[pl.BlockSpec((None, rows, cc), lambda q, i, c: (q, c[0] * nb + i, 0)),
                  pl.BlockSpec((None, rows, cc), lambda q, i, c: (q, i, 0))],
        out_specs=pl.BlockSpec((None, rows, cc), lambda q, i, c: (q, i, 0)))
    return pl.pallas_call(body, name=name, grid_spec=grid_spec,
                          out_shape=jax.ShapeDtypeStruct((nq, r2, cc), BF),
                          compiler_params=_params(2))(c_idx, g, rb)


def sum_chips(rc, c_idx, *, name, rows):
    nq, r2, cc = rc.shape
    nb = r2 // rows

    def body(c_ref, r_ref, o_ref):
        parts = [r_ref[q].astype(F32) for q in range(4)]
        o_ref[...] = ((parts[0] + parts[1]) + parts[2]) + parts[3]

    grid_spec = pltpu.PrefetchScalarGridSpec(
        num_scalar_prefetch=1, grid=(nb,),
        in_specs=[pl.BlockSpec((nq, rows, cc), lambda i, c: (0, i, 0))],
        out_specs=pl.BlockSpec((rows, cc), lambda i, c: (c[0] * nb + i, 0)))
    return pl.pallas_call(body, name=name, grid_spec=grid_spec,
                          out_shape=jax.ShapeDtypeStruct((2 * r2, cc), F32),
                          compiler_params=_params(1))(c_idx, rc)


def _place():
    return lax.axis_index("x"), lax.axis_index("y"), lax.axis_index("c")


def all_gather8(xs, *, name, own_half):
    m = xs.shape[0] // 2 if own_half else xs.shape[0]
    n = xs.shape[1]

    def body(x_ref, out_ref, send_sems, recv_sems, local_sem):
        x, y, c = _place()
        me, sibling = (x, y, c), (x, y, 1 - c)
        chips = [(1 - x, y), (x, 1 - y), (1 - x, 1 - y)]
        src_own = x_ref.at[pl.ds(c * m, m), :] if own_half else x_ref

        def rows(px, py, pc):
            return out_ref.at[pl.ds((4 * px + 2 * py + pc) * m, m), :]

        def copy(k, block, to, src=None):
            return pltpu.make_async_remote_copy(
                src_ref=rows(*block) if src is None else src, dst_ref=rows(*block),
                send_sem=send_sems.at[k], recv_sem=recv_sems.at[k], device_id=to, device_id_type=MESH)

        mine = pltpu.make_async_copy(src_own, rows(*me), local_sem)
        mine.start()
        first = [copy(0, me, sibling, src=src_own)]
        first += [copy(1 + j, me, (*chip, c), src=src_own) for j, chip in enumerate(chips)]
        for cp in first:
            cp.start()
        passed = [copy(4 + j, (*chip, c), sibling) for j, chip in enumerate(chips)]
        for j, chip in enumerate(chips):
            copy(1 + j, (*chip, c), me).wait_recv()
            passed[j].start()
        copy(0, sibling, me).wait_recv()
        for j, chip in enumerate(chips):
            copy(4 + j, (*chip, 1 - c), me).wait_recv()
        for cp in first + passed:
            cp.wait_send()
        mine.wait()

    return pl.pallas_call(
        body, name=name, out_shape=jax.ShapeDtypeStruct((8 * m, n), xs.dtype),
        in_specs=[pl.BlockSpec(memory_space=pl.ANY)], out_specs=pl.BlockSpec(memory_space=pl.ANY),
        scratch_shapes=[pltpu.SemaphoreType.DMA((7,)), pltpu.SemaphoreType.DMA((7,)), pltpu.SemaphoreType.DMA],
    )(xs)


def swap_halves(g, *, name):
    nq, r, cc = g.shape
    r2 = r // 2

    def body(g_ref, out_ref, send_sem, recv_sem):
        x, y, c = _place()
        cp = pltpu.make_async_remote_copy(
            src_ref=g_ref.at[:, pl.ds((1 - c) * r2, r2), :], dst_ref=out_ref,
            send_sem=send_sem, recv_sem=recv_sem, device_id=(x, y, 1 - c), device_id_type=MESH)
        cp.start()
        cp.wait()

    return pl.pallas_call(
        body, name=name, out_shape=jax.ShapeDtypeStruct((nq, r2, cc), g.dtype),
        in_specs=[pl.BlockSpec(memory_space=pl.ANY)], out_specs=pl.BlockSpec(memory_space=pl.ANY),
        scratch_shapes=[pltpu.SemaphoreType.DMA, pltpu.SemaphoreType.DMA],
    )(g)


def exchange_chips(p, *, name):
    nq, r2, cc = p.shape

    def body(p_ref, out_ref, send_sems, recv_sems, local_sem):
        x, y, c = _place()
        q0 = 2 * x + y
        chips = [(1 - x, y), (x, 1 - y), (1 - x, 1 - y)]
        mine = pltpu.make_async_copy(p_ref.at[q0], out_ref.at[q0], local_sem)
        mine.start()
        sends = []
        for j, (cx, cy) in enumerate(chips):
            sends.append(pltpu.make_async_remote_copy(
                src_ref=p_ref.at[2 * cx + cy], dst_ref=out_ref.at[q0],
                send_sem=send_sems.at[j], recv_sem=recv_sems.at[j],
                device_id=(cx, cy, c), device_id_type=MESH))
        for cp in sends:
            cp.start()
        for j, (cx, cy) in enumerate(chips):
            qj = 2 * cx + cy
            pltpu.make_async_remote_copy(
                src_ref=p_ref.at[qj], dst_ref=out_ref.at[qj],
                send_sem=send_sems.at[j], recv_sem=recv_sems.at[j],
                device_id=(cx, cy, c), device_id_type=MESH).wait_recv()
        for cp in sends:
            cp.wait_send()
        mine.wait()

    return pl.pallas_call(
        body, name=name, out_shape=jax.ShapeDtypeStruct((nq, r2, cc), p.dtype),
        in_specs=[pl.BlockSpec(memory_space=pl.ANY)], out_specs=pl.BlockSpec(memory_space=pl.ANY),
        scratch_shapes=[pltpu.SemaphoreType.DMA((3,)), pltpu.SemaphoreType.DMA((3,)), pltpu.SemaphoreType.DMA],
    )(p)


def join_halves(tot, *, name):
    r2, cc = tot.shape[0] // 2, tot.shape[1]

    def body(t_ref, out_ref, send_sem, recv_sem):
        del t_ref
        x, y, c = _place()
        mine = out_ref.at[pl.ds(c * r2, r2), :]
        theirs = out_ref.at[pl.ds((1 - c) * r2, r2), :]
        cp = pltpu.make_async_remote_copy(
            src_ref=mine, dst_ref=mine, send_sem=send_sem, recv_sem=recv_sem,
            device_id=(x, y, 1 - c), device_id_type=MESH)
        cp.start()
        pltpu.make_async_remote_copy(
            src_ref=theirs, dst_ref=theirs, send_sem=send_sem, recv_sem=recv_sem,
            device_id=(x, y, 1 - c), device_id_type=MESH).wait_recv()
        cp.wait_send()

    return pl.pallas_call(
        body, name=name, out_shape=jax.ShapeDtypeStruct(tot.shape, tot.dtype),
        in_specs=[pl.BlockSpec(memory_space=pl.ANY)], out_specs=pl.BlockSpec(memory_space=pl.ANY),
        scratch_shapes=[pltpu.SemaphoreType.DMA, pltpu.SemaphoreType.DMA],
        input_output_aliases={0: 0},
    )(tot)


def _pack_local_shard(big, small_vec):
    parts = [w.reshape(-1, PACK_C).astype(BF) for w in big]
    srow = lax.bitcast_convert_type(small_vec, BF).reshape(1, PACK_C)
    parts.append(jnp.pad(srow, ((0, PACK_PAD - 1), (0, 0))))
    return jnp.concatenate(parts, axis=0)


def _split_rows(a, axis):
    out, off = [], 0
    for n in PACK_ROWS:
        out.append(lax.slice_in_dim(a, off, off + n, axis=axis))
        off += n
    return out, off


def _unpack_gathered(gw):
    (p_in, p_grp, p_out, m_in, m_qb, m_kvb, m_out), off = _split_rows(gw, 1)
    w = {}
    w["pool_w_in"] = p_in.reshape(4, D_MODEL, 1024).transpose(1, 0, 2).reshape(D_MODEL, 2 * POOL_WIDTH)
    w["pool_w_group"] = p_grp.reshape(4, 4, 128, POOL_GROUP).transpose(1, 0, 2, 3).reshape(4, POOL_GROUP, POOL_GROUP)
    w["pool_w_out"] = p_out.reshape(POOL_WIDTH, D_MODEL)
    win = m_in.reshape(4, D_MODEL, 688).transpose(1, 0, 2).reshape(D_MODEL, 2752)
    w["mla_w_in"] = jnp.concatenate(
        [win[:, 384:640], win[:, 640:704], jnp.zeros((D_MODEL, 64), BF), win[:, 0:384], win[:, 704:]], axis=1)
    wq = m_qb.reshape(4, Q_LORA, 768).transpose(1, 0, 2).reshape(Q_LORA, N_HEADS, QK_NOPE + QK_ROPE)
    w["mla_w_q_b"] = jnp.pad(wq, ((0, 0), (0, 0), (0, HEAD_PAD - QK_NOPE - QK_ROPE))).reshape(Q_LORA, N_HEADS * HEAD_PAD)
    w["mla_w_kv_b"] = m_kvb.reshape(4, KV_LORA, 1024).transpose(1, 0, 2).reshape(KV_LORA, 4096)
    w["mla_w_out"] = m_out.reshape(MLA_WIDTH, D_MODEL)
    small = lax.bitcast_convert_type(gw[:, off, :].reshape(4, 512, 2), F32)
    w["mla_norm"] = small[:, :256].reshape(1, D_MODEL)
    w["mla_q_norm"] = small[:, 256:352].reshape(1, Q_LORA)
    w["mla_kv_norm"] = small[:, 352:416].reshape(1, KV_LORA)
    return w


def _pack_grads(g):
    parts = [
        g["pool_w_in"].reshape(D_MODEL, 4, 1024).transpose(1, 0, 2),
        g["pool_w_group"].reshape(4, 4, 128, POOL_GROUP).transpose(1, 0, 2, 3).reshape(4, 256, PACK_C),
        g["pool_w_out"].reshape(4, 512, PACK_C),
        g["mla_w_in"].reshape(D_MODEL, 4, 688).transpose(1, 0, 2).reshape(4, 688, PACK_C),
        g["mla_w_q_b"].reshape(Q_LORA, 4, 768).transpose(1, 0, 2).reshape(4, 288, PACK_C),
        g["mla_w_kv_b"].reshape(KV_LORA, 4, 1024).transpose(1, 0, 2),
        g["mla_w_out"].reshape(4, 512, PACK_C),
        jnp.zeros((4, PACK_PAD, PACK_C), F32),
    ]
    return jnp.concatenate(parts, axis=1)


def kernel(x, positions, pool_norm, pool_w_in, pool_w_group, pool_scale, pool_w_out, mla_norm, mla_w_in, mla_q_norm, mla_w_q_b, mla_kv_norm, mla_w_kv_b, mla_w_out, final_norm, loss_target, m_pool_norm, m_pool_w_in, m_pool_w_group, m_pool_scale, m_pool_w_out, m_mla_norm, m_mla_w_in, m_mla_q_norm, m_mla_w_q_b, m_mla_kv_norm, m_mla_w_kv_b, m_mla_w_out, m_final_norm, v_pool_norm, v_pool_w_in, v_pool_w_group, v_pool_scale, v_pool_w_out, v_mla_norm, v_mla_w_in, v_mla_q_norm, v_mla_w_q_b, v_mla_kv_norm, v_mla_w_kv_b, v_mla_w_out, v_final_norm):
    s = x.shape[1]
    tq = min(512, s)
    x0 = x.reshape(s, D_MODEL)
    tgt = loss_target.reshape(s, D_MODEL)
    cx, cy, cc_idx = _place()
    chip = 2 * cx + cy

    big_names = ("pool_w_in", "pool_w_group", "pool_w_out", "mla_w_in", "mla_w_q_b", "mla_w_kv_b", "mla_w_out")
    big_w = dict(zip(big_names, (pool_w_in, pool_w_group, pool_w_out, mla_w_in, mla_w_q_b, mla_w_kv_b, mla_w_out)))
    big_m = dict(zip(big_names, (m_pool_w_in, m_pool_w_group, m_pool_w_out, m_mla_w_in, m_mla_w_q_b, m_mla_w_kv_b, m_mla_w_out)))
    big_v = dict(zip(big_names, (v_pool_w_in, v_pool_w_group, v_pool_w_out, v_mla_w_in, v_mla_w_q_b, v_mla_w_kv_b, v_mla_w_out)))

    small_vec = jnp.concatenate([mla_norm.reshape(-1), mla_q_norm.reshape(-1), mla_kv_norm.reshape(-1),
                                 jnp.zeros((96,), F32)])
    packed = _pack_local_shard([big_w[n] for n in big_names], small_vec)
    gathered = all_gather8(packed, name="gather_weights", own_half=True).reshape(4, PACK_R, PACK_C)
    w = _unpack_gathered(gathered)
    g_pool = pool_norm.reshape(1, D_MODEL)
    g_final = final_norm.reshape(1, D_MODEL)
    sc_pool = pool_scale.reshape(1, POOL_WIDTH)

    inv_freq = 1.0 / (ROPE_THETA ** (jnp.arange(0, QK_ROPE, 2, dtype=F32) / QK_ROPE))
    ang = positions.reshape(s).astype(F32)[:, None] * inv_freq
    cos, sin = jnp.cos(ang), jnp.sin(ang)
    z32, z64, z96 = (jnp.zeros((s, n), F32) for n in (32, 64, 96))
    t_cc = jnp.concatenate([cos, cos, z64], axis=1)
    t_sa = jnp.concatenate([-sin, z96], axis=1)
    t_sb = jnp.concatenate([z32, sin, z64], axis=1)

    h0 = norm_fwd(x0, g_pool, col=0, width=D_MODEL, name="pool_norm_fwd")
    uz = mm_nn(h0, w["pool_w_in"], name="pool_in_proj", out_dtype=F32)
    pd = pool_prep(uz, name="pool_window")
    mm = gmm_nn(pd, w["pool_w_group"], name="pool_group_mix")
    y1 = pool_gate(mm, uz, sc_pool, name="pool_gate")
    x1 = mm_nn(y1, w["pool_w_out"], name="pool_out_proj", out_dtype=F32, add=x0)

    h1 = norm_fwd(x1, w["mla_norm"], col=0, width=D_MODEL, name="mla_norm_fwd")
    proj = mm_nn(h1, w["mla_w_in"], name="mla_in_proj", out_dtype=F32, tn=P_WIDTH // 2)
    qn = norm_fwd(proj, w["mla_q_norm"], col=P_Q, width=Q_LORA, name="mla_q_norm_fwd")
    kvn = norm_fwd(proj, w["mla_kv_norm"], col=P_KV, width=KV_LORA, name="mla_kv_norm_fwd")
    q_pre = mm_nn(qn, w["mla_w_q_b"], name="mla_q_proj", out_dtype=F32, tk=Q_LORA)
    kv = mm_nn(kvn, w["mla_w_kv_b"], name="mla_kv_proj", out_dtype=BF, tk=KV_LORA)
    qr = rope_q(q_pre, t_cc, t_sa, t_sb, name="mla_rope_q")
    kc = pack_k(kv, proj, t_cc, t_sa, t_sb, name="mla_pack_k")
    o, lse = attn_fwd(qr, kc, kv, name="mla_attn_fwd", tq=tq)
    y2 = mla_gate(o, proj, name="mla_gate")
    x2 = mm_nn(y2, w["mla_w_out"], name="mla_out_proj", out_dtype=F32, add=x1)

    dx2, d_final, loss_part = final_loss(x2, g_final, tgt, name="final_norm_loss")

    grads = {}
    dy2 = mm_nt(dx2, w["mla_w_out"], name="mla_out_proj_dx", out_dtype=F32)
    grads["mla_w_out"] = mm_tn(y2, dx2, name="mla_out_proj_dw")
    do, dz2, delta = mla_gate_bwd(dy2, o, proj, name="mla_gate_bwd", tq=tq)
    dkv, dkr, dqr = attn_bwd(qr, kc, kv, do, lse, delta, name="mla_attn_bwd", tq=tq)
    dq_pre = unrope_q(dqr, t_cc, t_sa, t_sb, name="mla_unrope_q")
    dkr_pre = unrope_k(dkr, t_cc, t_sa, t_sb, name="mla_unrope_k")
    dqn = mm_nt(dq_pre, w["mla_w_q_b"], name="mla_q_proj_dx", out_dtype=F32, tn=Q_LORA, tk=2048)
    g_qb = mm_tn(qn, dq_pre, name="mla_q_proj_dw", tm=Q_LORA, tn=2048)
    dkvn = mm_nt(dkv, w["mla_w_kv_b"], name="mla_kv_proj_dx", out_dtype=F32, tn=KV_LORA, tk=2048)
    grads["mla_w_kv_b"] = mm_tn(kvn, dkv, name="mla_kv_proj_dw", tm=KV_LORA, tn=2048)
    dq_lat, d_qnorm = norm_bwd(proj, w["mla_q_norm"], dqn, col=P_Q, width=Q_LORA, name="mla_q_norm_bwd", out_dtype=BF)
    dkv_lat, d_kvnorm = norm_bwd(proj, w["mla_kv_norm"], dkvn, col=P_KV, width=KV_LORA, name="mla_kv_norm_bwd", out_dtype=BF)
    dsmall = jnp.concatenate([dkv_lat, dkr_pre, dq_lat], axis=1)
    dh1 = mm_nt(dsmall, w["mla_w_in"], name="mla_in_proj_dx_a", out_dtype=F32, tk=P_SMALL)
    dh1 = mm_nt(dz2, w["mla_w_in"][:, P_Z:], name="mla_in_proj_dx_b", out_dtype=F32, add=dh1)
    g_in_a = mm_tn(h1, dsmall, name="mla_in_proj_dw_a", tn=P_SMALL)
    g_in_b = mm_tn(h1, dz2, name="mla_in_proj_dw_b", tn=2048)
    dx1, d_mnorm = norm_bwd(x1, w["mla_norm"], dh1, col=0, width=D_MODEL, name="mla_norm_bwd", res=dx2)

    dy1 = mm_nt(dx1, w["pool_w_out"], name="pool_out_proj_dx", out_dtype=F32)
    grads["pool_w_out"] = mm_tn(y1, dx1, name="pool_out_proj_dw")
    dmm, dz1, d_scale = pool_gate_bwd(dy1, mm, uz, sc_pool, name="pool_gate_bwd")
    dpd = gmm_nt(dmm, w["pool_w_group"], name="pool_group_mix_dx")
    grads["pool_w_group"] = gmm_tn(pd, dmm, 4, name="pool_group_mix_dw")
    du = pool_prep_bwd(dpd, name="pool_window_bwd")
    dh0 = mm_nt(du, w["pool_w_in"], name="pool_in_proj_dx_u", out_dtype=F32, tk=1024)
    dh0 = mm_nt(dz1, w["pool_w_in"], name="pool_in_proj_dx_z", out_dtype=F32, b_col=POOL_WIDTH, add=dh0, tk=1024)
    g_pin_u = mm_tn(h0, du, name="pool_in_proj_dw_u", tn=2048)
    g_pin_z = mm_tn(h0, dz1, name="pool_in_proj_dw_z", tn=2048)
    grad_x, d_pnorm = norm_bwd(x0, g_pool, dh0, col=0, width=D_MODEL, name="pool_norm_bwd", res=dx1)

    grads["pool_w_in"] = jnp.concatenate([g_pin_u, g_pin_z], axis=1)
    g_in = jnp.concatenate([g_in_a, g_in_b], axis=1)
    grads["mla_w_in"] = jnp.concatenate([g_in[:, P_Q:P_Z], g_in[:, P_KV:P_KV + KV_LORA],
                                         g_in[:, P_KR:P_KR + QK_ROPE], g_in[:, P_Z:]], axis=1)
    grads["mla_w_q_b"] = g_qb.reshape(Q_LORA, N_HEADS, HEAD_PAD)[:, :, :QK_NOPE + QK_ROPE].reshape(Q_LORA, -1)

    gp = _pack_grads(grads)
    half_rows = PACK_R // 2 // 3
    sib = swap_halves(gp, name="grad_swap_halves")
    core_idx = cc_idx.reshape(1).astype(jnp.int32)
    pre = add_halves(gp, sib, core_idx, name="grad_add_halves", rows=half_rows)
    got = exchange_chips(pre, name="grad_exchange_chips")
    tot = sum_chips(got, core_idx, name="grad_sum_chips", rows=half_rows)
    red = join_halves(tot, name="grad_join_halves")
    red_parts, _ = _split_rows(red, 0)

    sv = jnp.concatenate([d_pnorm.reshape(-1), d_scale.reshape(-1), d_final.reshape(-1), d_mnorm.reshape(-1),
                          d_qnorm.reshape(-1), d_kvnorm.reshape(-1), loss_part[0, :1],
                          jnp.zeros((SV_ROWS * SV_COLS - SV_OFF["loss"] - 1,), F32)]).reshape(SV_ROWS, SV_COLS)
    sv_all = all_gather8(sv, name="gather_small_grads", own_half=False)
    sv_sum = sum_devices(sv_all, name="sum_small_grads").reshape(-1)
    loss = sv_sum[SV_OFF["loss"]]

    def sv_take(key, n):
        return lax.slice_in_dim(sv_sum, SV_OFF[key], SV_OFF[key] + n)

    out_g, out_d, out_m, out_v = {}, {}, {}, {}
    for name, part in zip(big_names, red_parts):
        shp = big_w[name].shape
        g2 = part.reshape(shp)
        two_d = (-1, shp[-1])
        d_, m_, v_ = adamw(big_w[name].reshape(two_d), g2.reshape(two_d), big_m[name].reshape(two_d),
                           big_v[name].reshape(two_d), name="adamw_" + name)
        out_g[name], out_d[name], out_m[name], out_v[name] = g2, d_.reshape(shp), m_.reshape(shp), v_.reshape(shp)

    small = [
        ("pool_norm", pool_norm, m_pool_norm, v_pool_norm, sv_take("pool_norm", 1024)),
        ("pool_scale", pool_scale, m_pool_scale, v_pool_scale, sv_take("pool_scale", 2048)),
        ("final_norm", final_norm, m_final_norm, v_final_norm, sv_take("final_norm", 1024)),
        ("mla_norm", mla_norm, m_mla_norm, v_mla_norm,
         lax.dynamic_slice_in_dim(sv_take("mla_norm", 1024), chip * 256, 256)),
        ("mla_q_norm", mla_q_norm, m_mla_q_norm, v_mla_q_norm,
         lax.dynamic_slice_in_dim(sv_take("q_norm", 384), chip * 96, 96)),
        ("mla_kv_norm", mla_kv_norm, m_mla_kv_norm, v_mla_kv_norm,
         lax.dynamic_slice_in_dim(sv_take("kv_norm", 256), chip * 64, 64)),
    ]
    sw = jnp.concatenate([t[1].reshape(-1) for t in small] + [jnp.zeros((96,), F32)]).reshape(1, -1)
    sm = jnp.concatenate([t[2].reshape(-1) for t in small] + [jnp.zeros((96,), F32)]).reshape(1, -1)
    s_v = jnp.concatenate([t[3].reshape(-1) for t in small] + [jnp.ones((96,), F32)]).reshape(1, -1)
    sg = jnp.concatenate([t[4].reshape(-1) for t in small] + [jnp.zeros((96,), F32)]).reshape(1, -1)
    sd_, sm_, sv_ = adamw(sw, sg, sm, s_v, name="adamw_vectors")
    off = 0
    for name, wt, _, _, gvec in small:
        n = gvec.shape[0]
        shp = wt.shape
        out_g[name] = gvec.reshape(shp)
        out_d[name] = sd_[0, off:off + n].reshape(shp)
        out_m[name] = sm_[0, off:off + n].reshape(shp)
        out_v[name] = sv_[0, off:off + n].reshape(shp)
        off += n

    order = ("pool_norm", "pool_w_in", "pool_w_group", "pool_scale", "pool_w_out", "mla_norm", "mla_w_in",
             "mla_q_norm", "mla_w_q_b", "mla_kv_norm", "mla_w_kv_b", "mla_w_out", "final_norm")
    return (loss, grad_x.reshape(x.shape), *[out_g[n] for n in order], *[out_d[n] for n in order],
            *[out_m[n] for n in order], *[out_v[n] for n in order])
```

```python
import functools

import jax
import jax.numpy as jnp
from jax import lax
from jax.experimental import pallas as pl
from jax.experimental.pallas import tpu as pltpu

F32 = jnp.float32
BF = jnp.bfloat16
MESH = pl.DeviceIdType.MESH

D_MODEL = 1024
POOL_WIDTH = 2048
POOL_WINDOWS = (2, 4, 8, 16)
POOL_GROUP = 512
HALO = 16
N_HEADS = 16
QK_NOPE = 128
QK_ROPE = 64
V_DIM = 128
HEAD_PAD = 256
Q_LORA = 384
KV_LORA = 256
MLA_WIDTH = 2048
ROPE_THETA = 10000.0
EPS = 1e-6
SCALE = (QK_NOPE + QK_ROPE) ** -0.5
SCALE_LOG2E = SCALE * 1.4426950408889634
NEG = -1e30

P_KV, P_KR, P_Q, P_Z = 0, 256, 384, 768
P_SMALL = 768
P_WIDTH = 2816

ADAM_LR = 0.001
ADAM_B1 = 0.9
ADAM_B2 = 0.999
ADAM_EPS = 1e-08
ADAM_WD = 0.01
ADAM_STEP = 10

NN = (((1,), (0,)), ((), ()))
NT = (((1,), (1,)), ((), ()))
TN = (((0,), (0,)), ((), ()))

PACK_ROWS = (1024, 256, 512, 688, 288, 256, 512)
PACK_PAD = 16
PACK_R = sum(PACK_ROWS) + PACK_PAD
PACK_C = 1024
SV_OFF = dict(pool_norm=0, pool_scale=1024, final_norm=3072, mla_norm=4096, q_norm=5120, kv_norm=5504, loss=5760)
SV_ROWS, SV_COLS = 8, 768

VMEM_LIMIT = 56 * 1024 * 1024


def _params(n_axes, vmem=None):
    return pltpu.CompilerParams(dimension_semantics=("arbitrary",) * n_axes,
                                vmem_limit_bytes=VMEM_LIMIT if vmem is None else vmem)


def _sigmoid(z):
    return 1.0 / (1.0 + jnp.exp(-z))


def _mm(a, b, *, dims, grid, a_spec, b_spec, o_spec, out_shape, out_dtype, acc_shape, name,
        add=None, add_spec=None):
    nk = grid[-1]
    kax = len(grid) - 1

    def body(*refs):
        if add is None:
            a_ref, b_ref, o_ref = refs[:3]
            add_ref = None
            rest = refs[3:]
        else:
            a_ref, b_ref, add_ref, o_ref = refs[:4]
            rest = refs[4:]
        part = lax.dot_general(a_ref[...].astype(BF), b_ref[...].astype(BF), dims,
                               preferred_element_type=F32)

        def finish(r):
            if add_ref is not None:
                r = r + add_ref[...]
            o_ref[...] = r.astype(o_ref.dtype)

        if nk == 1:
            finish(part)
        else:
            acc = rest[0]
            k = pl.program_id(kax)

            @pl.when(k == 0)
            def _():
                acc[...] = part

            @pl.when(k > 0)
            def _():
                acc[...] += part

            @pl.when(k == nk - 1)
            def _():
                finish(acc[...])

    in_specs = [a_spec, b_spec]
    args = [a, b]
    if add is not None:
        in_specs.append(add_spec)
        args.append(add)
    scratch = [] if nk == 1 else [pltpu.VMEM(acc_shape, F32)]
    return pl.pallas_call(
        body, name=name, grid=grid, in_specs=in_specs, out_specs=o_spec,
        out_shape=jax.ShapeDtypeStruct(out_shape, out_dtype), scratch_shapes=scratch,
        compiler_params=_params(len(grid)))(*args)


def _pick(n, t):
    t = min(n, t)
    assert n % t == 0, (n, t)
    return t


def mm_nn(a, b, *, name, out_dtype, a_col=0, k_size=None, add=None, tm=1024, tn=1024, tk=1024):
    m = a.shape[0]
    kk, n = b.shape
    assert k_size is None or k_size == kk
    tm, tn, tk = _pick(m, tm), _pick(n, tn), _pick(kk, tk)
    assert a_col % tk == 0
    ko = a_col // tk
    return _mm(a, b, dims=NN, grid=(m // tm, n // tn, kk // tk),
               a_spec=pl.BlockSpec((tm, tk), lambda i, j, k: (i, ko + k)),
               b_spec=pl.BlockSpec((tk, tn), lambda i, j, k: (k, j)),
               o_spec=pl.BlockSpec((tm, tn), lambda i, j, k: (i, j)),
               add=add, add_spec=pl.BlockSpec((tm, tn), lambda i, j, k: (i, j)),
               out_shape=(m, n), out_dtype=out_dtype, acc_shape=(tm, tn), name=name)


def mm_nt(a, b, *, name, out_dtype, b_col=0, add=None, tm=1024, tn=1024, tk=1024):
    m, kk = a.shape
    n = b.shape[0]
    tm, tn, tk = _pick(m, tm), _pick(n, tn), _pick(kk, tk)
    assert b_col % tk == 0
    ko = b_col // tk
    return _mm(a, b, dims=NT, grid=(m // tm, n // tn, kk // tk),
               a_spec=pl.BlockSpec((tm, tk), lambda i, j, k: (i, k)),
               b_spec=pl.BlockSpec((tn, tk), lambda i, j, k: (j, ko + k)),
               o_spec=pl.BlockSpec((tm, tn), lambda i, j, k: (i, j)),
               add=add, add_spec=pl.BlockSpec((tm, tn), lambda i, j, k: (i, j)),
               out_shape=(m, n), out_dtype=out_dtype, acc_shape=(tm, tn), name=name)


def mm_tn(a, b, *, name, a_col=0, m_size=None, b_col=0, n_size=None, tm=1024, tn=1024, tk=1024):
    s = a.shape[0]
    m = a.shape[1] if m_size is None else m_size
    n = b.shape[1] if n_size is None else n_size
    tm, tn, tk = _pick(m, tm), _pick(n, tn), _pick(s, tk)
    assert a_col % tm == 0 and b_col % tn == 0
    ao, bo = a_col // tm, b_col // tn
    return _mm(a, b, dims=TN, grid=(m // tm, n // tn, s // tk),
               a_spec=pl.BlockSpec((tk, tm), lambda i, j, k: (k, ao + i)),
               b_spec=pl.BlockSpec((tk, tn), lambda i, j, k: (k, bo + j)),
               o_spec=pl.BlockSpec((tm, tn), lambda i, j, k: (i, j)),
               out_shape=(m, n), out_dtype=F32, acc_shape=(tm, tn), name=name)


def gmm_nt(a, w, *, name, tm=1024):
    s = a.shape[0]
    g, kk, n = w.shape
    tm = _pick(s, tm)
    return _mm(a, w, dims=NT, grid=(s // tm, g, 1),
               a_spec=pl.BlockSpec((tm, n), lambda i, gi, k: (i, gi)),
               b_spec=pl.BlockSpec((None, kk, n), lambda i, gi, k: (gi, 0, 0)),
               o_spec=pl.BlockSpec((tm, kk), lambda i, gi, k: (i, gi)),
               out_shape=(s, g * kk), out_dtype=F32, acc_shape=(tm, kk), name=name)


def gmm_tn(a, b, g, *, name, tk=1024):
    s = a.shape[0]
    kk, n = a.shape[1] // g, b.shape[1] // g
    tk = _pick(s, tk)
    return _mm(a, b, dims=TN, grid=(g, s // tk),
               a_spec=pl.BlockSpec((tk, kk), lambda gi, k: (k, gi)),
               b_spec=pl.BlockSpec((tk, n), lambda gi, k: (k, gi)),
               o_spec=pl.BlockSpec((None, kk, n), lambda gi, k: (gi, 0, 0)),
               out_shape=(g, kk, n), out_dtype=F32, acc_shape=(kk, n), name=name)


def norm_fwd(x, g, *, col, width, name, t=512):
    s = x.shape[0]
    t = _pick(s, t)
    cb = col // width
    assert col % width == 0

    def body(x_ref, g_ref, o_ref):
        xv = x_ref[...]
        inv = lax.rsqrt(jnp.mean(xv * xv, axis=-1, keepdims=True) + EPS)
        o_ref[...] = ((xv * inv) * g_ref[...]).astype(o_ref.dtype)

    return pl.pallas_call(
        body, name=name, grid=(s // t,),
        in_specs=[pl.BlockSpec((t, width), lambda i: (i, cb)), pl.BlockSpec((1, width), lambda i: (0, 0))],
        out_specs=pl.BlockSpec((t, width), lambda i: (i, 0)),
        out_shape=jax.ShapeDtypeStruct((s, width), BF), compiler_params=_params(1))(x, g)


def norm_bwd(x, g, dh, *, col, width, name, res=None, out_dtype=F32, t=512):
    s = x.shape[0]
    t = _pick(s, t)
    cb = col // width
    assert col % width == 0

    def body(*refs):
        if res is None:
            x_ref, g_ref, dh_ref, dx_ref, dg_ref = refs
        else:
            x_ref, g_ref, dh_ref, res_ref, dx_ref, dg_ref = refs
        xv = x_ref[...]
        inv = lax.rsqrt(jnp.mean(xv * xv, axis=-1, keepdims=True) + EPS)
        xhat = xv * inv
        dh_v = dh_ref[...]
        part = jnp.sum(dh_v * xhat, axis=0, keepdims=True)

        @pl.when(pl.program_id(0) == 0)
        def _():
            dg_ref[...] = part

        @pl.when(pl.program_id(0) > 0)
        def _():
            dg_ref[...] += part

        dxhat = dh_v * g_ref[...]
        dx = inv * (dxhat - xhat * jnp.mean(dxhat * xhat, axis=-1, keepdims=True))
        if res is not None:
            dx = dx + res_ref[...]
        dx_ref[...] = dx.astype(dx_ref.dtype)

    row = pl.BlockSpec((t, width), lambda i: (i, 0))
    vec = pl.BlockSpec((1, width), lambda i: (0, 0))
    in_specs = [pl.BlockSpec((t, width), lambda i: (i, cb)), vec, row]
    args = [x, g, dh]
    if res is not None:
        in_specs.append(row)
        args.append(res)
    return pl.pallas_call(
        body, name=name, grid=(s // t,), in_specs=in_specs, out_specs=[row, vec],
        out_shape=[jax.ShapeDtypeStruct((s, width), out_dtype), jax.ShapeDtypeStruct((1, width), F32)],
        compiler_params=_params(1))(*args)


def final_loss(x2, gf, tgt, *, name, t=512):
    s, d = x2.shape
    t = _pick(s, t)

    def body(x_ref, g_ref, t_ref, dx_ref, dg_ref, loss_ref):
        xv = x_ref[...]
        inv = lax.rsqrt(jnp.mean(xv * xv, axis=-1, keepdims=True) + EPS)
        xhat = xv * inv
        gv = g_ref[...]
        diff = xhat * gv - t_ref[...]
        row_err = jnp.mean(diff * diff, axis=-1, keepdims=True)
        lpart = jnp.broadcast_to(0.5 * jnp.sum(row_err, axis=0, keepdims=True), (1, 128))
        dout = diff * (1.0 / d)
        gpart = jnp.sum(dout * xhat, axis=0, keepdims=True)

        @pl.when(pl.program_id(0) == 0)
        def _():
            dg_ref[...] = gpart
            loss_ref[...] = lpart

        @pl.when(pl.program_id(0) > 0)
        def _():
            dg_ref[...] += gpart
            loss_ref[...] += lpart

        dxhat = dout * gv
        dx_ref[...] = inv * (dxhat - xhat * jnp.mean(dxhat * xhat, axis=-1, keepdims=True))

    row = pl.BlockSpec((t, d), lambda i: (i, 0))
    vec = pl.BlockSpec((1, d), lambda i: (0, 0))
    return pl.pallas_call(
        body, name=name, grid=(s // t,), in_specs=[row, vec, row],
        out_specs=[row, vec, pl.BlockSpec((1, 128), lambda i: (0, 0))],
        out_shape=[jax.ShapeDtypeStruct((s, d), F32), jax.ShapeDtypeStruct((1, d), F32),
                   jax.ShapeDtypeStruct((1, 128), F32)],
        compiler_params=_params(1))(x2, gf, tgt)


def pool_prep(uz, *, name, t=256):
    s = uz.shape[0]
    t = _pick(s, t)
    hb = t // HALO

    def body(u_ref, halo_ref, o_ref, buf):
        i = pl.program_id(0)
        buf[pl.ds(HALO, t), :] = u_ref[...]

        @pl.when(i == 0)
        def _():
            buf[pl.ds(0, HALO), :] = jnp.zeros((HALO, POOL_WIDTH), F32)

        @pl.when(i > 0)
        def _():
            buf[pl.ds(0, HALO), :] = halo_ref[...]

        pos = i * t + lax.broadcasted_iota(jnp.int32, (t, POOL_GROUP), 0)
        for g, w in enumerate(POOL_WINDOWS):
            cols = pl.ds(g * POOL_GROUP, POOL_GROUP)
            cur = buf[pl.ds(HALO, t), cols]
            acc = cur
            for k in range(1, w):
                acc = acc + buf[pl.ds(HALO - k, t), cols]
            cnt = jnp.minimum(pos + 1, w).astype(F32)
            o_ref[:, cols] = (acc / cnt - cur).astype(o_ref.dtype)

    return pl.pallas_call(
        body, name=name, grid=(s // t,),
        in_specs=[pl.BlockSpec((t, POOL_WIDTH), lambda i: (i, 0)),
                  pl.BlockSpec((HALO, POOL_WIDTH), lambda i: (jnp.maximum(i * hb - 1, 0), 0))],
        out_specs=pl.BlockSpec((t, POOL_WIDTH), lambda i: (i, 0)),
        out_shape=jax.ShapeDtypeStruct((s, POOL_WIDTH), BF),
        scratch_shapes=[pltpu.VMEM((t + HALO, POOL_WIDTH), F32)],
        compiler_params=_params(1))(uz, uz)


def pool_prep_bwd(dpd, *, name, t=256):
    s = dpd.shape[0]
    t = _pick(s, t)
    hb = t // HALO
    n = s // t

    def body(d_ref, halo_ref, o_ref, buf):
        i = pl.program_id(0)
        pos = i * t + lax.broadcasted_iota(jnp.int32, (t, POOL_GROUP), 0)
        for g, w in enumerate(POOL_WINDOWS):
            cols = pl.ds(g * POOL_GROUP, POOL_GROUP)
            cnt = jnp.minimum(pos + 1, w).astype(F32)
            buf[pl.ds(0, t), cols] = d_ref[:, cols] / cnt

            @pl.when(i < n - 1)
            def _():
                buf[pl.ds(t, HALO), cols] = halo_ref[:, cols] / float(w)

            @pl.when(i == n - 1)
            def _():
                buf[pl.ds(t, HALO), cols] = jnp.zeros((HALO, POOL_GROUP), F32)

        for g, w in enumerate(POOL_WINDOWS):
            cols = pl.ds(g * POOL_GROUP, POOL_GROUP)
            acc = buf[pl.ds(0, t), cols]
            for k in range(1, w):
                acc = acc + buf[pl.ds(k, t), cols]
            o_ref[:, cols] = (acc - d_ref[:, cols]).astype(o_ref.dtype)

    return pl.pallas_call(
        body, name=name, grid=(n,),
        in_specs=[pl.BlockSpec((t, POOL_WIDTH), lambda i: (i, 0)),
                  pl.BlockSpec((HALO, POOL_WIDTH), lambda i: (jnp.minimum((i + 1) * hb, n * hb - 1), 0))],
        out_specs=pl.BlockSpec((t, POOL_WIDTH), lambda i: (i, 0)),
        out_shape=jax.ShapeDtypeStruct((s, POOL_WIDTH), BF),
        scratch_shapes=[pltpu.VMEM((t + HALO, POOL_WIDTH), F32)],
        compiler_params=_params(1))(dpd, dpd)


CHUNK = 512


def _chunks(width, step=CHUNK):
    return [slice(c, c + step) for c in range(0, width, step)]


def pool_mix_gate(pd, wg, uz, scale, *, name, tm=1024):
    s = pd.shape[0]
    g = wg.shape[0]
    tm = _pick(s, tm)

    def body(a_ref, w_ref, z_ref, sc_ref, mm_ref, y_ref):
        mm = jnp.dot(a_ref[...], w_ref[...], preferred_element_type=F32)
        mm_ref[...] = mm
        z = z_ref[...]
        y_ref[...] = ((mm * sc_ref[...]) * (z * _sigmoid(z))).astype(y_ref.dtype)

    blk = pl.BlockSpec((tm, POOL_GROUP), lambda i, gi: (i, gi))
    return pl.pallas_call(
        body, name=name, grid=(s // tm, g),
        in_specs=[blk, pl.BlockSpec((None, POOL_GROUP, POOL_GROUP), lambda i, gi: (gi, 0, 0)),
                  pl.BlockSpec((tm, POOL_GROUP), lambda i, gi: (i, g + gi)),
                  pl.BlockSpec((1, POOL_GROUP), lambda i, gi: (0, gi))],
        out_specs=[blk, blk],
        out_shape=[jax.ShapeDtypeStruct((s, POOL_WIDTH), F32), jax.ShapeDtypeStruct((s, POOL_WIDTH), BF)],
        compiler_params=_params(2))(pd, wg, uz, scale)


def pool_out_dx_gate(dx, w_out, mm, uz, scale, *, name, tm=512):
    s, d = dx.shape
    tm = _pick(s, tm)

    def body(dx_ref, w_ref, mm_ref, z_ref, sc_ref, dmm_ref, dz_ref, dsc_ref):
        dxv = dx_ref[...].astype(BF)
        parts = []
        for c in _chunks(POOL_WIDTH):
            dyv = lax.dot_general(dxv, w_ref[c, :], NT, preferred_element_type=F32)
            z = z_ref[:, c]
            sig = _sigmoid(z)
            mmv = mm_ref[:, c]
            scv = sc_ref[:, c]
            dmixed = dyv * (z * sig)
            dmm_ref[:, c] = (dmixed * scv).astype(dmm_ref.dtype)
            dz_ref[:, c] = (dyv * (mmv * scv) * (sig * (1.0 + z * (1.0 - sig)))).astype(dz_ref.dtype)
            parts.append(jnp.sum(dmixed * mmv, axis=0, keepdims=True))

        @pl.when(pl.program_id(0) == 0)
        def _():
            for c, part in zip(_chunks(POOL_WIDTH), parts):
                dsc_ref[:, c] = part

        @pl.when(pl.program_id(0) > 0)
        def _():
            for c, part in zip(_chunks(POOL_WIDTH), parts):
                dsc_ref[:, c] += part

    blk = pl.BlockSpec((tm, POOL_WIDTH), lambda i: (i, 0))
    vec = pl.BlockSpec((1, POOL_WIDTH), lambda i: (0, 0))
    return pl.pallas_call(
        body, name=name, grid=(s // tm,),
        in_specs=[pl.BlockSpec((tm, d), lambda i: (i, 0)), pl.BlockSpec((POOL_WIDTH, d), lambda i: (0, 0)),
                  blk, pl.BlockSpec((tm, POOL_WIDTH), lambda i: (i, 1)), vec],
        out_specs=[blk, blk, vec],
        out_shape=[jax.ShapeDtypeStruct((s, POOL_WIDTH), BF), jax.ShapeDtypeStruct((s, POOL_WIDTH), BF),
                   jax.ShapeDtypeStruct((1, POOL_WIDTH), F32)],
        compiler_params=_params(1))(dx, w_out, mm, uz, scale)


def _rope(a, cc, sa, sb):
    return a * cc + pltpu.roll(a, 96, 1) * sa + pltpu.roll(a, 32, 1) * sb


def _unrope(d, cc, sa, sb):
    return d * cc + pltpu.roll(d * sa, 32, 1) + pltpu.roll(d * sb, 96, 1)


def q_proj_rope(qn, wq, cc, sa, sb, *, name, tm=1024, heads=4):
    s, kk = qn.shape
    tm = _pick(s, tm)
    tn = heads * HEAD_PAD

    def body(a_ref, b_ref, cc_ref, sa_ref, sb_ref, o_ref):
        q = jnp.dot(a_ref[...], b_ref[...], preferred_element_type=F32)
        for h in range(heads):
            nope = slice(h * HEAD_PAD, h * HEAD_PAD + QK_NOPE)
            rope = slice(h * HEAD_PAD + QK_NOPE, (h + 1) * HEAD_PAD)
            o_ref[:, nope] = q[:, nope].astype(o_ref.dtype)
            o_ref[:, rope] = _rope(q[:, rope], cc_ref[...], sa_ref[...], sb_ref[...]).astype(o_ref.dtype)

    tab = pl.BlockSpec((tm, 128), lambda i, j: (i, 0))
    return pl.pallas_call(
        body, name=name, grid=(s // tm, N_HEADS // heads),
        in_specs=[pl.BlockSpec((tm, kk), lambda i, j: (i, 0)), pl.BlockSpec((kk, tn), lambda i, j: (0, j)),
                  tab, tab, tab],
        out_specs=pl.BlockSpec((tm, tn), lambda i, j: (i, j)),
        out_shape=jax.ShapeDtypeStruct((s, N_HEADS * HEAD_PAD), BF), compiler_params=_params(2))(qn, wq, cc, sa, sb)


def rope_k(proj, cc, sa, sb, *, name, t=512):
    s = proj.shape[0]
    t = _pick(s, t)
    kr_blk = P_KR // 128

    def body(kr_ref, cc_ref, sa_ref, sb_ref, o_ref):
        o_ref[...] = _rope(kr_ref[...], cc_ref[...], sa_ref[...], sb_ref[...]).astype(o_ref.dtype)

    tab = pl.BlockSpec((t, 128), lambda i: (i, 0))
    return pl.pallas_call(
        body, name=name, grid=(s // t,),
        in_specs=[pl.BlockSpec((t, 128), lambda i: (i, kr_blk)), tab, tab, tab], out_specs=tab,
        out_shape=jax.ShapeDtypeStruct((s, 128), BF), compiler_params=_params(1))(proj, cc, sa, sb)


def unrope_k(dkr, cc, sa, sb, *, name, t=512):
    s = dkr.shape[0]
    t = _pick(s, t)

    def body(d_ref, cc_ref, sa_ref, sb_ref, o_ref):
        o_ref[...] = _unrope(d_ref[...], cc_ref[...], sa_ref[...], sb_ref[...]).astype(o_ref.dtype)

    tab = pl.BlockSpec((t, 128), lambda i: (i, 0))
    return pl.pallas_call(
        body, name=name, grid=(s // t,), in_specs=[tab, tab, tab, tab], out_specs=tab,
        out_shape=jax.ShapeDtypeStruct((s, 128), BF), compiler_params=_params(1))(dkr, cc, sa, sb)


def mla_out_dx_gate(dx, w_out, o, proj, *, name, tq):
    s, d = dx.shape
    nq = s // tq

    def body(dx_ref, w_ref, o_ref, p_ref, do_ref, dz_ref, dl_ref):
        dxv = dx_ref[...].astype(BF)
        for c in _chunks(MLA_WIDTH):
            dy_c = lax.dot_general(dxv, w_ref[c, :], NT, preferred_element_type=F32)
            for h in range(c.start // V_DIM, c.stop // V_DIM):
                hc = slice(h * V_DIM, (h + 1) * V_DIM)
                z = p_ref[:, slice(P_Z + hc.start, P_Z + hc.stop)]
                sig = _sigmoid(z)
                dyv = dy_c[:, hc.start - c.start:hc.stop - c.start]
                ov = o_ref[:, hc]
                dov = dyv * (z * sig)
                do_ref[:, hc] = dov.astype(do_ref.dtype)
                dz_ref[:, hc] = (dyv * ov * (sig * (1.0 + z * (1.0 - sig)))).astype(dz_ref.dtype)
                delta = jnp.sum(dov * ov, axis=-1, keepdims=True)
                dl_ref[h] = jnp.broadcast_to(delta, (tq, 128)).T[:8, :]

    blk = pl.BlockSpec((tq, MLA_WIDTH), lambda i: (i, 0))
    return pl.pallas_call(
        body, name=name, grid=(nq,),
        in_specs=[pl.BlockSpec((tq, d), lambda i: (i, 0)), pl.BlockSpec((MLA_WIDTH, d), lambda i: (0, 0)),
                  blk, pl.BlockSpec((tq, P_WIDTH), lambda i: (i, 0))],
        out_specs=[blk, blk, pl.BlockSpec((N_HEADS, None, 8, tq), lambda i: (0, i, 0, 0))],
        out_shape=[jax.ShapeDtypeStruct((s, MLA_WIDTH), BF), jax.ShapeDtypeStruct((s, MLA_WIDTH), BF),
                   jax.ShapeDtypeStruct((N_HEADS, nq, 8, tq), F32)],
        compiler_params=_params(1))(dx, w_out, o, proj)


FWD_GROUPS = (4, 2, 1)
BWD_GROUPS = (4, 2, 1)


def _for_groups(first, count, groups, fn):
    lead = groups[-1]
    for g in groups[:-1][::-1]:
        lead = jnp.where(count >= g, g, lead)
    for g in groups:
        @pl.when(lead == g)
        def _(g=g):
            fn(first, g, True)
    first = first + lead
    count = count - lead
    for g in groups:
        n = count // g

        def one(p, carry, g=g, first=first):
            fn(first + p * g, g, False)
            return carry

        lax.fori_loop(0, n, one, 0)
        first = first + n * g
        count = count - n * g


def attn_fwd(qr, kv, krr, proj, *, name, tq):
    s = qr.shape[0]
    nq = s // tq
    z_blk = P_Z // V_DIM

    def body(kn_ref, v_ref, kr_ref, q_ref, z_ref, o_ref, y_ref, lse_ref, acc_sc, m_sc):
        j = pl.program_id(1)

        @pl.when(j == 0)
        def _():
            acc_sc[...] = jnp.zeros((s, 2 * V_DIM), F32)
            m_sc[...] = jnp.full((nq, 8, tq), NEG, F32)

        k = jnp.concatenate([kn_ref[...], kr_ref[...]], axis=1)
        vx = jnp.concatenate([v_ref[...], jnp.ones((tq, V_DIM), BF)], axis=1)

        def update(i, n_tiles, masked):
            rows = pl.ds(pl.multiple_of(i * tq, tq), n_tiles * tq)
            st = lax.dot_general(k, q_ref[rows, :], NT, preferred_element_type=F32) * SCALE_LOG2E
            if masked:
                krow = lax.broadcasted_iota(jnp.int32, (tq, n_tiles * tq), 0)
                qcol = lax.broadcasted_iota(jnp.int32, (tq, n_tiles * tq), 1)
                st = jnp.where(qcol >= krow, st, NEG)
            m_prev = jnp.concatenate([m_sc[i + n, pl.ds(0, 1), :] for n in range(n_tiles)], axis=1)
            m_new = jnp.maximum(m_prev, jnp.max(st, axis=0, keepdims=True))
            alpha_c = jnp.broadcast_to(jnp.exp2(m_prev - m_new), (128, n_tiles * tq)).T
            pt = jnp.exp2(st - m_new).astype(BF)
            pv = lax.dot_general(pt, vx, TN, preferred_element_type=F32)
            for cols in (slice(0, V_DIM), slice(V_DIM, 2 * V_DIM)):
                acc_sc[rows, cols] = alpha_c * acc_sc[rows, cols] + pv[:, cols]
            for n in range(n_tiles):
                m_sc[i + n, pl.ds(0, 1), :] = m_new[:, n * tq:(n + 1) * tq]

        _for_groups(j, nq - j, FWD_GROUPS, update)
        mine = pl.ds(pl.multiple_of(j * tq, tq), tq)
        l = acc_sc[mine, V_DIM:]
        o = acc_sc[mine, :V_DIM] / l
        o_ref[...] = o
        z = z_ref[...]
        y_ref[...] = (o * (z * _sigmoid(z))).astype(y_ref.dtype)
        lse_ref[...] = jnp.broadcast_to(m_sc[j, pl.ds(0, 1), :], (8, tq)) + jnp.log2(l).T[:8, :]

    tile = pl.BlockSpec((tq, V_DIM), lambda h, j: (j, h))
    return pl.pallas_call(
        body, name=name, grid=(N_HEADS, nq),
        in_specs=[pl.BlockSpec((tq, QK_NOPE), lambda h, j: (j, 2 * h)),
                  pl.BlockSpec((tq, V_DIM), lambda h, j: (j, 2 * h + 1)),
                  pl.BlockSpec((tq, 128), lambda h, j: (j, 0)),
                  pl.BlockSpec((s, HEAD_PAD), lambda h, j: (0, h)),
                  pl.BlockSpec((tq, V_DIM), lambda h, j: (j, z_blk + h))],
        out_specs=[tile, tile, pl.BlockSpec((None, None, 8, tq), lambda h, j: (h, j, 0, 0))],
        out_shape=[jax.ShapeDtypeStruct((s, N_HEADS * V_DIM), F32),
                   jax.ShapeDtypeStruct((s, N_HEADS * V_DIM), BF),
                   jax.ShapeDtypeStruct((N_HEADS, nq, 8, tq), F32)],
        scratch_shapes=[pltpu.VMEM((s, 2 * V_DIM), F32), pltpu.VMEM((nq, 8, tq), F32)],
        compiler_params=_params(2))(kv, kv, krr, qr, proj)


def attn_bwd(qr, kv, krr, do, lse, delta, cc, sa, sb, *, name, tq):
    s = qr.shape[0]
    nq = s // tq

    def body(kn_ref, v_ref, kr_ref, q_ref, do_ref, lse_ref, dl_ref, cc_ref, sa_ref, sb_ref,
             dkv_ref, dkr_ref, dq_ref, dq_sc, dk_sc, dv_sc):
        h = pl.program_id(0)
        j = pl.program_id(1)

        @pl.when(j == 0)
        def _():
            dq_sc[...] = jnp.zeros((s, HEAD_PAD), F32)

        dk_sc[...] = jnp.zeros((tq, HEAD_PAD), F32)
        dv_sc[...] = jnp.zeros((tq, V_DIM), F32)
        k = jnp.concatenate([kn_ref[...], kr_ref[...]], axis=1)
        v = v_ref[...]

        def step(i, n_tiles, masked):
            r0 = pl.multiple_of(i * tq, tq)
            rows = pl.ds(r0, n_tiles * tq)
            q = q_ref[rows, :]
            dov = do_ref[rows, :]
            lse_row = jnp.concatenate([lse_ref[i + n, pl.ds(0, 1), :] for n in range(n_tiles)], axis=1)
            dl_row = jnp.concatenate([dl_ref[i + n, pl.ds(0, 1), :] for n in range(n_tiles)], axis=1)
            st = lax.dot_general(k, q, NT, preferred_element_type=F32) * SCALE_LOG2E
            if masked:
                krow = lax.broadcasted_iota(jnp.int32, (tq, n_tiles * tq), 0)
                qcol = lax.broadcasted_iota(jnp.int32, (tq, n_tiles * tq), 1)
                st = jnp.where(qcol >= krow, st, NEG)
            pt = jnp.exp2(st - lse_row)
            dpt = lax.dot_general(v, dov, NT, preferred_element_type=F32)
            dst = (pt * (dpt - dl_row)).astype(BF)
            dv_sc[...] += jnp.dot(pt.astype(BF), dov, preferred_element_type=F32)
            dk_sc[...] += jnp.dot(dst, q, preferred_element_type=F32)
            dq_sc[rows, :] += lax.dot_general(dst, k, TN, preferred_element_type=F32)

        _for_groups(j, nq - j, BWD_GROUPS, step)
        dkv_ref[:, :QK_NOPE] = (dk_sc[:, :QK_NOPE] * SCALE).astype(dkv_ref.dtype)
        dkv_ref[:, QK_NOPE:] = dv_sc[...].astype(dkv_ref.dtype)
        mine = pl.ds(pl.multiple_of(j * tq, tq), tq)
        dkr = dk_sc[:, QK_NOPE:] * SCALE

        @pl.when(h == 0)
        def _():
            dkr_ref[mine, :] = dkr

        @pl.when(h > 0)
        def _():
            dkr_ref[mine, :] += dkr

        dq_ref[:, :QK_NOPE] = (dq_sc[mine, :QK_NOPE] * SCALE).astype(dq_ref.dtype)
        dq_ref[:, QK_NOPE:] = _unrope(dq_sc[mine, QK_NOPE:] * SCALE, cc_ref[...], sa_ref[...],
                                      sb_ref[...]).astype(dq_ref.dtype)

    rows = pl.BlockSpec((None, nq, 8, tq), lambda h, j: (h, 0, 0, 0))
    tab = pl.BlockSpec((tq, 128), lambda h, j: (j, 0))
    return pl.pallas_call(
        body, name=name, grid=(N_HEADS, nq),
        in_specs=[pl.BlockSpec((tq, QK_NOPE), lambda h, j: (j, 2 * h)),
                  pl.BlockSpec((tq, V_DIM), lambda h, j: (j, 2 * h + 1)), tab,
                  pl.BlockSpec((s, HEAD_PAD), lambda h, j: (0, h)),
                  pl.BlockSpec((s, V_DIM), lambda h, j: (0, h)), rows, rows, tab, tab, tab],
        out_specs=[pl.BlockSpec((tq, 256), lambda h, j: (j, h)),
                   pl.BlockSpec((s, 128), lambda h, j: (0, 0)),
                   pl.BlockSpec((tq, HEAD_PAD), lambda h, j: (j, h))],
        out_shape=[jax.ShapeDtypeStruct((s, N_HEADS * 256), BF),
                   jax.ShapeDtypeStruct((s, 128), F32),
                   jax.ShapeDtypeStruct((s, N_HEADS * HEAD_PAD), BF)],
        scratch_shapes=[pltpu.VMEM((s, HEAD_PAD), F32), pltpu.VMEM((tq, HEAD_PAD), F32),
                        pltpu.VMEM((tq, V_DIM), F32)],
        compiler_params=_params(2))(kv, kv, krr, qr, do, lse, delta, cc, sa, sb)


def adamw(w, g, m, v, *, name, t=256):
    r, c = w.shape
    t = r if r % t else t
    c1 = 1.0 - ADAM_B1 ** ADAM_STEP
    c2 = 1.0 - ADAM_B2 ** ADAM_STEP

    def body(w_ref, g_ref, m_ref, v_ref, d_ref, nm_ref, nv_ref):
        gv = g_ref[...]
        nm = ADAM_B1 * m_ref[...] + (1.0 - ADAM_B1) * gv
        nv = ADAM_B2 * v_ref[...] + (1.0 - ADAM_B2) * (gv * gv)
        nm_ref[...] = nm
        nv_ref[...] = nv
        d_ref[...] = -ADAM_LR * ((nm / c1) / (jnp.sqrt(nv / c2) + ADAM_EPS) + ADAM_WD * w_ref[...])

    blk = pl.BlockSpec((t, c), lambda i: (i, 0))
    return pl.pallas_call(
        body, name=name, grid=(r // t,), in_specs=[blk] * 4, out_specs=[blk] * 3,
        out_shape=[jax.ShapeDtypeStruct((r, c), F32)] * 3, compiler_params=_params(1))(w, g, m, v)


def sum_devices(parts, *, name):
    def body(p_ref, o_ref):
        acc = p_ref[pl.ds(0, SV_ROWS), :]
        for d in range(1, 8):
            acc = acc + p_ref[pl.ds(d * SV_ROWS, SV_ROWS), :]
        o_ref[...] = acc

    return pl.pallas_call(body, name=name, out_shape=jax.ShapeDtypeStruct((SV_ROWS, SV_COLS), F32))(parts)


def add_halves(g, rb, c_idx, *, name, rows):
    nq, r2, cc = rb.shape
    nb = r2 // rows

    def body(c_ref, g_ref, r_ref, o_ref):
        o_ref[...] = (g_ref[...] + r_ref[...]).astype(o_ref.dtype)

    grid_spec = pltpu.PrefetchScalarGridSpec(
        num_scalar_prefetch=1, grid=(nq, nb),
        in_specs=[pl.BlockSpec((None, rows, cc), lambda q, i, c: (q, c[0] * nb + i, 0)),
                  pl.BlockSpec((None, rows, cc), lambda q, i, c: (q, i, 0))],
        out_specs=pl.BlockSpec((None, rows, cc), lambda q, i, c: (q, i, 0)))
    return pl.pallas_call(body, name=name, grid_spec=grid_spec,
                          out_shape=jax.ShapeDtypeStruct((nq, r2, cc), BF),
                          compiler_params=_params(2))(c_idx, g, rb)


def sum_chips(rc, c_idx, *, name, rows):
    nq, r2, cc = rc.shape
    nb = r2 // rows

    def body(c_ref, r_ref, o_ref):
        parts = [r_ref[q].astype(F32) for q in range(4)]
        o_ref[...] = ((parts[0] + parts[1]) + parts[2]) + parts[3]

    grid_spec = pltpu.PrefetchScalarGridSpec(
        num_scalar_prefetch=1, grid=(nb,),
        in_specs=[pl.BlockSpec((nq, rows, cc), lambda i, c: (0, i, 0))],
        out_specs=pl.BlockSpec((rows, cc), lambda i, c: (c[0] * nb + i, 0)))
    return pl.pallas_call(body, name=name, grid_spec=grid_spec,
                          out_shape=jax.ShapeDtypeStruct((2 * r2, cc), F32),
                          compiler_params=_params(1))(c_idx, rc)


def _place():
    return lax.axis_index("x"), lax.axis_index("y"), lax.axis_index("c")


def all_gather8(xs, *, name, own_half):
    m = xs.shape[0] // 2 if own_half else xs.shape[0]
    n = xs.shape[1]

    def body(x_ref, out_ref, send_sems, recv_sems, local_sem):
        x, y, c = _place()
        me, sibling = (x, y, c), (x, y, 1 - c)
        chips = [(1 - x, y), (x, 1 - y), (1 - x, 1 - y)]
        src_own = x_ref.at[pl.ds(c * m, m), :] if own_half else x_ref

        def rows(px, py, pc):
            return out_ref.at[pl.ds((4 * px + 2 * py + pc) * m, m), :]

        def copy(k, block, to, src=None):
            return pltpu.make_async_remote_copy(
                src_ref=rows(*block) if src is None else src, dst_ref=rows(*block),
                send_sem=send_sems.at[k], recv_sem=recv_sems.at[k], device_id=to, device_id_type=MESH)

        mine = pltpu.make_async_copy(src_own, rows(*me), local_sem)
        mine.start()
        first = [copy(0, me, sibling, src=src_own)]
        first += [copy(1 + j, me, (*chip, c), src=src_own) for j, chip in enumerate(chips)]
        for cp in first:
            cp.start()
        passed = [copy(4 + j, (*chip, c), sibling) for j, chip in enumerate(chips)]
        for j, chip in enumerate(chips):
            copy(1 + j, (*chip, c), me).wait_recv()
            passed[j].start()
        copy(0, sibling, me).wait_recv()
        for j, chip in enumerate(chips):
            copy(4 + j, (*chip, 1 - c), me).wait_recv()
        for cp in first + passed:
            cp.wait_send()
        mine.wait()

    return pl.pallas_call(
        body, name=name, out_shape=jax.ShapeDtypeStruct((8 * m, n), xs.dtype),
        in_specs=[pl.BlockSpec(memory_space=pl.ANY)], out_specs=pl.BlockSpec(memory_space=pl.ANY),
        scratch_shapes=[pltpu.SemaphoreType.DMA((7,)), pltpu.SemaphoreType.DMA((7,)), pltpu.SemaphoreType.DMA],
    )(xs)


def swap_halves(g, *, name):
    nq, r, cc = g.shape
    r2 = r // 2

    def body(g_ref, out_ref, send_sem, recv_sem):
        x, y, c = _place()
        cp = pltpu.make_async_remote_copy(
            src_ref=g_ref.at[:, pl.ds((1 - c) * r2, r2), :], dst_ref=out_ref,
            send_sem=send_sem, recv_sem=recv_sem, device_id=(x, y, 1 - c), device_id_type=MESH)
        cp.start()
        cp.wait()

    return pl.pallas_call(
        body, name=name, out_shape=jax.ShapeDtypeStruct((nq, r2, cc), g.dtype),
        in_specs=[pl.BlockSpec(memory_space=pl.ANY)], out_specs=pl.BlockSpec(memory_space=pl.ANY),
        scratch_shapes=[pltpu.SemaphoreType.DMA, pltpu.SemaphoreType.DMA],
    )(g)


def exchange_chips(p, *, name):
    nq, r2, cc = p.shape

    def body(p_ref, out_ref, send_sems, recv_sems, local_sem):
        x, y, c = _place()
        q0 = 2 * x + y
        chips = [(1 - x, y), (x, 1 - y), (1 - x, 1 - y)]
        mine = pltpu.make_async_copy(p_ref.at[q0], out_ref.at[q0], local_sem)
        mine.start()
        sends = []
        for j, (cx, cy) in enumerate(chips):
            sends.append(pltpu.make_async_remote_copy(
                src_ref=p_ref.at[2 * cx + cy], dst_ref=out_ref.at[q0],
                send_sem=send_sems.at[j], recv_sem=recv_sems.at[j],
                device_id=(cx, cy, c), device_id_type=MESH))
        for cp in sends:
            cp.start()
        for j, (cx, cy) in enumerate(chips):
            qj = 2 * cx + cy
            pltpu.make_async_remote_copy(
                src_ref=p_ref.at[qj], dst_ref=out_ref.at[qj],
                send_sem=send_sems.at[j], recv_sem=recv_sems.at[j],
                device_id=(cx, cy, c), device_id_type=MESH).wait_recv()
        for cp in sends:
            cp.wait_send()
        mine.wait()

    return pl.pallas_call(
        body, name=name, out_shape=jax.ShapeDtypeStruct((nq, r2, cc), p.dtype),
        in_specs=[pl.BlockSpec(memory_space=pl.ANY)], out_specs=pl.BlockSpec(memory_space=pl.ANY),
        scratch_shapes=[pltpu.SemaphoreType.DMA((3,)), pltpu.SemaphoreType.DMA((3,)), pltpu.SemaphoreType.DMA],
    )(p)


def join_halves(tot, *, name):
    r2, cc = tot.shape[0] // 2, tot.shape[1]

    def body(t_ref, out_ref, send_sem, recv_sem):
        del t_ref
        x, y, c = _place()
        mine = out_ref.at[pl.ds(c * r2, r2), :]
        theirs = out_ref.at[pl.ds((1 - c) * r2, r2), :]
        cp = pltpu.make_async_remote_copy(
            src_ref=mine, dst_ref=mine, send_sem=send_sem, recv_sem=recv_sem,
            device_id=(x, y, 1 - c), device_id_type=MESH)
        cp.start()
        pltpu.make_async_remote_copy(
            src_ref=theirs, dst_ref=theirs, send_sem=send_sem, recv_sem=recv_sem,
            device_id=(x, y, 1 - c), device_id_type=MESH).wait_recv()
        cp.wait_send()

    return pl.pallas_call(
        body, name=name, out_shape=jax.ShapeDtypeStruct(tot.shape, tot.dtype),
        in_specs=[pl.BlockSpec(memory_space=pl.ANY)], out_specs=pl.BlockSpec(memory_space=pl.ANY),
        scratch_shapes=[pltpu.SemaphoreType.DMA, pltpu.SemaphoreType.DMA],
        input_output_aliases={0: 0},
    )(tot)


def _pack_local_shard(big, small_vec):
    parts = [w.reshape(-1, PACK_C).astype(BF) for w in big]
    srow = lax.bitcast_convert_type(small_vec, BF).reshape(1, PACK_C)
    parts.append(jnp.pad(srow, ((0, PACK_PAD - 1), (0, 0))))
    return jnp.concatenate(parts, axis=0)


def _split_rows(a, axis):
    out, off = [], 0
    for n in PACK_ROWS:
        out.append(lax.slice_in_dim(a, off, off + n, axis=axis))
        off += n
    return out, off


def _unpack_gathered(gw):
    (p_in, p_grp, p_out, m_in, m_qb, m_kvb, m_out), off = _split_rows(gw, 1)
    w = {}
    w["pool_w_in"] = p_in.reshape(4, D_MODEL, 1024).transpose(1, 0, 2).reshape(D_MODEL, 2 * POOL_WIDTH)
    w["pool_w_group"] = p_grp.reshape(4, 4, 128, POOL_GROUP).transpose(1, 0, 2, 3).reshape(4, POOL_GROUP, POOL_GROUP)
    w["pool_w_out"] = p_out.reshape(POOL_WIDTH, D_MODEL)
    win = m_in.reshape(4, D_MODEL, 688).transpose(1, 0, 2).reshape(D_MODEL, 2752)
    w["mla_w_in"] = jnp.concatenate(
        [win[:, 384:640], win[:, 640:704], jnp.zeros((D_MODEL, 64), BF), win[:, 0:384], win[:, 704:]], axis=1)
    wq = m_qb.reshape(4, Q_LORA, 768).transpose(1, 0, 2).reshape(Q_LORA, N_HEADS, QK_NOPE + QK_ROPE)
    w["mla_w_q_b"] = jnp.pad(wq, ((0, 0), (0, 0), (0, HEAD_PAD - QK_NOPE - QK_ROPE))).reshape(Q_LORA, N_HEADS * HEAD_PAD)
    w["mla_w_kv_b"] = m_kvb.reshape(4, KV_LORA, 1024).transpose(1, 0, 2).reshape(KV_LORA, 4096)
    w["mla_w_out"] = m_out.reshape(MLA_WIDTH, D_MODEL)
    small = lax.bitcast_convert_type(gw[:, off, :].reshape(4, 512, 2), F32)
    w["mla_norm"] = small[:, :256].reshape(1, D_MODEL)
    w["mla_q_norm"] = small[:, 256:352].reshape(1, Q_LORA)
    w["mla_kv_norm"] = small[:, 352:416].reshape(1, KV_LORA)
    return w


def _pack_grads(g):
    parts = [
        g["pool_w_in"].reshape(D_MODEL, 4, 1024).transpose(1, 0, 2),
        g["pool_w_group"].reshape(4, 4, 128, POOL_GROUP).transpose(1, 0, 2, 3).reshape(4, 256, PACK_C),
        g["pool_w_out"].reshape(4, 512, PACK_C),
        g["mla_w_in"].reshape(D_MODEL, 4, 688).transpose(1, 0, 2).reshape(4, 688, PACK_C),
        g["mla_w_q_b"].reshape(Q_LORA, 4, 768).transpose(1, 0, 2).reshape(4, 288, PACK_C),
        g["mla_w_kv_b"].reshape(KV_LORA, 4, 1024).transpose(1, 0, 2),
        g["mla_w_out"].reshape(4, 512, PACK_C),
        jnp.zeros((4, PACK_PAD, PACK_C), F32),
    ]
    return jnp.concatenate(parts, axis=1)


def kernel(x, positions, pool_norm, pool_w_in, pool_w_group, pool_scale, pool_w_out, mla_norm, mla_w_in, mla_q_norm, mla_w_q_b, mla_kv_norm, mla_w_kv_b, mla_w_out, final_norm, loss_target, m_pool_norm, m_pool_w_in, m_pool_w_group, m_pool_scale, m_pool_w_out, m_mla_norm, m_mla_w_in, m_mla_q_norm, m_mla_w_q_b, m_mla_kv_norm, m_mla_w_kv_b, m_mla_w_out, m_final_norm, v_pool_norm, v_pool_w_in, v_pool_w_group, v_pool_scale, v_pool_w_out, v_mla_norm, v_mla_w_in, v_mla_q_norm, v_mla_w_q_b, v_mla_kv_norm, v_mla_w_kv_b, v_mla_w_out, v_final_norm):
    s = x.shape[1]
    tq = min(512, s)
    x0 = x.reshape(s, D_MODEL)
    tgt = loss_target.reshape(s, D_MODEL)
    cx, cy, cc_idx = _place()
    chip = 2 * cx + cy

    big_names = ("pool_w_in", "pool_w_group", "pool_w_out", "mla_w_in", "mla_w_q_b", "mla_w_kv_b", "mla_w_out")
    big_w = dict(zip(big_names, (pool_w_in, pool_w_group, pool_w_out, mla_w_in, mla_w_q_b, mla_w_kv_b, mla_w_out)))
    big_m = dict(zip(big_names, (m_pool_w_in, m_pool_w_group, m_pool_w_out, m_mla_w_in, m_mla_w_q_b, m_mla_w_kv_b, m_mla_w_out)))
    big_v = dict(zip(big_names, (v_pool_w_in, v_pool_w_group, v_pool_w_out, v_mla_w_in, v_mla_w_q_b, v_mla_w_kv_b, v_mla_w_out)))

    small_vec = jnp.concatenate([mla_norm.reshape(-1), mla_q_norm.reshape(-1), mla_kv_norm.reshape(-1),
                                 jnp.zeros((96,), F32)])
    packed = _pack_local_shard([big_w[n] for n in big_names], small_vec)
    gathered = all_gather8(packed, name="gather_weights", own_half=True).reshape(4, PACK_R, PACK_C)
    w = _unpack_gathered(gathered)
    g_pool = pool_norm.reshape(1, D_MODEL)
    g_final = final_norm.reshape(1, D_MODEL)
    sc_pool = pool_scale.reshape(1, POOL_WIDTH)

    inv_freq = 1.0 / (ROPE_THETA ** (jnp.arange(0, QK_ROPE, 2, dtype=F32) / QK_ROPE))
    ang = positions.reshape(s).astype(F32)[:, None] * inv_freq
    cos, sin = jnp.cos(ang), jnp.sin(ang)
    z32, z64, z96 = (jnp.zeros((s, n), F32) for n in (32, 64, 96))
    t_cc = jnp.concatenate([cos, cos, z64], axis=1)
    t_sa = jnp.concatenate([-sin, z96], axis=1)
    t_sb = jnp.concatenate([z32, sin, z64], axis=1)

    h0 = norm_fwd(x0, g_pool, col=0, width=D_MODEL, name="pool_norm_fwd")
    uz = mm_nn(h0, w["pool_w_in"], name="pool_in_proj", out_dtype=F32)
    pd = pool_prep(uz, name="pool_window")
    mm, y1 = pool_mix_gate(pd, w["pool_w_group"], uz, sc_pool, name="pool_group_mix")
    x1 = mm_nn(y1, w["pool_w_out"], name="pool_out_proj", out_dtype=F32, add=x0)

    h1 = norm_fwd(x1, w["mla_norm"], col=0, width=D_MODEL, name="mla_norm_fwd")
    proj = mm_nn(h1, w["mla_w_in"], name="mla_in_proj", out_dtype=F32, tn=P_WIDTH // 2)
    qn = norm_fwd(proj, w["mla_q_norm"], col=P_Q, width=Q_LORA, name="mla_q_norm_fwd")
    kvn = norm_fwd(proj, w["mla_kv_norm"], col=P_KV, width=KV_LORA, name="mla_kv_norm_fwd")
    qr = q_proj_rope(qn, w["mla_w_q_b"], t_cc, t_sa, t_sb, name="mla_q_proj")
    kv = mm_nn(kvn, w["mla_w_kv_b"], name="mla_kv_proj", out_dtype=BF, tk=KV_LORA)
    krr = rope_k(proj, t_cc, t_sa, t_sb, name="mla_rope_k")
    o, y2, lse = attn_fwd(qr, kv, krr, proj, name="mla_attn_fwd", tq=tq)
    x2 = mm_nn(y2, w["mla_w_out"], name="mla_out_proj", out_dtype=F32, add=x1)

    dx2, d_final, loss_part = final_loss(x2, g_final, tgt, name="final_norm_loss")

    grads = {}
    grads["mla_w_out"] = mm_tn(y2, dx2, name="mla_out_proj_dw")
    do, dz2, delta = mla_out_dx_gate(dx2, w["mla_w_out"], o, proj, name="mla_out_proj_dx", tq=tq)
    dkv, dkr, dq_pre = attn_bwd(qr, kv, krr, do, lse, delta, t_cc, t_sa, t_sb, name="mla_attn_bwd", tq=tq)
    dkr_pre = unrope_k(dkr, t_cc, t_sa, t_sb, name="mla_unrope_k")
    dqn = mm_nt(dq_pre, w["mla_w_q_b"], name="mla_q_proj_dx", out_dtype=F32, tn=Q_LORA, tk=2048)
    g_qb = mm_tn(qn, dq_pre, name="mla_q_proj_dw", tm=Q_LORA, tn=2048)
    dkvn = mm_nt(dkv, w["mla_w_kv_b"], name="mla_kv_proj_dx", out_dtype=F32, tn=KV_LORA, tk=2048)
    grads["mla_w_kv_b"] = mm_tn(kvn, dkv, name="mla_kv_proj_dw", tm=KV_LORA, tn=2048)
    dq_lat, d_qnorm = norm_bwd(proj, w["mla_q_norm"], dqn, col=P_Q, width=Q_LORA, name="mla_q_norm_bwd", out_dtype=BF)
    dkv_lat, d_kvnorm = norm_bwd(proj, w["mla_kv_norm"], dkvn, col=P_KV, width=KV_LORA, name="mla_kv_norm_bwd", out_dtype=BF)
    dsmall = jnp.concatenate([dkv_lat, dkr_pre, dq_lat], axis=1)
    dh1 = mm_nt(dsmall, w["mla_w_in"], name="mla_in_proj_dx_a", out_dtype=F32, tk=P_SMALL)
    dh1 = mm_nt(dz2, w["mla_w_in"][:, P_Z:], name="mla_in_proj_dx_b", out_dtype=F32, add=dh1)
    g_in_a = mm_tn(h1, dsmall, name="mla_in_proj_dw_a", tn=P_SMALL)
    g_in_b = mm_tn(h1, dz2, name="mla_in_proj_dw_b", tn=2048)
    dx1, d_mnorm = norm_bwd(x1, w["mla_norm"], dh1, col=0, width=D_MODEL, name="mla_norm_bwd", res=dx2)

    grads["pool_w_out"] = mm_tn(y1, dx1, name="pool_out_proj_dw")
    dmm, dz1, d_scale = pool_out_dx_gate(dx1, w["pool_w_out"], mm, uz, sc_pool, name="pool_out_proj_dx")
    dpd = gmm_nt(dmm, w["pool_w_group"], name="pool_group_mix_dx")
    grads["pool_w_group"] = gmm_tn(pd, dmm, 4, name="pool_group_mix_dw")
    du = pool_prep_bwd(dpd, name="pool_window_bwd")
    dh0 = mm_nt(du, w["pool_w_in"], name="pool_in_proj_dx_u", out_dtype=F32, tk=1024)
    dh0 = mm_nt(dz1, w["pool_w_in"], name="pool_in_proj_dx_z", out_dtype=F32, b_col=POOL_WIDTH, add=dh0, tk=1024)
    g_pin_u = mm_tn(h0, du, name="pool_in_proj_dw_u", tn=2048)
    g_pin_z = mm_tn(h0, dz1, name="pool_in_proj_dw_z", tn=2048)
    grad_x, d_pnorm = norm_bwd(x0, g_pool, dh0, col=0, width=D_MODEL, name="pool_norm_bwd", res=dx1)

    grads["pool_w_in"] = jnp.concatenate([g_pin_u, g_pin_z], axis=1)
    g_in = jnp.concatenate([g_in_a, g_in_b], axis=1)
    grads["mla_w_in"] = jnp.concatenate([g_in[:, P_Q:P_Z], g_in[:, P_KV:P_KV + KV_LORA],
                                         g_in[:, P_KR:P_KR + QK_ROPE], g_in[:, P_Z:]], axis=1)
    grads["mla_w_q_b"] = g_qb.reshape(Q_LORA, N_HEADS, HEAD_PAD)[:, :, :QK_NOPE + QK_ROPE].reshape(Q_LORA, -1)

    gp = _pack_grads(grads)
    half_rows = PACK_R // 2 // 3
    sib = swap_halves(gp, name="grad_swap_halves")
    core_idx = cc_idx.reshape(1).astype(jnp.int32)
    pre = add_halves(gp, sib, core_idx, name="grad_add_halves", rows=half_rows)
    got = exchange_chips(pre, name="grad_exchange_chips")
    tot = sum_chips(got, core_idx, name="grad_sum_chips", rows=half_rows)
    red = join_halves(tot, name="grad_join_halves")
    red_parts, _ = _split_rows(red, 0)

    sv = jnp.concatenate([d_pnorm.reshape(-1), d_scale.reshape(-1), d_final.reshape(-1), d_mnorm.reshape(-1),
                          d_qnorm.reshape(-1), d_kvnorm.reshape(-1), loss_part[0, :1],
                          jnp.zeros((SV_ROWS * SV_COLS - SV_OFF["loss"] - 1,), F32)]).reshape(SV_ROWS, SV_COLS)
    sv_all = all_gather8(sv, name="gather_small_grads", own_half=False)
    sv_sum = sum_devices(sv_all, name="sum_small_grads").reshape(-1)
    loss = sv_sum[SV_OFF["loss"]]

    def sv_take(key, n):
        return lax.slice_in_dim(sv_sum, SV_OFF[key], SV_OFF[key] + n)

    out_g, out_d, out_m, out_v = {}, {}, {}, {}
    for name, part in zip(big_names, red_parts):
        shp = big_w[name].shape
        g2 = part.reshape(shp)
        two_d = (-1, shp[-1])
        d_, m_, v_ = adamw(big_w[name].reshape(two_d), g2.reshape(two_d), big_m[name].reshape(two_d),
                           big_v[name].reshape(two_d), name="adamw_" + name)
        out_g[name], out_d[name], out_m[name], out_v[name] = g2, d_.reshape(shp), m_.reshape(shp), v_.reshape(shp)

    small = [
        ("pool_norm", pool_norm, m_pool_norm, v_pool_norm, sv_take("pool_norm", 1024)),
        ("pool_scale", pool_scale, m_pool_scale, v_pool_scale, sv_take("pool_scale", 2048)),
        ("final_norm", final_norm, m_final_norm, v_final_norm, sv_take("final_norm", 1024)),
        ("mla_norm", mla_norm, m_mla_norm, v_mla_norm,
         lax.dynamic_slice_in_dim(sv_take("mla_norm", 1024), chip * 256, 256)),
        ("mla_q_norm", mla_q_norm, m_mla_q_norm, v_mla_q_norm,
         lax.dynamic_slice_in_dim(sv_take("q_norm", 384), chip * 96, 96)),
        ("mla_kv_norm", mla_kv_norm, m_mla_kv_norm, v_mla_kv_norm,
         lax.dynamic_slice_in_dim(sv_take("kv_norm", 256), chip * 64, 64)),
    ]
    sw = jnp.concatenate([t[1].reshape(-1) for t in small] + [jnp.zeros((96,), F32)]).reshape(1, -1)
    sm = jnp.concatenate([t[2].reshape(-1) for t in small] + [jnp.zeros((96,), F32)]).reshape(1, -1)
    s_v = jnp.concatenate([t[3].reshape(-1) for t in small] + [jnp.ones((96,), F32)]).reshape(1, -1)
    sg = jnp.concatenate([t[4].reshape(-1) for t in small] + [jnp.zeros((96,), F32)]).reshape(1, -1)
    sd_, sm_, sv_ = adamw(sw, sg, sm, s_v, name="adamw_vectors")
    off = 0
    for name, wt, _, _, gvec in small:
        n = gvec.shape[0]
        shp = wt.shape
        out_g[name] = gvec.reshape(shp)
        out_d[name] = sd_[0, off:off + n].reshape(shp)
        out_m[name] = sm_[0, off:off + n].reshape(shp)
        out_v[name] = sv_[0, off:off + n].reshape(shp)
        off += n

    order = ("pool_norm", "pool_w_in", "pool_w_group", "pool_scale", "pool_w_out", "mla_norm", "mla_w_in",
             "mla_q_norm", "mla_w_q_b", "mla_kv_norm", "mla_w_kv_b", "mla_w_out", "final_norm")
    return (loss, grad_x.reshape(x.shape), *[out_g[n] for n in order], *[out_d[n] for n in order],
            *[out_m[n] for n in order], *[out_v[n] for n in order])
```

```python
import functools
from typing import Callable, NamedTuple

import jax
import jax.numpy as jnp
from jax import lax
from jax.experimental import pallas as pl
from jax.experimental.pallas import tpu as pltpu

F32 = jnp.float32
BF = jnp.bfloat16
MESH = pl.DeviceIdType.MESH

D_MODEL = 1024
POOL_WIDTH = 2048
POOL_WINDOWS = (2, 4, 8, 16)
POOL_GROUP = 512
HALO = 16
N_HEADS = 16
QK_NOPE = 128
QK_ROPE = 64
V_DIM = 128
HEAD_PAD = 256
Q_LORA = 384
KV_LORA = 256
MLA_WIDTH = 2048
ROPE_THETA = 10000.0
EPS = 1e-6
SCALE = (QK_NOPE + QK_ROPE) ** -0.5
SCALE_LOG2E = SCALE * 1.4426950408889634
NEG = -1e30

P_KV, P_KR, P_Q, P_Z = 0, 256, 384, 768
P_SMALL = 768
P_WIDTH = 2816

ADAM_LR = 0.001
ADAM_B1 = 0.9
ADAM_B2 = 0.999
ADAM_EPS = 1e-08
ADAM_WD = 0.01
ADAM_STEP = 10

NN = (((1,), (0,)), ((), ()))
NT = (((1,), (1,)), ((), ()))
TN = (((0,), (0,)), ((), ()))

POOL_ROWS = (1024, 256, 512)
MLA_ROWS = (688, 288, 256, 512)
PACK_PAD = 16
POOL_R = sum(POOL_ROWS)
MLA_R = sum(MLA_ROWS) + PACK_PAD
PACK_C = 1024
SV_OFF = dict(pool_norm=0, pool_scale=1024, final_norm=3072, mla_norm=4096, q_norm=5120, kv_norm=5504, loss=5760)
SV_ROWS, SV_COLS = 8, 768

VMEM_LIMIT = 56 * 1024 * 1024


def _params(n_axes, vmem=None):
    return pltpu.CompilerParams(dimension_semantics=("arbitrary",) * n_axes,
                                vmem_limit_bytes=VMEM_LIMIT if vmem is None else vmem)


def _sigmoid(z):
    return 1.0 / (1.0 + jnp.exp(-z))


class Exchange(NamedTuple):
    operands: tuple
    out_shapes: tuple
    aliases: dict
    n_sems: int
    start: Callable
    wait: Callable


HBM_SPEC = pl.BlockSpec(memory_space=pl.ANY)


def _exchange_scratch(ex):
    return [pltpu.SemaphoreType.DMA((ex.n_sems,)), pltpu.SemaphoreType.DMA((ex.n_sems,)), pltpu.SemaphoreType.DMA]


def run_exchange(ex, *, name):
    n_in, n_out = len(ex.operands), len(ex.out_shapes)

    def body(*refs):
        args = (refs[:n_in], refs[n_in:n_in + n_out]) + tuple(refs[n_in + n_out:])
        ex.start(*args)
        ex.wait(*args)

    return pl.pallas_call(
        body, name=name, out_shape=list(ex.out_shapes), in_specs=[HBM_SPEC] * n_in,
        out_specs=[HBM_SPEC] * n_out, scratch_shapes=_exchange_scratch(ex),
        input_output_aliases=dict(ex.aliases))(*ex.operands)


def _call(core, *, name, grid, in_specs, out_specs, out_shape, args, scratch=(), host=None):
    in_specs, out_specs, out_shape = list(in_specs), list(out_specs), list(out_shape)
    params = _params(len(grid))
    if host is None:
        return pl.pallas_call(core, name=name, grid=grid, in_specs=in_specs, out_specs=out_specs,
                              out_shape=out_shape, scratch_shapes=list(scratch), compiler_params=params)(*args)
    n_in, n_out = len(in_specs), len(out_specs)
    n_hin, n_hout = len(host.operands), len(host.out_shapes)

    def body(*refs):
        ins, refs = refs[:n_in], refs[n_in:]
        h_in, refs = refs[:n_hin], refs[n_hin:]
        outs, refs = refs[:n_out], refs[n_out:]
        h_out, refs = refs[:n_hout], refs[n_hout:]
        own_scratch, sems = refs[:-3], refs[-3:]
        ids = [pl.program_id(ax) for ax in range(len(grid))]
        first = functools.reduce(jnp.logical_and, [i == 0 for i in ids])
        last = functools.reduce(jnp.logical_and, [i == n - 1 for i, n in zip(ids, grid)])

        @pl.when(first)
        def _():
            host.start(h_in, h_out, *sems)

        core(*ins, *outs, *own_scratch)

        @pl.when(last)
        def _():
            host.wait(h_in, h_out, *sems)

    return pl.pallas_call(
        body, name=name, grid=grid, in_specs=in_specs + [HBM_SPEC] * n_hin,
        out_specs=out_specs + [HBM_SPEC] * n_hout, out_shape=out_shape + list(host.out_shapes),
        scratch_shapes=list(scratch) + _exchange_scratch(host),
        input_output_aliases={n_in + i: n_out + o for i, o in host.aliases.items()},
        compiler_params=params)(*args, *host.operands)


def _mm(a, b, *, dims, grid, a_spec, b_spec, o_spec, out_shape, out_dtype, acc_shape, name,
        add=None, add_spec=None, host=None):
    nk = grid[-1]
    kax = len(grid) - 1

    def body(*refs):
        if add is None:
            a_ref, b_ref, o_ref = refs[:3]
            add_ref = None
            rest = refs[3:]
        else:
            a_ref, b_ref, add_ref, o_ref = refs[:4]
            rest = refs[4:]
        part = lax.dot_general(a_ref[...].astype(BF), b_ref[...].astype(BF), dims,
                               preferred_element_type=F32)

        def finish(r):
            if add_ref is not None:
                r = r + add_ref[...]
            o_ref[...] = r.astype(o_ref.dtype)

        if nk == 1:
            finish(part)
        else:
            acc = rest[0]
            k = pl.program_id(kax)

            @pl.when(k == 0)
            def _():
                acc[...] = part

            @pl.when(k > 0)
            def _():
                acc[...] += part

            @pl.when(k == nk - 1)
            def _():
                finish(acc[...])

    in_specs = [a_spec, b_spec]
    args = [a, b]
    if add is not None:
        in_specs.append(add_spec)
        args.append(add)
    out = _call(body, name=name, grid=grid, in_specs=in_specs, out_specs=[o_spec],
                out_shape=[jax.ShapeDtypeStruct(out_shape, out_dtype)], args=args,
                scratch=[] if nk == 1 else [pltpu.VMEM(acc_shape, F32)], host=host)
    return out[0] if host is None else out


def _pick(n, t):
    t = min(n, t)
    assert n % t == 0, (n, t)
    return t


def mm_nn(a, b, *, name, out_dtype, add=None, tm=1024, tn=1024, tk=2048, host=None):
    m = a.shape[0]
    kk, n = b.shape
    tm, tn, tk = _pick(m, tm), _pick(n, tn), _pick(kk, tk)
    return _mm(a, b, dims=NN, grid=(m // tm, n // tn, kk // tk),
               a_spec=pl.BlockSpec((tm, tk), lambda i, j, k: (i, k)),
               b_spec=pl.BlockSpec((tk, tn), lambda i, j, k: (k, j)),
               o_spec=pl.BlockSpec((tm, tn), lambda i, j, k: (i, j)),
               add=add, add_spec=pl.BlockSpec((tm, tn), lambda i, j, k: (i, j)),
               out_shape=(m, n), out_dtype=out_dtype, acc_shape=(tm, tn), name=name, host=host)


def mm_nt(a, b, *, name, out_dtype, b_col=0, add=None, tm=1024, tn=1024, tk=2048):
    m, kk = a.shape
    n = b.shape[0]
    tm, tn, tk = _pick(m, tm), _pick(n, tn), _pick(kk, tk)
    assert b_col % tk == 0
    ko = b_col // tk
    return _mm(a, b, dims=NT, grid=(m // tm, n // tn, kk // tk),
               a_spec=pl.BlockSpec((tm, tk), lambda i, j, k: (i, k)),
               b_spec=pl.BlockSpec((tn, tk), lambda i, j, k: (j, ko + k)),
               o_spec=pl.BlockSpec((tm, tn), lambda i, j, k: (i, j)),
               add=add, add_spec=pl.BlockSpec((tm, tn), lambda i, j, k: (i, j)),
               out_shape=(m, n), out_dtype=out_dtype, acc_shape=(tm, tn), name=name)


def mm_tn(a, b, *, name, tm=1024, tn=1024, tk=2048, host=None):
    s, m = a.shape
    n = b.shape[1]
    tm, tn, tk = _pick(m, tm), _pick(n, tn), _pick(s, tk)
    return _mm(a, b, dims=TN, grid=(m // tm, n // tn, s // tk),
               a_spec=pl.BlockSpec((tk, tm), lambda i, j, k: (k, i)),
               b_spec=pl.BlockSpec((tk, tn), lambda i, j, k: (k, j)),
               o_spec=pl.BlockSpec((tm, tn), lambda i, j, k: (i, j)),
               out_shape=(m, n), out_dtype=F32, acc_shape=(tm, tn), name=name, host=host)


def gmm_nt(a, w, *, name, tm=1024, host=None):
    s = a.shape[0]
    g, kk, n = w.shape
    tm = _pick(s, tm)
    return _mm(a, w, dims=NT, grid=(s // tm, g, 1),
               a_spec=pl.BlockSpec((tm, n), lambda i, gi, k: (i, gi)),
               b_spec=pl.BlockSpec((None, kk, n), lambda i, gi, k: (gi, 0, 0)),
               o_spec=pl.BlockSpec((tm, kk), lambda i, gi, k: (i, gi)),
               out_shape=(s, g * kk), out_dtype=F32, acc_shape=(tm, kk), name=name, host=host)


def gmm_tn(a, b, g, *, name, tk=2048):
    s = a.shape[0]
    kk, n = a.shape[1] // g, b.shape[1] // g
    tk = _pick(s, tk)
    return _mm(a, b, dims=TN, grid=(g, s // tk),
               a_spec=pl.BlockSpec((tk, kk), lambda gi, k: (k, gi)),
               b_spec=pl.BlockSpec((tk, n), lambda gi, k: (k, gi)),
               o_spec=pl.BlockSpec((None, kk, n), lambda gi, k: (gi, 0, 0)),
               out_shape=(g, kk, n), out_dtype=F32, acc_shape=(kk, n), name=name)


def norm_fwd(x, g, *, col, width, name, t=512):
    s = x.shape[0]
    t = _pick(s, t)
    cb = col // width
    assert col % width == 0

    def body(x_ref, g_ref, o_ref):
        xv = x_ref[...]
        inv = lax.rsqrt(jnp.mean(xv * xv, axis=-1, keepdims=True) + EPS)
        o_ref[...] = ((xv * inv) * g_ref[...]).astype(o_ref.dtype)

    return pl.pallas_call(
        body, name=name, grid=(s // t,),
        in_specs=[pl.BlockSpec((t, width), lambda i: (i, cb)), pl.BlockSpec((1, width), lambda i: (0, 0))],
        out_specs=pl.BlockSpec((t, width), lambda i: (i, 0)),
        out_shape=jax.ShapeDtypeStruct((s, width), BF), compiler_params=_params(1))(x, g)


def norm_bwd(x, g, dh, *, col, width, name, res=None, out_dtype=F32, t=512):
    s = x.shape[0]
    t = _pick(s, t)
    cb = col // width
    assert col % width == 0

    def body(*refs):
        if res is None:
            x_ref, g_ref, dh_ref, dx_ref, dg_ref = refs
        else:
            x_ref, g_ref, dh_ref, res_ref, dx_ref, dg_ref = refs
        xv = x_ref[...]
        inv = lax.rsqrt(jnp.mean(xv * xv, axis=-1, keepdims=True) + EPS)
        xhat = xv * inv
        dh_v = dh_ref[...]
        part = jnp.sum(dh_v * xhat, axis=0, keepdims=True)

        @pl.when(pl.program_id(0) == 0)
        def _():
            dg_ref[...] = part

        @pl.when(pl.program_id(0) > 0)
        def _():
            dg_ref[...] += part

        dxhat = dh_v * g_ref[...]
        dx = inv * (dxhat - xhat * jnp.mean(dxhat * xhat, axis=-1, keepdims=True))
        if res is not None:
            dx = dx + res_ref[...]
        dx_ref[...] = dx.astype(dx_ref.dtype)

    row = pl.BlockSpec((t, width), lambda i: (i, 0))
    vec = pl.BlockSpec((1, width), lambda i: (0, 0))
    in_specs = [pl.BlockSpec((t, width), lambda i: (i, cb)), vec, row]
    args = [x, g, dh]
    if res is not None:
        in_specs.append(row)
        args.append(res)
    return pl.pallas_call(
        body, name=name, grid=(s // t,), in_specs=in_specs, out_specs=[row, vec],
        out_shape=[jax.ShapeDtypeStruct((s, width), out_dtype), jax.ShapeDtypeStruct((1, width), F32)],
        compiler_params=_params(1))(*args)


def final_loss(x2, gf, tgt, *, name, t=512):
    s, d = x2.shape
    t = _pick(s, t)

    def body(x_ref, g_ref, t_ref, dx_ref, dg_ref, loss_ref):
        xv = x_ref[...]
        inv = lax.rsqrt(jnp.mean(xv * xv, axis=-1, keepdims=True) + EPS)
        xhat = xv * inv
        gv = g_ref[...]
        diff = xhat * gv - t_ref[...]
        row_err = jnp.mean(diff * diff, axis=-1, keepdims=True)
        lpart = jnp.broadcast_to(0.5 * jnp.sum(row_err, axis=0, keepdims=True), (1, 128))
        dout = diff * (1.0 / d)
        gpart = jnp.sum(dout * xhat, axis=0, keepdims=True)

        @pl.when(pl.program_id(0) == 0)
        def _():
            dg_ref[...] = gpart
            loss_ref[...] = lpart

        @pl.when(pl.program_id(0) > 0)
        def _():
            dg_ref[...] += gpart
            loss_ref[...] += lpart

        dxhat = dout * gv
        dx_ref[...] = inv * (dxhat - xhat * jnp.mean(dxhat * xhat, axis=-1, keepdims=True))

    row = pl.BlockSpec((t, d), lambda i: (i, 0))
    vec = pl.BlockSpec((1, d), lambda i: (0, 0))
    return pl.pallas_call(
        body, name=name, grid=(s // t,), in_specs=[row, vec, row],
        out_specs=[row, vec, pl.BlockSpec((1, 128), lambda i: (0, 0))],
        out_shape=[jax.ShapeDtypeStruct((s, d), F32), jax.ShapeDtypeStruct((1, d), F32),
                   jax.ShapeDtypeStruct((1, 128), F32)],
        compiler_params=_params(1))(x2, gf, tgt)


def pool_prep(uz, *, name, t=256):
    s = uz.shape[0]
    t = _pick(s, t)
    hb = t // HALO

    def body(u_ref, halo_ref, o_ref, buf):
        i = pl.program_id(0)
        buf[pl.ds(HALO, t), :] = u_ref[...]

        @pl.when(i == 0)
        def _():
            buf[pl.ds(0, HALO), :] = jnp.zeros((HALO, POOL_WIDTH), F32)

        @pl.when(i > 0)
        def _():
            buf[pl.ds(0, HALO), :] = halo_ref[...]

        pos = i * t + lax.broadcasted_iota(jnp.int32, (t, POOL_GROUP), 0)
        for g, w in enumerate(POOL_WINDOWS):
            cols = pl.ds(g * POOL_GROUP, POOL_GROUP)
            cur = buf[pl.ds(HALO, t), cols]
            acc = cur
            for k in range(1, w):
                acc = acc + buf[pl.ds(HALO - k, t), cols]
            cnt = jnp.minimum(pos + 1, w).astype(F32)
            o_ref[:, cols] = (acc / cnt - cur).astype(o_ref.dtype)

    return pl.pallas_call(
        body, name=name, grid=(s // t,),
        in_specs=[pl.BlockSpec((t, POOL_WIDTH), lambda i: (i, 0)),
                  pl.BlockSpec((HALO, POOL_WIDTH), lambda i: (jnp.maximum(i * hb - 1, 0), 0))],
        out_specs=pl.BlockSpec((t, POOL_WIDTH), lambda i: (i, 0)),
        out_shape=jax.ShapeDtypeStruct((s, POOL_WIDTH), BF),
        scratch_shapes=[pltpu.VMEM((t + HALO, POOL_WIDTH), F32)],
        compiler_params=_params(1))(uz, uz)


def pool_prep_bwd(dpd, *, name, t=256):
    s = dpd.shape[0]
    t = _pick(s, t)
    hb = t // HALO
    n = s // t

    def body(d_ref, halo_ref, o_ref, buf):
        i = pl.program_id(0)
        pos = i * t + lax.broadcasted_iota(jnp.int32, (t, POOL_GROUP), 0)
        for g, w in enumerate(POOL_WINDOWS):
            cols = pl.ds(g * POOL_GROUP, POOL_GROUP)
            cnt = jnp.minimum(pos + 1, w).astype(F32)
            buf[pl.ds(0, t), cols] = d_ref[:, cols] / cnt

            @pl.when(i < n - 1)
            def _():
                buf[pl.ds(t, HALO), cols] = halo_ref[:, cols] / float(w)

            @pl.when(i == n - 1)
            def _():
                buf[pl.ds(t, HALO), cols] = jnp.zeros((HALO, POOL_GROUP), F32)

        for g, w in enumerate(POOL_WINDOWS):
            cols = pl.ds(g * POOL_GROUP, POOL_GROUP)
            acc = buf[pl.ds(0, t), cols]
            for k in range(1, w):
                acc = acc + buf[pl.ds(k, t), cols]
            o_ref[:, cols] = (acc - d_ref[:, cols]).astype(o_ref.dtype)

    return pl.pallas_call(
        body, name=name, grid=(n,),
        in_specs=[pl.BlockSpec((t, POOL_WIDTH), lambda i: (i, 0)),
                  pl.BlockSpec((HALO, POOL_WIDTH), lambda i: (jnp.minimum((i + 1) * hb, n * hb - 1), 0))],
        out_specs=pl.BlockSpec((t, POOL_WIDTH), lambda i: (i, 0)),
        out_shape=jax.ShapeDtypeStruct((s, POOL_WIDTH), BF),
        scratch_shapes=[pltpu.VMEM((t + HALO, POOL_WIDTH), F32)],
        compiler_params=_params(1))(dpd, dpd)


CHUNK = 512


def _chunks(width, step=CHUNK):
    return [slice(c, c + step) for c in range(0, width, step)]


def pool_mix_gate(pd, wg, uz, scale, *, name, tm=1024):
    s = pd.shape[0]
    g = wg.shape[0]
    tm = _pick(s, tm)

    def body(a_ref, w_ref, z_ref, sc_ref, mm_ref, y_ref):
        mm = jnp.dot(a_ref[...], w_ref[...], preferred_element_type=F32)
        mm_ref[...] = mm
        z = z_ref[...]
        y_ref[...] = ((mm * sc_ref[...]) * (z * _sigmoid(z))).astype(y_ref.dtype)

    blk = pl.BlockSpec((tm, POOL_GROUP), lambda i, gi: (i, gi))
    return pl.pallas_call(
        body, name=name, grid=(s // tm, g),
        in_specs=[blk, pl.BlockSpec((None, POOL_GROUP, POOL_GROUP), lambda i, gi: (gi, 0, 0)),
                  pl.BlockSpec((tm, POOL_GROUP), lambda i, gi: (i, g + gi)),
                  pl.BlockSpec((1, POOL_GROUP), lambda i, gi: (0, gi))],
        out_specs=[blk, blk],
        out_shape=[jax.ShapeDtypeStruct((s, POOL_WIDTH), F32), jax.ShapeDtypeStruct((s, POOL_WIDTH), BF)],
        compiler_params=_params(2))(pd, wg, uz, scale)


def pool_out_dx_gate(dx, w_out, mm, uz, scale, *, name, tm=512, host=None):
    s, d = dx.shape
    tm = _pick(s, tm)

    def body(dx_ref, w_ref, mm_ref, z_ref, sc_ref, dmm_ref, dz_ref, dsc_ref):
        dxv = dx_ref[...].astype(BF)
        parts = []
        for c in _chunks(POOL_WIDTH):
            dyv = lax.dot_general(dxv, w_ref[c, :], NT, preferred_element_type=F32)
            z = z_ref[:, c]
            sig = _sigmoid(z)
            mmv = mm_ref[:, c]
            scv = sc_ref[:, c]
            dmixed = dyv * (z * sig)
            dmm_ref[:, c] = (dmixed * scv).astype(dmm_ref.dtype)
            dz_ref[:, c] = (dyv * (mmv * scv) * (sig * (1.0 + z * (1.0 - sig)))).astype(dz_ref.dtype)
            parts.append(jnp.sum(dmixed * mmv, axis=0, keepdims=True))

        @pl.when(pl.program_id(0) == 0)
        def _():
            for c, part in zip(_chunks(POOL_WIDTH), parts):
                dsc_ref[:, c] = part

        @pl.when(pl.program_id(0) > 0)
        def _():
            for c, part in zip(_chunks(POOL_WIDTH), parts):
                dsc_ref[:, c] += part

    blk = pl.BlockSpec((tm, POOL_WIDTH), lambda i: (i, 0))
    vec = pl.BlockSpec((1, POOL_WIDTH), lambda i: (0, 0))
    return _call(
        body, name=name, grid=(s // tm,),
        in_specs=[pl.BlockSpec((tm, d), lambda i: (i, 0)), pl.BlockSpec((POOL_WIDTH, d), lambda i: (0, 0)),
                  blk, pl.BlockSpec((tm, POOL_WIDTH), lambda i: (i, 1)), vec],
        out_specs=[blk, blk, vec],
        out_shape=[jax.ShapeDtypeStruct((s, POOL_WIDTH), BF), jax.ShapeDtypeStruct((s, POOL_WIDTH), BF),
                   jax.ShapeDtypeStruct((1, POOL_WIDTH), F32)],
        args=[dx, w_out, mm, uz, scale], host=host)


def _rope(a, cc, sa, sb):
    return a * cc + pltpu.roll(a, 96, 1) * sa + pltpu.roll(a, 32, 1) * sb


def _unrope(d, cc, sa, sb):
    return d * cc + pltpu.roll(d * sa, 32, 1) + pltpu.roll(d * sb, 96, 1)


def q_proj_rope(qn, wq, cc, sa, sb, *, name, tm=1024, heads=4):
    s, kk = qn.shape
    tm = _pick(s, tm)
    tn = heads * HEAD_PAD

    def body(a_ref, b_ref, cc_ref, sa_ref, sb_ref, o_ref):
        q = jnp.dot(a_ref[...], b_ref[...], preferred_element_type=F32)
        for h in range(heads):
            nope = slice(h * HEAD_PAD, h * HEAD_PAD + QK_NOPE)
            rope = slice(h * HEAD_PAD + QK_NOPE, (h + 1) * HEAD_PAD)
            o_ref[:, nope] = q[:, nope].astype(o_ref.dtype)
            o_ref[:, rope] = _rope(q[:, rope], cc_ref[...], sa_ref[...], sb_ref[...]).astype(o_ref.dtype)

    tab = pl.BlockSpec((tm, 128), lambda i, j: (i, 0))
    return pl.pallas_call(
        body, name=name, grid=(s // tm, N_HEADS // heads),
        in_specs=[pl.BlockSpec((tm, kk), lambda i, j: (i, 0)), pl.BlockSpec((kk, tn), lambda i, j: (0, j)),
                  tab, tab, tab],
        out_specs=pl.BlockSpec((tm, tn), lambda i, j: (i, j)),
        out_shape=jax.ShapeDtypeStruct((s, N_HEADS * HEAD_PAD), BF), compiler_params=_params(2))(qn, wq, cc, sa, sb)


def rope_k(proj, cc, sa, sb, *, name, t=512):
    s = proj.shape[0]
    t = _pick(s, t)
    kr_blk = P_KR // 128

    def body(kr_ref, cc_ref, sa_ref, sb_ref, o_ref):
        o_ref[...] = _rope(kr_ref[...], cc_ref[...], sa_ref[...], sb_ref[...]).astype(o_ref.dtype)

    tab = pl.BlockSpec((t, 128), lambda i: (i, 0))
    return pl.pallas_call(
        body, name=name, grid=(s // t,),
        in_specs=[pl.BlockSpec((t, 128), lambda i: (i, kr_blk)), tab, tab, tab], out_specs=tab,
        out_shape=jax.ShapeDtypeStruct((s, 128), BF), compiler_params=_params(1))(proj, cc, sa, sb)


def unrope_k(dkr, cc, sa, sb, *, name, t=512):
    s = dkr.shape[0]
    t = _pick(s, t)

    def body(d_ref, cc_ref, sa_ref, sb_ref, o_ref):
        o_ref[...] = _unrope(d_ref[...], cc_ref[...], sa_ref[...], sb_ref[...]).astype(o_ref.dtype)

    tab = pl.BlockSpec((t, 128), lambda i: (i, 0))
    return pl.pallas_call(
        body, name=name, grid=(s // t,), in_specs=[tab, tab, tab, tab], out_specs=tab,
        out_shape=jax.ShapeDtypeStruct((s, 128), BF), compiler_params=_params(1))(dkr, cc, sa, sb)


def mla_out_dx_gate(dx, w_out, o, proj, *, name, tq):
    s, d = dx.shape
    nq = s // tq

    def body(dx_ref, w_ref, o_ref, p_ref, do_ref, dz_ref, dl_ref):
        dxv = dx_ref[...].astype(BF)
        for c in _chunks(MLA_WIDTH):
            dy_c = lax.dot_general(dxv, w_ref[c, :], NT, preferred_element_type=F32)
            for h in range(c.start // V_DIM, c.stop // V_DIM):
                hc = slice(h * V_DIM, (h + 1) * V_DIM)
                z = p_ref[:, slice(P_Z + hc.start, P_Z + hc.stop)]
                sig = _sigmoid(z)
                dyv = dy_c[:, hc.start - c.start:hc.stop - c.start]
                ov = o_ref[:, hc]
                dov = dyv * (z * sig)
                do_ref[:, hc] = dov.astype(do_ref.dtype)
                dz_ref[:, hc] = (dyv * ov * (sig * (1.0 + z * (1.0 - sig)))).astype(dz_ref.dtype)
                delta = jnp.sum(dov * ov, axis=-1, keepdims=True)
                dl_ref[h] = jnp.broadcast_to(delta, (tq, 128)).T[:8, :]

    blk = pl.BlockSpec((tq, MLA_WIDTH), lambda i: (i, 0))
    return pl.pallas_call(
        body, name=name, grid=(nq,),
        in_specs=[pl.BlockSpec((tq, d), lambda i: (i, 0)), pl.BlockSpec((MLA_WIDTH, d), lambda i: (0, 0)),
                  blk, pl.BlockSpec((tq, P_WIDTH), lambda i: (i, 0))],
        out_specs=[blk, blk, pl.BlockSpec((N_HEADS, None, 8, tq), lambda i: (0, i, 0, 0))],
        out_shape=[jax.ShapeDtypeStruct((s, MLA_WIDTH), BF), jax.ShapeDtypeStruct((s, MLA_WIDTH), BF),
                   jax.ShapeDtypeStruct((N_HEADS, nq, 8, tq), F32)],
        compiler_params=_params(1))(dx, w_out, o, proj)


FWD_GROUPS = (4, 2, 1)
BWD_GROUPS = (4, 2, 1)


def _for_groups(first, count, groups, fn):
    lead = groups[-1]
    for g in groups[:-1][::-1]:
        lead = jnp.where(count >= g, g, lead)
    for g in groups:
        @pl.when(lead == g)
        def _(g=g):
            fn(first, g, True)
    first = first + lead
    count = count - lead
    for g in groups:
        n = count // g

        def one(p, carry, g=g, first=first):
            fn(first + p * g, g, False)
            return carry

        lax.fori_loop(0, n, one, 0)
        first = first + n * g
        count = count - n * g


def attn_fwd(qr, kv, krr, proj, *, name, tq):
    s = qr.shape[0]
    nq = s // tq
    z_blk = P_Z // V_DIM

    def body(kn_ref, v_ref, kr_ref, q_ref, z_ref, o_ref, y_ref, lse_ref, acc_sc, m_sc):
        j = pl.program_id(1)

        @pl.when(j == 0)
        def _():
            acc_sc[...] = jnp.zeros((s, 2 * V_DIM), F32)
            m_sc[...] = jnp.full((nq, 8, tq), NEG, F32)

        k = jnp.concatenate([kn_ref[...], kr_ref[...]], axis=1)
        vx = jnp.concatenate([v_ref[...], jnp.ones((tq, V_DIM), BF)], axis=1)

        def update(i, n_tiles, masked):
            rows = pl.ds(pl.multiple_of(i * tq, tq), n_tiles * tq)
            st = lax.dot_general(k, q_ref[rows, :], NT, preferred_element_type=F32) * SCALE_LOG2E
            if masked:
                krow = lax.broadcasted_iota(jnp.int32, (tq, n_tiles * tq), 0)
                qcol = lax.broadcasted_iota(jnp.int32, (tq, n_tiles * tq), 1)
                st = jnp.where(qcol >= krow, st, NEG)
            m_prev = jnp.concatenate([m_sc[i + n, pl.ds(0, 1), :] for n in range(n_tiles)], axis=1)
            m_new = jnp.maximum(m_prev, jnp.max(st, axis=0, keepdims=True))
            alpha_c = jnp.broadcast_to(jnp.exp2(m_prev - m_new), (128, n_tiles * tq)).T
            pt = jnp.exp2(st - m_new).astype(BF)
            pv = lax.dot_general(pt, vx, TN, preferred_element_type=F32)
            for cols in (slice(0, V_DIM), slice(V_DIM, 2 * V_DIM)):
                acc_sc[rows, cols] = alpha_c * acc_sc[rows, cols] + pv[:, cols]
            for n in range(n_tiles):
                m_sc[i + n, pl.ds(0, 1), :] = m_new[:, n * tq:(n + 1) * tq]

        _for_groups(j, nq - j, FWD_GROUPS, update)
        mine = pl.ds(pl.multiple_of(j * tq, tq), tq)
        l = acc_sc[mine, V_DIM:]
        o = acc_sc[mine, :V_DIM] / l
        o_ref[...] = o
        z = z_ref[...]
        y_ref[...] = (o * (z * _sigmoid(z))).astype(y_ref.dtype)
        lse_ref[...] = jnp.broadcast_to(m_sc[j, pl.ds(0, 1), :], (8, tq)) + jnp.log2(l).T[:8, :]

    tile = pl.BlockSpec((tq, V_DIM), lambda h, j: (j, h))
    return pl.pallas_call(
        body, name=name, grid=(N_HEADS, nq),
        in_specs=[pl.BlockSpec((tq, QK_NOPE), lambda h, j: (j, 2 * h)),
                  pl.BlockSpec((tq, V_DIM), lambda h, j: (j, 2 * h + 1)),
                  pl.BlockSpec((tq, 128), lambda h, j: (j, 0)),
                  pl.BlockSpec((s, HEAD_PAD), lambda h, j: (0, h)),
                  pl.BlockSpec((tq, V_DIM), lambda h, j: (j, z_blk + h))],
        out_specs=[tile, tile, pl.BlockSpec((None, None, 8, tq), lambda h, j: (h, j, 0, 0))],
        out_shape=[jax.ShapeDtypeStruct((s, N_HEADS * V_DIM), F32),
                   jax.ShapeDtypeStruct((s, N_HEADS * V_DIM), BF),
                   jax.ShapeDtypeStruct((N_HEADS, nq, 8, tq), F32)],
        scratch_shapes=[pltpu.VMEM((s, 2 * V_DIM), F32), pltpu.VMEM((nq, 8, tq), F32)],
        compiler_params=_params(2))(kv, kv, krr, qr, proj)


def attn_bwd(qr, kv, krr, do, lse, delta, cc, sa, sb, *, name, tq):
    s = qr.shape[0]
    nq = s // tq

    def body(kn_ref, v_ref, kr_ref, q_ref, do_ref, lse_ref, dl_ref, cc_ref, sa_ref, sb_ref,
             dkv_ref, dkr_ref, dq_ref, dq_sc, dk_sc, dv_sc):
        h = pl.program_id(0)
        j = pl.program_id(1)

        @pl.when(j == 0)
        def _():
            dq_sc[...] = jnp.zeros((s, HEAD_PAD), F32)

        dk_sc[...] = jnp.zeros((tq, HEAD_PAD), F32)
        dv_sc[...] = jnp.zeros((tq, V_DIM), F32)
        k = jnp.concatenate([kn_ref[...], kr_ref[...]], axis=1)
        v = v_ref[...]

        def step(i, n_tiles, masked):
            r0 = pl.multiple_of(i * tq, tq)
            rows = pl.ds(r0, n_tiles * tq)
            q = q_ref[rows, :]
            dov = do_ref[rows, :]
            lse_row = jnp.concatenate([lse_ref[i + n, pl.ds(0, 1), :] for n in range(n_tiles)], axis=1)
            dl_row = jnp.concatenate([dl_ref[i + n, pl.ds(0, 1), :] for n in range(n_tiles)], axis=1)
            st = lax.dot_general(k, q, NT, preferred_element_type=F32) * SCALE_LOG2E
            if masked:
                krow = lax.broadcasted_iota(jnp.int32, (tq, n_tiles * tq), 0)
                qcol = lax.broadcasted_iota(jnp.int32, (tq, n_tiles * tq), 1)
                st = jnp.where(qcol >= krow, st, NEG)
            pt = jnp.exp2(st - lse_row)
            dpt = lax.dot_general(v, dov, NT, preferred_element_type=F32)
            dst = (pt * (dpt - dl_row)).astype(BF)
            dv_sc[...] += jnp.dot(pt.astype(BF), dov, preferred_element_type=F32)
            dk_sc[...] += jnp.dot(dst, q, preferred_element_type=F32)
            dq_sc[rows, :] += lax.dot_general(dst, k, TN, preferred_element_type=F32)

        _for_groups(j, nq - j, BWD_GROUPS, step)
        dkv_ref[:, :QK_NOPE] = (dk_sc[:, :QK_NOPE] * SCALE).astype(dkv_ref.dtype)
        dkv_ref[:, QK_NOPE:] = dv_sc[...].astype(dkv_ref.dtype)
        mine = pl.ds(pl.multiple_of(j * tq, tq), tq)
        dkr = dk_sc[:, QK_NOPE:] * SCALE

        @pl.when(h == 0)
        def _():
            dkr_ref[mine, :] = dkr

        @pl.when(h > 0)
        def _():
            dkr_ref[mine, :] += dkr

        dq_ref[:, :QK_NOPE] = (dq_sc[mine, :QK_NOPE] * SCALE).astype(dq_ref.dtype)
        dq_ref[:, QK_NOPE:] = _unrope(dq_sc[mine, QK_NOPE:] * SCALE, cc_ref[...], sa_ref[...],
                                      sb_ref[...]).astype(dq_ref.dtype)

    rows = pl.BlockSpec((None, nq, 8, tq), lambda h, j: (h, 0, 0, 0))
    tab = pl.BlockSpec((tq, 128), lambda h, j: (j, 0))
    return pl.pallas_call(
        body, name=name, grid=(N_HEADS, nq),
        in_specs=[pl.BlockSpec((tq, QK_NOPE), lambda h, j: (j, 2 * h)),
                  pl.BlockSpec((tq, V_DIM), lambda h, j: (j, 2 * h + 1)), tab,
                  pl.BlockSpec((s, HEAD_PAD), lambda h, j: (0, h)),
                  pl.BlockSpec((s, V_DIM), lambda h, j: (0, h)), rows, rows, tab, tab, tab],
        out_specs=[pl.BlockSpec((tq, 256), lambda h, j: (j, h)),
                   pl.BlockSpec((s, 128), lambda h, j: (0, 0)),
                   pl.BlockSpec((tq, HEAD_PAD), lambda h, j: (j, h))],
        out_shape=[jax.ShapeDtypeStruct((s, N_HEADS * 256), BF),
                   jax.ShapeDtypeStruct((s, 128), F32),
                   jax.ShapeDtypeStruct((s, N_HEADS * HEAD_PAD), BF)],
        scratch_shapes=[pltpu.VMEM((s, HEAD_PAD), F32), pltpu.VMEM((tq, HEAD_PAD), F32),
                        pltpu.VMEM((tq, V_DIM), F32)],
        compiler_params=_params(2))(kv, kv, krr, qr, do, lse, delta, cc, sa, sb)


def adamw(w, g, m, v, *, name, t=256):
    r, c = w.shape
    t = r if r % t else t
    c1 = 1.0 - ADAM_B1 ** ADAM_STEP
    c2 = 1.0 - ADAM_B2 ** ADAM_STEP

    def body(w_ref, g_ref, m_ref, v_ref, d_ref, nm_ref, nv_ref):
        gv = g_ref[...]
        nm = ADAM_B1 * m_ref[...] + (1.0 - ADAM_B1) * gv
        nv = ADAM_B2 * v_ref[...] + (1.0 - ADAM_B2) * (gv * gv)
        nm_ref[...] = nm
        nv_ref[...] = nv
        d_ref[...] = -ADAM_LR * ((nm / c1) / (jnp.sqrt(nv / c2) + ADAM_EPS) + ADAM_WD * w_ref[...])

    blk = pl.BlockSpec((t, c), lambda i: (i, 0))
    return pl.pallas_call(
        body, name=name, grid=(r // t,), in_specs=[blk] * 4, out_specs=[blk] * 3,
        out_shape=[jax.ShapeDtypeStruct((r, c), F32)] * 3, compiler_params=_params(1))(w, g, m, v)


def sum_devices(parts, *, name):
    def body(p_ref, o_ref):
        acc = p_ref[pl.ds(0, SV_ROWS), :]
        for d in range(1, 8):
            acc = acc + p_ref[pl.ds(d * SV_ROWS, SV_ROWS), :]
        o_ref[...] = acc

    return pl.pallas_call(body, name=name, out_shape=jax.ShapeDtypeStruct((SV_ROWS, SV_COLS), F32))(parts)


def add_halves(g, rb, c_idx, *, name, rows):
    nq, r2, cc = rb.shape
    nb = r2 // rows

    def body(c_ref, g_ref, r_ref, o_ref):
        o_ref[...] = (g_ref[...] + r_ref[...]).astype(o_ref.dtype)

    grid_spec = pltpu.PrefetchScalarGridSpec(
        num_scalar_prefetch=1, grid=(nq, nb),
        in_specs=[pl.BlockSpec((None, rows, cc), lambda q, i, c: (q, c[0] * nb + i, 0)),
                  pl.BlockSpec((None, rows, cc), lambda q, i, c: (q, i, 0))],
        out_specs=pl.BlockSpec((None, rows, cc), lambda q, i, c: (q, i, 0)))
    return pl.pallas_call(body, name=name, grid_spec=grid_spec,
                          out_shape=jax.ShapeDtypeStruct((nq, r2, cc), BF),
                          compiler_params=_params(2))(c_idx, g, rb)


def sum_chips(rc, c_idx, *, name, rows):
    nq, r2, cc = rc.shape
    nb = r2 // rows

    def body(c_ref, r_ref, o_ref):
        parts = [r_ref[q].astype(F32) for q in range(4)]
        o_ref[...] = ((parts[0] + parts[1]) + parts[2]) + parts[3]

    grid_spec = pltpu.PrefetchScalarGridSpec(
        num_scalar_prefetch=1, grid=(nb,),
        in_specs=[pl.BlockSpec((nq, rows, cc), lambda i, c: (0, i, 0))],
        out_specs=pl.BlockSpec((rows, cc), lambda i, c: (c[0] * nb + i, 0)))
    return pl.pallas_call(body, name=name, grid_spec=grid_spec,
                          out_shape=jax.ShapeDtypeStruct((2 * r2, cc), F32),
                          compiler_params=_params(1))(c_idx, rc)


def _place():
    return lax.axis_index("x"), lax.axis_index("y"), lax.axis_index("c")


def all_gather8(xs, *, name, own_half):
    m = xs.shape[0] // 2 if own_half else xs.shape[0]
    n = xs.shape[1]

    def body(x_ref, out_ref, send_sems, recv_sems, local_sem):
        x, y, c = _place()
        me, sibling = (x, y, c), (x, y, 1 - c)
        chips = [(1 - x, y), (x, 1 - y), (1 - x, 1 - y)]
        src_own = x_ref.at[pl.ds(c * m, m), :] if own_half else x_ref

        def rows(px, py, pc):
            return out_ref.at[pl.ds((4 * px + 2 * py + pc) * m, m), :]

        def copy(k, block, to, src=None):
            return pltpu.make_async_remote_copy(
                src_ref=rows(*block) if src is None else src, dst_ref=rows(*block),
                send_sem=send_sems.at[k], recv_sem=recv_sems.at[k], device_id=to, device_id_type=MESH)

        mine = pltpu.make_async_copy(src_own, rows(*me), local_sem)
        mine.start()
        first = [copy(0, me, sibling, src=src_own)]
        first += [copy(1 + j, me, (*chip, c), src=src_own) for j, chip in enumerate(chips)]
        for cp in first:
            cp.start()
        passed = [copy(4 + j, (*chip, c), sibling) for j, chip in enumerate(chips)]
        for j, chip in enumerate(chips):
            copy(1 + j, (*chip, c), me).wait_recv()
            passed[j].start()
        copy(0, sibling, me).wait_recv()
        for j, chip in enumerate(chips):
            copy(4 + j, (*chip, 1 - c), me).wait_recv()
        for cp in first + passed:
            cp.wait_send()
        mine.wait()

    return pl.pallas_call(
        body, name=name, out_shape=jax.ShapeDtypeStruct((8 * m, n), xs.dtype),
        in_specs=[pl.BlockSpec(memory_space=pl.ANY)], out_specs=pl.BlockSpec(memory_space=pl.ANY),
        scratch_shapes=[pltpu.SemaphoreType.DMA((7,)), pltpu.SemaphoreType.DMA((7,)), pltpu.SemaphoreType.DMA],
    )(xs)


def _other_chips():
    x, y, c = _place()
    return [(1 - x, y), (x, 1 - y), (1 - x, 1 - y)]


def _remote(src, dst, send_sems, recv_sems, k, to):
    return pltpu.make_async_remote_copy(src_ref=src, dst_ref=dst, send_sem=send_sems.at[k], recv_sem=recv_sems.at[k],
                                        device_id=to, device_id_type=MESH)


def gather_ici(xs):
    r, cc = xs.shape
    m = r // 2

    def copies(ins, outs, ss, rs, landing):
        x, y, c = _place()
        half = pl.ds(c * m, m)
        return [_remote(ins[0].at[half, :], outs[0].at[(2 * cx + cy) if landing else (2 * x + y), half, :],
                        ss, rs, j, (cx, cy, c)) for j, (cx, cy) in enumerate(_other_chips())]

    def start(ins, outs, ss, rs, ls):
        for cp in copies(ins, outs, ss, rs, False):
            cp.start()

    def wait(ins, outs, ss, rs, ls):
        for cp in copies(ins, outs, ss, rs, True):
            cp.wait_recv()
        for cp in copies(ins, outs, ss, rs, False):
            cp.wait_send()

    return Exchange((xs,), (jax.ShapeDtypeStruct((4, r, cc), xs.dtype),), {}, 3, start, wait)


def gather_forward(buf):
    m = buf.shape[1] // 2

    def copies(outs, ss, rs, landing):
        x, y, c = _place()
        half = pl.ds(((1 - c) if landing else c) * m, m)
        return [_remote(outs[0].at[2 * cx + cy, half, :], outs[0].at[2 * cx + cy, half, :], ss, rs, j, (x, y, 1 - c))
                for j, (cx, cy) in enumerate(_other_chips())]

    def start(ins, outs, ss, rs, ls):
        for cp in copies(outs, ss, rs, False):
            cp.start()

    def wait(ins, outs, ss, rs, ls):
        for cp in copies(outs, ss, rs, True):
            cp.wait_recv()
        for cp in copies(outs, ss, rs, False):
            cp.wait_send()

    return Exchange((buf,), (jax.ShapeDtypeStruct(buf.shape, buf.dtype),), {0: 0}, 3, start, wait)


def swap_halves(g):
    nq, r, cc = g.shape
    r2 = r // 2

    def copy(ins, outs, ss, rs):
        x, y, c = _place()
        return _remote(ins[0].at[:, pl.ds((1 - c) * r2, r2), :], outs[0], ss, rs, 0, (x, y, 1 - c))

    def start(ins, outs, ss, rs, ls):
        copy(ins, outs, ss, rs).start()

    def wait(ins, outs, ss, rs, ls):
        copy(ins, outs, ss, rs).wait()

    return Exchange((g,), (jax.ShapeDtypeStruct((nq, r2, cc), g.dtype),), {}, 1, start, wait)


def exchange_chips(p):
    def own(ins, outs, ls):
        x, y, c = _place()
        return pltpu.make_async_copy(ins[0].at[2 * x + y], outs[0].at[2 * x + y], ls)

    def copies(ins, outs, ss, rs, landing):
        x, y, c = _place()
        return [_remote(ins[0].at[2 * cx + cy], outs[0].at[(2 * cx + cy) if landing else (2 * x + y)],
                        ss, rs, j, (cx, cy, c)) for j, (cx, cy) in enumerate(_other_chips())]

    def start(ins, outs, ss, rs, ls):
        own(ins, outs, ls).start()
        for cp in copies(ins, outs, ss, rs, False):
            cp.start()

    def wait(ins, outs, ss, rs, ls):
        for cp in copies(ins, outs, ss, rs, True):
            cp.wait_recv()
        for cp in copies(ins, outs, ss, rs, False):
            cp.wait_send()
        own(ins, outs, ls).wait()

    return Exchange((p,), (jax.ShapeDtypeStruct(p.shape, p.dtype),), {}, 3, start, wait)


def join_halves(tot):
    r2 = tot.shape[0] // 2

    def copy(outs, ss, rs, landing):
        x, y, c = _place()
        half = outs[0].at[pl.ds(((1 - c) if landing else c) * r2, r2), :]
        return _remote(half, half, ss, rs, 0, (x, y, 1 - c))

    def start(ins, outs, ss, rs, ls):
        copy(outs, ss, rs, False).start()

    def wait(ins, outs, ss, rs, ls):
        copy(outs, ss, rs, True).wait_recv()
        copy(outs, ss, rs, False).wait_send()

    return Exchange((tot,), (jax.ShapeDtypeStruct(tot.shape, tot.dtype),), {0: 0}, 1, start, wait)


def _pack_shard(blocks, small_vec=None):
    parts = [w.reshape(-1, PACK_C).astype(BF) for w in blocks]
    if small_vec is not None:
        srow = lax.bitcast_convert_type(small_vec, BF).reshape(1, PACK_C)
        parts.append(jnp.pad(srow, ((0, PACK_PAD - 1), (0, 0))))
    return jnp.concatenate(parts, axis=0)


def _split_rows(a, rows, axis):
    out, off = [], 0
    for n in rows:
        out.append(lax.slice_in_dim(a, off, off + n, axis=axis))
        off += n
    return out


def _unpack_pool(gw):
    p_in, p_grp, p_out = _split_rows(gw, POOL_ROWS, 1)
    return dict(
        pool_w_in=p_in.reshape(4, D_MODEL, 1024).transpose(1, 0, 2).reshape(D_MODEL, 2 * POOL_WIDTH),
        pool_w_group=p_grp.reshape(4, 4, 128, POOL_GROUP).transpose(1, 0, 2, 3).reshape(4, POOL_GROUP, POOL_GROUP),
        pool_w_out=p_out.reshape(POOL_WIDTH, D_MODEL))


def _unpack_mla(gw):
    m_in, m_qb, m_kvb, m_out, small = _split_rows(gw, MLA_ROWS + (PACK_PAD,), 1)
    w = {}
    win = m_in.reshape(4, D_MODEL, 688).transpose(1, 0, 2).reshape(D_MODEL, 2752)
    w["mla_w_in"] = jnp.concatenate(
        [win[:, 384:640], win[:, 640:704], jnp.zeros((D_MODEL, 64), BF), win[:, 0:384], win[:, 704:]], axis=1)
    wq = m_qb.reshape(4, Q_LORA, 768).transpose(1, 0, 2).reshape(Q_LORA, N_HEADS, QK_NOPE + QK_ROPE)
    w["mla_w_q_b"] = jnp.pad(wq, ((0, 0), (0, 0), (0, HEAD_PAD - QK_NOPE - QK_ROPE))).reshape(Q_LORA, N_HEADS * HEAD_PAD)
    w["mla_w_kv_b"] = m_kvb.reshape(4, KV_LORA, 1024).transpose(1, 0, 2).reshape(KV_LORA, 4096)
    w["mla_w_out"] = m_out.reshape(MLA_WIDTH, D_MODEL)
    small = lax.bitcast_convert_type(small[:, 0, :].reshape(4, 512, 2), F32)
    w["mla_norm"] = small[:, :256].reshape(1, D_MODEL)
    w["mla_q_norm"] = small[:, 256:352].reshape(1, Q_LORA)
    w["mla_kv_norm"] = small[:, 352:416].reshape(1, KV_LORA)
    return w


def _pack_pool_grads(g):
    return jnp.concatenate([
        g["pool_w_in"].reshape(D_MODEL, 4, 1024).transpose(1, 0, 2),
        g["pool_w_group"].reshape(4, 4, 128, POOL_GROUP).transpose(1, 0, 2, 3).reshape(4, 256, PACK_C),
        g["pool_w_out"].reshape(4, 512, PACK_C)], axis=1)


def _pack_mla_grads(g):
    return jnp.concatenate([
        g["mla_w_in"].reshape(D_MODEL, 4, 688).transpose(1, 0, 2).reshape(4, 688, PACK_C),
        g["mla_w_q_b"].reshape(Q_LORA, 4, 768).transpose(1, 0, 2).reshape(4, 288, PACK_C),
        g["mla_w_kv_b"].reshape(KV_LORA, 4, 1024).transpose(1, 0, 2),
        g["mla_w_out"].reshape(4, 512, PACK_C),
        jnp.zeros((4, PACK_PAD, PACK_C), F32)], axis=1)


def kernel(x, positions, pool_norm, pool_w_in, pool_w_group, pool_scale, pool_w_out, mla_norm, mla_w_in, mla_q_norm, mla_w_q_b, mla_kv_norm, mla_w_kv_b, mla_w_out, final_norm, loss_target, m_pool_norm, m_pool_w_in, m_pool_w_group, m_pool_scale, m_pool_w_out, m_mla_norm, m_mla_w_in, m_mla_q_norm, m_mla_w_q_b, m_mla_kv_norm, m_mla_w_kv_b, m_mla_w_out, m_final_norm, v_pool_norm, v_pool_w_in, v_pool_w_group, v_pool_scale, v_pool_w_out, v_mla_norm, v_mla_w_in, v_mla_q_norm, v_mla_w_q_b, v_mla_kv_norm, v_mla_w_kv_b, v_mla_w_out, v_final_norm):
    s = x.shape[1]
    tq = min(512, s)
    x0 = x.reshape(s, D_MODEL)
    tgt = loss_target.reshape(s, D_MODEL)
    cx, cy, cc_idx = _place()
    chip = 2 * cx + cy

    big_names = ("pool_w_in", "pool_w_group", "pool_w_out", "mla_w_in", "mla_w_q_b", "mla_w_kv_b", "mla_w_out")
    big_w = dict(zip(big_names, (pool_w_in, pool_w_group, pool_w_out, mla_w_in, mla_w_q_b, mla_w_kv_b, mla_w_out)))
    big_m = dict(zip(big_names, (m_pool_w_in, m_pool_w_group, m_pool_w_out, m_mla_w_in, m_mla_w_q_b, m_mla_w_kv_b, m_mla_w_out)))
    big_v = dict(zip(big_names, (v_pool_w_in, v_pool_w_group, v_pool_w_out, v_mla_w_in, v_mla_w_q_b, v_mla_w_kv_b, v_mla_w_out)))

    small_vec = jnp.concatenate([mla_norm.reshape(-1), mla_q_norm.reshape(-1), mla_kv_norm.reshape(-1),
                                 jnp.zeros((96,), F32)])
    pool_packed = _pack_shard([big_w[n] for n in big_names[:3]])
    mla_packed = _pack_shard([big_w[n] for n in big_names[3:]], small_vec)
    w = _unpack_pool(all_gather8(pool_packed, name="gather_pool_weights", own_half=True).reshape(4, POOL_R, PACK_C))
    g_pool = pool_norm.reshape(1, D_MODEL)
    g_final = final_norm.reshape(1, D_MODEL)
    sc_pool = pool_scale.reshape(1, POOL_WIDTH)

    inv_freq = 1.0 / (ROPE_THETA ** (jnp.arange(0, QK_ROPE, 2, dtype=F32) / QK_ROPE))
    ang = positions.reshape(s).astype(F32)[:, None] * inv_freq
    cos, sin = jnp.cos(ang), jnp.sin(ang)
    z32, z64, z96 = (jnp.zeros((s, n), F32) for n in (32, 64, 96))
    t_cc = jnp.concatenate([cos, cos, z64], axis=1)
    t_sa = jnp.concatenate([-sin, z96], axis=1)
    t_sb = jnp.concatenate([z32, sin, z64], axis=1)

    h0 = norm_fwd(x0, g_pool, col=0, width=D_MODEL, name="pool_norm_fwd")
    uz, mla_land = mm_nn(h0, w["pool_w_in"], name="pool_in_proj", out_dtype=F32, host=gather_ici(mla_packed))
    pd = pool_prep(uz, name="pool_window")
    mm, y1 = pool_mix_gate(pd, w["pool_w_group"], uz, sc_pool, name="pool_group_mix")
    x1, mla_land = mm_nn(y1, w["pool_w_out"], name="pool_out_proj", out_dtype=F32, add=x0,
                         host=gather_forward(mla_land))
    w.update(_unpack_mla(lax.dynamic_update_slice_in_dim(mla_land, mla_packed[None], chip, axis=0)))

    h1 = norm_fwd(x1, w["mla_norm"], col=0, width=D_MODEL, name="mla_norm_fwd")
    proj = mm_nn(h1, w["mla_w_in"], name="mla_in_proj", out_dtype=F32, tn=P_WIDTH // 2)
    qn = norm_fwd(proj, w["mla_q_norm"], col=P_Q, width=Q_LORA, name="mla_q_norm_fwd")
    kvn = norm_fwd(proj, w["mla_kv_norm"], col=P_KV, width=KV_LORA, name="mla_kv_norm_fwd")
    qr = q_proj_rope(qn, w["mla_w_q_b"], t_cc, t_sa, t_sb, name="mla_q_proj")
    kv = mm_nn(kvn, w["mla_w_kv_b"], name="mla_kv_proj", out_dtype=BF, tk=KV_LORA)
    krr = rope_k(proj, t_cc, t_sa, t_sb, name="mla_rope_k")
    o, y2, lse = attn_fwd(qr, kv, krr, proj, name="mla_attn_fwd", tq=tq)
    x2 = mm_nn(y2, w["mla_w_out"], name="mla_out_proj", out_dtype=F32, add=x1)

    dx2, d_final, loss_part = final_loss(x2, g_final, tgt, name="final_norm_loss")

    grads = {}
    grads["mla_w_out"] = mm_tn(y2, dx2, name="mla_out_proj_dw")
    do, dz2, delta = mla_out_dx_gate(dx2, w["mla_w_out"], o, proj, name="mla_out_proj_dx", tq=tq)
    dkv, dkr, dq_pre = attn_bwd(qr, kv, krr, do, lse, delta, t_cc, t_sa, t_sb, name="mla_attn_bwd", tq=tq)
    dkr_pre = unrope_k(dkr, t_cc, t_sa, t_sb, name="mla_unrope_k")
    dqn = mm_nt(dq_pre, w["mla_w_q_b"], name="mla_q_proj_dx", out_dtype=F32, tn=Q_LORA, tk=4096)
    g_qb = mm_tn(qn, dq_pre, name="mla_q_proj_dw", tm=Q_LORA, tn=2048)
    dkvn = mm_nt(dkv, w["mla_w_kv_b"], name="mla_kv_proj_dx", out_dtype=F32, tn=KV_LORA, tk=4096)
    grads["mla_w_kv_b"] = mm_tn(kvn, dkv, name="mla_kv_proj_dw", tm=KV_LORA, tn=2048)
    dq_lat, d_qnorm = norm_bwd(proj, w["mla_q_norm"], dqn, col=P_Q, width=Q_LORA, name="mla_q_norm_bwd", out_dtype=BF)
    dkv_lat, d_kvnorm = norm_bwd(proj, w["mla_kv_norm"], dkvn, col=P_KV, width=KV_LORA, name="mla_kv_norm_bwd", out_dtype=BF)
    dsmall = jnp.concatenate([dkv_lat, dkr_pre, dq_lat], axis=1)
    dh1 = mm_nt(dsmall, w["mla_w_in"], name="mla_in_proj_dx_a", out_dtype=F32, tk=P_SMALL)
    dh1 = mm_nt(dz2, w["mla_w_in"][:, P_Z:], name="mla_in_proj_dx_b", out_dtype=F32, add=dh1)
    g_in_a = mm_tn(h1, dsmall, name="mla_in_proj_dw_a", tn=P_SMALL)
    g_in_b = mm_tn(h1, dz2, name="mla_in_proj_dw_b")
    dx1, d_mnorm = norm_bwd(x1, w["mla_norm"], dh1, col=0, width=D_MODEL, name="mla_norm_bwd", res=dx2)

    g_in = jnp.concatenate([g_in_a, g_in_b], axis=1)
    grads["mla_w_in"] = jnp.concatenate([g_in[:, P_Q:P_Z], g_in[:, P_KV:P_KV + KV_LORA],
                                         g_in[:, P_KR:P_KR + QK_ROPE], g_in[:, P_Z:]], axis=1)
    grads["mla_w_q_b"] = g_qb.reshape(Q_LORA, N_HEADS, HEAD_PAD)[:, :, :QK_NOPE + QK_ROPE].reshape(Q_LORA, -1)
    core_idx = cc_idx.reshape(1).astype(jnp.int32)
    gp_mla = _pack_mla_grads(grads)

    grads["pool_w_out"], sib = mm_tn(y1, dx1, name="pool_out_proj_dw", host=swap_halves(gp_mla))
    pre = add_halves(gp_mla, sib, core_idx, name="mla_grad_add_halves", rows=MLA_R // 2)
    dmm, dz1, d_scale, got = pool_out_dx_gate(dx1, w["pool_w_out"], mm, uz, sc_pool, name="pool_out_proj_dx",
                                              host=exchange_chips(pre))
    tot = sum_chips(got, core_idx, name="mla_grad_sum_chips", rows=MLA_R // 2)
    dpd, red_mla = gmm_nt(dmm, w["pool_w_group"], name="pool_group_mix_dx", host=join_halves(tot))
    grads["pool_w_group"] = gmm_tn(pd, dmm, 4, name="pool_group_mix_dw")
    du = pool_prep_bwd(dpd, name="pool_window_bwd")
    dh0 = mm_nt(du, w["pool_w_in"], name="pool_in_proj_dx_u", out_dtype=F32)
    dh0 = mm_nt(dz1, w["pool_w_in"], name="pool_in_proj_dx_z", out_dtype=F32, b_col=POOL_WIDTH, add=dh0)
    g_pin_u = mm_tn(h0, du, name="pool_in_proj_dw_u")
    g_pin_z = mm_tn(h0, dz1, name="pool_in_proj_dw_z")
    grad_x, d_pnorm = norm_bwd(x0, g_pool, dh0, col=0, width=D_MODEL, name="pool_norm_bwd", res=dx1)
    grads["pool_w_in"] = jnp.concatenate([g_pin_u, g_pin_z], axis=1)

    gp_pool = _pack_pool_grads(grads)
    sib = run_exchange(swap_halves(gp_pool), name="pool_grad_swap_halves")[0]
    pre = add_halves(gp_pool, sib, core_idx, name="pool_grad_add_halves", rows=POOL_R // 2)
    got = run_exchange(exchange_chips(pre), name="pool_grad_exchange_chips")[0]
    tot = sum_chips(got, core_idx, name="pool_grad_sum_chips", rows=POOL_R // 2)
    red_pool = run_exchange(join_halves(tot), name="pool_grad_join_halves")[0]
    red_parts = _split_rows(red_pool, POOL_ROWS, 0) + _split_rows(red_mla, MLA_ROWS, 0)

    sv = jnp.concatenate([d_pnorm.reshape(-1), d_scale.reshape(-1), d_final.reshape(-1), d_mnorm.reshape(-1),
                          d_qnorm.reshape(-1), d_kvnorm.reshape(-1), loss_part[0, :1],
                          jnp.zeros((SV_ROWS * SV_COLS - SV_OFF["loss"] - 1,), F32)]).reshape(SV_ROWS, SV_COLS)
    sv_all = all_gather8(sv, name="gather_small_grads", own_half=False)
    sv_sum = sum_devices(sv_all, name="sum_small_grads").reshape(-1)
    loss = sv_sum[SV_OFF["loss"]]

    def sv_take(key, n):
        return lax.slice_in_dim(sv_sum, SV_OFF[key], SV_OFF[key] + n)

    out_g, out_d, out_m, out_v = {}, {}, {}, {}
    for name, part in zip(big_names, red_parts):
        shp = big_w[name].shape
        g2 = part.reshape(shp)
        two_d = (-1, shp[-1])
        d_, m_, v_ = adamw(big_w[name].reshape(two_d), g2.reshape(two_d), big_m[name].reshape(two_d),
                           big_v[name].reshape(two_d), name="adamw_" + name)
        out_g[name], out_d[name], out_m[name], out_v[name] = g2, d_.reshape(shp), m_.reshape(shp), v_.reshape(shp)

    small = [
        ("pool_norm", pool_norm, m_pool_norm, v_pool_norm, sv_take("pool_norm", 1024)),
        ("pool_scale", pool_scale, m_pool_scale, v_pool_scale, sv_take("pool_scale", 2048)),
        ("final_norm", final_norm, m_final_norm, v_final_norm, sv_take("final_norm", 1024)),
        ("mla_norm", mla_norm, m_mla_norm, v_mla_norm,
         lax.dynamic_slice_in_dim(sv_take("mla_norm", 1024), chip * 256, 256)),
        ("mla_q_norm", mla_q_norm, m_mla_q_norm, v_mla_q_norm,
         lax.dynamic_slice_in_dim(sv_take("q_norm", 384), chip * 96, 96)),
        ("mla_kv_norm", mla_kv_norm, m_mla_kv_norm, v_mla_kv_norm,
         lax.dynamic_slice_in_dim(sv_take("kv_norm", 256), chip * 64, 64)),
    ]
    sw = jnp.concatenate([t[1].reshape(-1) for t in small] + [jnp.zeros((96,), F32)]).reshape(1, -1)
    sm = jnp.concatenate([t[2].reshape(-1) for t in small] + [jnp.zeros((96,), F32)]).reshape(1, -1)
    s_v = jnp.concatenate([t[3].reshape(-1) for t in small] + [jnp.ones((96,), F32)]).reshape(1, -1)
    sg = jnp.concatenate([t[4].reshape(-1) for t in small] + [jnp.zeros((96,), F32)]).reshape(1, -1)
    sd_, sm_, sv_ = adamw(sw, sg, sm, s_v, name="adamw_vectors")
    off = 0
    for name, wt, _, _, gvec in small:
        n = gvec.shape[0]
        shp = wt.shape
        out_g[name] = gvec.reshape(shp)
        out_d[name] = sd_[0, off:off + n].reshape(shp)
        out_m[name] = sm_[0, off:off + n].reshape(shp)
        out_v[name] = sv_[0, off:off + n].reshape(shp)
        off += n

    order = ("pool_norm", "pool_w_in", "pool_w_group", "pool_scale", "pool_w_out", "mla_norm", "mla_w_in",
             "mla_q_norm", "mla_w_q_b", "mla_kv_norm", "mla_w_kv_b", "mla_w_out", "final_norm")
    return (loss, grad_x.reshape(x.shape), *[out_g[n] for n in order], *[out_d[n] for n in order],
            *[out_m[n] for n in order], *[out_v[n] for n in order])
```

```python
import functools
from typing import Callable, NamedTuple

import jax
import jax.numpy as jnp
from jax import lax
from jax.experimental import pallas as pl
from jax.experimental.pallas import tpu as pltpu

F32 = jnp.float32
BF = jnp.bfloat16
MESH = pl.DeviceIdType.MESH

D_MODEL = 1024
POOL_WIDTH = 2048
POOL_WINDOWS = (2, 4, 8, 16)
POOL_GROUP = 512
HALO = 16
N_HEADS = 16
QK_NOPE = 128
QK_ROPE = 64
V_DIM = 128
HEAD_PAD = 256
Q_LORA = 384
KV_LORA = 256
MLA_WIDTH = 2048
ROPE_THETA = 10000.0
EPS = 1e-6
SCALE = (QK_NOPE + QK_ROPE) ** -0.5
SCALE_LOG2E = SCALE * 1.4426950408889634
NEG = -1e30

P_KV, P_KR, P_Q, P_Z = 0, 256, 384, 768
P_SMALL = 768
P_WIDTH = 2816

ADAM_LR = 0.001
ADAM_B1 = 0.9
ADAM_B2 = 0.999
ADAM_EPS = 1e-08
ADAM_WD = 0.01
ADAM_STEP = 10

NN = (((1,), (0,)), ((), ()))
NT = (((1,), (1,)), ((), ()))
TN = (((0,), (0,)), ((), ()))

POOL_ROWS = (1024, 256, 512)
MLA_ROWS = (688, 288, 256, 512)
PACK_PAD = 16
POOL_R = sum(POOL_ROWS)
MLA_R = sum(MLA_ROWS) + PACK_PAD
PACK_C = 1024
SV_OFF = dict(pool_norm=0, pool_scale=1024, final_norm=3072, mla_norm=4096, q_norm=5120, kv_norm=5504, loss=5760)
SV_ROWS, SV_COLS = 8, 768

VMEM_LIMIT = 56 * 1024 * 1024


def _params(n_axes, vmem=None):
    return pltpu.CompilerParams(dimension_semantics=("arbitrary",) * n_axes,
                                vmem_limit_bytes=VMEM_LIMIT if vmem is None else vmem)


def _sigmoid(z):
    return 1.0 / (1.0 + jnp.exp(-z))


class Exchange(NamedTuple):
    operands: tuple
    out_shapes: tuple
    aliases: dict
    n_sems: int
    start: Callable
    wait: Callable


HBM_SPEC = pl.BlockSpec(memory_space=pl.ANY)


def _exchange_scratch(ex):
    return [pltpu.SemaphoreType.DMA((ex.n_sems,)), pltpu.SemaphoreType.DMA((ex.n_sems,)), pltpu.SemaphoreType.DMA]


def run_exchange(ex, *, name):
    n_in, n_out = len(ex.operands), len(ex.out_shapes)

    def body(*refs):
        args = (refs[:n_in], refs[n_in:n_in + n_out]) + tuple(refs[n_in + n_out:])
        ex.start(*args)
        ex.wait(*args)

    return pl.pallas_call(
        body, name=name, out_shape=list(ex.out_shapes), in_specs=[HBM_SPEC] * n_in,
        out_specs=[HBM_SPEC] * n_out, scratch_shapes=_exchange_scratch(ex),
        input_output_aliases=dict(ex.aliases))(*ex.operands)


def _call(core, *, name, grid, in_specs, out_specs, out_shape, args, scratch=(), host=None):
    in_specs, out_specs, out_shape = list(in_specs), list(out_specs), list(out_shape)
    params = _params(len(grid))
    if host is None:
        return pl.pallas_call(core, name=name, grid=grid, in_specs=in_specs, out_specs=out_specs,
                              out_shape=out_shape, scratch_shapes=list(scratch), compiler_params=params)(*args)
    n_in, n_out = len(in_specs), len(out_specs)
    n_hin, n_hout = len(host.operands), len(host.out_shapes)

    def body(*refs):
        ins, refs = refs[:n_in], refs[n_in:]
        h_in, refs = refs[:n_hin], refs[n_hin:]
        outs, refs = refs[:n_out], refs[n_out:]
        h_out, refs = refs[:n_hout], refs[n_hout:]
        own_scratch, sems = refs[:-3], refs[-3:]
        ids = [pl.program_id(ax) for ax in range(len(grid))]
        first = functools.reduce(jnp.logical_and, [i == 0 for i in ids])
        last = functools.reduce(jnp.logical_and, [i == n - 1 for i, n in zip(ids, grid)])

        @pl.when(first)
        def _():
            host.start(h_in, h_out, *sems)

        core(*ins, *outs, *own_scratch)

        @pl.when(last)
        def _():
            host.wait(h_in, h_out, *sems)

    return pl.pallas_call(
        body, name=name, grid=grid, in_specs=in_specs + [HBM_SPEC] * n_hin,
        out_specs=out_specs + [HBM_SPEC] * n_hout, out_shape=out_shape + list(host.out_shapes),
        scratch_shapes=list(scratch) + _exchange_scratch(host),
        input_output_aliases={n_in + i: n_out + o for i, o in host.aliases.items()},
        compiler_params=params)(*args, *host.operands)


def _mm(a, b, *, dims, grid, a_spec, b_spec, o_spec, out_shape, out_dtype, acc_shape, name,
        add=None, add_spec=None, host=None):
    nk = grid[-1]
    kax = len(grid) - 1

    def body(*refs):
        if add is None:
            a_ref, b_ref, o_ref = refs[:3]
            add_ref = None
            rest = refs[3:]
        else:
            a_ref, b_ref, add_ref, o_ref = refs[:4]
            rest = refs[4:]
        part = lax.dot_general(a_ref[...].astype(BF), b_ref[...].astype(BF), dims,
                               preferred_element_type=F32)

        def finish(r):
            if add_ref is not None:
                r = r + add_ref[...]
            o_ref[...] = r.astype(o_ref.dtype)

        if nk == 1:
            finish(part)
        else:
            acc = rest[0]
            k = pl.program_id(kax)

            @pl.when(k == 0)
            def _():
                acc[...] = part

            @pl.when(k > 0)
            def _():
                acc[...] += part

            @pl.when(k == nk - 1)
            def _():
                finish(acc[...])

    in_specs = [a_spec, b_spec]
    args = [a, b]
    if add is not None:
        in_specs.append(add_spec)
        args.append(add)
    out = _call(body, name=name, grid=grid, in_specs=in_specs, out_specs=[o_spec],
                out_shape=[jax.ShapeDtypeStruct(out_shape, out_dtype)], args=args,
                scratch=[] if nk == 1 else [pltpu.VMEM(acc_shape, F32)], host=host)
    return out[0] if host is None else out


def _pick(n, t):
    t = min(n, t)
    assert n % t == 0, (n, t)
    return t


def mm_nn(a, b, *, name, out_dtype, add=None, tm=1024, tn=1024, tk=2048, host=None):
    m = a.shape[0]
    kk, n = b.shape
    tm, tn, tk = _pick(m, tm), _pick(n, tn), _pick(kk, tk)
    return _mm(a, b, dims=NN, grid=(m // tm, n // tn, kk // tk),
               a_spec=pl.BlockSpec((tm, tk), lambda i, j, k: (i, k)),
               b_spec=pl.BlockSpec((tk, tn), lambda i, j, k: (k, j)),
               o_spec=pl.BlockSpec((tm, tn), lambda i, j, k: (i, j)),
               add=add, add_spec=pl.BlockSpec((tm, tn), lambda i, j, k: (i, j)),
               out_shape=(m, n), out_dtype=out_dtype, acc_shape=(tm, tn), name=name, host=host)


def mm_nt(a, b, *, name, out_dtype, b_col=0, add=None, tm=1024, tn=1024, tk=2048, host=None):
    m, kk = a.shape
    n = b.shape[0]
    tm, tn, tk = _pick(m, tm), _pick(n, tn), _pick(kk, tk)
    assert b_col % tk == 0
    ko = b_col // tk
    return _mm(a, b, dims=NT, grid=(m // tm, n // tn, kk // tk),
               a_spec=pl.BlockSpec((tm, tk), lambda i, j, k: (i, k)),
               b_spec=pl.BlockSpec((tn, tk), lambda i, j, k: (j, ko + k)),
               o_spec=pl.BlockSpec((tm, tn), lambda i, j, k: (i, j)),
               add=add, add_spec=pl.BlockSpec((tm, tn), lambda i, j, k: (i, j)),
               out_shape=(m, n), out_dtype=out_dtype, acc_shape=(tm, tn), name=name, host=host)


def mm_tn(a, b, *, name, tm=1024, tn=1024, tk=2048, host=None):
    s, m = a.shape
    n = b.shape[1]
    tm, tn, tk = _pick(m, tm), _pick(n, tn), _pick(s, tk)
    return _mm(a, b, dims=TN, grid=(m // tm, n // tn, s // tk),
               a_spec=pl.BlockSpec((tk, tm), lambda i, j, k: (k, i)),
               b_spec=pl.BlockSpec((tk, tn), lambda i, j, k: (k, j)),
               o_spec=pl.BlockSpec((tm, tn), lambda i, j, k: (i, j)),
               out_shape=(m, n), out_dtype=F32, acc_shape=(tm, tn), name=name, host=host)


def gmm_nt(a, w, *, name, tm=1024, host=None):
    s = a.shape[0]
    g, kk, n = w.shape
    tm = _pick(s, tm)
    return _mm(a, w, dims=NT, grid=(s // tm, g, 1),
               a_spec=pl.BlockSpec((tm, n), lambda i, gi, k: (i, gi)),
               b_spec=pl.BlockSpec((None, kk, n), lambda i, gi, k: (gi, 0, 0)),
               o_spec=pl.BlockSpec((tm, kk), lambda i, gi, k: (i, gi)),
               out_shape=(s, g * kk), out_dtype=F32, acc_shape=(tm, kk), name=name, host=host)


def gmm_tn(a, b, g, *, name, tk=2048):
    s = a.shape[0]
    kk, n = a.shape[1] // g, b.shape[1] // g
    tk = _pick(s, tk)
    return _mm(a, b, dims=TN, grid=(g, s // tk),
               a_spec=pl.BlockSpec((tk, kk), lambda gi, k: (k, gi)),
               b_spec=pl.BlockSpec((tk, n), lambda gi, k: (k, gi)),
               o_spec=pl.BlockSpec((None, kk, n), lambda gi, k: (gi, 0, 0)),
               out_shape=(g, kk, n), out_dtype=F32, acc_shape=(kk, n), name=name)


def norm_fwd(x, g, *, col, width, name, t=512):
    s = x.shape[0]
    t = _pick(s, t)
    cb = col // width
    assert col % width == 0

    def body(x_ref, g_ref, o_ref):
        xv = x_ref[...]
        inv = lax.rsqrt(jnp.mean(xv * xv, axis=-1, keepdims=True) + EPS)
        o_ref[...] = ((xv * inv) * g_ref[...]).astype(o_ref.dtype)

    return pl.pallas_call(
        body, name=name, grid=(s // t,),
        in_specs=[pl.BlockSpec((t, width), lambda i: (i, cb)), pl.BlockSpec((1, width), lambda i: (0, 0))],
        out_specs=pl.BlockSpec((t, width), lambda i: (i, 0)),
        out_shape=jax.ShapeDtypeStruct((s, width), BF), compiler_params=_params(1))(x, g)


def _accumulate(ref, part):
    @pl.when(pl.program_id(0) == 0)
    def _():
        ref[...] = part

    @pl.when(pl.program_id(0) > 0)
    def _():
        ref[...] += part


def norm_bwd(x, g, dh, *, col, width, name, out_dtype, t=512):
    s = x.shape[0]
    t = _pick(s, t)
    cb = col // width
    assert col % width == 0

    def body(x_ref, g_ref, dh_ref, dx_ref, dg_ref):
        xv = x_ref[...]
        inv = lax.rsqrt(jnp.mean(xv * xv, axis=-1, keepdims=True) + EPS)
        xhat = xv * inv
        dh_v = dh_ref[...]
        _accumulate(dg_ref, jnp.sum(dh_v * xhat, axis=0, keepdims=True))
        dxhat = dh_v * g_ref[...]
        dx = inv * (dxhat - xhat * jnp.mean(dxhat * xhat, axis=-1, keepdims=True))
        dx_ref[...] = dx.astype(dx_ref.dtype)

    row = pl.BlockSpec((t, width), lambda i: (i, 0))
    vec = pl.BlockSpec((1, width), lambda i: (0, 0))
    return pl.pallas_call(
        body, name=name, grid=(s // t,), in_specs=[pl.BlockSpec((t, width), lambda i: (i, cb)), vec, row],
        out_specs=[row, vec],
        out_shape=[jax.ShapeDtypeStruct((s, width), out_dtype), jax.ShapeDtypeStruct((1, width), F32)],
        compiler_params=_params(1))(x, g, dh)


def mm_nt_norm_bwd(a, b, add, x, g, res, *, name, b_col=0, tm=512, host=None):
    s, kk = a.shape
    d = b.shape[0]
    tm = _pick(s, tm)
    assert b_col % kk == 0

    def body(a_ref, b_ref, add_ref, x_ref, g_ref, res_ref, dx_ref, dg_ref):
        dh = lax.dot_general(a_ref[...].astype(BF), b_ref[...].astype(BF), NT,
                             preferred_element_type=F32) + add_ref[...]
        xv = x_ref[...]
        inv = lax.rsqrt(jnp.mean(xv * xv, axis=-1, keepdims=True) + EPS)
        xhat = xv * inv
        _accumulate(dg_ref, jnp.sum(dh * xhat, axis=0, keepdims=True))
        dxhat = dh * g_ref[...]
        dx_ref[...] = inv * (dxhat - xhat * jnp.mean(dxhat * xhat, axis=-1, keepdims=True)) + res_ref[...]

    row = pl.BlockSpec((tm, d), lambda i: (i, 0))
    vec = pl.BlockSpec((1, d), lambda i: (0, 0))
    return _call(
        body, name=name, grid=(s // tm,),
        in_specs=[pl.BlockSpec((tm, kk), lambda i: (i, 0)), pl.BlockSpec((d, kk), lambda i: (0, b_col // kk)),
                  row, row, vec, row],
        out_specs=[row, vec],
        out_shape=[jax.ShapeDtypeStruct((s, d), F32), jax.ShapeDtypeStruct((1, d), F32)],
        args=[a, b, add, x, g, res], host=host)


def mm_nn_loss(a, b, add, gf, tgt, *, name, tm=512):
    s, kk = a.shape
    d = b.shape[1]
    tm = _pick(s, tm)

    def body(a_ref, b_ref, add_ref, g_ref, t_ref, dx_ref, dg_ref, loss_ref):
        xv = jnp.dot(a_ref[...].astype(BF), b_ref[...].astype(BF), preferred_element_type=F32) + add_ref[...]
        inv = lax.rsqrt(jnp.mean(xv * xv, axis=-1, keepdims=True) + EPS)
        xhat = xv * inv
        gv = g_ref[...]
        diff = xhat * gv - t_ref[...]
        row_err = jnp.mean(diff * diff, axis=-1, keepdims=True)
        _accumulate(loss_ref, jnp.broadcast_to(0.5 * jnp.sum(row_err, axis=0, keepdims=True), (1, 128)))
        dout = diff * (1.0 / d)
        _accumulate(dg_ref, jnp.sum(dout * xhat, axis=0, keepdims=True))
        dxhat = dout * gv
        dx_ref[...] = inv * (dxhat - xhat * jnp.mean(dxhat * xhat, axis=-1, keepdims=True))

    row = pl.BlockSpec((tm, d), lambda i: (i, 0))
    vec = pl.BlockSpec((1, d), lambda i: (0, 0))
    return _call(
        body, name=name, grid=(s // tm,),
        in_specs=[pl.BlockSpec((tm, kk), lambda i: (i, 0)), pl.BlockSpec((kk, d), lambda i: (0, 0)), row, vec, row],
        out_specs=[row, vec, pl.BlockSpec((1, 128), lambda i: (0, 0))],
        out_shape=[jax.ShapeDtypeStruct((s, d), F32), jax.ShapeDtypeStruct((1, d), F32),
                   jax.ShapeDtypeStruct((1, 128), F32)],
        args=[a, b, add, gf, tgt])


def pool_prep(uz, *, name, t=256):
    s = uz.shape[0]
    t = _pick(s, t)
    hb = t // HALO

    def body(u_ref, halo_ref, o_ref, buf):
        i = pl.program_id(0)
        buf[pl.ds(HALO, t), :] = u_ref[...]

        @pl.when(i == 0)
        def _():
            buf[pl.ds(0, HALO), :] = jnp.zeros((HALO, POOL_WIDTH), F32)

        @pl.when(i > 0)
        def _():
            buf[pl.ds(0, HALO), :] = halo_ref[...]

        pos = i * t + lax.broadcasted_iota(jnp.int32, (t, POOL_GROUP), 0)
        for g, w in enumerate(POOL_WINDOWS):
            cols = pl.ds(g * POOL_GROUP, POOL_GROUP)
            cur = buf[pl.ds(HALO, t), cols]
            acc = cur
            for k in range(1, w):
                acc = acc + buf[pl.ds(HALO - k, t), cols]
            cnt = jnp.minimum(pos + 1, w).astype(F32)
            o_ref[:, cols] = (acc / cnt - cur).astype(o_ref.dtype)

    return pl.pallas_call(
        body, name=name, grid=(s // t,),
        in_specs=[pl.BlockSpec((t, POOL_WIDTH), lambda i: (i, 0)),
                  pl.BlockSpec((HALO, POOL_WIDTH), lambda i: (jnp.maximum(i * hb - 1, 0), 0))],
        out_specs=pl.BlockSpec((t, POOL_WIDTH), lambda i: (i, 0)),
        out_shape=jax.ShapeDtypeStruct((s, POOL_WIDTH), BF),
        scratch_shapes=[pltpu.VMEM((t + HALO, POOL_WIDTH), F32)],
        compiler_params=_params(1))(uz, uz)


def pool_prep_bwd(dpd, *, name, t=256):
    s = dpd.shape[0]
    t = _pick(s, t)
    hb = t // HALO
    n = s // t

    def body(d_ref, halo_ref, o_ref, buf):
        i = pl.program_id(0)
        pos = i * t + lax.broadcasted_iota(jnp.int32, (t, POOL_GROUP), 0)
        for g, w in enumerate(POOL_WINDOWS):
            cols = pl.ds(g * POOL_GROUP, POOL_GROUP)
            cnt = jnp.minimum(pos + 1, w).astype(F32)
            buf[pl.ds(0, t), cols] = d_ref[:, cols] / cnt

            @pl.when(i < n - 1)
            def _():
                buf[pl.ds(t, HALO), cols] = halo_ref[:, cols] / float(w)

            @pl.when(i == n - 1)
            def _():
                buf[pl.ds(t, HALO), cols] = jnp.zeros((HALO, POOL_GROUP), F32)

        for g, w in enumerate(POOL_WINDOWS):
            cols = pl.ds(g * POOL_GROUP, POOL_GROUP)
            acc = buf[pl.ds(0, t), cols]
            for k in range(1, w):
                acc = acc + buf[pl.ds(k, t), cols]
            o_ref[:, cols] = (acc - d_ref[:, cols]).astype(o_ref.dtype)

    return pl.pallas_call(
        body, name=name, grid=(n,),
        in_specs=[pl.BlockSpec((t, POOL_WIDTH), lambda i: (i, 0)),
                  pl.BlockSpec((HALO, POOL_WIDTH), lambda i: (jnp.minimum((i + 1) * hb, n * hb - 1), 0))],
        out_specs=pl.BlockSpec((t, POOL_WIDTH), lambda i: (i, 0)),
        out_shape=jax.ShapeDtypeStruct((s, POOL_WIDTH), BF),
        scratch_shapes=[pltpu.VMEM((t + HALO, POOL_WIDTH), F32)],
        compiler_params=_params(1))(dpd, dpd)


CHUNK = 512


def _chunks(width, step=CHUNK):
    return [slice(c, c + step) for c in range(0, width, step)]


def pool_mix_gate(pd, wg, uz, scale, *, name, tm=1024):
    s = pd.shape[0]
    g = wg.shape[0]
    tm = _pick(s, tm)

    def body(a_ref, w_ref, z_ref, sc_ref, mm_ref, y_ref):
        mm = jnp.dot(a_ref[...], w_ref[...], preferred_element_type=F32)
        mm_ref[...] = mm
        z = z_ref[...]
        y_ref[...] = ((mm * sc_ref[...]) * (z * _sigmoid(z))).astype(y_ref.dtype)

    blk = pl.BlockSpec((tm, POOL_GROUP), lambda i, gi: (i, gi))
    return pl.pallas_call(
        body, name=name, grid=(s // tm, g),
        in_specs=[blk, pl.BlockSpec((None, POOL_GROUP, POOL_GROUP), lambda i, gi: (gi, 0, 0)),
                  pl.BlockSpec((tm, POOL_GROUP), lambda i, gi: (i, g + gi)),
                  pl.BlockSpec((1, POOL_GROUP), lambda i, gi: (0, gi))],
        out_specs=[blk, blk],
        out_shape=[jax.ShapeDtypeStruct((s, POOL_WIDTH), F32), jax.ShapeDtypeStruct((s, POOL_WIDTH), BF)],
        compiler_params=_params(2))(pd, wg, uz, scale)


def pool_out_dx_gate(dx, w_out, mm, uz, scale, *, name, tm=512, host=None):
    s, d = dx.shape
    tm = _pick(s, tm)

    def body(dx_ref, w_ref, mm_ref, z_ref, sc_ref, dmm_ref, dz_ref, dsc_ref):
        dxv = dx_ref[...].astype(BF)
        parts = []
        for c in _chunks(POOL_WIDTH):
            dyv = lax.dot_general(dxv, w_ref[c, :], NT, preferred_element_type=F32)
            z = z_ref[:, c]
            sig = _sigmoid(z)
            mmv = mm_ref[:, c]
            scv = sc_ref[:, c]
            dmixed = dyv * (z * sig)
            dmm_ref[:, c] = (dmixed * scv).astype(dmm_ref.dtype)
            dz_ref[:, c] = (dyv * (mmv * scv) * (sig * (1.0 + z * (1.0 - sig)))).astype(dz_ref.dtype)
            parts.append(jnp.sum(dmixed * mmv, axis=0, keepdims=True))

        @pl.when(pl.program_id(0) == 0)
        def _():
            for c, part in zip(_chunks(POOL_WIDTH), parts):
                dsc_ref[:, c] = part

        @pl.when(pl.program_id(0) > 0)
        def _():
            for c, part in zip(_chunks(POOL_WIDTH), parts):
                dsc_ref[:, c] += part

    blk = pl.BlockSpec((tm, POOL_WIDTH), lambda i: (i, 0))
    vec = pl.BlockSpec((1, POOL_WIDTH), lambda i: (0, 0))
    return _call(
        body, name=name, grid=(s // tm,),
        in_specs=[pl.BlockSpec((tm, d), lambda i: (i, 0)), pl.BlockSpec((POOL_WIDTH, d), lambda i: (0, 0)),
                  blk, pl.BlockSpec((tm, POOL_WIDTH), lambda i: (i, 1)), vec],
        out_specs=[blk, blk, vec],
        out_shape=[jax.ShapeDtypeStruct((s, POOL_WIDTH), BF), jax.ShapeDtypeStruct((s, POOL_WIDTH), BF),
                   jax.ShapeDtypeStruct((1, POOL_WIDTH), F32)],
        args=[dx, w_out, mm, uz, scale], host=host)


def _rope(a, cc, sa, sb):
    return a * cc + pltpu.roll(a, 96, 1) * sa + pltpu.roll(a, 32, 1) * sb


def _unrope(d, cc, sa, sb):
    return d * cc + pltpu.roll(d * sa, 32, 1) + pltpu.roll(d * sb, 96, 1)


def q_proj_rope(qn, wq, cc, sa, sb, *, name, tm=1024, heads=4):
    s, kk = qn.shape
    tm = _pick(s, tm)
    tn = heads * HEAD_PAD

    def body(a_ref, b_ref, cc_ref, sa_ref, sb_ref, o_ref):
        q = jnp.dot(a_ref[...], b_ref[...], preferred_element_type=F32)
        for h in range(heads):
            nope = slice(h * HEAD_PAD, h * HEAD_PAD + QK_NOPE)
            rope = slice(h * HEAD_PAD + QK_NOPE, (h + 1) * HEAD_PAD)
            o_ref[:, nope] = q[:, nope].astype(o_ref.dtype)
            o_ref[:, rope] = _rope(q[:, rope], cc_ref[...], sa_ref[...], sb_ref[...]).astype(o_ref.dtype)

    tab = pl.BlockSpec((tm, 128), lambda i, j: (i, 0))
    return pl.pallas_call(
        body, name=name, grid=(s // tm, N_HEADS // heads),
        in_specs=[pl.BlockSpec((tm, kk), lambda i, j: (i, 0)), pl.BlockSpec((kk, tn), lambda i, j: (0, j)),
                  tab, tab, tab],
        out_specs=pl.BlockSpec((tm, tn), lambda i, j: (i, j)),
        out_shape=jax.ShapeDtypeStruct((s, N_HEADS * HEAD_PAD), BF), compiler_params=_params(2))(qn, wq, cc, sa, sb)


def rope_k(proj, cc, sa, sb, *, name, t=512):
    s = proj.shape[0]
    t = _pick(s, t)
    kr_blk = P_KR // 128

    def body(kr_ref, cc_ref, sa_ref, sb_ref, o_ref):
        o_ref[...] = _rope(kr_ref[...], cc_ref[...], sa_ref[...], sb_ref[...]).astype(o_ref.dtype)

    tab = pl.BlockSpec((t, 128), lambda i: (i, 0))
    return pl.pallas_call(
        body, name=name, grid=(s // t,),
        in_specs=[pl.BlockSpec((t, 128), lambda i: (i, kr_blk)), tab, tab, tab], out_specs=tab,
        out_shape=jax.ShapeDtypeStruct((s, 128), BF), compiler_params=_params(1))(proj, cc, sa, sb)


def unrope_k(dkr, cc, sa, sb, *, name, t=512):
    s = dkr.shape[0]
    t = _pick(s, t)

    def body(d_ref, cc_ref, sa_ref, sb_ref, o_ref):
        o_ref[...] = _unrope(d_ref[...], cc_ref[...], sa_ref[...], sb_ref[...]).astype(o_ref.dtype)

    tab = pl.BlockSpec((t, 128), lambda i: (i, 0))
    return pl.pallas_call(
        body, name=name, grid=(s // t,), in_specs=[tab, tab, tab, tab], out_specs=tab,
        out_shape=jax.ShapeDtypeStruct((s, 128), BF), compiler_params=_params(1))(dkr, cc, sa, sb)


def mla_out_dx_gate(dx, w_out, o, proj, *, name, tq):
    s, d = dx.shape
    nq = s // tq

    def body(dx_ref, w_ref, o_ref, p_ref, do_ref, dz_ref, dl_ref):
        dxv = dx_ref[...].astype(BF)
        for c in _chunks(MLA_WIDTH):
            dy_c = lax.dot_general(dxv, w_ref[c, :], NT, preferred_element_type=F32)
            for h in range(c.start // V_DIM, c.stop // V_DIM):
                hc = slice(h * V_DIM, (h + 1) * V_DIM)
                z = p_ref[:, slice(P_Z + hc.start, P_Z + hc.stop)]
                sig = _sigmoid(z)
                dyv = dy_c[:, hc.start - c.start:hc.stop - c.start]
                ov = o_ref[:, hc]
                dov = dyv * (z * sig)
                do_ref[:, hc] = dov.astype(do_ref.dtype)
                dz_ref[:, hc] = (dyv * ov * (sig * (1.0 + z * (1.0 - sig)))).astype(dz_ref.dtype)
                delta = jnp.sum(dov * ov, axis=-1, keepdims=True)
                dl_ref[h] = jnp.broadcast_to(delta, (tq, 128)).T[:8, :]

    blk = pl.BlockSpec((tq, MLA_WIDTH), lambda i: (i, 0))
    return pl.pallas_call(
        body, name=name, grid=(nq,),
        in_specs=[pl.BlockSpec((tq, d), lambda i: (i, 0)), pl.BlockSpec((MLA_WIDTH, d), lambda i: (0, 0)),
                  blk, pl.BlockSpec((tq, P_WIDTH), lambda i: (i, 0))],
        out_specs=[blk, blk, pl.BlockSpec((N_HEADS, None, 8, tq), lambda i: (0, i, 0, 0))],
        out_shape=[jax.ShapeDtypeStruct((s, MLA_WIDTH), BF), jax.ShapeDtypeStruct((s, MLA_WIDTH), BF),
                   jax.ShapeDtypeStruct((N_HEADS, nq, 8, tq), F32)],
        compiler_params=_params(1))(dx, w_out, o, proj)


FWD_GROUPS = (4, 2, 1)
BWD_GROUPS = (4, 2, 1)


def _for_groups(first, count, groups, fn):
    lead = groups[-1]
    for g in groups[:-1][::-1]:
        lead = jnp.where(count >= g, g, lead)
    for g in groups:
        @pl.when(lead == g)
        def _(g=g):
            fn(first, g, True)
    first = first + lead
    count = count - lead
    for g in groups:
        n = count // g

        def one(p, carry, g=g, first=first):
            fn(first + p * g, g, False)
            return carry

        lax.fori_loop(0, n, one, 0)
        first = first + n * g
        count = count - n * g


def attn_fwd(qr, kv, krr, proj, *, name, tq):
    s = qr.shape[0]
    nq = s // tq
    z_blk = P_Z // V_DIM

    def body(kn_ref, v_ref, kr_ref, q_ref, z_ref, o_ref, y_ref, lse_ref, acc_sc, m_sc):
        j = pl.program_id(1)

        @pl.when(j == 0)
        def _():
            acc_sc[...] = jnp.zeros((s, 2 * V_DIM), F32)
            m_sc[...] = jnp.full((nq, 8, tq), NEG, F32)

        k = jnp.concatenate([kn_ref[...], kr_ref[...]], axis=1)
        vx = jnp.concatenate([v_ref[...], jnp.ones((tq, V_DIM), BF)], axis=1)

        def update(i, n_tiles, masked):
            rows = pl.ds(pl.multiple_of(i * tq, tq), n_tiles * tq)
            st = lax.dot_general(k, q_ref[rows, :], NT, preferred_element_type=F32) * SCALE_LOG2E
            if masked:
                krow = lax.broadcasted_iota(jnp.int32, (tq, n_tiles * tq), 0)
                qcol = lax.broadcasted_iota(jnp.int32, (tq, n_tiles * tq), 1)
                st = jnp.where(qcol >= krow, st, NEG)
            m_prev = jnp.concatenate([m_sc[i + n, pl.ds(0, 1), :] for n in range(n_tiles)], axis=1)
            m_new = jnp.maximum(m_prev, jnp.max(st, axis=0, keepdims=True))
            alpha_c = jnp.broadcast_to(jnp.exp2(m_prev - m_new), (128, n_tiles * tq)).T
            pt = jnp.exp2(st - m_new).astype(BF)
            pv = lax.dot_general(pt, vx, TN, preferred_element_type=F32)
            for cols in (slice(0, V_DIM), slice(V_DIM, 2 * V_DIM)):
                acc_sc[rows, cols] = alpha_c * acc_sc[rows, cols] + pv[:, cols]
            for n in range(n_tiles):
                m_sc[i + n, pl.ds(0, 1), :] = m_new[:, n * tq:(n + 1) * tq]

        _for_groups(j, nq - j, FWD_GROUPS, update)
        mine = pl.ds(pl.multiple_of(j * tq, tq), tq)
        l = acc_sc[mine, V_DIM:]
        o = acc_sc[mine, :V_DIM] / l
        o_ref[...] = o
        z = z_ref[...]
        y_ref[...] = (o * (z * _sigmoid(z))).astype(y_ref.dtype)
        lse_ref[...] = jnp.broadcast_to(m_sc[j, pl.ds(0, 1), :], (8, tq)) + jnp.log2(l).T[:8, :]

    tile = pl.BlockSpec((tq, V_DIM), lambda h, j: (j, h))
    return pl.pallas_call(
        body, name=name, grid=(N_HEADS, nq),
        in_specs=[pl.BlockSpec((tq, QK_NOPE), lambda h, j: (j, 2 * h)),
                  pl.BlockSpec((tq, V_DIM), lambda h, j: (j, 2 * h + 1)),
                  pl.BlockSpec((tq, 128), lambda h, j: (j, 0)),
                  pl.BlockSpec((s, HEAD_PAD), lambda h, j: (0, h)),
                  pl.BlockSpec((tq, V_DIM), lambda h, j: (j, z_blk + h))],
        out_specs=[tile, tile, pl.BlockSpec((None, None, 8, tq), lambda h, j: (h, j, 0, 0))],
        out_shape=[jax.ShapeDtypeStruct((s, N_HEADS * V_DIM), F32),
                   jax.ShapeDtypeStruct((s, N_HEADS * V_DIM), BF),
                   jax.ShapeDtypeStruct((N_HEADS, nq, 8, tq), F32)],
        scratch_shapes=[pltpu.VMEM((s, 2 * V_DIM), F32), pltpu.VMEM((nq, 8, tq), F32)],
        compiler_params=_params(2))(kv, kv, krr, qr, proj)


def attn_bwd(qr, kv, krr, do, lse, delta, cc, sa, sb, *, name, tq):
    s = qr.shape[0]
    nq = s // tq

    def body(kn_ref, v_ref, kr_ref, q_ref, do_ref, lse_ref, dl_ref, cc_ref, sa_ref, sb_ref,
             dkv_ref, dkr_ref, dq_ref, dq_sc, dk_sc, dv_sc):
        h = pl.program_id(0)
        j = pl.program_id(1)

        @pl.when(j == 0)
        def _():
            dq_sc[...] = jnp.zeros((s, HEAD_PAD), F32)

        dk_sc[...] = jnp.zeros((tq, HEAD_PAD), F32)
        dv_sc[...] = jnp.zeros((tq, V_DIM), F32)
        k = jnp.concatenate([kn_ref[...], kr_ref[...]], axis=1)
        v = v_ref[...]

        def step(i, n_tiles, masked):
            r0 = pl.multiple_of(i * tq, tq)
            rows = pl.ds(r0, n_tiles * tq)
            q = q_ref[rows, :]
            dov = do_ref[rows, :]
            lse_row = jnp.concatenate([lse_ref[i + n, pl.ds(0, 1), :] for n in range(n_tiles)], axis=1)
            dl_row = jnp.concatenate([dl_ref[i + n, pl.ds(0, 1), :] for n in range(n_tiles)], axis=1)
            st = lax.dot_general(k, q, NT, preferred_element_type=F32) * SCALE_LOG2E
            if masked:
                krow = lax.broadcasted_iota(jnp.int32, (tq, n_tiles * tq), 0)
                qcol = lax.broadcasted_iota(jnp.int32, (tq, n_tiles * tq), 1)
                st = jnp.where(qcol >= krow, st, NEG)
            pt = jnp.exp2(st - lse_row)
            dpt = lax.dot_general(v, dov, NT, preferred_element_type=F32)
            dst = (pt * (dpt - dl_row)).astype(BF)
            dv_sc[...] += jnp.dot(pt.astype(BF), dov, preferred_element_type=F32)
            dk_sc[...] += jnp.dot(dst, q, preferred_element_type=F32)
            dq_sc[rows, :] += lax.dot_general(dst, k, TN, preferred_element_type=F32)

        _for_groups(j, nq - j, BWD_GROUPS, step)
        dkv_ref[:, :QK_NOPE] = (dk_sc[:, :QK_NOPE] * SCALE).astype(dkv_ref.dtype)
        dkv_ref[:, QK_NOPE:] = dv_sc[...].astype(dkv_ref.dtype)
        mine = pl.ds(pl.multiple_of(j * tq, tq), tq)
        dkr = dk_sc[:, QK_NOPE:] * SCALE

        @pl.when(h == 0)
        def _():
            dkr_ref[mine, :] = dkr

        @pl.when(h > 0)
        def _():
            dkr_ref[mine, :] += dkr

        dq_ref[:, :QK_NOPE] = (dq_sc[mine, :QK_NOPE] * SCALE).astype(dq_ref.dtype)
        dq_ref[:, QK_NOPE:] = _unrope(dq_sc[mine, QK_NOPE:] * SCALE, cc_ref[...], sa_ref[...],
                                      sb_ref[...]).astype(dq_ref.dtype)

    rows = pl.BlockSpec((None, nq, 8, tq), lambda h, j: (h, 0, 0, 0))
    tab = pl.BlockSpec((tq, 128), lambda h, j: (j, 0))
    return pl.pallas_call(
        body, name=name, grid=(N_HEADS, nq),
        in_specs=[pl.BlockSpec((tq, QK_NOPE), lambda h, j: (j, 2 * h)),
                  pl.BlockSpec((tq, V_DIM), lambda h, j: (j, 2 * h + 1)), tab,
                  pl.BlockSpec((s, HEAD_PAD), lambda h, j: (0, h)),
                  pl.BlockSpec((s, V_DIM), lambda h, j: (0, h)), rows, rows, tab, tab, tab],
        out_specs=[pl.BlockSpec((tq, 256), lambda h, j: (j, h)),
                   pl.BlockSpec((s, 128), lambda h, j: (0, 0)),
                   pl.BlockSpec((tq, HEAD_PAD), lambda h, j: (j, h))],
        out_shape=[jax.ShapeDtypeStruct((s, N_HEADS * 256), BF),
                   jax.ShapeDtypeStruct((s, 128), F32),
                   jax.ShapeDtypeStruct((s, N_HEADS * HEAD_PAD), BF)],
        scratch_shapes=[pltpu.VMEM((s, HEAD_PAD), F32), pltpu.VMEM((tq, HEAD_PAD), F32),
                        pltpu.VMEM((tq, V_DIM), F32)],
        compiler_params=_params(2))(kv, kv, krr, qr, do, lse, delta, cc, sa, sb)


def adamw(w, g, m, v, *, name, t=256):
    r, c = w.shape
    t = r if r % t else t
    c1 = 1.0 - ADAM_B1 ** ADAM_STEP
    c2 = 1.0 - ADAM_B2 ** ADAM_STEP

    def body(w_ref, g_ref, m_ref, v_ref, d_ref, nm_ref, nv_ref):
        gv = g_ref[...]
        nm = ADAM_B1 * m_ref[...] + (1.0 - ADAM_B1) * gv
        nv = ADAM_B2 * v_ref[...] + (1.0 - ADAM_B2) * (gv * gv)
        nm_ref[...] = nm
        nv_ref[...] = nv
        d_ref[...] = -ADAM_LR * ((nm / c1) / (jnp.sqrt(nv / c2) + ADAM_EPS) + ADAM_WD * w_ref[...])

    blk = pl.BlockSpec((t, c), lambda i: (i, 0))
    return pl.pallas_call(
        body, name=name, grid=(r // t,), in_specs=[blk] * 4, out_specs=[blk] * 3,
        out_shape=[jax.ShapeDtypeStruct((r, c), F32)] * 3, compiler_params=_params(1))(w, g, m, v)


def sum_devices(parts, *, name):
    def body(p_ref, o_ref):
        acc = p_ref[pl.ds(0, SV_ROWS), :]
        for d in range(1, 8):
            acc = acc + p_ref[pl.ds(d * SV_ROWS, SV_ROWS), :]
        o_ref[...] = acc

    return pl.pallas_call(body, name=name, out_shape=jax.ShapeDtypeStruct((SV_ROWS, SV_COLS), F32))(parts)


def add_halves(g, rb, c_idx, *, name, rows):
    nq, r2, cc = rb.shape
    nb = r2 // rows

    def body(c_ref, g_ref, r_ref, o_ref):
        o_ref[...] = (g_ref[...] + r_ref[...]).astype(o_ref.dtype)

    grid_spec = pltpu.PrefetchScalarGridSpec(
        num_scalar_prefetch=1, grid=(nq, nb),
        in_specs=[pl.BlockSpec((None, rows, cc), lambda q, i, c: (q, c[0] * nb + i, 0)),
                  pl.BlockSpec((None, rows, cc), lambda q, i, c: (q, i, 0))],
        out_specs=pl.BlockSpec((None, rows, cc), lambda q, i, c: (q, i, 0)))
    return pl.pallas_call(body, name=name, grid_spec=grid_spec,
                          out_shape=jax.ShapeDtypeStruct((nq, r2, cc), BF),
                          compiler_params=_params(2))(c_idx, g, rb)


def sum_chips(rc, c_idx, *, name, rows):
    nq, r2, cc = rc.shape
    nb = r2 // rows

    def body(c_ref, r_ref, o_ref):
        parts = [r_ref[q].astype(F32) for q in range(4)]
        o_ref[...] = ((parts[0] + parts[1]) + parts[2]) + parts[3]

    grid_spec = pltpu.PrefetchScalarGridSpec(
        num_scalar_prefetch=1, grid=(nb,),
        in_specs=[pl.BlockSpec((nq, rows, cc), lambda i, c: (0, i, 0))],
        out_specs=pl.BlockSpec((rows, cc), lambda i, c: (c[0] * nb + i, 0)))
    return pl.pallas_call(body, name=name, grid_spec=grid_spec,
                          out_shape=jax.ShapeDtypeStruct((2 * r2, cc), F32),
                          compiler_params=_params(1))(c_idx, rc)


def _place():
    return lax.axis_index("x"), lax.axis_index("y"), lax.axis_index("c")


def all_gather8(xs, *, name, own_half):
    m = xs.shape[0] // 2 if own_half else xs.shape[0]
    n = xs.shape[1]

    def body(x_ref, out_ref, send_sems, recv_sems, local_sem):
        x, y, c = _place()
        me, sibling = (x, y, c), (x, y, 1 - c)
        chips = [(1 - x, y), (x, 1 - y), (1 - x, 1 - y)]
        src_own = x_ref.at[pl.ds(c * m, m), :] if own_half else x_ref

        def rows(px, py, pc):
            return out_ref.at[pl.ds((4 * px + 2 * py + pc) * m, m), :]

        def copy(k, block, to, src=None):
            return pltpu.make_async_remote_copy(
                src_ref=rows(*block) if src is None else src, dst_ref=rows(*block),
                send_sem=send_sems.at[k], recv_sem=recv_sems.at[k], device_id=to, device_id_type=MESH)

        mine = pltpu.make_async_copy(src_own, rows(*me), local_sem)
        mine.start()
        first = [copy(0, me, sibling, src=src_own)]
        first += [copy(1 + j, me, (*chip, c), src=src_own) for j, chip in enumerate(chips)]
        for cp in first:
            cp.start()
        passed = [copy(4 + j, (*chip, c), sibling) for j, chip in enumerate(chips)]
        for j, chip in enumerate(chips):
            copy(1 + j, (*chip, c), me).wait_recv()
            passed[j].start()
        copy(0, sibling, me).wait_recv()
        for j, chip in enumerate(chips):
            copy(4 + j, (*chip, 1 - c), me).wait_recv()
        for cp in first + passed:
            cp.wait_send()
        mine.wait()

    return pl.pallas_call(
        body, name=name, out_shape=jax.ShapeDtypeStruct((8 * m, n), xs.dtype),
        in_specs=[pl.BlockSpec(memory_space=pl.ANY)], out_specs=pl.BlockSpec(memory_space=pl.ANY),
        scratch_shapes=[pltpu.SemaphoreType.DMA((7,)), pltpu.SemaphoreType.DMA((7,)), pltpu.SemaphoreType.DMA],
    )(xs)


def _other_chips():
    x, y, c = _place()
    return [(1 - x, y), (x, 1 - y), (1 - x, 1 - y)]


def _remote(src, dst, send_sems, recv_sems, k, to):
    return pltpu.make_async_remote_copy(src_ref=src, dst_ref=dst, send_sem=send_sems.at[k], recv_sem=recv_sems.at[k],
                                        device_id=to, device_id_type=MESH)


def gather_ici(xs):
    r, cc = xs.shape
    m = r // 2

    def copies(ins, outs, ss, rs, landing):
        x, y, c = _place()
        half = pl.ds(c * m, m)
        return [_remote(ins[0].at[half, :], outs[0].at[(2 * cx + cy) if landing else (2 * x + y), half, :],
                        ss, rs, j, (cx, cy, c)) for j, (cx, cy) in enumerate(_other_chips())]

    def start(ins, outs, ss, rs, ls):
        for cp in copies(ins, outs, ss, rs, False):
            cp.start()

    def wait(ins, outs, ss, rs, ls):
        for cp in copies(ins, outs, ss, rs, True):
            cp.wait_recv()
        for cp in copies(ins, outs, ss, rs, False):
            cp.wait_send()

    return Exchange((xs,), (jax.ShapeDtypeStruct((4, r, cc), xs.dtype),), {}, 3, start, wait)


def gather_forward(buf):
    m = buf.shape[1] // 2

    def copies(outs, ss, rs, landing):
        x, y, c = _place()
        half = pl.ds(((1 - c) if landing else c) * m, m)
        return [_remote(outs[0].at[2 * cx + cy, half, :], outs[0].at[2 * cx + cy, half, :], ss, rs, j, (x, y, 1 - c))
                for j, (cx, cy) in enumerate(_other_chips())]

    def start(ins, outs, ss, rs, ls):
        for cp in copies(outs, ss, rs, False):
            cp.start()

    def wait(ins, outs, ss, rs, ls):
        for cp in copies(outs, ss, rs, True):
            cp.wait_recv()
        for cp in copies(outs, ss, rs, False):
            cp.wait_send()

    return Exchange((buf,), (jax.ShapeDtypeStruct(buf.shape, buf.dtype),), {0: 0}, 3, start, wait)


def swap_halves(g):
    nq, r, cc = g.shape
    r2 = r // 2

    def copy(ins, outs, ss, rs):
        x, y, c = _place()
        return _remote(ins[0].at[:, pl.ds((1 - c) * r2, r2), :], outs[0], ss, rs, 0, (x, y, 1 - c))

    def start(ins, outs, ss, rs, ls):
        copy(ins, outs, ss, rs).start()

    def wait(ins, outs, ss, rs, ls):
        copy(ins, outs, ss, rs).wait()

    return Exchange((g,), (jax.ShapeDtypeStruct((nq, r2, cc), g.dtype),), {}, 1, start, wait)


def exchange_chips(p):
    def own(ins, outs, ls):
        x, y, c = _place()
        return pltpu.make_async_copy(ins[0].at[2 * x + y], outs[0].at[2 * x + y], ls)

    def copies(ins, outs, ss, rs, landing):
        x, y, c = _place()
        return [_remote(ins[0].at[2 * cx + cy], outs[0].at[(2 * cx + cy) if landing else (2 * x + y)],
                        ss, rs, j, (cx, cy, c)) for j, (cx, cy) in enumerate(_other_chips())]

    def start(ins, outs, ss, rs, ls):
        own(ins, outs, ls).start()
        for cp in copies(ins, outs, ss, rs, False):
            cp.start()

    def wait(ins, outs, ss, rs, ls):
        for cp in copies(ins, outs, ss, rs, True):
            cp.wait_recv()
        for cp in copies(ins, outs, ss, rs, False):
            cp.wait_send()
        own(ins, outs, ls).wait()

    return Exchange((p,), (jax.ShapeDtypeStruct(p.shape, p.dtype),), {}, 3, start, wait)


def join_halves(tot):
    r2 = tot.shape[0] // 2

    def copy(outs, ss, rs, landing):
        x, y, c = _place()
        half = outs[0].at[pl.ds(((1 - c) if landing else c) * r2, r2), :]
        return _remote(half, half, ss, rs, 0, (x, y, 1 - c))

    def start(ins, outs, ss, rs, ls):
        copy(outs, ss, rs, False).start()

    def wait(ins, outs, ss, rs, ls):
        copy(outs, ss, rs, True).wait_recv()
        copy(outs, ss, rs, False).wait_send()

    return Exchange((tot,), (jax.ShapeDtypeStruct(tot.shape, tot.dtype),), {0: 0}, 1, start, wait)


def _pack_shard(blocks, small_vec=None):
    parts = [w.reshape(-1, PACK_C).astype(BF) for w in blocks]
    if small_vec is not None:
        srow = lax.bitcast_convert_type(small_vec, BF).reshape(1, PACK_C)
        parts.append(jnp.pad(srow, ((0, PACK_PAD - 1), (0, 0))))
    return jnp.concatenate(parts, axis=0)


def _split_rows(a, rows, axis):
    out, off = [], 0
    for n in rows:
        out.append(lax.slice_in_dim(a, off, off + n, axis=axis))
        off += n
    return out


def _unpack_pool(gw):
    p_in, p_grp, p_out = _split_rows(gw, POOL_ROWS, 1)
    return dict(
        pool_w_in=p_in.reshape(4, D_MODEL, 1024).transpose(1, 0, 2).reshape(D_MODEL, 2 * POOL_WIDTH),
        pool_w_group=p_grp.reshape(4, 4, 128, POOL_GROUP).transpose(1, 0, 2, 3).reshape(4, POOL_GROUP, POOL_GROUP),
        pool_w_out=p_out.reshape(POOL_WIDTH, D_MODEL))


def _unpack_mla(gw):
    m_in, m_qb, m_kvb, m_out, small = _split_rows(gw, MLA_ROWS + (PACK_PAD,), 1)
    w = {}
    win = m_in.reshape(4, D_MODEL, 688).transpose(1, 0, 2).reshape(D_MODEL, 2752)
    w["mla_w_in"] = jnp.concatenate(
        [win[:, 384:640], win[:, 640:704], jnp.zeros((D_MODEL, 64), BF), win[:, 0:384], win[:, 704:]], axis=1)
    wq = m_qb.reshape(4, Q_LORA, 768).transpose(1, 0, 2).reshape(Q_LORA, N_HEADS, QK_NOPE + QK_ROPE)
    w["mla_w_q_b"] = jnp.pad(wq, ((0, 0), (0, 0), (0, HEAD_PAD - QK_NOPE - QK_ROPE))).reshape(Q_LORA, N_HEADS * HEAD_PAD)
    w["mla_w_kv_b"] = m_kvb.reshape(4, KV_LORA, 1024).transpose(1, 0, 2).reshape(KV_LORA, 4096)
    w["mla_w_out"] = m_out.reshape(MLA_WIDTH, D_MODEL)
    small = lax.bitcast_convert_type(small[:, 0, :].reshape(4, 512, 2), F32)
    w["mla_norm"] = small[:, :256].reshape(1, D_MODEL)
    w["mla_q_norm"] = small[:, 256:352].reshape(1, Q_LORA)
    w["mla_kv_norm"] = small[:, 352:416].reshape(1, KV_LORA)
    return w


def _pack_pool_grads(g):
    return jnp.concatenate([
        g["pool_w_in"].reshape(D_MODEL, 4, 1024).transpose(1, 0, 2),
        g["pool_w_group"].reshape(4, 4, 128, POOL_GROUP).transpose(1, 0, 2, 3).reshape(4, 256, PACK_C),
        g["pool_w_out"].reshape(4, 512, PACK_C)], axis=1)


def _pack_mla_grads(g):
    return jnp.concatenate([
        g["mla_w_in"].reshape(D_MODEL, 4, 688).transpose(1, 0, 2).reshape(4, 688, PACK_C),
        g["mla_w_q_b"].reshape(Q_LORA, 4, 768).transpose(1, 0, 2).reshape(4, 288, PACK_C),
        g["mla_w_kv_b"].reshape(KV_LORA, 4, 1024).transpose(1, 0, 2),
        g["mla_w_out"].reshape(4, 512, PACK_C),
        jnp.zeros((4, PACK_PAD, PACK_C), F32)], axis=1)


def kernel(x, positions, pool_norm, pool_w_in, pool_w_group, pool_scale, pool_w_out, mla_norm, mla_w_in, mla_q_norm, mla_w_q_b, mla_kv_norm, mla_w_kv_b, mla_w_out, final_norm, loss_target, m_pool_norm, m_pool_w_in, m_pool_w_group, m_pool_scale, m_pool_w_out, m_mla_norm, m_mla_w_in, m_mla_q_norm, m_mla_w_q_b, m_mla_kv_norm, m_mla_w_kv_b, m_mla_w_out, m_final_norm, v_pool_norm, v_pool_w_in, v_pool_w_group, v_pool_scale, v_pool_w_out, v_mla_norm, v_mla_w_in, v_mla_q_norm, v_mla_w_q_b, v_mla_kv_norm, v_mla_w_kv_b, v_mla_w_out, v_final_norm):
    s = x.shape[1]
    tq = min(512, s)
    x0 = x.reshape(s, D_MODEL)
    tgt = loss_target.reshape(s, D_MODEL)
    cx, cy, cc_idx = _place()
    chip = 2 * cx + cy

    big_names = ("pool_w_in", "pool_w_group", "pool_w_out", "mla_w_in", "mla_w_q_b", "mla_w_kv_b", "mla_w_out")
    big_w = dict(zip(big_names, (pool_w_in, pool_w_group, pool_w_out, mla_w_in, mla_w_q_b, mla_w_kv_b, mla_w_out)))
    big_m = dict(zip(big_names, (m_pool_w_in, m_pool_w_group, m_pool_w_out, m_mla_w_in, m_mla_w_q_b, m_mla_w_kv_b, m_mla_w_out)))
    big_v = dict(zip(big_names, (v_pool_w_in, v_pool_w_group, v_pool_w_out, v_mla_w_in, v_mla_w_q_b, v_mla_w_kv_b, v_mla_w_out)))

    small_vec = jnp.concatenate([mla_norm.reshape(-1), mla_q_norm.reshape(-1), mla_kv_norm.reshape(-1),
                                 jnp.zeros((96,), F32)])
    pool_packed = _pack_shard([big_w[n] for n in big_names[:3]])
    mla_packed = _pack_shard([big_w[n] for n in big_names[3:]], small_vec)
    w = _unpack_pool(all_gather8(pool_packed, name="gather_pool_weights", own_half=True).reshape(4, POOL_R, PACK_C))
    g_pool = pool_norm.reshape(1, D_MODEL)
    g_final = final_norm.reshape(1, D_MODEL)
    sc_pool = pool_scale.reshape(1, POOL_WIDTH)

    inv_freq = 1.0 / (ROPE_THETA ** (jnp.arange(0, QK_ROPE, 2, dtype=F32) / QK_ROPE))
    ang = positions.reshape(s).astype(F32)[:, None] * inv_freq
    cos, sin = jnp.cos(ang), jnp.sin(ang)
    z32, z64, z96 = (jnp.zeros((s, n), F32) for n in (32, 64, 96))
    t_cc = jnp.concatenate([cos, cos, z64], axis=1)
    t_sa = jnp.concatenate([-sin, z96], axis=1)
    t_sb = jnp.concatenate([z32, sin, z64], axis=1)

    h0 = norm_fwd(x0, g_pool, col=0, width=D_MODEL, name="pool_norm_fwd")
    uz, mla_land = mm_nn(h0, w["pool_w_in"], name="pool_in_proj", out_dtype=F32, host=gather_ici(mla_packed))
    pd = pool_prep(uz, name="pool_window")
    mm, y1 = pool_mix_gate(pd, w["pool_w_group"], uz, sc_pool, name="pool_group_mix")
    x1, mla_land = mm_nn(y1, w["pool_w_out"], name="pool_out_proj", out_dtype=F32, add=x0,
                         host=gather_forward(mla_land))
    w.update(_unpack_mla(lax.dynamic_update_slice_in_dim(mla_land, mla_packed[None], chip, axis=0)))

    h1 = norm_fwd(x1, w["mla_norm"], col=0, width=D_MODEL, name="mla_norm_fwd")
    proj = mm_nn(h1, w["mla_w_in"], name="mla_in_proj", out_dtype=F32, tn=P_WIDTH // 2)
    qn = norm_fwd(proj, w["mla_q_norm"], col=P_Q, width=Q_LORA, name="mla_q_norm_fwd")
    kvn = norm_fwd(proj, w["mla_kv_norm"], col=P_KV, width=KV_LORA, name="mla_kv_norm_fwd")
    qr = q_proj_rope(qn, w["mla_w_q_b"], t_cc, t_sa, t_sb, name="mla_q_proj")
    kv = mm_nn(kvn, w["mla_w_kv_b"], name="mla_kv_proj", out_dtype=BF, tk=KV_LORA)
    krr = rope_k(proj, t_cc, t_sa, t_sb, name="mla_rope_k")
    o, y2, lse = attn_fwd(qr, kv, krr, proj, name="mla_attn_fwd", tq=tq)
    dx2, d_final, loss_part = mm_nn_loss(y2, w["mla_w_out"], x1, g_final, tgt, name="mla_out_proj_loss")

    grads = {}
    grads["mla_w_out"] = mm_tn(y2, dx2, name="mla_out_proj_dw")
    do, dz2, delta = mla_out_dx_gate(dx2, w["mla_w_out"], o, proj, name="mla_out_proj_dx", tq=tq)
    dkv, dkr, dq_pre = attn_bwd(qr, kv, krr, do, lse, delta, t_cc, t_sa, t_sb, name="mla_attn_bwd", tq=tq)
    dkr_pre = unrope_k(dkr, t_cc, t_sa, t_sb, name="mla_unrope_k")
    dqn = mm_nt(dq_pre, w["mla_w_q_b"], name="mla_q_proj_dx", out_dtype=F32, tn=Q_LORA, tk=4096)
    g_qb = mm_tn(qn, dq_pre, name="mla_q_proj_dw", tm=Q_LORA, tn=2048)
    dkvn = mm_nt(dkv, w["mla_w_kv_b"], name="mla_kv_proj_dx", out_dtype=F32, tn=KV_LORA, tk=4096)
    grads["mla_w_kv_b"] = mm_tn(kvn, dkv, name="mla_kv_proj_dw", tm=KV_LORA, tn=2048)
    dq_lat, d_qnorm = norm_bwd(proj, w["mla_q_norm"], dqn, col=P_Q, width=Q_LORA, name="mla_q_norm_bwd", out_dtype=BF)
    dkv_lat, d_kvnorm = norm_bwd(proj, w["mla_kv_norm"], dkvn, col=P_KV, width=KV_LORA, name="mla_kv_norm_bwd", out_dtype=BF)
    dsmall = jnp.concatenate([dkv_lat, dkr_pre, dq_lat], axis=1)
    dh1 = mm_nt(dsmall, w["mla_w_in"], name="mla_in_proj_dx_a", out_dtype=F32, tk=P_SMALL)
    dx1, d_mnorm = mm_nt_norm_bwd(dz2, w["mla_w_in"][:, P_Z:], dh1, x1, w["mla_norm"], dx2, name="mla_in_proj_dx_b")
    g_in_a = mm_tn(h1, dsmall, name="mla_in_proj_dw_a", tn=P_SMALL)
    g_in_b = mm_tn(h1, dz2, name="mla_in_proj_dw_b")

    g_in = jnp.concatenate([g_in_a, g_in_b], axis=1)
    grads["mla_w_in"] = jnp.concatenate([g_in[:, P_Q:P_Z], g_in[:, P_KV:P_KV + KV_LORA],
                                         g_in[:, P_KR:P_KR + QK_ROPE], g_in[:, P_Z:]], axis=1)
    grads["mla_w_q_b"] = g_qb.reshape(Q_LORA, N_HEADS, HEAD_PAD)[:, :, :QK_NOPE + QK_ROPE].reshape(Q_LORA, -1)
    core_idx = cc_idx.reshape(1).astype(jnp.int32)
    gp_mla = _pack_mla_grads(grads)

    grads["pool_w_out"], sib = mm_tn(y1, dx1, name="pool_out_proj_dw", host=swap_halves(gp_mla))
    pre = add_halves(gp_mla, sib, core_idx, name="mla_grad_add_halves", rows=MLA_R // 2)
    dmm, dz1, d_scale, got = pool_out_dx_gate(dx1, w["pool_w_out"], mm, uz, sc_pool, name="pool_out_proj_dx",
                                              host=exchange_chips(pre))
    tot = sum_chips(got, core_idx, name="mla_grad_sum_chips", rows=MLA_R // 2)
    dpd, red_mla = gmm_nt(dmm, w["pool_w_group"], name="pool_group_mix_dx", host=join_halves(tot))
    grads["pool_w_group"] = gmm_tn(pd, dmm, 4, name="pool_group_mix_dw")
    du = pool_prep_bwd(dpd, name="pool_window_bwd")
    g_pin_u = mm_tn(h0, du, name="pool_in_proj_dw_u")
    g_pin_z = mm_tn(h0, dz1, name="pool_in_proj_dw_z")
    grads["pool_w_in"] = jnp.concatenate([g_pin_u, g_pin_z], axis=1)

    gp_pool = _pack_pool_grads(grads)
    dh0, sib = mm_nt(du, w["pool_w_in"], name="pool_in_proj_dx_u", out_dtype=F32, host=swap_halves(gp_pool))
    pre = add_halves(gp_pool, sib, core_idx, name="pool_grad_add_halves", rows=POOL_R // 2)
    grad_x, d_pnorm, got = mm_nt_norm_bwd(dz1, w["pool_w_in"], dh0, x0, g_pool, dx1, name="pool_in_proj_dx_z",
                                          b_col=POOL_WIDTH, host=exchange_chips(pre))
    tot = sum_chips(got, core_idx, name="pool_grad_sum_chips", rows=POOL_R // 2)
    red_pool = run_exchange(join_halves(tot), name="pool_grad_join_halves")[0]
    red_parts = _split_rows(red_pool, POOL_ROWS, 0) + _split_rows(red_mla, MLA_ROWS, 0)

    sv = jnp.concatenate([d_pnorm.reshape(-1), d_scale.reshape(-1), d_final.reshape(-1), d_mnorm.reshape(-1),
                          d_qnorm.reshape(-1), d_kvnorm.reshape(-1), loss_part[0, :1],
                          jnp.zeros((SV_ROWS * SV_COLS - SV_OFF["loss"] - 1,), F32)]).reshape(SV_ROWS, SV_COLS)
    sv_all = all_gather8(sv, name="gather_small_grads", own_half=False)
    sv_sum = sum_devices(sv_all, name="sum_small_grads").reshape(-1)
    loss = sv_sum[SV_OFF["loss"]]

    def sv_take(key, n):
        return lax.slice_in_dim(sv_sum, SV_OFF[key], SV_OFF[key] + n)

    out_g, out_d, out_m, out_v = {}, {}, {}, {}
    for name, part in zip(big_names, red_parts):
        shp = big_w[name].shape
        g2 = part.reshape(shp)
        two_d = (-1, shp[-1])
        d_, m_, v_ = adamw(big_w[name].reshape(two_d), g2.reshape(two_d), big_m[name].reshape(two_d),
                           big_v[name].reshape(two_d), name="adamw_" + name)
        out_g[name], out_d[name], out_m[name], out_v[name] = g2, d_.reshape(shp), m_.reshape(shp), v_.reshape(shp)

    small = [
        ("pool_norm", pool_norm, m_pool_norm, v_pool_norm, sv_take("pool_norm", 1024)),
        ("pool_scale", pool_scale, m_pool_scale, v_pool_scale, sv_take("pool_scale", 2048)),
        ("final_norm", final_norm, m_final_norm, v_final_norm, sv_take("final_norm", 1024)),
        ("mla_norm", mla_norm, m_mla_norm, v_mla_norm,
         lax.dynamic_slice_in_dim(sv_take("mla_norm", 1024), chip * 256, 256)),
        ("mla_q_norm", mla_q_norm, m_mla_q_norm, v_mla_q_norm,
         lax.dynamic_slice_in_dim(sv_take("q_norm", 384), chip * 96, 96)),
        ("mla_kv_norm", mla_kv_norm, m_mla_kv_norm, v_mla_kv_norm,
         lax.dynamic_slice_in_dim(sv_take("kv_norm", 256), chip * 64, 64)),
    ]
    sw = jnp.concatenate([t[1].reshape(-1) for t in small] + [jnp.zeros((96,), F32)]).reshape(1, -1)
    sm = jnp.concatenate([t[2].reshape(-1) for t in small] + [jnp.zeros((96,), F32)]).reshape(1, -1)
    s_v = jnp.concatenate([t[3].reshape(-1) for t in small] + [jnp.ones((96,), F32)]).reshape(1, -1)
    sg = jnp.concatenate([t[4].reshape(-1) for t in small] + [jnp.zeros((96,), F32)]).reshape(1, -1)
    sd_, sm_, sv_ = adamw(sw, sg, sm, s_v, name="adamw_vectors")
    off = 0
    for name, wt, _, _, gvec in small:
        n = gvec.shape[0]
        shp = wt.shape
        out_g[name] = gvec.reshape(shp)
        out_d[name] = sd_[0, off:off + n].reshape(shp)
        out_m[name] = sm_[0, off:off + n].reshape(shp)
        out_v[name] = sv_[0, off:off + n].reshape(shp)
        off += n

    order = ("pool_norm", "pool_w_in", "pool_w_group", "pool_scale", "pool_w_out", "mla_norm", "mla_w_in",
             "mla_q_norm", "mla_w_q_b", "mla_kv_norm", "mla_w_kv_b", "mla_w_out", "final_norm")
    return (loss, grad_x.reshape(x.shape), *[out_g[n] for n in order], *[out_d[n] for n in order],
            *[out_m[n] for n in order], *[out_v[n] for n in order])
```

```python
import functools
from typing import Callable, NamedTuple

import jax
import jax.numpy as jnp
from jax import lax
from jax.experimental import pallas as pl
from jax.experimental.pallas import tpu as pltpu

F32 = jnp.float32
BF = jnp.bfloat16
MESH = pl.DeviceIdType.MESH

D_MODEL = 1024
POOL_WIDTH = 2048
POOL_WINDOWS = (2, 4, 8, 16)
POOL_GROUP = 512
HALO = 16
N_HEADS = 16
QK_NOPE = 128
QK_ROPE = 64
V_DIM = 128
HEAD_PAD = 256
Q_LORA = 384
KV_LORA = 256
MLA_WIDTH = 2048
ROPE_THETA = 10000.0
EPS = 1e-6
SCALE = (QK_NOPE + QK_ROPE) ** -0.5
SCALE_LOG2E = SCALE * 1.4426950408889634
NEG = -1e30

P_KV, P_KR, P_Q, P_Z = 0, 256, 384, 768
P_SMALL = 768
P_WIDTH = 2816

ADAM_LR = 0.001
ADAM_B1 = 0.9
ADAM_B2 = 0.999
ADAM_EPS = 1e-08
ADAM_WD = 0.01
ADAM_STEP = 10

NN = (((1,), (0,)), ((), ()))
NT = (((1,), (1,)), ((), ()))
TN = (((0,), (0,)), ((), ()))

POOL_ROWS = (1024, 256, 512)
MLA_ROWS = (688, 288, 256, 512)
PACK_PAD = 16
POOL_R = sum(POOL_ROWS)
MLA_R = sum(MLA_ROWS) + PACK_PAD
PACK_C = 1024
SV_OFF = dict(pool_norm=0, pool_scale=1024, final_norm=3072, mla_norm=4096, q_norm=5120, kv_norm=5504, loss=5760)
SV_ROWS, SV_COLS = 8, 768

VMEM_LIMIT = 56 * 1024 * 1024


def _params(n_axes, vmem=None):
    return pltpu.CompilerParams(dimension_semantics=("arbitrary",) * n_axes,
                                vmem_limit_bytes=VMEM_LIMIT if vmem is None else vmem)


def _sigmoid(z):
    return 1.0 / (1.0 + jnp.exp(-z))


class Exchange(NamedTuple):
    operands: tuple
    out_shapes: tuple
    aliases: dict
    n_sems: int
    start: Callable
    wait: Callable


HBM_SPEC = pl.BlockSpec(memory_space=pl.ANY)


def _exchange_scratch(ex):
    return [pltpu.SemaphoreType.DMA((ex.n_sems,)), pltpu.SemaphoreType.DMA((ex.n_sems,)), pltpu.SemaphoreType.DMA]


def run_exchange(ex, *, name):
    n_in, n_out = len(ex.operands), len(ex.out_shapes)

    def body(*refs):
        args = (refs[:n_in], refs[n_in:n_in + n_out]) + tuple(refs[n_in + n_out:])
        ex.start(*args)
        ex.wait(*args)

    return pl.pallas_call(
        body, name=name, out_shape=list(ex.out_shapes), in_specs=[HBM_SPEC] * n_in,
        out_specs=[HBM_SPEC] * n_out, scratch_shapes=_exchange_scratch(ex),
        input_output_aliases=dict(ex.aliases))(*ex.operands)


def _call(core, *, name, grid, in_specs, out_specs, out_shape, args, scratch=(), host=None):
    in_specs, out_specs, out_shape = list(in_specs), list(out_specs), list(out_shape)
    params = _params(len(grid))
    if host is None:
        return pl.pallas_call(core, name=name, grid=grid, in_specs=in_specs, out_specs=out_specs,
                              out_shape=out_shape, scratch_shapes=list(scratch), compiler_params=params)(*args)
    n_in, n_out = len(in_specs), len(out_specs)
    n_hin, n_hout = len(host.operands), len(host.out_shapes)

    def body(*refs):
        ins, refs = refs[:n_in], refs[n_in:]
        h_in, refs = refs[:n_hin], refs[n_hin:]
        outs, refs = refs[:n_out], refs[n_out:]
        h_out, refs = refs[:n_hout], refs[n_hout:]
        own_scratch, sems = refs[:-3], refs[-3:]
        ids = [pl.program_id(ax) for ax in range(len(grid))]
        first = functools.reduce(jnp.logical_and, [i == 0 for i in ids])
        last = functools.reduce(jnp.logical_and, [i == n - 1 for i, n in zip(ids, grid)])

        @pl.when(first)
        def _():
            host.start(h_in, h_out, *sems)

        core(*ins, *outs, *own_scratch)

        @pl.when(last)
        def _():
            host.wait(h_in, h_out, *sems)

    return pl.pallas_call(
        body, name=name, grid=grid, in_specs=in_specs + [HBM_SPEC] * n_hin,
        out_specs=out_specs + [HBM_SPEC] * n_hout, out_shape=out_shape + list(host.out_shapes),
        scratch_shapes=list(scratch) + _exchange_scratch(host),
        input_output_aliases={n_in + i: n_out + o for i, o in host.aliases.items()},
        compiler_params=params)(*args, *host.operands)


def _mm(a, b, *, dims, grid, a_spec, b_spec, o_spec, out_shape, out_dtype, acc_shape, name,
        add=None, add_spec=None, host=None):
    nk = grid[-1]
    kax = len(grid) - 1

    def body(*refs):
        if add is None:
            a_ref, b_ref, o_ref = refs[:3]
            add_ref = None
            rest = refs[3:]
        else:
            a_ref, b_ref, add_ref, o_ref = refs[:4]
            rest = refs[4:]
        part = lax.dot_general(a_ref[...].astype(BF), b_ref[...].astype(BF), dims,
                               preferred_element_type=F32)

        def finish(r):
            if add_ref is not None:
                r = r + add_ref[...]
            o_ref[...] = r.astype(o_ref.dtype)

        if nk == 1:
            finish(part)
        else:
            acc = rest[0]
            k = pl.program_id(kax)

            @pl.when(k == 0)
            def _():
                acc[...] = part

            @pl.when(k > 0)
            def _():
                acc[...] += part

            @pl.when(k == nk - 1)
            def _():
                finish(acc[...])

    in_specs = [a_spec, b_spec]
    args = [a, b]
    if add is not None:
        in_specs.append(add_spec)
        args.append(add)
    out = _call(body, name=name, grid=grid, in_specs=in_specs, out_specs=[o_spec],
                out_shape=[jax.ShapeDtypeStruct(out_shape, out_dtype)], args=args,
                scratch=[] if nk == 1 else [pltpu.VMEM(acc_shape, F32)], host=host)
    return out[0] if host is None else out


def _pick(n, t):
    t = min(n, t)
    assert n % t == 0, (n, t)
    return t


def mm_nn(a, b, *, name, out_dtype, add=None, tm=1024, tn=1024, tk=2048, host=None):
    m = a.shape[0]
    kk, n = b.shape
    tm, tn, tk = _pick(m, tm), _pick(n, tn), _pick(kk, tk)
    return _mm(a, b, dims=NN, grid=(m // tm, n // tn, kk // tk),
               a_spec=pl.BlockSpec((tm, tk), lambda i, j, k: (i, k)),
               b_spec=pl.BlockSpec((tk, tn), lambda i, j, k: (k, j)),
               o_spec=pl.BlockSpec((tm, tn), lambda i, j, k: (i, j)),
               add=add, add_spec=pl.BlockSpec((tm, tn), lambda i, j, k: (i, j)),
               out_shape=(m, n), out_dtype=out_dtype, acc_shape=(tm, tn), name=name, host=host)


def mm_nt(a, b, *, name, out_dtype, b_col=0, add=None, tm=1024, tn=1024, tk=2048, host=None):
    m, kk = a.shape
    n = b.shape[0]
    tm, tn, tk = _pick(m, tm), _pick(n, tn), _pick(kk, tk)
    assert b_col % tk == 0
    ko = b_col // tk
    return _mm(a, b, dims=NT, grid=(m // tm, n // tn, kk // tk),
               a_spec=pl.BlockSpec((tm, tk), lambda i, j, k: (i, k)),
               b_spec=pl.BlockSpec((tn, tk), lambda i, j, k: (j, ko + k)),
               o_spec=pl.BlockSpec((tm, tn), lambda i, j, k: (i, j)),
               add=add, add_spec=pl.BlockSpec((tm, tn), lambda i, j, k: (i, j)),
               out_shape=(m, n), out_dtype=out_dtype, acc_shape=(tm, tn), name=name, host=host)


def mm_tn(a, b, *, name, tm=1024, tn=1024, tk=2048, host=None, by_column_block=False):
    s, m = a.shape
    n = b.shape[1]
    tm, tn, tk = _pick(m, tm), _pick(n, tn), _pick(s, tk)
    if by_column_block:
        out_shape, o_spec = (n // tn, m, tn), pl.BlockSpec((None, tm, tn), lambda i, j, k: (j, i, 0))
    else:
        out_shape, o_spec = (m, n), pl.BlockSpec((tm, tn), lambda i, j, k: (i, j))
    return _mm(a, b, dims=TN, grid=(m // tm, n // tn, s // tk),
               a_spec=pl.BlockSpec((tk, tm), lambda i, j, k: (k, i)),
               b_spec=pl.BlockSpec((tk, tn), lambda i, j, k: (k, j)),
               o_spec=o_spec, out_shape=out_shape, out_dtype=F32, acc_shape=(tm, tn), name=name, host=host)


def gmm_nt(a, w, *, name, tm=1024, host=None):
    s = a.shape[0]
    g, kk, n = w.shape
    tm = _pick(s, tm)
    return _mm(a, w, dims=NT, grid=(s // tm, g, 1),
               a_spec=pl.BlockSpec((tm, n), lambda i, gi, k: (i, gi)),
               b_spec=pl.BlockSpec((None, kk, n), lambda i, gi, k: (gi, 0, 0)),
               o_spec=pl.BlockSpec((tm, kk), lambda i, gi, k: (i, gi)),
               out_shape=(s, g * kk), out_dtype=F32, acc_shape=(tm, kk), name=name, host=host)


def gmm_tn(a, b, g, *, name, tk=2048):
    s = a.shape[0]
    kk, n = a.shape[1] // g, b.shape[1] // g
    tk = _pick(s, tk)
    return _mm(a, b, dims=TN, grid=(g, s // tk),
               a_spec=pl.BlockSpec((tk, kk), lambda gi, k: (k, gi)),
               b_spec=pl.BlockSpec((tk, n), lambda gi, k: (k, gi)),
               o_spec=pl.BlockSpec((None, kk, n), lambda gi, k: (gi, 0, 0)),
               out_shape=(g, kk, n), out_dtype=F32, acc_shape=(kk, n), name=name)


def norm_fwd(x, g, *, col, width, name, t=512):
    s = x.shape[0]
    t = _pick(s, t)
    cb = col // width
    assert col % width == 0

    def body(x_ref, g_ref, o_ref):
        xv = x_ref[...]
        inv = lax.rsqrt(jnp.mean(xv * xv, axis=-1, keepdims=True) + EPS)
        o_ref[...] = ((xv * inv) * g_ref[...]).astype(o_ref.dtype)

    return pl.pallas_call(
        body, name=name, grid=(s // t,),
        in_specs=[pl.BlockSpec((t, width), lambda i: (i, cb)), pl.BlockSpec((1, width), lambda i: (0, 0))],
        out_specs=pl.BlockSpec((t, width), lambda i: (i, 0)),
        out_shape=jax.ShapeDtypeStruct((s, width), BF), compiler_params=_params(1))(x, g)


def _accumulate(ref, part):
    @pl.when(pl.program_id(0) == 0)
    def _():
        ref[...] = part

    @pl.when(pl.program_id(0) > 0)
    def _():
        ref[...] += part


def norm_bwd(x, g, dh, *, col, width, name, out_dtype, t=512):
    s = x.shape[0]
    t = _pick(s, t)
    cb = col // width
    assert col % width == 0

    def body(x_ref, g_ref, dh_ref, dx_ref, dg_ref):
        xv = x_ref[...]
        inv = lax.rsqrt(jnp.mean(xv * xv, axis=-1, keepdims=True) + EPS)
        xhat = xv * inv
        dh_v = dh_ref[...]
        _accumulate(dg_ref, jnp.sum(dh_v * xhat, axis=0, keepdims=True))
        dxhat = dh_v * g_ref[...]
        dx = inv * (dxhat - xhat * jnp.mean(dxhat * xhat, axis=-1, keepdims=True))
        dx_ref[...] = dx.astype(dx_ref.dtype)

    row = pl.BlockSpec((t, width), lambda i: (i, 0))
    vec = pl.BlockSpec((1, width), lambda i: (0, 0))
    return pl.pallas_call(
        body, name=name, grid=(s // t,), in_specs=[pl.BlockSpec((t, width), lambda i: (i, cb)), vec, row],
        out_specs=[row, vec],
        out_shape=[jax.ShapeDtypeStruct((s, width), out_dtype), jax.ShapeDtypeStruct((1, width), F32)],
        compiler_params=_params(1))(x, g, dh)


def mm_nt_norm_bwd(a, b, add, x, g, res, *, name, b_col=0, tm=512, host=None):
    s, kk = a.shape
    d = b.shape[0]
    tm = _pick(s, tm)
    assert b_col % kk == 0

    def body(a_ref, b_ref, add_ref, x_ref, g_ref, res_ref, dx_ref, dg_ref):
        dh = lax.dot_general(a_ref[...].astype(BF), b_ref[...].astype(BF), NT,
                             preferred_element_type=F32) + add_ref[...]
        xv = x_ref[...]
        inv = lax.rsqrt(jnp.mean(xv * xv, axis=-1, keepdims=True) + EPS)
        xhat = xv * inv
        _accumulate(dg_ref, jnp.sum(dh * xhat, axis=0, keepdims=True))
        dxhat = dh * g_ref[...]
        dx_ref[...] = inv * (dxhat - xhat * jnp.mean(dxhat * xhat, axis=-1, keepdims=True)) + res_ref[...]

    row = pl.BlockSpec((tm, d), lambda i: (i, 0))
    vec = pl.BlockSpec((1, d), lambda i: (0, 0))
    return _call(
        body, name=name, grid=(s // tm,),
        in_specs=[pl.BlockSpec((tm, kk), lambda i: (i, 0)), pl.BlockSpec((d, kk), lambda i: (0, b_col // kk)),
                  row, row, vec, row],
        out_specs=[row, vec],
        out_shape=[jax.ShapeDtypeStruct((s, d), F32), jax.ShapeDtypeStruct((1, d), F32)],
        args=[a, b, add, x, g, res], host=host)


def mm_nn_loss(a, b, add, gf, tgt, *, name, tm=512):
    s, kk = a.shape
    d = b.shape[1]
    tm = _pick(s, tm)

    def body(a_ref, b_ref, add_ref, g_ref, t_ref, dx_ref, dg_ref, loss_ref):
        xv = jnp.dot(a_ref[...].astype(BF), b_ref[...].astype(BF), preferred_element_type=F32) + add_ref[...]
        inv = lax.rsqrt(jnp.mean(xv * xv, axis=-1, keepdims=True) + EPS)
        xhat = xv * inv
        gv = g_ref[...]
        diff = xhat * gv - t_ref[...]
        row_err = jnp.mean(diff * diff, axis=-1, keepdims=True)
        _accumulate(loss_ref, jnp.broadcast_to(0.5 * jnp.sum(row_err, axis=0, keepdims=True), (1, 128)))
        dout = diff * (1.0 / d)
        _accumulate(dg_ref, jnp.sum(dout * xhat, axis=0, keepdims=True))
        dxhat = dout * gv
        dx_ref[...] = inv * (dxhat - xhat * jnp.mean(dxhat * xhat, axis=-1, keepdims=True))

    row = pl.BlockSpec((tm, d), lambda i: (i, 0))
    vec = pl.BlockSpec((1, d), lambda i: (0, 0))
    return _call(
        body, name=name, grid=(s // tm,),
        in_specs=[pl.BlockSpec((tm, kk), lambda i: (i, 0)), pl.BlockSpec((kk, d), lambda i: (0, 0)), row, vec, row],
        out_specs=[row, vec, pl.BlockSpec((1, 128), lambda i: (0, 0))],
        out_shape=[jax.ShapeDtypeStruct((s, d), F32), jax.ShapeDtypeStruct((1, d), F32),
                   jax.ShapeDtypeStruct((1, 128), F32)],
        args=[a, b, add, gf, tgt])


def pool_prep(uz, *, name, t=256):
    s = uz.shape[0]
    t = _pick(s, t)
    hb = t // HALO

    def body(u_ref, halo_ref, o_ref, buf):
        i = pl.program_id(0)
        buf[pl.ds(HALO, t), :] = u_ref[...]

        @pl.when(i == 0)
        def _():
            buf[pl.ds(0, HALO), :] = jnp.zeros((HALO, POOL_WIDTH), F32)

        @pl.when(i > 0)
        def _():
            buf[pl.ds(0, HALO), :] = halo_ref[...]

        pos = i * t + lax.broadcasted_iota(jnp.int32, (t, POOL_GROUP), 0)
        for g, w in enumerate(POOL_WINDOWS):
            cols = pl.ds(g * POOL_GROUP, POOL_GROUP)
            cur = buf[pl.ds(HALO, t), cols]
            acc = cur
            for k in range(1, w):
                acc = acc + buf[pl.ds(HALO - k, t), cols]
            cnt = jnp.minimum(pos + 1, w).astype(F32)
            o_ref[:, cols] = (acc / cnt - cur).astype(o_ref.dtype)

    return pl.pallas_call(
        body, name=name, grid=(s // t,),
        in_specs=[pl.BlockSpec((t, POOL_WIDTH), lambda i: (i, 0)),
                  pl.BlockSpec((HALO, POOL_WIDTH), lambda i: (jnp.maximum(i * hb - 1, 0), 0))],
        out_specs=pl.BlockSpec((t, POOL_WIDTH), lambda i: (i, 0)),
        out_shape=jax.ShapeDtypeStruct((s, POOL_WIDTH), BF),
        scratch_shapes=[pltpu.VMEM((t + HALO, POOL_WIDTH), F32)],
        compiler_params=_params(1))(uz, uz)


def pool_prep_bwd(dpd, *, name, t=256):
    s = dpd.shape[0]
    t = _pick(s, t)
    hb = t // HALO
    n = s // t

    def body(d_ref, halo_ref, o_ref, buf):
        i = pl.program_id(0)
        pos = i * t + lax.broadcasted_iota(jnp.int32, (t, POOL_GROUP), 0)
        for g, w in enumerate(POOL_WINDOWS):
            cols = pl.ds(g * POOL_GROUP, POOL_GROUP)
            cnt = jnp.minimum(pos + 1, w).astype(F32)
            buf[pl.ds(0, t), cols] = d_ref[:, cols] / cnt

            @pl.when(i < n - 1)
            def _():
                buf[pl.ds(t, HALO), cols] = halo_ref[:, cols] / float(w)

            @pl.when(i == n - 1)
            def _():
                buf[pl.ds(t, HALO), cols] = jnp.zeros((HALO, POOL_GROUP), F32)

        for g, w in enumerate(POOL_WINDOWS):
            cols = pl.ds(g * POOL_GROUP, POOL_GROUP)
            acc = buf[pl.ds(0, t), cols]
            for k in range(1, w):
                acc = acc + buf[pl.ds(k, t), cols]
            o_ref[:, cols] = (acc - d_ref[:, cols]).astype(o_ref.dtype)

    return pl.pallas_call(
        body, name=name, grid=(n,),
        in_specs=[pl.BlockSpec((t, POOL_WIDTH), lambda i: (i, 0)),
                  pl.BlockSpec((HALO, POOL_WIDTH), lambda i: (jnp.minimum((i + 1) * hb, n * hb - 1), 0))],
        out_specs=pl.BlockSpec((t, POOL_WIDTH), lambda i: (i, 0)),
        out_shape=jax.ShapeDtypeStruct((s, POOL_WIDTH), BF),
        scratch_shapes=[pltpu.VMEM((t + HALO, POOL_WIDTH), F32)],
        compiler_params=_params(1))(dpd, dpd)


CHUNK = 512


def _chunks(width, step=CHUNK):
    return [slice(c, c + step) for c in range(0, width, step)]


def pool_mix_gate(pd, wg, uz, scale, *, name, tm=1024):
    s = pd.shape[0]
    g = wg.shape[0]
    tm = _pick(s, tm)

    def body(a_ref, w_ref, z_ref, sc_ref, mm_ref, y_ref):
        mm = jnp.dot(a_ref[...], w_ref[...], preferred_element_type=F32)
        mm_ref[...] = mm
        z = z_ref[...]
        y_ref[...] = ((mm * sc_ref[...]) * (z * _sigmoid(z))).astype(y_ref.dtype)

    blk = pl.BlockSpec((tm, POOL_GROUP), lambda i, gi: (i, gi))
    return pl.pallas_call(
        body, name=name, grid=(s // tm, g),
        in_specs=[blk, pl.BlockSpec((None, POOL_GROUP, POOL_GROUP), lambda i, gi: (gi, 0, 0)),
                  pl.BlockSpec((tm, POOL_GROUP), lambda i, gi: (i, g + gi)),
                  pl.BlockSpec((1, POOL_GROUP), lambda i, gi: (0, gi))],
        out_specs=[blk, blk],
        out_shape=[jax.ShapeDtypeStruct((s, POOL_WIDTH), F32), jax.ShapeDtypeStruct((s, POOL_WIDTH), BF)],
        compiler_params=_params(2))(pd, wg, uz, scale)


def pool_out_dx_gate(dx, w_out, mm, uz, scale, *, name, tm=512, host=None):
    s, d = dx.shape
    tm = _pick(s, tm)

    def body(dx_ref, w_ref, mm_ref, z_ref, sc_ref, dmm_ref, dz_ref, dsc_ref):
        dxv = dx_ref[...].astype(BF)
        parts = []
        for c in _chunks(POOL_WIDTH):
            dyv = lax.dot_general(dxv, w_ref[c, :], NT, preferred_element_type=F32)
            z = z_ref[:, c]
            sig = _sigmoid(z)
            mmv = mm_ref[:, c]
            scv = sc_ref[:, c]
            dmixed = dyv * (z * sig)
            dmm_ref[:, c] = (dmixed * scv).astype(dmm_ref.dtype)
            dz_ref[:, c] = (dyv * (mmv * scv) * (sig * (1.0 + z * (1.0 - sig)))).astype(dz_ref.dtype)
            parts.append(jnp.sum(dmixed * mmv, axis=0, keepdims=True))

        @pl.when(pl.program_id(0) == 0)
        def _():
            for c, part in zip(_chunks(POOL_WIDTH), parts):
                dsc_ref[:, c] = part

        @pl.when(pl.program_id(0) > 0)
        def _():
            for c, part in zip(_chunks(POOL_WIDTH), parts):
                dsc_ref[:, c] += part

    blk = pl.BlockSpec((tm, POOL_WIDTH), lambda i: (i, 0))
    vec = pl.BlockSpec((1, POOL_WIDTH), lambda i: (0, 0))
    return _call(
        body, name=name, grid=(s // tm,),
        in_specs=[pl.BlockSpec((tm, d), lambda i: (i, 0)), pl.BlockSpec((POOL_WIDTH, d), lambda i: (0, 0)),
                  blk, pl.BlockSpec((tm, POOL_WIDTH), lambda i: (i, 1)), vec],
        out_specs=[blk, blk, vec],
        out_shape=[jax.ShapeDtypeStruct((s, POOL_WIDTH), BF), jax.ShapeDtypeStruct((s, POOL_WIDTH), BF),
                   jax.ShapeDtypeStruct((1, POOL_WIDTH), F32)],
        args=[dx, w_out, mm, uz, scale], host=host)


def _rope(a, cc, sa, sb):
    return a * cc + pltpu.roll(a, 96, 1) * sa + pltpu.roll(a, 32, 1) * sb


def _unrope(d, cc, sa, sb):
    return d * cc + pltpu.roll(d * sa, 32, 1) + pltpu.roll(d * sb, 96, 1)


def q_proj_rope(qn, wq, cc, sa, sb, *, name, tm=1024, heads=4):
    s, kk = qn.shape
    tm = _pick(s, tm)
    tn = heads * HEAD_PAD

    def body(a_ref, b_ref, cc_ref, sa_ref, sb_ref, o_ref):
        q = jnp.dot(a_ref[...], b_ref[...], preferred_element_type=F32)
        for h in range(heads):
            nope = slice(h * HEAD_PAD, h * HEAD_PAD + QK_NOPE)
            rope = slice(h * HEAD_PAD + QK_NOPE, (h + 1) * HEAD_PAD)
            o_ref[:, nope] = q[:, nope].astype(o_ref.dtype)
            o_ref[:, rope] = _rope(q[:, rope], cc_ref[...], sa_ref[...], sb_ref[...]).astype(o_ref.dtype)

    tab = pl.BlockSpec((tm, 128), lambda i, j: (i, 0))
    return pl.pallas_call(
        body, name=name, grid=(s // tm, N_HEADS // heads),
        in_specs=[pl.BlockSpec((tm, kk), lambda i, j: (i, 0)), pl.BlockSpec((kk, tn), lambda i, j: (0, j)),
                  tab, tab, tab],
        out_specs=pl.BlockSpec((tm, tn), lambda i, j: (i, j)),
        out_shape=jax.ShapeDtypeStruct((s, N_HEADS * HEAD_PAD), BF), compiler_params=_params(2))(qn, wq, cc, sa, sb)


def rope_k(proj, cc, sa, sb, *, name, t=512):
    s = proj.shape[0]
    t = _pick(s, t)
    kr_blk = P_KR // 128

    def body(kr_ref, cc_ref, sa_ref, sb_ref, o_ref):
        o_ref[...] = _rope(kr_ref[...], cc_ref[...], sa_ref[...], sb_ref[...]).astype(o_ref.dtype)

    tab = pl.BlockSpec((t, 128), lambda i: (i, 0))
    return pl.pallas_call(
        body, name=name, grid=(s // t,),
        in_specs=[pl.BlockSpec((t, 128), lambda i: (i, kr_blk)), tab, tab, tab], out_specs=tab,
        out_shape=jax.ShapeDtypeStruct((s, 128), BF), compiler_params=_params(1))(proj, cc, sa, sb)


def unrope_k(dkr, cc, sa, sb, *, name, t=512):
    s = dkr.shape[0]
    t = _pick(s, t)

    def body(d_ref, cc_ref, sa_ref, sb_ref, o_ref):
        o_ref[...] = _unrope(d_ref[...], cc_ref[...], sa_ref[...], sb_ref[...]).astype(o_ref.dtype)

    tab = pl.BlockSpec((t, 128), lambda i: (i, 0))
    return pl.pallas_call(
        body, name=name, grid=(s // t,), in_specs=[tab, tab, tab, tab], out_specs=tab,
        out_shape=jax.ShapeDtypeStruct((s, 128), BF), compiler_params=_params(1))(dkr, cc, sa, sb)


def mla_out_dx_gate(dx, w_out, o, proj, *, name, tq):
    s, d = dx.shape
    nq = s // tq

    def body(dx_ref, w_ref, o_ref, p_ref, do_ref, dz_ref, dl_ref):
        dxv = dx_ref[...].astype(BF)
        for c in _chunks(MLA_WIDTH):
            dy_c = lax.dot_general(dxv, w_ref[c, :], NT, preferred_element_type=F32)
            for h in range(c.start // V_DIM, c.stop // V_DIM):
                hc = slice(h * V_DIM, (h + 1) * V_DIM)
                z = p_ref[:, slice(P_Z + hc.start, P_Z + hc.stop)]
                sig = _sigmoid(z)
                dyv = dy_c[:, hc.start - c.start:hc.stop - c.start]
                ov = o_ref[:, hc]
                dov = dyv * (z * sig)
                do_ref[:, hc] = dov.astype(do_ref.dtype)
                dz_ref[:, hc] = (dyv * ov * (sig * (1.0 + z * (1.0 - sig)))).astype(dz_ref.dtype)
                delta = jnp.sum(dov * ov, axis=-1, keepdims=True)
                dl_ref[h] = jnp.broadcast_to(delta, (tq, 128)).T[:8, :]

    blk = pl.BlockSpec((tq, MLA_WIDTH), lambda i: (i, 0))
    return pl.pallas_call(
        body, name=name, grid=(nq,),
        in_specs=[pl.BlockSpec((tq, d), lambda i: (i, 0)), pl.BlockSpec((MLA_WIDTH, d), lambda i: (0, 0)),
                  blk, pl.BlockSpec((tq, P_WIDTH), lambda i: (i, 0))],
        out_specs=[blk, blk, pl.BlockSpec((N_HEADS, None, 8, tq), lambda i: (0, i, 0, 0))],
        out_shape=[jax.ShapeDtypeStruct((s, MLA_WIDTH), BF), jax.ShapeDtypeStruct((s, MLA_WIDTH), BF),
                   jax.ShapeDtypeStruct((N_HEADS, nq, 8, tq), F32)],
        compiler_params=_params(1))(dx, w_out, o, proj)


FWD_GROUPS = (4, 2, 1)
BWD_GROUPS = (4, 2, 1)


def _for_groups(first, count, groups, fn):
    lead = groups[-1]
    for g in groups[:-1][::-1]:
        lead = jnp.where(count >= g, g, lead)
    for g in groups:
        @pl.when(lead == g)
        def _(g=g):
            fn(first, g, True)
    first = first + lead
    count = count - lead
    for g in groups:
        n = count // g

        def one(p, carry, g=g, first=first):
            fn(first + p * g, g, False)
            return carry

        lax.fori_loop(0, n, one, 0)
        first = first + n * g
        count = count - n * g


def attn_fwd(qr, kv, krr, proj, *, name, tq):
    s = qr.shape[0]
    nq = s // tq
    z_blk = P_Z // V_DIM

    def body(kn_ref, v_ref, kr_ref, q_ref, z_ref, o_ref, y_ref, lse_ref, acc_sc, m_sc):
        j = pl.program_id(1)

        @pl.when(j == 0)
        def _():
            acc_sc[...] = jnp.zeros((nq, 2 * V_DIM, tq), F32)
            m_sc[...] = jnp.full((nq, 8, tq), NEG, F32)

        k = jnp.concatenate([kn_ref[...], kr_ref[...]], axis=1)
        vxt = jnp.concatenate([v_ref[...].astype(F32).T.astype(BF), jnp.ones((V_DIM, tq), BF)], axis=0)

        def update(i, n_tiles, masked):
            rows = pl.ds(pl.multiple_of(i * tq, tq), n_tiles * tq)
            st = lax.dot_general(k, q_ref[rows, :], NT, preferred_element_type=F32) * SCALE_LOG2E
            if masked:
                krow = lax.broadcasted_iota(jnp.int32, (tq, n_tiles * tq), 0)
                qcol = lax.broadcasted_iota(jnp.int32, (tq, n_tiles * tq), 1)
                st = jnp.where(qcol >= krow, st, NEG)
            m_prev = jnp.concatenate([m_sc[i + n, pl.ds(0, 1), :] for n in range(n_tiles)], axis=1)
            m_new = jnp.maximum(m_prev, jnp.max(st, axis=0, keepdims=True))
            alpha = jnp.exp2(m_prev - m_new)
            pt = jnp.exp2(st - m_new).astype(BF)
            pv_t = jnp.dot(vxt, pt, preferred_element_type=F32)
            for n in range(n_tiles):
                cols = slice(n * tq, (n + 1) * tq)
                acc_sc[i + n] = alpha[:, cols] * acc_sc[i + n] + pv_t[:, cols]
                m_sc[i + n, pl.ds(0, 1), :] = m_new[:, cols]

        _for_groups(j, nq - j, FWD_GROUPS, update)
        l = acc_sc[j, V_DIM:, :]
        o = (acc_sc[j, :V_DIM, :] / l).T
        o_ref[...] = o
        z = z_ref[...]
        y_ref[...] = (o * (z * _sigmoid(z))).astype(y_ref.dtype)
        lse_ref[...] = m_sc[j, pl.ds(0, 1), :] + jnp.log2(l[:8, :])

    tile = pl.BlockSpec((tq, V_DIM), lambda h, j: (j, h))
    return pl.pallas_call(
        body, name=name, grid=(N_HEADS, nq),
        in_specs=[pl.BlockSpec((tq, QK_NOPE), lambda h, j: (j, 2 * h)),
                  pl.BlockSpec((tq, V_DIM), lambda h, j: (j, 2 * h + 1)),
                  pl.BlockSpec((tq, 128), lambda h, j: (j, 0)),
                  pl.BlockSpec((s, HEAD_PAD), lambda h, j: (0, h)),
                  pl.BlockSpec((tq, V_DIM), lambda h, j: (j, z_blk + h))],
        out_specs=[tile, tile, pl.BlockSpec((None, None, 8, tq), lambda h, j: (h, j, 0, 0))],
        out_shape=[jax.ShapeDtypeStruct((s, N_HEADS * V_DIM), F32),
                   jax.ShapeDtypeStruct((s, N_HEADS * V_DIM), BF),
                   jax.ShapeDtypeStruct((N_HEADS, nq, 8, tq), F32)],
        scratch_shapes=[pltpu.VMEM((nq, 2 * V_DIM, tq), F32), pltpu.VMEM((nq, 8, tq), F32)],
        compiler_params=_params(2))(kv, kv, krr, qr, proj)


def attn_bwd(qr, kv, krr, do, lse, delta, cc, sa, sb, *, name, tq):
    s = qr.shape[0]
    nq = s // tq

    def body(kn_ref, v_ref, kr_ref, q_ref, do_ref, lse_ref, dl_ref, cc_ref, sa_ref, sb_ref,
             dkv_ref, dkr_ref, dq_ref, dq_sc, dk_sc, dv_sc):
        h = pl.program_id(0)
        j = pl.program_id(1)

        @pl.when(j == 0)
        def _():
            dq_sc[...] = jnp.zeros((s, HEAD_PAD), F32)

        dk_sc[...] = jnp.zeros((tq, HEAD_PAD), F32)
        dv_sc[...] = jnp.zeros((tq, V_DIM), F32)
        k = jnp.concatenate([kn_ref[...], kr_ref[...]], axis=1)
        v = v_ref[...]

        def step(i, n_tiles, masked):
            r0 = pl.multiple_of(i * tq, tq)
            rows = pl.ds(r0, n_tiles * tq)
            q = q_ref[rows, :]
            dov = do_ref[rows, :]
            lse_row = jnp.concatenate([lse_ref[i + n, pl.ds(0, 1), :] for n in range(n_tiles)], axis=1)
            dl_row = jnp.concatenate([dl_ref[i + n, pl.ds(0, 1), :] for n in range(n_tiles)], axis=1)
            st = lax.dot_general(k, q, NT, preferred_element_type=F32) * SCALE_LOG2E
            if masked:
                krow = lax.broadcasted_iota(jnp.int32, (tq, n_tiles * tq), 0)
                qcol = lax.broadcasted_iota(jnp.int32, (tq, n_tiles * tq), 1)
                st = jnp.where(qcol >= krow, st, NEG)
            pt = jnp.exp2(st - lse_row)
            dpt = lax.dot_general(v, dov, NT, preferred_element_type=F32)
            dst = (pt * (dpt - dl_row)).astype(BF)
            dv_sc[...] += jnp.dot(pt.astype(BF), dov, preferred_element_type=F32)
            dk_sc[...] += jnp.dot(dst, q, preferred_element_type=F32)
            dq_sc[rows, :] += lax.dot_general(dst, k, TN, preferred_element_type=F32)

        _for_groups(j, nq - j, BWD_GROUPS, step)
        dkv_ref[:, :QK_NOPE] = (dk_sc[:, :QK_NOPE] * SCALE).astype(dkv_ref.dtype)
        dkv_ref[:, QK_NOPE:] = dv_sc[...].astype(dkv_ref.dtype)
        mine = pl.ds(pl.multiple_of(j * tq, tq), tq)
        dkr = dk_sc[:, QK_NOPE:] * SCALE

        @pl.when(h == 0)
        def _():
            dkr_ref[mine, :] = dkr

        @pl.when(h > 0)
        def _():
            dkr_ref[mine, :] += dkr

        dq_ref[:, :QK_NOPE] = (dq_sc[mine, :QK_NOPE] * SCALE).astype(dq_ref.dtype)
        dq_ref[:, QK_NOPE:] = _unrope(dq_sc[mine, QK_NOPE:] * SCALE, cc_ref[...], sa_ref[...],
                                      sb_ref[...]).astype(dq_ref.dtype)

    rows = pl.BlockSpec((None, nq, 8, tq), lambda h, j: (h, 0, 0, 0))
    tab = pl.BlockSpec((tq, 128), lambda h, j: (j, 0))
    return pl.pallas_call(
        body, name=name, grid=(N_HEADS, nq),
        in_specs=[pl.BlockSpec((tq, QK_NOPE), lambda h, j: (j, 2 * h)),
                  pl.BlockSpec((tq, V_DIM), lambda h, j: (j, 2 * h + 1)), tab,
                  pl.BlockSpec((s, HEAD_PAD), lambda h, j: (0, h)),
                  pl.BlockSpec((s, V_DIM), lambda h, j: (0, h)), rows, rows, tab, tab, tab],
        out_specs=[pl.BlockSpec((tq, 256), lambda h, j: (j, h)),
                   pl.BlockSpec((s, 128), lambda h, j: (0, 0)),
                   pl.BlockSpec((tq, HEAD_PAD), lambda h, j: (j, h))],
        out_shape=[jax.ShapeDtypeStruct((s, N_HEADS * 256), BF),
                   jax.ShapeDtypeStruct((s, 128), F32),
                   jax.ShapeDtypeStruct((s, N_HEADS * HEAD_PAD), BF)],
        scratch_shapes=[pltpu.VMEM((s, HEAD_PAD), F32), pltpu.VMEM((tq, HEAD_PAD), F32),
                        pltpu.VMEM((tq, V_DIM), F32)],
        compiler_params=_params(2))(kv, kv, krr, qr, do, lse, delta, cc, sa, sb)


def adamw(w, g, m, v, *, name, t=256):
    r, c = w.shape
    t = r if r % t else t
    c1 = 1.0 - ADAM_B1 ** ADAM_STEP
    c2 = 1.0 - ADAM_B2 ** ADAM_STEP

    def body(w_ref, g_ref, m_ref, v_ref, d_ref, nm_ref, nv_ref):
        gv = g_ref[...]
        nm = ADAM_B1 * m_ref[...] + (1.0 - ADAM_B1) * gv
        nv = ADAM_B2 * v_ref[...] + (1.0 - ADAM_B2) * (gv * gv)
        nm_ref[...] = nm
        nv_ref[...] = nv
        d_ref[...] = -ADAM_LR * ((nm / c1) / (jnp.sqrt(nv / c2) + ADAM_EPS) + ADAM_WD * w_ref[...])

    blk = pl.BlockSpec((t, c), lambda i: (i, 0))
    return pl.pallas_call(
        body, name=name, grid=(r // t,), in_specs=[blk] * 4, out_specs=[blk] * 3,
        out_shape=[jax.ShapeDtypeStruct((r, c), F32)] * 3, compiler_params=_params(1))(w, g, m, v)


def sum_devices(parts, *, name):
    def body(p_ref, o_ref):
        acc = p_ref[pl.ds(0, SV_ROWS), :]
        for d in range(1, 8):
            acc = acc + p_ref[pl.ds(d * SV_ROWS, SV_ROWS), :]
        o_ref[...] = acc

    return pl.pallas_call(body, name=name, out_shape=jax.ShapeDtypeStruct((SV_ROWS, SV_COLS), F32))(parts)


def add_halves(g, rb, c_idx, *, name, rows):
    nq, r2, cc = rb.shape
    nb = r2 // rows

    def body(c_ref, g_ref, r_ref, o_ref):
        o_ref[...] = (g_ref[...] + r_ref[...]).astype(o_ref.dtype)

    grid_spec = pltpu.PrefetchScalarGridSpec(
        num_scalar_prefetch=1, grid=(nq, nb),
        in_specs=[pl.BlockSpec((None, rows, cc), lambda q, i, c: (q, c[0] * nb + i, 0)),
                  pl.BlockSpec((None, rows, cc), lambda q, i, c: (q, i, 0))],
        out_specs=pl.BlockSpec((None, rows, cc), lambda q, i, c: (q, i, 0)))
    return pl.pallas_call(body, name=name, grid_spec=grid_spec,
                          out_shape=jax.ShapeDtypeStruct((nq, r2, cc), BF),
                          compiler_params=_params(2))(c_idx, g, rb)


def sum_chips(rc, c_idx, *, name, rows):
    nq, r2, cc = rc.shape
    nb = r2 // rows

    def body(c_ref, r_ref, o_ref):
        parts = [r_ref[q].astype(F32) for q in range(4)]
        o_ref[...] = ((parts[0] + parts[1]) + parts[2]) + parts[3]

    grid_spec = pltpu.PrefetchScalarGridSpec(
        num_scalar_prefetch=1, grid=(nb,),
        in_specs=[pl.BlockSpec((nq, rows, cc), lambda i, c: (0, i, 0))],
        out_specs=pl.BlockSpec((rows, cc), lambda i, c: (c[0] * nb + i, 0)))
    return pl.pallas_call(body, name=name, grid_spec=grid_spec,
                          out_shape=jax.ShapeDtypeStruct((2 * r2, cc), F32),
                          compiler_params=_params(1))(c_idx, rc)


def _place():
    return lax.axis_index("x"), lax.axis_index("y"), lax.axis_index("c")


def all_gather8(xs, *, name, own_half):
    m = xs.shape[0] // 2 if own_half else xs.shape[0]
    n = xs.shape[1]

    def body(x_ref, out_ref, send_sems, recv_sems, local_sem):
        x, y, c = _place()
        me, sibling = (x, y, c), (x, y, 1 - c)
        chips = [(1 - x, y), (x, 1 - y), (1 - x, 1 - y)]
        src_own = x_ref.at[pl.ds(c * m, m), :] if own_half else x_ref

        def rows(px, py, pc):
            return out_ref.at[pl.ds((4 * px + 2 * py + pc) * m, m), :]

        def copy(k, block, to, src=None):
            return pltpu.make_async_remote_copy(
                src_ref=rows(*block) if src is None else src, dst_ref=rows(*block),
                send_sem=send_sems.at[k], recv_sem=recv_sems.at[k], device_id=to, device_id_type=MESH)

        mine = pltpu.make_async_copy(src_own, rows(*me), local_sem)
        mine.start()
        first = [copy(0, me, sibling, src=src_own)]
        first += [copy(1 + j, me, (*chip, c), src=src_own) for j, chip in enumerate(chips)]
        for cp in first:
            cp.start()
        passed = [copy(4 + j, (*chip, c), sibling) for j, chip in enumerate(chips)]
        for j, chip in enumerate(chips):
            copy(1 + j, (*chip, c), me).wait_recv()
            passed[j].start()
        copy(0, sibling, me).wait_recv()
        for j, chip in enumerate(chips):
            copy(4 + j, (*chip, 1 - c), me).wait_recv()
        for cp in first + passed:
            cp.wait_send()
        mine.wait()

    return pl.pallas_call(
        body, name=name, out_shape=jax.ShapeDtypeStruct((8 * m, n), xs.dtype),
        in_specs=[pl.BlockSpec(memory_space=pl.ANY)], out_specs=pl.BlockSpec(memory_space=pl.ANY),
        scratch_shapes=[pltpu.SemaphoreType.DMA((7,)), pltpu.SemaphoreType.DMA((7,)), pltpu.SemaphoreType.DMA],
    )(xs)


def _other_chips():
    x, y, c = _place()
    return [(1 - x, y), (x, 1 - y), (1 - x, 1 - y)]


def _remote(src, dst, send_sems, recv_sems, k, to):
    return pltpu.make_async_remote_copy(src_ref=src, dst_ref=dst, send_sem=send_sems.at[k], recv_sem=recv_sems.at[k],
                                        device_id=to, device_id_type=MESH)


def gather_ici(xs):
    r, cc = xs.shape
    m = r // 2

    def copies(ins, outs, ss, rs, landing):
        x, y, c = _place()
        half = pl.ds(c * m, m)
        return [_remote(ins[0].at[half, :], outs[0].at[(2 * cx + cy) if landing else (2 * x + y), half, :],
                        ss, rs, j, (cx, cy, c)) for j, (cx, cy) in enumerate(_other_chips())]

    def start(ins, outs, ss, rs, ls):
        for cp in copies(ins, outs, ss, rs, False):
            cp.start()

    def wait(ins, outs, ss, rs, ls):
        for cp in copies(ins, outs, ss, rs, True):
            cp.wait_recv()
        for cp in copies(ins, outs, ss, rs, False):
            cp.wait_send()

    return Exchange((xs,), (jax.ShapeDtypeStruct((4, r, cc), xs.dtype),), {}, 3, start, wait)


def gather_forward(buf):
    m = buf.shape[1] // 2

    def copies(outs, ss, rs, landing):
        x, y, c = _place()
        half = pl.ds(((1 - c) if landing else c) * m, m)
        return [_remote(outs[0].at[2 * cx + cy, half, :], outs[0].at[2 * cx + cy, half, :], ss, rs, j, (x, y, 1 - c))
                for j, (cx, cy) in enumerate(_other_chips())]

    def start(ins, outs, ss, rs, ls):
        for cp in copies(outs, ss, rs, False):
            cp.start()

    def wait(ins, outs, ss, rs, ls):
        for cp in copies(outs, ss, rs, True):
            cp.wait_recv()
        for cp in copies(outs, ss, rs, False):
            cp.wait_send()

    return Exchange((buf,), (jax.ShapeDtypeStruct(buf.shape, buf.dtype),), {0: 0}, 3, start, wait)


def swap_halves(g):
    nq, r, cc = g.shape
    r2 = r // 2

    def copy(ins, outs, ss, rs):
        x, y, c = _place()
        return _remote(ins[0].at[:, pl.ds((1 - c) * r2, r2), :], outs[0], ss, rs, 0, (x, y, 1 - c))

    def start(ins, outs, ss, rs, ls):
        copy(ins, outs, ss, rs).start()

    def wait(ins, outs, ss, rs, ls):
        copy(ins, outs, ss, rs).wait()

    return Exchange((g,), (jax.ShapeDtypeStruct((nq, r2, cc), g.dtype),), {}, 1, start, wait)


def exchange_chips(p):
    def own(ins, outs, ls):
        x, y, c = _place()
        return pltpu.make_async_copy(ins[0].at[2 * x + y], outs[0].at[2 * x + y], ls)

    def copies(ins, outs, ss, rs, landing):
        x, y, c = _place()
        return [_remote(ins[0].at[2 * cx + cy], outs[0].at[(2 * cx + cy) if landing else (2 * x + y)],
                        ss, rs, j, (cx, cy, c)) for j, (cx, cy) in enumerate(_other_chips())]

    def start(ins, outs, ss, rs, ls):
        own(ins, outs, ls).start()
        for cp in copies(ins, outs, ss, rs, False):
            cp.start()

    def wait(ins, outs, ss, rs, ls):
        for cp in copies(ins, outs, ss, rs, True):
            cp.wait_recv()
        for cp in copies(ins, outs, ss, rs, False):
            cp.wait_send()
        own(ins, outs, ls).wait()

    return Exchange((p,), (jax.ShapeDtypeStruct(p.shape, p.dtype),), {}, 3, start, wait)


def join_halves(tot):
    r2 = tot.shape[0] // 2

    def copy(outs, ss, rs, landing):
        x, y, c = _place()
        half = outs[0].at[pl.ds(((1 - c) if landing else c) * r2, r2), :]
        return _remote(half, half, ss, rs, 0, (x, y, 1 - c))

    def start(ins, outs, ss, rs, ls):
        copy(outs, ss, rs, False).start()

    def wait(ins, outs, ss, rs, ls):
        copy(outs, ss, rs, True).wait_recv()
        copy(outs, ss, rs, False).wait_send()

    return Exchange((tot,), (jax.ShapeDtypeStruct(tot.shape, tot.dtype),), {0: 0}, 1, start, wait)


def _pack_shard(blocks, small_vec=None):
    parts = [w.reshape(-1, PACK_C).astype(BF) for w in blocks]
    if small_vec is not None:
        srow = lax.bitcast_convert_type(small_vec, BF).reshape(1, PACK_C)
        parts.append(jnp.pad(srow, ((0, PACK_PAD - 1), (0, 0))))
    return jnp.concatenate(parts, axis=0)


def _split_rows(a, rows, axis):
    out, off = [], 0
    for n in rows:
        out.append(lax.slice_in_dim(a, off, off + n, axis=axis))
        off += n
    return out


def _unpack_pool(gw):
    p_in, p_grp, p_out = _split_rows(gw, POOL_ROWS, 1)
    return dict(
        pool_w_in=p_in.reshape(4, D_MODEL, 1024).transpose(1, 0, 2).reshape(D_MODEL, 2 * POOL_WIDTH),
        pool_w_group=p_grp.reshape(4, 4, 128, POOL_GROUP).transpose(1, 0, 2, 3).reshape(4, POOL_GROUP, POOL_GROUP),
        pool_w_out=p_out.reshape(POOL_WIDTH, D_MODEL))


def _unpack_mla(gw):
    m_in, m_qb, m_kvb, m_out, small = _split_rows(gw, MLA_ROWS + (PACK_PAD,), 1)
    w = {}
    win = m_in.reshape(4, D_MODEL, 688).transpose(1, 0, 2).reshape(D_MODEL, 2752)
    w["mla_w_in"] = jnp.concatenate(
        [win[:, 384:640], win[:, 640:704], jnp.zeros((D_MODEL, 64), BF), win[:, 0:384], win[:, 704:]], axis=1)
    wq = m_qb.reshape(4, Q_LORA, 768).transpose(1, 0, 2).reshape(Q_LORA, N_HEADS, QK_NOPE + QK_ROPE)
    w["mla_w_q_b"] = jnp.pad(wq, ((0, 0), (0, 0), (0, HEAD_PAD - QK_NOPE - QK_ROPE))).reshape(Q_LORA, N_HEADS * HEAD_PAD)
    w["mla_w_kv_b"] = m_kvb.reshape(4, KV_LORA, 1024).transpose(1, 0, 2).reshape(KV_LORA, 4096)
    w["mla_w_out"] = m_out.reshape(MLA_WIDTH, D_MODEL)
    small = lax.bitcast_convert_type(small[:, 0, :].reshape(4, 512, 2), F32)
    w["mla_norm"] = small[:, :256].reshape(1, D_MODEL)
    w["mla_q_norm"] = small[:, 256:352].reshape(1, Q_LORA)
    w["mla_kv_norm"] = small[:, 352:416].reshape(1, KV_LORA)
    return w


def _pack_pool_grads(g):
    return jnp.concatenate([
        g["pool_w_in"],
        g["pool_w_group"].reshape(4, 4, 128, POOL_GROUP).transpose(1, 0, 2, 3).reshape(4, 256, PACK_C),
        g["pool_w_out"].reshape(4, 512, PACK_C)], axis=1)


def _pack_mla_grads(g):
    return jnp.concatenate([
        g["mla_w_in"].reshape(D_MODEL, 4, 688).transpose(1, 0, 2).reshape(4, 688, PACK_C),
        g["mla_w_q_b"].reshape(Q_LORA, 4, 768).transpose(1, 0, 2).reshape(4, 288, PACK_C),
        g["mla_w_kv_b"],
        g["mla_w_out"].reshape(4, 512, PACK_C),
        jnp.zeros((4, PACK_PAD, PACK_C), F32)], axis=1)


def kernel(x, positions, pool_norm, pool_w_in, pool_w_group, pool_scale, pool_w_out, mla_norm, mla_w_in, mla_q_norm, mla_w_q_b, mla_kv_norm, mla_w_kv_b, mla_w_out, final_norm, loss_target, m_pool_norm, m_pool_w_in, m_pool_w_group, m_pool_scale, m_pool_w_out, m_mla_norm, m_mla_w_in, m_mla_q_norm, m_mla_w_q_b, m_mla_kv_norm, m_mla_w_kv_b, m_mla_w_out, m_final_norm, v_pool_norm, v_pool_w_in, v_pool_w_group, v_pool_scale, v_pool_w_out, v_mla_norm, v_mla_w_in, v_mla_q_norm, v_mla_w_q_b, v_mla_kv_norm, v_mla_w_kv_b, v_mla_w_out, v_final_norm):
    s = x.shape[1]
    tq = min(512, s)
    x0 = x.reshape(s, D_MODEL)
    tgt = loss_target.reshape(s, D_MODEL)
    cx, cy, cc_idx = _place()
    chip = 2 * cx + cy

    big_names = ("pool_w_in", "pool_w_group", "pool_w_out", "mla_w_in", "mla_w_q_b", "mla_w_kv_b", "mla_w_out")
    big_w = dict(zip(big_names, (pool_w_in, pool_w_group, pool_w_out, mla_w_in, mla_w_q_b, mla_w_kv_b, mla_w_out)))
    big_m = dict(zip(big_names, (m_pool_w_in, m_pool_w_group, m_pool_w_out, m_mla_w_in, m_mla_w_q_b, m_mla_w_kv_b, m_mla_w_out)))
    big_v = dict(zip(big_names, (v_pool_w_in, v_pool_w_group, v_pool_w_out, v_mla_w_in, v_mla_w_q_b, v_mla_w_kv_b, v_mla_w_out)))

    small_vec = jnp.concatenate([mla_norm.reshape(-1), mla_q_norm.reshape(-1), mla_kv_norm.reshape(-1),
                                 jnp.zeros((96,), F32)])
    pool_packed = _pack_shard([big_w[n] for n in big_names[:3]])
    mla_packed = _pack_shard([big_w[n] for n in big_names[3:]], small_vec)
    w = _unpack_pool(all_gather8(pool_packed, name="gather_pool_weights", own_half=True).reshape(4, POOL_R, PACK_C))
    g_pool = pool_norm.reshape(1, D_MODEL)
    g_final = final_norm.reshape(1, D_MODEL)
    sc_pool = pool_scale.reshape(1, POOL_WIDTH)

    inv_freq = 1.0 / (ROPE_THETA ** (jnp.arange(0, QK_ROPE, 2, dtype=F32) / QK_ROPE))
    ang = positions.reshape(s).astype(F32)[:, None] * inv_freq
    cos, sin = jnp.cos(ang), jnp.sin(ang)
    z32, z64, z96 = (jnp.zeros((s, n), F32) for n in (32, 64, 96))
    t_cc = jnp.concatenate([cos, cos, z64], axis=1)
    t_sa = jnp.concatenate([-sin, z96], axis=1)
    t_sb = jnp.concatenate([z32, sin, z64], axis=1)

    h0 = norm_fwd(x0, g_pool, col=0, width=D_MODEL, name="pool_norm_fwd")
    uz, mla_land = mm_nn(h0, w["pool_w_in"], name="pool_in_proj", out_dtype=F32, host=gather_ici(mla_packed))
    pd = pool_prep(uz, name="pool_window")
    mm, y1 = pool_mix_gate(pd, w["pool_w_group"], uz, sc_pool, name="pool_group_mix")
    x1, mla_land = mm_nn(y1, w["pool_w_out"], name="pool_out_proj", out_dtype=F32, add=x0,
                         host=gather_forward(mla_land))
    w.update(_unpack_mla(lax.dynamic_update_slice_in_dim(mla_land, mla_packed[None], chip, axis=0)))

    h1 = norm_fwd(x1, w["mla_norm"], col=0, width=D_MODEL, name="mla_norm_fwd")
    proj = mm_nn(h1, w["mla_w_in"], name="mla_in_proj", out_dtype=F32, tn=P_WIDTH // 2)
    qn = norm_fwd(proj, w["mla_q_norm"], col=P_Q, width=Q_LORA, name="mla_q_norm_fwd")
    kvn = norm_fwd(proj, w["mla_kv_norm"], col=P_KV, width=KV_LORA, name="mla_kv_norm_fwd")
    qr = q_proj_rope(qn, w["mla_w_q_b"], t_cc, t_sa, t_sb, name="mla_q_proj")
    kv = mm_nn(kvn, w["mla_w_kv_b"], name="mla_kv_proj", out_dtype=BF, tk=KV_LORA)
    krr = rope_k(proj, t_cc, t_sa, t_sb, name="mla_rope_k")
    o, y2, lse = attn_fwd(qr, kv, krr, proj, name="mla_attn_fwd", tq=tq)
    dx2, d_final, loss_part = mm_nn_loss(y2, w["mla_w_out"], x1, g_final, tgt, name="mla_out_proj_loss")

    grads = {}
    grads["mla_w_out"] = mm_tn(y2, dx2, name="mla_out_proj_dw")
    do, dz2, delta = mla_out_dx_gate(dx2, w["mla_w_out"], o, proj, name="mla_out_proj_dx", tq=tq)
    dkv, dkr, dq_pre = attn_bwd(qr, kv, krr, do, lse, delta, t_cc, t_sa, t_sb, name="mla_attn_bwd", tq=tq)
    dkr_pre = unrope_k(dkr, t_cc, t_sa, t_sb, name="mla_unrope_k")
    dqn = mm_nt(dq_pre, w["mla_w_q_b"], name="mla_q_proj_dx", out_dtype=F32, tn=Q_LORA, tk=4096)
    g_qb = mm_tn(qn, dq_pre, name="mla_q_proj_dw", tm=Q_LORA, tn=2048)
    dkvn = mm_nt(dkv, w["mla_w_kv_b"], name="mla_kv_proj_dx", out_dtype=F32, tn=KV_LORA, tk=4096)
    grads["mla_w_kv_b"] = mm_tn(kvn, dkv, name="mla_kv_proj_dw", tm=KV_LORA, by_column_block=True)
    dq_lat, d_qnorm = norm_bwd(proj, w["mla_q_norm"], dqn, col=P_Q, width=Q_LORA, name="mla_q_norm_bwd", out_dtype=BF)
    dkv_lat, d_kvnorm = norm_bwd(proj, w["mla_kv_norm"], dkvn, col=P_KV, width=KV_LORA, name="mla_kv_norm_bwd", out_dtype=BF)
    dsmall = jnp.concatenate([dkv_lat, dkr_pre, dq_lat], axis=1)
    dh1 = mm_nt(dsmall, w["mla_w_in"], name="mla_in_proj_dx_a", out_dtype=F32, tk=P_SMALL)
    dx1, d_mnorm = mm_nt_norm_bwd(dz2, w["mla_w_in"][:, P_Z:], dh1, x1, w["mla_norm"], dx2, name="mla_in_proj_dx_b")
    g_in_a = mm_tn(h1, dsmall, name="mla_in_proj_dw_a", tn=P_SMALL)
    g_in_b = mm_tn(h1, dz2, name="mla_in_proj_dw_b")

    g_in = jnp.concatenate([g_in_a, g_in_b], axis=1)
    grads["mla_w_in"] = jnp.concatenate([g_in[:, P_Q:P_Z], g_in[:, P_KV:P_KV + KV_LORA],
                                         g_in[:, P_KR:P_KR + QK_ROPE], g_in[:, P_Z:]], axis=1)
    grads["mla_w_q_b"] = g_qb.reshape(Q_LORA, N_HEADS, HEAD_PAD)[:, :, :QK_NOPE + QK_ROPE].reshape(Q_LORA, -1)
    core_idx = cc_idx.reshape(1).astype(jnp.int32)
    gp_mla = _pack_mla_grads(grads)

    grads["pool_w_out"], sib = mm_tn(y1, dx1, name="pool_out_proj_dw", host=swap_halves(gp_mla))
    pre = add_halves(gp_mla, sib, core_idx, name="mla_grad_add_halves", rows=MLA_R // 2)
    dmm, dz1, d_scale, got = pool_out_dx_gate(dx1, w["pool_w_out"], mm, uz, sc_pool, name="pool_out_proj_dx",
                                              host=exchange_chips(pre))
    tot = sum_chips(got, core_idx, name="mla_grad_sum_chips", rows=MLA_R // 2)
    dpd, red_mla = gmm_nt(dmm, w["pool_w_group"], name="pool_group_mix_dx", host=join_halves(tot))
    grads["pool_w_group"] = gmm_tn(pd, dmm, 4, name="pool_group_mix_dw")
    du = pool_prep_bwd(dpd, name="pool_window_bwd")
    g_pin_u = mm_tn(h0, du, name="pool_in_proj_dw_u", by_column_block=True)
    g_pin_z = mm_tn(h0, dz1, name="pool_in_proj_dw_z", by_column_block=True)
    grads["pool_w_in"] = jnp.concatenate([g_pin_u, g_pin_z], axis=0)

    gp_pool = _pack_pool_grads(grads)
    dh0, sib = mm_nt(du, w["pool_w_in"], name="pool_in_proj_dx_u", out_dtype=F32, host=swap_halves(gp_pool))
    pre = add_halves(gp_pool, sib, core_idx, name="pool_grad_add_halves", rows=POOL_R // 2)
    grad_x, d_pnorm, got = mm_nt_norm_bwd(dz1, w["pool_w_in"], dh0, x0, g_pool, dx1, name="pool_in_proj_dx_z",
                                          b_col=POOL_WIDTH, host=exchange_chips(pre))
    tot = sum_chips(got, core_idx, name="pool_grad_sum_chips", rows=POOL_R // 2)
    red_pool = run_exchange(join_halves(tot), name="pool_grad_join_halves")[0]
    red_parts = _split_rows(red_pool, POOL_ROWS, 0) + _split_rows(red_mla, MLA_ROWS, 0)

    sv = jnp.concatenate([d_pnorm.reshape(-1), d_scale.reshape(-1), d_final.reshape(-1), d_mnorm.reshape(-1),
                          d_qnorm.reshape(-1), d_kvnorm.reshape(-1), loss_part[0, :1],
                          jnp.zeros((SV_ROWS * SV_COLS - SV_OFF["loss"] - 1,), F32)]).reshape(SV_ROWS, SV_COLS)
    sv_all = all_gather8(sv, name="gather_small_grads", own_half=False)
    sv_sum = sum_devices(sv_all, name="sum_small_grads").reshape(-1)
    loss = sv_sum[SV_OFF["loss"]]

    def sv_take(key, n):
        return lax.slice_in_dim(sv_sum, SV_OFF[key], SV_OFF[key] + n)

    out_g, out_d, out_m, out_v = {}, {}, {}, {}
    for name, part in zip(big_names, red_parts):
        shp = big_w[name].shape
        g2 = part.reshape(shp)
        two_d = (-1, shp[-1])
        d_, m_, v_ = adamw(big_w[name].reshape(two_d), g2.reshape(two_d), big_m[name].reshape(two_d),
                           big_v[name].reshape(two_d), name="adamw_" + name)
        out_g[name], out_d[name], out_m[name], out_v[name] = g2, d_.reshape(shp), m_.reshape(shp), v_.reshape(shp)

    small = [
        ("pool_norm", pool_norm, m_pool_norm, v_pool_norm, sv_take("pool_norm", 1024)),
        ("pool_scale", pool_scale, m_pool_scale, v_pool_scale, sv_take("pool_scale", 2048)),
        ("final_norm", final_norm, m_final_norm, v_final_norm, sv_take("final_norm", 1024)),
        ("mla_norm", mla_norm, m_mla_norm, v_mla_norm,
         lax.dynamic_slice_in_dim(sv_take("mla_norm", 1024), chip * 256, 256)),
        ("mla_q_norm", mla_q_norm, m_mla_q_norm, v_mla_q_norm,
         lax.dynamic_slice_in_dim(sv_take("q_norm", 384), chip * 96, 96)),
        ("mla_kv_norm", mla_kv_norm, m_mla_kv_norm, v_mla_kv_norm,
         lax.dynamic_slice_in_dim(sv_take("kv_norm", 256), chip * 64, 64)),
    ]
    sw = jnp.concatenate([t[1].reshape(-1) for t in small] + [jnp.zeros((96,), F32)]).reshape(1, -1)
    sm = jnp.concatenate([t[2].reshape(-1) for t in small] + [jnp.zeros((96,), F32)]).reshape(1, -1)
    s_v = jnp.concatenate([t[3].reshape(-1) for t in small] + [jnp.ones((96,), F32)]).reshape(1, -1)
    sg = jnp.concatenate([t[4].reshape(-1) for t in small] + [jnp.zeros((96,), F32)]).reshape(1, -1)
    sd_, sm_, sv_ = adamw(sw, sg, sm, s_v, name="adamw_vectors")
    off = 0
    for name, wt, _, _, gvec in small:
        n = gvec.shape[0]
        shp = wt.shape
        out_g[name] = gvec.reshape(shp)
        out_d[name] = sd_[0, off:off + n].reshape(shp)
        out_m[name] = sm_[0, off:off + n].reshape(shp)
        out_v[name] = sv_[0, off:off + n].reshape(shp)
        off += n

    order = ("pool_norm", "pool_w_in", "pool_w_group", "pool_scale", "pool_w_out", "mla_norm", "mla_w_in",
             "mla_q_norm", "mla_w_q_b", "mla_kv_norm", "mla_w_kv_b", "mla_w_out", "final_norm")
    return (loss, grad_x.reshape(x.shape), *[out_g[n] for n in order], *[out_d[n] for n in order],
            *[out_m[n] for n in order], *[out_v[n] for n in order])
```

```python
import functools
from typing import Callable, NamedTuple

import jax
import jax.numpy as jnp
from jax import lax
from jax.experimental import pallas as pl
from jax.experimental.pallas import tpu as pltpu

F32 = jnp.float32
BF = jnp.bfloat16
MESH = pl.DeviceIdType.MESH

D_MODEL = 1024
POOL_WIDTH = 2048
POOL_WINDOWS = (2, 4, 8, 16)
POOL_GROUP = 512
HALO = 16
N_HEADS = 16
QK_NOPE = 128
QK_ROPE = 64
V_DIM = 128
HEAD_PAD = 256
Q_LORA = 384
KV_LORA = 256
MLA_WIDTH = 2048
ROPE_THETA = 10000.0
EPS = 1e-6
SCALE = (QK_NOPE + QK_ROPE) ** -0.5
SCALE_LOG2E = SCALE * 1.4426950408889634
NEG = -1e30

P_KV, P_KR, P_Q, P_Z = 0, 256, 384, 768
P_SMALL = 768
P_WIDTH = 2816

ADAM_LR = 0.001
ADAM_B1 = 0.9
ADAM_B2 = 0.999
ADAM_EPS = 1e-08
ADAM_WD = 0.01
ADAM_STEP = 10

NN = (((1,), (0,)), ((), ()))
NT = (((1,), (1,)), ((), ()))
TN = (((0,), (0,)), ((), ()))

POOL_ROWS = (1024, 256, 512)
MLA_ROWS = (688, 288, 256, 512)
PACK_PAD = 16
POOL_R = sum(POOL_ROWS)
MLA_R = sum(MLA_ROWS) + PACK_PAD
PACK_C = 1024
SV_OFF = dict(pool_norm=0, pool_scale=1024, final_norm=3072, mla_norm=4096, q_norm=5120, kv_norm=5504, loss=5760)
SV_ROWS, SV_COLS = 8, 768

VMEM_LIMIT = 56 * 1024 * 1024


def _params(n_axes, vmem=None):
    return pltpu.CompilerParams(dimension_semantics=("arbitrary",) * n_axes,
                                vmem_limit_bytes=VMEM_LIMIT if vmem is None else vmem)


def _sigmoid(z):
    return 1.0 / (1.0 + jnp.exp(-z))


class Exchange(NamedTuple):
    operands: tuple
    out_shapes: tuple
    aliases: dict
    n_sems: int
    start: Callable
    wait: Callable


HBM_SPEC = pl.BlockSpec(memory_space=pl.ANY)


def _exchange_scratch(ex):
    return [pltpu.SemaphoreType.DMA((ex.n_sems,)), pltpu.SemaphoreType.DMA((ex.n_sems,)), pltpu.SemaphoreType.DMA]


def run_exchange(ex, *, name):
    n_in, n_out = len(ex.operands), len(ex.out_shapes)

    def body(*refs):
        args = (refs[:n_in], refs[n_in:n_in + n_out]) + tuple(refs[n_in + n_out:])
        ex.start(*args)
        ex.wait(*args)

    return pl.pallas_call(
        body, name=name, out_shape=list(ex.out_shapes), in_specs=[HBM_SPEC] * n_in,
        out_specs=[HBM_SPEC] * n_out, scratch_shapes=_exchange_scratch(ex),
        input_output_aliases=dict(ex.aliases))(*ex.operands)


def _call(core, *, name, grid, in_specs, out_specs, out_shape, args, scratch=(), host=None):
    in_specs, out_specs, out_shape = list(in_specs), list(out_specs), list(out_shape)
    params = _params(len(grid))
    if host is None:
        return pl.pallas_call(core, name=name, grid=grid, in_specs=in_specs, out_specs=out_specs,
                              out_shape=out_shape, scratch_shapes=list(scratch), compiler_params=params)(*args)
    n_in, n_out = len(in_specs), len(out_specs)
    n_hin, n_hout = len(host.operands), len(host.out_shapes)

    def body(*refs):
        ins, refs = refs[:n_in], refs[n_in:]
        h_in, refs = refs[:n_hin], refs[n_hin:]
        outs, refs = refs[:n_out], refs[n_out:]
        h_out, refs = refs[:n_hout], refs[n_hout:]
        own_scratch, sems = refs[:-3], refs[-3:]
        ids = [pl.program_id(ax) for ax in range(len(grid))]
        first = functools.reduce(jnp.logical_and, [i == 0 for i in ids])
        last = functools.reduce(jnp.logical_and, [i == n - 1 for i, n in zip(ids, grid)])

        @pl.when(first)
        def _():
            host.start(h_in, h_out, *sems)

        core(*ins, *outs, *own_scratch)

        @pl.when(last)
        def _():
            host.wait(h_in, h_out, *sems)

    return pl.pallas_call(
        body, name=name, grid=grid, in_specs=in_specs + [HBM_SPEC] * n_hin,
        out_specs=out_specs + [HBM_SPEC] * n_hout, out_shape=out_shape + list(host.out_shapes),
        scratch_shapes=list(scratch) + _exchange_scratch(host),
        input_output_aliases={n_in + i: n_out + o for i, o in host.aliases.items()},
        compiler_params=params)(*args, *host.operands)


def _mm(a, b, *, dims, grid, a_spec, b_spec, o_spec, out_shape, out_dtype, acc_shape, name,
        add=None, add_spec=None, host=None):
    nk = grid[-1]
    kax = len(grid) - 1

    def body(*refs):
        if add is None:
            a_ref, b_ref, o_ref = refs[:3]
            add_ref = None
            rest = refs[3:]
        else:
            a_ref, b_ref, add_ref, o_ref = refs[:4]
            rest = refs[4:]
        part = lax.dot_general(a_ref[...].astype(BF), b_ref[...].astype(BF), dims,
                               preferred_element_type=F32)

        def finish(r):
            if add_ref is not None:
                r = r + add_ref[...]
            o_ref[...] = r.astype(o_ref.dtype)

        if nk == 1:
            finish(part)
        else:
            acc = rest[0]
            k = pl.program_id(kax)

            @pl.when(k == 0)
            def _():
                acc[...] = part

            @pl.when(k > 0)
            def _():
                acc[...] += part

            @pl.when(k == nk - 1)
            def _():
                finish(acc[...])

    in_specs = [a_spec, b_spec]
    args = [a, b]
    if add is not None:
        in_specs.append(add_spec)
        args.append(add)
    out = _call(body, name=name, grid=grid, in_specs=in_specs, out_specs=[o_spec],
                out_shape=[jax.ShapeDtypeStruct(out_shape, out_dtype)], args=args,
                scratch=[] if nk == 1 else [pltpu.VMEM(acc_shape, F32)], host=host)
    return out[0] if host is None else out


def _pick(n, t):
    t = min(n, t)
    assert n % t == 0, (n, t)
    return t


def mm_nn(a, b, *, name, out_dtype, add=None, tm=1024, tn=1024, tk=2048, host=None):
    m = a.shape[0]
    kk, n = b.shape
    tm, tn, tk = _pick(m, tm), _pick(n, tn), _pick(kk, tk)
    return _mm(a, b, dims=NN, grid=(m // tm, n // tn, kk // tk),
               a_spec=pl.BlockSpec((tm, tk), lambda i, j, k: (i, k)),
               b_spec=pl.BlockSpec((tk, tn), lambda i, j, k: (k, j)),
               o_spec=pl.BlockSpec((tm, tn), lambda i, j, k: (i, j)),
               add=add, add_spec=pl.BlockSpec((tm, tn), lambda i, j, k: (i, j)),
               out_shape=(m, n), out_dtype=out_dtype, acc_shape=(tm, tn), name=name, host=host)


def mm_nt(a, b, *, name, out_dtype, b_col=0, add=None, tm=1024, tn=1024, tk=2048, host=None):
    m, kk = a.shape
    n = b.shape[0]
    tm, tn, tk = _pick(m, tm), _pick(n, tn), _pick(kk, tk)
    assert b_col % tk == 0
    ko = b_col // tk
    return _mm(a, b, dims=NT, grid=(m // tm, n // tn, kk // tk),
               a_spec=pl.BlockSpec((tm, tk), lambda i, j, k: (i, k)),
               b_spec=pl.BlockSpec((tn, tk), lambda i, j, k: (j, ko + k)),
               o_spec=pl.BlockSpec((tm, tn), lambda i, j, k: (i, j)),
               add=add, add_spec=pl.BlockSpec((tm, tn), lambda i, j, k: (i, j)),
               out_shape=(m, n), out_dtype=out_dtype, acc_shape=(tm, tn), name=name, host=host)


def mm_tn(a, b, *, name, tm=1024, tn=1024, tk=2048, host=None, by_column_block=False):
    s, m = a.shape
    n = b.shape[1]
    tm, tn, tk = _pick(m, tm), _pick(n, tn), _pick(s, tk)
    if by_column_block:
        out_shape, o_spec = (n // tn, m, tn), pl.BlockSpec((None, tm, tn), lambda i, j, k: (j, i, 0))
    else:
        out_shape, o_spec = (m, n), pl.BlockSpec((tm, tn), lambda i, j, k: (i, j))
    return _mm(a, b, dims=TN, grid=(m // tm, n // tn, s // tk),
               a_spec=pl.BlockSpec((tk, tm), lambda i, j, k: (k, i)),
               b_spec=pl.BlockSpec((tk, tn), lambda i, j, k: (k, j)),
               o_spec=o_spec, out_shape=out_shape, out_dtype=F32, acc_shape=(tm, tn), name=name, host=host)


def gmm_nt(a, w, *, name, tm=1024, host=None):
    s = a.shape[0]
    g, kk, n = w.shape
    tm = _pick(s, tm)
    return _mm(a, w, dims=NT, grid=(s // tm, g, 1),
               a_spec=pl.BlockSpec((tm, n), lambda i, gi, k: (i, gi)),
               b_spec=pl.BlockSpec((None, kk, n), lambda i, gi, k: (gi, 0, 0)),
               o_spec=pl.BlockSpec((tm, kk), lambda i, gi, k: (i, gi)),
               out_shape=(s, g * kk), out_dtype=F32, acc_shape=(tm, kk), name=name, host=host)


def gmm_tn(a, b, g, *, name, tk=2048):
    s = a.shape[0]
    kk, n = a.shape[1] // g, b.shape[1] // g
    tk = _pick(s, tk)
    return _mm(a, b, dims=TN, grid=(g, s // tk),
               a_spec=pl.BlockSpec((tk, kk), lambda gi, k: (k, gi)),
               b_spec=pl.BlockSpec((tk, n), lambda gi, k: (k, gi)),
               o_spec=pl.BlockSpec((None, kk, n), lambda gi, k: (gi, 0, 0)),
               out_shape=(g, kk, n), out_dtype=F32, acc_shape=(kk, n), name=name)


def norm_fwd(x, g, *, col, width, name, t=512):
    s = x.shape[0]
    t = _pick(s, t)
    cb = col // width
    assert col % width == 0

    def body(x_ref, g_ref, o_ref):
        xv = x_ref[...]
        inv = lax.rsqrt(jnp.mean(xv * xv, axis=-1, keepdims=True) + EPS)
        o_ref[...] = ((xv * inv) * g_ref[...]).astype(o_ref.dtype)

    return pl.pallas_call(
        body, name=name, grid=(s // t,),
        in_specs=[pl.BlockSpec((t, width), lambda i: (i, cb)), pl.BlockSpec((1, width), lambda i: (0, 0))],
        out_specs=pl.BlockSpec((t, width), lambda i: (i, 0)),
        out_shape=jax.ShapeDtypeStruct((s, width), BF), compiler_params=_params(1))(x, g)


def _accumulate(ref, part):
    @pl.when(pl.program_id(0) == 0)
    def _():
        ref[...] = part

    @pl.when(pl.program_id(0) > 0)
    def _():
        ref[...] += part


def norm_bwd(x, g, dh, *, col, width, name, out_dtype, t=512):
    s = x.shape[0]
    t = _pick(s, t)
    cb = col // width
    assert col % width == 0

    def body(x_ref, g_ref, dh_ref, dx_ref, dg_ref):
        xv = x_ref[...]
        inv = lax.rsqrt(jnp.mean(xv * xv, axis=-1, keepdims=True) + EPS)
        xhat = xv * inv
        dh_v = dh_ref[...]
        _accumulate(dg_ref, jnp.sum(dh_v * xhat, axis=0, keepdims=True))
        dxhat = dh_v * g_ref[...]
        dx = inv * (dxhat - xhat * jnp.mean(dxhat * xhat, axis=-1, keepdims=True))
        dx_ref[...] = dx.astype(dx_ref.dtype)

    row = pl.BlockSpec((t, width), lambda i: (i, 0))
    vec = pl.BlockSpec((1, width), lambda i: (0, 0))
    return pl.pallas_call(
        body, name=name, grid=(s // t,), in_specs=[pl.BlockSpec((t, width), lambda i: (i, cb)), vec, row],
        out_specs=[row, vec],
        out_shape=[jax.ShapeDtypeStruct((s, width), out_dtype), jax.ShapeDtypeStruct((1, width), F32)],
        compiler_params=_params(1))(x, g, dh)


def mm_nt_norm_bwd(a, b, add, x, g, res, *, name, b_col=0, tm=512, host=None):
    s, kk = a.shape
    d = b.shape[0]
    tm = _pick(s, tm)
    assert b_col % kk == 0

    def body(a_ref, b_ref, add_ref, x_ref, g_ref, res_ref, dx_ref, dg_ref):
        dh = lax.dot_general(a_ref[...].astype(BF), b_ref[...].astype(BF), NT,
                             preferred_element_type=F32) + add_ref[...]
        xv = x_ref[...]
        inv = lax.rsqrt(jnp.mean(xv * xv, axis=-1, keepdims=True) + EPS)
        xhat = xv * inv
        _accumulate(dg_ref, jnp.sum(dh * xhat, axis=0, keepdims=True))
        dxhat = dh * g_ref[...]
        dx_ref[...] = inv * (dxhat - xhat * jnp.mean(dxhat * xhat, axis=-1, keepdims=True)) + res_ref[...]

    row = pl.BlockSpec((tm, d), lambda i: (i, 0))
    vec = pl.BlockSpec((1, d), lambda i: (0, 0))
    return _call(
        body, name=name, grid=(s // tm,),
        in_specs=[pl.BlockSpec((tm, kk), lambda i: (i, 0)), pl.BlockSpec((d, kk), lambda i: (0, b_col // kk)),
                  row, row, vec, row],
        out_specs=[row, vec],
        out_shape=[jax.ShapeDtypeStruct((s, d), F32), jax.ShapeDtypeStruct((1, d), F32)],
        args=[a, b, add, x, g, res], host=host)


def mm_nn_loss(a, b, add, gf, tgt, *, name, tm=512):
    s, kk = a.shape
    d = b.shape[1]
    tm = _pick(s, tm)

    def body(a_ref, b_ref, add_ref, g_ref, t_ref, dx_ref, dg_ref, loss_ref):
        xv = jnp.dot(a_ref[...].astype(BF), b_ref[...].astype(BF), preferred_element_type=F32) + add_ref[...]
        inv = lax.rsqrt(jnp.mean(xv * xv, axis=-1, keepdims=True) + EPS)
        xhat = xv * inv
        gv = g_ref[...]
        diff = xhat * gv - t_ref[...]
        row_err = jnp.mean(diff * diff, axis=-1, keepdims=True)
        _accumulate(loss_ref, jnp.broadcast_to(0.5 * jnp.sum(row_err, axis=0, keepdims=True), (1, 128)))
        dout = diff * (1.0 / d)
        _accumulate(dg_ref, jnp.sum(dout * xhat, axis=0, keepdims=True))
        dxhat = dout * gv
        dx_ref[...] = inv * (dxhat - xhat * jnp.mean(dxhat * xhat, axis=-1, keepdims=True))

    row = pl.BlockSpec((tm, d), lambda i: (i, 0))
    vec = pl.BlockSpec((1, d), lambda i: (0, 0))
    return _call(
        body, name=name, grid=(s // tm,),
        in_specs=[pl.BlockSpec((tm, kk), lambda i: (i, 0)), pl.BlockSpec((kk, d), lambda i: (0, 0)), row, vec, row],
        out_specs=[row, vec, pl.BlockSpec((1, 128), lambda i: (0, 0))],
        out_shape=[jax.ShapeDtypeStruct((s, d), F32), jax.ShapeDtypeStruct((1, d), F32),
                   jax.ShapeDtypeStruct((1, 128), F32)],
        args=[a, b, add, gf, tgt])


ROW_CHUNK = 56


def pool_prep(uz, *, name, t=256):
    s = uz.shape[0]
    t = _pick(s, t)
    hb = t // HALO

    lead = 2 * HALO
    live = t + lead - 8
    assert live % ROW_CHUNK == 0

    def body(u_ref, halo_ref, o_ref, buf_a, buf_b):
        i = pl.program_id(0)
        buf_a[pl.ds(lead, t), :] = u_ref[...]
        buf_a[pl.ds(0, HALO), :] = jnp.zeros((HALO, POOL_WIDTH), F32)
        buf_b[pl.ds(0, 8), :] = jnp.zeros((8, POOL_WIDTH), F32)

        @pl.when(i == 0)
        def _():
            buf_a[pl.ds(HALO, HALO), :] = jnp.zeros((HALO, POOL_WIDTH), F32)

        @pl.when(i > 0)
        def _():
            buf_a[pl.ds(HALO, HALO), :] = halo_ref[...]

        pos = i * t + lax.broadcasted_iota(jnp.int32, (t, POOL_GROUP), 0)
        for g, w in enumerate(POOL_WINDOWS):
            cols = pl.ds(g * POOL_GROUP, POOL_GROUP)
            src, dst, shift = buf_a, buf_b, 1
            while shift < w:
                for r0 in range(8, 8 + live, ROW_CHUNK):
                    dst[pl.ds(r0, ROW_CHUNK), cols] = (src[pl.ds(r0, ROW_CHUNK), cols]
                                                       + src[pl.ds(r0 - shift, ROW_CHUNK), cols])
                src, dst, shift = dst, src, 2 * shift
            cnt = jnp.minimum(pos + 1, w).astype(F32)
            o_ref[:, cols] = (src[pl.ds(lead, t), cols] / cnt - u_ref[:, cols]).astype(o_ref.dtype)

    return pl.pallas_call(
        body, name=name, grid=(s // t,),
        in_specs=[pl.BlockSpec((t, POOL_WIDTH), lambda i: (i, 0)),
                  pl.BlockSpec((HALO, POOL_WIDTH), lambda i: (jnp.maximum(i * hb - 1, 0), 0))],
        out_specs=pl.BlockSpec((t, POOL_WIDTH), lambda i: (i, 0)),
        out_shape=jax.ShapeDtypeStruct((s, POOL_WIDTH), BF),
        scratch_shapes=[pltpu.VMEM((t + lead, POOL_WIDTH), F32), pltpu.VMEM((t + lead, POOL_WIDTH), F32)],
        compiler_params=_params(1))(uz, uz)


def pool_prep_bwd(dpd, *, name, t=256):
    s = dpd.shape[0]
    t = _pick(s, t)
    hb = t // HALO
    n = s // t

    tail = 2 * HALO
    live = t + tail - 8
    assert live % ROW_CHUNK == 0

    def body(d_ref, halo_ref, o_ref, buf_a, buf_b):
        i = pl.program_id(0)
        buf_a[pl.ds(t + HALO, HALO), :] = jnp.zeros((HALO, POOL_WIDTH), F32)
        buf_b[pl.ds(live, 8), :] = jnp.zeros((8, POOL_WIDTH), F32)
        pos = i * t + lax.broadcasted_iota(jnp.int32, (t, POOL_GROUP), 0)
        for g, w in enumerate(POOL_WINDOWS):
            cols = pl.ds(g * POOL_GROUP, POOL_GROUP)
            cnt = jnp.minimum(pos + 1, w).astype(F32)
            buf_a[pl.ds(0, t), cols] = d_ref[:, cols] / cnt

            @pl.when(i < n - 1)
            def _():
                buf_a[pl.ds(t, HALO), cols] = halo_ref[:, cols] / float(w)

            @pl.when(i == n - 1)
            def _():
                buf_a[pl.ds(t, HALO), cols] = jnp.zeros((HALO, POOL_GROUP), F32)

        for g, w in enumerate(POOL_WINDOWS):
            cols = pl.ds(g * POOL_GROUP, POOL_GROUP)
            src, dst, shift = buf_a, buf_b, 1
            while shift < w:
                for r0 in range(0, live, ROW_CHUNK):
                    dst[pl.ds(r0, ROW_CHUNK), cols] = (src[pl.ds(r0, ROW_CHUNK), cols]
                                                       + src[pl.ds(r0 + shift, ROW_CHUNK), cols])
                src, dst, shift = dst, src, 2 * shift
            o_ref[:, cols] = (src[pl.ds(0, t), cols] - d_ref[:, cols]).astype(o_ref.dtype)

    return pl.pallas_call(
        body, name=name, grid=(n,),
        in_specs=[pl.BlockSpec((t, POOL_WIDTH), lambda i: (i, 0)),
                  pl.BlockSpec((HALO, POOL_WIDTH), lambda i: (jnp.minimum((i + 1) * hb, n * hb - 1), 0))],
        out_specs=pl.BlockSpec((t, POOL_WIDTH), lambda i: (i, 0)),
        out_shape=jax.ShapeDtypeStruct((s, POOL_WIDTH), BF),
        scratch_shapes=[pltpu.VMEM((t + tail, POOL_WIDTH), F32), pltpu.VMEM((t + tail, POOL_WIDTH), F32)],
        compiler_params=_params(1))(dpd, dpd)


CHUNK = 512


def _chunks(width, step=CHUNK):
    return [slice(c, c + step) for c in range(0, width, step)]


def pool_mix_gate(pd, wg, uz, scale, *, name, tm=1024):
    s = pd.shape[0]
    g = wg.shape[0]
    tm = _pick(s, tm)

    def body(a_ref, w_ref, z_ref, sc_ref, mm_ref, y_ref):
        mm = jnp.dot(a_ref[...], w_ref[...], preferred_element_type=F32)
        mm_ref[...] = mm
        z = z_ref[...]
        y_ref[...] = ((mm * sc_ref[...]) * (z * _sigmoid(z))).astype(y_ref.dtype)

    blk = pl.BlockSpec((tm, POOL_GROUP), lambda i, gi: (i, gi))
    return pl.pallas_call(
        body, name=name, grid=(s // tm, g),
        in_specs=[blk, pl.BlockSpec((None, POOL_GROUP, POOL_GROUP), lambda i, gi: (gi, 0, 0)),
                  pl.BlockSpec((tm, POOL_GROUP), lambda i, gi: (i, g + gi)),
                  pl.BlockSpec((1, POOL_GROUP), lambda i, gi: (0, gi))],
        out_specs=[blk, blk],
        out_shape=[jax.ShapeDtypeStruct((s, POOL_WIDTH), F32), jax.ShapeDtypeStruct((s, POOL_WIDTH), BF)],
        compiler_params=_params(2))(pd, wg, uz, scale)


def pool_out_dx_gate(dx, w_out, mm, uz, scale, *, name, tm=512, host=None):
    s, d = dx.shape
    tm = _pick(s, tm)

    def body(dx_ref, w_ref, mm_ref, z_ref, sc_ref, dmm_ref, dz_ref, dsc_ref):
        dxv = dx_ref[...].astype(BF)
        parts = []
        for c in _chunks(POOL_WIDTH):
            dyv = lax.dot_general(dxv, w_ref[c, :], NT, preferred_element_type=F32)
            z = z_ref[:, c]
            sig = _sigmoid(z)
            mmv = mm_ref[:, c]
            scv = sc_ref[:, c]
            dmixed = dyv * (z * sig)
            dmm_ref[:, c] = (dmixed * scv).astype(dmm_ref.dtype)
            dz_ref[:, c] = (dyv * (mmv * scv) * (sig * (1.0 + z * (1.0 - sig)))).astype(dz_ref.dtype)
            parts.append(jnp.sum(dmixed * mmv, axis=0, keepdims=True))

        @pl.when(pl.program_id(0) == 0)
        def _():
            for c, part in zip(_chunks(POOL_WIDTH), parts):
                dsc_ref[:, c] = part

        @pl.when(pl.program_id(0) > 0)
        def _():
            for c, part in zip(_chunks(POOL_WIDTH), parts):
                dsc_ref[:, c] += part

    blk = pl.BlockSpec((tm, POOL_WIDTH), lambda i: (i, 0))
    vec = pl.BlockSpec((1, POOL_WIDTH), lambda i: (0, 0))
    return _call(
        body, name=name, grid=(s // tm,),
        in_specs=[pl.BlockSpec((tm, d), lambda i: (i, 0)), pl.BlockSpec((POOL_WIDTH, d), lambda i: (0, 0)),
                  blk, pl.BlockSpec((tm, POOL_WIDTH), lambda i: (i, 1)), vec],
        out_specs=[blk, blk, vec],
        out_shape=[jax.ShapeDtypeStruct((s, POOL_WIDTH), BF), jax.ShapeDtypeStruct((s, POOL_WIDTH), BF),
                   jax.ShapeDtypeStruct((1, POOL_WIDTH), F32)],
        args=[dx, w_out, mm, uz, scale], host=host)


def _rope(a, cc, sa, sb):
    return a * cc + pltpu.roll(a, 96, 1) * sa + pltpu.roll(a, 32, 1) * sb


def _unrope(d, cc, sa, sb):
    return d * cc + pltpu.roll(d * sa, 32, 1) + pltpu.roll(d * sb, 96, 1)


def q_proj_rope(qn, wq, cc, sa, sb, *, name, tm=1024, heads=8):
    s, kk = qn.shape
    tm = _pick(s, tm)
    tn = heads * HEAD_PAD

    def body(a_ref, b_ref, cc_ref, sa_ref, sb_ref, o_ref):
        q = jnp.dot(a_ref[...], b_ref[...], preferred_element_type=F32)
        for h in range(heads):
            nope = slice(h * HEAD_PAD, h * HEAD_PAD + QK_NOPE)
            rope = slice(h * HEAD_PAD + QK_NOPE, (h + 1) * HEAD_PAD)
            o_ref[:, nope] = q[:, nope].astype(o_ref.dtype)
            o_ref[:, rope] = _rope(q[:, rope], cc_ref[...], sa_ref[...], sb_ref[...]).astype(o_ref.dtype)

    tab = pl.BlockSpec((tm, 128), lambda i, j: (i, 0))
    return pl.pallas_call(
        body, name=name, grid=(s // tm, N_HEADS // heads),
        in_specs=[pl.BlockSpec((tm, kk), lambda i, j: (i, 0)), pl.BlockSpec((kk, tn), lambda i, j: (0, j)),
                  tab, tab, tab],
        out_specs=pl.BlockSpec((tm, tn), lambda i, j: (i, j)),
        out_shape=jax.ShapeDtypeStruct((s, N_HEADS * HEAD_PAD), BF), compiler_params=_params(2))(qn, wq, cc, sa, sb)


def rope_k(proj, cc, sa, sb, *, name, t=512):
    s = proj.shape[0]
    t = _pick(s, t)
    kr_blk = P_KR // 128

    def body(kr_ref, cc_ref, sa_ref, sb_ref, o_ref):
        o_ref[...] = _rope(kr_ref[...], cc_ref[...], sa_ref[...], sb_ref[...]).astype(o_ref.dtype)

    tab = pl.BlockSpec((t, 128), lambda i: (i, 0))
    return pl.pallas_call(
        body, name=name, grid=(s // t,),
        in_specs=[pl.BlockSpec((t, 128), lambda i: (i, kr_blk)), tab, tab, tab], out_specs=tab,
        out_shape=jax.ShapeDtypeStruct((s, 128), BF), compiler_params=_params(1))(proj, cc, sa, sb)


def unrope_k(dkr, cc, sa, sb, *, name, t=512):
    s = dkr.shape[0]
    t = _pick(s, t)

    def body(d_ref, cc_ref, sa_ref, sb_ref, o_ref):
        o_ref[...] = _unrope(d_ref[...], cc_ref[...], sa_ref[...], sb_ref[...]).astype(o_ref.dtype)

    tab = pl.BlockSpec((t, 128), lambda i: (i, 0))
    return pl.pallas_call(
        body, name=name, grid=(s // t,), in_specs=[tab, tab, tab, tab], out_specs=tab,
        out_shape=jax.ShapeDtypeStruct((s, 128), BF), compiler_params=_params(1))(dkr, cc, sa, sb)


def mla_out_dx_gate(dx, w_out, o, proj, *, name, tq):
    s, d = dx.shape
    nq = s // tq

    def body(dx_ref, w_ref, o_ref, p_ref, do_ref, dz_ref, dl_ref):
        dxv = dx_ref[...].astype(BF)
        for c in _chunks(MLA_WIDTH):
            dy_c = lax.dot_general(dxv, w_ref[c, :], NT, preferred_element_type=F32)
            for h in range(c.start // V_DIM, c.stop // V_DIM):
                hc = slice(h * V_DIM, (h + 1) * V_DIM)
                z = p_ref[:, slice(P_Z + hc.start, P_Z + hc.stop)]
                sig = _sigmoid(z)
                dyv = dy_c[:, hc.start - c.start:hc.stop - c.start]
                ov = o_ref[:, hc]
                dov = dyv * (z * sig)
                do_ref[:, hc] = dov.astype(do_ref.dtype)
                dz_ref[:, hc] = (dyv * ov * (sig * (1.0 + z * (1.0 - sig)))).astype(dz_ref.dtype)
                delta = jnp.sum(dov * ov, axis=-1, keepdims=True)
                dl_ref[h] = jnp.broadcast_to(delta, (tq, 128)).T[:8, :]

    blk = pl.BlockSpec((tq, MLA_WIDTH), lambda i: (i, 0))
    return pl.pallas_call(
        body, name=name, grid=(nq,),
        in_specs=[pl.BlockSpec((tq, d), lambda i: (i, 0)), pl.BlockSpec((MLA_WIDTH, d), lambda i: (0, 0)),
                  blk, pl.BlockSpec((tq, P_WIDTH), lambda i: (i, 0))],
        out_specs=[blk, blk, pl.BlockSpec((N_HEADS, None, 8, tq), lambda i: (0, i, 0, 0))],
        out_shape=[jax.ShapeDtypeStruct((s, MLA_WIDTH), BF), jax.ShapeDtypeStruct((s, MLA_WIDTH), BF),
                   jax.ShapeDtypeStruct((N_HEADS, nq, 8, tq), F32)],
        compiler_params=_params(1))(dx, w_out, o, proj)


FWD_GROUPS = (4, 2, 1)
BWD_GROUPS = (4, 2, 1)


def _for_groups(first, count, groups, fn):
    lead = groups[-1]
    for g in groups[:-1][::-1]:
        lead = jnp.where(count >= g, g, lead)
    for g in groups:
        @pl.when(lead == g)
        def _(g=g):
            fn(first, g, True)
    first = first + lead
    count = count - lead
    for g in groups:
        n = count // g

        def one(p, carry, g=g, first=first):
            fn(first + p * g, g, False)
            return carry

        lax.fori_loop(0, n, one, 0)
        first = first + n * g
        count = count - n * g


def attn_fwd(qr, kv, krr, proj, *, name, tq):
    s = qr.shape[0]
    nq = s // tq
    z_blk = P_Z // V_DIM

    def body(kn_ref, v_ref, kr_ref, q_ref, z_ref, o_ref, y_ref, lse_ref, acc_sc, m_sc):
        j = pl.program_id(1)

        @pl.when(j == 0)
        def _():
            acc_sc[...] = jnp.zeros((nq, 2 * V_DIM, tq), F32)
            m_sc[...] = jnp.full((nq, 8, tq), NEG, F32)

        k = jnp.concatenate([kn_ref[...], kr_ref[...]], axis=1)
        vxt = jnp.concatenate([v_ref[...].astype(F32).T.astype(BF), jnp.ones((V_DIM, tq), BF)], axis=0)

        def update(i, n_tiles, masked):
            rows = pl.ds(pl.multiple_of(i * tq, tq), n_tiles * tq)
            st = lax.dot_general(k, q_ref[rows, :], NT, preferred_element_type=F32) * SCALE_LOG2E
            if masked:
                krow = lax.broadcasted_iota(jnp.int32, (tq, n_tiles * tq), 0)
                qcol = lax.broadcasted_iota(jnp.int32, (tq, n_tiles * tq), 1)
                st = jnp.where(qcol >= krow, st, NEG)
            m_prev = jnp.concatenate([m_sc[i + n, pl.ds(0, 1), :] for n in range(n_tiles)], axis=1)
            m_new = jnp.maximum(m_prev, jnp.max(st, axis=0, keepdims=True))
            alpha = jnp.exp2(m_prev - m_new)
            pt = jnp.exp2(st - m_new).astype(BF)
            pv_t = jnp.dot(vxt, pt, preferred_element_type=F32)
            for n in range(n_tiles):
                cols = slice(n * tq, (n + 1) * tq)
                acc_sc[i + n] = alpha[:, cols] * acc_sc[i + n] + pv_t[:, cols]
                m_sc[i + n, pl.ds(0, 1), :] = m_new[:, cols]

        _for_groups(j, nq - j, FWD_GROUPS, update)
        l = acc_sc[j, V_DIM:, :]
        o = (acc_sc[j, :V_DIM, :] / l).T
        o_ref[...] = o
        z = z_ref[...]
        y_ref[...] = (o * (z * _sigmoid(z))).astype(y_ref.dtype)
        lse_ref[...] = m_sc[j, pl.ds(0, 1), :] + jnp.log2(l[:8, :])

    tile = pl.BlockSpec((tq, V_DIM), lambda h, j: (j, h))
    return pl.pallas_call(
        body, name=name, grid=(N_HEADS, nq),
        in_specs=[pl.BlockSpec((tq, QK_NOPE), lambda h, j: (j, 2 * h)),
                  pl.BlockSpec((tq, V_DIM), lambda h, j: (j, 2 * h + 1)),
                  pl.BlockSpec((tq, 128), lambda h, j: (j, 0)),
                  pl.BlockSpec((s, HEAD_PAD), lambda h, j: (0, h)),
                  pl.BlockSpec((tq, V_DIM), lambda h, j: (j, z_blk + h))],
        out_specs=[tile, tile, pl.BlockSpec((None, None, 8, tq), lambda h, j: (h, j, 0, 0))],
        out_shape=[jax.ShapeDtypeStruct((s, N_HEADS * V_DIM), F32),
                   jax.ShapeDtypeStruct((s, N_HEADS * V_DIM), BF),
                   jax.ShapeDtypeStruct((N_HEADS, nq, 8, tq), F32)],
        scratch_shapes=[pltpu.VMEM((nq, 2 * V_DIM, tq), F32), pltpu.VMEM((nq, 8, tq), F32)],
        compiler_params=_params(2))(kv, kv, krr, qr, proj)


def attn_bwd(qr, kv, krr, do, lse, delta, cc, sa, sb, *, name, tq):
    s = qr.shape[0]
    nq = s // tq

    def body(kn_ref, v_ref, kr_ref, q_ref, do_ref, lse_ref, dl_ref, cc_ref, sa_ref, sb_ref,
             dkv_ref, dkr_ref, dq_ref, dq_sc, dk_sc, dv_sc):
        h = pl.program_id(0)
        j = pl.program_id(1)

        @pl.when(j == 0)
        def _():
            dq_sc[...] = jnp.zeros((s, HEAD_PAD), F32)

        dk_sc[...] = jnp.zeros((tq, HEAD_PAD), F32)
        dv_sc[...] = jnp.zeros((tq, V_DIM), F32)
        k = jnp.concatenate([kn_ref[...], kr_ref[...]], axis=1)
        v = v_ref[...]

        def step(i, n_tiles, masked):
            r0 = pl.multiple_of(i * tq, tq)
            rows = pl.ds(r0, n_tiles * tq)
            q = q_ref[rows, :]
            dov = do_ref[rows, :]
            lse_row = jnp.concatenate([lse_ref[i + n, pl.ds(0, 1), :] for n in range(n_tiles)], axis=1)
            dl_row = jnp.concatenate([dl_ref[i + n, pl.ds(0, 1), :] for n in range(n_tiles)], axis=1)
            st = lax.dot_general(k, q, NT, preferred_element_type=F32) * SCALE_LOG2E
            if masked:
                krow = lax.broadcasted_iota(jnp.int32, (tq, n_tiles * tq), 0)
                qcol = lax.broadcasted_iota(jnp.int32, (tq, n_tiles * tq), 1)
                st = jnp.where(qcol >= krow, st, NEG)
            pt = jnp.exp2(st - lse_row)
            dpt = lax.dot_general(v, dov, NT, preferred_element_type=F32)
            dst = (pt * (dpt - dl_row)).astype(BF)
            dv_sc[...] += jnp.dot(pt.astype(BF), dov, preferred_element_type=F32)
            dk_sc[...] += jnp.dot(dst, q, preferred_element_type=F32)
            dq_sc[rows, :] += lax.dot_general(dst, k, TN, preferred_element_type=F32)

        _for_groups(j, nq - j, BWD_GROUPS, step)
        dkv_ref[:, :QK_NOPE] = (dk_sc[:, :QK_NOPE] * SCALE).astype(dkv_ref.dtype)
        dkv_ref[:, QK_NOPE:] = dv_sc[...].astype(dkv_ref.dtype)
        mine = pl.ds(pl.multiple_of(j * tq, tq), tq)
        dkr = dk_sc[:, QK_NOPE:] * SCALE

        @pl.when(h == 0)
        def _():
            dkr_ref[mine, :] = dkr

        @pl.when(h > 0)
        def _():
            dkr_ref[mine, :] += dkr

        dq_ref[:, :QK_NOPE] = (dq_sc[mine, :QK_NOPE] * SCALE).astype(dq_ref.dtype)
        dq_ref[:, QK_NOPE:] = _unrope(dq_sc[mine, QK_NOPE:] * SCALE, cc_ref[...], sa_ref[...],
                                      sb_ref[...]).astype(dq_ref.dtype)

    rows = pl.BlockSpec((None, nq, 8, tq), lambda h, j: (h, 0, 0, 0))
    tab = pl.BlockSpec((tq, 128), lambda h, j: (j, 0))
    return pl.pallas_call(
        body, name=name, grid=(N_HEADS, nq),
        in_specs=[pl.BlockSpec((tq, QK_NOPE), lambda h, j: (j, 2 * h)),
                  pl.BlockSpec((tq, V_DIM), lambda h, j: (j, 2 * h + 1)), tab,
                  pl.BlockSpec((s, HEAD_PAD), lambda h, j: (0, h)),
                  pl.BlockSpec((s, V_DIM), lambda h, j: (0, h)), rows, rows, tab, tab, tab],
        out_specs=[pl.BlockSpec((tq, 256), lambda h, j: (j, h)),
                   pl.BlockSpec((s, 128), lambda h, j: (0, 0)),
                   pl.BlockSpec((tq, HEAD_PAD), lambda h, j: (j, h))],
        out_shape=[jax.ShapeDtypeStruct((s, N_HEADS * 256), BF),
                   jax.ShapeDtypeStruct((s, 128), F32),
                   jax.ShapeDtypeStruct((s, N_HEADS * HEAD_PAD), BF)],
        scratch_shapes=[pltpu.VMEM((s, HEAD_PAD), F32), pltpu.VMEM((tq, HEAD_PAD), F32),
                        pltpu.VMEM((tq, V_DIM), F32)],
        compiler_params=_params(2))(kv, kv, krr, qr, do, lse, delta, cc, sa, sb)


def adamw(w, g, m, v, *, name, t=256):
    r, c = w.shape
    t = r if r % t else t
    c1 = 1.0 - ADAM_B1 ** ADAM_STEP
    c2 = 1.0 - ADAM_B2 ** ADAM_STEP

    def body(w_ref, g_ref, m_ref, v_ref, d_ref, nm_ref, nv_ref):
        gv = g_ref[...]
        nm = ADAM_B1 * m_ref[...] + (1.0 - ADAM_B1) * gv
        nv = ADAM_B2 * v_ref[...] + (1.0 - ADAM_B2) * (gv * gv)
        nm_ref[...] = nm
        nv_ref[...] = nv
        d_ref[...] = -ADAM_LR * ((nm / c1) / (jnp.sqrt(nv / c2) + ADAM_EPS) + ADAM_WD * w_ref[...])

    blk = pl.BlockSpec((t, c), lambda i: (i, 0))
    return pl.pallas_call(
        body, name=name, grid=(r // t,), in_specs=[blk] * 4, out_specs=[blk] * 3,
        out_shape=[jax.ShapeDtypeStruct((r, c), F32)] * 3, compiler_params=_params(1))(w, g, m, v)


def sum_devices(parts, *, name):
    def body(p_ref, o_ref):
        acc = p_ref[pl.ds(0, SV_ROWS), :]
        for d in range(1, 8):
            acc = acc + p_ref[pl.ds(d * SV_ROWS, SV_ROWS), :]
        o_ref[...] = acc

    return pl.pallas_call(body, name=name, out_shape=jax.ShapeDtypeStruct((SV_ROWS, SV_COLS), F32))(parts)


def add_halves(g, rb, c_idx, *, name, rows):
    nq, r2, cc = rb.shape
    nb = r2 // rows

    def body(c_ref, g_ref, r_ref, o_ref):
        o_ref[...] = (g_ref[...] + r_ref[...]).astype(o_ref.dtype)

    grid_spec = pltpu.PrefetchScalarGridSpec(
        num_scalar_prefetch=1, grid=(nq, nb),
        in_specs=[pl.BlockSpec((None, rows, cc), lambda q, i, c: (q, c[0] * nb + i, 0)),
                  pl.BlockSpec((None, rows, cc), lambda q, i, c: (q, i, 0))],
        out_specs=pl.BlockSpec((None, rows, cc), lambda q, i, c: (q, i, 0)))
    return pl.pallas_call(body, name=name, grid_spec=grid_spec,
                          out_shape=jax.ShapeDtypeStruct((nq, r2, cc), BF),
                          compiler_params=_params(2))(c_idx, g, rb)


def sum_chips(rc, c_idx, *, name, rows):
    nq, r2, cc = rc.shape
    nb = r2 // rows

    def body(c_ref, r_ref, o_ref):
        parts = [r_ref[q].astype(F32) for q in range(4)]
        o_ref[...] = ((parts[0] + parts[1]) + parts[2]) + parts[3]

    grid_spec = pltpu.PrefetchScalarGridSpec(
        num_scalar_prefetch=1, grid=(nb,),
        in_specs=[pl.BlockSpec((nq, rows, cc), lambda i, c: (0, i, 0))],
        out_specs=pl.BlockSpec((rows, cc), lambda i, c: (c[0] * nb + i, 0)))
    return pl.pallas_call(body, name=name, grid_spec=grid_spec,
                          out_shape=jax.ShapeDtypeStruct((2 * r2, cc), F32),
                          compiler_params=_params(1))(c_idx, rc)


def _place():
    return lax.axis_index("x"), lax.axis_index("y"), lax.axis_index("c")


def all_gather8(xs, *, name, own_half):
    m = xs.shape[0] // 2 if own_half else xs.shape[0]
    n = xs.shape[1]

    def body(x_ref, out_ref, send_sems, recv_sems, local_sem):
        x, y, c = _place()
        me, sibling = (x, y, c), (x, y, 1 - c)
        chips = [(1 - x, y), (x, 1 - y), (1 - x, 1 - y)]
        src_own = x_ref.at[pl.ds(c * m, m), :] if own_half else x_ref

        def rows(px, py, pc):
            return out_ref.at[pl.ds((4 * px + 2 * py + pc) * m, m), :]

        def copy(k, block, to, src=None):
            return pltpu.make_async_remote_copy(
                src_ref=rows(*block) if src is None else src, dst_ref=rows(*block),
                send_sem=send_sems.at[k], recv_sem=recv_sems.at[k], device_id=to, device_id_type=MESH)

        mine = pltpu.make_async_copy(src_own, rows(*me), local_sem)
        mine.start()
        first = [copy(0, me, sibling, src=src_own)]
        first += [copy(1 + j, me, (*chip, c), src=src_own) for j, chip in enumerate(chips)]
        for cp in first:
            cp.start()
        passed = [copy(4 + j, (*chip, c), sibling) for j, chip in enumerate(chips)]
        for j, chip in enumerate(chips):
            copy(1 + j, (*chip, c), me).wait_recv()
            passed[j].start()
        copy(0, sibling, me).wait_recv()
        for j, chip in enumerate(chips):
            copy(4 + j, (*chip, 1 - c), me).wait_recv()
        for cp in first + passed:
            cp.wait_send()
        mine.wait()

    return pl.pallas_call(
        body, name=name, out_shape=jax.ShapeDtypeStruct((8 * m, n), xs.dtype),
        in_specs=[pl.BlockSpec(memory_space=pl.ANY)], out_specs=pl.BlockSpec(memory_space=pl.ANY),
        scratch_shapes=[pltpu.SemaphoreType.DMA((7,)), pltpu.SemaphoreType.DMA((7,)), pltpu.SemaphoreType.DMA],
    )(xs)


def _other_chips():
    x, y, c = _place()
    return [(1 - x, y), (x, 1 - y), (1 - x, 1 - y)]


def _remote(src, dst, send_sems, recv_sems, k, to):
    return pltpu.make_async_remote_copy(src_ref=src, dst_ref=dst, send_sem=send_sems.at[k], recv_sem=recv_sems.at[k],
                                        device_id=to, device_id_type=MESH)


def gather_ici(xs):
    r, cc = xs.shape
    m = r // 2

    def copies(ins, outs, ss, rs, landing):
        x, y, c = _place()
        half = pl.ds(c * m, m)
        return [_remote(ins[0].at[half, :], outs[0].at[(2 * cx + cy) if landing else (2 * x + y), half, :],
                        ss, rs, j, (cx, cy, c)) for j, (cx, cy) in enumerate(_other_chips())]

    def start(ins, outs, ss, rs, ls):
        for cp in copies(ins, outs, ss, rs, False):
            cp.start()

    def wait(ins, outs, ss, rs, ls):
        for cp in copies(ins, outs, ss, rs, True):
            cp.wait_recv()
        for cp in copies(ins, outs, ss, rs, False):
            cp.wait_send()

    return Exchange((xs,), (jax.ShapeDtypeStruct((4, r, cc), xs.dtype),), {}, 3, start, wait)


def gather_forward(buf):
    m = buf.shape[1] // 2

    def copies(outs, ss, rs, landing):
        x, y, c = _place()
        half = pl.ds(((1 - c) if landing else c) * m, m)
        return [_remote(outs[0].at[2 * cx + cy, half, :], outs[0].at[2 * cx + cy, half, :], ss, rs, j, (x, y, 1 - c))
                for j, (cx, cy) in enumerate(_other_chips())]

    def start(ins, outs, ss, rs, ls):
        for cp in copies(outs, ss, rs, False):
            cp.start()

    def wait(ins, outs, ss, rs, ls):
        for cp in copies(outs, ss, rs, True):
            cp.wait_recv()
        for cp in copies(outs, ss, rs, False):
            cp.wait_send()

    return Exchange((buf,), (jax.ShapeDtypeStruct(buf.shape, buf.dtype),), {0: 0}, 3, start, wait)


def swap_halves(g):
    nq, r, cc = g.shape
    r2 = r // 2

    def copy(ins, outs, ss, rs):
        x, y, c = _place()
        return _remote(ins[0].at[:, pl.ds((1 - c) * r2, r2), :], outs[0], ss, rs, 0, (x, y, 1 - c))

    def start(ins, outs, ss, rs, ls):
        copy(ins, outs, ss, rs).start()

    def wait(ins, outs, ss, rs, ls):
        copy(ins, outs, ss, rs).wait()

    return Exchange((g,), (jax.ShapeDtypeStruct((nq, r2, cc), g.dtype),), {}, 1, start, wait)


def exchange_chips(p):
    def own(ins, outs, ls):
        x, y, c = _place()
        return pltpu.make_async_copy(ins[0].at[2 * x + y], outs[0].at[2 * x + y], ls)

    def copies(ins, outs, ss, rs, landing):
        x, y, c = _place()
        return [_remote(ins[0].at[2 * cx + cy], outs[0].at[(2 * cx + cy) if landing else (2 * x + y)],
                        ss, rs, j, (cx, cy, c)) for j, (cx, cy) in enumerate(_other_chips())]

    def start(ins, outs, ss, rs, ls):
        own(ins, outs, ls).start()
        for cp in copies(ins, outs, ss, rs, False):
            cp.start()

    def wait(ins, outs, ss, rs, ls):
        for cp in copies(ins, outs, ss, rs, True):
            cp.wait_recv()
        for cp in copies(ins, outs, ss, rs, False):
            cp.wait_send()
        own(ins, outs, ls).wait()

    return Exchange((p,), (jax.ShapeDtypeStruct(p.shape, p.dtype),), {}, 3, start, wait)


def join_halves(tot):
    r2 = tot.shape[0] // 2

    def copy(outs, ss, rs, landing):
        x, y, c = _place()
        half = outs[0].at[pl.ds(((1 - c) if landing else c) * r2, r2), :]
        return _remote(half, half, ss, rs, 0, (x, y, 1 - c))

    def start(ins, outs, ss, rs, ls):
        copy(outs, ss, rs, False).start()

    def wait(ins, outs, ss, rs, ls):
        copy(outs, ss, rs, True).wait_recv()
        copy(outs, ss, rs, False).wait_send()

    return Exchange((tot,), (jax.ShapeDtypeStruct(tot.shape, tot.dtype),), {0: 0}, 1, start, wait)


def _pack_shard(blocks, small_vec=None):
    parts = [w.reshape(-1, PACK_C).astype(BF) for w in blocks]
    if small_vec is not None:
        srow = lax.bitcast_convert_type(small_vec, BF).reshape(1, PACK_C)
        parts.append(jnp.pad(srow, ((0, PACK_PAD - 1), (0, 0))))
    return jnp.concatenate(parts, axis=0)


def _split_rows(a, rows, axis):
    out, off = [], 0
    for n in rows:
        out.append(lax.slice_in_dim(a, off, off + n, axis=axis))
        off += n
    return out


def _unpack_pool(gw):
    p_in, p_grp, p_out = _split_rows(gw, POOL_ROWS, 1)
    return dict(
        pool_w_in=p_in.reshape(4, D_MODEL, 1024).transpose(1, 0, 2).reshape(D_MODEL, 2 * POOL_WIDTH),
        pool_w_group=p_grp.reshape(4, 4, 128, POOL_GROUP).transpose(1, 0, 2, 3).reshape(4, POOL_GROUP, POOL_GROUP),
        pool_w_out=p_out.reshape(POOL_WIDTH, D_MODEL))


def _unpack_mla(gw):
    m_in, m_qb, m_kvb, m_out, small = _split_rows(gw, MLA_ROWS + (PACK_PAD,), 1)
    w = {}
    win = m_in.reshape(4, D_MODEL, 688).transpose(1, 0, 2).reshape(D_MODEL, 2752)
    w["mla_w_in"] = jnp.concatenate(
        [win[:, 384:640], win[:, 640:704], jnp.zeros((D_MODEL, 64), BF), win[:, 0:384], win[:, 704:]], axis=1)
    wq = m_qb.reshape(4, Q_LORA, 768).transpose(1, 0, 2).reshape(Q_LORA, N_HEADS, QK_NOPE + QK_ROPE)
    w["mla_w_q_b"] = jnp.pad(wq, ((0, 0), (0, 0), (0, HEAD_PAD - QK_NOPE - QK_ROPE))).reshape(Q_LORA, N_HEADS * HEAD_PAD)
    w["mla_w_kv_b"] = m_kvb.reshape(4, KV_LORA, 1024).transpose(1, 0, 2).reshape(KV_LORA, 4096)
    w["mla_w_out"] = m_out.reshape(MLA_WIDTH, D_MODEL)
    small = lax.bitcast_convert_type(small[:, 0, :].reshape(4, 512, 2), F32)
    w["mla_norm"] = small[:, :256].reshape(1, D_MODEL)
    w["mla_q_norm"] = small[:, 256:352].reshape(1, Q_LORA)
    w["mla_kv_norm"] = small[:, 352:416].reshape(1, KV_LORA)
    return w


def _pack_pool_grads(g):
    return jnp.concatenate([
        g["pool_w_in"],
        g["pool_w_group"].reshape(4, 4, 128, POOL_GROUP).transpose(1, 0, 2, 3).reshape(4, 256, PACK_C),
        g["pool_w_out"].reshape(4, 512, PACK_C)], axis=1)


def _pack_mla_grads(g):
    return jnp.concatenate([
        g["mla_w_in"].reshape(D_MODEL, 4, 688).transpose(1, 0, 2).reshape(4, 688, PACK_C),
        g["mla_w_q_b"].reshape(Q_LORA, 4, 768).transpose(1, 0, 2).reshape(4, 288, PACK_C),
        g["mla_w_kv_b"],
        g["mla_w_out"].reshape(4, 512, PACK_C),
        jnp.zeros((4, PACK_PAD, PACK_C), F32)], axis=1)


def kernel(x, positions, pool_norm, pool_w_in, pool_w_group, pool_scale, pool_w_out, mla_norm, mla_w_in, mla_q_norm, mla_w_q_b, mla_kv_norm, mla_w_kv_b, mla_w_out, final_norm, loss_target, m_pool_norm, m_pool_w_in, m_pool_w_group, m_pool_scale, m_pool_w_out, m_mla_norm, m_mla_w_in, m_mla_q_norm, m_mla_w_q_b, m_mla_kv_norm, m_mla_w_kv_b, m_mla_w_out, m_final_norm, v_pool_norm, v_pool_w_in, v_pool_w_group, v_pool_scale, v_pool_w_out, v_mla_norm, v_mla_w_in, v_mla_q_norm, v_mla_w_q_b, v_mla_kv_norm, v_mla_w_kv_b, v_mla_w_out, v_final_norm):
    s = x.shape[1]
    tq = min(512, s)
    x0 = x.reshape(s, D_MODEL)
    tgt = loss_target.reshape(s, D_MODEL)
    cx, cy, cc_idx = _place()
    chip = 2 * cx + cy

    big_names = ("pool_w_in", "pool_w_group", "pool_w_out", "mla_w_in", "mla_w_q_b", "mla_w_kv_b", "mla_w_out")
    big_w = dict(zip(big_names, (pool_w_in, pool_w_group, pool_w_out, mla_w_in, mla_w_q_b, mla_w_kv_b, mla_w_out)))
    big_m = dict(zip(big_names, (m_pool_w_in, m_pool_w_group, m_pool_w_out, m_mla_w_in, m_mla_w_q_b, m_mla_w_kv_b, m_mla_w_out)))
    big_v = dict(zip(big_names, (v_pool_w_in, v_pool_w_group, v_pool_w_out, v_mla_w_in, v_mla_w_q_b, v_mla_w_kv_b, v_mla_w_out)))

    small_vec = jnp.concatenate([mla_norm.reshape(-1), mla_q_norm.reshape(-1), mla_kv_norm.reshape(-1),
                                 jnp.zeros((96,), F32)])
    pool_packed = _pack_shard([big_w[n] for n in big_names[:3]])
    mla_packed = _pack_shard([big_w[n] for n in big_names[3:]], small_vec)
    w = _unpack_pool(all_gather8(pool_packed, name="gather_pool_weights", own_half=True).reshape(4, POOL_R, PACK_C))
    g_pool = pool_norm.reshape(1, D_MODEL)
    g_final = final_norm.reshape(1, D_MODEL)
    sc_pool = pool_scale.reshape(1, POOL_WIDTH)

    inv_freq = 1.0 / (ROPE_THETA ** (jnp.arange(0, QK_ROPE, 2, dtype=F32) / QK_ROPE))
    ang = positions.reshape(s).astype(F32)[:, None] * inv_freq
    cos, sin = jnp.cos(ang), jnp.sin(ang)
    z32, z64, z96 = (jnp.zeros((s, n), F32) for n in (32, 64, 96))
    t_cc = jnp.concatenate([cos, cos, z64], axis=1)
    t_sa = jnp.concatenate([-sin, z96], axis=1)
    t_sb = jnp.concatenate([z32, sin, z64], axis=1)

    h0 = norm_fwd(x0, g_pool, col=0, width=D_MODEL, name="pool_norm_fwd")
    uz, mla_land = mm_nn(h0, w["pool_w_in"], name="pool_in_proj", out_dtype=F32, host=gather_ici(mla_packed))
    pd = pool_prep(uz, name="pool_window")
    mm, y1 = pool_mix_gate(pd, w["pool_w_group"], uz, sc_pool, name="pool_group_mix")
    x1, mla_land = mm_nn(y1, w["pool_w_out"], name="pool_out_proj", out_dtype=F32, add=x0,
                         host=gather_forward(mla_land))
    w.update(_unpack_mla(lax.dynamic_update_slice_in_dim(mla_land, mla_packed[None], chip, axis=0)))

    h1 = norm_fwd(x1, w["mla_norm"], col=0, width=D_MODEL, name="mla_norm_fwd")
    proj = mm_nn(h1, w["mla_w_in"], name="mla_in_proj", out_dtype=F32, tn=P_WIDTH // 2)
    qn = norm_fwd(proj, w["mla_q_norm"], col=P_Q, width=Q_LORA, name="mla_q_norm_fwd")
    kvn = norm_fwd(proj, w["mla_kv_norm"], col=P_KV, width=KV_LORA, name="mla_kv_norm_fwd")
    qr = q_proj_rope(qn, w["mla_w_q_b"], t_cc, t_sa, t_sb, name="mla_q_proj")
    kv = mm_nn(kvn, w["mla_w_kv_b"], name="mla_kv_proj", out_dtype=BF, tn=2048)
    krr = rope_k(proj, t_cc, t_sa, t_sb, name="mla_rope_k")
    o, y2, lse = attn_fwd(qr, kv, krr, proj, name="mla_attn_fwd", tq=tq)
    dx2, d_final, loss_part = mm_nn_loss(y2, w["mla_w_out"], x1, g_final, tgt, name="mla_out_proj_loss")

    grads = {}
    grads["mla_w_out"] = mm_tn(y2, dx2, name="mla_out_proj_dw")
    do, dz2, delta = mla_out_dx_gate(dx2, w["mla_w_out"], o, proj, name="mla_out_proj_dx", tq=tq)
    dkv, dkr, dq_pre = attn_bwd(qr, kv, krr, do, lse, delta, t_cc, t_sa, t_sb, name="mla_attn_bwd", tq=tq)
    dkr_pre = unrope_k(dkr, t_cc, t_sa, t_sb, name="mla_unrope_k")
    dqn = mm_nt(dq_pre, w["mla_w_q_b"], name="mla_q_proj_dx", out_dtype=F32, tn=Q_LORA, tk=4096)
    g_qb = mm_tn(qn, dq_pre, name="mla_q_proj_dw", tm=Q_LORA, tn=2048)
    dkvn = mm_nt(dkv, w["mla_w_kv_b"], name="mla_kv_proj_dx", out_dtype=F32, tn=KV_LORA, tk=4096)
    grads["mla_w_kv_b"] = mm_tn(kvn, dkv, name="mla_kv_proj_dw", tm=KV_LORA, by_column_block=True)
    dq_lat, d_qnorm = norm_bwd(proj, w["mla_q_norm"], dqn, col=P_Q, width=Q_LORA, name="mla_q_norm_bwd", out_dtype=BF)
    dkv_lat, d_kvnorm = norm_bwd(proj, w["mla_kv_norm"], dkvn, col=P_KV, width=KV_LORA, name="mla_kv_norm_bwd", out_dtype=BF)
    dsmall = jnp.concatenate([dkv_lat, dkr_pre, dq_lat], axis=1)
    dh1 = mm_nt(dsmall, w["mla_w_in"], name="mla_in_proj_dx_a", out_dtype=F32, tk=P_SMALL)
    dx1, d_mnorm = mm_nt_norm_bwd(dz2, w["mla_w_in"][:, P_Z:], dh1, x1, w["mla_norm"], dx2, name="mla_in_proj_dx_b")
    g_in_a = mm_tn(h1, dsmall, name="mla_in_proj_dw_a", tn=P_SMALL)
    g_in_b = mm_tn(h1, dz2, name="mla_in_proj_dw_b")

    g_in = jnp.concatenate([g_in_a, g_in_b], axis=1)
    grads["mla_w_in"] = jnp.concatenate([g_in[:, P_Q:P_Z], g_in[:, P_KV:P_KV + KV_LORA],
                                         g_in[:, P_KR:P_KR + QK_ROPE], g_in[:, P_Z:]], axis=1)
    grads["mla_w_q_b"] = g_qb.reshape(Q_LORA, N_HEADS, HEAD_PAD)[:, :, :QK_NOPE + QK_ROPE].reshape(Q_LORA, -1)
    core_idx = cc_idx.reshape(1).astype(jnp.int32)
    gp_mla = _pack_mla_grads(grads)

    grads["pool_w_out"], sib = mm_tn(y1, dx1, name="pool_out_proj_dw", host=swap_halves(gp_mla))
    pre = add_halves(gp_mla, sib, core_idx, name="mla_grad_add_halves", rows=MLA_R // 2)
    dmm, dz1, d_scale, got = pool_out_dx_gate(dx1, w["pool_w_out"], mm, uz, sc_pool, name="pool_out_proj_dx",
                                              host=exchange_chips(pre))
    tot = sum_chips(got, core_idx, name="mla_grad_sum_chips", rows=MLA_R // 2)
    dpd, red_mla = gmm_nt(dmm, w["pool_w_group"], name="pool_group_mix_dx", host=join_halves(tot))
    grads["pool_w_group"] = gmm_tn(pd, dmm, 4, name="pool_group_mix_dw")
    du = pool_prep_bwd(dpd, name="pool_window_bwd")
    g_pin_u = mm_tn(h0, du, name="pool_in_proj_dw_u", by_column_block=True)
    g_pin_z = mm_tn(h0, dz1, name="pool_in_proj_dw_z", by_column_block=True)
    grads["pool_w_in"] = jnp.concatenate([g_pin_u, g_pin_z], axis=0)

    gp_pool = _pack_pool_grads(grads)
    dh0, sib = mm_nt(du, w["pool_w_in"], name="pool_in_proj_dx_u", out_dtype=F32, host=swap_halves(gp_pool))
    pre = add_halves(gp_pool, sib, core_idx, name="pool_grad_add_halves", rows=POOL_R // 2)
    grad_x, d_pnorm, got = mm_nt_norm_bwd(dz1, w["pool_w_in"], dh0, x0, g_pool, dx1, name="pool_in_proj_dx_z",
                                          b_col=POOL_WIDTH, host=exchange_chips(pre))
    tot = sum_chips(got, core_idx, name="pool_grad_sum_chips", rows=POOL_R // 2)
    red_pool = run_exchange(join_halves(tot), name="pool_grad_join_halves")[0]
    red_parts = _split_rows(red_pool, POOL_ROWS, 0) + _split_rows(red_mla, MLA_ROWS, 0)

    sv = jnp.concatenate([d_pnorm.reshape(-1), d_scale.reshape(-1), d_final.reshape(-1), d_mnorm.reshape(-1),
                          d_qnorm.reshape(-1), d_kvnorm.reshape(-1), loss_part[0, :1],
                          jnp.zeros((SV_ROWS * SV_COLS - SV_OFF["loss"] - 1,), F32)]).reshape(SV_ROWS, SV_COLS)
    sv_all = all_gather8(sv, name="gather_small_grads", own_half=False)
    sv_sum = sum_devices(sv_all, name="sum_small_grads").reshape(-1)
    loss = sv_sum[SV_OFF["loss"]]

    def sv_take(key, n):
        return lax.slice_in_dim(sv_sum, SV_OFF[key], SV_OFF[key] + n)

    out_g, out_d, out_m, out_v = {}, {}, {}, {}
    for name, part in zip(big_names, red_parts):
        shp = big_w[name].shape
        g2 = part.reshape(shp)
        two_d = (-1, shp[-1])
        d_, m_, v_ = adamw(big_w[name].reshape(two_d), g2.reshape(two_d), big_m[name].reshape(two_d),
                           big_v[name].reshape(two_d), name="adamw_" + name)
        out_g[name], out_d[name], out_m[name], out_v[name] = g2, d_.reshape(shp), m_.reshape(shp), v_.reshape(shp)

    small = [
        ("pool_norm", pool_norm, m_pool_norm, v_pool_norm, sv_take("pool_norm", 1024)),
        ("pool_scale", pool_scale, m_pool_scale, v_pool_scale, sv_take("pool_scale", 2048)),
        ("final_norm", final_norm, m_final_norm, v_final_norm, sv_take("final_norm", 1024)),
        ("mla_norm", mla_norm, m_mla_norm, v_mla_norm,
         lax.dynamic_slice_in_dim(sv_take("mla_norm", 1024), chip * 256, 256)),
        ("mla_q_norm", mla_q_norm, m_mla_q_norm, v_mla_q_norm,
         lax.dynamic_slice_in_dim(sv_take("q_norm", 384), chip * 96, 96)),
        ("mla_kv_norm", mla_kv_norm, m_mla_kv_norm, v_mla_kv_norm,
         lax.dynamic_slice_in_dim(sv_take("kv_norm", 256), chip * 64, 64)),
    ]
    sw = jnp.concatenate([t[1].reshape(-1) for t in small] + [jnp.zeros((96,), F32)]).reshape(1, -1)
    sm = jnp.concatenate([t[2].reshape(-1) for t in small] + [jnp.zeros((96,), F32)]).reshape(1, -1)
    s_v = jnp.concatenate([t[3].reshape(-1) for t in small] + [jnp.ones((96,), F32)]).reshape(1, -1)
    sg = jnp.concatenate([t[4].reshape(-1) for t in small] + [jnp.zeros((96,), F32)]).reshape(1, -1)
    sd_, sm_, sv_ = adamw(sw, sg, sm, s_v, name="adamw_vectors")
    off = 0
    for name, wt, _, _, gvec in small:
        n = gvec.shape[0]
        shp = wt.shape
        out_g[name] = gvec.reshape(shp)
        out_d[name] = sd_[0, off:off + n].reshape(shp)
        out_m[name] = sm_[0, off:off + n].reshape(shp)
        out_v[name] = sv_[0, off:off + n].reshape(shp)
        off += n

    order = ("pool_norm", "pool_w_in", "pool_w_group", "pool_scale", "pool_w_out", "mla_norm", "mla_w_in",
             "mla_q_norm", "mla_w_q_b", "mla_kv_norm", "mla_w_kv_b", "mla_w_out", "final_norm")
    return (loss, grad_x.reshape(x.shape), *[out_g[n] for n in order], *[out_d[n] for n in order],
            *[out_m[n] for n in order], *[out_v[n] for n in order])
```

```python
import functools
from typing import Callable, NamedTuple

import jax
import jax.numpy as jnp
from jax import lax
from jax.experimental import pallas as pl
from jax.experimental.pallas import tpu as pltpu

F32 = jnp.float32
BF = jnp.bfloat16
MESH = pl.DeviceIdType.MESH

D_MODEL = 1024
POOL_WIDTH = 2048
POOL_WINDOWS = (2, 4, 8, 16)
POOL_GROUP = 512
HALO = 16
N_HEADS = 16
QK_NOPE = 128
QK_ROPE = 64
V_DIM = 128
HEAD_PAD = 256
Q_LORA = 384
KV_LORA = 256
MLA_WIDTH = 2048
ROPE_THETA = 10000.0
EPS = 1e-6
SCALE = (QK_NOPE + QK_ROPE) ** -0.5
SCALE_LOG2E = SCALE * 1.4426950408889634
NEG = -1e30

P_KV, P_KR, P_Q, P_Z = 0, 256, 384, 768
P_SMALL = 768
P_WIDTH = 2816

ADAM_LR = 0.001
ADAM_B1 = 0.9
ADAM_B2 = 0.999
ADAM_EPS = 1e-08
ADAM_WD = 0.01
ADAM_STEP = 10

NN = (((1,), (0,)), ((), ()))
NT = (((1,), (1,)), ((), ()))
TN = (((0,), (0,)), ((), ()))

POOL_ROWS = (1024, 256, 512)
MLA_ROWS = (688, 288, 256, 512)
PACK_PAD = 16
POOL_R = sum(POOL_ROWS)
MLA_R = sum(MLA_ROWS) + PACK_PAD
PACK_C = 1024
SV_OFF = dict(pool_norm=0, pool_scale=1024, final_norm=3072, mla_norm=4096, q_norm=5120, kv_norm=5504, loss=5760)
SV_ROWS, SV_COLS = 8, 768

VMEM_LIMIT = 56 * 1024 * 1024


def _params(n_axes, vmem=None):
    return pltpu.CompilerParams(dimension_semantics=("arbitrary",) * n_axes,
                                vmem_limit_bytes=VMEM_LIMIT if vmem is None else vmem)


def _sigmoid(z):
    return 1.0 / (1.0 + jnp.exp(-z))


class Exchange(NamedTuple):
    operands: tuple
    out_shapes: tuple
    aliases: dict
    n_sems: int
    start: Callable
    wait: Callable


HBM_SPEC = pl.BlockSpec(memory_space=pl.ANY)


def _exchange_scratch(ex):
    return [pltpu.SemaphoreType.DMA((ex.n_sems,)), pltpu.SemaphoreType.DMA((ex.n_sems,)), pltpu.SemaphoreType.DMA]


def run_exchange(ex, *, name):
    n_in, n_out = len(ex.operands), len(ex.out_shapes)

    def body(*refs):
        args = (refs[:n_in], refs[n_in:n_in + n_out]) + tuple(refs[n_in + n_out:])
        ex.start(*args)
        ex.wait(*args)

    return pl.pallas_call(
        body, name=name, out_shape=list(ex.out_shapes), in_specs=[HBM_SPEC] * n_in,
        out_specs=[HBM_SPEC] * n_out, scratch_shapes=_exchange_scratch(ex),
        input_output_aliases=dict(ex.aliases))(*ex.operands)


def _call(core, *, name, grid, in_specs, out_specs, out_shape, args, scratch=(), host=None):
    in_specs, out_specs, out_shape = list(in_specs), list(out_specs), list(out_shape)
    params = _params(len(grid))
    if host is None:
        return pl.pallas_call(core, name=name, grid=grid, in_specs=in_specs, out_specs=out_specs,
                              out_shape=out_shape, scratch_shapes=list(scratch), compiler_params=params)(*args)
    n_in, n_out = len(in_specs), len(out_specs)
    n_hin, n_hout = len(host.operands), len(host.out_shapes)

    def body(*refs):
        ins, refs = refs[:n_in], refs[n_in:]
        h_in, refs = refs[:n_hin], refs[n_hin:]
        outs, refs = refs[:n_out], refs[n_out:]
        h_out, refs = refs[:n_hout], refs[n_hout:]
        own_scratch, sems = refs[:-3], refs[-3:]
        ids = [pl.program_id(ax) for ax in range(len(grid))]
        first = functools.reduce(jnp.logical_and, [i == 0 for i in ids])
        last = functools.reduce(jnp.logical_and, [i == n - 1 for i, n in zip(ids, grid)])

        @pl.when(first)
        def _():
            host.start(h_in, h_out, *sems)

        core(*ins, *outs, *own_scratch)

        @pl.when(last)
        def _():
            host.wait(h_in, h_out, *sems)

    return pl.pallas_call(
        body, name=name, grid=grid, in_specs=in_specs + [HBM_SPEC] * n_hin,
        out_specs=out_specs + [HBM_SPEC] * n_hout, out_shape=out_shape + list(host.out_shapes),
        scratch_shapes=list(scratch) + _exchange_scratch(host),
        input_output_aliases={n_in + i: n_out + o for i, o in host.aliases.items()},
        compiler_params=params)(*args, *host.operands)


def _mm(a, b, *, dims, grid, a_spec, b_spec, o_spec, out_shape, out_dtype, acc_shape, name,
        add=None, add_spec=None, host=None):
    nk = grid[-1]
    kax = len(grid) - 1

    def body(*refs):
        if add is None:
            a_ref, b_ref, o_ref = refs[:3]
            add_ref = None
            rest = refs[3:]
        else:
            a_ref, b_ref, add_ref, o_ref = refs[:4]
            rest = refs[4:]
        part = lax.dot_general(a_ref[...].astype(BF), b_ref[...].astype(BF), dims,
                               preferred_element_type=F32)

        def finish(r):
            if add_ref is not None:
                r = r + add_ref[...]
            o_ref[...] = r.astype(o_ref.dtype)

        if nk == 1:
            finish(part)
        else:
            acc = rest[0]
            k = pl.program_id(kax)

            @pl.when(k == 0)
            def _():
                acc[...] = part

            @pl.when(k > 0)
            def _():
                acc[...] += part

            @pl.when(k == nk - 1)
            def _():
                finish(acc[...])

    in_specs = [a_spec, b_spec]
    args = [a, b]
    if add is not None:
        in_specs.append(add_spec)
        args.append(add)
    out = _call(body, name=name, grid=grid, in_specs=in_specs, out_specs=[o_spec],
                out_shape=[jax.ShapeDtypeStruct(out_shape, out_dtype)], args=args,
                scratch=[] if nk == 1 else [pltpu.VMEM(acc_shape, F32)], host=host)
    return out[0] if host is None else out


def _pick(n, t):
    t = min(n, t)
    assert n % t == 0, (n, t)
    return t


def mm_nn(a, b, *, name, out_dtype, add=None, tm=1024, tn=1024, tk=2048, host=None):
    m = a.shape[0]
    kk, n = b.shape
    tm, tn, tk = _pick(m, tm), _pick(n, tn), _pick(kk, tk)
    return _mm(a, b, dims=NN, grid=(m // tm, n // tn, kk // tk),
               a_spec=pl.BlockSpec((tm, tk), lambda i, j, k: (i, k)),
               b_spec=pl.BlockSpec((tk, tn), lambda i, j, k: (k, j)),
               o_spec=pl.BlockSpec((tm, tn), lambda i, j, k: (i, j)),
               add=add, add_spec=pl.BlockSpec((tm, tn), lambda i, j, k: (i, j)),
               out_shape=(m, n), out_dtype=out_dtype, acc_shape=(tm, tn), name=name, host=host)


def mm_nt(a, b, *, name, out_dtype, b_col=0, add=None, tm=1024, tn=1024, tk=2048, host=None):
    m, kk = a.shape
    n = b.shape[0]
    tm, tn, tk = _pick(m, tm), _pick(n, tn), _pick(kk, tk)
    assert b_col % tk == 0
    ko = b_col // tk
    return _mm(a, b, dims=NT, grid=(m // tm, n // tn, kk // tk),
               a_spec=pl.BlockSpec((tm, tk), lambda i, j, k: (i, k)),
               b_spec=pl.BlockSpec((tn, tk), lambda i, j, k: (j, ko + k)),
               o_spec=pl.BlockSpec((tm, tn), lambda i, j, k: (i, j)),
               add=add, add_spec=pl.BlockSpec((tm, tn), lambda i, j, k: (i, j)),
               out_shape=(m, n), out_dtype=out_dtype, acc_shape=(tm, tn), name=name, host=host)


def mm_tn(a, b, *, name, tm=1024, tn=1024, tk=2048, host=None, by_column_block=False):
    s, m = a.shape
    n = b.shape[1]
    tm, tn, tk = _pick(m, tm), _pick(n, tn), _pick(s, tk)
    if by_column_block:
        out_shape, o_spec = (n // tn, m, tn), pl.BlockSpec((None, tm, tn), lambda i, j, k: (j, i, 0))
    else:
        out_shape, o_spec = (m, n), pl.BlockSpec((tm, tn), lambda i, j, k: (i, j))
    return _mm(a, b, dims=TN, grid=(m // tm, n // tn, s // tk),
               a_spec=pl.BlockSpec((tk, tm), lambda i, j, k: (k, i)),
               b_spec=pl.BlockSpec((tk, tn), lambda i, j, k: (k, j)),
               o_spec=o_spec, out_shape=out_shape, out_dtype=F32, acc_shape=(tm, tn), name=name, host=host)


def gmm_nt(a, w, *, name, tm=1024, host=None):
    s = a.shape[0]
    g, kk, n = w.shape
    tm = _pick(s, tm)
    return _mm(a, w, dims=NT, grid=(s // tm, g, 1),
               a_spec=pl.BlockSpec((tm, n), lambda i, gi, k: (i, gi)),
               b_spec=pl.BlockSpec((None, kk, n), lambda i, gi, k: (gi, 0, 0)),
               o_spec=pl.BlockSpec((tm, kk), lambda i, gi, k: (i, gi)),
               out_shape=(s, g * kk), out_dtype=F32, acc_shape=(tm, kk), name=name, host=host)


def gmm_tn(a, b, g, *, name, tk=2048):
    s = a.shape[0]
    kk, n = a.shape[1] // g, b.shape[1] // g
    tk = _pick(s, tk)
    return _mm(a, b, dims=TN, grid=(g, s // tk),
               a_spec=pl.BlockSpec((tk, kk), lambda gi, k: (k, gi)),
               b_spec=pl.BlockSpec((tk, n), lambda gi, k: (k, gi)),
               o_spec=pl.BlockSpec((None, kk, n), lambda gi, k: (gi, 0, 0)),
               out_shape=(g, kk, n), out_dtype=F32, acc_shape=(kk, n), name=name)


def _rms(xv, gv):
    inv = lax.rsqrt(jnp.mean(xv * xv, axis=-1, keepdims=True) + EPS)
    return (xv * inv) * gv


def _rms_bwd(xv, gv, dh):
    inv = lax.rsqrt(jnp.mean(xv * xv, axis=-1, keepdims=True) + EPS)
    xhat = xv * inv
    dxhat = dh * gv
    dx = inv * (dxhat - xhat * jnp.mean(dxhat * xhat, axis=-1, keepdims=True))
    return dx, jnp.sum(dh * xhat, axis=0, keepdims=True)


def norm_fwd(x, g, *, name, t=512):
    s, width = x.shape
    t = _pick(s, t)

    def body(x_ref, g_ref, o_ref):
        o_ref[...] = _rms(x_ref[...], g_ref[...]).astype(o_ref.dtype)

    row = pl.BlockSpec((t, width), lambda i: (i, 0))
    return pl.pallas_call(
        body, name=name, grid=(s // t,), in_specs=[row, pl.BlockSpec((1, width), lambda i: (0, 0))],
        out_specs=row, out_shape=jax.ShapeDtypeStruct((s, width), BF), compiler_params=_params(1))(x, g)


def _accumulate(ref, part):
    @pl.when(pl.program_id(0) == 0)
    def _():
        ref[...] = part

    @pl.when(pl.program_id(0) > 0)
    def _():
        ref[...] += part


def mm_nt_norm_bwd(a, b, other, x, g, res, *, name, b_col=0, tm=512, host=None):
    s, kk = a.shape
    d = b.shape[0]
    tm = _pick(s, tm)
    assert b_col % kk == 0
    pair = isinstance(other, tuple)

    def body(a_ref, b_ref, *refs):
        dh = lax.dot_general(a_ref[...].astype(BF), b_ref[...].astype(BF), NT, preferred_element_type=F32)
        if pair:
            a2_ref, b2_ref, x_ref, g_ref, res_ref, dx_ref, dg_ref = refs
            dh = dh + lax.dot_general(a2_ref[...].astype(BF), b2_ref[...].astype(BF), NT, preferred_element_type=F32)
        else:
            add_ref, x_ref, g_ref, res_ref, dx_ref, dg_ref = refs
            dh = dh + add_ref[...]
        dx, dg = _rms_bwd(x_ref[...], g_ref[...], dh)
        _accumulate(dg_ref, dg)
        dx_ref[...] = dx + res_ref[...]

    row = pl.BlockSpec((tm, d), lambda i: (i, 0))
    vec = pl.BlockSpec((1, d), lambda i: (0, 0))
    if pair:
        k2 = other[0].shape[1]
        other_specs = [pl.BlockSpec((tm, k2), lambda i: (i, 0)), pl.BlockSpec((d, k2), lambda i: (0, 0))]
        other_args = list(other)
    else:
        other_specs, other_args = [row], [other]
    return _call(
        body, name=name, grid=(s // tm,),
        in_specs=[pl.BlockSpec((tm, kk), lambda i: (i, 0)), pl.BlockSpec((d, kk), lambda i: (0, b_col // kk)),
                  *other_specs, row, vec, row],
        out_specs=[row, vec],
        out_shape=[jax.ShapeDtypeStruct((s, d), F32), jax.ShapeDtypeStruct((1, d), F32)],
        args=[a, b, *other_args, x, g, res], host=host)


def mm_nn_loss(a, b, add, gf, tgt, *, name, tm=512):
    s, kk = a.shape
    d = b.shape[1]
    tm = _pick(s, tm)

    def body(a_ref, b_ref, add_ref, g_ref, t_ref, dx_ref, dg_ref, loss_ref):
        xv = jnp.dot(a_ref[...].astype(BF), b_ref[...].astype(BF), preferred_element_type=F32) + add_ref[...]
        inv = lax.rsqrt(jnp.mean(xv * xv, axis=-1, keepdims=True) + EPS)
        xhat = xv * inv
        gv = g_ref[...]
        diff = xhat * gv - t_ref[...]
        row_err = jnp.mean(diff * diff, axis=-1, keepdims=True)
        _accumulate(loss_ref, jnp.broadcast_to(0.5 * jnp.sum(row_err, axis=0, keepdims=True), (1, 128)))
        dout = diff * (1.0 / d)
        _accumulate(dg_ref, jnp.sum(dout * xhat, axis=0, keepdims=True))
        dxhat = dout * gv
        dx_ref[...] = inv * (dxhat - xhat * jnp.mean(dxhat * xhat, axis=-1, keepdims=True))

    row = pl.BlockSpec((tm, d), lambda i: (i, 0))
    vec = pl.BlockSpec((1, d), lambda i: (0, 0))
    return _call(
        body, name=name, grid=(s // tm,),
        in_specs=[pl.BlockSpec((tm, kk), lambda i: (i, 0)), pl.BlockSpec((kk, d), lambda i: (0, 0)), row, vec, row],
        out_specs=[row, vec, pl.BlockSpec((1, 128), lambda i: (0, 0))],
        out_shape=[jax.ShapeDtypeStruct((s, d), F32), jax.ShapeDtypeStruct((1, d), F32),
                   jax.ShapeDtypeStruct((1, 128), F32)],
        args=[a, b, add, gf, tgt])


ROW_CHUNK = 56


def pool_prep(uz, *, name, t=256):
    s = uz.shape[0]
    t = _pick(s, t)
    hb = t // HALO

    lead = 2 * HALO
    live = t + lead - 8
    assert live % ROW_CHUNK == 0

    def body(u_ref, halo_ref, o_ref, buf_a, buf_b):
        i = pl.program_id(0)
        buf_a[pl.ds(lead, t), :] = u_ref[...]
        buf_a[pl.ds(0, HALO), :] = jnp.zeros((HALO, POOL_WIDTH), F32)
        buf_b[pl.ds(0, 8), :] = jnp.zeros((8, POOL_WIDTH), F32)

        @pl.when(i == 0)
        def _():
            buf_a[pl.ds(HALO, HALO), :] = jnp.zeros((HALO, POOL_WIDTH), F32)

        @pl.when(i > 0)
        def _():
            buf_a[pl.ds(HALO, HALO), :] = halo_ref[...]

        pos = i * t + lax.broadcasted_iota(jnp.int32, (t, POOL_GROUP), 0)
        for g, w in enumerate(POOL_WINDOWS):
            cols = pl.ds(g * POOL_GROUP, POOL_GROUP)
            src, dst, shift = buf_a, buf_b, 1
            while shift < w:
                for r0 in range(8, 8 + live, ROW_CHUNK):
                    dst[pl.ds(r0, ROW_CHUNK), cols] = (src[pl.ds(r0, ROW_CHUNK), cols]
                                                       + src[pl.ds(r0 - shift, ROW_CHUNK), cols])
                src, dst, shift = dst, src, 2 * shift
            cnt = jnp.minimum(pos + 1, w).astype(F32)
            o_ref[:, cols] = (src[pl.ds(lead, t), cols] / cnt - u_ref[:, cols]).astype(o_ref.dtype)

    return pl.pallas_call(
        body, name=name, grid=(s // t,),
        in_specs=[pl.BlockSpec((t, POOL_WIDTH), lambda i: (i, 0)),
                  pl.BlockSpec((HALO, POOL_WIDTH), lambda i: (jnp.maximum(i * hb - 1, 0), 0))],
        out_specs=pl.BlockSpec((t, POOL_WIDTH), lambda i: (i, 0)),
        out_shape=jax.ShapeDtypeStruct((s, POOL_WIDTH), BF),
        scratch_shapes=[pltpu.VMEM((t + lead, POOL_WIDTH), F32), pltpu.VMEM((t + lead, POOL_WIDTH), F32)],
        compiler_params=_params(1))(uz, uz)


def pool_prep_bwd(dpd, *, name, t=256):
    s = dpd.shape[0]
    t = _pick(s, t)
    hb = t // HALO
    n = s // t

    tail = 2 * HALO
    live = t + tail - 8
    assert live % ROW_CHUNK == 0

    def body(d_ref, halo_ref, o_ref, buf_a, buf_b):
        i = pl.program_id(0)
        buf_a[pl.ds(t + HALO, HALO), :] = jnp.zeros((HALO, POOL_WIDTH), F32)
        buf_b[pl.ds(live, 8), :] = jnp.zeros((8, POOL_WIDTH), F32)
        pos = i * t + lax.broadcasted_iota(jnp.int32, (t, POOL_GROUP), 0)
        for g, w in enumerate(POOL_WINDOWS):
            cols = pl.ds(g * POOL_GROUP, POOL_GROUP)
            cnt = jnp.minimum(pos + 1, w).astype(F32)
            buf_a[pl.ds(0, t), cols] = d_ref[:, cols] / cnt

            @pl.when(i < n - 1)
            def _():
                buf_a[pl.ds(t, HALO), cols] = halo_ref[:, cols] / float(w)

            @pl.when(i == n - 1)
            def _():
                buf_a[pl.ds(t, HALO), cols] = jnp.zeros((HALO, POOL_GROUP), F32)

        for g, w in enumerate(POOL_WINDOWS):
            cols = pl.ds(g * POOL_GROUP, POOL_GROUP)
            src, dst, shift = buf_a, buf_b, 1
            while shift < w:
                for r0 in range(0, live, ROW_CHUNK):
                    dst[pl.ds(r0, ROW_CHUNK), cols] = (src[pl.ds(r0, ROW_CHUNK), cols]
                                                       + src[pl.ds(r0 + shift, ROW_CHUNK), cols])
                src, dst, shift = dst, src, 2 * shift
            o_ref[:, cols] = (src[pl.ds(0, t), cols] - d_ref[:, cols]).astype(o_ref.dtype)

    return pl.pallas_call(
        body, name=name, grid=(n,),
        in_specs=[pl.BlockSpec((t, POOL_WIDTH), lambda i: (i, 0)),
                  pl.BlockSpec((HALO, POOL_WIDTH), lambda i: (jnp.minimum((i + 1) * hb, n * hb - 1), 0))],
        out_specs=pl.BlockSpec((t, POOL_WIDTH), lambda i: (i, 0)),
        out_shape=jax.ShapeDtypeStruct((s, POOL_WIDTH), BF),
        scratch_shapes=[pltpu.VMEM((t + tail, POOL_WIDTH), F32), pltpu.VMEM((t + tail, POOL_WIDTH), F32)],
        compiler_params=_params(1))(dpd, dpd)


CHUNK = 512


def _chunks(width, step=CHUNK):
    return [slice(c, c + step) for c in range(0, width, step)]


def pool_mix_gate(pd, wg, uz, scale, *, name, tm=1024):
    s = pd.shape[0]
    g = wg.shape[0]
    tm = _pick(s, tm)

    def body(a_ref, w_ref, z_ref, sc_ref, mm_ref, y_ref):
        mm = jnp.dot(a_ref[...], w_ref[...], preferred_element_type=F32)
        mm_ref[...] = mm
        z = z_ref[...]
        y_ref[...] = ((mm * sc_ref[...]) * (z * _sigmoid(z))).astype(y_ref.dtype)

    blk = pl.BlockSpec((tm, POOL_GROUP), lambda i, gi: (i, gi))
    return pl.pallas_call(
        body, name=name, grid=(s // tm, g),
        in_specs=[blk, pl.BlockSpec((None, POOL_GROUP, POOL_GROUP), lambda i, gi: (gi, 0, 0)),
                  pl.BlockSpec((tm, POOL_GROUP), lambda i, gi: (i, g + gi)),
                  pl.BlockSpec((1, POOL_GROUP), lambda i, gi: (0, gi))],
        out_specs=[blk, blk],
        out_shape=[jax.ShapeDtypeStruct((s, POOL_WIDTH), F32), jax.ShapeDtypeStruct((s, POOL_WIDTH), BF)],
        compiler_params=_params(2))(pd, wg, uz, scale)


def pool_out_dx_gate(dx, w_out, mm, uz, scale, *, name, tm=512, host=None):
    s, d = dx.shape
    tm = _pick(s, tm)

    def body(dx_ref, w_ref, mm_ref, z_ref, sc_ref, dmm_ref, dz_ref, dsc_ref):
        dxv = dx_ref[...].astype(BF)
        parts = []
        for c in _chunks(POOL_WIDTH):
            dyv = lax.dot_general(dxv, w_ref[c, :], NT, preferred_element_type=F32)
            z = z_ref[:, c]
            sig = _sigmoid(z)
            mmv = mm_ref[:, c]
            scv = sc_ref[:, c]
            dmixed = dyv * (z * sig)
            dmm_ref[:, c] = (dmixed * scv).astype(dmm_ref.dtype)
            dz_ref[:, c] = (dyv * (mmv * scv) * (sig * (1.0 + z * (1.0 - sig)))).astype(dz_ref.dtype)
            parts.append(jnp.sum(dmixed * mmv, axis=0, keepdims=True))

        @pl.when(pl.program_id(0) == 0)
        def _():
            for c, part in zip(_chunks(POOL_WIDTH), parts):
                dsc_ref[:, c] = part

        @pl.when(pl.program_id(0) > 0)
        def _():
            for c, part in zip(_chunks(POOL_WIDTH), parts):
                dsc_ref[:, c] += part

    blk = pl.BlockSpec((tm, POOL_WIDTH), lambda i: (i, 0))
    vec = pl.BlockSpec((1, POOL_WIDTH), lambda i: (0, 0))
    return _call(
        body, name=name, grid=(s // tm,),
        in_specs=[pl.BlockSpec((tm, d), lambda i: (i, 0)), pl.BlockSpec((POOL_WIDTH, d), lambda i: (0, 0)),
                  blk, pl.BlockSpec((tm, POOL_WIDTH), lambda i: (i, 1)), vec],
        out_specs=[blk, blk, vec],
        out_shape=[jax.ShapeDtypeStruct((s, POOL_WIDTH), BF), jax.ShapeDtypeStruct((s, POOL_WIDTH), BF),
                   jax.ShapeDtypeStruct((1, POOL_WIDTH), F32)],
        args=[dx, w_out, mm, uz, scale], host=host)


def _rope(a, cc, sa, sb):
    return a * cc + pltpu.roll(a, 96, 1) * sa + pltpu.roll(a, 32, 1) * sb


def _unrope(d, cc, sa, sb):
    return d * cc + pltpu.roll(d * sa, 32, 1) + pltpu.roll(d * sb, 96, 1)


def q_proj_rope(qn, wq, cc, sa, sb, *, name, tm=1024, heads=8):
    s, kk = qn.shape
    tm = _pick(s, tm)
    tn = heads * HEAD_PAD

    def body(a_ref, b_ref, cc_ref, sa_ref, sb_ref, o_ref):
        q = jnp.dot(a_ref[...], b_ref[...], preferred_element_type=F32)
        for h in range(heads):
            nope = slice(h * HEAD_PAD, h * HEAD_PAD + QK_NOPE)
            rope = slice(h * HEAD_PAD + QK_NOPE, (h + 1) * HEAD_PAD)
            o_ref[:, nope] = q[:, nope].astype(o_ref.dtype)
            o_ref[:, rope] = _rope(q[:, rope], cc_ref[...], sa_ref[...], sb_ref[...]).astype(o_ref.dtype)

    tab = pl.BlockSpec((tm, 128), lambda i, j: (i, 0))
    return pl.pallas_call(
        body, name=name, grid=(s // tm, N_HEADS // heads),
        in_specs=[pl.BlockSpec((tm, kk), lambda i, j: (i, 0)), pl.BlockSpec((kk, tn), lambda i, j: (0, j)),
                  tab, tab, tab],
        out_specs=pl.BlockSpec((tm, tn), lambda i, j: (i, j)),
        out_shape=jax.ShapeDtypeStruct((s, N_HEADS * HEAD_PAD), BF), compiler_params=_params(2))(qn, wq, cc, sa, sb)


LAT_KV = slice(P_KV, P_KV + KV_LORA)
LAT_KR = slice(P_KR, P_KR + 128)
LAT_Q = slice(P_Q, P_Q + Q_LORA)


def latent_fwd(proj, g_q, g_kv, cc, sa, sb, *, name, t=512):
    s = proj.shape[0]
    t = _pick(s, t)

    def body(p_ref, gq_ref, gkv_ref, cc_ref, sa_ref, sb_ref, qn_ref, kvn_ref, kr_ref):
        qn_ref[...] = _rms(p_ref[:, LAT_Q], gq_ref[...]).astype(qn_ref.dtype)
        kvn_ref[...] = _rms(p_ref[:, LAT_KV], gkv_ref[...]).astype(kvn_ref.dtype)
        kr_ref[...] = _rope(p_ref[:, LAT_KR], cc_ref[...], sa_ref[...], sb_ref[...]).astype(kr_ref.dtype)

    tab = pl.BlockSpec((t, 128), lambda i: (i, 0))
    return pl.pallas_call(
        body, name=name, grid=(s // t,),
        in_specs=[pl.BlockSpec((t, P_SMALL), lambda i: (i, 0)), pl.BlockSpec((1, Q_LORA), lambda i: (0, 0)),
                  pl.BlockSpec((1, KV_LORA), lambda i: (0, 0)), tab, tab, tab],
        out_specs=[pl.BlockSpec((t, Q_LORA), lambda i: (i, 0)), pl.BlockSpec((t, KV_LORA), lambda i: (i, 0)), tab],
        out_shape=[jax.ShapeDtypeStruct((s, Q_LORA), BF), jax.ShapeDtypeStruct((s, KV_LORA), BF),
                   jax.ShapeDtypeStruct((s, 128), BF)],
        compiler_params=_params(1))(proj, g_q, g_kv, cc, sa, sb)


def latent_bwd(proj, g_q, g_kv, dqn, dkvn, dkr, cc, sa, sb, *, name, t=512):
    s = proj.shape[0]
    t = _pick(s, t)

    def body(p_ref, gq_ref, gkv_ref, dqn_ref, dkvn_ref, dkr_ref, cc_ref, sa_ref, sb_ref, d_ref, dgq_ref, dgkv_ref):
        dq, dgq = _rms_bwd(p_ref[:, LAT_Q], gq_ref[...], dqn_ref[...])
        dkv, dgkv = _rms_bwd(p_ref[:, LAT_KV], gkv_ref[...], dkvn_ref[...])
        d_ref[:, LAT_Q] = dq.astype(d_ref.dtype)
        d_ref[:, LAT_KV] = dkv.astype(d_ref.dtype)
        d_ref[:, LAT_KR] = _unrope(dkr_ref[...], cc_ref[...], sa_ref[...], sb_ref[...]).astype(d_ref.dtype)
        _accumulate(dgq_ref, dgq)
        _accumulate(dgkv_ref, dgkv)

    tab = pl.BlockSpec((t, 128), lambda i: (i, 0))
    small = pl.BlockSpec((t, P_SMALL), lambda i: (i, 0))
    gq = pl.BlockSpec((1, Q_LORA), lambda i: (0, 0))
    gkv = pl.BlockSpec((1, KV_LORA), lambda i: (0, 0))
    return pl.pallas_call(
        body, name=name, grid=(s // t,),
        in_specs=[small, gq, gkv, pl.BlockSpec((t, Q_LORA), lambda i: (i, 0)),
                  pl.BlockSpec((t, KV_LORA), lambda i: (i, 0)), tab, tab, tab, tab],
        out_specs=[small, gq, gkv],
        out_shape=[jax.ShapeDtypeStruct((s, P_SMALL), BF), jax.ShapeDtypeStruct((1, Q_LORA), F32),
                   jax.ShapeDtypeStruct((1, KV_LORA), F32)],
        compiler_params=_params(1))(proj, g_q, g_kv, dqn, dkvn, dkr, cc, sa, sb)


def mla_out_dx_gate(dx, w_out, o, proj, *, name, tq):
    s, d = dx.shape
    nq = s // tq

    def body(dx_ref, w_ref, o_ref, p_ref, do_ref, dz_ref, dl_ref):
        dxv = dx_ref[...].astype(BF)
        lane = lax.broadcasted_iota(jnp.int32, (tq, 128), 1)
        deltas = jnp.zeros((tq, 128), F32)
        for c in _chunks(MLA_WIDTH):
            dy_c = lax.dot_general(dxv, w_ref[c, :], NT, preferred_element_type=F32)
            for h in range(c.start // V_DIM, c.stop // V_DIM):
                hc = slice(h * V_DIM, (h + 1) * V_DIM)
                z = p_ref[:, slice(P_Z + hc.start, P_Z + hc.stop)]
                sig = _sigmoid(z)
                dyv = dy_c[:, hc.start - c.start:hc.stop - c.start]
                ov = o_ref[:, hc]
                dov = dyv * (z * sig)
                do_ref[:, hc] = dov.astype(do_ref.dtype)
                dz_ref[:, hc] = (dyv * ov * (sig * (1.0 + z * (1.0 - sig)))).astype(dz_ref.dtype)
                deltas = jnp.where(lane == h, jnp.sum(dov * ov, axis=-1, keepdims=True), deltas)
        rows = deltas.T
        for h in range(N_HEADS):
            dl_ref[h] = jnp.broadcast_to(rows[h:h + 1, :], (8, tq))

    blk = pl.BlockSpec((tq, MLA_WIDTH), lambda i: (i, 0))
    return pl.pallas_call(
        body, name=name, grid=(nq,),
        in_specs=[pl.BlockSpec((tq, d), lambda i: (i, 0)), pl.BlockSpec((MLA_WIDTH, d), lambda i: (0, 0)),
                  blk, pl.BlockSpec((tq, P_WIDTH), lambda i: (i, 0))],
        out_specs=[blk, blk, pl.BlockSpec((N_HEADS, None, 8, tq), lambda i: (0, i, 0, 0))],
        out_shape=[jax.ShapeDtypeStruct((s, MLA_WIDTH), BF), jax.ShapeDtypeStruct((s, MLA_WIDTH), BF),
                   jax.ShapeDtypeStruct((N_HEADS, nq, 8, tq), F32)],
        compiler_params=_params(1))(dx, w_out, o, proj)


FWD_GROUPS = (4, 2, 1)
BWD_GROUPS = (4, 2, 1)


def _for_groups(first, count, groups, fn):
    lead = groups[-1]
    for g in groups[:-1][::-1]:
        lead = jnp.where(count >= g, g, lead)
    for g in groups:
        @pl.when(lead == g)
        def _(g=g):
            fn(first, g, True)
    first = first + lead
    count = count - lead
    for g in groups:
        n = count // g

        def one(p, carry, g=g, first=first):
            fn(first + p * g, g, False)
            return carry

        lax.fori_loop(0, n, one, 0)
        first = first + n * g
        count = count - n * g


def attn_fwd(qr, kv, krr, proj, *, name, tq):
    s = qr.shape[0]
    nq = s // tq
    z_blk = P_Z // V_DIM

    def body(kn_ref, v_ref, kr_ref, q_ref, z_ref, o_ref, y_ref, lse_ref, acc_sc, m_sc):
        j = pl.program_id(1)

        @pl.when(j == 0)
        def _():
            acc_sc[...] = jnp.zeros((nq, 2 * V_DIM, tq), F32)
            m_sc[...] = jnp.full((nq, 8, tq), NEG, F32)

        k = jnp.concatenate([kn_ref[...], kr_ref[...]], axis=1)
        vxt = jnp.concatenate([v_ref[...].astype(F32).T.astype(BF), jnp.ones((V_DIM, tq), BF)], axis=0)

        def update(i, n_tiles, masked):
            rows = pl.ds(pl.multiple_of(i * tq, tq), n_tiles * tq)
            st = lax.dot_general(k, q_ref[rows, :], NT, preferred_element_type=F32) * SCALE_LOG2E
            if masked:
                krow = lax.broadcasted_iota(jnp.int32, (tq, n_tiles * tq), 0)
                qcol = lax.broadcasted_iota(jnp.int32, (tq, n_tiles * tq), 1)
                st = jnp.where(qcol >= krow, st, NEG)
            m_prev = jnp.concatenate([m_sc[i + n, pl.ds(0, 1), :] for n in range(n_tiles)], axis=1)
            m_new = jnp.maximum(m_prev, jnp.max(st, axis=0, keepdims=True))
            alpha = jnp.exp2(m_prev - m_new)
            pt = jnp.exp2(st - m_new).astype(BF)
            pv_t = jnp.dot(vxt, pt, preferred_element_type=F32)
            for n in range(n_tiles):
                cols = slice(n * tq, (n + 1) * tq)
                acc_sc[i + n] = alpha[:, cols] * acc_sc[i + n] + pv_t[:, cols]
                m_sc[i + n, pl.ds(0, 1), :] = m_new[:, cols]

        _for_groups(j, nq - j, FWD_GROUPS, update)
        l = acc_sc[j, V_DIM:, :]
        o = (acc_sc[j, :V_DIM, :] / l).T
        o_ref[...] = o
        z = z_ref[...]
        y_ref[...] = (o * (z * _sigmoid(z))).astype(y_ref.dtype)
        lse_ref[...] = m_sc[j, pl.ds(0, 1), :] + jnp.log2(l[:8, :])

    tile = pl.BlockSpec((tq, V_DIM), lambda h, j: (j, h))
    return pl.pallas_call(
        body, name=name, grid=(N_HEADS, nq),
        in_specs=[pl.BlockSpec((tq, QK_NOPE), lambda h, j: (j, 2 * h)),
                  pl.BlockSpec((tq, V_DIM), lambda h, j: (j, 2 * h + 1)),
                  pl.BlockSpec((tq, 128), lambda h, j: (j, 0)),
                  pl.BlockSpec((s, HEAD_PAD), lambda h, j: (0, h)),
                  pl.BlockSpec((tq, V_DIM), lambda h, j: (j, z_blk + h))],
        out_specs=[tile, tile, pl.BlockSpec((None, None, 8, tq), lambda h, j: (h, j, 0, 0))],
        out_shape=[jax.ShapeDtypeStruct((s, N_HEADS * V_DIM), F32),
                   jax.ShapeDtypeStruct((s, N_HEADS * V_DIM), BF),
                   jax.ShapeDtypeStruct((N_HEADS, nq, 8, tq), F32)],
        scratch_shapes=[pltpu.VMEM((nq, 2 * V_DIM, tq), F32), pltpu.VMEM((nq, 8, tq), F32)],
        compiler_params=_params(2))(kv, kv, krr, qr, proj)


def attn_bwd(qr, kv, krr, do, lse, delta, cc, sa, sb, *, name, tq):
    s = qr.shape[0]
    nq = s // tq

    def body(kn_ref, v_ref, kr_ref, q_ref, do_ref, lse_ref, dl_ref, cc_ref, sa_ref, sb_ref,
             dkv_ref, dkr_ref, dq_ref, dq_sc, dk_sc, dv_sc):
        h = pl.program_id(0)
        j = pl.program_id(1)

        @pl.when(j == 0)
        def _():
            dq_sc[...] = jnp.zeros((s, HEAD_PAD), F32)

        dk_sc[...] = jnp.zeros((tq, HEAD_PAD), F32)
        dv_sc[...] = jnp.zeros((tq, V_DIM), F32)
        k = jnp.concatenate([kn_ref[...], kr_ref[...]], axis=1)
        v = v_ref[...]

        def step(i, n_tiles, masked):
            r0 = pl.multiple_of(i * tq, tq)
            rows = pl.ds(r0, n_tiles * tq)
            q = q_ref[rows, :]
            dov = do_ref[rows, :]
            lse_row = jnp.concatenate([lse_ref[i + n, pl.ds(0, 1), :] for n in range(n_tiles)], axis=1)
            dl_row = jnp.concatenate([dl_ref[i + n, pl.ds(0, 1), :] for n in range(n_tiles)], axis=1)
            st = lax.dot_general(k, q, NT, preferred_element_type=F32) * SCALE_LOG2E
            if masked:
                krow = lax.broadcasted_iota(jnp.int32, (tq, n_tiles * tq), 0)
                qcol = lax.broadcasted_iota(jnp.int32, (tq, n_tiles * tq), 1)
                st = jnp.where(qcol >= krow, st, NEG)
            pt = jnp.exp2(st - lse_row)
            dpt = lax.dot_general(v, dov, NT, preferred_element_type=F32)
            dst = (pt * (dpt - dl_row)).astype(BF)
            dv_sc[...] += jnp.dot(pt.astype(BF), dov, preferred_element_type=F32)
            dk_sc[...] += jnp.dot(dst, q, preferred_element_type=F32)
            dq_sc[rows, :] += lax.dot_general(dst, k, TN, preferred_element_type=F32)

        _for_groups(j, nq - j, BWD_GROUPS, step)
        dkv_ref[:, :QK_NOPE] = (dk_sc[:, :QK_NOPE] * SCALE).astype(dkv_ref.dtype)
        dkv_ref[:, QK_NOPE:] = dv_sc[...].astype(dkv_ref.dtype)
        mine = pl.ds(pl.multiple_of(j * tq, tq), tq)
        dkr = dk_sc[:, QK_NOPE:] * SCALE

        @pl.when(h == 0)
        def _():
            dkr_ref[mine, :] = dkr

        @pl.when(h > 0)
        def _():
            dkr_ref[mine, :] += dkr

        dq_ref[:, :QK_NOPE] = (dq_sc[mine, :QK_NOPE] * SCALE).astype(dq_ref.dtype)
        dq_ref[:, QK_NOPE:] = _unrope(dq_sc[mine, QK_NOPE:] * SCALE, cc_ref[...], sa_ref[...],
                                      sb_ref[...]).astype(dq_ref.dtype)

    rows = pl.BlockSpec((None, nq, 8, tq), lambda h, j: (h, 0, 0, 0))
    tab = pl.BlockSpec((tq, 128), lambda h, j: (j, 0))
    return pl.pallas_call(
        body, name=name, grid=(N_HEADS, nq),
        in_specs=[pl.BlockSpec((tq, QK_NOPE), lambda h, j: (j, 2 * h)),
                  pl.BlockSpec((tq, V_DIM), lambda h, j: (j, 2 * h + 1)), tab,
                  pl.BlockSpec((s, HEAD_PAD), lambda h, j: (0, h)),
                  pl.BlockSpec((s, V_DIM), lambda h, j: (0, h)), rows, rows, tab, tab, tab],
        out_specs=[pl.BlockSpec((tq, 256), lambda h, j: (j, h)),
                   pl.BlockSpec((s, 128), lambda h, j: (0, 0)),
                   pl.BlockSpec((tq, HEAD_PAD), lambda h, j: (j, h))],
        out_shape=[jax.ShapeDtypeStruct((s, N_HEADS * 256), BF),
                   jax.ShapeDtypeStruct((s, 128), F32),
                   jax.ShapeDtypeStruct((s, N_HEADS * HEAD_PAD), BF)],
        scratch_shapes=[pltpu.VMEM((s, HEAD_PAD), F32), pltpu.VMEM((tq, HEAD_PAD), F32),
                        pltpu.VMEM((tq, V_DIM), F32)],
        compiler_params=_params(2))(kv, kv, krr, qr, do, lse, delta, cc, sa, sb)


def adamw(w, g, m, v, *, name, t=256):
    r, c = w.shape
    t = r if r % t else t
    c1 = 1.0 - ADAM_B1 ** ADAM_STEP
    c2 = 1.0 - ADAM_B2 ** ADAM_STEP

    def body(w_ref, g_ref, m_ref, v_ref, d_ref, nm_ref, nv_ref):
        gv = g_ref[...]
        nm = ADAM_B1 * m_ref[...] + (1.0 - ADAM_B1) * gv
        nv = ADAM_B2 * v_ref[...] + (1.0 - ADAM_B2) * (gv * gv)
        nm_ref[...] = nm
        nv_ref[...] = nv
        d_ref[...] = -ADAM_LR * ((nm / c1) / (jnp.sqrt(nv / c2) + ADAM_EPS) + ADAM_WD * w_ref[...])

    blk = pl.BlockSpec((t, c), lambda i: (i, 0))
    return pl.pallas_call(
        body, name=name, grid=(r // t,), in_specs=[blk] * 4, out_specs=[blk] * 3,
        out_shape=[jax.ShapeDtypeStruct((r, c), F32)] * 3, compiler_params=_params(1))(w, g, m, v)


def sum_devices(parts, *, name):
    def body(p_ref, o_ref):
        acc = p_ref[pl.ds(0, SV_ROWS), :]
        for d in range(1, 8):
            acc = acc + p_ref[pl.ds(d * SV_ROWS, SV_ROWS), :]
        o_ref[...] = acc

    return pl.pallas_call(body, name=name, out_shape=jax.ShapeDtypeStruct((SV_ROWS, SV_COLS), F32))(parts)


def add_halves(g, rb, c_idx, *, name, rows):
    nq, r2, cc = rb.shape
    nb = r2 // rows

    def body(c_ref, g_ref, r_ref, o_ref):
        o_ref[...] = (g_ref[...] + r_ref[...]).astype(o_ref.dtype)

    grid_spec = pltpu.PrefetchScalarGridSpec(
        num_scalar_prefetch=1, grid=(nq, nb),
        in_specs=[pl.BlockSpec((None, rows, cc), lambda q, i, c: (q, c[0] * nb + i, 0)),
                  pl.BlockSpec((None, rows, cc), lambda q, i, c: (q, i, 0))],
        out_specs=pl.BlockSpec((None, rows, cc), lambda q, i, c: (q, i, 0)))
    return pl.pallas_call(body, name=name, grid_spec=grid_spec,
                          out_shape=jax.ShapeDtypeStruct((nq, r2, cc), BF),
                          compiler_params=_params(2))(c_idx, g, rb)


def sum_chips(rc, c_idx, *, name, rows):
    nq, r2, cc = rc.shape
    nb = r2 // rows

    def body(c_ref, r_ref, o_ref):
        parts = [r_ref[q].astype(F32) for q in range(4)]
        o_ref[...] = ((parts[0] + parts[1]) + parts[2]) + parts[3]

    grid_spec = pltpu.PrefetchScalarGridSpec(
        num_scalar_prefetch=1, grid=(nb,),
        in_specs=[pl.BlockSpec((nq, rows, cc), lambda i, c: (0, i, 0))],
        out_specs=pl.BlockSpec((rows, cc), lambda i, c: (c[0] * nb + i, 0)))
    return pl.pallas_call(body, name=name, grid_spec=grid_spec,
                          out_shape=jax.ShapeDtypeStruct((2 * r2, cc), F32),
                          compiler_params=_params(1))(c_idx, rc)


def _place():
    return lax.axis_index("x"), lax.axis_index("y"), lax.axis_index("c")


def all_gather8(xs, *, name, own_half):
    m = xs.shape[0] // 2 if own_half else xs.shape[0]
    n = xs.shape[1]

    def body(x_ref, out_ref, send_sems, recv_sems, local_sem):
        x, y, c = _place()
        me, sibling = (x, y, c), (x, y, 1 - c)
        chips = [(1 - x, y), (x, 1 - y), (1 - x, 1 - y)]
        src_own = x_ref.at[pl.ds(c * m, m), :] if own_half else x_ref

        def rows(px, py, pc):
            return out_ref.at[pl.ds((4 * px + 2 * py + pc) * m, m), :]

        def copy(k, block, to, src=None):
            return pltpu.make_async_remote_copy(
                src_ref=rows(*block) if src is None else src, dst_ref=rows(*block),
                send_sem=send_sems.at[k], recv_sem=recv_sems.at[k], device_id=to, device_id_type=MESH)

        mine = pltpu.make_async_copy(src_own, rows(*me), local_sem)
        mine.start()
        first = [copy(0, me, sibling, src=src_own)]
        first += [copy(1 + j, me, (*chip, c), src=src_own) for j, chip in enumerate(chips)]
        for cp in first:
            cp.start()
        passed = [copy(4 + j, (*chip, c), sibling) for j, chip in enumerate(chips)]
        for j, chip in enumerate(chips):
            copy(1 + j, (*chip, c), me).wait_recv()
            passed[j].start()
        copy(0, sibling, me).wait_recv()
        for j, chip in enumerate(chips):
            copy(4 + j, (*chip, 1 - c), me).wait_recv()
        for cp in first + passed:
            cp.wait_send()
        mine.wait()

    return pl.pallas_call(
        body, name=name, out_shape=jax.ShapeDtypeStruct((8 * m, n), xs.dtype),
        in_specs=[pl.BlockSpec(memory_space=pl.ANY)], out_specs=pl.BlockSpec(memory_space=pl.ANY),
        scratch_shapes=[pltpu.SemaphoreType.DMA((7,)), pltpu.SemaphoreType.DMA((7,)), pltpu.SemaphoreType.DMA],
    )(xs)


def _other_chips():
    x, y, c = _place()
    return [(1 - x, y), (x, 1 - y), (1 - x, 1 - y)]


def _remote(src, dst, send_sems, recv_sems, k, to):
    return pltpu.make_async_remote_copy(src_ref=src, dst_ref=dst, send_sem=send_sems.at[k], recv_sem=recv_sems.at[k],
                                        device_id=to, device_id_type=MESH)


def gather_ici(xs):
    r, cc = xs.shape
    m = r // 2

    def copies(ins, outs, ss, rs, landing):
        x, y, c = _place()
        half = pl.ds(c * m, m)
        return [_remote(ins[0].at[half, :], outs[0].at[(2 * cx + cy) if landing else (2 * x + y), half, :],
                        ss, rs, j, (cx, cy, c)) for j, (cx, cy) in enumerate(_other_chips())]

    def start(ins, outs, ss, rs, ls):
        for cp in copies(ins, outs, ss, rs, False):
            cp.start()

    def wait(ins, outs, ss, rs, ls):
        for cp in copies(ins, outs, ss, rs, True):
            cp.wait_recv()
        for cp in copies(ins, outs, ss, rs, False):
            cp.wait_send()

    return Exchange((xs,), (jax.ShapeDtypeStruct((4, r, cc), xs.dtype),), {}, 3, start, wait)


def gather_forward(buf):
    m = buf.shape[1] // 2

    def copies(outs, ss, rs, landing):
        x, y, c = _place()
        half = pl.ds(((1 - c) if landing else c) * m, m)
        return [_remote(outs[0].at[2 * cx + cy, half, :], outs[0].at[2 * cx + cy, half, :], ss, rs, j, (x, y, 1 - c))
                for j, (cx, cy) in enumerate(_other_chips())]

    def start(ins, outs, ss, rs, ls):
        for cp in copies(outs, ss, rs, False):
            cp.start()

    def wait(ins, outs, ss, rs, ls):
        for cp in copies(outs, ss, rs, True):
            cp.wait_recv()
        for cp in copies(outs, ss, rs, False):
            cp.wait_send()

    return Exchange((buf,), (jax.ShapeDtypeStruct(buf.shape, buf.dtype),), {0: 0}, 3, start, wait)


def swap_halves(g):
    nq, r, cc = g.shape
    r2 = r // 2

    def copy(ins, outs, ss, rs):
        x, y, c = _place()
        return _remote(ins[0].at[:, pl.ds((1 - c) * r2, r2), :], outs[0], ss, rs, 0, (x, y, 1 - c))

    def start(ins, outs, ss, rs, ls):
        copy(ins, outs, ss, rs).start()

    def wait(ins, outs, ss, rs, ls):
        copy(ins, outs, ss, rs).wait()

    return Exchange((g,), (jax.ShapeDtypeStruct((nq, r2, cc), g.dtype),), {}, 1, start, wait)


def exchange_chips(p):
    def own(ins, outs, ls):
        x, y, c = _place()
        return pltpu.make_async_copy(ins[0].at[2 * x + y], outs[0].at[2 * x + y], ls)

    def copies(ins, outs, ss, rs, landing):
        x, y, c = _place()
        return [_remote(ins[0].at[2 * cx + cy], outs[0].at[(2 * cx + cy) if landing else (2 * x + y)],
                        ss, rs, j, (cx, cy, c)) for j, (cx, cy) in enumerate(_other_chips())]

    def start(ins, outs, ss, rs, ls):
        own(ins, outs, ls).start()
        for cp in copies(ins, outs, ss, rs, False):
            cp.start()

    def wait(ins, outs, ss, rs, ls):
        for cp in copies(ins, outs, ss, rs, True):
            cp.wait_recv()
        for cp in copies(ins, outs, ss, rs, False):
            cp.wait_send()
        own(ins, outs, ls).wait()

    return Exchange((p,), (jax.ShapeDtypeStruct(p.shape, p.dtype),), {}, 3, start, wait)


def join_halves(tot):
    r2 = tot.shape[0] // 2

    def copy(outs, ss, rs, landing):
        x, y, c = _place()
        half = outs[0].at[pl.ds(((1 - c) if landing else c) * r2, r2), :]
        return _remote(half, half, ss, rs, 0, (x, y, 1 - c))

    def start(ins, outs, ss, rs, ls):
        copy(outs, ss, rs, False).start()

    def wait(ins, outs, ss, rs, ls):
        copy(outs, ss, rs, True).wait_recv()
        copy(outs, ss, rs, False).wait_send()

    return Exchange((tot,), (jax.ShapeDtypeStruct(tot.shape, tot.dtype),), {0: 0}, 1, start, wait)


def _pack_shard(blocks, small_vec=None):
    parts = [w.reshape(-1, PACK_C).astype(BF) for w in blocks]
    if small_vec is not None:
        srow = lax.bitcast_convert_type(small_vec, BF).reshape(1, PACK_C)
        parts.append(jnp.pad(srow, ((0, PACK_PAD - 1), (0, 0))))
    return jnp.concatenate(parts, axis=0)


def _split_rows(a, rows, axis):
    out, off = [], 0
    for n in rows:
        out.append(lax.slice_in_dim(a, off, off + n, axis=axis))
        off += n
    return out


def _unpack_pool(gw):
    p_in, p_grp, p_out = _split_rows(gw, POOL_ROWS, 1)
    return dict(
        pool_w_in=p_in.reshape(4, D_MODEL, 1024).transpose(1, 0, 2).reshape(D_MODEL, 2 * POOL_WIDTH),
        pool_w_group=p_grp.reshape(4, 4, 128, POOL_GROUP).transpose(1, 0, 2, 3).reshape(4, POOL_GROUP, POOL_GROUP),
        pool_w_out=p_out.reshape(POOL_WIDTH, D_MODEL))


def _unpack_mla(gw):
    m_in, m_qb, m_kvb, m_out, small = _split_rows(gw, MLA_ROWS + (PACK_PAD,), 1)
    w = {}
    win = m_in.reshape(4, D_MODEL, 688).transpose(1, 0, 2).reshape(D_MODEL, 2752)
    w["mla_w_in"] = jnp.concatenate(
        [win[:, 384:640], win[:, 640:704], jnp.zeros((D_MODEL, 64), BF), win[:, 0:384], win[:, 704:]], axis=1)
    wq = m_qb.reshape(4, Q_LORA, 768).transpose(1, 0, 2).reshape(Q_LORA, N_HEADS, QK_NOPE + QK_ROPE)
    w["mla_w_q_b"] = jnp.pad(wq, ((0, 0), (0, 0), (0, HEAD_PAD - QK_NOPE - QK_ROPE))).reshape(Q_LORA, N_HEADS * HEAD_PAD)
    w["mla_w_kv_b"] = m_kvb.reshape(4, KV_LORA, 1024).transpose(1, 0, 2).reshape(KV_LORA, 4096)
    w["mla_w_out"] = m_out.reshape(MLA_WIDTH, D_MODEL)
    small = lax.bitcast_convert_type(small[:, 0, :].reshape(4, 512, 2), F32)
    w["mla_norm"] = small[:, :256].reshape(1, D_MODEL)
    w["mla_q_norm"] = small[:, 256:352].reshape(1, Q_LORA)
    w["mla_kv_norm"] = small[:, 352:416].reshape(1, KV_LORA)
    return w


def _pack_pool_grads(g):
    return jnp.concatenate([
        g["pool_w_in"],
        g["pool_w_group"].reshape(4, 4, 128, POOL_GROUP).transpose(1, 0, 2, 3).reshape(4, 256, PACK_C),
        g["pool_w_out"].reshape(4, 512, PACK_C)], axis=1)


def _pack_mla_grads(g):
    return jnp.concatenate([
        g["mla_w_in"].reshape(D_MODEL, 4, 688).transpose(1, 0, 2).reshape(4, 688, PACK_C),
        g["mla_w_q_b"].reshape(Q_LORA, 4, 768).transpose(1, 0, 2).reshape(4, 288, PACK_C),
        g["mla_w_kv_b"],
        g["mla_w_out"].reshape(4, 512, PACK_C),
        jnp.zeros((4, PACK_PAD, PACK_C), F32)], axis=1)


def kernel(x, positions, pool_norm, pool_w_in, pool_w_group, pool_scale, pool_w_out, mla_norm, mla_w_in, mla_q_norm, mla_w_q_b, mla_kv_norm, mla_w_kv_b, mla_w_out, final_norm, loss_target, m_pool_norm, m_pool_w_in, m_pool_w_group, m_pool_scale, m_pool_w_out, m_mla_norm, m_mla_w_in, m_mla_q_norm, m_mla_w_q_b, m_mla_kv_norm, m_mla_w_kv_b, m_mla_w_out, m_final_norm, v_pool_norm, v_pool_w_in, v_pool_w_group, v_pool_scale, v_pool_w_out, v_mla_norm, v_mla_w_in, v_mla_q_norm, v_mla_w_q_b, v_mla_kv_norm, v_mla_w_kv_b, v_mla_w_out, v_final_norm):
    s = x.shape[1]
    tq = min(512, s)
    x0 = x.reshape(s, D_MODEL)
    tgt = loss_target.reshape(s, D_MODEL)
    cx, cy, cc_idx = _place()
    chip = 2 * cx + cy

    big_names = ("pool_w_in", "pool_w_group", "pool_w_out", "mla_w_in", "mla_w_q_b", "mla_w_kv_b", "mla_w_out")
    big_w = dict(zip(big_names, (pool_w_in, pool_w_group, pool_w_out, mla_w_in, mla_w_q_b, mla_w_kv_b, mla_w_out)))
    big_m = dict(zip(big_names, (m_pool_w_in, m_pool_w_group, m_pool_w_out, m_mla_w_in, m_mla_w_q_b, m_mla_w_kv_b, m_mla_w_out)))
    big_v = dict(zip(big_names, (v_pool_w_in, v_pool_w_group, v_pool_w_out, v_mla_w_in, v_mla_w_q_b, v_mla_w_kv_b, v_mla_w_out)))

    small_vec = jnp.concatenate([mla_norm.reshape(-1), mla_q_norm.reshape(-1), mla_kv_norm.reshape(-1),
                                 jnp.zeros((96,), F32)])
    pool_packed = _pack_shard([big_w[n] for n in big_names[:3]])
    mla_packed = _pack_shard([big_w[n] for n in big_names[3:]], small_vec)
    w = _unpack_pool(all_gather8(pool_packed, name="gather_pool_weights", own_half=True).reshape(4, POOL_R, PACK_C))
    g_pool = pool_norm.reshape(1, D_MODEL)
    g_final = final_norm.reshape(1, D_MODEL)
    sc_pool = pool_scale.reshape(1, POOL_WIDTH)

    inv_freq = 1.0 / (ROPE_THETA ** (jnp.arange(0, QK_ROPE, 2, dtype=F32) / QK_ROPE))
    ang = positions.reshape(s).astype(F32)[:, None] * inv_freq
    cos, sin = jnp.cos(ang), jnp.sin(ang)
    z32, z64, z96 = (jnp.zeros((s, n), F32) for n in (32, 64, 96))
    t_cc = jnp.concatenate([cos, cos, z64], axis=1)
    t_sa = jnp.concatenate([-sin, z96], axis=1)
    t_sb = jnp.concatenate([z32, sin, z64], axis=1)

    h0 = norm_fwd(x0, g_pool, name="pool_norm_fwd")
    uz, mla_land = mm_nn(h0, w["pool_w_in"], name="pool_in_proj", out_dtype=F32, host=gather_ici(mla_packed))
    pd = pool_prep(uz, name="pool_window")
    mm, y1 = pool_mix_gate(pd, w["pool_w_group"], uz, sc_pool, name="pool_group_mix")
    x1, mla_land = mm_nn(y1, w["pool_w_out"], name="pool_out_proj", out_dtype=F32, add=x0,
                         host=gather_forward(mla_land))
    w.update(_unpack_mla(lax.dynamic_update_slice_in_dim(mla_land, mla_packed[None], chip, axis=0)))

    h1 = norm_fwd(x1, w["mla_norm"], name="mla_norm_fwd")
    proj = mm_nn(h1, w["mla_w_in"], name="mla_in_proj", out_dtype=F32, tn=P_WIDTH // 2)
    qn, kvn, krr = latent_fwd(proj, w["mla_q_norm"], w["mla_kv_norm"], t_cc, t_sa, t_sb, name="mla_latent_fwd")
    qr = q_proj_rope(qn, w["mla_w_q_b"], t_cc, t_sa, t_sb, name="mla_q_proj")
    kv = mm_nn(kvn, w["mla_w_kv_b"], name="mla_kv_proj", out_dtype=BF, tn=2048)
    o, y2, lse = attn_fwd(qr, kv, krr, proj, name="mla_attn_fwd", tq=tq)
    dx2, d_final, loss_part = mm_nn_loss(y2, w["mla_w_out"], x1, g_final, tgt, name="mla_out_proj_loss")

    grads = {}
    grads["mla_w_out"] = mm_tn(y2, dx2, name="mla_out_proj_dw")
    do, dz2, delta = mla_out_dx_gate(dx2, w["mla_w_out"], o, proj, name="mla_out_proj_dx", tq=tq)
    dkv, dkr, dq_pre = attn_bwd(qr, kv, krr, do, lse, delta, t_cc, t_sa, t_sb, name="mla_attn_bwd", tq=tq)
    dqn = mm_nt(dq_pre, w["mla_w_q_b"], name="mla_q_proj_dx", out_dtype=F32, tn=Q_LORA, tk=4096)
    g_qb = mm_tn(qn, dq_pre, name="mla_q_proj_dw", tm=Q_LORA, tn=2048)
    dkvn = mm_nt(dkv, w["mla_w_kv_b"], name="mla_kv_proj_dx", out_dtype=F32, tn=KV_LORA, tk=4096)
    grads["mla_w_kv_b"] = mm_tn(kvn, dkv, name="mla_kv_proj_dw", tm=KV_LORA, by_column_block=True)
    dsmall, d_qnorm, d_kvnorm = latent_bwd(proj, w["mla_q_norm"], w["mla_kv_norm"], dqn, dkvn, dkr,
                                           t_cc, t_sa, t_sb, name="mla_latent_bwd")
    dx1, d_mnorm = mm_nt_norm_bwd(dz2, w["mla_w_in"][:, P_Z:], (dsmall, w["mla_w_in"]), x1, w["mla_norm"], dx2,
                                  name="mla_in_proj_dx")
    g_in_a = mm_tn(h1, dsmall, name="mla_in_proj_dw_a", tn=P_SMALL)
    g_in_b = mm_tn(h1, dz2, name="mla_in_proj_dw_b")

    g_in = jnp.concatenate([g_in_a, g_in_b], axis=1)
    grads["mla_w_in"] = jnp.concatenate([g_in[:, P_Q:P_Z], g_in[:, P_KV:P_KV + KV_LORA],
                                         g_in[:, P_KR:P_KR + QK_ROPE], g_in[:, P_Z:]], axis=1)
    grads["mla_w_q_b"] = g_qb.reshape(Q_LORA, N_HEADS, HEAD_PAD)[:, :, :QK_NOPE + QK_ROPE].reshape(Q_LORA, -1)
    core_idx = cc_idx.reshape(1).astype(jnp.int32)
    gp_mla = _pack_mla_grads(grads)

    grads["pool_w_out"], sib = mm_tn(y1, dx1, name="pool_out_proj_dw", host=swap_halves(gp_mla))
    pre = add_halves(gp_mla, sib, core_idx, name="mla_grad_add_halves", rows=MLA_R // 2)
    dmm, dz1, d_scale, got = pool_out_dx_gate(dx1, w["pool_w_out"], mm, uz, sc_pool, name="pool_out_proj_dx",
                                              host=exchange_chips(pre))
    tot = sum_chips(got, core_idx, name="mla_grad_sum_chips", rows=MLA_R // 2)
    dpd, red_mla = gmm_nt(dmm, w["pool_w_group"], name="pool_group_mix_dx", host=join_halves(tot))
    grads["pool_w_group"] = gmm_tn(pd, dmm, 4, name="pool_group_mix_dw")
    du = pool_prep_bwd(dpd, name="pool_window_bwd")
    g_pin_u = mm_tn(h0, du, name="pool_in_proj_dw_u", by_column_block=True)
    g_pin_z = mm_tn(h0, dz1, name="pool_in_proj_dw_z", by_column_block=True)
    grads["pool_w_in"] = jnp.concatenate([g_pin_u, g_pin_z], axis=0)

    gp_pool = _pack_pool_grads(grads)
    dh0, sib = mm_nt(du, w["pool_w_in"], name="pool_in_proj_dx_u", out_dtype=F32, host=swap_halves(gp_pool))
    pre = add_halves(gp_pool, sib, core_idx, name="pool_grad_add_halves", rows=POOL_R // 2)
    grad_x, d_pnorm, got = mm_nt_norm_bwd(dz1, w["pool_w_in"], dh0, x0, g_pool, dx1, name="pool_in_proj_dx_z",
                                          b_col=POOL_WIDTH, host=exchange_chips(pre))
    tot = sum_chips(got, core_idx, name="pool_grad_sum_chips", rows=POOL_R // 2)
    red_pool = run_exchange(join_halves(tot), name="pool_grad_join_halves")[0]
    red_parts = _split_rows(red_pool, POOL_ROWS, 0) + _split_rows(red_mla, MLA_ROWS, 0)

    sv = jnp.concatenate([d_pnorm.reshape(-1), d_scale.reshape(-1), d_final.reshape(-1), d_mnorm.reshape(-1),
                          d_qnorm.reshape(-1), d_kvnorm.reshape(-1), loss_part[0, :1],
                          jnp.zeros((SV_ROWS * SV_COLS - SV_OFF["loss"] - 1,), F32)]).reshape(SV_ROWS, SV_COLS)
    sv_all = all_gather8(sv, name="gather_small_grads", own_half=False)
    sv_sum = sum_devices(sv_all, name="sum_small_grads").reshape(-1)
    loss = sv_sum[SV_OFF["loss"]]

    def sv_take(key, n):
        return lax.slice_in_dim(sv_sum, SV_OFF[key], SV_OFF[key] + n)

    out_g, out_d, out_m, out_v = {}, {}, {}, {}
    for name, part in zip(big_names, red_parts):
        shp = big_w[name].shape
        g2 = part.reshape(shp)
        two_d = (-1, shp[-1])
        d_, m_, v_ = adamw(big_w[name].reshape(two_d), g2.reshape(two_d), big_m[name].reshape(two_d),
                           big_v[name].reshape(two_d), name="adamw_" + name)
        out_g[name], out_d[name], out_m[name], out_v[name] = g2, d_.reshape(shp), m_.reshape(shp), v_.reshape(shp)

    small = [
        ("pool_norm", pool_norm, m_pool_norm, v_pool_norm, sv_take("pool_norm", 1024)),
        ("pool_scale", pool_scale, m_pool_scale, v_pool_scale, sv_take("pool_scale", 2048)),
        ("final_norm", final_norm, m_final_norm, v_final_norm, sv_take("final_norm", 1024)),
        ("mla_norm", mla_norm, m_mla_norm, v_mla_norm,
         lax.dynamic_slice_in_dim(sv_take("mla_norm", 1024), chip * 256, 256)),
        ("mla_q_norm", mla_q_norm, m_mla_q_norm, v_mla_q_norm,
         lax.dynamic_slice_in_dim(sv_take("q_norm", 384), chip * 96, 96)),
        ("mla_kv_norm", mla_kv_norm, m_mla_kv_norm, v_mla_kv_norm,
         lax.dynamic_slice_in_dim(sv_take("kv_norm", 256), chip * 64, 64)),
    ]
    sw = jnp.concatenate([t[1].reshape(-1) for t in small] + [jnp.zeros((96,), F32)]).reshape(1, -1)
    sm = jnp.concatenate([t[2].reshape(-1) for t in small] + [jnp.zeros((96,), F32)]).reshape(1, -1)
    s_v = jnp.concatenate([t[3].reshape(-1) for t in small] + [jnp.ones((96,), F32)]).reshape(1, -1)
    sg = jnp.concatenate([t[4].reshape(-1) for t in small] + [jnp.zeros((96,), F32)]).reshape(1, -1)
    sd_, sm_, sv_ = adamw(sw, sg, sm, s_v, name="adamw_vectors")
    off = 0
    for name, wt, _, _, gvec in small:
        n = gvec.shape[0]
        shp = wt.shape
        out_g[name] = gvec.reshape(shp)
        out_d[name] = sd_[0, off:off + n].reshape(shp)
        out_m[name] = sm_[0, off:off + n].reshape(shp)
        out_v[name] = sv_[0, off:off + n].reshape(shp)
        off += n

    order = ("pool_norm", "pool_w_in", "pool_w_group", "pool_scale", "pool_w_out", "mla_norm", "mla_w_in",
             "mla_q_norm", "mla_w_q_b", "mla_kv_norm", "mla_w_kv_b", "mla_w_out", "final_norm")
    return (loss, grad_x.reshape(x.shape), *[out_g[n] for n in order], *[out_d[n] for n in order],
            *[out_m[n] for n in order], *[out_v[n] for n in order])
```

```python
import functools
from typing import Callable, NamedTuple

import jax
import jax.numpy as jnp
from jax import lax
from jax.experimental import pallas as pl
from jax.experimental.pallas import tpu as pltpu

F32 = jnp.float32
BF = jnp.bfloat16
MESH = pl.DeviceIdType.MESH

D_MODEL = 1024
POOL_WIDTH = 2048
POOL_WINDOWS = (2, 4, 8, 16)
POOL_GROUP = 512
HALO = 16
N_HEADS = 16
QK_NOPE = 128
QK_ROPE = 64
V_DIM = 128
HEAD_PAD = 256
Q_LORA = 384
KV_LORA = 256
MLA_WIDTH = 2048
ROPE_THETA = 10000.0
EPS = 1e-6
SCALE = (QK_NOPE + QK_ROPE) ** -0.5
SCALE_LOG2E = SCALE * 1.4426950408889634
NEG = -1e30

P_KV, P_KR, P_Q, P_Z = 0, 256, 384, 768
P_SMALL = 768
P_WIDTH = 2816

ADAM_LR = 0.001
ADAM_B1 = 0.9
ADAM_B2 = 0.999
ADAM_EPS = 1e-08
ADAM_WD = 0.01
ADAM_STEP = 10

NN = (((1,), (0,)), ((), ()))
NT = (((1,), (1,)), ((), ()))
TN = (((0,), (0,)), ((), ()))

POOL_ROWS = (1024, 256, 512)
MLA_ROWS = (688, 288, 256, 512)
PACK_PAD = 16
POOL_R = sum(POOL_ROWS)
MLA_R = sum(MLA_ROWS) + PACK_PAD
PACK_C = 1024
SV_OFF = dict(pool_norm=0, pool_scale=1024, final_norm=3072, mla_norm=4096, q_norm=5120, kv_norm=5504, loss=5760)
SV_ROWS, SV_COLS = 8, 768

VMEM_LIMIT = 56 * 1024 * 1024


def _params(n_axes, vmem=None):
    return pltpu.CompilerParams(dimension_semantics=("arbitrary",) * n_axes,
                                vmem_limit_bytes=VMEM_LIMIT if vmem is None else vmem)


def _sigmoid(z):
    return 1.0 / (1.0 + jnp.exp(-z))


class Exchange(NamedTuple):
    operands: tuple
    out_shapes: tuple
    aliases: dict
    n_sems: int
    start: Callable
    wait: Callable


HBM_SPEC = pl.BlockSpec(memory_space=pl.ANY)


def _exchange_scratch(ex):
    return [pltpu.SemaphoreType.DMA((ex.n_sems,)), pltpu.SemaphoreType.DMA((ex.n_sems,)), pltpu.SemaphoreType.DMA]


def run_exchange(ex, *, name):
    n_in, n_out = len(ex.operands), len(ex.out_shapes)

    def body(*refs):
        args = (refs[:n_in], refs[n_in:n_in + n_out]) + tuple(refs[n_in + n_out:])
        ex.start(*args)
        ex.wait(*args)

    return pl.pallas_call(
        body, name=name, out_shape=list(ex.out_shapes), in_specs=[HBM_SPEC] * n_in,
        out_specs=[HBM_SPEC] * n_out, scratch_shapes=_exchange_scratch(ex),
        input_output_aliases=dict(ex.aliases))(*ex.operands)


def _call(core, *, name, grid, in_specs, out_specs, out_shape, args, scratch=(), host=None):
    in_specs, out_specs, out_shape = list(in_specs), list(out_specs), list(out_shape)
    params = _params(len(grid))
    if host is None:
        return pl.pallas_call(core, name=name, grid=grid, in_specs=in_specs, out_specs=out_specs,
                              out_shape=out_shape, scratch_shapes=list(scratch), compiler_params=params)(*args)
    n_in, n_out = len(in_specs), len(out_specs)
    n_hin, n_hout = len(host.operands), len(host.out_shapes)

    def body(*refs):
        ins, refs = refs[:n_in], refs[n_in:]
        h_in, refs = refs[:n_hin], refs[n_hin:]
        outs, refs = refs[:n_out], refs[n_out:]
        h_out, refs = refs[:n_hout], refs[n_hout:]
        own_scratch, sems = refs[:-3], refs[-3:]
        ids = [pl.program_id(ax) for ax in range(len(grid))]
        first = functools.reduce(jnp.logical_and, [i == 0 for i in ids])
        last = functools.reduce(jnp.logical_and, [i == n - 1 for i, n in zip(ids, grid)])

        @pl.when(first)
        def _():
            host.start(h_in, h_out, *sems)

        core(*ins, *outs, *own_scratch)

        @pl.when(last)
        def _():
            host.wait(h_in, h_out, *sems)

    return pl.pallas_call(
        body, name=name, grid=grid, in_specs=in_specs + [HBM_SPEC] * n_hin,
        out_specs=out_specs + [HBM_SPEC] * n_hout, out_shape=out_shape + list(host.out_shapes),
        scratch_shapes=list(scratch) + _exchange_scratch(host),
        input_output_aliases={n_in + i: n_out + o for i, o in host.aliases.items()},
        compiler_params=params)(*args, *host.operands)


def _mm(a, b, *, dims, grid, a_spec, b_spec, o_spec, out_shape, out_dtype, acc_shape, name,
        add=None, add_spec=None, host=None):
    nk = grid[-1]
    kax = len(grid) - 1

    def body(*refs):
        if add is None:
            a_ref, b_ref, o_ref = refs[:3]
            add_ref = None
            rest = refs[3:]
        else:
            a_ref, b_ref, add_ref, o_ref = refs[:4]
            rest = refs[4:]
        part = lax.dot_general(a_ref[...].astype(BF), b_ref[...].astype(BF), dims,
                               preferred_element_type=F32)

        def finish(r):
            if add_ref is not None:
                r = r + add_ref[...]
            o_ref[...] = r.astype(o_ref.dtype)

        if nk == 1:
            finish(part)
        else:
            acc = rest[0]
            k = pl.program_id(kax)

            @pl.when(k == 0)
            def _():
                acc[...] = part

            @pl.when(k > 0)
            def _():
                acc[...] += part

            @pl.when(k == nk - 1)
            def _():
                finish(acc[...])

    in_specs = [a_spec, b_spec]
    args = [a, b]
    if add is not None:
        in_specs.append(add_spec)
        args.append(add)
    out = _call(body, name=name, grid=grid, in_specs=in_specs, out_specs=[o_spec],
                out_shape=[jax.ShapeDtypeStruct(out_shape, out_dtype)], args=args,
                scratch=[] if nk == 1 else [pltpu.VMEM(acc_shape, F32)], host=host)
    return out[0] if host is None else out


def _pick(n, t):
    t = min(n, t)
    assert n % t == 0, (n, t)
    return t


def mm_nn(a, b, *, name, out_dtype, add=None, tm=1024, tn=1024, tk=2048, host=None):
    m = a.shape[0]
    kk, n = b.shape
    tm, tn, tk = _pick(m, tm), _pick(n, tn), _pick(kk, tk)
    return _mm(a, b, dims=NN, grid=(m // tm, n // tn, kk // tk),
               a_spec=pl.BlockSpec((tm, tk), lambda i, j, k: (i, k)),
               b_spec=pl.BlockSpec((tk, tn), lambda i, j, k: (k, j)),
               o_spec=pl.BlockSpec((tm, tn), lambda i, j, k: (i, j)),
               add=add, add_spec=pl.BlockSpec((tm, tn), lambda i, j, k: (i, j)),
               out_shape=(m, n), out_dtype=out_dtype, acc_shape=(tm, tn), name=name, host=host)


def mm_nt(a, b, *, name, out_dtype, b_col=0, add=None, tm=1024, tn=1024, tk=2048, host=None):
    m, kk = a.shape
    n = b.shape[0]
    tm, tn, tk = _pick(m, tm), _pick(n, tn), _pick(kk, tk)
    assert b_col % tk == 0
    ko = b_col // tk
    return _mm(a, b, dims=NT, grid=(m // tm, n // tn, kk // tk),
               a_spec=pl.BlockSpec((tm, tk), lambda i, j, k: (i, k)),
               b_spec=pl.BlockSpec((tn, tk), lambda i, j, k: (j, ko + k)),
               o_spec=pl.BlockSpec((tm, tn), lambda i, j, k: (i, j)),
               add=add, add_spec=pl.BlockSpec((tm, tn), lambda i, j, k: (i, j)),
               out_shape=(m, n), out_dtype=out_dtype, acc_shape=(tm, tn), name=name, host=host)


def mm_tn(a, b, *, name, tm=1024, tn=1024, tk=2048, host=None, by_column_block=False):
    s, m = a.shape
    n = b.shape[1]
    tm, tn, tk = _pick(m, tm), _pick(n, tn), _pick(s, tk)
    if by_column_block:
        out_shape, o_spec = (n // tn, m, tn), pl.BlockSpec((None, tm, tn), lambda i, j, k: (j, i, 0))
    else:
        out_shape, o_spec = (m, n), pl.BlockSpec((tm, tn), lambda i, j, k: (i, j))
    return _mm(a, b, dims=TN, grid=(m // tm, n // tn, s // tk),
               a_spec=pl.BlockSpec((tk, tm), lambda i, j, k: (k, i)),
               b_spec=pl.BlockSpec((tk, tn), lambda i, j, k: (k, j)),
               o_spec=o_spec, out_shape=out_shape, out_dtype=F32, acc_shape=(tm, tn), name=name, host=host)


def gmm_nt(a, w, *, name, tm=1024, host=None):
    s = a.shape[0]
    g, kk, n = w.shape
    tm = _pick(s, tm)
    return _mm(a, w, dims=NT, grid=(s // tm, g, 1),
               a_spec=pl.BlockSpec((tm, n), lambda i, gi, k: (i, gi)),
               b_spec=pl.BlockSpec((None, kk, n), lambda i, gi, k: (gi, 0, 0)),
               o_spec=pl.BlockSpec((tm, kk), lambda i, gi, k: (i, gi)),
               out_shape=(s, g * kk), out_dtype=F32, acc_shape=(tm, kk), name=name, host=host)


def gmm_tn(a, b, g, *, name, tk=2048):
    s = a.shape[0]
    kk, n = a.shape[1] // g, b.shape[1] // g
    tk = _pick(s, tk)
    return _mm(a, b, dims=TN, grid=(g, s // tk),
               a_spec=pl.BlockSpec((tk, kk), lambda gi, k: (k, gi)),
               b_spec=pl.BlockSpec((tk, n), lambda gi, k: (k, gi)),
               o_spec=pl.BlockSpec((None, kk, n), lambda gi, k: (gi, 0, 0)),
               out_shape=(g, kk, n), out_dtype=F32, acc_shape=(kk, n), name=name)


def _rms(xv, gv):
    inv = lax.rsqrt(jnp.mean(xv * xv, axis=-1, keepdims=True) + EPS)
    return (xv * inv) * gv


def _rms_bwd(xv, gv, dh):
    inv = lax.rsqrt(jnp.mean(xv * xv, axis=-1, keepdims=True) + EPS)
    xhat = xv * inv
    dxhat = dh * gv
    dx = inv * (dxhat - xhat * jnp.mean(dxhat * xhat, axis=-1, keepdims=True))
    return dx, jnp.sum(dh * xhat, axis=0, keepdims=True)


def norm_fwd(x, g, *, name, t=512):
    s, width = x.shape
    t = _pick(s, t)

    def body(x_ref, g_ref, o_ref):
        o_ref[...] = _rms(x_ref[...], g_ref[...]).astype(o_ref.dtype)

    row = pl.BlockSpec((t, width), lambda i: (i, 0))
    return pl.pallas_call(
        body, name=name, grid=(s // t,), in_specs=[row, pl.BlockSpec((1, width), lambda i: (0, 0))],
        out_specs=row, out_shape=jax.ShapeDtypeStruct((s, width), BF), compiler_params=_params(1))(x, g)


def _accumulate(ref, part):
    @pl.when(pl.program_id(0) == 0)
    def _():
        ref[...] = part

    @pl.when(pl.program_id(0) > 0)
    def _():
        ref[...] += part


def mm_nt_norm_bwd(a, b, other, x, g, res, *, name, b_col=0, tm=512, host=None):
    s, kk = a.shape
    d = b.shape[0]
    tm = _pick(s, tm)
    assert b_col % kk == 0
    pair = isinstance(other, tuple)

    def body(a_ref, b_ref, *refs):
        dh = lax.dot_general(a_ref[...].astype(BF), b_ref[...].astype(BF), NT, preferred_element_type=F32)
        if pair:
            a2_ref, b2_ref, x_ref, g_ref, res_ref, dx_ref, dg_ref = refs
            dh = dh + lax.dot_general(a2_ref[...].astype(BF), b2_ref[...].astype(BF), NT, preferred_element_type=F32)
        else:
            add_ref, x_ref, g_ref, res_ref, dx_ref, dg_ref = refs
            dh = dh + add_ref[...]
        dx, dg = _rms_bwd(x_ref[...], g_ref[...], dh)
        _accumulate(dg_ref, dg)
        dx_ref[...] = dx + res_ref[...]

    row = pl.BlockSpec((tm, d), lambda i: (i, 0))
    vec = pl.BlockSpec((1, d), lambda i: (0, 0))
    if pair:
        k2 = other[0].shape[1]
        other_specs = [pl.BlockSpec((tm, k2), lambda i: (i, 0)), pl.BlockSpec((d, k2), lambda i: (0, 0))]
        other_args = list(other)
    else:
        other_specs, other_args = [row], [other]
    return _call(
        body, name=name, grid=(s // tm,),
        in_specs=[pl.BlockSpec((tm, kk), lambda i: (i, 0)), pl.BlockSpec((d, kk), lambda i: (0, b_col // kk)),
                  *other_specs, row, vec, row],
        out_specs=[row, vec],
        out_shape=[jax.ShapeDtypeStruct((s, d), F32), jax.ShapeDtypeStruct((1, d), F32)],
        args=[a, b, *other_args, x, g, res], host=host)


def mm_nn_loss(a, b, add, gf, tgt, *, name, tm=512):
    s, kk = a.shape
    d = b.shape[1]
    tm = _pick(s, tm)

    def body(a_ref, b_ref, add_ref, g_ref, t_ref, dx_ref, dg_ref, loss_ref):
        xv = jnp.dot(a_ref[...].astype(BF), b_ref[...].astype(BF), preferred_element_type=F32) + add_ref[...]
        inv = lax.rsqrt(jnp.mean(xv * xv, axis=-1, keepdims=True) + EPS)
        xhat = xv * inv
        gv = g_ref[...]
        diff = xhat * gv - t_ref[...]
        row_err = jnp.mean(diff * diff, axis=-1, keepdims=True)
        _accumulate(loss_ref, jnp.broadcast_to(0.5 * jnp.sum(row_err, axis=0, keepdims=True), (1, 128)))
        dout = diff * (1.0 / d)
        _accumulate(dg_ref, jnp.sum(dout * xhat, axis=0, keepdims=True))
        dxhat = dout * gv
        dx_ref[...] = inv * (dxhat - xhat * jnp.mean(dxhat * xhat, axis=-1, keepdims=True))

    row = pl.BlockSpec((tm, d), lambda i: (i, 0))
    vec = pl.BlockSpec((1, d), lambda i: (0, 0))
    return _call(
        body, name=name, grid=(s // tm,),
        in_specs=[pl.BlockSpec((tm, kk), lambda i: (i, 0)), pl.BlockSpec((kk, d), lambda i: (0, 0)), row, vec, row],
        out_specs=[row, vec, pl.BlockSpec((1, 128), lambda i: (0, 0))],
        out_shape=[jax.ShapeDtypeStruct((s, d), F32), jax.ShapeDtypeStruct((1, d), F32),
                   jax.ShapeDtypeStruct((1, 128), F32)],
        args=[a, b, add, gf, tgt])


ROW_CHUNK = 56


def pool_prep(uz, *, name, t=256):
    s = uz.shape[0]
    t = _pick(s, t)
    hb = t // HALO

    lead = 2 * HALO
    live = t + lead - 8
    assert live % ROW_CHUNK == 0

    def body(u_ref, halo_ref, o_ref, buf_a, buf_b):
        i = pl.program_id(0)
        buf_a[pl.ds(lead, t), :] = u_ref[...]
        buf_a[pl.ds(0, HALO), :] = jnp.zeros((HALO, POOL_WIDTH), F32)
        buf_b[pl.ds(0, 8), :] = jnp.zeros((8, POOL_WIDTH), F32)

        @pl.when(i == 0)
        def _():
            buf_a[pl.ds(HALO, HALO), :] = jnp.zeros((HALO, POOL_WIDTH), F32)

        @pl.when(i > 0)
        def _():
            buf_a[pl.ds(HALO, HALO), :] = halo_ref[...]

        pos = i * t + lax.broadcasted_iota(jnp.int32, (t, POOL_GROUP), 0)
        for g, w in enumerate(POOL_WINDOWS):
            cols = pl.ds(g * POOL_GROUP, POOL_GROUP)
            src, dst, shift = buf_a, buf_b, 1
            while shift < w:
                for r0 in range(8, 8 + live, ROW_CHUNK):
                    dst[pl.ds(r0, ROW_CHUNK), cols] = (src[pl.ds(r0, ROW_CHUNK), cols]
                                                       + src[pl.ds(r0 - shift, ROW_CHUNK), cols])
                src, dst, shift = dst, src, 2 * shift
            cnt = jnp.minimum(pos + 1, w).astype(F32)
            o_ref[:, cols] = (src[pl.ds(lead, t), cols] / cnt - u_ref[:, cols]).astype(o_ref.dtype)

    return pl.pallas_call(
        body, name=name, grid=(s // t,),
        in_specs=[pl.BlockSpec((t, POOL_WIDTH), lambda i: (i, 0)),
                  pl.BlockSpec((HALO, POOL_WIDTH), lambda i: (jnp.maximum(i * hb - 1, 0), 0))],
        out_specs=pl.BlockSpec((t, POOL_WIDTH), lambda i: (i, 0)),
        out_shape=jax.ShapeDtypeStruct((s, POOL_WIDTH), BF),
        scratch_shapes=[pltpu.VMEM((t + lead, POOL_WIDTH), F32), pltpu.VMEM((t + lead, POOL_WIDTH), F32)],
        compiler_params=_params(1))(uz, uz)


def pool_prep_bwd(dpd, *, name, t=256):
    s = dpd.shape[0]
    t = _pick(s, t)
    hb = t // HALO
    n = s // t

    tail = 2 * HALO
    live = t + tail - 8
    assert live % ROW_CHUNK == 0

    def body(d_ref, halo_ref, o_ref, buf_a, buf_b):
        i = pl.program_id(0)
        buf_a[pl.ds(t + HALO, HALO), :] = jnp.zeros((HALO, POOL_WIDTH), F32)
        buf_b[pl.ds(live, 8), :] = jnp.zeros((8, POOL_WIDTH), F32)
        pos = i * t + lax.broadcasted_iota(jnp.int32, (t, POOL_GROUP), 0)
        for g, w in enumerate(POOL_WINDOWS):
            cols = pl.ds(g * POOL_GROUP, POOL_GROUP)
            cnt = jnp.minimum(pos + 1, w).astype(F32)
            buf_a[pl.ds(0, t), cols] = d_ref[:, cols] / cnt

            @pl.when(i < n - 1)
            def _():
                buf_a[pl.ds(t, HALO), cols] = halo_ref[:, cols] / float(w)

            @pl.when(i == n - 1)
            def _():
                buf_a[pl.ds(t, HALO), cols] = jnp.zeros((HALO, POOL_GROUP), F32)

        for g, w in enumerate(POOL_WINDOWS):
            cols = pl.ds(g * POOL_GROUP, POOL_GROUP)
            src, dst, shift = buf_a, buf_b, 1
            while shift < w:
                for r0 in range(0, live, ROW_CHUNK):
                    dst[pl.ds(r0, ROW_CHUNK), cols] = (src[pl.ds(r0, ROW_CHUNK), cols]
                                                       + src[pl.ds(r0 + shift, ROW_CHUNK), cols])
                src, dst, shift = dst, src, 2 * shift
            o_ref[:, cols] = (src[pl.ds(0, t), cols] - d_ref[:, cols]).astype(o_ref.dtype)

    return pl.pallas_call(
        body, name=name, grid=(n,),
        in_specs=[pl.BlockSpec((t, POOL_WIDTH), lambda i: (i, 0)),
                  pl.BlockSpec((HALO, POOL_WIDTH), lambda i: (jnp.minimum((i + 1) * hb, n * hb - 1), 0))],
        out_specs=pl.BlockSpec((t, POOL_WIDTH), lambda i: (i, 0)),
        out_shape=jax.ShapeDtypeStruct((s, POOL_WIDTH), BF),
        scratch_shapes=[pltpu.VMEM((t + tail, POOL_WIDTH), F32), pltpu.VMEM((t + tail, POOL_WIDTH), F32)],
        compiler_params=_params(1))(dpd, dpd)


CHUNK = 512


def _chunks(width, step=CHUNK):
    return [slice(c, c + step) for c in range(0, width, step)]


def pool_mix_gate(pd, wg, uz, scale, *, name, tm=1024):
    s = pd.shape[0]
    g = wg.shape[0]
    tm = _pick(s, tm)

    def body(a_ref, w_ref, z_ref, sc_ref, mm_ref, y_ref):
        mm = jnp.dot(a_ref[...], w_ref[...], preferred_element_type=F32)
        mm_ref[...] = mm
        z = z_ref[...]
        y_ref[...] = ((mm * sc_ref[...]) * (z * _sigmoid(z))).astype(y_ref.dtype)

    blk = pl.BlockSpec((tm, POOL_GROUP), lambda i, gi: (i, gi))
    return pl.pallas_call(
        body, name=name, grid=(s // tm, g),
        in_specs=[blk, pl.BlockSpec((None, POOL_GROUP, POOL_GROUP), lambda i, gi: (gi, 0, 0)),
                  pl.BlockSpec((tm, POOL_GROUP), lambda i, gi: (i, g + gi)),
                  pl.BlockSpec((1, POOL_GROUP), lambda i, gi: (0, gi))],
        out_specs=[blk, blk],
        out_shape=[jax.ShapeDtypeStruct((s, POOL_WIDTH), F32), jax.ShapeDtypeStruct((s, POOL_WIDTH), BF)],
        compiler_params=_params(2))(pd, wg, uz, scale)


def pool_out_dx_gate(dx, w_out, mm, uz, scale, *, name, tm=512, host=None):
    s, d = dx.shape
    tm = _pick(s, tm)

    def body(dx_ref, w_ref, mm_ref, z_ref, sc_ref, dmm_ref, dz_ref, dsc_ref):
        dxv = dx_ref[...].astype(BF)
        parts = []
        for c in _chunks(POOL_WIDTH):
            dyv = lax.dot_general(dxv, w_ref[c, :], NT, preferred_element_type=F32)
            z = z_ref[:, c]
            sig = _sigmoid(z)
            mmv = mm_ref[:, c]
            scv = sc_ref[:, c]
            dmixed = dyv * (z * sig)
            dmm_ref[:, c] = (dmixed * scv).astype(dmm_ref.dtype)
            dz_ref[:, c] = (dyv * (mmv * scv) * (sig * (1.0 + z * (1.0 - sig)))).astype(dz_ref.dtype)
            parts.append(jnp.sum(dmixed * mmv, axis=0, keepdims=True))

        @pl.when(pl.program_id(0) == 0)
        def _():
            for c, part in zip(_chunks(POOL_WIDTH), parts):
                dsc_ref[:, c] = part

        @pl.when(pl.program_id(0) > 0)
        def _():
            for c, part in zip(_chunks(POOL_WIDTH), parts):
                dsc_ref[:, c] += part

    blk = pl.BlockSpec((tm, POOL_WIDTH), lambda i: (i, 0))
    vec = pl.BlockSpec((1, POOL_WIDTH), lambda i: (0, 0))
    return _call(
        body, name=name, grid=(s // tm,),
        in_specs=[pl.BlockSpec((tm, d), lambda i: (i, 0)), pl.BlockSpec((POOL_WIDTH, d), lambda i: (0, 0)),
                  blk, pl.BlockSpec((tm, POOL_WIDTH), lambda i: (i, 1)), vec],
        out_specs=[blk, blk, vec],
        out_shape=[jax.ShapeDtypeStruct((s, POOL_WIDTH), BF), jax.ShapeDtypeStruct((s, POOL_WIDTH), BF),
                   jax.ShapeDtypeStruct((1, POOL_WIDTH), F32)],
        args=[dx, w_out, mm, uz, scale], host=host)


def _rope(a, cc, sa, sb):
    return a * cc + pltpu.roll(a, 96, 1) * sa + pltpu.roll(a, 32, 1) * sb


def _unrope(d, cc, sa, sb):
    return d * cc + pltpu.roll(d * sa, 32, 1) + pltpu.roll(d * sb, 96, 1)


def q_proj_rope(qn, wq, cc, sa, sb, *, name, tm=1024, heads=8):
    s, kk = qn.shape
    tm = _pick(s, tm)
    tn = heads * HEAD_PAD

    def body(a_ref, b_ref, cc_ref, sa_ref, sb_ref, o_ref):
        q = jnp.dot(a_ref[...], b_ref[...], preferred_element_type=F32)
        for h in range(heads):
            nope = slice(h * HEAD_PAD, h * HEAD_PAD + QK_NOPE)
            rope = slice(h * HEAD_PAD + QK_NOPE, (h + 1) * HEAD_PAD)
            o_ref[:, nope] = q[:, nope].astype(o_ref.dtype)
            o_ref[:, rope] = _rope(q[:, rope], cc_ref[...], sa_ref[...], sb_ref[...]).astype(o_ref.dtype)

    tab = pl.BlockSpec((tm, 128), lambda i, j: (i, 0))
    return pl.pallas_call(
        body, name=name, grid=(s // tm, N_HEADS // heads),
        in_specs=[pl.BlockSpec((tm, kk), lambda i, j: (i, 0)), pl.BlockSpec((kk, tn), lambda i, j: (0, j)),
                  tab, tab, tab],
        out_specs=pl.BlockSpec((tm, tn), lambda i, j: (i, j)),
        out_shape=jax.ShapeDtypeStruct((s, N_HEADS * HEAD_PAD), BF), compiler_params=_params(2))(qn, wq, cc, sa, sb)


LAT_KV = slice(P_KV, P_KV + KV_LORA)
LAT_KR = slice(P_KR, P_KR + 128)
LAT_Q = slice(P_Q, P_Q + Q_LORA)


def latent_fwd(proj, g_q, g_kv, cc, sa, sb, *, name, t=512):
    s = proj.shape[0]
    t = _pick(s, t)

    def body(p_ref, gq_ref, gkv_ref, cc_ref, sa_ref, sb_ref, qn_ref, kvn_ref, kr_ref):
        qn_ref[...] = _rms(p_ref[:, LAT_Q], gq_ref[...]).astype(qn_ref.dtype)
        kvn_ref[...] = _rms(p_ref[:, LAT_KV], gkv_ref[...]).astype(kvn_ref.dtype)
        kr_ref[...] = _rope(p_ref[:, LAT_KR], cc_ref[...], sa_ref[...], sb_ref[...]).astype(kr_ref.dtype)

    tab = pl.BlockSpec((t, 128), lambda i: (i, 0))
    return pl.pallas_call(
        body, name=name, grid=(s // t,),
        in_specs=[pl.BlockSpec((t, P_SMALL), lambda i: (i, 0)), pl.BlockSpec((1, Q_LORA), lambda i: (0, 0)),
                  pl.BlockSpec((1, KV_LORA), lambda i: (0, 0)), tab, tab, tab],
        out_specs=[pl.BlockSpec((t, Q_LORA), lambda i: (i, 0)), pl.BlockSpec((t, KV_LORA), lambda i: (i, 0)), tab],
        out_shape=[jax.ShapeDtypeStruct((s, Q_LORA), BF), jax.ShapeDtypeStruct((s, KV_LORA), BF),
                   jax.ShapeDtypeStruct((s, 128), BF)],
        compiler_params=_params(1))(proj, g_q, g_kv, cc, sa, sb)


def latent_bwd(proj, g_q, g_kv, dqn, dkvn, dkr, cc, sa, sb, *, name, t=512):
    s = proj.shape[0]
    t = _pick(s, t)

    def body(p_ref, gq_ref, gkv_ref, dqn_ref, dkvn_ref, dkr_ref, cc_ref, sa_ref, sb_ref, d_ref, dgq_ref, dgkv_ref):
        dq, dgq = _rms_bwd(p_ref[:, LAT_Q], gq_ref[...], dqn_ref[...])
        dkv, dgkv = _rms_bwd(p_ref[:, LAT_KV], gkv_ref[...], dkvn_ref[...])
        d_ref[:, LAT_Q] = dq.astype(d_ref.dtype)
        d_ref[:, LAT_KV] = dkv.astype(d_ref.dtype)
        d_ref[:, LAT_KR] = _unrope(dkr_ref[...], cc_ref[...], sa_ref[...], sb_ref[...]).astype(d_ref.dtype)
        _accumulate(dgq_ref, dgq)
        _accumulate(dgkv_ref, dgkv)

    tab = pl.BlockSpec((t, 128), lambda i: (i, 0))
    small = pl.BlockSpec((t, P_SMALL), lambda i: (i, 0))
    gq = pl.BlockSpec((1, Q_LORA), lambda i: (0, 0))
    gkv = pl.BlockSpec((1, KV_LORA), lambda i: (0, 0))
    return pl.pallas_call(
        body, name=name, grid=(s // t,),
        in_specs=[small, gq, gkv, pl.BlockSpec((t, Q_LORA), lambda i: (i, 0)),
                  pl.BlockSpec((t, KV_LORA), lambda i: (i, 0)), tab, tab, tab, tab],
        out_specs=[small, gq, gkv],
        out_shape=[jax.ShapeDtypeStruct((s, P_SMALL), BF), jax.ShapeDtypeStruct((1, Q_LORA), F32),
                   jax.ShapeDtypeStruct((1, KV_LORA), F32)],
        compiler_params=_params(1))(proj, g_q, g_kv, dqn, dkvn, dkr, cc, sa, sb)


def mla_out_dx_gate(dx, w_out, o, proj, *, name, tq):
    s, d = dx.shape
    nq = s // tq

    def body(dx_ref, w_ref, o_ref, p_ref, do_ref, dz_ref, dl_ref):
        dxv = dx_ref[...].astype(BF)
        lane = lax.broadcasted_iota(jnp.int32, (tq, 128), 1)
        deltas = jnp.zeros((tq, 128), F32)
        for c in _chunks(MLA_WIDTH):
            dy_c = lax.dot_general(dxv, w_ref[c, :], NT, preferred_element_type=F32)
            for h in range(c.start // V_DIM, c.stop // V_DIM):
                hc = slice(h * V_DIM, (h + 1) * V_DIM)
                z = p_ref[:, slice(P_Z + hc.start, P_Z + hc.stop)]
                sig = _sigmoid(z)
                dyv = dy_c[:, hc.start - c.start:hc.stop - c.start]
                ov = o_ref[:, hc]
                dov = dyv * (z * sig)
                do_ref[:, hc] = dov.astype(do_ref.dtype)
                dz_ref[:, hc] = (dyv * ov * (sig * (1.0 + z * (1.0 - sig)))).astype(dz_ref.dtype)
                deltas = jnp.where(lane == h, jnp.sum(dov * ov, axis=-1, keepdims=True), deltas)
        rows = deltas.T
        for h in range(N_HEADS):
            dl_ref[h] = jnp.broadcast_to(rows[h:h + 1, :], (8, tq))

    blk = pl.BlockSpec((tq, MLA_WIDTH), lambda i: (i, 0))
    return pl.pallas_call(
        body, name=name, grid=(nq,),
        in_specs=[pl.BlockSpec((tq, d), lambda i: (i, 0)), pl.BlockSpec((MLA_WIDTH, d), lambda i: (0, 0)),
                  blk, pl.BlockSpec((tq, P_WIDTH), lambda i: (i, 0))],
        out_specs=[blk, blk, pl.BlockSpec((N_HEADS, None, 8, tq), lambda i: (0, i, 0, 0))],
        out_shape=[jax.ShapeDtypeStruct((s, MLA_WIDTH), BF), jax.ShapeDtypeStruct((s, MLA_WIDTH), BF),
                   jax.ShapeDtypeStruct((N_HEADS, nq, 8, tq), F32)],
        compiler_params=_params(1))(dx, w_out, o, proj)


FWD_GROUPS = (4, 3, 2, 1)
BWD_GROUPS = (4, 3, 2, 1)


def _for_groups(first, count, groups, fn):
    lead = groups[-1]
    for g in groups[:-1][::-1]:
        lead = jnp.where(count >= g, g, lead)
    for g in groups:
        @pl.when(lead == g)
        def _(g=g):
            fn(first, g, True)
    first = first + lead
    count = count - lead
    for g in groups:
        n = count // g

        def one(p, carry, g=g, first=first):
            fn(first + p * g, g, False)
            return carry

        lax.fori_loop(0, n, one, 0)
        first = first + n * g
        count = count - n * g


def attn_fwd(qr, kv, krr, proj, *, name, tq):
    s = qr.shape[0]
    nq = s // tq
    z_blk = P_Z // V_DIM

    def body(kn_ref, v_ref, kr_ref, q_ref, z_ref, o_ref, y_ref, lse_ref, acc_sc, m_sc):
        j = pl.program_id(1)

        @pl.when(j == 0)
        def _():
            acc_sc[...] = jnp.zeros((nq, 2 * V_DIM, tq), F32)
            m_sc[...] = jnp.full((nq, 8, tq), NEG, F32)

        k = jnp.concatenate([kn_ref[...], kr_ref[...]], axis=1)
        vxt = jnp.concatenate([v_ref[...].astype(F32).T.astype(BF), jnp.ones((V_DIM, tq), BF)], axis=0)

        def update(i, n_tiles, masked):
            rows = pl.ds(pl.multiple_of(i * tq, tq), n_tiles * tq)
            st = lax.dot_general(k, q_ref[rows, :], NT, preferred_element_type=F32) * SCALE_LOG2E
            if masked:
                krow = lax.broadcasted_iota(jnp.int32, (tq, n_tiles * tq), 0)
                qcol = lax.broadcasted_iota(jnp.int32, (tq, n_tiles * tq), 1)
                st = jnp.where(qcol >= krow, st, NEG)
            m_prev = jnp.concatenate([m_sc[i + n, pl.ds(0, 1), :] for n in range(n_tiles)], axis=1)
            m_new = jnp.maximum(m_prev, jnp.max(st, axis=0, keepdims=True))
            alpha = jnp.exp2(m_prev - m_new)
            pt = jnp.exp2(st - m_new).astype(BF)
            pv_t = jnp.dot(vxt, pt, preferred_element_type=F32)
            for n in range(n_tiles):
                cols = slice(n * tq, (n + 1) * tq)
                acc_sc[i + n] = alpha[:, cols] * acc_sc[i + n] + pv_t[:, cols]
                m_sc[i + n, pl.ds(0, 1), :] = m_new[:, cols]

        _for_groups(j, nq - j, FWD_GROUPS, update)
        l = acc_sc[j, V_DIM:, :]
        o = (acc_sc[j, :V_DIM, :] / l).T
        o_ref[...] = o
        z = z_ref[...]
        y_ref[...] = (o * (z * _sigmoid(z))).astype(y_ref.dtype)
        lse_ref[...] = m_sc[j, pl.ds(0, 1), :] + jnp.log2(l[:8, :])

    tile = pl.BlockSpec((tq, V_DIM), lambda h, j: (j, h))
    return pl.pallas_call(
        body, name=name, grid=(N_HEADS, nq),
        in_specs=[pl.BlockSpec((tq, QK_NOPE), lambda h, j: (j, 2 * h)),
                  pl.BlockSpec((tq, V_DIM), lambda h, j: (j, 2 * h + 1)),
                  pl.BlockSpec((tq, 128), lambda h, j: (j, 0)),
                  pl.BlockSpec((s, HEAD_PAD), lambda h, j: (0, h)),
                  pl.BlockSpec((tq, V_DIM), lambda h, j: (j, z_blk + h))],
        out_specs=[tile, tile, pl.BlockSpec((None, None, 8, tq), lambda h, j: (h, j, 0, 0))],
        out_shape=[jax.ShapeDtypeStruct((s, N_HEADS * V_DIM), F32),
                   jax.ShapeDtypeStruct((s, N_HEADS * V_DIM), BF),
                   jax.ShapeDtypeStruct((N_HEADS, nq, 8, tq), F32)],
        scratch_shapes=[pltpu.VMEM((nq, 2 * V_DIM, tq), F32), pltpu.VMEM((nq, 8, tq), F32)],
        compiler_params=_params(2))(kv, kv, krr, qr, proj)


def attn_bwd(qr, kv, krr, do, lse, delta, cc, sa, sb, *, name, tq):
    s = qr.shape[0]
    nq = s // tq

    def body(kn_ref, v_ref, kr_ref, q_ref, do_ref, lse_ref, dl_ref, cc_ref, sa_ref, sb_ref,
             dkv_ref, dkr_ref, dq_ref, dq_sc, dk_sc, dv_sc):
        h = pl.program_id(0)
        j = pl.program_id(1)

        @pl.when(j == 0)
        def _():
            dq_sc[...] = jnp.zeros((s, HEAD_PAD), F32)

        dk_sc[...] = jnp.zeros((tq, HEAD_PAD), F32)
        dv_sc[...] = jnp.zeros((tq, V_DIM), F32)
        k = jnp.concatenate([kn_ref[...], kr_ref[...]], axis=1)
        v = v_ref[...]

        def step(i, n_tiles, masked):
            r0 = pl.multiple_of(i * tq, tq)
            rows = pl.ds(r0, n_tiles * tq)
            q = q_ref[rows, :]
            dov = do_ref[rows, :]
            lse_row = jnp.concatenate([lse_ref[i + n, pl.ds(0, 1), :] for n in range(n_tiles)], axis=1)
            dl_row = jnp.concatenate([dl_ref[i + n, pl.ds(0, 1), :] for n in range(n_tiles)], axis=1)
            st = lax.dot_general(k, q, NT, preferred_element_type=F32) * SCALE_LOG2E
            if masked:
                krow = lax.broadcasted_iota(jnp.int32, (tq, n_tiles * tq), 0)
                qcol = lax.broadcasted_iota(jnp.int32, (tq, n_tiles * tq), 1)
                st = jnp.where(qcol >= krow, st, NEG)
            pt = jnp.exp2(st - lse_row)
            dpt = lax.dot_general(v, dov, NT, preferred_element_type=F32)
            dst = (pt * (dpt - dl_row)).astype(BF)
            dv_sc[...] += jnp.dot(pt.astype(BF), dov, preferred_element_type=F32)
            dk_sc[...] += jnp.dot(dst, q, preferred_element_type=F32)
            dq_sc[rows, :] += lax.dot_general(dst, k, TN, preferred_element_type=F32)

        _for_groups(j, nq - j, BWD_GROUPS, step)
        dkv_ref[:, :QK_NOPE] = (dk_sc[:, :QK_NOPE] * SCALE).astype(dkv_ref.dtype)
        dkv_ref[:, QK_NOPE:] = dv_sc[...].astype(dkv_ref.dtype)
        mine = pl.ds(pl.multiple_of(j * tq, tq), tq)
        dkr = dk_sc[:, QK_NOPE:] * SCALE

        @pl.when(h == 0)
        def _():
            dkr_ref[mine, :] = dkr

        @pl.when(h > 0)
        def _():
            dkr_ref[mine, :] += dkr

        dq_ref[:, :QK_NOPE] = (dq_sc[mine, :QK_NOPE] * SCALE).astype(dq_ref.dtype)
        dq_ref[:, QK_NOPE:] = _unrope(dq_sc[mine, QK_NOPE:] * SCALE, cc_ref[...], sa_ref[...],
                                      sb_ref[...]).astype(dq_ref.dtype)

    rows = pl.BlockSpec((None, nq, 8, tq), lambda h, j: (h, 0, 0, 0))
    tab = pl.BlockSpec((tq, 128), lambda h, j: (j, 0))
    return pl.pallas_call(
        body, name=name, grid=(N_HEADS, nq),
        in_specs=[pl.BlockSpec((tq, QK_NOPE), lambda h, j: (j, 2 * h)),
                  pl.BlockSpec((tq, V_DIM), lambda h, j: (j, 2 * h + 1)), tab,
                  pl.BlockSpec((s, HEAD_PAD), lambda h, j: (0, h)),
                  pl.BlockSpec((s, V_DIM), lambda h, j: (0, h)), rows, rows, tab, tab, tab],
        out_specs=[pl.BlockSpec((tq, 256), lambda h, j: (j, h)),
                   pl.BlockSpec((s, 128), lambda h, j: (0, 0)),
                   pl.BlockSpec((tq, HEAD_PAD), lambda h, j: (j, h))],
        out_shape=[jax.ShapeDtypeStruct((s, N_HEADS * 256), BF),
                   jax.ShapeDtypeStruct((s, 128), F32),
                   jax.ShapeDtypeStruct((s, N_HEADS * HEAD_PAD), BF)],
        scratch_shapes=[pltpu.VMEM((s, HEAD_PAD), F32), pltpu.VMEM((tq, HEAD_PAD), F32),
                        pltpu.VMEM((tq, V_DIM), F32)],
        compiler_params=_params(2))(kv, kv, krr, qr, do, lse, delta, cc, sa, sb)


def adamw(w, g, m, v, *, name, t=256):
    r, c = w.shape
    t = r if r % t else t
    c1 = 1.0 - ADAM_B1 ** ADAM_STEP
    c2 = 1.0 - ADAM_B2 ** ADAM_STEP

    def body(w_ref, g_ref, m_ref, v_ref, d_ref, nm_ref, nv_ref):
        gv = g_ref[...]
        nm = ADAM_B1 * m_ref[...] + (1.0 - ADAM_B1) * gv
        nv = ADAM_B2 * v_ref[...] + (1.0 - ADAM_B2) * (gv * gv)
        nm_ref[...] = nm
        nv_ref[...] = nv
        d_ref[...] = -ADAM_LR * ((nm / c1) / (jnp.sqrt(nv / c2) + ADAM_EPS) + ADAM_WD * w_ref[...])

    blk = pl.BlockSpec((t, c), lambda i: (i, 0))
    return pl.pallas_call(
        body, name=name, grid=(r // t,), in_specs=[blk] * 4, out_specs=[blk] * 3,
        out_shape=[jax.ShapeDtypeStruct((r, c), F32)] * 3, compiler_params=_params(1))(w, g, m, v)


def sum_devices(parts, *, name):
    def body(p_ref, o_ref):
        acc = p_ref[pl.ds(0, SV_ROWS), :]
        for d in range(1, 8):
            acc = acc + p_ref[pl.ds(d * SV_ROWS, SV_ROWS), :]
        o_ref[...] = acc

    return pl.pallas_call(body, name=name, out_shape=jax.ShapeDtypeStruct((SV_ROWS, SV_COLS), F32))(parts)


def add_halves(g, rb, c_idx, *, name, rows):
    nq, r2, cc = rb.shape
    nb = r2 // rows

    def body(c_ref, g_ref, r_ref, o_ref):
        o_ref[...] = (g_ref[...] + r_ref[...]).astype(o_ref.dtype)

    grid_spec = pltpu.PrefetchScalarGridSpec(
        num_scalar_prefetch=1, grid=(nq, nb),
        in_specs=[pl.BlockSpec((None, rows, cc), lambda q, i, c: (q, c[0] * nb + i, 0)),
                  pl.BlockSpec((None, rows, cc), lambda q, i, c: (q, i, 0))],
        out_specs=pl.BlockSpec((None, rows, cc), lambda q, i, c: (q, i, 0)))
    return pl.pallas_call(body, name=name, grid_spec=grid_spec,
                          out_shape=jax.ShapeDtypeStruct((nq, r2, cc), BF),
                          compiler_params=_params(2))(c_idx, g, rb)


def sum_chips(rc, c_idx, *, name, rows):
    nq, r2, cc = rc.shape
    nb = r2 // rows

    def body(c_ref, r_ref, o_ref):
        parts = [r_ref[q].astype(F32) for q in range(4)]
        o_ref[...] = ((parts[0] + parts[1]) + parts[2]) + parts[3]

    grid_spec = pltpu.PrefetchScalarGridSpec(
        num_scalar_prefetch=1, grid=(nb,),
        in_specs=[pl.BlockSpec((nq, rows, cc), lambda i, c: (0, i, 0))],
        out_specs=pl.BlockSpec((rows, cc), lambda i, c: (c[0] * nb + i, 0)))
    return pl.pallas_call(body, name=name, grid_spec=grid_spec,
                          out_shape=jax.ShapeDtypeStruct((2 * r2, cc), F32),
                          compiler_params=_params(1))(c_idx, rc)


def _place():
    return lax.axis_index("x"), lax.axis_index("y"), lax.axis_index("c")


def all_gather8(xs, *, name, own_half):
    m = xs.shape[0] // 2 if own_half else xs.shape[0]
    n = xs.shape[1]

    def body(x_ref, out_ref, send_sems, recv_sems, local_sem):
        x, y, c = _place()
        me, sibling = (x, y, c), (x, y, 1 - c)
        chips = [(1 - x, y), (x, 1 - y), (1 - x, 1 - y)]
        src_own = x_ref.at[pl.ds(c * m, m), :] if own_half else x_ref

        def rows(px, py, pc):
            return out_ref.at[pl.ds((4 * px + 2 * py + pc) * m, m), :]

        def copy(k, block, to, src=None):
            return pltpu.make_async_remote_copy(
                src_ref=rows(*block) if src is None else src, dst_ref=rows(*block),
                send_sem=send_sems.at[k], recv_sem=recv_sems.at[k], device_id=to, device_id_type=MESH)

        mine = pltpu.make_async_copy(src_own, rows(*me), local_sem)
        mine.start()
        first = [copy(0, me, sibling, src=src_own)]
        first += [copy(1 + j, me, (*chip, c), src=src_own) for j, chip in enumerate(chips)]
        for cp in first:
            cp.start()
        passed = [copy(4 + j, (*chip, c), sibling) for j, chip in enumerate(chips)]
        for j, chip in enumerate(chips):
            copy(1 + j, (*chip, c), me).wait_recv()
            passed[j].start()
        copy(0, sibling, me).wait_recv()
        for j, chip in enumerate(chips):
            copy(4 + j, (*chip, 1 - c), me).wait_recv()
        for cp in first + passed:
            cp.wait_send()
        mine.wait()

    return pl.pallas_call(
        body, name=name, out_shape=jax.ShapeDtypeStruct((8 * m, n), xs.dtype),
        in_specs=[pl.BlockSpec(memory_space=pl.ANY)], out_specs=pl.BlockSpec(memory_space=pl.ANY),
        scratch_shapes=[pltpu.SemaphoreType.DMA((7,)), pltpu.SemaphoreType.DMA((7,)), pltpu.SemaphoreType.DMA],
    )(xs)


def _other_chips():
    x, y, c = _place()
    return [(1 - x, y), (x, 1 - y), (1 - x, 1 - y)]


def _remote(src, dst, send_sems, recv_sems, k, to):
    return pltpu.make_async_remote_copy(src_ref=src, dst_ref=dst, send_sem=send_sems.at[k], recv_sem=recv_sems.at[k],
                                        device_id=to, device_id_type=MESH)


def gather_ici(xs):
    r, cc = xs.shape
    m = r // 2

    def copies(ins, outs, ss, rs, landing):
        x, y, c = _place()
        half = pl.ds(c * m, m)
        return [_remote(ins[0].at[half, :], outs[0].at[(2 * cx + cy) if landing else (2 * x + y), half, :],
                        ss, rs, j, (cx, cy, c)) for j, (cx, cy) in enumerate(_other_chips())]

    def start(ins, outs, ss, rs, ls):
        for cp in copies(ins, outs, ss, rs, False):
            cp.start()

    def wait(ins, outs, ss, rs, ls):
        for cp in copies(ins, outs, ss, rs, True):
            cp.wait_recv()
        for cp in copies(ins, outs, ss, rs, False):
            cp.wait_send()

    return Exchange((xs,), (jax.ShapeDtypeStruct((4, r, cc), xs.dtype),), {}, 3, start, wait)


def gather_forward(buf):
    m = buf.shape[1] // 2

    def copies(outs, ss, rs, landing):
        x, y, c = _place()
        half = pl.ds(((1 - c) if landing else c) * m, m)
        return [_remote(outs[0].at[2 * cx + cy, half, :], outs[0].at[2 * cx + cy, half, :], ss, rs, j, (x, y, 1 - c))
                for j, (cx, cy) in enumerate(_other_chips())]

    def start(ins, outs, ss, rs, ls):
        for cp in copies(outs, ss, rs, False):
            cp.start()

    def wait(ins, outs, ss, rs, ls):
        for cp in copies(outs, ss, rs, True):
            cp.wait_recv()
        for cp in copies(outs, ss, rs, False):
            cp.wait_send()

    return Exchange((buf,), (jax.ShapeDtypeStruct(buf.shape, buf.dtype),), {0: 0}, 3, start, wait)


def swap_halves(g):
    nq, r, cc = g.shape
    r2 = r // 2

    def copy(ins, outs, ss, rs):
        x, y, c = _place()
        return _remote(ins[0].at[:, pl.ds((1 - c) * r2, r2), :], outs[0], ss, rs, 0, (x, y, 1 - c))

    def start(ins, outs, ss, rs, ls):
        copy(ins, outs, ss, rs).start()

    def wait(ins, outs, ss, rs, ls):
        copy(ins, outs, ss, rs).wait()

    return Exchange((g,), (jax.ShapeDtypeStruct((nq, r2, cc), g.dtype),), {}, 1, start, wait)


def exchange_chips(p):
    def own(ins, outs, ls):
        x, y, c = _place()
        return pltpu.make_async_copy(ins[0].at[2 * x + y], outs[0].at[2 * x + y], ls)

    def copies(ins, outs, ss, rs, landing):
        x, y, c = _place()
        return [_remote(ins[0].at[2 * cx + cy], outs[0].at[(2 * cx + cy) if landing else (2 * x + y)],
                        ss, rs, j, (cx, cy, c)) for j, (cx, cy) in enumerate(_other_chips())]

    def start(ins, outs, ss, rs, ls):
        own(ins, outs, ls).start()
        for cp in copies(ins, outs, ss, rs, False):
            cp.start()

    def wait(ins, outs, ss, rs, ls):
        for cp in copies(ins, outs, ss, rs, True):
            cp.wait_recv()
        for cp in copies(ins, outs, ss, rs, False):
            cp.wait_send()
        own(ins, outs, ls).wait()

    return Exchange((p,), (jax.ShapeDtypeStruct(p.shape, p.dtype),), {}, 3, start, wait)


def join_halves(tot):
    r2 = tot.shape[0] // 2

    def copy(outs, ss, rs, landing):
        x, y, c = _place()
        half = outs[0].at[pl.ds(((1 - c) if landing else c) * r2, r2), :]
        return _remote(half, half, ss, rs, 0, (x, y, 1 - c))

    def start(ins, outs, ss, rs, ls):
        copy(outs, ss, rs, False).start()

    def wait(ins, outs, ss, rs, ls):
        copy(outs, ss, rs, True).wait_recv()
        copy(outs, ss, rs, False).wait_send()

    return Exchange((tot,), (jax.ShapeDtypeStruct(tot.shape, tot.dtype),), {0: 0}, 1, start, wait)


def _pack_shard(blocks, small_vec=None):
    parts = [w.reshape(-1, PACK_C).astype(BF) for w in blocks]
    if small_vec is not None:
        srow = lax.bitcast_convert_type(small_vec, BF).reshape(1, PACK_C)
        parts.append(jnp.pad(srow, ((0, PACK_PAD - 1), (0, 0))))
    return jnp.concatenate(parts, axis=0)


def _split_rows(a, rows, axis):
    out, off = [], 0
    for n in rows:
        out.append(lax.slice_in_dim(a, off, off + n, axis=axis))
        off += n
    return out


def _unpack_pool(gw):
    p_in, p_grp, p_out = _split_rows(gw, POOL_ROWS, 1)
    return dict(
        pool_w_in=p_in.reshape(4, D_MODEL, 1024).transpose(1, 0, 2).reshape(D_MODEL, 2 * POOL_WIDTH),
        pool_w_group=p_grp.reshape(4, 4, 128, POOL_GROUP).transpose(1, 0, 2, 3).reshape(4, POOL_GROUP, POOL_GROUP),
        pool_w_out=p_out.reshape(POOL_WIDTH, D_MODEL))


def _unpack_mla(gw):
    m_in, m_qb, m_kvb, m_out, small = _split_rows(gw, MLA_ROWS + (PACK_PAD,), 1)
    w = {}
    win = m_in.reshape(4, D_MODEL, 688).transpose(1, 0, 2).reshape(D_MODEL, 2752)
    w["mla_w_in"] = jnp.concatenate(
        [win[:, 384:640], win[:, 640:704], jnp.zeros((D_MODEL, 64), BF), win[:, 0:384], win[:, 704:]], axis=1)
    wq = m_qb.reshape(4, Q_LORA, 768).transpose(1, 0, 2).reshape(Q_LORA, N_HEADS, QK_NOPE + QK_ROPE)
    w["mla_w_q_b"] = jnp.pad(wq, ((0, 0), (0, 0), (0, HEAD_PAD - QK_NOPE - QK_ROPE))).reshape(Q_LORA, N_HEADS * HEAD_PAD)
    w["mla_w_kv_b"] = m_kvb.reshape(4, KV_LORA, 1024).transpose(1, 0, 2).reshape(KV_LORA, 4096)
    w["mla_w_out"] = m_out.reshape(MLA_WIDTH, D_MODEL)
    small = lax.bitcast_convert_type(small[:, 0, :].reshape(4, 512, 2), F32)
    w["mla_norm"] = small[:, :256].reshape(1, D_MODEL)
    w["mla_q_norm"] = small[:, 256:352].reshape(1, Q_LORA)
    w["mla_kv_norm"] = small[:, 352:416].reshape(1, KV_LORA)
    return w


def _pack_pool_grads(g):
    return jnp.concatenate([
        g["pool_w_in"],
        g["pool_w_group"].reshape(4, 4, 128, POOL_GROUP).transpose(1, 0, 2, 3).reshape(4, 256, PACK_C),
        g["pool_w_out"].reshape(4, 512, PACK_C)], axis=1)


def _pack_mla_grads(g):
    return jnp.concatenate([
        g["mla_w_in"].reshape(D_MODEL, 4, 688).transpose(1, 0, 2).reshape(4, 688, PACK_C),
        g["mla_w_q_b"].reshape(Q_LORA, 4, 768).transpose(1, 0, 2).reshape(4, 288, PACK_C),
        g["mla_w_kv_b"],
        g["mla_w_out"].reshape(4, 512, PACK_C),
        jnp.zeros((4, PACK_PAD, PACK_C), F32)], axis=1)


def kernel(x, positions, pool_norm, pool_w_in, pool_w_group, pool_scale, pool_w_out, mla_norm, mla_w_in, mla_q_norm, mla_w_q_b, mla_kv_norm, mla_w_kv_b, mla_w_out, final_norm, loss_target, m_pool_norm, m_pool_w_in, m_pool_w_group, m_pool_scale, m_pool_w_out, m_mla_norm, m_mla_w_in, m_mla_q_norm, m_mla_w_q_b, m_mla_kv_norm, m_mla_w_kv_b, m_mla_w_out, m_final_norm, v_pool_norm, v_pool_w_in, v_pool_w_group, v_pool_scale, v_pool_w_out, v_mla_norm, v_mla_w_in, v_mla_q_norm, v_mla_w_q_b, v_mla_kv_norm, v_mla_w_kv_b, v_mla_w_out, v_final_norm):
    s = x.shape[1]
    tq = min(512, s)
    x0 = x.reshape(s, D_MODEL)
    tgt = loss_target.reshape(s, D_MODEL)
    cx, cy, cc_idx = _place()
    chip = 2 * cx + cy

    big_names = ("pool_w_in", "pool_w_group", "pool_w_out", "mla_w_in", "mla_w_q_b", "mla_w_kv_b", "mla_w_out")
    big_w = dict(zip(big_names, (pool_w_in, pool_w_group, pool_w_out, mla_w_in, mla_w_q_b, mla_w_kv_b, mla_w_out)))
    big_m = dict(zip(big_names, (m_pool_w_in, m_pool_w_group, m_pool_w_out, m_mla_w_in, m_mla_w_q_b, m_mla_w_kv_b, m_mla_w_out)))
    big_v = dict(zip(big_names, (v_pool_w_in, v_pool_w_group, v_pool_w_out, v_mla_w_in, v_mla_w_q_b, v_mla_w_kv_b, v_mla_w_out)))

    small_vec = jnp.concatenate([mla_norm.reshape(-1), mla_q_norm.reshape(-1), mla_kv_norm.reshape(-1),
                                 jnp.zeros((96,), F32)])
    pool_packed = _pack_shard([big_w[n] for n in big_names[:3]])
    mla_packed = _pack_shard([big_w[n] for n in big_names[3:]], small_vec)
    w = _unpack_pool(all_gather8(pool_packed, name="gather_pool_weights", own_half=True).reshape(4, POOL_R, PACK_C))
    g_pool = pool_norm.reshape(1, D_MODEL)
    g_final = final_norm.reshape(1, D_MODEL)
    sc_pool = pool_scale.reshape(1, POOL_WIDTH)

    inv_freq = 1.0 / (ROPE_THETA ** (jnp.arange(0, QK_ROPE, 2, dtype=F32) / QK_ROPE))
    ang = positions.reshape(s).astype(F32)[:, None] * inv_freq
    cos, sin = jnp.cos(ang), jnp.sin(ang)
    z32, z64, z96 = (jnp.zeros((s, n), F32) for n in (32, 64, 96))
    t_cc = jnp.concatenate([cos, cos, z64], axis=1)
    t_sa = jnp.concatenate([-sin, z96], axis=1)
    t_sb = jnp.concatenate([z32, sin, z64], axis=1)

    h0 = norm_fwd(x0, g_pool, name="pool_norm_fwd")
    uz, mla_land = mm_nn(h0, w["pool_w_in"], name="pool_in_proj", out_dtype=F32, host=gather_ici(mla_packed))
    pd = pool_prep(uz, name="pool_window")
    mm, y1 = pool_mix_gate(pd, w["pool_w_group"], uz, sc_pool, name="pool_group_mix")
    x1, mla_land = mm_nn(y1, w["pool_w_out"], name="pool_out_proj", out_dtype=F32, add=x0,
                         host=gather_forward(mla_land))
    w.update(_unpack_mla(lax.dynamic_update_slice_in_dim(mla_land, mla_packed[None], chip, axis=0)))

    h1 = norm_fwd(x1, w["mla_norm"], name="mla_norm_fwd")
    proj = mm_nn(h1, w["mla_w_in"], name="mla_in_proj", out_dtype=F32, tn=P_WIDTH // 2)
    qn, kvn, krr = latent_fwd(proj, w["mla_q_norm"], w["mla_kv_norm"], t_cc, t_sa, t_sb, name="mla_latent_fwd")
    qr = q_proj_rope(qn, w["mla_w_q_b"], t_cc, t_sa, t_sb, name="mla_q_proj")
    kv = mm_nn(kvn, w["mla_w_kv_b"], name="mla_kv_proj", out_dtype=BF, tn=2048)
    o, y2, lse = attn_fwd(qr, kv, krr, proj, name="mla_attn_fwd", tq=tq)
    dx2, d_final, loss_part = mm_nn_loss(y2, w["mla_w_out"], x1, g_final, tgt, name="mla_out_proj_loss")

    grads = {}
    grads["mla_w_out"] = mm_tn(y2, dx2, name="mla_out_proj_dw")
    do, dz2, delta = mla_out_dx_gate(dx2, w["mla_w_out"], o, proj, name="mla_out_proj_dx", tq=tq)
    dkv, dkr, dq_pre = attn_bwd(qr, kv, krr, do, lse, delta, t_cc, t_sa, t_sb, name="mla_attn_bwd", tq=tq)
    dqn = mm_nt(dq_pre, w["mla_w_q_b"], name="mla_q_proj_dx", out_dtype=F32, tn=Q_LORA, tk=4096)
    g_qb = mm_tn(qn, dq_pre, name="mla_q_proj_dw", tm=Q_LORA, tn=2048)
    dkvn = mm_nt(dkv, w["mla_w_kv_b"], name="mla_kv_proj_dx", out_dtype=F32, tn=KV_LORA, tk=4096)
    grads["mla_w_kv_b"] = mm_tn(kvn, dkv, name="mla_kv_proj_dw", tm=KV_LORA, by_column_block=True)
    dsmall, d_qnorm, d_kvnorm = latent_bwd(proj, w["mla_q_norm"], w["mla_kv_norm"], dqn, dkvn, dkr,
                                           t_cc, t_sa, t_sb, name="mla_latent_bwd")
    dx1, d_mnorm = mm_nt_norm_bwd(dz2, w["mla_w_in"][:, P_Z:], (dsmall, w["mla_w_in"]), x1, w["mla_norm"], dx2,
                                  name="mla_in_proj_dx")
    g_in_a = mm_tn(h1, dsmall, name="mla_in_proj_dw_a", tn=P_SMALL)
    g_in_b = mm_tn(h1, dz2, name="mla_in_proj_dw_b")

    g_in = jnp.concatenate([g_in_a, g_in_b], axis=1)
    grads["mla_w_in"] = jnp.concatenate([g_in[:, P_Q:P_Z], g_in[:, P_KV:P_KV + KV_LORA],
                                         g_in[:, P_KR:P_KR + QK_ROPE], g_in[:, P_Z:]], axis=1)
    grads["mla_w_q_b"] = g_qb.reshape(Q_LORA, N_HEADS, HEAD_PAD)[:, :, :QK_NOPE + QK_ROPE].reshape(Q_LORA, -1)
    core_idx = cc_idx.reshape(1).astype(jnp.int32)
    gp_mla = _pack_mla_grads(grads)

    grads["pool_w_out"], sib = mm_tn(y1, dx1, name="pool_out_proj_dw", host=swap_halves(gp_mla))
    pre = add_halves(gp_mla, sib, core_idx, name="mla_grad_add_halves", rows=MLA_R // 2)
    dmm, dz1, d_scale, got = pool_out_dx_gate(dx1, w["pool_w_out"], mm, uz, sc_pool, name="pool_out_proj_dx",
                                              host=exchange_chips(pre))
    tot = sum_chips(got, core_idx, name="mla_grad_sum_chips", rows=MLA_R // 2)
    dpd, red_mla = gmm_nt(dmm, w["pool_w_group"], name="pool_group_mix_dx", host=join_halves(tot))
    grads["pool_w_group"] = gmm_tn(pd, dmm, 4, name="pool_group_mix_dw")
    du = pool_prep_bwd(dpd, name="pool_window_bwd")
    g_pin_u = mm_tn(h0, du, name="pool_in_proj_dw_u", by_column_block=True)
    g_pin_z = mm_tn(h0, dz1, name="pool_in_proj_dw_z", by_column_block=True)
    grads["pool_w_in"] = jnp.concatenate([g_pin_u, g_pin_z], axis=0)

    gp_pool = _pack_pool_grads(grads)
    dh0, sib = mm_nt(du, w["pool_w_in"], name="pool_in_proj_dx_u", out_dtype=F32, host=swap_halves(gp_pool))
    pre = add_halves(gp_pool, sib, core_idx, name="pool_grad_add_halves", rows=POOL_R // 2)
    grad_x, d_pnorm, got = mm_nt_norm_bwd(dz1, w["pool_w_in"], dh0, x0, g_pool, dx1, name="pool_in_proj_dx_z",
                                          b_col=POOL_WIDTH, host=exchange_chips(pre))
    tot = sum_chips(got, core_idx, name="pool_grad_sum_chips", rows=POOL_R // 2)
    red_pool = run_exchange(join_halves(tot), name="pool_grad_join_halves")[0]
    red_parts = _split_rows(red_pool, POOL_ROWS, 0) + _split_rows(red_mla, MLA_ROWS, 0)

    sv = jnp.concatenate([d_pnorm.reshape(-1), d_scale.reshape(-1), d_final.reshape(-1), d_mnorm.reshape(-1),
                          d_qnorm.reshape(-1), d_kvnorm.reshape(-1), loss_part[0, :1],
                          jnp.zeros((SV_ROWS * SV_COLS - SV_OFF["loss"] - 1,), F32)]).reshape(SV_ROWS, SV_COLS)
    sv_all = all_gather8(sv, name="gather_small_grads", own_half=False)
    sv_sum = sum_devices(sv_all, name="sum_small_grads").reshape(-1)
    loss = sv_sum[SV_OFF["loss"]]

    def sv_take(key, n):
        return lax.slice_in_dim(sv_sum, SV_OFF[key], SV_OFF[key] + n)

    out_g, out_d, out_m, out_v = {}, {}, {}, {}
    for name, part in zip(big_names, red_parts):
        shp = big_w[name].shape
        g2 = part.reshape(shp)
        two_d = (-1, shp[-1])
        d_, m_, v_ = adamw(big_w[name].reshape(two_d), g2.reshape(two_d), big_m[name].reshape(two_d),
                           big_v[name].reshape(two_d), name="adamw_" + name)
        out_g[name], out_d[name], out_m[name], out_v[name] = g2, d_.reshape(shp), m_.reshape(shp), v_.reshape(shp)

    small = [
        ("pool_norm", pool_norm, m_pool_norm, v_pool_norm, sv_take("pool_norm", 1024)),
        ("pool_scale", pool_scale, m_pool_scale, v_pool_scale, sv_take("pool_scale", 2048)),
        ("final_norm", final_norm, m_final_norm, v_final_norm, sv_take("final_norm", 1024)),
        ("mla_norm", mla_norm, m_mla_norm, v_mla_norm,
         lax.dynamic_slice_in_dim(sv_take("mla_norm", 1024), chip * 256, 256)),
        ("mla_q_norm", mla_q_norm, m_mla_q_norm, v_mla_q_norm,
         lax.dynamic_slice_in_dim(sv_take("q_norm", 384), chip * 96, 96)),
        ("mla_kv_norm", mla_kv_norm, m_mla_kv_norm, v_mla_kv_norm,
         lax.dynamic_slice_in_dim(sv_take("kv_norm", 256), chip * 64, 64)),
    ]
    sw = jnp.concatenate([t[1].reshape(-1) for t in small] + [jnp.zeros((96,), F32)]).reshape(1, -1)
    sm = jnp.concatenate([t[2].reshape(-1) for t in small] + [jnp.zeros((96,), F32)]).reshape(1, -1)
    s_v = jnp.concatenate([t[3].reshape(-1) for t in small] + [jnp.ones((96,), F32)]).reshape(1, -1)
    sg = jnp.concatenate([t[4].reshape(-1) for t in small] + [jnp.zeros((96,), F32)]).reshape(1, -1)
    sd_, sm_, sv_ = adamw(sw, sg, sm, s_v, name="adamw_vectors")
    off = 0
    for name, wt, _, _, gvec in small:
        n = gvec.shape[0]
        shp = wt.shape
        out_g[name] = gvec.reshape(shp)
        out_d[name] = sd_[0, off:off + n].reshape(shp)
        out_m[name] = sm_[0, off:off + n].reshape(shp)
        out_v[name] = sv_[0, off:off + n].reshape(shp)
        off += n

    order = ("pool_norm", "pool_w_in", "pool_w_group", "pool_scale", "pool_w_out", "mla_norm", "mla_w_in",
             "mla_q_norm", "mla_w_q_b", "mla_kv_norm", "mla_w_kv_b", "mla_w_out", "final_norm")
    return (loss, grad_x.reshape(x.shape), *[out_g[n] for n in order], *[out_d[n] for n in order],
            *[out_m[n] for n in order], *[out_v[n] for n in order])
```

```python
import functools
from typing import Callable, NamedTuple

import jax
import jax.numpy as jnp
from jax import lax
from jax.experimental import pallas as pl
from jax.experimental.pallas import tpu as pltpu

F32 = jnp.float32
BF = jnp.bfloat16
MESH = pl.DeviceIdType.MESH

D_MODEL = 1024
POOL_WIDTH = 2048
POOL_WINDOWS = (2, 4, 8, 16)
POOL_GROUP = 512
HALO = 16
N_HEADS = 16
QK_NOPE = 128
QK_ROPE = 64
V_DIM = 128
HEAD_PAD = 256
Q_LORA = 384
KV_LORA = 256
MLA_WIDTH = 2048
ROPE_THETA = 10000.0
EPS = 1e-6
SCALE = (QK_NOPE + QK_ROPE) ** -0.5
SCALE_LOG2E = SCALE * 1.4426950408889634
NEG = -1e30

P_KV, P_KR, P_Q, P_Z = 0, 256, 384, 768
P_SMALL = 768
P_WIDTH = 2816

ADAM_LR = 0.001
ADAM_B1 = 0.9
ADAM_B2 = 0.999
ADAM_EPS = 1e-08
ADAM_WD = 0.01
ADAM_STEP = 10

NN = (((1,), (0,)), ((), ()))
NT = (((1,), (1,)), ((), ()))
TN = (((0,), (0,)), ((), ()))

POOL_ROWS = (1024, 256, 512)
MLA_ROWS = (688, 288, 256, 512)
PACK_PAD = 16
POOL_R = sum(POOL_ROWS)
MLA_R = sum(MLA_ROWS) + PACK_PAD
PACK_C = 1024
SV_OFF = dict(pool_norm=0, pool_scale=1024, final_norm=3072, mla_norm=4096, q_norm=5120, kv_norm=5504, loss=5760)
SV_ROWS, SV_COLS = 8, 768

VMEM_LIMIT = 56 * 1024 * 1024


def _params(n_axes, vmem=None):
    return pltpu.CompilerParams(dimension_semantics=("arbitrary",) * n_axes,
                                vmem_limit_bytes=VMEM_LIMIT if vmem is None else vmem)


def _sigmoid(z):
    return 1.0 / (1.0 + jnp.exp(-z))


class Exchange(NamedTuple):
    operands: tuple
    out_shapes: tuple
    aliases: dict
    n_sems: int
    start: Callable
    wait: Callable


HBM_SPEC = pl.BlockSpec(memory_space=pl.ANY)


def _exchange_scratch(ex):
    return [pltpu.SemaphoreType.DMA((ex.n_sems,)), pltpu.SemaphoreType.DMA((ex.n_sems,)), pltpu.SemaphoreType.DMA]


def run_exchange(ex, *, name):
    n_in, n_out = len(ex.operands), len(ex.out_shapes)

    def body(*refs):
        args = (refs[:n_in], refs[n_in:n_in + n_out]) + tuple(refs[n_in + n_out:])
        ex.start(*args)
        ex.wait(*args)

    return pl.pallas_call(
        body, name=name, out_shape=list(ex.out_shapes), in_specs=[HBM_SPEC] * n_in,
        out_specs=[HBM_SPEC] * n_out, scratch_shapes=_exchange_scratch(ex),
        input_output_aliases=dict(ex.aliases))(*ex.operands)


def _call(core, *, name, grid, in_specs, out_specs, out_shape, args, scratch=(), host=None):
    in_specs, out_specs, out_shape = list(in_specs), list(out_specs), list(out_shape)
    params = _params(len(grid))
    if host is None:
        return pl.pallas_call(core, name=name, grid=grid, in_specs=in_specs, out_specs=out_specs,
                              out_shape=out_shape, scratch_shapes=list(scratch), compiler_params=params)(*args)
    n_in, n_out = len(in_specs), len(out_specs)
    n_hin, n_hout = len(host.operands), len(host.out_shapes)

    def body(*refs):
        ins, refs = refs[:n_in], refs[n_in:]
        h_in, refs = refs[:n_hin], refs[n_hin:]
        outs, refs = refs[:n_out], refs[n_out:]
        h_out, refs = refs[:n_hout], refs[n_hout:]
        own_scratch, sems = refs[:-3], refs[-3:]
        ids = [pl.program_id(ax) for ax in range(len(grid))]
        first = functools.reduce(jnp.logical_and, [i == 0 for i in ids])
        last = functools.reduce(jnp.logical_and, [i == n - 1 for i, n in zip(ids, grid)])

        @pl.when(first)
        def _():
            host.start(h_in, h_out, *sems)

        core(*ins, *outs, *own_scratch)

        @pl.when(last)
        def _():
            host.wait(h_in, h_out, *sems)

    return pl.pallas_call(
        body, name=name, grid=grid, in_specs=in_specs + [HBM_SPEC] * n_hin,
        out_specs=out_specs + [HBM_SPEC] * n_hout, out_shape=out_shape + list(host.out_shapes),
        scratch_shapes=list(scratch) + _exchange_scratch(host),
        input_output_aliases={n_in + i: n_out + o for i, o in host.aliases.items()},
        compiler_params=params)(*args, *host.operands)


def _mm(a, b, *, dims, grid, a_spec, b_spec, o_spec, out_shape, out_dtype, acc_shape, name,
        add=None, add_spec=None, host=None):
    nk = grid[-1]
    kax = len(grid) - 1

    def body(*refs):
        if add is None:
            a_ref, b_ref, o_ref = refs[:3]
            add_ref = None
            rest = refs[3:]
        else:
            a_ref, b_ref, add_ref, o_ref = refs[:4]
            rest = refs[4:]
        part = lax.dot_general(a_ref[...].astype(BF), b_ref[...].astype(BF), dims,
                               preferred_element_type=F32)

        def finish(r):
            if add_ref is not None:
                r = r + add_ref[...]
            o_ref[...] = r.astype(o_ref.dtype)

        if nk == 1:
            finish(part)
        else:
            acc = rest[0]
            k = pl.program_id(kax)

            @pl.when(k == 0)
            def _():
                acc[...] = part

            @pl.when(k > 0)
            def _():
                acc[...] += part

            @pl.when(k == nk - 1)
            def _():
                finish(acc[...])

    in_specs = [a_spec, b_spec]
    args = [a, b]
    if add is not None:
        in_specs.append(add_spec)
        args.append(add)
    out = _call(body, name=name, grid=grid, in_specs=in_specs, out_specs=[o_spec],
                out_shape=[jax.ShapeDtypeStruct(out_shape, out_dtype)], args=args,
                scratch=[] if nk == 1 else [pltpu.VMEM(acc_shape, F32)], host=host)
    return out[0] if host is None else out


def _pick(n, t):
    t = min(n, t)
    assert n % t == 0, (n, t)
    return t


def mm_nn(a, b, *, name, out_dtype, add=None, tm=1024, tn=1024, tk=2048, host=None):
    m = a.shape[0]
    kk, n = b.shape
    tm, tn, tk = _pick(m, tm), _pick(n, tn), _pick(kk, tk)
    return _mm(a, b, dims=NN, grid=(m // tm, n // tn, kk // tk),
               a_spec=pl.BlockSpec((tm, tk), lambda i, j, k: (i, k)),
               b_spec=pl.BlockSpec((tk, tn), lambda i, j, k: (k, j)),
               o_spec=pl.BlockSpec((tm, tn), lambda i, j, k: (i, j)),
               add=add, add_spec=pl.BlockSpec((tm, tn), lambda i, j, k: (i, j)),
               out_shape=(m, n), out_dtype=out_dtype, acc_shape=(tm, tn), name=name, host=host)


def mm_nt(a, b, *, name, out_dtype, b_col=0, add=None, tm=1024, tn=1024, tk=2048, host=None):
    m, kk = a.shape
    n = b.shape[0]
    tm, tn, tk = _pick(m, tm), _pick(n, tn), _pick(kk, tk)
    assert b_col % tk == 0
    ko = b_col // tk
    return _mm(a, b, dims=NT, grid=(m // tm, n // tn, kk // tk),
               a_spec=pl.BlockSpec((tm, tk), lambda i, j, k: (i, k)),
               b_spec=pl.BlockSpec((tn, tk), lambda i, j, k: (j, ko + k)),
               o_spec=pl.BlockSpec((tm, tn), lambda i, j, k: (i, j)),
               add=add, add_spec=pl.BlockSpec((tm, tn), lambda i, j, k: (i, j)),
               out_shape=(m, n), out_dtype=out_dtype, acc_shape=(tm, tn), name=name, host=host)


def mm_tn(a, b, *, name, tm=1024, tn=1024, tk=2048, host=None, by_column_block=False):
    s, m = a.shape
    n = b.shape[1]
    tm, tn, tk = _pick(m, tm), _pick(n, tn), _pick(s, tk)
    if by_column_block:
        out_shape, o_spec = (n // tn, m, tn), pl.BlockSpec((None, tm, tn), lambda i, j, k: (j, i, 0))
    else:
        out_shape, o_spec = (m, n), pl.BlockSpec((tm, tn), lambda i, j, k: (i, j))
    return _mm(a, b, dims=TN, grid=(m // tm, n // tn, s // tk),
               a_spec=pl.BlockSpec((tk, tm), lambda i, j, k: (k, i)),
               b_spec=pl.BlockSpec((tk, tn), lambda i, j, k: (k, j)),
               o_spec=o_spec, out_shape=out_shape, out_dtype=F32, acc_shape=(tm, tn), name=name, host=host)


def gmm_nt(a, w, *, name, tm=1024, host=None):
    s = a.shape[0]
    g, kk, n = w.shape
    tm = _pick(s, tm)
    return _mm(a, w, dims=NT, grid=(s // tm, g, 1),
               a_spec=pl.BlockSpec((tm, n), lambda i, gi, k: (i, gi)),
               b_spec=pl.BlockSpec((None, kk, n), lambda i, gi, k: (gi, 0, 0)),
               o_spec=pl.BlockSpec((tm, kk), lambda i, gi, k: (i, gi)),
               out_shape=(s, g * kk), out_dtype=F32, acc_shape=(tm, kk), name=name, host=host)


def gmm_tn(a, b, g, *, name, tk=2048):
    s = a.shape[0]
    kk, n = a.shape[1] // g, b.shape[1] // g
    tk = _pick(s, tk)
    return _mm(a, b, dims=TN, grid=(g, s // tk),
               a_spec=pl.BlockSpec((tk, kk), lambda gi, k: (k, gi)),
               b_spec=pl.BlockSpec((tk, n), lambda gi, k: (k, gi)),
               o_spec=pl.BlockSpec((None, kk, n), lambda gi, k: (gi, 0, 0)),
               out_shape=(g, kk, n), out_dtype=F32, acc_shape=(kk, n), name=name)


def _rms(xv, gv):
    inv = lax.rsqrt(jnp.mean(xv * xv, axis=-1, keepdims=True) + EPS)
    return (xv * inv) * gv


def _rms_bwd(xv, gv, dh):
    inv = lax.rsqrt(jnp.mean(xv * xv, axis=-1, keepdims=True) + EPS)
    xhat = xv * inv
    dxhat = dh * gv
    dx = inv * (dxhat - xhat * jnp.mean(dxhat * xhat, axis=-1, keepdims=True))
    return dx, jnp.sum(dh * xhat, axis=0, keepdims=True)


def norm_fwd(x, g, *, name, t=512):
    s, width = x.shape
    t = _pick(s, t)

    def body(x_ref, g_ref, o_ref):
        o_ref[...] = _rms(x_ref[...], g_ref[...]).astype(o_ref.dtype)

    row = pl.BlockSpec((t, width), lambda i: (i, 0))
    return pl.pallas_call(
        body, name=name, grid=(s // t,), in_specs=[row, pl.BlockSpec((1, width), lambda i: (0, 0))],
        out_specs=row, out_shape=jax.ShapeDtypeStruct((s, width), BF), compiler_params=_params(1))(x, g)


def _accumulate(ref, part):
    @pl.when(pl.program_id(0) == 0)
    def _():
        ref[...] = part

    @pl.when(pl.program_id(0) > 0)
    def _():
        ref[...] += part


def mm_nt_norm_bwd(a, b, other, x, g, res, *, name, b_col=0, tm=512, host=None):
    s, kk = a.shape
    d = b.shape[0]
    tm = _pick(s, tm)
    assert b_col % kk == 0
    pair = isinstance(other, tuple)

    def body(a_ref, b_ref, *refs):
        dh = lax.dot_general(a_ref[...].astype(BF), b_ref[...].astype(BF), NT, preferred_element_type=F32)
        if pair:
            a2_ref, b2_ref, x_ref, g_ref, res_ref, dx_ref, dg_ref = refs
            dh = dh + lax.dot_general(a2_ref[...].astype(BF), b2_ref[...].astype(BF), NT, preferred_element_type=F32)
        else:
            add_ref, x_ref, g_ref, res_ref, dx_ref, dg_ref = refs
            dh = dh + add_ref[...]
        dx, dg = _rms_bwd(x_ref[...], g_ref[...], dh)
        _accumulate(dg_ref, dg)
        dx_ref[...] = dx + res_ref[...]

    row = pl.BlockSpec((tm, d), lambda i: (i, 0))
    vec = pl.BlockSpec((1, d), lambda i: (0, 0))
    if pair:
        k2 = other[0].shape[1]
        other_specs = [pl.BlockSpec((tm, k2), lambda i: (i, 0)), pl.BlockSpec((d, k2), lambda i: (0, 0))]
        other_args = list(other)
    else:
        other_specs, other_args = [row], [other]
    return _call(
        body, name=name, grid=(s // tm,),
        in_specs=[pl.BlockSpec((tm, kk), lambda i: (i, 0)), pl.BlockSpec((d, kk), lambda i: (0, b_col // kk)),
                  *other_specs, row, vec, row],
        out_specs=[row, vec],
        out_shape=[jax.ShapeDtypeStruct((s, d), F32), jax.ShapeDtypeStruct((1, d), F32)],
        args=[a, b, *other_args, x, g, res], host=host)


def mm_nn_loss(a, b, add, gf, tgt, *, name, tm=512):
    s, kk = a.shape
    d = b.shape[1]
    tm = _pick(s, tm)

    def body(a_ref, b_ref, add_ref, g_ref, t_ref, dx_ref, dg_ref, loss_ref):
        xv = jnp.dot(a_ref[...].astype(BF), b_ref[...].astype(BF), preferred_element_type=F32) + add_ref[...]
        inv = lax.rsqrt(jnp.mean(xv * xv, axis=-1, keepdims=True) + EPS)
        xhat = xv * inv
        gv = g_ref[...]
        diff = xhat * gv - t_ref[...]
        row_err = jnp.mean(diff * diff, axis=-1, keepdims=True)
        _accumulate(loss_ref, jnp.broadcast_to(0.5 * jnp.sum(row_err, axis=0, keepdims=True), (1, 128)))
        dout = diff * (1.0 / d)
        _accumulate(dg_ref, jnp.sum(dout * xhat, axis=0, keepdims=True))
        dxhat = dout * gv
        dx_ref[...] = inv * (dxhat - xhat * jnp.mean(dxhat * xhat, axis=-1, keepdims=True))

    row = pl.BlockSpec((tm, d), lambda i: (i, 0))
    vec = pl.BlockSpec((1, d), lambda i: (0, 0))
    return _call(
        body, name=name, grid=(s // tm,),
        in_specs=[pl.BlockSpec((tm, kk), lambda i: (i, 0)), pl.BlockSpec((kk, d), lambda i: (0, 0)), row, vec, row],
        out_specs=[row, vec, pl.BlockSpec((1, 128), lambda i: (0, 0))],
        out_shape=[jax.ShapeDtypeStruct((s, d), F32), jax.ShapeDtypeStruct((1, d), F32),
                   jax.ShapeDtypeStruct((1, 128), F32)],
        args=[a, b, add, gf, tgt])


ROW_CHUNK = 56


def pool_prep(uz, *, name, t=256):
    s = uz.shape[0]
    t = _pick(s, t)
    hb = t // HALO

    lead = 2 * HALO
    live = t + lead - 8
    assert live % ROW_CHUNK == 0

    def body(u_ref, halo_ref, o_ref, buf_a, buf_b):
        i = pl.program_id(0)
        buf_a[pl.ds(lead, t), :] = u_ref[...]
        buf_a[pl.ds(0, HALO), :] = jnp.zeros((HALO, POOL_WIDTH), F32)
        buf_b[pl.ds(0, 8), :] = jnp.zeros((8, POOL_WIDTH), F32)

        @pl.when(i == 0)
        def _():
            buf_a[pl.ds(HALO, HALO), :] = jnp.zeros((HALO, POOL_WIDTH), F32)

        @pl.when(i > 0)
        def _():
            buf_a[pl.ds(HALO, HALO), :] = halo_ref[...]

        pos = i * t + lax.broadcasted_iota(jnp.int32, (t, POOL_GROUP), 0)
        for g, w in enumerate(POOL_WINDOWS):
            cols = pl.ds(g * POOL_GROUP, POOL_GROUP)
            src, dst, shift = buf_a, buf_b, 1
            while shift < w:
                for r0 in range(8, 8 + live, ROW_CHUNK):
                    dst[pl.ds(r0, ROW_CHUNK), cols] = (src[pl.ds(r0, ROW_CHUNK), cols]
                                                       + src[pl.ds(r0 - shift, ROW_CHUNK), cols])
                src, dst, shift = dst, src, 2 * shift
            cnt = jnp.minimum(pos + 1, w).astype(F32)
            o_ref[:, cols] = (src[pl.ds(lead, t), cols] / cnt - u_ref[:, cols]).astype(o_ref.dtype)

    return pl.pallas_call(
        body, name=name, grid=(s // t,),
        in_specs=[pl.BlockSpec((t, POOL_WIDTH), lambda i: (i, 0)),
                  pl.BlockSpec((HALO, POOL_WIDTH), lambda i: (jnp.maximum(i * hb - 1, 0), 0))],
        out_specs=pl.BlockSpec((t, POOL_WIDTH), lambda i: (i, 0)),
        out_shape=jax.ShapeDtypeStruct((s, POOL_WIDTH), BF),
        scratch_shapes=[pltpu.VMEM((t + lead, POOL_WIDTH), F32), pltpu.VMEM((t + lead, POOL_WIDTH), F32)],
        compiler_params=_params(1))(uz, uz)


def pool_prep_bwd(dpd, *, name, t=256):
    s = dpd.shape[0]
    t = _pick(s, t)
    hb = t // HALO
    n = s // t

    tail = 2 * HALO
    live = t + tail - 8
    assert live % ROW_CHUNK == 0

    def body(d_ref, halo_ref, o_ref, buf_a, buf_b):
        i = pl.program_id(0)
        buf_a[pl.ds(t + HALO, HALO), :] = jnp.zeros((HALO, POOL_WIDTH), F32)
        buf_b[pl.ds(live, 8), :] = jnp.zeros((8, POOL_WIDTH), F32)
        pos = i * t + lax.broadcasted_iota(jnp.int32, (t, POOL_GROUP), 0)
        for g, w in enumerate(POOL_WINDOWS):
            cols = pl.ds(g * POOL_GROUP, POOL_GROUP)
            cnt = jnp.minimum(pos + 1, w).astype(F32)
            buf_a[pl.ds(0, t), cols] = d_ref[:, cols] / cnt

            @pl.when(i < n - 1)
            def _():
                buf_a[pl.ds(t, HALO), cols] = halo_ref[:, cols] / float(w)

            @pl.when(i == n - 1)
            def _():
                buf_a[pl.ds(t, HALO), cols] = jnp.zeros((HALO, POOL_GROUP), F32)

        for g, w in enumerate(POOL_WINDOWS):
            cols = pl.ds(g * POOL_GROUP, POOL_GROUP)
            src, dst, shift = buf_a, buf_b, 1
            while shift < w:
                for r0 in range(0, live, ROW_CHUNK):
                    dst[pl.ds(r0, ROW_CHUNK), cols] = (src[pl.ds(r0, ROW_CHUNK), cols]
                                                       + src[pl.ds(r0 + shift, ROW_CHUNK), cols])
                src, dst, shift = dst, src, 2 * shift
            o_ref[:, cols] = (src[pl.ds(0, t), cols] - d_ref[:, cols]).astype(o_ref.dtype)

    return pl.pallas_call(
        body, name=name, grid=(n,),
        in_specs=[pl.BlockSpec((t, POOL_WIDTH), lambda i: (i, 0)),
                  pl.BlockSpec((HALO, POOL_WIDTH), lambda i: (jnp.minimum((i + 1) * hb, n * hb - 1), 0))],
        out_specs=pl.BlockSpec((t, POOL_WIDTH), lambda i: (i, 0)),
        out_shape=jax.ShapeDtypeStruct((s, POOL_WIDTH), BF),
        scratch_shapes=[pltpu.VMEM((t + tail, POOL_WIDTH), F32), pltpu.VMEM((t + tail, POOL_WIDTH), F32)],
        compiler_params=_params(1))(dpd, dpd)


CHUNK = 512


def _chunks(width, step=CHUNK):
    return [slice(c, c + step) for c in range(0, width, step)]


def pool_mix_gate(pd, wg, uz, scale, *, name, tm=1024):
    s = pd.shape[0]
    g = wg.shape[0]
    tm = _pick(s, tm)

    def body(a_ref, w_ref, z_ref, sc_ref, y_ref):
        mm = jnp.dot(a_ref[...], w_ref[...], preferred_element_type=F32)
        z = z_ref[...]
        y_ref[...] = ((mm * sc_ref[...]) * (z * _sigmoid(z))).astype(y_ref.dtype)

    blk = pl.BlockSpec((tm, POOL_GROUP), lambda i, gi: (i, gi))
    return pl.pallas_call(
        body, name=name, grid=(s // tm, g),
        in_specs=[blk, pl.BlockSpec((None, POOL_GROUP, POOL_GROUP), lambda i, gi: (gi, 0, 0)),
                  pl.BlockSpec((tm, POOL_GROUP), lambda i, gi: (i, g + gi)),
                  pl.BlockSpec((1, POOL_GROUP), lambda i, gi: (0, gi))],
        out_specs=blk, out_shape=jax.ShapeDtypeStruct((s, POOL_WIDTH), BF),
        compiler_params=_params(2))(pd, wg, uz, scale)


def pool_out_dx_gate(dx, w_out, pd, wg, uz, scale, *, name, tm=512, host=None):
    s, d = dx.shape
    tm = _pick(s, tm)
    assert CHUNK == POOL_GROUP

    def body(dx_ref, w_ref, pd_ref, wg_ref, z_ref, sc_ref, dmm_ref, dz_ref, dsc_ref):
        dxv = dx_ref[...].astype(BF)
        parts = []
        for g, c in enumerate(_chunks(POOL_WIDTH)):
            dyv = lax.dot_general(dxv, w_ref[c, :], NT, preferred_element_type=F32)
            z = z_ref[:, c]
            sig = _sigmoid(z)
            mmv = jnp.dot(pd_ref[:, c], wg_ref[g], preferred_element_type=F32)
            scv = sc_ref[:, c]
            dmixed = dyv * (z * sig)
            dmm_ref[:, c] = (dmixed * scv).astype(dmm_ref.dtype)
            dz_ref[:, c] = (dyv * (mmv * scv) * (sig * (1.0 + z * (1.0 - sig)))).astype(dz_ref.dtype)
            parts.append(jnp.sum(dmixed * mmv, axis=0, keepdims=True))

        @pl.when(pl.program_id(0) == 0)
        def _():
            for c, part in zip(_chunks(POOL_WIDTH), parts):
                dsc_ref[:, c] = part

        @pl.when(pl.program_id(0) > 0)
        def _():
            for c, part in zip(_chunks(POOL_WIDTH), parts):
                dsc_ref[:, c] += part

    blk = pl.BlockSpec((tm, POOL_WIDTH), lambda i: (i, 0))
    vec = pl.BlockSpec((1, POOL_WIDTH), lambda i: (0, 0))
    return _call(
        body, name=name, grid=(s // tm,),
        in_specs=[pl.BlockSpec((tm, d), lambda i: (i, 0)), pl.BlockSpec((POOL_WIDTH, d), lambda i: (0, 0)),
                  blk, pl.BlockSpec(wg.shape, lambda i: (0, 0, 0)),
                  pl.BlockSpec((tm, POOL_WIDTH), lambda i: (i, 1)), vec],
        out_specs=[blk, blk, vec],
        out_shape=[jax.ShapeDtypeStruct((s, POOL_WIDTH), BF), jax.ShapeDtypeStruct((s, POOL_WIDTH), BF),
                   jax.ShapeDtypeStruct((1, POOL_WIDTH), F32)],
        args=[dx, w_out, pd, wg, uz, scale], host=host)


def _rope(a, cc, sa, sb):
    return a * cc + pltpu.roll(a, 96, 1) * sa + pltpu.roll(a, 32, 1) * sb


def _unrope(d, cc, sa, sb):
    return d * cc + pltpu.roll(d * sa, 32, 1) + pltpu.roll(d * sb, 96, 1)


def q_proj_rope(qn, wq, cc, sa, sb, *, name, tm=1024, heads=8):
    s, kk = qn.shape
    tm = _pick(s, tm)
    tn = heads * HEAD_PAD

    def body(a_ref, b_ref, cc_ref, sa_ref, sb_ref, o_ref):
        q = jnp.dot(a_ref[...], b_ref[...], preferred_element_type=F32)
        for h in range(heads):
            nope = slice(h * HEAD_PAD, h * HEAD_PAD + QK_NOPE)
            rope = slice(h * HEAD_PAD + QK_NOPE, (h + 1) * HEAD_PAD)
            o_ref[:, nope] = q[:, nope].astype(o_ref.dtype)
            o_ref[:, rope] = _rope(q[:, rope], cc_ref[...], sa_ref[...], sb_ref[...]).astype(o_ref.dtype)

    tab = pl.BlockSpec((tm, 128), lambda i, j: (i, 0))
    return pl.pallas_call(
        body, name=name, grid=(s // tm, N_HEADS // heads),
        in_specs=[pl.BlockSpec((tm, kk), lambda i, j: (i, 0)), pl.BlockSpec((kk, tn), lambda i, j: (0, j)),
                  tab, tab, tab],
        out_specs=pl.BlockSpec((tm, tn), lambda i, j: (i, j)),
        out_shape=jax.ShapeDtypeStruct((s, N_HEADS * HEAD_PAD), BF), compiler_params=_params(2))(qn, wq, cc, sa, sb)


LAT_KV = slice(P_KV, P_KV + KV_LORA)
LAT_KR = slice(P_KR, P_KR + 128)
LAT_Q = slice(P_Q, P_Q + Q_LORA)


def latent_fwd(proj, g_q, g_kv, cc, sa, sb, *, name, t=512):
    s = proj.shape[0]
    t = _pick(s, t)

    def body(p_ref, gq_ref, gkv_ref, cc_ref, sa_ref, sb_ref, qn_ref, kvn_ref, kr_ref):
        qn_ref[...] = _rms(p_ref[:, LAT_Q], gq_ref[...]).astype(qn_ref.dtype)
        kvn_ref[...] = _rms(p_ref[:, LAT_KV], gkv_ref[...]).astype(kvn_ref.dtype)
        kr_ref[...] = _rope(p_ref[:, LAT_KR], cc_ref[...], sa_ref[...], sb_ref[...]).astype(kr_ref.dtype)

    tab = pl.BlockSpec((t, 128), lambda i: (i, 0))
    return pl.pallas_call(
        body, name=name, grid=(s // t,),
        in_specs=[pl.BlockSpec((t, P_SMALL), lambda i: (i, 0)), pl.BlockSpec((1, Q_LORA), lambda i: (0, 0)),
                  pl.BlockSpec((1, KV_LORA), lambda i: (0, 0)), tab, tab, tab],
        out_specs=[pl.BlockSpec((t, Q_LORA), lambda i: (i, 0)), pl.BlockSpec((t, KV_LORA), lambda i: (i, 0)), tab],
        out_shape=[jax.ShapeDtypeStruct((s, Q_LORA), BF), jax.ShapeDtypeStruct((s, KV_LORA), BF),
                   jax.ShapeDtypeStruct((s, 128), BF)],
        compiler_params=_params(1))(proj, g_q, g_kv, cc, sa, sb)


def latent_bwd(proj, g_q, g_kv, dqn, dkvn, dkr, cc, sa, sb, *, name, t=512):
    s = proj.shape[0]
    t = _pick(s, t)

    def body(p_ref, gq_ref, gkv_ref, dqn_ref, dkvn_ref, dkr_ref, cc_ref, sa_ref, sb_ref, d_ref, dgq_ref, dgkv_ref):
        dq, dgq = _rms_bwd(p_ref[:, LAT_Q], gq_ref[...], dqn_ref[...])
        dkv, dgkv = _rms_bwd(p_ref[:, LAT_KV], gkv_ref[...], dkvn_ref[...])
        d_ref[:, LAT_Q] = dq.astype(d_ref.dtype)
        d_ref[:, LAT_KV] = dkv.astype(d_ref.dtype)
        d_ref[:, LAT_KR] = _unrope(dkr_ref[...], cc_ref[...], sa_ref[...], sb_ref[...]).astype(d_ref.dtype)
        _accumulate(dgq_ref, dgq)
        _accumulate(dgkv_ref, dgkv)

    tab = pl.BlockSpec((t, 128), lambda i: (i, 0))
    small = pl.BlockSpec((t, P_SMALL), lambda i: (i, 0))
    gq = pl.BlockSpec((1, Q_LORA), lambda i: (0, 0))
    gkv = pl.BlockSpec((1, KV_LORA), lambda i: (0, 0))
    return pl.pallas_call(
        body, name=name, grid=(s // t,),
        in_specs=[small, gq, gkv, pl.BlockSpec((t, Q_LORA), lambda i: (i, 0)),
                  pl.BlockSpec((t, KV_LORA), lambda i: (i, 0)), tab, tab, tab, tab],
        out_specs=[small, gq, gkv],
        out_shape=[jax.ShapeDtypeStruct((s, P_SMALL), BF), jax.ShapeDtypeStruct((1, Q_LORA), F32),
                   jax.ShapeDtypeStruct((1, KV_LORA), F32)],
        compiler_params=_params(1))(proj, g_q, g_kv, dqn, dkvn, dkr, cc, sa, sb)


def mla_out_dx_gate(dx, w_out, o, proj, *, name, tq):
    s, d = dx.shape
    nq = s // tq

    def body(dx_ref, w_ref, o_ref, p_ref, do_ref, dz_ref, dl_ref):
        dxv = dx_ref[...].astype(BF)
        lane = lax.broadcasted_iota(jnp.int32, (tq, 128), 1)
        deltas = jnp.zeros((tq, 128), F32)
        for c in _chunks(MLA_WIDTH):
            dy_c = lax.dot_general(dxv, w_ref[c, :], NT, preferred_element_type=F32)
            for h in range(c.start // V_DIM, c.stop // V_DIM):
                hc = slice(h * V_DIM, (h + 1) * V_DIM)
                z = p_ref[:, slice(P_Z + hc.start, P_Z + hc.stop)]
                sig = _sigmoid(z)
                dyv = dy_c[:, hc.start - c.start:hc.stop - c.start]
                ov = o_ref[:, hc]
                dov = dyv * (z * sig)
                do_ref[:, hc] = dov.astype(do_ref.dtype)
                dz_ref[:, hc] = (dyv * ov * (sig * (1.0 + z * (1.0 - sig)))).astype(dz_ref.dtype)
                deltas = jnp.where(lane == h, jnp.sum(dov * ov, axis=-1, keepdims=True), deltas)
        rows = deltas.T
        for h in range(N_HEADS):
            dl_ref[h] = jnp.broadcast_to(rows[h:h + 1, :], (8, tq))

    blk = pl.BlockSpec((tq, MLA_WIDTH), lambda i: (i, 0))
    return pl.pallas_call(
        body, name=name, grid=(nq,),
        in_specs=[pl.BlockSpec((tq, d), lambda i: (i, 0)), pl.BlockSpec((MLA_WIDTH, d), lambda i: (0, 0)),
                  blk, pl.BlockSpec((tq, P_WIDTH), lambda i: (i, 0))],
        out_specs=[blk, blk, pl.BlockSpec((N_HEADS, None, 8, tq), lambda i: (0, i, 0, 0))],
        out_shape=[jax.ShapeDtypeStruct((s, MLA_WIDTH), BF), jax.ShapeDtypeStruct((s, MLA_WIDTH), BF),
                   jax.ShapeDtypeStruct((N_HEADS, nq, 8, tq), F32)],
        compiler_params=_params(1))(dx, w_out, o, proj)


FWD_GROUPS = (4, 3, 2, 1)
BWD_GROUPS = (4, 3, 2, 1)


def _for_groups(first, count, groups, fn):
    lead = groups[-1]
    for g in groups[:-1][::-1]:
        lead = jnp.where(count >= g, g, lead)
    for g in groups:
        @pl.when(lead == g)
        def _(g=g):
            fn(first, g, True)
    first = first + lead
    count = count - lead
    for g in groups:
        n = count // g

        def one(p, carry, g=g, first=first):
            fn(first + p * g, g, False)
            return carry

        lax.fori_loop(0, n, one, 0)
        first = first + n * g
        count = count - n * g


def attn_fwd(qr, kv, krr, proj, *, name, tq):
    s = qr.shape[0]
    nq = s // tq
    z_blk = P_Z // V_DIM

    def body(kn_ref, v_ref, kr_ref, q_ref, z_ref, o_ref, y_ref, lse_ref, acc_sc, m_sc):
        j = pl.program_id(1)

        @pl.when(j == 0)
        def _():
            acc_sc[...] = jnp.zeros((nq, 2 * V_DIM, tq), F32)
            m_sc[...] = jnp.full((nq, 8, tq), NEG, F32)

        k = jnp.concatenate([kn_ref[...], kr_ref[...]], axis=1)
        vxt = jnp.concatenate([v_ref[...].astype(F32).T.astype(BF), jnp.ones((V_DIM, tq), BF)], axis=0)

        def update(i, n_tiles, masked):
            rows = pl.ds(pl.multiple_of(i * tq, tq), n_tiles * tq)
            st = lax.dot_general(k, q_ref[rows, :], NT, preferred_element_type=F32) * SCALE_LOG2E
            if masked:
                krow = lax.broadcasted_iota(jnp.int32, (tq, n_tiles * tq), 0)
                qcol = lax.broadcasted_iota(jnp.int32, (tq, n_tiles * tq), 1)
                st = jnp.where(qcol >= krow, st, NEG)
            m_prev = jnp.concatenate([m_sc[i + n, pl.ds(0, 1), :] for n in range(n_tiles)], axis=1)
            m_new = jnp.maximum(m_prev, jnp.max(st, axis=0, keepdims=True))
            alpha = jnp.exp2(m_prev - m_new)
            pt = jnp.exp2(st - m_new).astype(BF)
            pv_t = jnp.dot(vxt, pt, preferred_element_type=F32)
            for n in range(n_tiles):
                cols = slice(n * tq, (n + 1) * tq)
                acc_sc[i + n] = alpha[:, cols] * acc_sc[i + n] + pv_t[:, cols]
                m_sc[i + n, pl.ds(0, 1), :] = m_new[:, cols]

        _for_groups(j, nq - j, FWD_GROUPS, update)
        l = acc_sc[j, V_DIM:, :]
        o = (acc_sc[j, :V_DIM, :] / l).T
        o_ref[...] = o
        z = z_ref[...]
        y_ref[...] = (o * (z * _sigmoid(z))).astype(y_ref.dtype)
        lse_ref[...] = m_sc[j, pl.ds(0, 1), :] + jnp.log2(l[:8, :])

    tile = pl.BlockSpec((tq, V_DIM), lambda h, j: (j, h))
    return pl.pallas_call(
        body, name=name, grid=(N_HEADS, nq),
        in_specs=[pl.BlockSpec((tq, QK_NOPE), lambda h, j: (j, 2 * h)),
                  pl.BlockSpec((tq, V_DIM), lambda h, j: (j, 2 * h + 1)),
                  pl.BlockSpec((tq, 128), lambda h, j: (j, 0)),
                  pl.BlockSpec((s, HEAD_PAD), lambda h, j: (0, h)),
                  pl.BlockSpec((tq, V_DIM), lambda h, j: (j, z_blk + h))],
        out_specs=[tile, tile, pl.BlockSpec((None, None, 8, tq), lambda h, j: (h, j, 0, 0))],
        out_shape=[jax.ShapeDtypeStruct((s, N_HEADS * V_DIM), F32),
                   jax.ShapeDtypeStruct((s, N_HEADS * V_DIM), BF),
                   jax.ShapeDtypeStruct((N_HEADS, nq, 8, tq), F32)],
        scratch_shapes=[pltpu.VMEM((nq, 2 * V_DIM, tq), F32), pltpu.VMEM((nq, 8, tq), F32)],
        compiler_params=_params(2))(kv, kv, krr, qr, proj)


def attn_bwd(qr, kv, krr, do, lse, delta, cc, sa, sb, *, name, tq):
    s = qr.shape[0]
    nq = s // tq

    def body(kn_ref, v_ref, kr_ref, q_ref, do_ref, lse_ref, dl_ref, cc_ref, sa_ref, sb_ref,
             dkv_ref, dkr_ref, dq_ref, dq_sc, dk_sc, dv_sc):
        h = pl.program_id(0)
        j = pl.program_id(1)

        @pl.when(j == 0)
        def _():
            dq_sc[...] = jnp.zeros((s, HEAD_PAD), F32)

        dk_sc[...] = jnp.zeros((tq, HEAD_PAD), F32)
        dv_sc[...] = jnp.zeros((tq, V_DIM), F32)
        k = jnp.concatenate([kn_ref[...], kr_ref[...]], axis=1)
        v = v_ref[...]

        def step(i, n_tiles, masked):
            r0 = pl.multiple_of(i * tq, tq)
            rows = pl.ds(r0, n_tiles * tq)
            q = q_ref[rows, :]
            dov = do_ref[rows, :]
            lse_row = jnp.concatenate([lse_ref[i + n, pl.ds(0, 1), :] for n in range(n_tiles)], axis=1)
            dl_row = jnp.concatenate([dl_ref[i + n, pl.ds(0, 1), :] for n in range(n_tiles)], axis=1)
            st = lax.dot_general(k, q, NT, preferred_element_type=F32) * SCALE_LOG2E
            if masked:
                krow = lax.broadcasted_iota(jnp.int32, (tq, n_tiles * tq), 0)
                qcol = lax.broadcasted_iota(jnp.int32, (tq, n_tiles * tq), 1)
                st = jnp.where(qcol >= krow, st, NEG)
            pt = jnp.exp2(st - lse_row)
            dpt = lax.dot_general(v, dov, NT, preferred_element_type=F32)
            dst = (pt * (dpt - dl_row)).astype(BF)
            dv_sc[...] += jnp.dot(pt.astype(BF), dov, preferred_element_type=F32)
            dk_sc[...] += jnp.dot(dst, q, preferred_element_type=F32)
            dq_sc[rows, :] += lax.dot_general(dst, k, TN, preferred_element_type=F32)

        _for_groups(j, nq - j, BWD_GROUPS, step)
        dkv_ref[:, :QK_NOPE] = (dk_sc[:, :QK_NOPE] * SCALE).astype(dkv_ref.dtype)
        dkv_ref[:, QK_NOPE:] = dv_sc[...].astype(dkv_ref.dtype)
        mine = pl.ds(pl.multiple_of(j * tq, tq), tq)
        dkr = dk_sc[:, QK_NOPE:] * SCALE

        @pl.when(h == 0)
        def _():
            dkr_ref[mine, :] = dkr

        @pl.when(h > 0)
        def _():
            dkr_ref[mine, :] += dkr

        dq_ref[:, :QK_NOPE] = (dq_sc[mine, :QK_NOPE] * SCALE).astype(dq_ref.dtype)
        dq_ref[:, QK_NOPE:] = _unrope(dq_sc[mine, QK_NOPE:] * SCALE, cc_ref[...], sa_ref[...],
                                      sb_ref[...]).astype(dq_ref.dtype)

    rows = pl.BlockSpec((None, nq, 8, tq), lambda h, j: (h, 0, 0, 0))
    tab = pl.BlockSpec((tq, 128), lambda h, j: (j, 0))
    return pl.pallas_call(
        body, name=name, grid=(N_HEADS, nq),
        in_specs=[pl.BlockSpec((tq, QK_NOPE), lambda h, j: (j, 2 * h)),
                  pl.BlockSpec((tq, V_DIM), lambda h, j: (j, 2 * h + 1)), tab,
                  pl.BlockSpec((s, HEAD_PAD), lambda h, j: (0, h)),
                  pl.BlockSpec((s, V_DIM), lambda h, j: (0, h)), rows, rows, tab, tab, tab],
        out_specs=[pl.BlockSpec((tq, 256), lambda h, j: (j, h)),
                   pl.BlockSpec((s, 128), lambda h, j: (0, 0)),
                   pl.BlockSpec((tq, HEAD_PAD), lambda h, j: (j, h))],
        out_shape=[jax.ShapeDtypeStruct((s, N_HEADS * 256), BF),
                   jax.ShapeDtypeStruct((s, 128), F32),
                   jax.ShapeDtypeStruct((s, N_HEADS * HEAD_PAD), BF)],
        scratch_shapes=[pltpu.VMEM((s, HEAD_PAD), F32), pltpu.VMEM((tq, HEAD_PAD), F32),
                        pltpu.VMEM((tq, V_DIM), F32)],
        compiler_params=_params(2))(kv, kv, krr, qr, do, lse, delta, cc, sa, sb)


def adamw(w, g, m, v, *, name, t=256):
    r, c = w.shape
    t = r if r % t else t
    c1 = 1.0 - ADAM_B1 ** ADAM_STEP
    c2 = 1.0 - ADAM_B2 ** ADAM_STEP

    def body(w_ref, g_ref, m_ref, v_ref, d_ref, nm_ref, nv_ref):
        gv = g_ref[...]
        nm = ADAM_B1 * m_ref[...] + (1.0 - ADAM_B1) * gv
        nv = ADAM_B2 * v_ref[...] + (1.0 - ADAM_B2) * (gv * gv)
        nm_ref[...] = nm
        nv_ref[...] = nv
        d_ref[...] = -ADAM_LR * ((nm / c1) / (jnp.sqrt(nv / c2) + ADAM_EPS) + ADAM_WD * w_ref[...])

    blk = pl.BlockSpec((t, c), lambda i: (i, 0))
    return pl.pallas_call(
        body, name=name, grid=(r // t,), in_specs=[blk] * 4, out_specs=[blk] * 3,
        out_shape=[jax.ShapeDtypeStruct((r, c), F32)] * 3, compiler_params=_params(1))(w, g, m, v)


def sum_devices(parts, *, name):
    def body(p_ref, o_ref):
        acc = p_ref[pl.ds(0, SV_ROWS), :]
        for d in range(1, 8):
            acc = acc + p_ref[pl.ds(d * SV_ROWS, SV_ROWS), :]
        o_ref[...] = acc

    return pl.pallas_call(body, name=name, out_shape=jax.ShapeDtypeStruct((SV_ROWS, SV_COLS), F32))(parts)


def add_halves(g, rb, c_idx, *, name, rows):
    nq, r2, cc = rb.shape
    nb = r2 // rows

    def body(c_ref, g_ref, r_ref, o_ref):
        o_ref[...] = (g_ref[...] + r_ref[...]).astype(o_ref.dtype)

    grid_spec = pltpu.PrefetchScalarGridSpec(
        num_scalar_prefetch=1, grid=(nq, nb),
        in_specs=[pl.BlockSpec((None, rows, cc), lambda q, i, c: (q, c[0] * nb + i, 0)),
                  pl.BlockSpec((None, rows, cc), lambda q, i, c: (q, i, 0))],
        out_specs=pl.BlockSpec((None, rows, cc), lambda q, i, c: (q, i, 0)))
    return pl.pallas_call(body, name=name, grid_spec=grid_spec,
                          out_shape=jax.ShapeDtypeStruct((nq, r2, cc), BF),
                          compiler_params=_params(2))(c_idx, g, rb)


def sum_chips(rc, c_idx, *, name, rows):
    nq, r2, cc = rc.shape
    nb = r2 // rows

    def body(c_ref, r_ref, o_ref):
        parts = [r_ref[q].astype(F32) for q in range(4)]
        o_ref[...] = ((parts[0] + parts[1]) + parts[2]) + parts[3]

    grid_spec = pltpu.PrefetchScalarGridSpec(
        num_scalar_prefetch=1, grid=(nb,),
        in_specs=[pl.BlockSpec((nq, rows, cc), lambda i, c: (0, i, 0))],
        out_specs=pl.BlockSpec((rows, cc), lambda i, c: (c[0] * nb + i, 0)))
    return pl.pallas_call(body, name=name, grid_spec=grid_spec,
                          out_shape=jax.ShapeDtypeStruct((2 * r2, cc), F32),
                          compiler_params=_params(1))(c_idx, rc)


def _place():
    return lax.axis_index("x"), lax.axis_index("y"), lax.axis_index("c")


def all_gather8(xs, *, name, own_half):
    m = xs.shape[0] // 2 if own_half else xs.shape[0]
    n = xs.shape[1]

    def body(x_ref, out_ref, send_sems, recv_sems, local_sem):
        x, y, c = _place()
        me, sibling = (x, y, c), (x, y, 1 - c)
        chips = [(1 - x, y), (x, 1 - y), (1 - x, 1 - y)]
        src_own = x_ref.at[pl.ds(c * m, m), :] if own_half else x_ref

        def rows(px, py, pc):
            return out_ref.at[pl.ds((4 * px + 2 * py + pc) * m, m), :]

        def copy(k, block, to, src=None):
            return pltpu.make_async_remote_copy(
                src_ref=rows(*block) if src is None else src, dst_ref=rows(*block),
                send_sem=send_sems.at[k], recv_sem=recv_sems.at[k], device_id=to, device_id_type=MESH)

        mine = pltpu.make_async_copy(src_own, rows(*me), local_sem)
        mine.start()
        first = [copy(0, me, sibling, src=src_own)]
        first += [copy(1 + j, me, (*chip, c), src=src_own) for j, chip in enumerate(chips)]
        for cp in first:
            cp.start()
        passed = [copy(4 + j, (*chip, c), sibling) for j, chip in enumerate(chips)]
        for j, chip in enumerate(chips):
            copy(1 + j, (*chip, c), me).wait_recv()
            passed[j].start()
        copy(0, sibling, me).wait_recv()
        for j, chip in enumerate(chips):
            copy(4 + j, (*chip, 1 - c), me).wait_recv()
        for cp in first + passed:
            cp.wait_send()
        mine.wait()

    return pl.pallas_call(
        body, name=name, out_shape=jax.ShapeDtypeStruct((8 * m, n), xs.dtype),
        in_specs=[pl.BlockSpec(memory_space=pl.ANY)], out_specs=pl.BlockSpec(memory_space=pl.ANY),
        scratch_shapes=[pltpu.SemaphoreType.DMA((7,)), pltpu.SemaphoreType.DMA((7,)), pltpu.SemaphoreType.DMA],
    )(xs)


def _other_chips():
    x, y, c = _place()
    return [(1 - x, y), (x, 1 - y), (1 - x, 1 - y)]


def _remote(src, dst, send_sems, recv_sems, k, to):
    return pltpu.make_async_remote_copy(src_ref=src, dst_ref=dst, send_sem=send_sems.at[k], recv_sem=recv_sems.at[k],
                                        device_id=to, device_id_type=MESH)


def gather_ici(xs):
    r, cc = xs.shape
    m = r // 2

    def copies(ins, outs, ss, rs, landing):
        x, y, c = _place()
        half = pl.ds(c * m, m)
        return [_remote(ins[0].at[half, :], outs[0].at[(2 * cx + cy) if landing else (2 * x + y), half, :],
                        ss, rs, j, (cx, cy, c)) for j, (cx, cy) in enumerate(_other_chips())]

    def start(ins, outs, ss, rs, ls):
        for cp in copies(ins, outs, ss, rs, False):
            cp.start()

    def wait(ins, outs, ss, rs, ls):
        for cp in copies(ins, outs, ss, rs, True):
            cp.wait_recv()
        for cp in copies(ins, outs, ss, rs, False):
            cp.wait_send()

    return Exchange((xs,), (jax.ShapeDtypeStruct((4, r, cc), xs.dtype),), {}, 3, start, wait)


def gather_forward(buf):
    m = buf.shape[1] // 2

    def copies(outs, ss, rs, landing):
        x, y, c = _place()
        half = pl.ds(((1 - c) if landing else c) * m, m)
        return [_remote(outs[0].at[2 * cx + cy, half, :], outs[0].at[2 * cx + cy, half, :], ss, rs, j, (x, y, 1 - c))
                for j, (cx, cy) in enumerate(_other_chips())]

    def start(ins, outs, ss, rs, ls):
        for cp in copies(outs, ss, rs, False):
            cp.start()

    def wait(ins, outs, ss, rs, ls):
        for cp in copies(outs, ss, rs, True):
            cp.wait_recv()
        for cp in copies(outs, ss, rs, False):
            cp.wait_send()

    return Exchange((buf,), (jax.ShapeDtypeStruct(buf.shape, buf.dtype),), {0: 0}, 3, start, wait)


def swap_halves(g):
    nq, r, cc = g.shape
    r2 = r // 2

    def copy(ins, outs, ss, rs):
        x, y, c = _place()
        return _remote(ins[0].at[:, pl.ds((1 - c) * r2, r2), :], outs[0], ss, rs, 0, (x, y, 1 - c))

    def start(ins, outs, ss, rs, ls):
        copy(ins, outs, ss, rs).start()

    def wait(ins, outs, ss, rs, ls):
        copy(ins, outs, ss, rs).wait()

    return Exchange((g,), (jax.ShapeDtypeStruct((nq, r2, cc), g.dtype),), {}, 1, start, wait)


def exchange_chips(p):
    def own(ins, outs, ls):
        x, y, c = _place()
        return pltpu.make_async_copy(ins[0].at[2 * x + y], outs[0].at[2 * x + y], ls)

    def copies(ins, outs, ss, rs, landing):
        x, y, c = _place()
        return [_remote(ins[0].at[2 * cx + cy], outs[0].at[(2 * cx + cy) if landing else (2 * x + y)],
                        ss, rs, j, (cx, cy, c)) for j, (cx, cy) in enumerate(_other_chips())]

    def start(ins, outs, ss, rs, ls):
        own(ins, outs, ls).start()
        for cp in copies(ins, outs, ss, rs, False):
            cp.start()

    def wait(ins, outs, ss, rs, ls):
        for cp in copies(ins, outs, ss, rs, True):
            cp.wait_recv()
        for cp in copies(ins, outs, ss, rs, False):
            cp.wait_send()
        own(ins, outs, ls).wait()

    return Exchange((p,), (jax.ShapeDtypeStruct(p.shape, p.dtype),), {}, 3, start, wait)


def join_halves(tot):
    r2 = tot.shape[0] // 2

    def copy(outs, ss, rs, landing):
        x, y, c = _place()
        half = outs[0].at[pl.ds(((1 - c) if landing else c) * r2, r2), :]
        return _remote(half, half, ss, rs, 0, (x, y, 1 - c))

    def start(ins, outs, ss, rs, ls):
        copy(outs, ss, rs, False).start()

    def wait(ins, outs, ss, rs, ls):
        copy(outs, ss, rs, True).wait_recv()
        copy(outs, ss, rs, False).wait_send()

    return Exchange((tot,), (jax.ShapeDtypeStruct(tot.shape, tot.dtype),), {0: 0}, 1, start, wait)


def _pack_shard(blocks, small_vec=None):
    parts = [w.reshape(-1, PACK_C).astype(BF) for w in blocks]
    if small_vec is not None:
        srow = lax.bitcast_convert_type(small_vec, BF).reshape(1, PACK_C)
        parts.append(jnp.pad(srow, ((0, PACK_PAD - 1), (0, 0))))
    return jnp.concatenate(parts, axis=0)


def _split_rows(a, rows, axis):
    out, off = [], 0
    for n in rows:
        out.append(lax.slice_in_dim(a, off, off + n, axis=axis))
        off += n
    return out


def _unpack_pool(gw):
    p_in, p_grp, p_out = _split_rows(gw, POOL_ROWS, 1)
    return dict(
        pool_w_in=p_in.reshape(4, D_MODEL, 1024).transpose(1, 0, 2).reshape(D_MODEL, 2 * POOL_WIDTH),
        pool_w_group=p_grp.reshape(4, 4, 128, POOL_GROUP).transpose(1, 0, 2, 3).reshape(4, POOL_GROUP, POOL_GROUP),
        pool_w_out=p_out.reshape(POOL_WIDTH, D_MODEL))


def _unpack_mla(gw):
    m_in, m_qb, m_kvb, m_out, small = _split_rows(gw, MLA_ROWS + (PACK_PAD,), 1)
    w = {}
    win = m_in.reshape(4, D_MODEL, 688).transpose(1, 0, 2).reshape(D_MODEL, 2752)
    w["mla_w_in"] = jnp.concatenate(
        [win[:, 384:640], win[:, 640:704], jnp.zeros((D_MODEL, 64), BF), win[:, 0:384], win[:, 704:]], axis=1)
    wq = m_qb.reshape(4, Q_LORA, 768).transpose(1, 0, 2).reshape(Q_LORA, N_HEADS, QK_NOPE + QK_ROPE)
    w["mla_w_q_b"] = jnp.pad(wq, ((0, 0), (0, 0), (0, HEAD_PAD - QK_NOPE - QK_ROPE))).reshape(Q_LORA, N_HEADS * HEAD_PAD)
    w["mla_w_kv_b"] = m_kvb.reshape(4, KV_LORA, 1024).transpose(1, 0, 2).reshape(KV_LORA, 4096)
    w["mla_w_out"] = m_out.reshape(MLA_WIDTH, D_MODEL)
    small = lax.bitcast_convert_type(small[:, 0, :].reshape(4, 512, 2), F32)
    w["mla_norm"] = small[:, :256].reshape(1, D_MODEL)
    w["mla_q_norm"] = small[:, 256:352].reshape(1, Q_LORA)
    w["mla_kv_norm"] = small[:, 352:416].reshape(1, KV_LORA)
    return w


def _pack_pool_grads(g):
    return jnp.concatenate([
        g["pool_w_in"],
        g["pool_w_group"].reshape(4, 4, 128, POOL_GROUP).transpose(1, 0, 2, 3).reshape(4, 256, PACK_C),
        g["pool_w_out"].reshape(4, 512, PACK_C)], axis=1)


def _pack_mla_grads(g):
    return jnp.concatenate([
        g["mla_w_in"].reshape(D_MODEL, 4, 688).transpose(1, 0, 2).reshape(4, 688, PACK_C),
        g["mla_w_q_b"].reshape(Q_LORA, 4, 768).transpose(1, 0, 2).reshape(4, 288, PACK_C),
        g["mla_w_kv_b"],
        g["mla_w_out"].reshape(4, 512, PACK_C),
        jnp.zeros((4, PACK_PAD, PACK_C), F32)], axis=1)


def kernel(x, positions, pool_norm, pool_w_in, pool_w_group, pool_scale, pool_w_out, mla_norm, mla_w_in, mla_q_norm, mla_w_q_b, mla_kv_norm, mla_w_kv_b, mla_w_out, final_norm, loss_target, m_pool_norm, m_pool_w_in, m_pool_w_group, m_pool_scale, m_pool_w_out, m_mla_norm, m_mla_w_in, m_mla_q_norm, m_mla_w_q_b, m_mla_kv_norm, m_mla_w_kv_b, m_mla_w_out, m_final_norm, v_pool_norm, v_pool_w_in, v_pool_w_group, v_pool_scale, v_pool_w_out, v_mla_norm, v_mla_w_in, v_mla_q_norm, v_mla_w_q_b, v_mla_kv_norm, v_mla_w_kv_b, v_mla_w_out, v_final_norm):
    s = x.shape[1]
    tq = min(512, s)
    x0 = x.reshape(s, D_MODEL)
    tgt = loss_target.reshape(s, D_MODEL)
    cx, cy, cc_idx = _place()
    chip = 2 * cx + cy

    big_names = ("pool_w_in", "pool_w_group", "pool_w_out", "mla_w_in", "mla_w_q_b", "mla_w_kv_b", "mla_w_out")
    big_w = dict(zip(big_names, (pool_w_in, pool_w_group, pool_w_out, mla_w_in, mla_w_q_b, mla_w_kv_b, mla_w_out)))
    big_m = dict(zip(big_names, (m_pool_w_in, m_pool_w_group, m_pool_w_out, m_mla_w_in, m_mla_w_q_b, m_mla_w_kv_b, m_mla_w_out)))
    big_v = dict(zip(big_names, (v_pool_w_in, v_pool_w_group, v_pool_w_out, v_mla_w_in, v_mla_w_q_b, v_mla_w_kv_b, v_mla_w_out)))

    small_vec = jnp.concatenate([mla_norm.reshape(-1), mla_q_norm.reshape(-1), mla_kv_norm.reshape(-1),
                                 jnp.zeros((96,), F32)])
    pool_packed = _pack_shard([big_w[n] for n in big_names[:3]])
    mla_packed = _pack_shard([big_w[n] for n in big_names[3:]], small_vec)
    w = _unpack_pool(all_gather8(pool_packed, name="gather_pool_weights", own_half=True).reshape(4, POOL_R, PACK_C))
    g_pool = pool_norm.reshape(1, D_MODEL)
    g_final = final_norm.reshape(1, D_MODEL)
    sc_pool = pool_scale.reshape(1, POOL_WIDTH)

    inv_freq = 1.0 / (ROPE_THETA ** (jnp.arange(0, QK_ROPE, 2, dtype=F32) / QK_ROPE))
    ang = positions.reshape(s).astype(F32)[:, None] * inv_freq
    cos, sin = jnp.cos(ang), jnp.sin(ang)
    z32, z64, z96 = (jnp.zeros((s, n), F32) for n in (32, 64, 96))
    t_cc = jnp.concatenate([cos, cos, z64], axis=1)
    t_sa = jnp.concatenate([-sin, z96], axis=1)
    t_sb = jnp.concatenate([z32, sin, z64], axis=1)

    h0 = norm_fwd(x0, g_pool, name="pool_norm_fwd")
    uz, mla_land = mm_nn(h0, w["pool_w_in"], name="pool_in_proj", out_dtype=F32, host=gather_ici(mla_packed))
    pd = pool_prep(uz, name="pool_window")
    y1 = pool_mix_gate(pd, w["pool_w_group"], uz, sc_pool, name="pool_group_mix")
    x1, mla_land = mm_nn(y1, w["pool_w_out"], name="pool_out_proj", out_dtype=F32, add=x0,
                         host=gather_forward(mla_land))
    w.update(_unpack_mla(lax.dynamic_update_slice_in_dim(mla_land, mla_packed[None], chip, axis=0)))

    h1 = norm_fwd(x1, w["mla_norm"], name="mla_norm_fwd")
    proj = mm_nn(h1, w["mla_w_in"], name="mla_in_proj", out_dtype=F32, tn=P_WIDTH // 2)
    qn, kvn, krr = latent_fwd(proj, w["mla_q_norm"], w["mla_kv_norm"], t_cc, t_sa, t_sb, name="mla_latent_fwd")
    qr = q_proj_rope(qn, w["mla_w_q_b"], t_cc, t_sa, t_sb, name="mla_q_proj")
    kv = mm_nn(kvn, w["mla_w_kv_b"], name="mla_kv_proj", out_dtype=BF, tn=2048)
    o, y2, lse = attn_fwd(qr, kv, krr, proj, name="mla_attn_fwd", tq=tq)
    dx2, d_final, loss_part = mm_nn_loss(y2, w["mla_w_out"], x1, g_final, tgt, name="mla_out_proj_loss")

    grads = {}
    grads["mla_w_out"] = mm_tn(y2, dx2, name="mla_out_proj_dw")
    do, dz2, delta = mla_out_dx_gate(dx2, w["mla_w_out"], o, proj, name="mla_out_proj_dx", tq=tq)
    dkv, dkr, dq_pre = attn_bwd(qr, kv, krr, do, lse, delta, t_cc, t_sa, t_sb, name="mla_attn_bwd", tq=tq)
    dqn = mm_nt(dq_pre, w["mla_w_q_b"], name="mla_q_proj_dx", out_dtype=F32, tn=Q_LORA, tk=4096)
    g_qb = mm_tn(qn, dq_pre, name="mla_q_proj_dw", tm=Q_LORA, tn=2048)
    dkvn = mm_nt(dkv, w["mla_w_kv_b"], name="mla_kv_proj_dx", out_dtype=F32, tn=KV_LORA, tk=4096)
    grads["mla_w_kv_b"] = mm_tn(kvn, dkv, name="mla_kv_proj_dw", tm=KV_LORA, by_column_block=True)
    dsmall, d_qnorm, d_kvnorm = latent_bwd(proj, w["mla_q_norm"], w["mla_kv_norm"], dqn, dkvn, dkr,
                                           t_cc, t_sa, t_sb, name="mla_latent_bwd")
    dx1, d_mnorm = mm_nt_norm_bwd(dz2, w["mla_w_in"][:, P_Z:], (dsmall, w["mla_w_in"]), x1, w["mla_norm"], dx2,
                                  name="mla_in_proj_dx")
    g_in_a = mm_tn(h1, dsmall, name="mla_in_proj_dw_a", tn=P_SMALL)
    g_in_b = mm_tn(h1, dz2, name="mla_in_proj_dw_b")

    g_in = jnp.concatenate([g_in_a, g_in_b], axis=1)
    grads["mla_w_in"] = jnp.concatenate([g_in[:, P_Q:P_Z], g_in[:, P_KV:P_KV + KV_LORA],
                                         g_in[:, P_KR:P_KR + QK_ROPE], g_in[:, P_Z:]], axis=1)
    grads["mla_w_q_b"] = g_qb.reshape(Q_LORA, N_HEADS, HEAD_PAD)[:, :, :QK_NOPE + QK_ROPE].reshape(Q_LORA, -1)
    core_idx = cc_idx.reshape(1).astype(jnp.int32)
    gp_mla = _pack_mla_grads(grads)

    grads["pool_w_out"], sib = mm_tn(y1, dx1, name="pool_out_proj_dw", host=swap_halves(gp_mla))
    pre = add_halves(gp_mla, sib, core_idx, name="mla_grad_add_halves", rows=MLA_R // 2)
    dmm, dz1, d_scale, got = pool_out_dx_gate(dx1, w["pool_w_out"], pd, w["pool_w_group"], uz, sc_pool,
                                              name="pool_out_proj_dx", host=exchange_chips(pre))
    tot = sum_chips(got, core_idx, name="mla_grad_sum_chips", rows=MLA_R // 2)
    dpd, red_mla = gmm_nt(dmm, w["pool_w_group"], name="pool_group_mix_dx", host=join_halves(tot))
    grads["pool_w_group"] = gmm_tn(pd, dmm, 4, name="pool_group_mix_dw")
    du = pool_prep_bwd(dpd, name="pool_window_bwd")
    g_pin_u = mm_tn(h0, du, name="pool_in_proj_dw_u", by_column_block=True)
    g_pin_z = mm_tn(h0, dz1, name="pool_in_proj_dw_z", by_column_block=True)
    grads["pool_w_in"] = jnp.concatenate([g_pin_u, g_pin_z], axis=0)

    gp_pool = _pack_pool_grads(grads)
    dh0, sib = mm_nt(du, w["pool_w_in"], name="pool_in_proj_dx_u", out_dtype=F32, host=swap_halves(gp_pool))
    pre = add_halves(gp_pool, sib, core_idx, name="pool_grad_add_halves", rows=POOL_R // 2)
    grad_x, d_pnorm, got = mm_nt_norm_bwd(dz1, w["pool_w_in"], dh0, x0, g_pool, dx1, name="pool_in_proj_dx_z",
                                          b_col=POOL_WIDTH, host=exchange_chips(pre))
    tot = sum_chips(got, core_idx, name="pool_grad_sum_chips", rows=POOL_R // 2)
    red_pool = run_exchange(join_halves(tot), name="pool_grad_join_halves")[0]
    red_parts = _split_rows(red_pool, POOL_ROWS, 0) + _split_rows(red_mla, MLA_ROWS, 0)

    sv = jnp.concatenate([d_pnorm.reshape(-1), d_scale.reshape(-1), d_final.reshape(-1), d_mnorm.reshape(-1),
                          d_qnorm.reshape(-1), d_kvnorm.reshape(-1), loss_part[0, :1],
                          jnp.zeros((SV_ROWS * SV_COLS - SV_OFF["loss"] - 1,), F32)]).reshape(SV_ROWS, SV_COLS)
    sv_all = all_gather8(sv, name="gather_small_grads", own_half=False)
    sv_sum = sum_devices(sv_all, name="sum_small_grads").reshape(-1)
    loss = sv_sum[SV_OFF["loss"]]

    def sv_take(key, n):
        return lax.slice_in_dim(sv_sum, SV_OFF[key], SV_OFF[key] + n)

    out_g, out_d, out_m, out_v = {}, {}, {}, {}
    for name, part in zip(big_names, red_parts):
        shp = big_w[name].shape
        g2 = part.reshape(shp)
        two_d = (-1, shp[-1])
        d_, m_, v_ = adamw(big_w[name].reshape(two_d), g2.reshape(two_d), big_m[name].reshape(two_d),
                           big_v[name].reshape(two_d), name="adamw_" + name)
        out_g[name], out_d[name], out_m[name], out_v[name] = g2, d_.reshape(shp), m_.reshape(shp), v_.reshape(shp)

    small = [
        ("pool_norm", pool_norm, m_pool_norm, v_pool_norm, sv_take("pool_norm", 1024)),
        ("pool_scale", pool_scale, m_pool_scale, v_pool_scale, sv_take("pool_scale", 2048)),
        ("final_norm", final_norm, m_final_norm, v_final_norm, sv_take("final_norm", 1024)),
        ("mla_norm", mla_norm, m_mla_norm, v_mla_norm,
         lax.dynamic_slice_in_dim(sv_take("mla_norm", 1024), chip * 256, 256)),
        ("mla_q_norm", mla_q_norm, m_mla_q_norm, v_mla_q_norm,
         lax.dynamic_slice_in_dim(sv_take("q_norm", 384), chip * 96, 96)),
        ("mla_kv_norm", mla_kv_norm, m_mla_kv_norm, v_mla_kv_norm,
         lax.dynamic_slice_in_dim(sv_take("kv_norm", 256), chip * 64, 64)),
    ]
    sw = jnp.concatenate([t[1].reshape(-1) for t in small] + [jnp.zeros((96,), F32)]).reshape(1, -1)
    sm = jnp.concatenate([t[2].reshape(-1) for t in small] + [jnp.zeros((96,), F32)]).reshape(1, -1)
    s_v = jnp.concatenate([t[3].reshape(-1) for t in small] + [jnp.ones((96,), F32)]).reshape(1, -1)
    sg = jnp.concatenate([t[4].reshape(-1) for t in small] + [jnp.zeros((96,), F32)]).reshape(1, -1)
    sd_, sm_, sv_ = adamw(sw, sg, sm, s_v, name="adamw_vectors")
    off = 0
    for name, wt, _, _, gvec in small:
        n = gvec.shape[0]
        shp = wt.shape
        out_g[name] = gvec.reshape(shp)
        out_d[name] = sd_[0, off:off + n].reshape(shp)
        out_m[name] = sm_[0, off:off + n].reshape(shp)
        out_v[name] = sv_[0, off:off + n].reshape(shp)
        off += n

    order = ("pool_norm", "pool_w_in", "pool_w_group", "pool_scale", "pool_w_out", "mla_norm", "mla_w_in",
             "mla_q_norm", "mla_w_q_b", "mla_kv_norm", "mla_w_kv_b", "mla_w_out", "final_norm")
    return (loss, grad_x.reshape(x.shape), *[out_g[n] for n in order], *[out_d[n] for n in order],
            *[out_m[n] for n in order], *[out_v[n] for n in order])
```

```python
import functools
from typing import Callable, NamedTuple

import jax
import jax.numpy as jnp
from jax import lax
from jax.experimental import pallas as pl
from jax.experimental.pallas import tpu as pltpu

F32 = jnp.float32
BF = jnp.bfloat16
MESH = pl.DeviceIdType.MESH

D_MODEL = 1024
POOL_WIDTH = 2048
POOL_WINDOWS = (2, 4, 8, 16)
POOL_GROUP = 512
HALO = 16
N_HEADS = 16
QK_NOPE = 128
QK_ROPE = 64
V_DIM = 128
HEAD_PAD = 256
Q_LORA = 384
KV_LORA = 256
MLA_WIDTH = 2048
ROPE_THETA = 10000.0
EPS = 1e-6
SCALE = (QK_NOPE + QK_ROPE) ** -0.5
SCALE_LOG2E = SCALE * 1.4426950408889634
NEG = -1e30

P_KV, P_KR, P_Q, P_Z = 0, 256, 384, 768
P_SMALL = 768
P_WIDTH = 2816

ADAM_LR = 0.001
ADAM_B1 = 0.9
ADAM_B2 = 0.999
ADAM_EPS = 1e-08
ADAM_WD = 0.01
ADAM_STEP = 10

NN = (((1,), (0,)), ((), ()))
NT = (((1,), (1,)), ((), ()))
TN = (((0,), (0,)), ((), ()))

POOL_ROWS = (1024, 256, 512)
MLA_ROWS = (688, 288, 256, 512)
PACK_PAD = 16
POOL_R = sum(POOL_ROWS)
MLA_R = sum(MLA_ROWS) + PACK_PAD
PACK_C = 1024
SV_OFF = dict(pool_norm=0, pool_scale=1024, final_norm=3072, mla_norm=4096, q_norm=5120, kv_norm=5504, loss=5760)
SV_ROWS, SV_COLS = 8, 768

VMEM_LIMIT = 56 * 1024 * 1024


def _params(n_axes, vmem=None):
    return pltpu.CompilerParams(dimension_semantics=("arbitrary",) * n_axes,
                                vmem_limit_bytes=VMEM_LIMIT if vmem is None else vmem)


def _sigmoid(z):
    return 1.0 / (1.0 + jnp.exp(-z))


class Exchange(NamedTuple):
    operands: tuple
    out_shapes: tuple
    aliases: dict
    n_sems: int
    start: Callable
    wait: Callable


HBM_SPEC = pl.BlockSpec(memory_space=pl.ANY)


def _exchange_scratch(ex):
    return [pltpu.SemaphoreType.DMA((ex.n_sems,)), pltpu.SemaphoreType.DMA((ex.n_sems,)), pltpu.SemaphoreType.DMA]


def run_exchange(ex, *, name):
    n_in, n_out = len(ex.operands), len(ex.out_shapes)

    def body(*refs):
        args = (refs[:n_in], refs[n_in:n_in + n_out]) + tuple(refs[n_in + n_out:])
        ex.start(*args)
        ex.wait(*args)

    return pl.pallas_call(
        body, name=name, out_shape=list(ex.out_shapes), in_specs=[HBM_SPEC] * n_in,
        out_specs=[HBM_SPEC] * n_out, scratch_shapes=_exchange_scratch(ex),
        input_output_aliases=dict(ex.aliases))(*ex.operands)


def _call(core, *, name, grid, in_specs, out_specs, out_shape, args, scratch=(), host=None):
    in_specs, out_specs, out_shape = list(in_specs), list(out_specs), list(out_shape)
    params = _params(len(grid))
    if host is None:
        return pl.pallas_call(core, name=name, grid=grid, in_specs=in_specs, out_specs=out_specs,
                              out_shape=out_shape, scratch_shapes=list(scratch), compiler_params=params)(*args)
    n_in, n_out = len(in_specs), len(out_specs)
    n_hin, n_hout = len(host.operands), len(host.out_shapes)

    def body(*refs):
        ins, refs = refs[:n_in], refs[n_in:]
        h_in, refs = refs[:n_hin], refs[n_hin:]
        outs, refs = refs[:n_out], refs[n_out:]
        h_out, refs = refs[:n_hout], refs[n_hout:]
        own_scratch, sems = refs[:-3], refs[-3:]
        ids = [pl.program_id(ax) for ax in range(len(grid))]
        first = functools.reduce(jnp.logical_and, [i == 0 for i in ids])
        last = functools.reduce(jnp.logical_and, [i == n - 1 for i, n in zip(ids, grid)])

        @pl.when(first)
        def _():
            host.start(h_in, h_out, *sems)

        core(*ins, *outs, *own_scratch)

        @pl.when(last)
        def _():
            host.wait(h_in, h_out, *sems)

    return pl.pallas_call(
        body, name=name, grid=grid, in_specs=in_specs + [HBM_SPEC] * n_hin,
        out_specs=out_specs + [HBM_SPEC] * n_hout, out_shape=out_shape + list(host.out_shapes),
        scratch_shapes=list(scratch) + _exchange_scratch(host),
        input_output_aliases={n_in + i: n_out + o for i, o in host.aliases.items()},
        compiler_params=params)(*args, *host.operands)


def _mm(a, b, *, dims, grid, a_spec, b_spec, o_spec, out_shape, out_dtype, acc_shape, name,
        add=None, add_spec=None, host=None):
    nk = grid[-1]
    kax = len(grid) - 1

    def body(*refs):
        if add is None:
            a_ref, b_ref, o_ref = refs[:3]
            add_ref = None
            rest = refs[3:]
        else:
            a_ref, b_ref, add_ref, o_ref = refs[:4]
            rest = refs[4:]
        part = lax.dot_general(a_ref[...].astype(BF), b_ref[...].astype(BF), dims,
                               preferred_element_type=F32)

        def finish(r):
            if add_ref is not None:
                r = r + add_ref[...]
            o_ref[...] = r.astype(o_ref.dtype)

        if nk == 1:
            finish(part)
        else:
            acc = rest[0]
            k = pl.program_id(kax)

            @pl.when(k == 0)
            def _():
                acc[...] = part

            @pl.when(k > 0)
            def _():
                acc[...] += part

            @pl.when(k == nk - 1)
            def _():
                finish(acc[...])

    in_specs = [a_spec, b_spec]
    args = [a, b]
    if add is not None:
        in_specs.append(add_spec)
        args.append(add)
    out = _call(body, name=name, grid=grid, in_specs=in_specs, out_specs=[o_spec],
                out_shape=[jax.ShapeDtypeStruct(out_shape, out_dtype)], args=args,
                scratch=[] if nk == 1 else [pltpu.VMEM(acc_shape, F32)], host=host)
    return out[0] if host is None else out


def _pick(n, t):
    t = min(n, t)
    assert n % t == 0, (n, t)
    return t


def mm_nn(a, b, *, name, out_dtype, add=None, tm=1024, tn=1024, tk=2048, host=None):
    m = a.shape[0]
    kk, n = b.shape
    tm, tn, tk = _pick(m, tm), _pick(n, tn), _pick(kk, tk)
    return _mm(a, b, dims=NN, grid=(m // tm, n // tn, kk // tk),
               a_spec=pl.BlockSpec((tm, tk), lambda i, j, k: (i, k)),
               b_spec=pl.BlockSpec((tk, tn), lambda i, j, k: (k, j)),
               o_spec=pl.BlockSpec((tm, tn), lambda i, j, k: (i, j)),
               add=add, add_spec=pl.BlockSpec((tm, tn), lambda i, j, k: (i, j)),
               out_shape=(m, n), out_dtype=out_dtype, acc_shape=(tm, tn), name=name, host=host)


def mm_nt(a, b, *, name, out_dtype, b_col=0, add=None, tm=1024, tn=1024, tk=2048, host=None):
    m, kk = a.shape
    n = b.shape[0]
    tm, tn, tk = _pick(m, tm), _pick(n, tn), _pick(kk, tk)
    assert b_col % tk == 0
    ko = b_col // tk
    return _mm(a, b, dims=NT, grid=(m // tm, n // tn, kk // tk),
               a_spec=pl.BlockSpec((tm, tk), lambda i, j, k: (i, k)),
               b_spec=pl.BlockSpec((tn, tk), lambda i, j, k: (j, ko + k)),
               o_spec=pl.BlockSpec((tm, tn), lambda i, j, k: (i, j)),
               add=add, add_spec=pl.BlockSpec((tm, tn), lambda i, j, k: (i, j)),
               out_shape=(m, n), out_dtype=out_dtype, acc_shape=(tm, tn), name=name, host=host)


def mm_tn(a, b, *, name, tm=1024, tn=1024, tk=2048, host=None, by_column_block=False):
    s, m = a.shape
    n = b.shape[1]
    tm, tn, tk = _pick(m, tm), _pick(n, tn), _pick(s, tk)
    if by_column_block:
        out_shape, o_spec = (n // tn, m, tn), pl.BlockSpec((None, tm, tn), lambda i, j, k: (j, i, 0))
    else:
        out_shape, o_spec = (m, n), pl.BlockSpec((tm, tn), lambda i, j, k: (i, j))
    return _mm(a, b, dims=TN, grid=(m // tm, n // tn, s // tk),
               a_spec=pl.BlockSpec((tk, tm), lambda i, j, k: (k, i)),
               b_spec=pl.BlockSpec((tk, tn), lambda i, j, k: (k, j)),
               o_spec=o_spec, out_shape=out_shape, out_dtype=F32, acc_shape=(tm, tn), name=name, host=host)


def gmm_nt(a, w, *, name, tm=1024, host=None):
    s = a.shape[0]
    g, kk, n = w.shape
    tm = _pick(s, tm)
    return _mm(a, w, dims=NT, grid=(s // tm, g, 1),
               a_spec=pl.BlockSpec((tm, n), lambda i, gi, k: (i, gi)),
               b_spec=pl.BlockSpec((None, kk, n), lambda i, gi, k: (gi, 0, 0)),
               o_spec=pl.BlockSpec((tm, kk), lambda i, gi, k: (i, gi)),
               out_shape=(s, g * kk), out_dtype=F32, acc_shape=(tm, kk), name=name, host=host)


def gmm_tn(a, b, g, *, name, tk=2048):
    s = a.shape[0]
    kk, n = a.shape[1] // g, b.shape[1] // g
    tk = _pick(s, tk)
    return _mm(a, b, dims=TN, grid=(g, s // tk),
               a_spec=pl.BlockSpec((tk, kk), lambda gi, k: (k, gi)),
               b_spec=pl.BlockSpec((tk, n), lambda gi, k: (k, gi)),
               o_spec=pl.BlockSpec((None, kk, n), lambda gi, k: (gi, 0, 0)),
               out_shape=(g, kk, n), out_dtype=F32, acc_shape=(kk, n), name=name)


def _rms(xv, gv):
    inv = lax.rsqrt(jnp.mean(xv * xv, axis=-1, keepdims=True) + EPS)
    return (xv * inv) * gv


def _rms_bwd(xv, gv, dh):
    inv = lax.rsqrt(jnp.mean(xv * xv, axis=-1, keepdims=True) + EPS)
    xhat = xv * inv
    dxhat = dh * gv
    dx = inv * (dxhat - xhat * jnp.mean(dxhat * xhat, axis=-1, keepdims=True))
    return dx, jnp.sum(dh * xhat, axis=0, keepdims=True)


def norm_fwd(x, g, *, name, t=1024):
    s, width = x.shape
    t = _pick(s, t)

    def body(x_ref, g_ref, o_ref):
        o_ref[...] = _rms(x_ref[...], g_ref[...]).astype(o_ref.dtype)

    row = pl.BlockSpec((t, width), lambda i: (i, 0))
    return pl.pallas_call(
        body, name=name, grid=(s // t,), in_specs=[row, pl.BlockSpec((1, width), lambda i: (0, 0))],
        out_specs=row, out_shape=jax.ShapeDtypeStruct((s, width), BF), compiler_params=_params(1))(x, g)


def _accumulate(ref, part):
    @pl.when(pl.program_id(0) == 0)
    def _():
        ref[...] = part

    @pl.when(pl.program_id(0) > 0)
    def _():
        ref[...] += part


def mm_nt_norm_bwd(a, b, other, x, g, res, *, name, b_col=0, tm=512, host=None):
    s, kk = a.shape
    d = b.shape[0]
    tm = _pick(s, tm)
    assert b_col % kk == 0
    pair = isinstance(other, tuple)

    def body(a_ref, b_ref, *refs):
        dh = lax.dot_general(a_ref[...].astype(BF), b_ref[...].astype(BF), NT, preferred_element_type=F32)
        if pair:
            a2_ref, b2_ref, x_ref, g_ref, res_ref, dx_ref, dg_ref = refs
            dh = dh + lax.dot_general(a2_ref[...].astype(BF), b2_ref[...].astype(BF), NT, preferred_element_type=F32)
        else:
            add_ref, x_ref, g_ref, res_ref, dx_ref, dg_ref = refs
            dh = dh + add_ref[...]
        dx, dg = _rms_bwd(x_ref[...], g_ref[...], dh)
        _accumulate(dg_ref, dg)
        dx_ref[...] = dx + res_ref[...]

    row = pl.BlockSpec((tm, d), lambda i: (i, 0))
    vec = pl.BlockSpec((1, d), lambda i: (0, 0))
    if pair:
        k2 = other[0].shape[1]
        other_specs = [pl.BlockSpec((tm, k2), lambda i: (i, 0)), pl.BlockSpec((d, k2), lambda i: (0, 0))]
        other_args = list(other)
    else:
        other_specs, other_args = [row], [other]
    return _call(
        body, name=name, grid=(s // tm,),
        in_specs=[pl.BlockSpec((tm, kk), lambda i: (i, 0)), pl.BlockSpec((d, kk), lambda i: (0, b_col // kk)),
                  *other_specs, row, vec, row],
        out_specs=[row, vec],
        out_shape=[jax.ShapeDtypeStruct((s, d), F32), jax.ShapeDtypeStruct((1, d), F32)],
        args=[a, b, *other_args, x, g, res], host=host)


def mm_nn_loss(a, b, add, gf, tgt, *, name, tm=512):
    s, kk = a.shape
    d = b.shape[1]
    tm = _pick(s, tm)

    def body(a_ref, b_ref, add_ref, g_ref, t_ref, dx_ref, dg_ref, loss_ref):
        xv = jnp.dot(a_ref[...].astype(BF), b_ref[...].astype(BF), preferred_element_type=F32) + add_ref[...]
        inv = lax.rsqrt(jnp.mean(xv * xv, axis=-1, keepdims=True) + EPS)
        xhat = xv * inv
        gv = g_ref[...]
        diff = xhat * gv - t_ref[...]
        row_err = jnp.mean(diff * diff, axis=-1, keepdims=True)
        _accumulate(loss_ref, jnp.broadcast_to(0.5 * jnp.sum(row_err, axis=0, keepdims=True), (1, 128)))
        dout = diff * (1.0 / d)
        _accumulate(dg_ref, jnp.sum(dout * xhat, axis=0, keepdims=True))
        dxhat = dout * gv
        dx_ref[...] = inv * (dxhat - xhat * jnp.mean(dxhat * xhat, axis=-1, keepdims=True))

    row = pl.BlockSpec((tm, d), lambda i: (i, 0))
    vec = pl.BlockSpec((1, d), lambda i: (0, 0))
    return _call(
        body, name=name, grid=(s // tm,),
        in_specs=[pl.BlockSpec((tm, kk), lambda i: (i, 0)), pl.BlockSpec((kk, d), lambda i: (0, 0)), row, vec, row],
        out_specs=[row, vec, pl.BlockSpec((1, 128), lambda i: (0, 0))],
        out_shape=[jax.ShapeDtypeStruct((s, d), F32), jax.ShapeDtypeStruct((1, d), F32),
                   jax.ShapeDtypeStruct((1, 128), F32)],
        args=[a, b, add, gf, tgt])


ROW_CHUNK = 56


def pool_prep(uz, *, name, t=256):
    s = uz.shape[0]
    t = _pick(s, t)
    hb = t // HALO

    lead = 2 * HALO
    live = t + lead - 8
    assert live % ROW_CHUNK == 0

    def body(u_ref, halo_ref, o_ref, buf_a, buf_b):
        i = pl.program_id(0)
        buf_a[pl.ds(lead, t), :] = u_ref[...]
        buf_a[pl.ds(0, HALO), :] = jnp.zeros((HALO, POOL_WIDTH), F32)
        buf_b[pl.ds(0, 8), :] = jnp.zeros((8, POOL_WIDTH), F32)

        @pl.when(i == 0)
        def _():
            buf_a[pl.ds(HALO, HALO), :] = jnp.zeros((HALO, POOL_WIDTH), F32)

        @pl.when(i > 0)
        def _():
            buf_a[pl.ds(HALO, HALO), :] = halo_ref[...]

        pos = i * t + lax.broadcasted_iota(jnp.int32, (t, POOL_GROUP), 0)
        for g, w in enumerate(POOL_WINDOWS):
            cols = pl.ds(g * POOL_GROUP, POOL_GROUP)
            src, dst, shift = buf_a, buf_b, 1
            while shift < w:
                for r0 in range(8, 8 + live, ROW_CHUNK):
                    dst[pl.ds(r0, ROW_CHUNK), cols] = (src[pl.ds(r0, ROW_CHUNK), cols]
                                                       + src[pl.ds(r0 - shift, ROW_CHUNK), cols])
                src, dst, shift = dst, src, 2 * shift
            cnt = jnp.minimum(pos + 1, w).astype(F32)
            o_ref[:, cols] = (src[pl.ds(lead, t), cols] / cnt - u_ref[:, cols]).astype(o_ref.dtype)

    return pl.pallas_call(
        body, name=name, grid=(s // t,),
        in_specs=[pl.BlockSpec((t, POOL_WIDTH), lambda i: (i, 0)),
                  pl.BlockSpec((HALO, POOL_WIDTH), lambda i: (jnp.maximum(i * hb - 1, 0), 0))],
        out_specs=pl.BlockSpec((t, POOL_WIDTH), lambda i: (i, 0)),
        out_shape=jax.ShapeDtypeStruct((s, POOL_WIDTH), BF),
        scratch_shapes=[pltpu.VMEM((t + lead, POOL_WIDTH), F32), pltpu.VMEM((t + lead, POOL_WIDTH), F32)],
        compiler_params=_params(1))(uz, uz)


def pool_prep_bwd(dpd, *, name, t=256):
    s = dpd.shape[0]
    t = _pick(s, t)
    hb = t // HALO
    n = s // t

    tail = 2 * HALO
    live = t + tail - 8
    assert live % ROW_CHUNK == 0

    def body(d_ref, halo_ref, o_ref, buf_a, buf_b):
        i = pl.program_id(0)
        buf_a[pl.ds(t + HALO, HALO), :] = jnp.zeros((HALO, POOL_WIDTH), F32)
        buf_b[pl.ds(live, 8), :] = jnp.zeros((8, POOL_WIDTH), F32)
        pos = i * t + lax.broadcasted_iota(jnp.int32, (t, POOL_GROUP), 0)
        for g, w in enumerate(POOL_WINDOWS):
            cols = pl.ds(g * POOL_GROUP, POOL_GROUP)
            cnt = jnp.minimum(pos + 1, w).astype(F32)
            buf_a[pl.ds(0, t), cols] = d_ref[:, cols] / cnt

            @pl.when(i < n - 1)
            def _():
                buf_a[pl.ds(t, HALO), cols] = halo_ref[:, cols] / float(w)

            @pl.when(i == n - 1)
            def _():
                buf_a[pl.ds(t, HALO), cols] = jnp.zeros((HALO, POOL_GROUP), F32)

        for g, w in enumerate(POOL_WINDOWS):
            cols = pl.ds(g * POOL_GROUP, POOL_GROUP)
            src, dst, shift = buf_a, buf_b, 1
            while shift < w:
                for r0 in range(0, live, ROW_CHUNK):
                    dst[pl.ds(r0, ROW_CHUNK), cols] = (src[pl.ds(r0, ROW_CHUNK), cols]
                                                       + src[pl.ds(r0 + shift, ROW_CHUNK), cols])
                src, dst, shift = dst, src, 2 * shift
            o_ref[:, cols] = (src[pl.ds(0, t), cols] - d_ref[:, cols]).astype(o_ref.dtype)

    return pl.pallas_call(
        body, name=name, grid=(n,),
        in_specs=[pl.BlockSpec((t, POOL_WIDTH), lambda i: (i, 0)),
                  pl.BlockSpec((HALO, POOL_WIDTH), lambda i: (jnp.minimum((i + 1) * hb, n * hb - 1), 0))],
        out_specs=pl.BlockSpec((t, POOL_WIDTH), lambda i: (i, 0)),
        out_shape=jax.ShapeDtypeStruct((s, POOL_WIDTH), BF),
        scratch_shapes=[pltpu.VMEM((t + tail, POOL_WIDTH), F32), pltpu.VMEM((t + tail, POOL_WIDTH), F32)],
        compiler_params=_params(1))(dpd, dpd)


CHUNK = 512


def _chunks(width, step=CHUNK):
    return [slice(c, c + step) for c in range(0, width, step)]


def pool_mix_gate(pd, wg, uz, scale, *, name, tm=1024):
    s = pd.shape[0]
    g = wg.shape[0]
    tm = _pick(s, tm)

    def body(a_ref, w_ref, z_ref, sc_ref, y_ref):
        mm = jnp.dot(a_ref[...], w_ref[...], preferred_element_type=F32)
        z = z_ref[...]
        y_ref[...] = ((mm * sc_ref[...]) * (z * _sigmoid(z))).astype(y_ref.dtype)

    blk = pl.BlockSpec((tm, POOL_GROUP), lambda i, gi: (i, gi))
    return pl.pallas_call(
        body, name=name, grid=(s // tm, g),
        in_specs=[blk, pl.BlockSpec((None, POOL_GROUP, POOL_GROUP), lambda i, gi: (gi, 0, 0)),
                  pl.BlockSpec((tm, POOL_GROUP), lambda i, gi: (i, g + gi)),
                  pl.BlockSpec((1, POOL_GROUP), lambda i, gi: (0, gi))],
        out_specs=blk, out_shape=jax.ShapeDtypeStruct((s, POOL_WIDTH), BF),
        compiler_params=_params(2))(pd, wg, uz, scale)


def pool_out_dx_gate(dx, w_out, pd, wg, uz, scale, *, name, tm=512, host=None):
    s, d = dx.shape
    tm = _pick(s, tm)
    assert CHUNK == POOL_GROUP

    def body(dx_ref, w_ref, pd_ref, wg_ref, z_ref, sc_ref, dmm_ref, dz_ref, dsc_ref):
        dxv = dx_ref[...].astype(BF)
        parts = []
        for g, c in enumerate(_chunks(POOL_WIDTH)):
            dyv = lax.dot_general(dxv, w_ref[c, :], NT, preferred_element_type=F32)
            z = z_ref[:, c]
            sig = _sigmoid(z)
            mmv = jnp.dot(pd_ref[:, c], wg_ref[g], preferred_element_type=F32)
            scv = sc_ref[:, c]
            dmixed = dyv * (z * sig)
            dmm_ref[:, c] = (dmixed * scv).astype(dmm_ref.dtype)
            dz_ref[:, c] = (dyv * (mmv * scv) * (sig * (1.0 + z * (1.0 - sig)))).astype(dz_ref.dtype)
            parts.append(jnp.sum(dmixed * mmv, axis=0, keepdims=True))

        @pl.when(pl.program_id(0) == 0)
        def _():
            for c, part in zip(_chunks(POOL_WIDTH), parts):
                dsc_ref[:, c] = part

        @pl.when(pl.program_id(0) > 0)
        def _():
            for c, part in zip(_chunks(POOL_WIDTH), parts):
                dsc_ref[:, c] += part

    blk = pl.BlockSpec((tm, POOL_WIDTH), lambda i: (i, 0))
    vec = pl.BlockSpec((1, POOL_WIDTH), lambda i: (0, 0))
    return _call(
        body, name=name, grid=(s // tm,),
        in_specs=[pl.BlockSpec((tm, d), lambda i: (i, 0)), pl.BlockSpec((POOL_WIDTH, d), lambda i: (0, 0)),
                  blk, pl.BlockSpec(wg.shape, lambda i: (0, 0, 0)),
                  pl.BlockSpec((tm, POOL_WIDTH), lambda i: (i, 1)), vec],
        out_specs=[blk, blk, vec],
        out_shape=[jax.ShapeDtypeStruct((s, POOL_WIDTH), BF), jax.ShapeDtypeStruct((s, POOL_WIDTH), BF),
                   jax.ShapeDtypeStruct((1, POOL_WIDTH), F32)],
        args=[dx, w_out, pd, wg, uz, scale], host=host)


def _rope(a, cc, sa, sb):
    return a * cc + pltpu.roll(a, 96, 1) * sa + pltpu.roll(a, 32, 1) * sb


def _unrope(d, cc, sa, sb):
    return d * cc + pltpu.roll(d * sa, 32, 1) + pltpu.roll(d * sb, 96, 1)


def q_proj_rope(qn, wq, cc, sa, sb, *, name, tm=1024, heads=8):
    s, kk = qn.shape
    tm = _pick(s, tm)
    tn = heads * HEAD_PAD

    def body(a_ref, b_ref, cc_ref, sa_ref, sb_ref, o_ref):
        q = jnp.dot(a_ref[...], b_ref[...], preferred_element_type=F32)
        for h in range(heads):
            nope = slice(h * HEAD_PAD, h * HEAD_PAD + QK_NOPE)
            rope = slice(h * HEAD_PAD + QK_NOPE, (h + 1) * HEAD_PAD)
            o_ref[:, nope] = q[:, nope].astype(o_ref.dtype)
            o_ref[:, rope] = _rope(q[:, rope], cc_ref[...], sa_ref[...], sb_ref[...]).astype(o_ref.dtype)

    tab = pl.BlockSpec((tm, 128), lambda i, j: (i, 0))
    return pl.pallas_call(
        body, name=name, grid=(s // tm, N_HEADS // heads),
        in_specs=[pl.BlockSpec((tm, kk), lambda i, j: (i, 0)), pl.BlockSpec((kk, tn), lambda i, j: (0, j)),
                  tab, tab, tab],
        out_specs=pl.BlockSpec((tm, tn), lambda i, j: (i, j)),
        out_shape=jax.ShapeDtypeStruct((s, N_HEADS * HEAD_PAD), BF), compiler_params=_params(2))(qn, wq, cc, sa, sb)


LAT_KV = slice(P_KV, P_KV + KV_LORA)
LAT_KR = slice(P_KR, P_KR + 128)
LAT_Q = slice(P_Q, P_Q + Q_LORA)


def latent_fwd(proj, g_q, g_kv, cc, sa, sb, *, name, t=1024):
    s = proj.shape[0]
    t = _pick(s, t)

    def body(p_ref, gq_ref, gkv_ref, cc_ref, sa_ref, sb_ref, qn_ref, kvn_ref, kr_ref):
        qn_ref[...] = _rms(p_ref[:, LAT_Q], gq_ref[...]).astype(qn_ref.dtype)
        kvn_ref[...] = _rms(p_ref[:, LAT_KV], gkv_ref[...]).astype(kvn_ref.dtype)
        kr_ref[...] = _rope(p_ref[:, LAT_KR], cc_ref[...], sa_ref[...], sb_ref[...]).astype(kr_ref.dtype)

    tab = pl.BlockSpec((t, 128), lambda i: (i, 0))
    return pl.pallas_call(
        body, name=name, grid=(s // t,),
        in_specs=[pl.BlockSpec((t, P_SMALL), lambda i: (i, 0)), pl.BlockSpec((1, Q_LORA), lambda i: (0, 0)),
                  pl.BlockSpec((1, KV_LORA), lambda i: (0, 0)), tab, tab, tab],
        out_specs=[pl.BlockSpec((t, Q_LORA), lambda i: (i, 0)), pl.BlockSpec((t, KV_LORA), lambda i: (i, 0)), tab],
        out_shape=[jax.ShapeDtypeStruct((s, Q_LORA), BF), jax.ShapeDtypeStruct((s, KV_LORA), BF),
                   jax.ShapeDtypeStruct((s, 128), BF)],
        compiler_params=_params(1))(proj, g_q, g_kv, cc, sa, sb)


def latent_bwd(proj, g_q, g_kv, dqn, dkvn, dkr, cc, sa, sb, *, name, t=1024):
    s = proj.shape[0]
    t = _pick(s, t)

    def body(p_ref, gq_ref, gkv_ref, dqn_ref, dkvn_ref, dkr_ref, cc_ref, sa_ref, sb_ref, d_ref, dgq_ref, dgkv_ref):
        dq, dgq = _rms_bwd(p_ref[:, LAT_Q], gq_ref[...], dqn_ref[...])
        dkv, dgkv = _rms_bwd(p_ref[:, LAT_KV], gkv_ref[...], dkvn_ref[...])
        d_ref[:, LAT_Q] = dq.astype(d_ref.dtype)
        d_ref[:, LAT_KV] = dkv.astype(d_ref.dtype)
        d_ref[:, LAT_KR] = _unrope(dkr_ref[...], cc_ref[...], sa_ref[...], sb_ref[...]).astype(d_ref.dtype)
        _accumulate(dgq_ref, dgq)
        _accumulate(dgkv_ref, dgkv)

    tab = pl.BlockSpec((t, 128), lambda i: (i, 0))
    small = pl.BlockSpec((t, P_SMALL), lambda i: (i, 0))
    gq = pl.BlockSpec((1, Q_LORA), lambda i: (0, 0))
    gkv = pl.BlockSpec((1, KV_LORA), lambda i: (0, 0))
    return pl.pallas_call(
        body, name=name, grid=(s // t,),
        in_specs=[small, gq, gkv, pl.BlockSpec((t, Q_LORA), lambda i: (i, 0)),
                  pl.BlockSpec((t, KV_LORA), lambda i: (i, 0)), tab, tab, tab, tab],
        out_specs=[small, gq, gkv],
        out_shape=[jax.ShapeDtypeStruct((s, P_SMALL), BF), jax.ShapeDtypeStruct((1, Q_LORA), F32),
                   jax.ShapeDtypeStruct((1, KV_LORA), F32)],
        compiler_params=_params(1))(proj, g_q, g_kv, dqn, dkvn, dkr, cc, sa, sb)


def mla_out_dx_gate(dx, w_out, o, proj, *, name, tq):
    s, d = dx.shape
    nq = s // tq

    def body(dx_ref, w_ref, o_ref, p_ref, do_ref, dz_ref, dl_ref):
        dxv = dx_ref[...].astype(BF)
        lane = lax.broadcasted_iota(jnp.int32, (tq, 128), 1)
        deltas = jnp.zeros((tq, 128), F32)
        for c in _chunks(MLA_WIDTH):
            dy_c = lax.dot_general(dxv, w_ref[c, :], NT, preferred_element_type=F32)
            for h in range(c.start // V_DIM, c.stop // V_DIM):
                hc = slice(h * V_DIM, (h + 1) * V_DIM)
                z = p_ref[:, slice(P_Z + hc.start, P_Z + hc.stop)]
                sig = _sigmoid(z)
                dyv = dy_c[:, hc.start - c.start:hc.stop - c.start]
                ov = o_ref[:, hc]
                dov = dyv * (z * sig)
                do_ref[:, hc] = dov.astype(do_ref.dtype)
                dz_ref[:, hc] = (dyv * ov * (sig * (1.0 + z * (1.0 - sig)))).astype(dz_ref.dtype)
                deltas = jnp.where(lane == h, jnp.sum(dov * ov, axis=-1, keepdims=True), deltas)
        rows = deltas.T
        for h in range(N_HEADS):
            dl_ref[h] = jnp.broadcast_to(rows[h:h + 1, :], (8, tq))

    blk = pl.BlockSpec((tq, MLA_WIDTH), lambda i: (i, 0))
    return pl.pallas_call(
        body, name=name, grid=(nq,),
        in_specs=[pl.BlockSpec((tq, d), lambda i: (i, 0)), pl.BlockSpec((MLA_WIDTH, d), lambda i: (0, 0)),
                  blk, pl.BlockSpec((tq, P_WIDTH), lambda i: (i, 0))],
        out_specs=[blk, blk, pl.BlockSpec((N_HEADS, None, 8, tq), lambda i: (0, i, 0, 0))],
        out_shape=[jax.ShapeDtypeStruct((s, MLA_WIDTH), BF), jax.ShapeDtypeStruct((s, MLA_WIDTH), BF),
                   jax.ShapeDtypeStruct((N_HEADS, nq, 8, tq), F32)],
        compiler_params=_params(1))(dx, w_out, o, proj)


FWD_GROUPS = (4, 3, 2, 1)
BWD_GROUPS = (4, 3, 2, 1)


def _for_groups(first, count, groups, fn):
    lead = groups[-1]
    for g in groups[:-1][::-1]:
        lead = jnp.where(count >= g, g, lead)
    for g in groups:
        @pl.when(lead == g)
        def _(g=g):
            fn(first, g, True)
    first = first + lead
    count = count - lead
    for g in groups:
        n = count // g

        def one(p, carry, g=g, first=first):
            fn(first + p * g, g, False)
            return carry

        lax.fori_loop(0, n, one, 0)
        first = first + n * g
        count = count - n * g


def attn_fwd(qr, kv, krr, proj, *, name, tq):
    s = qr.shape[0]
    nq = s // tq
    z_blk = P_Z // V_DIM

    def body(kn_ref, v_ref, kr_ref, q_ref, z_ref, o_ref, y_ref, lse_ref, acc_sc, m_sc):
        j = pl.program_id(1)

        @pl.when(j == 0)
        def _():
            acc_sc[...] = jnp.zeros((nq, 2 * V_DIM, tq), F32)
            m_sc[...] = jnp.full((nq, 8, tq), NEG, F32)

        k = jnp.concatenate([kn_ref[...], kr_ref[...]], axis=1)
        vxt = jnp.concatenate([v_ref[...].astype(F32).T.astype(BF), jnp.ones((V_DIM, tq), BF)], axis=0)

        def update(i, n_tiles, masked):
            rows = pl.ds(pl.multiple_of(i * tq, tq), n_tiles * tq)
            st = lax.dot_general(k, q_ref[rows, :], NT, preferred_element_type=F32) * SCALE_LOG2E
            if masked:
                krow = lax.broadcasted_iota(jnp.int32, (tq, n_tiles * tq), 0)
                qcol = lax.broadcasted_iota(jnp.int32, (tq, n_tiles * tq), 1)
                st = jnp.where(qcol >= krow, st, NEG)
            m_prev = jnp.concatenate([m_sc[i + n, pl.ds(0, 1), :] for n in range(n_tiles)], axis=1)
            m_new = jnp.maximum(m_prev, jnp.max(st, axis=0, keepdims=True))
            alpha = jnp.exp2(m_prev - m_new)
            pt = jnp.exp2(st - m_new).astype(BF)
            pv_t = jnp.dot(vxt, pt, preferred_element_type=F32)
            for n in range(n_tiles):
                cols = slice(n * tq, (n + 1) * tq)
                acc_sc[i + n] = alpha[:, cols] * acc_sc[i + n] + pv_t[:, cols]
                m_sc[i + n, pl.ds(0, 1), :] = m_new[:, cols]

        _for_groups(j, nq - j, FWD_GROUPS, update)
        l = acc_sc[j, V_DIM:, :]
        o = (acc_sc[j, :V_DIM, :] / l).T
        o_ref[...] = o
        z = z_ref[...]
        y_ref[...] = (o * (z * _sigmoid(z))).astype(y_ref.dtype)
        lse_ref[...] = m_sc[j, pl.ds(0, 1), :] + jnp.log2(l[:8, :])

    tile = pl.BlockSpec((tq, V_DIM), lambda h, j: (j, h))
    return pl.pallas_call(
        body, name=name, grid=(N_HEADS, nq),
        in_specs=[pl.BlockSpec((tq, QK_NOPE), lambda h, j: (j, 2 * h)),
                  pl.BlockSpec((tq, V_DIM), lambda h, j: (j, 2 * h + 1)),
                  pl.BlockSpec((tq, 128), lambda h, j: (j, 0)),
                  pl.BlockSpec((s, HEAD_PAD), lambda h, j: (0, h)),
                  pl.BlockSpec((tq, V_DIM), lambda h, j: (j, z_blk + h))],
        out_specs=[tile, tile, pl.BlockSpec((None, None, 8, tq), lambda h, j: (h, j, 0, 0))],
        out_shape=[jax.ShapeDtypeStruct((s, N_HEADS * V_DIM), F32),
                   jax.ShapeDtypeStruct((s, N_HEADS * V_DIM), BF),
                   jax.ShapeDtypeStruct((N_HEADS, nq, 8, tq), F32)],
        scratch_shapes=[pltpu.VMEM((nq, 2 * V_DIM, tq), F32), pltpu.VMEM((nq, 8, tq), F32)],
        compiler_params=_params(2))(kv, kv, krr, qr, proj)


def attn_bwd(qr, kv, krr, do, lse, delta, cc, sa, sb, *, name, tq):
    s = qr.shape[0]
    nq = s // tq

    def body(kn_ref, v_ref, kr_ref, q_ref, do_ref, lse_ref, dl_ref, cc_ref, sa_ref, sb_ref,
             dkv_ref, dkr_ref, dq_ref, dq_sc, dk_sc, dv_sc):
        h = pl.program_id(0)
        j = pl.program_id(1)

        @pl.when(j == 0)
        def _():
            dq_sc[...] = jnp.zeros((s, HEAD_PAD), F32)

        dk_sc[...] = jnp.zeros((tq, HEAD_PAD), F32)
        dv_sc[...] = jnp.zeros((tq, V_DIM), F32)
        k = jnp.concatenate([kn_ref[...], kr_ref[...]], axis=1)
        v = v_ref[...]

        def step(i, n_tiles, masked):
            r0 = pl.multiple_of(i * tq, tq)
            rows = pl.ds(r0, n_tiles * tq)
            q = q_ref[rows, :]
            dov = do_ref[rows, :]
            lse_row = jnp.concatenate([lse_ref[i + n, pl.ds(0, 1), :] for n in range(n_tiles)], axis=1)
            dl_row = jnp.concatenate([dl_ref[i + n, pl.ds(0, 1), :] for n in range(n_tiles)], axis=1)
            st = lax.dot_general(k, q, NT, preferred_element_type=F32) * SCALE_LOG2E
            if masked:
                krow = lax.broadcasted_iota(jnp.int32, (tq, n_tiles * tq), 0)
                qcol = lax.broadcasted_iota(jnp.int32, (tq, n_tiles * tq), 1)
                st = jnp.where(qcol >= krow, st, NEG)
            pt = jnp.exp2(st - lse_row)
            dpt = lax.dot_general(v, dov, NT, preferred_element_type=F32)
            dst = (pt * (dpt - dl_row)).astype(BF)
            dv_sc[...] += jnp.dot(pt.astype(BF), dov, preferred_element_type=F32)
            dk_sc[...] += jnp.dot(dst, q, preferred_element_type=F32)
            dq_sc[rows, :] += lax.dot_general(dst, k, TN, preferred_element_type=F32)

        _for_groups(j, nq - j, BWD_GROUPS, step)
        dkv_ref[:, :QK_NOPE] = (dk_sc[:, :QK_NOPE] * SCALE).astype(dkv_ref.dtype)
        dkv_ref[:, QK_NOPE:] = dv_sc[...].astype(dkv_ref.dtype)
        mine = pl.ds(pl.multiple_of(j * tq, tq), tq)
        dkr = dk_sc[:, QK_NOPE:] * SCALE

        @pl.when(h == 0)
        def _():
            dkr_ref[mine, :] = dkr

        @pl.when(h > 0)
        def _():
            dkr_ref[mine, :] += dkr

        dq_ref[:, :QK_NOPE] = (dq_sc[mine, :QK_NOPE] * SCALE).astype(dq_ref.dtype)
        dq_ref[:, QK_NOPE:] = _unrope(dq_sc[mine, QK_NOPE:] * SCALE, cc_ref[...], sa_ref[...],
                                      sb_ref[...]).astype(dq_ref.dtype)

    rows = pl.BlockSpec((None, nq, 8, tq), lambda h, j: (h, 0, 0, 0))
    tab = pl.BlockSpec((tq, 128), lambda h, j: (j, 0))
    return pl.pallas_call(
        body, name=name, grid=(N_HEADS, nq),
        in_specs=[pl.BlockSpec((tq, QK_NOPE), lambda h, j: (j, 2 * h)),
                  pl.BlockSpec((tq, V_DIM), lambda h, j: (j, 2 * h + 1)), tab,
                  pl.BlockSpec((s, HEAD_PAD), lambda h, j: (0, h)),
                  pl.BlockSpec((s, V_DIM), lambda h, j: (0, h)), rows, rows, tab, tab, tab],
        out_specs=[pl.BlockSpec((tq, 256), lambda h, j: (j, h)),
                   pl.BlockSpec((s, 128), lambda h, j: (0, 0)),
                   pl.BlockSpec((tq, HEAD_PAD), lambda h, j: (j, h))],
        out_shape=[jax.ShapeDtypeStruct((s, N_HEADS * 256), BF),
                   jax.ShapeDtypeStruct((s, 128), F32),
                   jax.ShapeDtypeStruct((s, N_HEADS * HEAD_PAD), BF)],
        scratch_shapes=[pltpu.VMEM((s, HEAD_PAD), F32), pltpu.VMEM((tq, HEAD_PAD), F32),
                        pltpu.VMEM((tq, V_DIM), F32)],
        compiler_params=_params(2))(kv, kv, krr, qr, do, lse, delta, cc, sa, sb)


def adamw(w, g, m, v, *, name, t=512):
    r, c = w.shape
    t = r if r % t else t
    c1 = 1.0 - ADAM_B1 ** ADAM_STEP
    c2 = 1.0 - ADAM_B2 ** ADAM_STEP

    def body(w_ref, g_ref, m_ref, v_ref, d_ref, nm_ref, nv_ref):
        gv = g_ref[...]
        nm = ADAM_B1 * m_ref[...] + (1.0 - ADAM_B1) * gv
        nv = ADAM_B2 * v_ref[...] + (1.0 - ADAM_B2) * (gv * gv)
        nm_ref[...] = nm
        nv_ref[...] = nv
        d_ref[...] = -ADAM_LR * ((nm / c1) / (jnp.sqrt(nv / c2) + ADAM_EPS) + ADAM_WD * w_ref[...])

    blk = pl.BlockSpec((t, c), lambda i: (i, 0))
    return pl.pallas_call(
        body, name=name, grid=(r // t,), in_specs=[blk] * 4, out_specs=[blk] * 3,
        out_shape=[jax.ShapeDtypeStruct((r, c), F32)] * 3, compiler_params=_params(1))(w, g, m, v)


def sum_devices(parts, *, name):
    def body(p_ref, o_ref):
        acc = p_ref[pl.ds(0, SV_ROWS), :]
        for d in range(1, 8):
            acc = acc + p_ref[pl.ds(d * SV_ROWS, SV_ROWS), :]
        o_ref[...] = acc

    return pl.pallas_call(body, name=name, out_shape=jax.ShapeDtypeStruct((SV_ROWS, SV_COLS), F32))(parts)


def add_halves(g, rb, c_idx, *, name, rows):
    nq, r2, cc = rb.shape
    nb = r2 // rows

    def body(c_ref, g_ref, r_ref, o_ref):
        o_ref[...] = (g_ref[...] + r_ref[...]).astype(o_ref.dtype)

    grid_spec = pltpu.PrefetchScalarGridSpec(
        num_scalar_prefetch=1, grid=(nq, nb),
        in_specs=[pl.BlockSpec((None, rows, cc), lambda q, i, c: (q, c[0] * nb + i, 0)),
                  pl.BlockSpec((None, rows, cc), lambda q, i, c: (q, i, 0))],
        out_specs=pl.BlockSpec((None, rows, cc), lambda q, i, c: (q, i, 0)))
    return pl.pallas_call(body, name=name, grid_spec=grid_spec,
                          out_shape=jax.ShapeDtypeStruct((nq, r2, cc), BF),
                          compiler_params=_params(2))(c_idx, g, rb)


def sum_chips(rc, c_idx, *, name, rows):
    nq, r2, cc = rc.shape
    nb = r2 // rows

    def body(c_ref, r_ref, o_ref):
        parts = [r_ref[q].astype(F32) for q in range(4)]
        o_ref[...] = ((parts[0] + parts[1]) + parts[2]) + parts[3]

    grid_spec = pltpu.PrefetchScalarGridSpec(
        num_scalar_prefetch=1, grid=(nb,),
        in_specs=[pl.BlockSpec((nq, rows, cc), lambda i, c: (0, i, 0))],
        out_specs=pl.BlockSpec((rows, cc), lambda i, c: (c[0] * nb + i, 0)))
    return pl.pallas_call(body, name=name, grid_spec=grid_spec,
                          out_shape=jax.ShapeDtypeStruct((2 * r2, cc), F32),
                          compiler_params=_params(1))(c_idx, rc)


def _place():
    return lax.axis_index("x"), lax.axis_index("y"), lax.axis_index("c")


def all_gather8(xs, *, name, own_half):
    m = xs.shape[0] // 2 if own_half else xs.shape[0]
    n = xs.shape[1]

    def body(x_ref, out_ref, send_sems, recv_sems, local_sem):
        x, y, c = _place()
        me, sibling = (x, y, c), (x, y, 1 - c)
        chips = [(1 - x, y), (x, 1 - y), (1 - x, 1 - y)]
        src_own = x_ref.at[pl.ds(c * m, m), :] if own_half else x_ref

        def rows(px, py, pc):
            return out_ref.at[pl.ds((4 * px + 2 * py + pc) * m, m), :]

        def copy(k, block, to, src=None):
            return pltpu.make_async_remote_copy(
                src_ref=rows(*block) if src is None else src, dst_ref=rows(*block),
                send_sem=send_sems.at[k], recv_sem=recv_sems.at[k], device_id=to, device_id_type=MESH)

        mine = pltpu.make_async_copy(src_own, rows(*me), local_sem)
        mine.start()
        first = [copy(0, me, sibling, src=src_own)]
        first += [copy(1 + j, me, (*chip, c), src=src_own) for j, chip in enumerate(chips)]
        for cp in first:
            cp.start()
        passed = [copy(4 + j, (*chip, c), sibling) for j, chip in enumerate(chips)]
        for j, chip in enumerate(chips):
            copy(1 + j, (*chip, c), me).wait_recv()
            passed[j].start()
        copy(0, sibling, me).wait_recv()
        for j, chip in enumerate(chips):
            copy(4 + j, (*chip, 1 - c), me).wait_recv()
        for cp in first + passed:
            cp.wait_send()
        mine.wait()

    return pl.pallas_call(
        body, name=name, out_shape=jax.ShapeDtypeStruct((8 * m, n), xs.dtype),
        in_specs=[pl.BlockSpec(memory_space=pl.ANY)], out_specs=pl.BlockSpec(memory_space=pl.ANY),
        scratch_shapes=[pltpu.SemaphoreType.DMA((7,)), pltpu.SemaphoreType.DMA((7,)), pltpu.SemaphoreType.DMA],
    )(xs)


def _other_chips():
    x, y, c = _place()
    return [(1 - x, y), (x, 1 - y), (1 - x, 1 - y)]


def _remote(src, dst, send_sems, recv_sems, k, to):
    return pltpu.make_async_remote_copy(src_ref=src, dst_ref=dst, send_sem=send_sems.at[k], recv_sem=recv_sems.at[k],
                                        device_id=to, device_id_type=MESH)


def gather_ici(xs):
    r, cc = xs.shape
    m = r // 2

    def copies(ins, outs, ss, rs, landing):
        x, y, c = _place()
        half = pl.ds(c * m, m)
        return [_remote(ins[0].at[half, :], outs[0].at[(2 * cx + cy) if landing else (2 * x + y), half, :],
                        ss, rs, j, (cx, cy, c)) for j, (cx, cy) in enumerate(_other_chips())]

    def start(ins, outs, ss, rs, ls):
        for cp in copies(ins, outs, ss, rs, False):
            cp.start()

    def wait(ins, outs, ss, rs, ls):
        for cp in copies(ins, outs, ss, rs, True):
            cp.wait_recv()
        for cp in copies(ins, outs, ss, rs, False):
            cp.wait_send()

    return Exchange((xs,), (jax.ShapeDtypeStruct((4, r, cc), xs.dtype),), {}, 3, start, wait)


def gather_forward(buf):
    m = buf.shape[1] // 2

    def copies(outs, ss, rs, landing):
        x, y, c = _place()
        half = pl.ds(((1 - c) if landing else c) * m, m)
        return [_remote(outs[0].at[2 * cx + cy, half, :], outs[0].at[2 * cx + cy, half, :], ss, rs, j, (x, y, 1 - c))
                for j, (cx, cy) in enumerate(_other_chips())]

    def start(ins, outs, ss, rs, ls):
        for cp in copies(outs, ss, rs, False):
            cp.start()

    def wait(ins, outs, ss, rs, ls):
        for cp in copies(outs, ss, rs, True):
            cp.wait_recv()
        for cp in copies(outs, ss, rs, False):
            cp.wait_send()

    return Exchange((buf,), (jax.ShapeDtypeStruct(buf.shape, buf.dtype),), {0: 0}, 3, start, wait)


def swap_halves(g):
    nq, r, cc = g.shape
    r2 = r // 2

    def copy(ins, outs, ss, rs):
        x, y, c = _place()
        return _remote(ins[0].at[:, pl.ds((1 - c) * r2, r2), :], outs[0], ss, rs, 0, (x, y, 1 - c))

    def start(ins, outs, ss, rs, ls):
        copy(ins, outs, ss, rs).start()

    def wait(ins, outs, ss, rs, ls):
        copy(ins, outs, ss, rs).wait()

    return Exchange((g,), (jax.ShapeDtypeStruct((nq, r2, cc), g.dtype),), {}, 1, start, wait)


def exchange_chips(p):
    def own(ins, outs, ls):
        x, y, c = _place()
        return pltpu.make_async_copy(ins[0].at[2 * x + y], outs[0].at[2 * x + y], ls)

    def copies(ins, outs, ss, rs, landing):
        x, y, c = _place()
        return [_remote(ins[0].at[2 * cx + cy], outs[0].at[(2 * cx + cy) if landing else (2 * x + y)],
                        ss, rs, j, (cx, cy, c)) for j, (cx, cy) in enumerate(_other_chips())]

    def start(ins, outs, ss, rs, ls):
        own(ins, outs, ls).start()
        for cp in copies(ins, outs, ss, rs, False):
            cp.start()

    def wait(ins, outs, ss, rs, ls):
        for cp in copies(ins, outs, ss, rs, True):
            cp.wait_recv()
        for cp in copies(ins, outs, ss, rs, False):
            cp.wait_send()
        own(ins, outs, ls).wait()

    return Exchange((p,), (jax.ShapeDtypeStruct(p.shape, p.dtype),), {}, 3, start, wait)


def join_halves(tot):
    r2 = tot.shape[0] // 2

    def copy(outs, ss, rs, landing):
        x, y, c = _place()
        half = outs[0].at[pl.ds(((1 - c) if landing else c) * r2, r2), :]
        return _remote(half, half, ss, rs, 0, (x, y, 1 - c))

    def start(ins, outs, ss, rs, ls):
        copy(outs, ss, rs, False).start()

    def wait(ins, outs, ss, rs, ls):
        copy(outs, ss, rs, True).wait_recv()
        copy(outs, ss, rs, False).wait_send()

    return Exchange((tot,), (jax.ShapeDtypeStruct(tot.shape, tot.dtype),), {0: 0}, 1, start, wait)


def _pack_shard(blocks, small_vec=None):
    parts = [w.reshape(-1, PACK_C).astype(BF) for w in blocks]
    if small_vec is not None:
        srow = lax.bitcast_convert_type(small_vec, BF).reshape(1, PACK_C)
        parts.append(jnp.pad(srow, ((0, PACK_PAD - 1), (0, 0))))
    return jnp.concatenate(parts, axis=0)


def _split_rows(a, rows, axis):
    out, off = [], 0
    for n in rows:
        out.append(lax.slice_in_dim(a, off, off + n, axis=axis))
        off += n
    return out


def _unpack_pool(gw):
    p_in, p_grp, p_out = _split_rows(gw, POOL_ROWS, 1)
    return dict(
        pool_w_in=p_in.reshape(4, D_MODEL, 1024).transpose(1, 0, 2).reshape(D_MODEL, 2 * POOL_WIDTH),
        pool_w_group=p_grp.reshape(4, 4, 128, POOL_GROUP).transpose(1, 0, 2, 3).reshape(4, POOL_GROUP, POOL_GROUP),
        pool_w_out=p_out.reshape(POOL_WIDTH, D_MODEL))


def _unpack_mla(gw):
    m_in, m_qb, m_kvb, m_out, small = _split_rows(gw, MLA_ROWS + (PACK_PAD,), 1)
    w = {}
    win = m_in.reshape(4, D_MODEL, 688).transpose(1, 0, 2).reshape(D_MODEL, 2752)
    w["mla_w_in"] = jnp.concatenate(
        [win[:, 384:640], win[:, 640:704], jnp.zeros((D_MODEL, 64), BF), win[:, 0:384], win[:, 704:]], axis=1)
    wq = m_qb.reshape(4, Q_LORA, 768).transpose(1, 0, 2).reshape(Q_LORA, N_HEADS, QK_NOPE + QK_ROPE)
    w["mla_w_q_b"] = jnp.pad(wq, ((0, 0), (0, 0), (0, HEAD_PAD - QK_NOPE - QK_ROPE))).reshape(Q_LORA, N_HEADS * HEAD_PAD)
    w["mla_w_kv_b"] = m_kvb.reshape(4, KV_LORA, 1024).transpose(1, 0, 2).reshape(KV_LORA, 4096)
    w["mla_w_out"] = m_out.reshape(MLA_WIDTH, D_MODEL)
    small = lax.bitcast_convert_type(small[:, 0, :].reshape(4, 512, 2), F32)
    w["mla_norm"] = small[:, :256].reshape(1, D_MODEL)
    w["mla_q_norm"] = small[:, 256:352].reshape(1, Q_LORA)
    w["mla_kv_norm"] = small[:, 352:416].reshape(1, KV_LORA)
    return w


def _pack_pool_grads(g):
    return jnp.concatenate([
        g["pool_w_in"],
        g["pool_w_group"].reshape(4, 4, 128, POOL_GROUP).transpose(1, 0, 2, 3).reshape(4, 256, PACK_C),
        g["pool_w_out"].reshape(4, 512, PACK_C)], axis=1)


def _pack_mla_grads(g):
    return jnp.concatenate([
        g["mla_w_in"].reshape(D_MODEL, 4, 688).transpose(1, 0, 2).reshape(4, 688, PACK_C),
        g["mla_w_q_b"].reshape(Q_LORA, 4, 768).transpose(1, 0, 2).reshape(4, 288, PACK_C),
        g["mla_w_kv_b"],
        g["mla_w_out"].reshape(4, 512, PACK_C),
        jnp.zeros((4, PACK_PAD, PACK_C), F32)], axis=1)


def kernel(x, positions, pool_norm, pool_w_in, pool_w_group, pool_scale, pool_w_out, mla_norm, mla_w_in, mla_q_norm, mla_w_q_b, mla_kv_norm, mla_w_kv_b, mla_w_out, final_norm, loss_target, m_pool_norm, m_pool_w_in, m_pool_w_group, m_pool_scale, m_pool_w_out, m_mla_norm, m_mla_w_in, m_mla_q_norm, m_mla_w_q_b, m_mla_kv_norm, m_mla_w_kv_b, m_mla_w_out, m_final_norm, v_pool_norm, v_pool_w_in, v_pool_w_group, v_pool_scale, v_pool_w_out, v_mla_norm, v_mla_w_in, v_mla_q_norm, v_mla_w_q_b, v_mla_kv_norm, v_mla_w_kv_b, v_mla_w_out, v_final_norm):
    s = x.shape[1]
    tq = min(512, s)
    x0 = x.reshape(s, D_MODEL)
    tgt = loss_target.reshape(s, D_MODEL)
    cx, cy, cc_idx = _place()
    chip = 2 * cx + cy

    big_names = ("pool_w_in", "pool_w_group", "pool_w_out", "mla_w_in", "mla_w_q_b", "mla_w_kv_b", "mla_w_out")
    big_w = dict(zip(big_names, (pool_w_in, pool_w_group, pool_w_out, mla_w_in, mla_w_q_b, mla_w_kv_b, mla_w_out)))
    big_m = dict(zip(big_names, (m_pool_w_in, m_pool_w_group, m_pool_w_out, m_mla_w_in, m_mla_w_q_b, m_mla_w_kv_b, m_mla_w_out)))
    big_v = dict(zip(big_names, (v_pool_w_in, v_pool_w_group, v_pool_w_out, v_mla_w_in, v_mla_w_q_b, v_mla_w_kv_b, v_mla_w_out)))

    small_vec = jnp.concatenate([mla_norm.reshape(-1), mla_q_norm.reshape(-1), mla_kv_norm.reshape(-1),
                                 jnp.zeros((96,), F32)])
    pool_packed = _pack_shard([big_w[n] for n in big_names[:3]])
    mla_packed = _pack_shard([big_w[n] for n in big_names[3:]], small_vec)
    w = _unpack_pool(all_gather8(pool_packed, name="gather_pool_weights", own_half=True).reshape(4, POOL_R, PACK_C))
    g_pool = pool_norm.reshape(1, D_MODEL)
    g_final = final_norm.reshape(1, D_MODEL)
    sc_pool = pool_scale.reshape(1, POOL_WIDTH)

    inv_freq = 1.0 / (ROPE_THETA ** (jnp.arange(0, QK_ROPE, 2, dtype=F32) / QK_ROPE))
    ang = positions.reshape(s).astype(F32)[:, None] * inv_freq
    cos, sin = jnp.cos(ang), jnp.sin(ang)
    z32, z64, z96 = (jnp.zeros((s, n), F32) for n in (32, 64, 96))
    t_cc = jnp.concatenate([cos, cos, z64], axis=1)
    t_sa = jnp.concatenate([-sin, z96], axis=1)
    t_sb = jnp.concatenate([z32, sin, z64], axis=1)

    h0 = norm_fwd(x0, g_pool, name="pool_norm_fwd")
    uz, mla_land = mm_nn(h0, w["pool_w_in"], name="pool_in_proj", out_dtype=F32, host=gather_ici(mla_packed))
    pd = pool_prep(uz, name="pool_window")
    y1 = pool_mix_gate(pd, w["pool_w_group"], uz, sc_pool, name="pool_group_mix")
    x1, mla_land = mm_nn(y1, w["pool_w_out"], name="pool_out_proj", out_dtype=F32, add=x0,
                         host=gather_forward(mla_land))
    w.update(_unpack_mla(lax.dynamic_update_slice_in_dim(mla_land, mla_packed[None], chip, axis=0)))

    h1 = norm_fwd(x1, w["mla_norm"], name="mla_norm_fwd")
    proj = mm_nn(h1, w["mla_w_in"], name="mla_in_proj", out_dtype=F32, tn=P_WIDTH // 2)
    qn, kvn, krr = latent_fwd(proj, w["mla_q_norm"], w["mla_kv_norm"], t_cc, t_sa, t_sb, name="mla_latent_fwd")
    qr = q_proj_rope(qn, w["mla_w_q_b"], t_cc, t_sa, t_sb, name="mla_q_proj")
    kv = mm_nn(kvn, w["mla_w_kv_b"], name="mla_kv_proj", out_dtype=BF, tn=2048)
    o, y2, lse = attn_fwd(qr, kv, krr, proj, name="mla_attn_fwd", tq=tq)
    dx2, d_final, loss_part = mm_nn_loss(y2, w["mla_w_out"], x1, g_final, tgt, name="mla_out_proj_loss")

    grads = {}
    grads["mla_w_out"] = mm_tn(y2, dx2, name="mla_out_proj_dw")
    do, dz2, delta = mla_out_dx_gate(dx2, w["mla_w_out"], o, proj, name="mla_out_proj_dx", tq=tq)
    dkv, dkr, dq_pre = attn_bwd(qr, kv, krr, do, lse, delta, t_cc, t_sa, t_sb, name="mla_attn_bwd", tq=tq)
    dqn = mm_nt(dq_pre, w["mla_w_q_b"], name="mla_q_proj_dx", out_dtype=F32, tn=Q_LORA, tk=4096)
    g_qb = mm_tn(qn, dq_pre, name="mla_q_proj_dw", tm=Q_LORA, tn=2048)
    dkvn = mm_nt(dkv, w["mla_w_kv_b"], name="mla_kv_proj_dx", out_dtype=F32, tn=KV_LORA, tk=4096)
    grads["mla_w_kv_b"] = mm_tn(kvn, dkv, name="mla_kv_proj_dw", tm=KV_LORA, tk=4096, by_column_block=True)
    dsmall, d_qnorm, d_kvnorm = latent_bwd(proj, w["mla_q_norm"], w["mla_kv_norm"], dqn, dkvn, dkr,
                                           t_cc, t_sa, t_sb, name="mla_latent_bwd")
    dx1, d_mnorm = mm_nt_norm_bwd(dz2, w["mla_w_in"][:, P_Z:], (dsmall, w["mla_w_in"]), x1, w["mla_norm"], dx2,
                                  name="mla_in_proj_dx")
    g_in_a = mm_tn(h1, dsmall, name="mla_in_proj_dw_a", tn=P_SMALL, tk=4096)
    g_in_b = mm_tn(h1, dz2, name="mla_in_proj_dw_b", tk=4096)

    g_in = jnp.concatenate([g_in_a, g_in_b], axis=1)
    grads["mla_w_in"] = jnp.concatenate([g_in[:, P_Q:P_Z], g_in[:, P_KV:P_KV + KV_LORA],
                                         g_in[:, P_KR:P_KR + QK_ROPE], g_in[:, P_Z:]], axis=1)
    grads["mla_w_q_b"] = g_qb.reshape(Q_LORA, N_HEADS, HEAD_PAD)[:, :, :QK_NOPE + QK_ROPE].reshape(Q_LORA, -1)
    core_idx = cc_idx.reshape(1).astype(jnp.int32)
    gp_mla = _pack_mla_grads(grads)

    grads["pool_w_out"], sib = mm_tn(y1, dx1, name="pool_out_proj_dw", host=swap_halves(gp_mla))
    pre = add_halves(gp_mla, sib, core_idx, name="mla_grad_add_halves", rows=MLA_R // 2)
    dmm, dz1, d_scale, got = pool_out_dx_gate(dx1, w["pool_w_out"], pd, w["pool_w_group"], uz, sc_pool,
                                              name="pool_out_proj_dx", host=exchange_chips(pre))
    tot = sum_chips(got, core_idx, name="mla_grad_sum_chips", rows=MLA_R // 2)
    dpd, red_mla = gmm_nt(dmm, w["pool_w_group"], name="pool_group_mix_dx", host=join_halves(tot))
    grads["pool_w_group"] = gmm_tn(pd, dmm, 4, name="pool_group_mix_dw", tk=4096)
    du = pool_prep_bwd(dpd, name="pool_window_bwd")
    g_pin_u = mm_tn(h0, du, name="pool_in_proj_dw_u", tk=4096, by_column_block=True)
    g_pin_z = mm_tn(h0, dz1, name="pool_in_proj_dw_z", tk=4096, by_column_block=True)
    grads["pool_w_in"] = jnp.concatenate([g_pin_u, g_pin_z], axis=0)

    gp_pool = _pack_pool_grads(grads)
    dh0, sib = mm_nt(du, w["pool_w_in"], name="pool_in_proj_dx_u", out_dtype=F32, host=swap_halves(gp_pool))
    pre = add_halves(gp_pool, sib, core_idx, name="pool_grad_add_halves", rows=POOL_R // 2)
    grad_x, d_pnorm, got = mm_nt_norm_bwd(dz1, w["pool_w_in"], dh0, x0, g_pool, dx1, name="pool_in_proj_dx_z",
                                          b_col=POOL_WIDTH, host=exchange_chips(pre))
    tot = sum_chips(got, core_idx, name="pool_grad_sum_chips", rows=POOL_R // 2)
    red_pool = run_exchange(join_halves(tot), name="pool_grad_join_halves")[0]
    red_parts = _split_rows(red_pool, POOL_ROWS, 0) + _split_rows(red_mla, MLA_ROWS, 0)

    sv = jnp.concatenate([d_pnorm.reshape(-1), d_scale.reshape(-1), d_final.reshape(-1), d_mnorm.reshape(-1),
                          d_qnorm.reshape(-1), d_kvnorm.reshape(-1), loss_part[0, :1],
                          jnp.zeros((SV_ROWS * SV_COLS - SV_OFF["loss"] - 1,), F32)]).reshape(SV_ROWS, SV_COLS)
    sv_all = all_gather8(sv, name="gather_small_grads", own_half=False)
    sv_sum = sum_devices(sv_all, name="sum_small_grads").reshape(-1)
    loss = sv_sum[SV_OFF["loss"]]

    def sv_take(key, n):
        return lax.slice_in_dim(sv_sum, SV_OFF[key], SV_OFF[key] + n)

    out_g, out_d, out_m, out_v = {}, {}, {}, {}
    for name, part in zip(big_names, red_parts):
        shp = big_w[name].shape
        g2 = part.reshape(shp)
        two_d = (-1, shp[-1])
        d_, m_, v_ = adamw(big_w[name].reshape(two_d), g2.reshape(two_d), big_m[name].reshape(two_d),
                           big_v[name].reshape(two_d), name="adamw_" + name)
        out_g[name], out_d[name], out_m[name], out_v[name] = g2, d_.reshape(shp), m_.reshape(shp), v_.reshape(shp)

    small = [
        ("pool_norm", pool_norm, m_pool_norm, v_pool_norm, sv_take("pool_norm", 1024)),
        ("pool_scale", pool_scale, m_pool_scale, v_pool_scale, sv_take("pool_scale", 2048)),
        ("final_norm", final_norm, m_final_norm, v_final_norm, sv_take("final_norm", 1024)),
        ("mla_norm", mla_norm, m_mla_norm, v_mla_norm,
         lax.dynamic_slice_in_dim(sv_take("mla_norm", 1024), chip * 256, 256)),
        ("mla_q_norm", mla_q_norm, m_mla_q_norm, v_mla_q_norm,
         lax.dynamic_slice_in_dim(sv_take("q_norm", 384), chip * 96, 96)),
        ("mla_kv_norm", mla_kv_norm, m_mla_kv_norm, v_mla_kv_norm,
         lax.dynamic_slice_in_dim(sv_take("kv_norm", 256), chip * 64, 64)),
    ]
    sw = jnp.concatenate([t[1].reshape(-1) for t in small] + [jnp.zeros((96,), F32)]).reshape(1, -1)
    sm = jnp.concatenate([t[2].reshape(-1) for t in small] + [jnp.zeros((96,), F32)]).reshape(1, -1)
    s_v = jnp.concatenate([t[3].reshape(-1) for t in small] + [jnp.ones((96,), F32)]).reshape(1, -1)
    sg = jnp.concatenate([t[4].reshape(-1) for t in small] + [jnp.zeros((96,), F32)]).reshape(1, -1)
    sd_, sm_, sv_ = adamw(sw, sg, sm, s_v, name="adamw_vectors")
    off = 0
    for name, wt, _, _, gvec in small:
        n = gvec.shape[0]
        shp = wt.shape
        out_g[name] = gvec.reshape(shp)
        out_d[name] = sd_[0, off:off + n].reshape(shp)
        out_m[name] = sm_[0, off:off + n].reshape(shp)
        out_v[name] = sv_[0, off:off + n].reshape(shp)
        off += n

    order = ("pool_norm", "pool_w_in", "pool_w_group", "pool_scale", "pool_w_out", "mla_norm", "mla_w_in",
             "mla_q_norm", "mla_w_q_b", "mla_kv_norm", "mla_w_kv_b", "mla_w_out", "final_norm")
    return (loss, grad_x.reshape(x.shape), *[out_g[n] for n in order], *[out_d[n] for n in order],
            *[out_m[n] for n in order], *[out_v[n] for n in order])
```

```python
import functools
from typing import Callable, NamedTuple

import jax
import jax.numpy as jnp
from jax import lax
from jax.experimental import pallas as pl
from jax.experimental.pallas import tpu as pltpu

F32 = jnp.float32
BF = jnp.bfloat16
MESH = pl.DeviceIdType.MESH

D_MODEL = 1024
POOL_WIDTH = 2048
POOL_WINDOWS = (2, 4, 8, 16)
POOL_GROUP = 512
HALO = 16
N_HEADS = 16
QK_NOPE = 128
QK_ROPE = 64
V_DIM = 128
HEAD_PAD = 256
Q_LORA = 384
KV_LORA = 256
MLA_WIDTH = 2048
ROPE_THETA = 10000.0
EPS = 1e-6
SCALE = (QK_NOPE + QK_ROPE) ** -0.5
SCALE_LOG2E = SCALE * 1.4426950408889634
NEG = -1e30

P_KV, P_KR, P_Q, P_Z = 0, 256, 384, 768
P_SMALL = 768
P_WIDTH = 2816

ADAM_LR = 0.001
ADAM_B1 = 0.9
ADAM_B2 = 0.999
ADAM_EPS = 1e-08
ADAM_WD = 0.01
ADAM_STEP = 10

NN = (((1,), (0,)), ((), ()))
NT = (((1,), (1,)), ((), ()))
TN = (((0,), (0,)), ((), ()))

POOL_ROWS = (1024, 256, 512)
MLA_ROWS = (688, 288, 256, 512)
PACK_PAD = 16
POOL_R = sum(POOL_ROWS)
MLA_R = sum(MLA_ROWS) + PACK_PAD
PACK_C = 1024
SV_OFF = dict(pool_norm=0, pool_scale=1024, final_norm=3072, mla_norm=4096, q_norm=5120, kv_norm=5504, loss=5760)
SV_ROWS, SV_COLS = 8, 768

VMEM_LIMIT = 56 * 1024 * 1024


def _params(n_axes, vmem=None):
    return pltpu.CompilerParams(dimension_semantics=("arbitrary",) * n_axes,
                                vmem_limit_bytes=VMEM_LIMIT if vmem is None else vmem)


def _sigmoid(z):
    return 1.0 / (1.0 + jnp.exp(-z))


class Exchange(NamedTuple):
    operands: tuple
    out_shapes: tuple
    aliases: dict
    n_sems: int
    start: Callable
    wait: Callable


HBM_SPEC = pl.BlockSpec(memory_space=pl.ANY)


def _exchange_scratch(ex):
    return [pltpu.SemaphoreType.DMA((ex.n_sems,)), pltpu.SemaphoreType.DMA((ex.n_sems,)), pltpu.SemaphoreType.DMA]


def run_exchange(ex, *, name):
    n_in, n_out = len(ex.operands), len(ex.out_shapes)

    def body(*refs):
        args = (refs[:n_in], refs[n_in:n_in + n_out]) + tuple(refs[n_in + n_out:])
        ex.start(*args)
        ex.wait(*args)

    return pl.pallas_call(
        body, name=name, out_shape=list(ex.out_shapes), in_specs=[HBM_SPEC] * n_in,
        out_specs=[HBM_SPEC] * n_out, scratch_shapes=_exchange_scratch(ex),
        input_output_aliases=dict(ex.aliases))(*ex.operands)


def _call(core, *, name, grid, in_specs, out_specs, out_shape, args, scratch=(), host=None):
    in_specs, out_specs, out_shape = list(in_specs), list(out_specs), list(out_shape)
    params = _params(len(grid))
    if host is None:
        return pl.pallas_call(core, name=name, grid=grid, in_specs=in_specs, out_specs=out_specs,
                              out_shape=out_shape, scratch_shapes=list(scratch), compiler_params=params)(*args)
    n_in, n_out = len(in_specs), len(out_specs)
    n_hin, n_hout = len(host.operands), len(host.out_shapes)

    def body(*refs):
        ins, refs = refs[:n_in], refs[n_in:]
        h_in, refs = refs[:n_hin], refs[n_hin:]
        outs, refs = refs[:n_out], refs[n_out:]
        h_out, refs = refs[:n_hout], refs[n_hout:]
        own_scratch, sems = refs[:-3], refs[-3:]
        ids = [pl.program_id(ax) for ax in range(len(grid))]
        first = functools.reduce(jnp.logical_and, [i == 0 for i in ids])
        last = functools.reduce(jnp.logical_and, [i == n - 1 for i, n in zip(ids, grid)])

        @pl.when(first)
        def _():
            host.start(h_in, h_out, *sems)

        core(*ins, *outs, *own_scratch)

        @pl.when(last)
        def _():
            host.wait(h_in, h_out, *sems)

    return pl.pallas_call(
        body, name=name, grid=grid, in_specs=in_specs + [HBM_SPEC] * n_hin,
        out_specs=out_specs + [HBM_SPEC] * n_hout, out_shape=out_shape + list(host.out_shapes),
        scratch_shapes=list(scratch) + _exchange_scratch(host),
        input_output_aliases={n_in + i: n_out + o for i, o in host.aliases.items()},
        compiler_params=params)(*args, *host.operands)


def _mm(a, b, *, dims, grid, a_spec, b_spec, o_spec, out_shape, out_dtype, acc_shape, name,
        add=None, add_spec=None, host=None):
    nk = grid[-1]
    kax = len(grid) - 1

    def body(*refs):
        if add is None:
            a_ref, b_ref, o_ref = refs[:3]
            add_ref = None
            rest = refs[3:]
        else:
            a_ref, b_ref, add_ref, o_ref = refs[:4]
            rest = refs[4:]
        part = lax.dot_general(a_ref[...].astype(BF), b_ref[...].astype(BF), dims,
                               preferred_element_type=F32)

        def finish(r):
            if add_ref is not None:
                r = r + add_ref[...]
            o_ref[...] = r.astype(o_ref.dtype)

        if nk == 1:
            finish(part)
        else:
            acc = rest[0]
            k = pl.program_id(kax)

            @pl.when(k == 0)
            def _():
                acc[...] = part

            @pl.when(k > 0)
            def _():
                acc[...] += part

            @pl.when(k == nk - 1)
            def _():
                finish(acc[...])

    in_specs = [a_spec, b_spec]
    args = [a, b]
    if add is not None:
        in_specs.append(add_spec)
        args.append(add)
    out = _call(body, name=name, grid=grid, in_specs=in_specs, out_specs=[o_spec],
                out_shape=[jax.ShapeDtypeStruct(out_shape, out_dtype)], args=args,
                scratch=[] if nk == 1 else [pltpu.VMEM(acc_shape, F32)], host=host)
    return out[0] if host is None else out


def _pick(n, t):
    t = min(n, t)
    assert n % t == 0, (n, t)
    return t


def mm_nn(a, b, *, name, out_dtype, add=None, tm=1024, tn=1024, tk=2048, host=None):
    m = a.shape[0]
    kk, n = b.shape
    tm, tn, tk = _pick(m, tm), _pick(n, tn), _pick(kk, tk)
    return _mm(a, b, dims=NN, grid=(m // tm, n // tn, kk // tk),
               a_spec=pl.BlockSpec((tm, tk), lambda i, j, k: (i, k)),
               b_spec=pl.BlockSpec((tk, tn), lambda i, j, k: (k, j)),
               o_spec=pl.BlockSpec((tm, tn), lambda i, j, k: (i, j)),
               add=add, add_spec=pl.BlockSpec((tm, tn), lambda i, j, k: (i, j)),
               out_shape=(m, n), out_dtype=out_dtype, acc_shape=(tm, tn), name=name, host=host)


def mm_nt(a, b, *, name, out_dtype, b_col=0, add=None, tm=1024, tn=1024, tk=2048, host=None):
    m, kk = a.shape
    n = b.shape[0]
    tm, tn, tk = _pick(m, tm), _pick(n, tn), _pick(kk, tk)
    assert b_col % tk == 0
    ko = b_col // tk
    return _mm(a, b, dims=NT, grid=(m // tm, n // tn, kk // tk),
               a_spec=pl.BlockSpec((tm, tk), lambda i, j, k: (i, k)),
               b_spec=pl.BlockSpec((tn, tk), lambda i, j, k: (j, ko + k)),
               o_spec=pl.BlockSpec((tm, tn), lambda i, j, k: (i, j)),
               add=add, add_spec=pl.BlockSpec((tm, tn), lambda i, j, k: (i, j)),
               out_shape=(m, n), out_dtype=out_dtype, acc_shape=(tm, tn), name=name, host=host)


def mm_tn(a, b, *, name, tm=1024, tn=1024, tk=2048, host=None, by_column_block=False):
    s, m = a.shape
    n = b.shape[1]
    tm, tn, tk = _pick(m, tm), _pick(n, tn), _pick(s, tk)
    if by_column_block:
        out_shape, o_spec = (n // tn, m, tn), pl.BlockSpec((None, tm, tn), lambda i, j, k: (j, i, 0))
    else:
        out_shape, o_spec = (m, n), pl.BlockSpec((tm, tn), lambda i, j, k: (i, j))
    return _mm(a, b, dims=TN, grid=(m // tm, n // tn, s // tk),
               a_spec=pl.BlockSpec((tk, tm), lambda i, j, k: (k, i)),
               b_spec=pl.BlockSpec((tk, tn), lambda i, j, k: (k, j)),
               o_spec=o_spec, out_shape=out_shape, out_dtype=F32, acc_shape=(tm, tn), name=name, host=host)


def gmm_nt(a, w, *, name, tm=1024, host=None):
    s = a.shape[0]
    g, kk, n = w.shape
    tm = _pick(s, tm)
    return _mm(a, w, dims=NT, grid=(s // tm, g, 1),
               a_spec=pl.BlockSpec((tm, n), lambda i, gi, k: (i, gi)),
               b_spec=pl.BlockSpec((None, kk, n), lambda i, gi, k: (gi, 0, 0)),
               o_spec=pl.BlockSpec((tm, kk), lambda i, gi, k: (i, gi)),
               out_shape=(s, g * kk), out_dtype=F32, acc_shape=(tm, kk), name=name, host=host)


def gmm_tn(a, b, g, *, name, tk=2048):
    s = a.shape[0]
    kk, n = a.shape[1] // g, b.shape[1] // g
    tk = _pick(s, tk)
    return _mm(a, b, dims=TN, grid=(g, s // tk),
               a_spec=pl.BlockSpec((tk, kk), lambda gi, k: (k, gi)),
               b_spec=pl.BlockSpec((tk, n), lambda gi, k: (k, gi)),
               o_spec=pl.BlockSpec((None, kk, n), lambda gi, k: (gi, 0, 0)),
               out_shape=(g, kk, n), out_dtype=F32, acc_shape=(kk, n), name=name)


def _rms(xv, gv):
    inv = lax.rsqrt(jnp.mean(xv * xv, axis=-1, keepdims=True) + EPS)
    return (xv * inv) * gv


def _rms_bwd(xv, gv, dh):
    inv = lax.rsqrt(jnp.mean(xv * xv, axis=-1, keepdims=True) + EPS)
    xhat = xv * inv
    dxhat = dh * gv
    dx = inv * (dxhat - xhat * jnp.mean(dxhat * xhat, axis=-1, keepdims=True))
    return dx, jnp.sum(dh * xhat, axis=0, keepdims=True)


def norm_fwd(x, g, *, name, t=1024):
    s, width = x.shape
    t = _pick(s, t)

    def body(x_ref, g_ref, o_ref):
        o_ref[...] = _rms(x_ref[...], g_ref[...]).astype(o_ref.dtype)

    row = pl.BlockSpec((t, width), lambda i: (i, 0))
    return pl.pallas_call(
        body, name=name, grid=(s // t,), in_specs=[row, pl.BlockSpec((1, width), lambda i: (0, 0))],
        out_specs=row, out_shape=jax.ShapeDtypeStruct((s, width), BF), compiler_params=_params(1))(x, g)


def _accumulate(ref, part):
    @pl.when(pl.program_id(0) == 0)
    def _():
        ref[...] = part

    @pl.when(pl.program_id(0) > 0)
    def _():
        ref[...] += part


def mm_nt_norm_bwd(a, b, other, x, g, res, *, name, b_col=0, tm=512, host=None):
    s, kk = a.shape
    d = b.shape[0]
    tm = _pick(s, tm)
    assert b_col % kk == 0
    pair = isinstance(other, tuple)

    def body(a_ref, b_ref, *refs):
        dh = lax.dot_general(a_ref[...].astype(BF), b_ref[...].astype(BF), NT, preferred_element_type=F32)
        if pair:
            a2_ref, b2_ref, x_ref, g_ref, res_ref, dx_ref, dg_ref = refs
            dh = dh + lax.dot_general(a2_ref[...].astype(BF), b2_ref[...].astype(BF), NT, preferred_element_type=F32)
        else:
            add_ref, x_ref, g_ref, res_ref, dx_ref, dg_ref = refs
            dh = dh + add_ref[...]
        dx, dg = _rms_bwd(x_ref[...], g_ref[...], dh)
        _accumulate(dg_ref, dg)
        dx_ref[...] = dx + res_ref[...]

    row = pl.BlockSpec((tm, d), lambda i: (i, 0))
    vec = pl.BlockSpec((1, d), lambda i: (0, 0))
    if pair:
        k2 = other[0].shape[1]
        other_specs = [pl.BlockSpec((tm, k2), lambda i: (i, 0)), pl.BlockSpec((d, k2), lambda i: (0, 0))]
        other_args = list(other)
    else:
        other_specs, other_args = [row], [other]
    return _call(
        body, name=name, grid=(s // tm,),
        in_specs=[pl.BlockSpec((tm, kk), lambda i: (i, 0)), pl.BlockSpec((d, kk), lambda i: (0, b_col // kk)),
                  *other_specs, row, vec, row],
        out_specs=[row, vec],
        out_shape=[jax.ShapeDtypeStruct((s, d), F32), jax.ShapeDtypeStruct((1, d), F32)],
        args=[a, b, *other_args, x, g, res], host=host)


def mm_nn_loss(a, b, add, gf, tgt, *, name, tm=512):
    s, kk = a.shape
    d = b.shape[1]
    tm = _pick(s, tm)

    def body(a_ref, b_ref, add_ref, g_ref, t_ref, dx_ref, dg_ref, loss_ref):
        xv = jnp.dot(a_ref[...].astype(BF), b_ref[...].astype(BF), preferred_element_type=F32) + add_ref[...]
        inv = lax.rsqrt(jnp.mean(xv * xv, axis=-1, keepdims=True) + EPS)
        xhat = xv * inv
        gv = g_ref[...]
        diff = xhat * gv - t_ref[...]
        row_err = jnp.mean(diff * diff, axis=-1, keepdims=True)
        _accumulate(loss_ref, jnp.broadcast_to(0.5 * jnp.sum(row_err, axis=0, keepdims=True), (1, 128)))
        dout = diff * (1.0 / d)
        _accumulate(dg_ref, jnp.sum(dout * xhat, axis=0, keepdims=True))
        dxhat = dout * gv
        dx_ref[...] = inv * (dxhat - xhat * jnp.mean(dxhat * xhat, axis=-1, keepdims=True))

    row = pl.BlockSpec((tm, d), lambda i: (i, 0))
    vec = pl.BlockSpec((1, d), lambda i: (0, 0))
    return _call(
        body, name=name, grid=(s // tm,),
        in_specs=[pl.BlockSpec((tm, kk), lambda i: (i, 0)), pl.BlockSpec((kk, d), lambda i: (0, 0)), row, vec, row],
        out_specs=[row, vec, pl.BlockSpec((1, 128), lambda i: (0, 0))],
        out_shape=[jax.ShapeDtypeStruct((s, d), F32), jax.ShapeDtypeStruct((1, d), F32),
                   jax.ShapeDtypeStruct((1, 128), F32)],
        args=[a, b, add, gf, tgt])


ROW_CHUNK = 56


def pool_prep(uz, *, name, t=256):
    s = uz.shape[0]
    t = _pick(s, t)
    hb = t // HALO

    lead = 2 * HALO
    live = t + lead - 8
    assert live % ROW_CHUNK == 0

    def body(u_ref, halo_ref, o_ref, buf_a, buf_b):
        i = pl.program_id(0)
        buf_a[pl.ds(lead, t), :] = u_ref[...]
        buf_a[pl.ds(0, HALO), :] = jnp.zeros((HALO, POOL_WIDTH), F32)
        buf_b[pl.ds(0, 8), :] = jnp.zeros((8, POOL_WIDTH), F32)

        @pl.when(i == 0)
        def _():
            buf_a[pl.ds(HALO, HALO), :] = jnp.zeros((HALO, POOL_WIDTH), F32)

        @pl.when(i > 0)
        def _():
            buf_a[pl.ds(HALO, HALO), :] = halo_ref[...]

        pos = i * t + lax.broadcasted_iota(jnp.int32, (t, POOL_GROUP), 0)
        for g, w in enumerate(POOL_WINDOWS):
            cols = pl.ds(g * POOL_GROUP, POOL_GROUP)
            src, dst, shift = buf_a, buf_b, 1
            while shift < w:
                for r0 in range(8, 8 + live, ROW_CHUNK):
                    dst[pl.ds(r0, ROW_CHUNK), cols] = (src[pl.ds(r0, ROW_CHUNK), cols]
                                                       + src[pl.ds(r0 - shift, ROW_CHUNK), cols])
                src, dst, shift = dst, src, 2 * shift
            cnt = jnp.minimum(pos + 1, w).astype(F32)
            o_ref[:, cols] = (src[pl.ds(lead, t), cols] / cnt - u_ref[:, cols]).astype(o_ref.dtype)

    return pl.pallas_call(
        body, name=name, grid=(s // t,),
        in_specs=[pl.BlockSpec((t, POOL_WIDTH), lambda i: (i, 0)),
                  pl.BlockSpec((HALO, POOL_WIDTH), lambda i: (jnp.maximum(i * hb - 1, 0), 0))],
        out_specs=pl.BlockSpec((t, POOL_WIDTH), lambda i: (i, 0)),
        out_shape=jax.ShapeDtypeStruct((s, POOL_WIDTH), BF),
        scratch_shapes=[pltpu.VMEM((t + lead, POOL_WIDTH), F32), pltpu.VMEM((t + lead, POOL_WIDTH), F32)],
        compiler_params=_params(1))(uz, uz)


def pool_prep_bwd(dpd, *, name, t=256):
    s = dpd.shape[0]
    t = _pick(s, t)
    hb = t // HALO
    n = s // t

    tail = 2 * HALO
    live = t + tail - 8
    assert live % ROW_CHUNK == 0

    def body(d_ref, halo_ref, o_ref, buf_a, buf_b):
        i = pl.program_id(0)
        buf_a[pl.ds(t + HALO, HALO), :] = jnp.zeros((HALO, POOL_WIDTH), F32)
        buf_b[pl.ds(live, 8), :] = jnp.zeros((8, POOL_WIDTH), F32)
        pos = i * t + lax.broadcasted_iota(jnp.int32, (t, POOL_GROUP), 0)
        for g, w in enumerate(POOL_WINDOWS):
            cols = pl.ds(g * POOL_GROUP, POOL_GROUP)
            cnt = jnp.minimum(pos + 1, w).astype(F32)
            buf_a[pl.ds(0, t), cols] = d_ref[:, cols] / cnt

            @pl.when(i < n - 1)
            def _():
                buf_a[pl.ds(t, HALO), cols] = halo_ref[:, cols] / float(w)

            @pl.when(i == n - 1)
            def _():
                buf_a[pl.ds(t, HALO), cols] = jnp.zeros((HALO, POOL_GROUP), F32)

        for g, w in enumerate(POOL_WINDOWS):
            cols = pl.ds(g * POOL_GROUP, POOL_GROUP)
            src, dst, shift = buf_a, buf_b, 1
            while shift < w:
                for r0 in range(0, live, ROW_CHUNK):
                    dst[pl.ds(r0, ROW_CHUNK), cols] = (src[pl.ds(r0, ROW_CHUNK), cols]
                                                       + src[pl.ds(r0 + shift, ROW_CHUNK), cols])
                src, dst, shift = dst, src, 2 * shift
            o_ref[:, cols] = (src[pl.ds(0, t), cols] - d_ref[:, cols]).astype(o_ref.dtype)

    return pl.pallas_call(
        body, name=name, grid=(n,),
        in_specs=[pl.BlockSpec((t, POOL_WIDTH), lambda i: (i, 0)),
                  pl.BlockSpec((HALO, POOL_WIDTH), lambda i: (jnp.minimum((i + 1) * hb, n * hb - 1), 0))],
        out_specs=pl.BlockSpec((t, POOL_WIDTH), lambda i: (i, 0)),
        out_shape=jax.ShapeDtypeStruct((s, POOL_WIDTH), BF),
        scratch_shapes=[pltpu.VMEM((t + tail, POOL_WIDTH), F32), pltpu.VMEM((t + tail, POOL_WIDTH), F32)],
        compiler_params=_params(1))(dpd, dpd)


CHUNK = 512


def _chunks(width, step=CHUNK):
    return [slice(c, c + step) for c in range(0, width, step)]


def pool_mix_gate(pd, wg, uz, scale, *, name, tm=1024):
    s = pd.shape[0]
    g = wg.shape[0]
    tm = _pick(s, tm)

    def body(a_ref, w_ref, z_ref, sc_ref, y_ref):
        mm = jnp.dot(a_ref[...], w_ref[...], preferred_element_type=F32)
        z = z_ref[...]
        y_ref[...] = ((mm * sc_ref[...]) * (z * _sigmoid(z))).astype(y_ref.dtype)

    blk = pl.BlockSpec((tm, POOL_GROUP), lambda i, gi: (i, gi))
    return pl.pallas_call(
        body, name=name, grid=(s // tm, g),
        in_specs=[blk, pl.BlockSpec((None, POOL_GROUP, POOL_GROUP), lambda i, gi: (gi, 0, 0)),
                  pl.BlockSpec((tm, POOL_GROUP), lambda i, gi: (i, g + gi)),
                  pl.BlockSpec((1, POOL_GROUP), lambda i, gi: (0, gi))],
        out_specs=blk, out_shape=jax.ShapeDtypeStruct((s, POOL_WIDTH), BF),
        compiler_params=_params(2))(pd, wg, uz, scale)


def pool_out_dx_gate(dx, w_out, pd, wg, uz, scale, *, name, tm=512, host=None):
    s, d = dx.shape
    tm = _pick(s, tm)
    assert CHUNK == POOL_GROUP

    def body(dx_ref, w_ref, pd_ref, wg_ref, z_ref, sc_ref, dmm_ref, dz_ref, dsc_ref):
        dxv = dx_ref[...].astype(BF)
        parts = []
        for g, c in enumerate(_chunks(POOL_WIDTH)):
            dyv = lax.dot_general(dxv, w_ref[c, :], NT, preferred_element_type=F32)
            z = z_ref[:, c]
            sig = _sigmoid(z)
            mmv = jnp.dot(pd_ref[:, c], wg_ref[g], preferred_element_type=F32)
            scv = sc_ref[:, c]
            dmixed = dyv * (z * sig)
            dmm_ref[:, c] = (dmixed * scv).astype(dmm_ref.dtype)
            dz_ref[:, c] = (dyv * (mmv * scv) * (sig * (1.0 + z * (1.0 - sig)))).astype(dz_ref.dtype)
            parts.append(jnp.sum(dmixed * mmv, axis=0, keepdims=True))

        @pl.when(pl.program_id(0) == 0)
        def _():
            for c, part in zip(_chunks(POOL_WIDTH), parts):
                dsc_ref[:, c] = part

        @pl.when(pl.program_id(0) > 0)
        def _():
            for c, part in zip(_chunks(POOL_WIDTH), parts):
                dsc_ref[:, c] += part

    blk = pl.BlockSpec((tm, POOL_WIDTH), lambda i: (i, 0))
    vec = pl.BlockSpec((1, POOL_WIDTH), lambda i: (0, 0))
    return _call(
        body, name=name, grid=(s // tm,),
        in_specs=[pl.BlockSpec((tm, d), lambda i: (i, 0)), pl.BlockSpec((POOL_WIDTH, d), lambda i: (0, 0)),
                  blk, pl.BlockSpec(wg.shape, lambda i: (0, 0, 0)),
                  pl.BlockSpec((tm, POOL_WIDTH), lambda i: (i, 1)), vec],
        out_specs=[blk, blk, vec],
        out_shape=[jax.ShapeDtypeStruct((s, POOL_WIDTH), BF), jax.ShapeDtypeStruct((s, POOL_WIDTH), BF),
                   jax.ShapeDtypeStruct((1, POOL_WIDTH), F32)],
        args=[dx, w_out, pd, wg, uz, scale], host=host)


def _rope(a, cc, sa, sb):
    return a * cc + pltpu.roll(a, 96, 1) * sa + pltpu.roll(a, 32, 1) * sb


def _unrope(d, cc, sa, sb):
    return d * cc + pltpu.roll(d * sa, 32, 1) + pltpu.roll(d * sb, 96, 1)


def q_proj_rope(qn, wq, cc, sa, sb, *, name, tm=1024, heads=8):
    s, kk = qn.shape
    tm = _pick(s, tm)
    tn = heads * HEAD_PAD

    def body(a_ref, b_ref, cc_ref, sa_ref, sb_ref, o_ref):
        q = jnp.dot(a_ref[...], b_ref[...], preferred_element_type=F32)
        for h in range(heads):
            nope = slice(h * HEAD_PAD, h * HEAD_PAD + QK_NOPE)
            rope = slice(h * HEAD_PAD + QK_NOPE, (h + 1) * HEAD_PAD)
            o_ref[:, nope] = q[:, nope].astype(o_ref.dtype)
            o_ref[:, rope] = _rope(q[:, rope], cc_ref[...], sa_ref[...], sb_ref[...]).astype(o_ref.dtype)

    tab = pl.BlockSpec((tm, 128), lambda i, j: (i, 0))
    return pl.pallas_call(
        body, name=name, grid=(s // tm, N_HEADS // heads),
        in_specs=[pl.BlockSpec((tm, kk), lambda i, j: (i, 0)), pl.BlockSpec((kk, tn), lambda i, j: (0, j)),
                  tab, tab, tab],
        out_specs=pl.BlockSpec((tm, tn), lambda i, j: (i, j)),
        out_shape=jax.ShapeDtypeStruct((s, N_HEADS * HEAD_PAD), BF), compiler_params=_params(2))(qn, wq, cc, sa, sb)


LAT_KV = slice(P_KV, P_KV + KV_LORA)
LAT_KR = slice(P_KR, P_KR + 128)
LAT_Q = slice(P_Q, P_Q + Q_LORA)


def latent_fwd(proj, g_q, g_kv, cc, sa, sb, *, name, t=1024):
    s = proj.shape[0]
    t = _pick(s, t)

    def body(p_ref, gq_ref, gkv_ref, cc_ref, sa_ref, sb_ref, qn_ref, kvn_ref, kr_ref):
        qn_ref[...] = _rms(p_ref[:, LAT_Q], gq_ref[...]).astype(qn_ref.dtype)
        kvn_ref[...] = _rms(p_ref[:, LAT_KV], gkv_ref[...]).astype(kvn_ref.dtype)
        kr_ref[...] = _rope(p_ref[:, LAT_KR], cc_ref[...], sa_ref[...], sb_ref[...]).astype(kr_ref.dtype)

    tab = pl.BlockSpec((t, 128), lambda i: (i, 0))
    return pl.pallas_call(
        body, name=name, grid=(s // t,),
        in_specs=[pl.BlockSpec((t, P_SMALL), lambda i: (i, 0)), pl.BlockSpec((1, Q_LORA), lambda i: (0, 0)),
                  pl.BlockSpec((1, KV_LORA), lambda i: (0, 0)), tab, tab, tab],
        out_specs=[pl.BlockSpec((t, Q_LORA), lambda i: (i, 0)), pl.BlockSpec((t, KV_LORA), lambda i: (i, 0)), tab],
        out_shape=[jax.ShapeDtypeStruct((s, Q_LORA), BF), jax.ShapeDtypeStruct((s, KV_LORA), BF),
                   jax.ShapeDtypeStruct((s, 128), BF)],
        compiler_params=_params(1))(proj, g_q, g_kv, cc, sa, sb)


def latent_bwd(proj, g_q, g_kv, dqn, dkvn, dkr, cc, sa, sb, *, name, t=1024):
    s = proj.shape[0]
    t = _pick(s, t)

    def body(p_ref, gq_ref, gkv_ref, dqn_ref, dkvn_ref, dkr_ref, cc_ref, sa_ref, sb_ref, d_ref, dgq_ref, dgkv_ref):
        dq, dgq = _rms_bwd(p_ref[:, LAT_Q], gq_ref[...], dqn_ref[...])
        dkv, dgkv = _rms_bwd(p_ref[:, LAT_KV], gkv_ref[...], dkvn_ref[...])
        d_ref[:, LAT_Q] = dq.astype(d_ref.dtype)
        d_ref[:, LAT_KV] = dkv.astype(d_ref.dtype)
        d_ref[:, LAT_KR] = _unrope(dkr_ref[...], cc_ref[...], sa_ref[...], sb_ref[...]).astype(d_ref.dtype)
        _accumulate(dgq_ref, dgq)
        _accumulate(dgkv_ref, dgkv)

    tab = pl.BlockSpec((t, 128), lambda i: (i, 0))
    small = pl.BlockSpec((t, P_SMALL), lambda i: (i, 0))
    gq = pl.BlockSpec((1, Q_LORA), lambda i: (0, 0))
    gkv = pl.BlockSpec((1, KV_LORA), lambda i: (0, 0))
    return pl.pallas_call(
        body, name=name, grid=(s // t,),
        in_specs=[small, gq, gkv, pl.BlockSpec((t, Q_LORA), lambda i: (i, 0)),
                  pl.BlockSpec((t, KV_LORA), lambda i: (i, 0)), tab, tab, tab, tab],
        out_specs=[small, gq, gkv],
        out_shape=[jax.ShapeDtypeStruct((s, P_SMALL), BF), jax.ShapeDtypeStruct((1, Q_LORA), F32),
                   jax.ShapeDtypeStruct((1, KV_LORA), F32)],
        compiler_params=_params(1))(proj, g_q, g_kv, dqn, dkvn, dkr, cc, sa, sb)


def mla_out_dx_gate(dx, w_out, o, proj, *, name, tq):
    s, d = dx.shape
    nq = s // tq
    t = min(512, tq)
    per = tq // t

    def body(dx_ref, w_ref, o_ref, p_ref, do_ref, dz_ref, dl_ref):
        dxv = dx_ref[...].astype(BF)
        lane = lax.broadcasted_iota(jnp.int32, (t, 128), 1)
        deltas = jnp.zeros((t, 128), F32)
        for c in _chunks(MLA_WIDTH):
            dy_c = lax.dot_general(dxv, w_ref[c, :], NT, preferred_element_type=F32)
            for h in range(c.start // V_DIM, c.stop // V_DIM):
                hc = slice(h * V_DIM, (h + 1) * V_DIM)
                z = p_ref[:, slice(P_Z + hc.start, P_Z + hc.stop)]
                sig = _sigmoid(z)
                dyv = dy_c[:, hc.start - c.start:hc.stop - c.start]
                ov = o_ref[:, hc]
                dov = dyv * (z * sig)
                do_ref[:, hc] = dov.astype(do_ref.dtype)
                dz_ref[:, hc] = (dyv * ov * (sig * (1.0 + z * (1.0 - sig)))).astype(dz_ref.dtype)
                deltas = jnp.where(lane == h, jnp.sum(dov * ov, axis=-1, keepdims=True), deltas)
        rows = deltas.T
        for h in range(N_HEADS):
            dl_ref[h] = jnp.broadcast_to(rows[h:h + 1, :], (8, t))

    blk = pl.BlockSpec((t, MLA_WIDTH), lambda i: (i, 0))
    return pl.pallas_call(
        body, name=name, grid=(s // t,),
        in_specs=[pl.BlockSpec((t, d), lambda i: (i, 0)), pl.BlockSpec((MLA_WIDTH, d), lambda i: (0, 0)),
                  blk, pl.BlockSpec((t, P_WIDTH), lambda i: (i, 0))],
        out_specs=[blk, blk, pl.BlockSpec((N_HEADS, None, 8, t), lambda i: (0, i // per, 0, i % per))],
        out_shape=[jax.ShapeDtypeStruct((s, MLA_WIDTH), BF), jax.ShapeDtypeStruct((s, MLA_WIDTH), BF),
                   jax.ShapeDtypeStruct((N_HEADS, nq, 8, tq), F32)],
        compiler_params=_params(1))(dx, w_out, o, proj)


FWD_GROUPS = (2, 1)
BWD_GROUPS = (2, 1)


def _for_groups(first, count, groups, fn):
    lead = groups[-1]
    for g in groups[:-1][::-1]:
        lead = jnp.where(count >= g, g, lead)
    for g in groups:
        @pl.when(lead == g)
        def _(g=g):
            fn(first, g, True)
    first = first + lead
    count = count - lead
    for g in groups:
        n = count // g

        def one(p, carry, g=g, first=first):
            fn(first + p * g, g, False)
            return carry

        lax.fori_loop(0, n, one, 0)
        first = first + n * g
        count = count - n * g


def attn_fwd(qr, kv, krr, proj, *, name, tq):
    s = qr.shape[0]
    nq = s // tq
    z_blk = P_Z // V_DIM

    def body(kn_ref, v_ref, kr_ref, q_ref, z_ref, o_ref, y_ref, lse_ref, acc_sc, m_sc):
        j = pl.program_id(1)

        @pl.when(j == 0)
        def _():
            acc_sc[...] = jnp.zeros((nq, 2 * V_DIM, tq), F32)
            m_sc[...] = jnp.full((nq, 8, tq), NEG, F32)

        k = jnp.concatenate([kn_ref[...], kr_ref[...]], axis=1)
        vxt = jnp.concatenate([v_ref[...].astype(F32).T.astype(BF), jnp.ones((V_DIM, tq), BF)], axis=0)

        def update(i, n_tiles, masked):
            rows = pl.ds(pl.multiple_of(i * tq, tq), n_tiles * tq)
            st = lax.dot_general(k, q_ref[rows, :], NT, preferred_element_type=F32) * SCALE_LOG2E
            if masked:
                krow = lax.broadcasted_iota(jnp.int32, (tq, n_tiles * tq), 0)
                qcol = lax.broadcasted_iota(jnp.int32, (tq, n_tiles * tq), 1)
                st = jnp.where(qcol >= krow, st, NEG)
            m_prev = jnp.concatenate([m_sc[i + n, pl.ds(0, 1), :] for n in range(n_tiles)], axis=1)
            m_new = jnp.maximum(m_prev, jnp.max(st, axis=0, keepdims=True))
            alpha = jnp.exp2(m_prev - m_new)
            pt = jnp.exp2(st - m_new).astype(BF)
            pv_t = jnp.dot(vxt, pt, preferred_element_type=F32)
            for n in range(n_tiles):
                cols = slice(n * tq, (n + 1) * tq)
                acc_sc[i + n] = alpha[:, cols] * acc_sc[i + n] + pv_t[:, cols]
                m_sc[i + n, pl.ds(0, 1), :] = m_new[:, cols]

        _for_groups(j, nq - j, FWD_GROUPS, update)
        l = acc_sc[j, V_DIM:, :]
        o = (acc_sc[j, :V_DIM, :] / l).T
        o_ref[...] = o
        z = z_ref[...]
        y_ref[...] = (o * (z * _sigmoid(z))).astype(y_ref.dtype)
        lse_ref[...] = m_sc[j, pl.ds(0, 1), :] + jnp.log2(l[:8, :])

    tile = pl.BlockSpec((tq, V_DIM), lambda h, j: (j, h))
    return pl.pallas_call(
        body, name=name, grid=(N_HEADS, nq),
        in_specs=[pl.BlockSpec((tq, QK_NOPE), lambda h, j: (j, 2 * h)),
                  pl.BlockSpec((tq, V_DIM), lambda h, j: (j, 2 * h + 1)),
                  pl.BlockSpec((tq, 128), lambda h, j: (j, 0)),
                  pl.BlockSpec((s, HEAD_PAD), lambda h, j: (0, h)),
                  pl.BlockSpec((tq, V_DIM), lambda h, j: (j, z_blk + h))],
        out_specs=[tile, tile, pl.BlockSpec((None, None, 8, tq), lambda h, j: (h, j, 0, 0))],
        out_shape=[jax.ShapeDtypeStruct((s, N_HEADS * V_DIM), F32),
                   jax.ShapeDtypeStruct((s, N_HEADS * V_DIM), BF),
                   jax.ShapeDtypeStruct((N_HEADS, nq, 8, tq), F32)],
        scratch_shapes=[pltpu.VMEM((nq, 2 * V_DIM, tq), F32), pltpu.VMEM((nq, 8, tq), F32)],
        compiler_params=_params(2))(kv, kv, krr, qr, proj)


def attn_bwd(qr, kv, krr, do, lse, delta, cc, sa, sb, *, name, tq):
    s = qr.shape[0]
    nq = s // tq

    def body(kn_ref, v_ref, kr_ref, q_ref, do_ref, lse_ref, dl_ref, cc_ref, sa_ref, sb_ref,
             dkv_ref, dkr_ref, dq_ref, dq_sc, dk_sc, dv_sc):
        h = pl.program_id(0)
        j = pl.program_id(1)

        @pl.when(j == 0)
        def _():
            dq_sc[...] = jnp.zeros((s, HEAD_PAD), F32)

        dk_sc[...] = jnp.zeros((tq, HEAD_PAD), F32)
        dv_sc[...] = jnp.zeros((tq, V_DIM), F32)
        k = jnp.concatenate([kn_ref[...], kr_ref[...]], axis=1)
        v = v_ref[...]

        def step(i, n_tiles, masked):
            r0 = pl.multiple_of(i * tq, tq)
            rows = pl.ds(r0, n_tiles * tq)
            q = q_ref[rows, :]
            dov = do_ref[rows, :]
            lse_row = jnp.concatenate([lse_ref[i + n, pl.ds(0, 1), :] for n in range(n_tiles)], axis=1)
            dl_row = jnp.concatenate([dl_ref[i + n, pl.ds(0, 1), :] for n in range(n_tiles)], axis=1)
            st = lax.dot_general(k, q, NT, preferred_element_type=F32) * SCALE_LOG2E
            if masked:
                krow = lax.broadcasted_iota(jnp.int32, (tq, n_tiles * tq), 0)
                qcol = lax.broadcasted_iota(jnp.int32, (tq, n_tiles * tq), 1)
                st = jnp.where(qcol >= krow, st, NEG)
            pt = jnp.exp2(st - lse_row)
            dpt = lax.dot_general(v, dov, NT, preferred_element_type=F32)
            dst = (pt * (dpt - dl_row)).astype(BF)
            dv_sc[...] += jnp.dot(pt.astype(BF), dov, preferred_element_type=F32)
            dk_sc[...] += jnp.dot(dst, q, preferred_element_type=F32)
            dq_sc[rows, :] += lax.dot_general(dst, k, TN, preferred_element_type=F32)

        _for_groups(j, nq - j, BWD_GROUPS, step)
        dkv_ref[:, :QK_NOPE] = (dk_sc[:, :QK_NOPE] * SCALE).astype(dkv_ref.dtype)
        dkv_ref[:, QK_NOPE:] = dv_sc[...].astype(dkv_ref.dtype)
        mine = pl.ds(pl.multiple_of(j * tq, tq), tq)
        dkr = dk_sc[:, QK_NOPE:] * SCALE

        @pl.when(h == 0)
        def _():
            dkr_ref[mine, :] = dkr

        @pl.when(h > 0)
        def _():
            dkr_ref[mine, :] += dkr

        dq_ref[:, :QK_NOPE] = (dq_sc[mine, :QK_NOPE] * SCALE).astype(dq_ref.dtype)
        dq_ref[:, QK_NOPE:] = _unrope(dq_sc[mine, QK_NOPE:] * SCALE, cc_ref[...], sa_ref[...],
                                      sb_ref[...]).astype(dq_ref.dtype)

    rows = pl.BlockSpec((None, nq, 8, tq), lambda h, j: (h, 0, 0, 0))
    tab = pl.BlockSpec((tq, 128), lambda h, j: (j, 0))
    return pl.pallas_call(
        body, name=name, grid=(N_HEADS, nq),
        in_specs=[pl.BlockSpec((tq, QK_NOPE), lambda h, j: (j, 2 * h)),
                  pl.BlockSpec((tq, V_DIM), lambda h, j: (j, 2 * h + 1)), tab,
                  pl.BlockSpec((s, HEAD_PAD), lambda h, j: (0, h)),
                  pl.BlockSpec((s, V_DIM), lambda h, j: (0, h)), rows, rows, tab, tab, tab],
        out_specs=[pl.BlockSpec((tq, 256), lambda h, j: (j, h)),
                   pl.BlockSpec((s, 128), lambda h, j: (0, 0)),
                   pl.BlockSpec((tq, HEAD_PAD), lambda h, j: (j, h))],
        out_shape=[jax.ShapeDtypeStruct((s, N_HEADS * 256), BF),
                   jax.ShapeDtypeStruct((s, 128), F32),
                   jax.ShapeDtypeStruct((s, N_HEADS * HEAD_PAD), BF)],
        scratch_shapes=[pltpu.VMEM((s, HEAD_PAD), F32), pltpu.VMEM((tq, HEAD_PAD), F32),
                        pltpu.VMEM((tq, V_DIM), F32)],
        compiler_params=_params(2))(kv, kv, krr, qr, do, lse, delta, cc, sa, sb)


def adamw(w, g, m, v, *, name, t=512):
    r, c = w.shape
    t = r if r % t else t
    c1 = 1.0 - ADAM_B1 ** ADAM_STEP
    c2 = 1.0 - ADAM_B2 ** ADAM_STEP

    def body(w_ref, g_ref, m_ref, v_ref, d_ref, nm_ref, nv_ref):
        gv = g_ref[...]
        nm = ADAM_B1 * m_ref[...] + (1.0 - ADAM_B1) * gv
        nv = ADAM_B2 * v_ref[...] + (1.0 - ADAM_B2) * (gv * gv)
        nm_ref[...] = nm
        nv_ref[...] = nv
        d_ref[...] = -ADAM_LR * ((nm / c1) / (jnp.sqrt(nv / c2) + ADAM_EPS) + ADAM_WD * w_ref[...])

    blk = pl.BlockSpec((t, c), lambda i: (i, 0))
    return pl.pallas_call(
        body, name=name, grid=(r // t,), in_specs=[blk] * 4, out_specs=[blk] * 3,
        out_shape=[jax.ShapeDtypeStruct((r, c), F32)] * 3, compiler_params=_params(1))(w, g, m, v)


def sum_devices(parts, *, name):
    def body(p_ref, o_ref):
        acc = p_ref[pl.ds(0, SV_ROWS), :]
        for d in range(1, 8):
            acc = acc + p_ref[pl.ds(d * SV_ROWS, SV_ROWS), :]
        o_ref[...] = acc

    return pl.pallas_call(body, name=name, out_shape=jax.ShapeDtypeStruct((SV_ROWS, SV_COLS), F32))(parts)


def add_halves(g, rb, c_idx, *, name, rows):
    nq, r2, cc = rb.shape
    nb = r2 // rows

    def body(c_ref, g_ref, r_ref, o_ref):
        o_ref[...] = (g_ref[...] + r_ref[...]).astype(o_ref.dtype)

    grid_spec = pltpu.PrefetchScalarGridSpec(
        num_scalar_prefetch=1, grid=(nq, nb),
        in_specs=[pl.BlockSpec((None, rows, cc), lambda q, i, c: (q, c[0] * nb + i, 0)),
                  pl.BlockSpec((None, rows, cc), lambda q, i, c: (q, i, 0))],
        out_specs=pl.BlockSpec((None, rows, cc), lambda q, i, c: (q, i, 0)))
    return pl.pallas_call(body, name=name, grid_spec=grid_spec,
                          out_shape=jax.ShapeDtypeStruct((nq, r2, cc), BF),
                          compiler_params=_params(2))(c_idx, g, rb)


def sum_chips(rc, c_idx, *, name, rows):
    nq, r2, cc = rc.shape
    nb = r2 // rows

    def body(c_ref, r_ref, o_ref):
        parts = [r_ref[q].astype(F32) for q in range(4)]
        o_ref[...] = ((parts[0] + parts[1]) + parts[2]) + parts[3]

    grid_spec = pltpu.PrefetchScalarGridSpec(
        num_scalar_prefetch=1, grid=(nb,),
        in_specs=[pl.BlockSpec((nq, rows, cc), lambda i, c: (0, i, 0))],
        out_specs=pl.BlockSpec((rows, cc), lambda i, c: (c[0] * nb + i, 0)))
    return pl.pallas_call(body, name=name, grid_spec=grid_spec,
                          out_shape=jax.ShapeDtypeStruct((2 * r2, cc), F32),
                          compiler_params=_params(1))(c_idx, rc)


def _place():
    return lax.axis_index("x"), lax.axis_index("y"), lax.axis_index("c")


def all_gather8(xs, *, name, own_half):
    m = xs.shape[0] // 2 if own_half else xs.shape[0]
    n = xs.shape[1]

    def body(x_ref, out_ref, send_sems, recv_sems, local_sem):
        x, y, c = _place()
        me, sibling = (x, y, c), (x, y, 1 - c)
        chips = [(1 - x, y), (x, 1 - y), (1 - x, 1 - y)]
        src_own = x_ref.at[pl.ds(c * m, m), :] if own_half else x_ref

        def rows(px, py, pc):
            return out_ref.at[pl.ds((4 * px + 2 * py + pc) * m, m), :]

        def copy(k, block, to, src=None):
            return pltpu.make_async_remote_copy(
                src_ref=rows(*block) if src is None else src, dst_ref=rows(*block),
                send_sem=send_sems.at[k], recv_sem=recv_sems.at[k], device_id=to, device_id_type=MESH)

        mine = pltpu.make_async_copy(src_own, rows(*me), local_sem)
        mine.start()
        first = [copy(0, me, sibling, src=src_own)]
        first += [copy(1 + j, me, (*chip, c), src=src_own) for j, chip in enumerate(chips)]
        for cp in first:
            cp.start()
        passed = [copy(4 + j, (*chip, c), sibling) for j, chip in enumerate(chips)]
        for j, chip in enumerate(chips):
            copy(1 + j, (*chip, c), me).wait_recv()
            passed[j].start()
        copy(0, sibling, me).wait_recv()
        for j, chip in enumerate(chips):
            copy(4 + j, (*chip, 1 - c), me).wait_recv()
        for cp in first + passed:
            cp.wait_send()
        mine.wait()

    return pl.pallas_call(
        body, name=name, out_shape=jax.ShapeDtypeStruct((8 * m, n), xs.dtype),
        in_specs=[pl.BlockSpec(memory_space=pl.ANY)], out_specs=pl.BlockSpec(memory_space=pl.ANY),
        scratch_shapes=[pltpu.SemaphoreType.DMA((7,)), pltpu.SemaphoreType.DMA((7,)), pltpu.SemaphoreType.DMA],
    )(xs)


def _other_chips():
    x, y, c = _place()
    return [(1 - x, y), (x, 1 - y), (1 - x, 1 - y)]


def _remote(src, dst, send_sems, recv_sems, k, to):
    return pltpu.make_async_remote_copy(src_ref=src, dst_ref=dst, send_sem=send_sems.at[k], recv_sem=recv_sems.at[k],
                                        device_id=to, device_id_type=MESH)


def gather_ici(xs):
    r, cc = xs.shape
    m = r // 2

    def copies(ins, outs, ss, rs, landing):
        x, y, c = _place()
        half = pl.ds(c * m, m)
        return [_remote(ins[0].at[half, :], outs[0].at[(2 * cx + cy) if landing else (2 * x + y), half, :],
                        ss, rs, j, (cx, cy, c)) for j, (cx, cy) in enumerate(_other_chips())]

    def start(ins, outs, ss, rs, ls):
        for cp in copies(ins, outs, ss, rs, False):
            cp.start()

    def wait(ins, outs, ss, rs, ls):
        for cp in copies(ins, outs, ss, rs, True):
            cp.wait_recv()
        for cp in copies(ins, outs, ss, rs, False):
            cp.wait_send()

    return Exchange((xs,), (jax.ShapeDtypeStruct((4, r, cc), xs.dtype),), {}, 3, start, wait)


def gather_forward(buf):
    m = buf.shape[1] // 2

    def copies(outs, ss, rs, landing):
        x, y, c = _place()
        half = pl.ds(((1 - c) if landing else c) * m, m)
        return [_remote(outs[0].at[2 * cx + cy, half, :], outs[0].at[2 * cx + cy, half, :], ss, rs, j, (x, y, 1 - c))
                for j, (cx, cy) in enumerate(_other_chips())]

    def start(ins, outs, ss, rs, ls):
        for cp in copies(outs, ss, rs, False):
            cp.start()

    def wait(ins, outs, ss, rs, ls):
        for cp in copies(outs, ss, rs, True):
            cp.wait_recv()
        for cp in copies(outs, ss, rs, False):
            cp.wait_send()

    return Exchange((buf,), (jax.ShapeDtypeStruct(buf.shape, buf.dtype),), {0: 0}, 3, start, wait)


def swap_halves(g):
    nq, r, cc = g.shape
    r2 = r // 2

    def copy(ins, outs, ss, rs):
        x, y, c = _place()
        return _remote(ins[0].at[:, pl.ds((1 - c) * r2, r2), :], outs[0], ss, rs, 0, (x, y, 1 - c))

    def start(ins, outs, ss, rs, ls):
        copy(ins, outs, ss, rs).start()

    def wait(ins, outs, ss, rs, ls):
        copy(ins, outs, ss, rs).wait()

    return Exchange((g,), (jax.ShapeDtypeStruct((nq, r2, cc), g.dtype),), {}, 1, start, wait)


def exchange_chips(p):
    def own(ins, outs, ls):
        x, y, c = _place()
        return pltpu.make_async_copy(ins[0].at[2 * x + y], outs[0].at[2 * x + y], ls)

    def copies(ins, outs, ss, rs, landing):
        x, y, c = _place()
        return [_remote(ins[0].at[2 * cx + cy], outs[0].at[(2 * cx + cy) if landing else (2 * x + y)],
                        ss, rs, j, (cx, cy, c)) for j, (cx, cy) in enumerate(_other_chips())]

    def start(ins, outs, ss, rs, ls):
        own(ins, outs, ls).start()
        for cp in copies(ins, outs, ss, rs, False):
            cp.start()

    def wait(ins, outs, ss, rs, ls):
        for cp in copies(ins, outs, ss, rs, True):
            cp.wait_recv()
        for cp in copies(ins, outs, ss, rs, False):
            cp.wait_send()
        own(ins, outs, ls).wait()

    return Exchange((p,), (jax.ShapeDtypeStruct(p.shape, p.dtype),), {}, 3, start, wait)


def join_halves(tot):
    r2 = tot.shape[0] // 2

    def copy(outs, ss, rs, landing):
        x, y, c = _place()
        half = outs[0].at[pl.ds(((1 - c) if landing else c) * r2, r2), :]
        return _remote(half, half, ss, rs, 0, (x, y, 1 - c))

    def start(ins, outs, ss, rs, ls):
        copy(outs, ss, rs, False).start()

    def wait(ins, outs, ss, rs, ls):
        copy(outs, ss, rs, True).wait_recv()
        copy(outs, ss, rs, False).wait_send()

    return Exchange((tot,), (jax.ShapeDtypeStruct(tot.shape, tot.dtype),), {0: 0}, 1, start, wait)


def _pack_shard(blocks, small_vec=None):
    parts = [w.reshape(-1, PACK_C).astype(BF) for w in blocks]
    if small_vec is not None:
        srow = lax.bitcast_convert_type(small_vec, BF).reshape(1, PACK_C)
        parts.append(jnp.pad(srow, ((0, PACK_PAD - 1), (0, 0))))
    return jnp.concatenate(parts, axis=0)


def _split_rows(a, rows, axis):
    out, off = [], 0
    for n in rows:
        out.append(lax.slice_in_dim(a, off, off + n, axis=axis))
        off += n
    return out


def _unpack_pool(gw):
    p_in, p_grp, p_out = _split_rows(gw, POOL_ROWS, 1)
    return dict(
        pool_w_in=p_in.reshape(4, D_MODEL, 1024).transpose(1, 0, 2).reshape(D_MODEL, 2 * POOL_WIDTH),
        pool_w_group=p_grp.reshape(4, 4, 128, POOL_GROUP).transpose(1, 0, 2, 3).reshape(4, POOL_GROUP, POOL_GROUP),
        pool_w_out=p_out.reshape(POOL_WIDTH, D_MODEL))


def _unpack_mla(gw):
    m_in, m_qb, m_kvb, m_out, small = _split_rows(gw, MLA_ROWS + (PACK_PAD,), 1)
    w = {}
    win = m_in.reshape(4, D_MODEL, 688).transpose(1, 0, 2).reshape(D_MODEL, 2752)
    w["mla_w_in"] = jnp.concatenate(
        [win[:, 384:640], win[:, 640:704], jnp.zeros((D_MODEL, 64), BF), win[:, 0:384], win[:, 704:]], axis=1)
    wq = m_qb.reshape(4, Q_LORA, 768).transpose(1, 0, 2).reshape(Q_LORA, N_HEADS, QK_NOPE + QK_ROPE)
    w["mla_w_q_b"] = jnp.pad(wq, ((0, 0), (0, 0), (0, HEAD_PAD - QK_NOPE - QK_ROPE))).reshape(Q_LORA, N_HEADS * HEAD_PAD)
    w["mla_w_kv_b"] = m_kvb.reshape(4, KV_LORA, 1024).transpose(1, 0, 2).reshape(KV_LORA, 4096)
    w["mla_w_out"] = m_out.reshape(MLA_WIDTH, D_MODEL)
    small = lax.bitcast_convert_type(small[:, 0, :].reshape(4, 512, 2), F32)
    w["mla_norm"] = small[:, :256].reshape(1, D_MODEL)
    w["mla_q_norm"] = small[:, 256:352].reshape(1, Q_LORA)
    w["mla_kv_norm"] = small[:, 352:416].reshape(1, KV_LORA)
    return w


def _pack_pool_grads(g):
    return jnp.concatenate([
        g["pool_w_in"],
        g["pool_w_group"].reshape(4, 4, 128, POOL_GROUP).transpose(1, 0, 2, 3).reshape(4, 256, PACK_C),
        g["pool_w_out"].reshape(4, 512, PACK_C)], axis=1)


def _pack_mla_grads(g):
    return jnp.concatenate([
        g["mla_w_in"].reshape(D_MODEL, 4, 688).transpose(1, 0, 2).reshape(4, 688, PACK_C),
        g["mla_w_q_b"].reshape(Q_LORA, 4, 768).transpose(1, 0, 2).reshape(4, 288, PACK_C),
        g["mla_w_kv_b"],
        g["mla_w_out"].reshape(4, 512, PACK_C),
        jnp.zeros((4, PACK_PAD, PACK_C), F32)], axis=1)


def kernel(x, positions, pool_norm, pool_w_in, pool_w_group, pool_scale, pool_w_out, mla_norm, mla_w_in, mla_q_norm, mla_w_q_b, mla_kv_norm, mla_w_kv_b, mla_w_out, final_norm, loss_target, m_pool_norm, m_pool_w_in, m_pool_w_group, m_pool_scale, m_pool_w_out, m_mla_norm, m_mla_w_in, m_mla_q_norm, m_mla_w_q_b, m_mla_kv_norm, m_mla_w_kv_b, m_mla_w_out, m_final_norm, v_pool_norm, v_pool_w_in, v_pool_w_group, v_pool_scale, v_pool_w_out, v_mla_norm, v_mla_w_in, v_mla_q_norm, v_mla_w_q_b, v_mla_kv_norm, v_mla_w_kv_b, v_mla_w_out, v_final_norm):
    s = x.shape[1]
    tq = min(1024, s)
    x0 = x.reshape(s, D_MODEL)
    tgt = loss_target.reshape(s, D_MODEL)
    cx, cy, cc_idx = _place()
    chip = 2 * cx + cy

    big_names = ("pool_w_in", "pool_w_group", "pool_w_out", "mla_w_in", "mla_w_q_b", "mla_w_kv_b", "mla_w_out")
    big_w = dict(zip(big_names, (pool_w_in, pool_w_group, pool_w_out, mla_w_in, mla_w_q_b, mla_w_kv_b, mla_w_out)))
    big_m = dict(zip(big_names, (m_pool_w_in, m_pool_w_group, m_pool_w_out, m_mla_w_in, m_mla_w_q_b, m_mla_w_kv_b, m_mla_w_out)))
    big_v = dict(zip(big_names, (v_pool_w_in, v_pool_w_group, v_pool_w_out, v_mla_w_in, v_mla_w_q_b, v_mla_w_kv_b, v_mla_w_out)))

    small_vec = jnp.concatenate([mla_norm.reshape(-1), mla_q_norm.reshape(-1), mla_kv_norm.reshape(-1),
                                 jnp.zeros((96,), F32)])
    pool_packed = _pack_shard([big_w[n] for n in big_names[:3]])
    mla_packed = _pack_shard([big_w[n] for n in big_names[3:]], small_vec)
    w = _unpack_pool(all_gather8(pool_packed, name="gather_pool_weights", own_half=True).reshape(4, POOL_R, PACK_C))
    g_pool = pool_norm.reshape(1, D_MODEL)
    g_final = final_norm.reshape(1, D_MODEL)
    sc_pool = pool_scale.reshape(1, POOL_WIDTH)

    inv_freq = 1.0 / (ROPE_THETA ** (jnp.arange(0, QK_ROPE, 2, dtype=F32) / QK_ROPE))
    ang = positions.reshape(s).astype(F32)[:, None] * inv_freq
    cos, sin = jnp.cos(ang), jnp.sin(ang)
    z32, z64, z96 = (jnp.zeros((s, n), F32) for n in (32, 64, 96))
    t_cc = jnp.concatenate([cos, cos, z64], axis=1)
    t_sa = jnp.concatenate([-sin, z96], axis=1)
    t_sb = jnp.concatenate([z32, sin, z64], axis=1)

    h0 = norm_fwd(x0, g_pool, name="pool_norm_fwd")
    uz, mla_land = mm_nn(h0, w["pool_w_in"], name="pool_in_proj", out_dtype=F32, host=gather_ici(mla_packed))
    pd = pool_prep(uz, name="pool_window")
    y1 = pool_mix_gate(pd, w["pool_w_group"], uz, sc_pool, name="pool_group_mix")
    x1, mla_land = mm_nn(y1, w["pool_w_out"], name="pool_out_proj", out_dtype=F32, add=x0,
                         host=gather_forward(mla_land))
    w.update(_unpack_mla(lax.dynamic_update_slice_in_dim(mla_land, mla_packed[None], chip, axis=0)))

    h1 = norm_fwd(x1, w["mla_norm"], name="mla_norm_fwd")
    proj = mm_nn(h1, w["mla_w_in"], name="mla_in_proj", out_dtype=F32, tn=P_WIDTH // 2)
    qn, kvn, krr = latent_fwd(proj, w["mla_q_norm"], w["mla_kv_norm"], t_cc, t_sa, t_sb, name="mla_latent_fwd")
    qr = q_proj_rope(qn, w["mla_w_q_b"], t_cc, t_sa, t_sb, name="mla_q_proj")
    kv = mm_nn(kvn, w["mla_w_kv_b"], name="mla_kv_proj", out_dtype=BF, tn=2048)
    o, y2, lse = attn_fwd(qr, kv, krr, proj, name="mla_attn_fwd", tq=tq)
    dx2, d_final, loss_part = mm_nn_loss(y2, w["mla_w_out"], x1, g_final, tgt, name="mla_out_proj_loss")

    grads = {}
    grads["mla_w_out"] = mm_tn(y2, dx2, name="mla_out_proj_dw")
    do, dz2, delta = mla_out_dx_gate(dx2, w["mla_w_out"], o, proj, name="mla_out_proj_dx", tq=tq)
    dkv, dkr, dq_pre = attn_bwd(qr, kv, krr, do, lse, delta, t_cc, t_sa, t_sb, name="mla_attn_bwd", tq=tq)
    dqn = mm_nt(dq_pre, w["mla_w_q_b"], name="mla_q_proj_dx", out_dtype=F32, tn=Q_LORA, tk=4096)
    g_qb = mm_tn(qn, dq_pre, name="mla_q_proj_dw", tm=Q_LORA, tn=2048)
    dkvn = mm_nt(dkv, w["mla_w_kv_b"], name="mla_kv_proj_dx", out_dtype=F32, tn=KV_LORA, tk=4096)
    grads["mla_w_kv_b"] = mm_tn(kvn, dkv, name="mla_kv_proj_dw", tm=KV_LORA, tk=4096, by_column_block=True)
    dsmall, d_qnorm, d_kvnorm = latent_bwd(proj, w["mla_q_norm"], w["mla_kv_norm"], dqn, dkvn, dkr,
                                           t_cc, t_sa, t_sb, name="mla_latent_bwd")
    dx1, d_mnorm = mm_nt_norm_bwd(dz2, w["mla_w_in"][:, P_Z:], (dsmall, w["mla_w_in"]), x1, w["mla_norm"], dx2,
                                  name="mla_in_proj_dx")
    g_in_a = mm_tn(h1, dsmall, name="mla_in_proj_dw_a", tn=P_SMALL, tk=4096)
    g_in_b = mm_tn(h1, dz2, name="mla_in_proj_dw_b", tk=4096)

    g_in = jnp.concatenate([g_in_a, g_in_b], axis=1)
    grads["mla_w_in"] = jnp.concatenate([g_in[:, P_Q:P_Z], g_in[:, P_KV:P_KV + KV_LORA],
                                         g_in[:, P_KR:P_KR + QK_ROPE], g_in[:, P_Z:]], axis=1)
    grads["mla_w_q_b"] = g_qb.reshape(Q_LORA, N_HEADS, HEAD_PAD)[:, :, :QK_NOPE + QK_ROPE].reshape(Q_LORA, -1)
    core_idx = cc_idx.reshape(1).astype(jnp.int32)
    gp_mla = _pack_mla_grads(grads)

    grads["pool_w_out"], sib = mm_tn(y1, dx1, name="pool_out_proj_dw", host=swap_halves(gp_mla))
    pre = add_halves(gp_mla, sib, core_idx, name="mla_grad_add_halves", rows=MLA_R // 2)
    dmm, dz1, d_scale, got = pool_out_dx_gate(dx1, w["pool_w_out"], pd, w["pool_w_group"], uz, sc_pool,
                                              name="pool_out_proj_dx", host=exchange_chips(pre))
    tot = sum_chips(got, core_idx, name="mla_grad_sum_chips", rows=MLA_R // 2)
    dpd, red_mla = gmm_nt(dmm, w["pool_w_group"], name="pool_group_mix_dx", host=join_halves(tot))
    grads["pool_w_group"] = gmm_tn(pd, dmm, 4, name="pool_group_mix_dw", tk=4096)
    du = pool_prep_bwd(dpd, name="pool_window_bwd")
    g_pin_u = mm_tn(h0, du, name="pool_in_proj_dw_u", tk=4096, by_column_block=True)
    g_pin_z = mm_tn(h0, dz1, name="pool_in_proj_dw_z", tk=4096, by_column_block=True)
    grads["pool_w_in"] = jnp.concatenate([g_pin_u, g_pin_z], axis=0)

    gp_pool = _pack_pool_grads(grads)
    dh0, sib = mm_nt(du, w["pool_w_in"], name="pool_in_proj_dx_u", out_dtype=F32, host=swap_halves(gp_pool))
    pre = add_halves(gp_pool, sib, core_idx, name="pool_grad_add_halves", rows=POOL_R // 2)
    grad_x, d_pnorm, got = mm_nt_norm_bwd(dz1, w["pool_w_in"], dh0, x0, g_pool, dx1, name="pool_in_proj_dx_z",
                                          b_col=POOL_WIDTH, host=exchange_chips(pre))
    tot = sum_chips(got, core_idx, name="pool_grad_sum_chips", rows=POOL_R // 2)
    red_pool = run_exchange(join_halves(tot), name="pool_grad_join_halves")[0]
    red_parts = _split_rows(red_pool, POOL_ROWS, 0) + _split_rows(red_mla, MLA_ROWS, 0)

    sv = jnp.concatenate([d_pnorm.reshape(-1), d_scale.reshape(-1), d_final.reshape(-1), d_mnorm.reshape(-1),
                          d_qnorm.reshape(-1), d_kvnorm.reshape(-1), loss_part[0, :1],
                          jnp.zeros((SV_ROWS * SV_COLS - SV_OFF["loss"] - 1,), F32)]).reshape(SV_ROWS, SV_COLS)
    sv_all = all_gather8(sv, name="gather_small_grads", own_half=False)
    sv_sum = sum_devices(sv_all, name="sum_small_grads").reshape(-1)
    loss = sv_sum[SV_OFF["loss"]]

    def sv_take(key, n):
        return lax.slice_in_dim(sv_sum, SV_OFF[key], SV_OFF[key] + n)

    out_g, out_d, out_m, out_v = {}, {}, {}, {}
    for name, part in zip(big_names, red_parts):
        shp = big_w[name].shape
        g2 = part.reshape(shp)
        two_d = (-1, shp[-1])
        d_, m_, v_ = adamw(big_w[name].reshape(two_d), g2.reshape(two_d), big_m[name].reshape(two_d),
                           big_v[name].reshape(two_d), name="adamw_" + name)
        out_g[name], out_d[name], out_m[name], out_v[name] = g2, d_.reshape(shp), m_.reshape(shp), v_.reshape(shp)

    small = [
        ("pool_norm", pool_norm, m_pool_norm, v_pool_norm, sv_take("pool_norm", 1024)),
        ("pool_scale", pool_scale, m_pool_scale, v_pool_scale, sv_take("pool_scale", 2048)),
        ("final_norm", final_norm, m_final_norm, v_final_norm, sv_take("final_norm", 1024)),
        ("mla_norm", mla_norm, m_mla_norm, v_mla_norm,
         lax.dynamic_slice_in_dim(sv_take("mla_norm", 1024), chip * 256, 256)),
        ("mla_q_norm", mla_q_norm, m_mla_q_norm, v_mla_q_norm,
         lax.dynamic_slice_in_dim(sv_take("q_norm", 384), chip * 96, 96)),
        ("mla_kv_norm", mla_kv_norm, m_mla_kv_norm, v_mla_kv_norm,
         lax.dynamic_slice_in_dim(sv_take("kv_norm", 256), chip * 64, 64)),
    ]
    sw = jnp.concatenate([t[1].reshape(-1) for t in small] + [jnp.zeros((96,), F32)]).reshape(1, -1)
    sm = jnp.concatenate([t[2].reshape(-1) for t in small] + [jnp.zeros((96,), F32)]).reshape(1, -1)
    s_v = jnp.concatenate([t[3].reshape(-1) for t in small] + [jnp.ones((96,), F32)]).reshape(1, -1)
    sg = jnp.concatenate([t[4].reshape(-1) for t in small] + [jnp.zeros((96,), F32)]).reshape(1, -1)
    sd_, sm_, sv_ = adamw(sw, sg, sm, s_v, name="adamw_vectors")
    off = 0
    for name, wt, _, _, gvec in small:
        n = gvec.shape[0]
        shp = wt.shape
        out_g[name] = gvec.reshape(shp)
        out_d[name] = sd_[0, off:off + n].reshape(shp)
        out_m[name] = sm_[0, off:off + n].reshape(shp)
        out_v[name] = sv_[0, off:off + n].reshape(shp)
        off += n

    order = ("pool_norm", "pool_w_in", "pool_w_group", "pool_scale", "pool_w_out", "mla_norm", "mla_w_in",
             "mla_q_norm", "mla_w_q_b", "mla_kv_norm", "mla_w_kv_b", "mla_w_out", "final_norm")
    return (loss, grad_x.reshape(x.shape), *[out_g[n] for n in order], *[out_d[n] for n in order],
            *[out_m[n] for n in order], *[out_v[n] for n in order])
```

```python
import functools
from typing import Callable, NamedTuple

import jax
import jax.numpy as jnp
from jax import lax
from jax.experimental import pallas as pl
from jax.experimental.pallas import tpu as pltpu

F32 = jnp.float32
BF = jnp.bfloat16
MESH = pl.DeviceIdType.MESH

D_MODEL = 1024
POOL_WIDTH = 2048
POOL_WINDOWS = (2, 4, 8, 16)
POOL_GROUP = 512
HALO = 16
N_HEADS = 16
QK_NOPE = 128
QK_ROPE = 64
V_DIM = 128
HEAD_PAD = 256
Q_LORA = 384
KV_LORA = 256
MLA_WIDTH = 2048
ROPE_THETA = 10000.0
EPS = 1e-6
SCALE = (QK_NOPE + QK_ROPE) ** -0.5
SCALE_LOG2E = SCALE * 1.4426950408889634
NEG = -1e30

P_KV, P_KR, P_Q, P_Z = 0, 256, 384, 768
P_SMALL = 768
P_WIDTH = 2816

ADAM_LR = 0.001
ADAM_B1 = 0.9
ADAM_B2 = 0.999
ADAM_EPS = 1e-08
ADAM_WD = 0.01
ADAM_STEP = 10

NN = (((1,), (0,)), ((), ()))
NT = (((1,), (1,)), ((), ()))
TN = (((0,), (0,)), ((), ()))

POOL_ROWS = (1024, 256, 512)
MLA_ROWS = (688, 288, 256, 512)
PACK_PAD = 16
POOL_R = sum(POOL_ROWS)
MLA_R = sum(MLA_ROWS) + PACK_PAD
PACK_C = 1024
SV_OFF = dict(pool_norm=0, pool_scale=1024, final_norm=3072, mla_norm=4096, q_norm=5120, kv_norm=5504, loss=5760)
SV_ROWS, SV_COLS = 8, 768

VMEM_LIMIT = 56 * 1024 * 1024


def _params(n_axes, vmem=None):
    return pltpu.CompilerParams(dimension_semantics=("arbitrary",) * n_axes,
                                vmem_limit_bytes=VMEM_LIMIT if vmem is None else vmem)


def _sigmoid(z):
    return 1.0 / (1.0 + jnp.exp(-z))


class Exchange(NamedTuple):
    operands: tuple
    out_shapes: tuple
    aliases: dict
    n_sems: int
    start: Callable
    wait: Callable


HBM_SPEC = pl.BlockSpec(memory_space=pl.ANY)


def _exchange_scratch(ex):
    return [pltpu.SemaphoreType.DMA((ex.n_sems,)), pltpu.SemaphoreType.DMA((ex.n_sems,)), pltpu.SemaphoreType.DMA]


def run_exchange(ex, *, name):
    n_in, n_out = len(ex.operands), len(ex.out_shapes)

    def body(*refs):
        args = (refs[:n_in], refs[n_in:n_in + n_out]) + tuple(refs[n_in + n_out:])
        ex.start(*args)
        ex.wait(*args)

    return pl.pallas_call(
        body, name=name, out_shape=list(ex.out_shapes), in_specs=[HBM_SPEC] * n_in,
        out_specs=[HBM_SPEC] * n_out, scratch_shapes=_exchange_scratch(ex),
        input_output_aliases=dict(ex.aliases))(*ex.operands)


def _call(core, *, name, grid, in_specs, out_specs, out_shape, args, scratch=(), host=None):
    in_specs, out_specs, out_shape = list(in_specs), list(out_specs), list(out_shape)
    params = _params(len(grid))
    if host is None:
        return pl.pallas_call(core, name=name, grid=grid, in_specs=in_specs, out_specs=out_specs,
                              out_shape=out_shape, scratch_shapes=list(scratch), compiler_params=params)(*args)
    n_in, n_out = len(in_specs), len(out_specs)
    n_hin, n_hout = len(host.operands), len(host.out_shapes)

    def body(*refs):
        ins, refs = refs[:n_in], refs[n_in:]
        h_in, refs = refs[:n_hin], refs[n_hin:]
        outs, refs = refs[:n_out], refs[n_out:]
        h_out, refs = refs[:n_hout], refs[n_hout:]
        own_scratch, sems = refs[:-3], refs[-3:]
        ids = [pl.program_id(ax) for ax in range(len(grid))]
        first = functools.reduce(jnp.logical_and, [i == 0 for i in ids])
        last = functools.reduce(jnp.logical_and, [i == n - 1 for i, n in zip(ids, grid)])

        @pl.when(first)
        def _():
            host.start(h_in, h_out, *sems)

        core(*ins, *outs, *own_scratch)

        @pl.when(last)
        def _():
            host.wait(h_in, h_out, *sems)

    return pl.pallas_call(
        body, name=name, grid=grid, in_specs=in_specs + [HBM_SPEC] * n_hin,
        out_specs=out_specs + [HBM_SPEC] * n_hout, out_shape=out_shape + list(host.out_shapes),
        scratch_shapes=list(scratch) + _exchange_scratch(host),
        input_output_aliases={n_in + i: n_out + o for i, o in host.aliases.items()},
        compiler_params=params)(*args, *host.operands)


def _mm(a, b, *, dims, grid, a_spec, b_spec, o_spec, out_shape, out_dtype, acc_shape, name,
        add=None, add_spec=None, host=None):
    nk = grid[-1]
    kax = len(grid) - 1

    def body(*refs):
        if add is None:
            a_ref, b_ref, o_ref = refs[:3]
            add_ref = None
            rest = refs[3:]
        else:
            a_ref, b_ref, add_ref, o_ref = refs[:4]
            rest = refs[4:]
        part = lax.dot_general(a_ref[...].astype(BF), b_ref[...].astype(BF), dims,
                               preferred_element_type=F32)

        def finish(r):
            if add_ref is not None:
                r = r + add_ref[...]
            o_ref[...] = r.astype(o_ref.dtype)

        if nk == 1:
            finish(part)
        else:
            acc = rest[0]
            k = pl.program_id(kax)

            @pl.when(k == 0)
            def _():
                acc[...] = part

            @pl.when(k > 0)
            def _():
                acc[...] += part

            @pl.when(k == nk - 1)
            def _():
                finish(acc[...])

    in_specs = [a_spec, b_spec]
    args = [a, b]
    if add is not None:
        in_specs.append(add_spec)
        args.append(add)
    out = _call(body, name=name, grid=grid, in_specs=in_specs, out_specs=[o_spec],
                out_shape=[jax.ShapeDtypeStruct(out_shape, out_dtype)], args=args,
                scratch=[] if nk == 1 else [pltpu.VMEM(acc_shape, F32)], host=host)
    return out[0] if host is None else out


def _pick(n, t):
    t = min(n, t)
    assert n % t == 0, (n, t)
    return t


def mm_nn(a, b, *, name, out_dtype, add=None, tm=1024, tn=1024, tk=2048, host=None):
    m = a.shape[0]
    kk, n = b.shape
    tm, tn, tk = _pick(m, tm), _pick(n, tn), _pick(kk, tk)
    return _mm(a, b, dims=NN, grid=(m // tm, n // tn, kk // tk),
               a_spec=pl.BlockSpec((tm, tk), lambda i, j, k: (i, k)),
               b_spec=pl.BlockSpec((tk, tn), lambda i, j, k: (k, j)),
               o_spec=pl.BlockSpec((tm, tn), lambda i, j, k: (i, j)),
               add=add, add_spec=pl.BlockSpec((tm, tn), lambda i, j, k: (i, j)),
               out_shape=(m, n), out_dtype=out_dtype, acc_shape=(tm, tn), name=name, host=host)


def mm_nt(a, b, *, name, out_dtype, b_col=0, add=None, tm=1024, tn=1024, tk=2048, host=None):
    m, kk = a.shape
    n = b.shape[0]
    tm, tn, tk = _pick(m, tm), _pick(n, tn), _pick(kk, tk)
    assert b_col % tk == 0
    ko = b_col // tk
    return _mm(a, b, dims=NT, grid=(m // tm, n // tn, kk // tk),
               a_spec=pl.BlockSpec((tm, tk), lambda i, j, k: (i, k)),
               b_spec=pl.BlockSpec((tn, tk), lambda i, j, k: (j, ko + k)),
               o_spec=pl.BlockSpec((tm, tn), lambda i, j, k: (i, j)),
               add=add, add_spec=pl.BlockSpec((tm, tn), lambda i, j, k: (i, j)),
               out_shape=(m, n), out_dtype=out_dtype, acc_shape=(tm, tn), name=name, host=host)


def mm_tn(a, b, *, name, tm=1024, tn=1024, tk=2048, host=None, by_column_block=False):
    s, m = a.shape
    n = b.shape[1]
    tm, tn, tk = _pick(m, tm), _pick(n, tn), _pick(s, tk)
    if by_column_block:
        out_shape, o_spec = (n // tn, m, tn), pl.BlockSpec((None, tm, tn), lambda i, j, k: (j, i, 0))
    else:
        out_shape, o_spec = (m, n), pl.BlockSpec((tm, tn), lambda i, j, k: (i, j))
    return _mm(a, b, dims=TN, grid=(m // tm, n // tn, s // tk),
               a_spec=pl.BlockSpec((tk, tm), lambda i, j, k: (k, i)),
               b_spec=pl.BlockSpec((tk, tn), lambda i, j, k: (k, j)),
               o_spec=o_spec, out_shape=out_shape, out_dtype=F32, acc_shape=(tm, tn), name=name, host=host)


def gmm_nt(a, w, *, name, tm=1024, host=None):
    s = a.shape[0]
    g, kk, n = w.shape
    tm = _pick(s, tm)
    return _mm(a, w, dims=NT, grid=(s // tm, g, 1),
               a_spec=pl.BlockSpec((tm, n), lambda i, gi, k: (i, gi)),
               b_spec=pl.BlockSpec((None, kk, n), lambda i, gi, k: (gi, 0, 0)),
               o_spec=pl.BlockSpec((tm, kk), lambda i, gi, k: (i, gi)),
               out_shape=(s, g * kk), out_dtype=F32, acc_shape=(tm, kk), name=name, host=host)


def gmm_tn(a, b, g, *, name, tk=2048):
    s = a.shape[0]
    kk, n = a.shape[1] // g, b.shape[1] // g
    tk = _pick(s, tk)
    return _mm(a, b, dims=TN, grid=(g, s // tk),
               a_spec=pl.BlockSpec((tk, kk), lambda gi, k: (k, gi)),
               b_spec=pl.BlockSpec((tk, n), lambda gi, k: (k, gi)),
               o_spec=pl.BlockSpec((None, kk, n), lambda gi, k: (gi, 0, 0)),
               out_shape=(g, kk, n), out_dtype=F32, acc_shape=(kk, n), name=name)


def _rms(xv, gv):
    inv = lax.rsqrt(jnp.mean(xv * xv, axis=-1, keepdims=True) + EPS)
    return (xv * inv) * gv


def _rms_bwd(xv, gv, dh):
    inv = lax.rsqrt(jnp.mean(xv * xv, axis=-1, keepdims=True) + EPS)
    xhat = xv * inv
    dxhat = dh * gv
    dx = inv * (dxhat - xhat * jnp.mean(dxhat * xhat, axis=-1, keepdims=True))
    return dx, jnp.sum(dh * xhat, axis=0, keepdims=True)


def norm_fwd(x, g, *, name, t=1024):
    s, width = x.shape
    t = _pick(s, t)

    def body(x_ref, g_ref, o_ref):
        o_ref[...] = _rms(x_ref[...], g_ref[...]).astype(o_ref.dtype)

    row = pl.BlockSpec((t, width), lambda i: (i, 0))
    return pl.pallas_call(
        body, name=name, grid=(s // t,), in_specs=[row, pl.BlockSpec((1, width), lambda i: (0, 0))],
        out_specs=row, out_shape=jax.ShapeDtypeStruct((s, width), BF), compiler_params=_params(1))(x, g)


def _accumulate(ref, part):
    @pl.when(pl.program_id(0) == 0)
    def _():
        ref[...] = part

    @pl.when(pl.program_id(0) > 0)
    def _():
        ref[...] += part


def mm_nt_norm_bwd(a, b, other, x, g, res, *, name, b_col=0, tm=512, host=None):
    s, kk = a.shape
    d = b.shape[0]
    tm = _pick(s, tm)
    assert b_col % kk == 0
    pair = isinstance(other, tuple)

    def body(a_ref, b_ref, *refs):
        dh = lax.dot_general(a_ref[...].astype(BF), b_ref[...].astype(BF), NT, preferred_element_type=F32)
        if pair:
            a2_ref, b2_ref, x_ref, g_ref, res_ref, dx_ref, dg_ref = refs
            dh = dh + lax.dot_general(a2_ref[...].astype(BF), b2_ref[...].astype(BF), NT, preferred_element_type=F32)
        else:
            add_ref, x_ref, g_ref, res_ref, dx_ref, dg_ref = refs
            dh = dh + add_ref[...]
        dx, dg = _rms_bwd(x_ref[...], g_ref[...], dh)
        _accumulate(dg_ref, dg)
        dx_ref[...] = dx + res_ref[...]

    row = pl.BlockSpec((tm, d), lambda i: (i, 0))
    vec = pl.BlockSpec((1, d), lambda i: (0, 0))
    if pair:
        k2 = other[0].shape[1]
        other_specs = [pl.BlockSpec((tm, k2), lambda i: (i, 0)), pl.BlockSpec((d, k2), lambda i: (0, 0))]
        other_args = list(other)
    else:
        other_specs, other_args = [row], [other]
    return _call(
        body, name=name, grid=(s // tm,),
        in_specs=[pl.BlockSpec((tm, kk), lambda i: (i, 0)), pl.BlockSpec((d, kk), lambda i: (0, b_col // kk)),
                  *other_specs, row, vec, row],
        out_specs=[row, vec],
        out_shape=[jax.ShapeDtypeStruct((s, d), F32), jax.ShapeDtypeStruct((1, d), F32)],
        args=[a, b, *other_args, x, g, res], host=host)


def mm_nn_loss(a, b, add, gf, tgt, *, name, tm=512):
    s, kk = a.shape
    d = b.shape[1]
    tm = _pick(s, tm)

    def body(a_ref, b_ref, add_ref, g_ref, t_ref, dx_ref, dg_ref, loss_ref):
        xv = jnp.dot(a_ref[...].astype(BF), b_ref[...].astype(BF), preferred_element_type=F32) + add_ref[...]
        inv = lax.rsqrt(jnp.mean(xv * xv, axis=-1, keepdims=True) + EPS)
        xhat = xv * inv
        gv = g_ref[...]
        diff = xhat * gv - t_ref[...]
        row_err = jnp.mean(diff * diff, axis=-1, keepdims=True)
        _accumulate(loss_ref, jnp.broadcast_to(0.5 * jnp.sum(row_err, axis=0, keepdims=True), (1, 128)))
        dout = diff * (1.0 / d)
        _accumulate(dg_ref, jnp.sum(dout * xhat, axis=0, keepdims=True))
        dxhat = dout * gv
        dx_ref[...] = inv * (dxhat - xhat * jnp.mean(dxhat * xhat, axis=-1, keepdims=True))

    row = pl.BlockSpec((tm, d), lambda i: (i, 0))
    vec = pl.BlockSpec((1, d), lambda i: (0, 0))
    return _call(
        body, name=name, grid=(s // tm,),
        in_specs=[pl.BlockSpec((tm, kk), lambda i: (i, 0)), pl.BlockSpec((kk, d), lambda i: (0, 0)), row, vec, row],
        out_specs=[row, vec, pl.BlockSpec((1, 128), lambda i: (0, 0))],
        out_shape=[jax.ShapeDtypeStruct((s, d), F32), jax.ShapeDtypeStruct((1, d), F32),
                   jax.ShapeDtypeStruct((1, 128), F32)],
        args=[a, b, add, gf, tgt])


ROW_CHUNK = 56


def pool_prep(uz, *, name, t=256):
    s = uz.shape[0]
    t = _pick(s, t)
    hb = t // HALO

    lead = 2 * HALO
    live = t + lead - 8
    assert live % ROW_CHUNK == 0

    def body(u_ref, halo_ref, o_ref, buf_a, buf_b):
        i = pl.program_id(0)
        buf_a[pl.ds(lead, t), :] = u_ref[...]
        buf_a[pl.ds(0, HALO), :] = jnp.zeros((HALO, POOL_WIDTH), F32)
        buf_b[pl.ds(0, 8), :] = jnp.zeros((8, POOL_WIDTH), F32)

        @pl.when(i == 0)
        def _():
            buf_a[pl.ds(HALO, HALO), :] = jnp.zeros((HALO, POOL_WIDTH), F32)

        @pl.when(i > 0)
        def _():
            buf_a[pl.ds(HALO, HALO), :] = halo_ref[...]

        pos = i * t + lax.broadcasted_iota(jnp.int32, (t, POOL_GROUP), 0)
        for g, w in enumerate(POOL_WINDOWS):
            cols = pl.ds(g * POOL_GROUP, POOL_GROUP)
            src, dst, shift = buf_a, buf_b, 1
            while shift < w:
                for r0 in range(8, 8 + live, ROW_CHUNK):
                    dst[pl.ds(r0, ROW_CHUNK), cols] = (src[pl.ds(r0, ROW_CHUNK), cols]
                                                       + src[pl.ds(r0 - shift, ROW_CHUNK), cols])
                src, dst, shift = dst, src, 2 * shift
            cnt = jnp.minimum(pos + 1, w).astype(F32)
            o_ref[:, cols] = (src[pl.ds(lead, t), cols] / cnt - u_ref[:, cols]).astype(o_ref.dtype)

    return pl.pallas_call(
        body, name=name, grid=(s // t,),
        in_specs=[pl.BlockSpec((t, POOL_WIDTH), lambda i: (i, 0)),
                  pl.BlockSpec((HALO, POOL_WIDTH), lambda i: (jnp.maximum(i * hb - 1, 0), 0))],
        out_specs=pl.BlockSpec((t, POOL_WIDTH), lambda i: (i, 0)),
        out_shape=jax.ShapeDtypeStruct((s, POOL_WIDTH), BF),
        scratch_shapes=[pltpu.VMEM((t + lead, POOL_WIDTH), F32), pltpu.VMEM((t + lead, POOL_WIDTH), F32)],
        compiler_params=_params(1))(uz, uz)


def pool_prep_bwd(dpd, *, name, t=256):
    s = dpd.shape[0]
    t = _pick(s, t)
    hb = t // HALO
    n = s // t

    tail = 2 * HALO
    live = t + tail - 8
    assert live % ROW_CHUNK == 0

    def body(d_ref, halo_ref, o_ref, buf_a, buf_b):
        i = pl.program_id(0)
        buf_a[pl.ds(t + HALO, HALO), :] = jnp.zeros((HALO, POOL_WIDTH), F32)
        buf_b[pl.ds(live, 8), :] = jnp.zeros((8, POOL_WIDTH), F32)
        pos = i * t + lax.broadcasted_iota(jnp.int32, (t, POOL_GROUP), 0)
        for g, w in enumerate(POOL_WINDOWS):
            cols = pl.ds(g * POOL_GROUP, POOL_GROUP)
            cnt = jnp.minimum(pos + 1, w).astype(F32)
            buf_a[pl.ds(0, t), cols] = d_ref[:, cols] / cnt

            @pl.when(i < n - 1)
            def _():
                buf_a[pl.ds(t, HALO), cols] = halo_ref[:, cols] / float(w)

            @pl.when(i == n - 1)
            def _():
                buf_a[pl.ds(t, HALO), cols] = jnp.zeros((HALO, POOL_GROUP), F32)

        for g, w in enumerate(POOL_WINDOWS):
            cols = pl.ds(g * POOL_GROUP, POOL_GROUP)
            src, dst, shift = buf_a, buf_b, 1
            while shift < w:
                for r0 in range(0, live, ROW_CHUNK):
                    dst[pl.ds(r0, ROW_CHUNK), cols] = (src[pl.ds(r0, ROW_CHUNK), cols]
                                                       + src[pl.ds(r0 + shift, ROW_CHUNK), cols])
                src, dst, shift = dst, src, 2 * shift
            o_ref[:, cols] = (src[pl.ds(0, t), cols] - d_ref[:, cols]).astype(o_ref.dtype)

    return pl.pallas_call(
        body, name=name, grid=(n,),
        in_specs=[pl.BlockSpec((t, POOL_WIDTH), lambda i: (i, 0)),
                  pl.BlockSpec((HALO, POOL_WIDTH), lambda i: (jnp.minimum((i + 1) * hb, n * hb - 1), 0))],
        out_specs=pl.BlockSpec((t, POOL_WIDTH), lambda i: (i, 0)),
        out_shape=jax.ShapeDtypeStruct((s, POOL_WIDTH), BF),
        scratch_shapes=[pltpu.VMEM((t + tail, POOL_WIDTH), F32), pltpu.VMEM((t + tail, POOL_WIDTH), F32)],
        compiler_params=_params(1))(dpd, dpd)


CHUNK = 512


def _chunks(width, step=CHUNK):
    return [slice(c, c + step) for c in range(0, width, step)]


def pool_mix_gate(pd, wg, uz, scale, *, name, tm=1024):
    s = pd.shape[0]
    g = wg.shape[0]
    tm = _pick(s, tm)

    def body(a_ref, w_ref, z_ref, sc_ref, y_ref):
        mm = jnp.dot(a_ref[...], w_ref[...], preferred_element_type=F32)
        z = z_ref[...]
        y_ref[...] = ((mm * sc_ref[...]) * (z * _sigmoid(z))).astype(y_ref.dtype)

    blk = pl.BlockSpec((tm, POOL_GROUP), lambda i, gi: (i, gi))
    return pl.pallas_call(
        body, name=name, grid=(s // tm, g),
        in_specs=[blk, pl.BlockSpec((None, POOL_GROUP, POOL_GROUP), lambda i, gi: (gi, 0, 0)),
                  pl.BlockSpec((tm, POOL_GROUP), lambda i, gi: (i, g + gi)),
                  pl.BlockSpec((1, POOL_GROUP), lambda i, gi: (0, gi))],
        out_specs=blk, out_shape=jax.ShapeDtypeStruct((s, POOL_WIDTH), BF),
        compiler_params=_params(2))(pd, wg, uz, scale)


def pool_out_dx_gate(dx, w_out, pd, wg, uz, scale, *, name, tm=512, host=None):
    s, d = dx.shape
    tm = _pick(s, tm)
    assert CHUNK == POOL_GROUP

    def body(dx_ref, w_ref, pd_ref, wg_ref, z_ref, sc_ref, dmm_ref, dz_ref, dsc_ref):
        dxv = dx_ref[...].astype(BF)
        parts = []
        for g, c in enumerate(_chunks(POOL_WIDTH)):
            dyv = lax.dot_general(dxv, w_ref[c, :], NT, preferred_element_type=F32)
            z = z_ref[:, c]
            sig = _sigmoid(z)
            mmv = jnp.dot(pd_ref[:, c], wg_ref[g], preferred_element_type=F32)
            scv = sc_ref[:, c]
            dmixed = dyv * (z * sig)
            dmm_ref[:, c] = (dmixed * scv).astype(dmm_ref.dtype)
            dz_ref[:, c] = (dyv * (mmv * scv) * (sig * (1.0 + z * (1.0 - sig)))).astype(dz_ref.dtype)
            parts.append(jnp.sum(dmixed * mmv, axis=0, keepdims=True))

        @pl.when(pl.program_id(0) == 0)
        def _():
            for c, part in zip(_chunks(POOL_WIDTH), parts):
                dsc_ref[:, c] = part

        @pl.when(pl.program_id(0) > 0)
        def _():
            for c, part in zip(_chunks(POOL_WIDTH), parts):
                dsc_ref[:, c] += part

    blk = pl.BlockSpec((tm, POOL_WIDTH), lambda i: (i, 0))
    vec = pl.BlockSpec((1, POOL_WIDTH), lambda i: (0, 0))
    return _call(
        body, name=name, grid=(s // tm,),
        in_specs=[pl.BlockSpec((tm, d), lambda i: (i, 0)), pl.BlockSpec((POOL_WIDTH, d), lambda i: (0, 0)),
                  blk, pl.BlockSpec(wg.shape, lambda i: (0, 0, 0)),
                  pl.BlockSpec((tm, POOL_WIDTH), lambda i: (i, 1)), vec],
        out_specs=[blk, blk, vec],
        out_shape=[jax.ShapeDtypeStruct((s, POOL_WIDTH), BF), jax.ShapeDtypeStruct((s, POOL_WIDTH), BF),
                   jax.ShapeDtypeStruct((1, POOL_WIDTH), F32)],
        args=[dx, w_out, pd, wg, uz, scale], host=host)


def _rope(a, cc, sa, sb):
    return a * cc + pltpu.roll(a, 96, 1) * sa + pltpu.roll(a, 32, 1) * sb


def _unrope(d, cc, sa, sb):
    return d * cc + pltpu.roll(d * sa, 32, 1) + pltpu.roll(d * sb, 96, 1)


def q_proj_rope(qn, wq, cc, sa, sb, *, name, tm=1024, heads=8):
    s, kk = qn.shape
    tm = _pick(s, tm)
    tn = heads * HEAD_PAD

    def body(a_ref, b_ref, cc_ref, sa_ref, sb_ref, o_ref):
        q = jnp.dot(a_ref[...], b_ref[...], preferred_element_type=F32)
        for h in range(heads):
            nope = slice(h * HEAD_PAD, h * HEAD_PAD + QK_NOPE)
            rope = slice(h * HEAD_PAD + QK_NOPE, (h + 1) * HEAD_PAD)
            o_ref[:, nope] = q[:, nope].astype(o_ref.dtype)
            o_ref[:, rope] = _rope(q[:, rope], cc_ref[...], sa_ref[...], sb_ref[...]).astype(o_ref.dtype)

    tab = pl.BlockSpec((tm, 128), lambda i, j: (i, 0))
    return pl.pallas_call(
        body, name=name, grid=(s // tm, N_HEADS // heads),
        in_specs=[pl.BlockSpec((tm, kk), lambda i, j: (i, 0)), pl.BlockSpec((kk, tn), lambda i, j: (0, j)),
                  tab, tab, tab],
        out_specs=pl.BlockSpec((tm, tn), lambda i, j: (i, j)),
        out_shape=jax.ShapeDtypeStruct((s, N_HEADS * HEAD_PAD), BF), compiler_params=_params(2))(qn, wq, cc, sa, sb)


LAT_KV = slice(P_KV, P_KV + KV_LORA)
LAT_KR = slice(P_KR, P_KR + 128)
LAT_Q = slice(P_Q, P_Q + Q_LORA)


def latent_fwd(proj, g_q, g_kv, cc, sa, sb, *, name, t=1024):
    s = proj.shape[0]
    t = _pick(s, t)

    def body(p_ref, gq_ref, gkv_ref, cc_ref, sa_ref, sb_ref, qn_ref, kvn_ref, kr_ref):
        qn_ref[...] = _rms(p_ref[:, LAT_Q], gq_ref[...]).astype(qn_ref.dtype)
        kvn_ref[...] = _rms(p_ref[:, LAT_KV], gkv_ref[...]).astype(kvn_ref.dtype)
        kr_ref[...] = _rope(p_ref[:, LAT_KR], cc_ref[...], sa_ref[...], sb_ref[...]).astype(kr_ref.dtype)

    tab = pl.BlockSpec((t, 128), lambda i: (i, 0))
    return pl.pallas_call(
        body, name=name, grid=(s // t,),
        in_specs=[pl.BlockSpec((t, P_SMALL), lambda i: (i, 0)), pl.BlockSpec((1, Q_LORA), lambda i: (0, 0)),
                  pl.BlockSpec((1, KV_LORA), lambda i: (0, 0)), tab, tab, tab],
        out_specs=[pl.BlockSpec((t, Q_LORA), lambda i: (i, 0)), pl.BlockSpec((t, KV_LORA), lambda i: (i, 0)), tab],
        out_shape=[jax.ShapeDtypeStruct((s, Q_LORA), BF), jax.ShapeDtypeStruct((s, KV_LORA), BF),
                   jax.ShapeDtypeStruct((s, 128), BF)],
        compiler_params=_params(1))(proj, g_q, g_kv, cc, sa, sb)


def latent_bwd(proj, g_q, g_kv, dqn, dkvn, dkr, cc, sa, sb, *, name, t=1024):
    s = proj.shape[0]
    t = _pick(s, t)

    def body(p_ref, gq_ref, gkv_ref, dqn_ref, dkvn_ref, dkr_ref, cc_ref, sa_ref, sb_ref, d_ref, dgq_ref, dgkv_ref):
        dq, dgq = _rms_bwd(p_ref[:, LAT_Q], gq_ref[...], dqn_ref[...])
        dkv, dgkv = _rms_bwd(p_ref[:, LAT_KV], gkv_ref[...], dkvn_ref[...])
        d_ref[:, LAT_Q] = dq.astype(d_ref.dtype)
        d_ref[:, LAT_KV] = dkv.astype(d_ref.dtype)
        d_ref[:, LAT_KR] = _unrope(dkr_ref[...], cc_ref[...], sa_ref[...], sb_ref[...]).astype(d_ref.dtype)
        _accumulate(dgq_ref, dgq)
        _accumulate(dgkv_ref, dgkv)

    tab = pl.BlockSpec((t, 128), lambda i: (i, 0))
    small = pl.BlockSpec((t, P_SMALL), lambda i: (i, 0))
    gq = pl.BlockSpec((1, Q_LORA), lambda i: (0, 0))
    gkv = pl.BlockSpec((1, KV_LORA), lambda i: (0, 0))
    return pl.pallas_call(
        body, name=name, grid=(s // t,),
        in_specs=[small, gq, gkv, pl.BlockSpec((t, Q_LORA), lambda i: (i, 0)),
                  pl.BlockSpec((t, KV_LORA), lambda i: (i, 0)), tab, tab, tab, tab],
        out_specs=[small, gq, gkv],
        out_shape=[jax.ShapeDtypeStruct((s, P_SMALL), BF), jax.ShapeDtypeStruct((1, Q_LORA), F32),
                   jax.ShapeDtypeStruct((1, KV_LORA), F32)],
        compiler_params=_params(1))(proj, g_q, g_kv, dqn, dkvn, dkr, cc, sa, sb)


def mla_out_dx_gate(dx, w_out, o, proj, *, name, tq):
    s, d = dx.shape
    nq = s // tq
    t = min(512, tq)
    per = tq // t

    def body(dx_ref, w_ref, o_ref, p_ref, do_ref, dz_ref, dl_ref):
        dxv = dx_ref[...].astype(BF)
        lane = lax.broadcasted_iota(jnp.int32, (t, 128), 1)
        deltas = jnp.zeros((t, 128), F32)
        for c in _chunks(MLA_WIDTH):
            dy_c = lax.dot_general(dxv, w_ref[c, :], NT, preferred_element_type=F32)
            for h in range(c.start // V_DIM, c.stop // V_DIM):
                hc = slice(h * V_DIM, (h + 1) * V_DIM)
                z = p_ref[:, slice(P_Z + hc.start, P_Z + hc.stop)]
                sig = _sigmoid(z)
                dyv = dy_c[:, hc.start - c.start:hc.stop - c.start]
                ov = o_ref[:, hc]
                dov = dyv * (z * sig)
                do_ref[:, hc] = dov.astype(do_ref.dtype)
                dz_ref[:, hc] = (dyv * ov * (sig * (1.0 + z * (1.0 - sig)))).astype(dz_ref.dtype)
                deltas = jnp.where(lane == h, jnp.sum(dov * ov, axis=-1, keepdims=True), deltas)
        rows = deltas.T
        for h in range(N_HEADS):
            dl_ref[h] = jnp.broadcast_to(rows[h:h + 1, :], (8, t))

    blk = pl.BlockSpec((t, MLA_WIDTH), lambda i: (i, 0))
    return pl.pallas_call(
        body, name=name, grid=(s // t,),
        in_specs=[pl.BlockSpec((t, d), lambda i: (i, 0)), pl.BlockSpec((MLA_WIDTH, d), lambda i: (0, 0)),
                  blk, pl.BlockSpec((t, P_WIDTH), lambda i: (i, 0))],
        out_specs=[blk, blk, pl.BlockSpec((N_HEADS, None, 8, t), lambda i: (0, i // per, 0, i % per))],
        out_shape=[jax.ShapeDtypeStruct((s, MLA_WIDTH), BF), jax.ShapeDtypeStruct((s, MLA_WIDTH), BF),
                   jax.ShapeDtypeStruct((N_HEADS, nq, 8, tq), F32)],
        compiler_params=_params(1))(dx, w_out, o, proj)


FWD_GROUPS = (4, 2, 1)
BWD_GROUPS = (2, 1)


def _for_groups(first, count, groups, fn):
    lead = groups[-1]
    for g in groups[:-1][::-1]:
        lead = jnp.where(count >= g, g, lead)
    for g in groups:
        @pl.when(lead == g)
        def _(g=g):
            fn(first, g, True)
    first = first + lead
    count = count - lead
    for g in groups:
        n = count // g

        def one(p, carry, g=g, first=first):
            fn(first + p * g, g, False)
            return carry

        lax.fori_loop(0, n, one, 0)
        first = first + n * g
        count = count - n * g


def attn_fwd(qr, kv, krr, proj, *, name, tq):
    s = qr.shape[0]
    nq = s // tq
    z_blk = P_Z // V_DIM

    def body(kn_ref, v_ref, kr_ref, q_ref, z_ref, o_ref, y_ref, lse_ref, acc_sc, m_sc):
        j = pl.program_id(1)

        @pl.when(j == 0)
        def _():
            acc_sc[...] = jnp.zeros((nq, 2 * V_DIM, tq), F32)
            m_sc[...] = jnp.full((nq, 8, tq), NEG, F32)

        k = jnp.concatenate([kn_ref[...], kr_ref[...]], axis=1)
        vxt = jnp.concatenate([v_ref[...].astype(F32).T.astype(BF), jnp.ones((V_DIM, tq), BF)], axis=0)

        def update(i, n_tiles, masked):
            rows = pl.ds(pl.multiple_of(i * tq, tq), n_tiles * tq)
            st = lax.dot_general(k, q_ref[rows, :], NT, preferred_element_type=F32) * SCALE_LOG2E
            if masked:
                krow = lax.broadcasted_iota(jnp.int32, (tq, n_tiles * tq), 0)
                qcol = lax.broadcasted_iota(jnp.int32, (tq, n_tiles * tq), 1)
                st = jnp.where(qcol >= krow, st, NEG)
            m_prev = jnp.concatenate([m_sc[i + n, pl.ds(0, 1), :] for n in range(n_tiles)], axis=1)
            m_new = jnp.maximum(m_prev, jnp.max(st, axis=0, keepdims=True))
            alpha = jnp.exp2(m_prev - m_new)
            pt = jnp.exp2(st - m_new).astype(BF)
            pv_t = jnp.dot(vxt, pt, preferred_element_type=F32)
            for n in range(n_tiles):
                cols = slice(n * tq, (n + 1) * tq)
                acc_sc[i + n] = alpha[:, cols] * acc_sc[i + n] + pv_t[:, cols]
                m_sc[i + n, pl.ds(0, 1), :] = m_new[:, cols]

        _for_groups(j, nq - j, FWD_GROUPS, update)
        l = acc_sc[j, V_DIM:, :]
        o = (acc_sc[j, :V_DIM, :] / l).T
        o_ref[...] = o
        z = z_ref[...]
        y_ref[...] = (o * (z * _sigmoid(z))).astype(y_ref.dtype)
        lse_ref[...] = m_sc[j, pl.ds(0, 1), :] + jnp.log2(l[:8, :])

    tile = pl.BlockSpec((tq, V_DIM), lambda h, j: (j, h))
    return pl.pallas_call(
        body, name=name, grid=(N_HEADS, nq),
        in_specs=[pl.BlockSpec((tq, QK_NOPE), lambda h, j: (j, 2 * h)),
                  pl.BlockSpec((tq, V_DIM), lambda h, j: (j, 2 * h + 1)),
                  pl.BlockSpec((tq, 128), lambda h, j: (j, 0)),
                  pl.BlockSpec((s, HEAD_PAD), lambda h, j: (0, h)),
                  pl.BlockSpec((tq, V_DIM), lambda h, j: (j, z_blk + h))],
        out_specs=[tile, tile, pl.BlockSpec((None, None, 8, tq), lambda h, j: (h, j, 0, 0))],
        out_shape=[jax.ShapeDtypeStruct((s, N_HEADS * V_DIM), F32),
                   jax.ShapeDtypeStruct((s, N_HEADS * V_DIM), BF),
                   jax.ShapeDtypeStruct((N_HEADS, nq, 8, tq), F32)],
        scratch_shapes=[pltpu.VMEM((nq, 2 * V_DIM, tq), F32), pltpu.VMEM((nq, 8, tq), F32)],
        compiler_params=_params(2))(kv, kv, krr, qr, proj)


def attn_bwd(qr, kv, krr, do, lse, delta, cc, sa, sb, *, name, tq):
    s = qr.shape[0]
    nq = s // tq

    def body(kn_ref, v_ref, kr_ref, q_ref, do_ref, lse_ref, dl_ref, cc_ref, sa_ref, sb_ref,
             dkv_ref, dkr_ref, dq_ref, dq_sc, dk_sc, dv_sc):
        h = pl.program_id(0)
        j = pl.program_id(1)

        @pl.when(j == 0)
        def _():
            dq_sc[...] = jnp.zeros((s, HEAD_PAD), F32)

        dk_sc[...] = jnp.zeros((tq, HEAD_PAD), F32)
        dv_sc[...] = jnp.zeros((tq, V_DIM), F32)
        k = jnp.concatenate([kn_ref[...], kr_ref[...]], axis=1)
        v = v_ref[...]

        def step(i, n_tiles, masked):
            r0 = pl.multiple_of(i * tq, tq)
            rows = pl.ds(r0, n_tiles * tq)
            q = q_ref[rows, :]
            dov = do_ref[rows, :]
            lse_row = jnp.concatenate([lse_ref[i + n, pl.ds(0, 1), :] for n in range(n_tiles)], axis=1)
            dl_row = jnp.concatenate([dl_ref[i + n, pl.ds(0, 1), :] for n in range(n_tiles)], axis=1)
            st = lax.dot_general(k, q, NT, preferred_element_type=F32) * SCALE_LOG2E
            if masked:
                krow = lax.broadcasted_iota(jnp.int32, (tq, n_tiles * tq), 0)
                qcol = lax.broadcasted_iota(jnp.int32, (tq, n_tiles * tq), 1)
                st = jnp.where(qcol >= krow, st, NEG)
            pt = jnp.exp2(st - lse_row)
            dpt = lax.dot_general(v, dov, NT, preferred_element_type=F32)
            dst = (pt * (dpt - dl_row)).astype(BF)
            dv_sc[...] += jnp.dot(pt.astype(BF), dov, preferred_element_type=F32)
            dk_sc[...] += jnp.dot(dst, q, preferred_element_type=F32)
            dq_sc[rows, :] += lax.dot_general(dst, k, TN, preferred_element_type=F32)

        _for_groups(j, nq - j, BWD_GROUPS, step)
        dkv_ref[:, :QK_NOPE] = (dk_sc[:, :QK_NOPE] * SCALE).astype(dkv_ref.dtype)
        dkv_ref[:, QK_NOPE:] = dv_sc[...].astype(dkv_ref.dtype)
        mine = pl.ds(pl.multiple_of(j * tq, tq), tq)
        dkr = dk_sc[:, QK_NOPE:] * SCALE

        @pl.when(h == 0)
        def _():
            dkr_ref[mine, :] = dkr

        @pl.when(h > 0)
        def _():
            dkr_ref[mine, :] += dkr

        dq_ref[:, :QK_NOPE] = (dq_sc[mine, :QK_NOPE] * SCALE).astype(dq_ref.dtype)
        dq_ref[:, QK_NOPE:] = _unrope(dq_sc[mine, QK_NOPE:] * SCALE, cc_ref[...], sa_ref[...],
                                      sb_ref[...]).astype(dq_ref.dtype)

    rows = pl.BlockSpec((None, nq, 8, tq), lambda h, j: (h, 0, 0, 0))
    tab = pl.BlockSpec((tq, 128), lambda h, j: (j, 0))
    return pl.pallas_call(
        body, name=name, grid=(N_HEADS, nq),
        in_specs=[pl.BlockSpec((tq, QK_NOPE), lambda h, j: (j, 2 * h)),
                  pl.BlockSpec((tq, V_DIM), lambda h, j: (j, 2 * h + 1)), tab,
                  pl.BlockSpec((s, HEAD_PAD), lambda h, j: (0, h)),
                  pl.BlockSpec((s, V_DIM), lambda h, j: (0, h)), rows, rows, tab, tab, tab],
        out_specs=[pl.BlockSpec((tq, 256), lambda h, j: (j, h)),
                   pl.BlockSpec((s, 128), lambda h, j: (0, 0)),
                   pl.BlockSpec((tq, HEAD_PAD), lambda h, j: (j, h))],
        out_shape=[jax.ShapeDtypeStruct((s, N_HEADS * 256), BF),
                   jax.ShapeDtypeStruct((s, 128), F32),
                   jax.ShapeDtypeStruct((s, N_HEADS * HEAD_PAD), BF)],
        scratch_shapes=[pltpu.VMEM((s, HEAD_PAD), F32), pltpu.VMEM((tq, HEAD_PAD), F32),
                        pltpu.VMEM((tq, V_DIM), F32)],
        compiler_params=_params(2))(kv, kv, krr, qr, do, lse, delta, cc, sa, sb)


def adamw(w, g, m, v, *, name, t=512):
    r, c = w.shape
    t = r if r % t else t
    c1 = 1.0 - ADAM_B1 ** ADAM_STEP
    c2 = 1.0 - ADAM_B2 ** ADAM_STEP

    def body(w_ref, g_ref, m_ref, v_ref, d_ref, nm_ref, nv_ref):
        gv = g_ref[...]
        nm = ADAM_B1 * m_ref[...] + (1.0 - ADAM_B1) * gv
        nv = ADAM_B2 * v_ref[...] + (1.0 - ADAM_B2) * (gv * gv)
        nm_ref[...] = nm
        nv_ref[...] = nv
        d_ref[...] = -ADAM_LR * ((nm / c1) / (jnp.sqrt(nv / c2) + ADAM_EPS) + ADAM_WD * w_ref[...])

    blk = pl.BlockSpec((t, c), lambda i: (i, 0))
    return pl.pallas_call(
        body, name=name, grid=(r // t,), in_specs=[blk] * 4, out_specs=[blk] * 3,
        out_shape=[jax.ShapeDtypeStruct((r, c), F32)] * 3, compiler_params=_params(1))(w, g, m, v)


def sum_devices(parts, *, name):
    def body(p_ref, o_ref):
        acc = p_ref[pl.ds(0, SV_ROWS), :]
        for d in range(1, 8):
            acc = acc + p_ref[pl.ds(d * SV_ROWS, SV_ROWS), :]
        o_ref[...] = acc

    return pl.pallas_call(body, name=name, out_shape=jax.ShapeDtypeStruct((SV_ROWS, SV_COLS), F32))(parts)


def add_halves(g, rb, c_idx, *, name, rows):
    nq, r2, cc = rb.shape
    nb = r2 // rows

    def body(c_ref, g_ref, r_ref, o_ref):
        o_ref[...] = (g_ref[...] + r_ref[...]).astype(o_ref.dtype)

    grid_spec = pltpu.PrefetchScalarGridSpec(
        num_scalar_prefetch=1, grid=(nq, nb),
        in_specs=[pl.BlockSpec((None, rows, cc), lambda q, i, c: (q, c[0] * nb + i, 0)),
                  pl.BlockSpec((None, rows, cc), lambda q, i, c: (q, i, 0))],
        out_specs=pl.BlockSpec((None, rows, cc), lambda q, i, c: (q, i, 0)))
    return pl.pallas_call(body, name=name, grid_spec=grid_spec,
                          out_shape=jax.ShapeDtypeStruct((nq, r2, cc), BF),
                          compiler_params=_params(2))(c_idx, g, rb)


def sum_chips(rc, c_idx, *, name, rows):
    nq, r2, cc = rc.shape
    nb = r2 // rows

    def body(c_ref, r_ref, o_ref):
        parts = [r_ref[q].astype(F32) for q in range(4)]
        o_ref[...] = ((parts[0] + parts[1]) + parts[2]) + parts[3]

    grid_spec = pltpu.PrefetchScalarGridSpec(
        num_scalar_prefetch=1, grid=(nb,),
        in_specs=[pl.BlockSpec((nq, rows, cc), lambda i, c: (0, i, 0))],
        out_specs=pl.BlockSpec((rows, cc), lambda i, c: (c[0] * nb + i, 0)))
    return pl.pallas_call(body, name=name, grid_spec=grid_spec,
                          out_shape=jax.ShapeDtypeStruct((2 * r2, cc), F32),
                          compiler_params=_params(1))(c_idx, rc)


def _place():
    return lax.axis_index("x"), lax.axis_index("y"), lax.axis_index("c")


def all_gather8(xs, *, name, own_half):
    m = xs.shape[0] // 2 if own_half else xs.shape[0]
    n = xs.shape[1]

    def body(x_ref, out_ref, send_sems, recv_sems, local_sem):
        x, y, c = _place()
        me, sibling = (x, y, c), (x, y, 1 - c)
        chips = [(1 - x, y), (x, 1 - y), (1 - x, 1 - y)]
        src_own = x_ref.at[pl.ds(c * m, m), :] if own_half else x_ref

        def rows(px, py, pc):
            return out_ref.at[pl.ds((4 * px + 2 * py + pc) * m, m), :]

        def copy(k, block, to, src=None):
            return pltpu.make_async_remote_copy(
                src_ref=rows(*block) if src is None else src, dst_ref=rows(*block),
                send_sem=send_sems.at[k], recv_sem=recv_sems.at[k], device_id=to, device_id_type=MESH)

        mine = pltpu.make_async_copy(src_own, rows(*me), local_sem)
        mine.start()
        first = [copy(0, me, sibling, src=src_own)]
        first += [copy(1 + j, me, (*chip, c), src=src_own) for j, chip in enumerate(chips)]
        for cp in first:
            cp.start()
        passed = [copy(4 + j, (*chip, c), sibling) for j, chip in enumerate(chips)]
        for j, chip in enumerate(chips):
            copy(1 + j, (*chip, c), me).wait_recv()
            passed[j].start()
        copy(0, sibling, me).wait_recv()
        for j, chip in enumerate(chips):
            copy(4 + j, (*chip, 1 - c), me).wait_recv()
        for cp in first + passed:
            cp.wait_send()
        mine.wait()

    return pl.pallas_call(
        body, name=name, out_shape=jax.ShapeDtypeStruct((8 * m, n), xs.dtype),
        in_specs=[pl.BlockSpec(memory_space=pl.ANY)], out_specs=pl.BlockSpec(memory_space=pl.ANY),
        scratch_shapes=[pltpu.SemaphoreType.DMA((7,)), pltpu.SemaphoreType.DMA((7,)), pltpu.SemaphoreType.DMA],
    )(xs)


def _other_chips():
    x, y, c = _place()
    return [(1 - x, y), (x, 1 - y), (1 - x, 1 - y)]


def _remote(src, dst, send_sems, recv_sems, k, to):
    return pltpu.make_async_remote_copy(src_ref=src, dst_ref=dst, send_sem=send_sems.at[k], recv_sem=recv_sems.at[k],
                                        device_id=to, device_id_type=MESH)


def gather_ici(xs):
    r, cc = xs.shape
    m = r // 2

    def copies(ins, outs, ss, rs, landing):
        x, y, c = _place()
        half = pl.ds(c * m, m)
        return [_remote(ins[0].at[half, :], outs[0].at[(2 * cx + cy) if landing else (2 * x + y), half, :],
                        ss, rs, j, (cx, cy, c)) for j, (cx, cy) in enumerate(_other_chips())]

    def start(ins, outs, ss, rs, ls):
        for cp in copies(ins, outs, ss, rs, False):
            cp.start()

    def wait(ins, outs, ss, rs, ls):
        for cp in copies(ins, outs, ss, rs, True):
            cp.wait_recv()
        for cp in copies(ins, outs, ss, rs, False):
            cp.wait_send()

    return Exchange((xs,), (jax.ShapeDtypeStruct((4, r, cc), xs.dtype),), {}, 3, start, wait)


def gather_forward(buf):
    m = buf.shape[1] // 2

    def copies(outs, ss, rs, landing):
        x, y, c = _place()
        half = pl.ds(((1 - c) if landing else c) * m, m)
        return [_remote(outs[0].at[2 * cx + cy, half, :], outs[0].at[2 * cx + cy, half, :], ss, rs, j, (x, y, 1 - c))
                for j, (cx, cy) in enumerate(_other_chips())]

    def start(ins, outs, ss, rs, ls):
        for cp in copies(outs, ss, rs, False):
            cp.start()

    def wait(ins, outs, ss, rs, ls):
        for cp in copies(outs, ss, rs, True):
            cp.wait_recv()
        for cp in copies(outs, ss, rs, False):
            cp.wait_send()

    return Exchange((buf,), (jax.ShapeDtypeStruct(buf.shape, buf.dtype),), {0: 0}, 3, start, wait)


def swap_halves(g):
    nq, r, cc = g.shape
    r2 = r // 2

    def copy(ins, outs, ss, rs):
        x, y, c = _place()
        return _remote(ins[0].at[:, pl.ds((1 - c) * r2, r2), :], outs[0], ss, rs, 0, (x, y, 1 - c))

    def start(ins, outs, ss, rs, ls):
        copy(ins, outs, ss, rs).start()

    def wait(ins, outs, ss, rs, ls):
        copy(ins, outs, ss, rs).wait()

    return Exchange((g,), (jax.ShapeDtypeStruct((nq, r2, cc), g.dtype),), {}, 1, start, wait)


def exchange_chips(p):
    def own(ins, outs, ls):
        x, y, c = _place()
        return pltpu.make_async_copy(ins[0].at[2 * x + y], outs[0].at[2 * x + y], ls)

    def copies(ins, outs, ss, rs, landing):
        x, y, c = _place()
        return [_remote(ins[0].at[2 * cx + cy], outs[0].at[(2 * cx + cy) if landing else (2 * x + y)],
                        ss, rs, j, (cx, cy, c)) for j, (cx, cy) in enumerate(_other_chips())]

    def start(ins, outs, ss, rs, ls):
        own(ins, outs, ls).start()
        for cp in copies(ins, outs, ss, rs, False):
            cp.start()

    def wait(ins, outs, ss, rs, ls):
        for cp in copies(ins, outs, ss, rs, True):
            cp.wait_recv()
        for cp in copies(ins, outs, ss, rs, False):
            cp.wait_send()
        own(ins, outs, ls).wait()

    return Exchange((p,), (jax.ShapeDtypeStruct(p.shape, p.dtype),), {}, 3, start, wait)


def join_halves(tot):
    r2 = tot.shape[0] // 2

    def copy(outs, ss, rs, landing):
        x, y, c = _place()
        half = outs[0].at[pl.ds(((1 - c) if landing else c) * r2, r2), :]
        return _remote(half, half, ss, rs, 0, (x, y, 1 - c))

    def start(ins, outs, ss, rs, ls):
        copy(outs, ss, rs, False).start()

    def wait(ins, outs, ss, rs, ls):
        copy(outs, ss, rs, True).wait_recv()
        copy(outs, ss, rs, False).wait_send()

    return Exchange((tot,), (jax.ShapeDtypeStruct(tot.shape, tot.dtype),), {0: 0}, 1, start, wait)


def _pack_shard(blocks, small_vec=None):
    parts = [w.reshape(-1, PACK_C).astype(BF) for w in blocks]
    if small_vec is not None:
        srow = lax.bitcast_convert_type(small_vec, BF).reshape(1, PACK_C)
        parts.append(jnp.pad(srow, ((0, PACK_PAD - 1), (0, 0))))
    return jnp.concatenate(parts, axis=0)


def _split_rows(a, rows, axis):
    out, off = [], 0
    for n in rows:
        out.append(lax.slice_in_dim(a, off, off + n, axis=axis))
        off += n
    return out


def _unpack_pool(gw):
    p_in, p_grp, p_out = _split_rows(gw, POOL_ROWS, 1)
    return dict(
        pool_w_in=p_in.reshape(4, D_MODEL, 1024).transpose(1, 0, 2).reshape(D_MODEL, 2 * POOL_WIDTH),
        pool_w_group=p_grp.reshape(4, 4, 128, POOL_GROUP).transpose(1, 0, 2, 3).reshape(4, POOL_GROUP, POOL_GROUP),
        pool_w_out=p_out.reshape(POOL_WIDTH, D_MODEL))


def _unpack_mla(gw):
    m_in, m_qb, m_kvb, m_out, small = _split_rows(gw, MLA_ROWS + (PACK_PAD,), 1)
    w = {}
    win = m_in.reshape(4, D_MODEL, 688).transpose(1, 0, 2).reshape(D_MODEL, 2752)
    w["mla_w_in"] = jnp.concatenate(
        [win[:, 384:640], win[:, 640:704], jnp.zeros((D_MODEL, 64), BF), win[:, 0:384], win[:, 704:]], axis=1)
    wq = m_qb.reshape(4, Q_LORA, 768).transpose(1, 0, 2).reshape(Q_LORA, N_HEADS, QK_NOPE + QK_ROPE)
    w["mla_w_q_b"] = jnp.pad(wq, ((0, 0), (0, 0), (0, HEAD_PAD - QK_NOPE - QK_ROPE))).reshape(Q_LORA, N_HEADS * HEAD_PAD)
    w["mla_w_kv_b"] = m_kvb.reshape(4, KV_LORA, 1024).transpose(1, 0, 2).reshape(KV_LORA, 4096)
    w["mla_w_out"] = m_out.reshape(MLA_WIDTH, D_MODEL)
    small = lax.bitcast_convert_type(small[:, 0, :].reshape(4, 512, 2), F32)
    w["mla_norm"] = small[:, :256].reshape(1, D_MODEL)
    w["mla_q_norm"] = small[:, 256:352].reshape(1, Q_LORA)
    w["mla_kv_norm"] = small[:, 352:416].reshape(1, KV_LORA)
    return w


def _pack_pool_grads(g):
    return jnp.concatenate([
        g["pool_w_in"],
        g["pool_w_group"].reshape(4, 4, 128, POOL_GROUP).transpose(1, 0, 2, 3).reshape(4, 256, PACK_C),
        g["pool_w_out"].reshape(4, 512, PACK_C)], axis=1)


def _pack_mla_grads(g):
    return jnp.concatenate([
        g["mla_w_in"].reshape(D_MODEL, 4, 688).transpose(1, 0, 2).reshape(4, 688, PACK_C),
        g["mla_w_q_b"].reshape(Q_LORA, 4, 768).transpose(1, 0, 2).reshape(4, 288, PACK_C),
        g["mla_w_kv_b"],
        g["mla_w_out"].reshape(4, 512, PACK_C),
        jnp.zeros((4, PACK_PAD, PACK_C), F32)], axis=1)


def kernel(x, positions, pool_norm, pool_w_in, pool_w_group, pool_scale, pool_w_out, mla_norm, mla_w_in, mla_q_norm, mla_w_q_b, mla_kv_norm, mla_w_kv_b, mla_w_out, final_norm, loss_target, m_pool_norm, m_pool_w_in, m_pool_w_group, m_pool_scale, m_pool_w_out, m_mla_norm, m_mla_w_in, m_mla_q_norm, m_mla_w_q_b, m_mla_kv_norm, m_mla_w_kv_b, m_mla_w_out, m_final_norm, v_pool_norm, v_pool_w_in, v_pool_w_group, v_pool_scale, v_pool_w_out, v_mla_norm, v_mla_w_in, v_mla_q_norm, v_mla_w_q_b, v_mla_kv_norm, v_mla_w_kv_b, v_mla_w_out, v_final_norm):
    s = x.shape[1]
    tq = min(1024, s)
    x0 = x.reshape(s, D_MODEL)
    tgt = loss_target.reshape(s, D_MODEL)
    cx, cy, cc_idx = _place()
    chip = 2 * cx + cy

    big_names = ("pool_w_in", "pool_w_group", "pool_w_out", "mla_w_in", "mla_w_q_b", "mla_w_kv_b", "mla_w_out")
    big_w = dict(zip(big_names, (pool_w_in, pool_w_group, pool_w_out, mla_w_in, mla_w_q_b, mla_w_kv_b, mla_w_out)))
    big_m = dict(zip(big_names, (m_pool_w_in, m_pool_w_group, m_pool_w_out, m_mla_w_in, m_mla_w_q_b, m_mla_w_kv_b, m_mla_w_out)))
    big_v = dict(zip(big_names, (v_pool_w_in, v_pool_w_group, v_pool_w_out, v_mla_w_in, v_mla_w_q_b, v_mla_w_kv_b, v_mla_w_out)))

    small_vec = jnp.concatenate([mla_norm.reshape(-1), mla_q_norm.reshape(-1), mla_kv_norm.reshape(-1),
                                 jnp.zeros((96,), F32)])
    pool_packed = _pack_shard([big_w[n] for n in big_names[:3]])
    mla_packed = _pack_shard([big_w[n] for n in big_names[3:]], small_vec)
    w = _unpack_pool(all_gather8(pool_packed, name="gather_pool_weights", own_half=True).reshape(4, POOL_R, PACK_C))
    g_pool = pool_norm.reshape(1, D_MODEL)
    g_final = final_norm.reshape(1, D_MODEL)
    sc_pool = pool_scale.reshape(1, POOL_WIDTH)

    inv_freq = 1.0 / (ROPE_THETA ** (jnp.arange(0, QK_ROPE, 2, dtype=F32) / QK_ROPE))
    ang = positions.reshape(s).astype(F32)[:, None] * inv_freq
    cos, sin = jnp.cos(ang), jnp.sin(ang)
    z32, z64, z96 = (jnp.zeros((s, n), F32) for n in (32, 64, 96))
    t_cc = jnp.concatenate([cos, cos, z64], axis=1)
    t_sa = jnp.concatenate([-sin, z96], axis=1)
    t_sb = jnp.concatenate([z32, sin, z64], axis=1)

    h0 = norm_fwd(x0, g_pool, name="pool_norm_fwd")
    uz, mla_land = mm_nn(h0, w["pool_w_in"], name="pool_in_proj", out_dtype=F32, host=gather_ici(mla_packed))
    pd = pool_prep(uz, name="pool_window")
    y1 = pool_mix_gate(pd, w["pool_w_group"], uz, sc_pool, name="pool_group_mix")
    x1, mla_land = mm_nn(y1, w["pool_w_out"], name="pool_out_proj", out_dtype=F32, add=x0,
                         host=gather_forward(mla_land))
    w.update(_unpack_mla(lax.dynamic_update_slice_in_dim(mla_land, mla_packed[None], chip, axis=0)))

    h1 = norm_fwd(x1, w["mla_norm"], name="mla_norm_fwd")
    proj = mm_nn(h1, w["mla_w_in"], name="mla_in_proj", out_dtype=F32, tn=P_WIDTH // 2)
    qn, kvn, krr = latent_fwd(proj, w["mla_q_norm"], w["mla_kv_norm"], t_cc, t_sa, t_sb, name="mla_latent_fwd")
    qr = q_proj_rope(qn, w["mla_w_q_b"], t_cc, t_sa, t_sb, name="mla_q_proj")
    kv = mm_nn(kvn, w["mla_w_kv_b"], name="mla_kv_proj", out_dtype=BF, tn=2048)
    o, y2, lse = attn_fwd(qr, kv, krr, proj, name="mla_attn_fwd", tq=tq)
    dx2, d_final, loss_part = mm_nn_loss(y2, w["mla_w_out"], x1, g_final, tgt, name="mla_out_proj_loss")

    grads = {}
    grads["mla_w_out"] = mm_tn(y2, dx2, name="mla_out_proj_dw")
    do, dz2, delta = mla_out_dx_gate(dx2, w["mla_w_out"], o, proj, name="mla_out_proj_dx", tq=tq)
    dkv, dkr, dq_pre = attn_bwd(qr, kv, krr, do, lse, delta, t_cc, t_sa, t_sb, name="mla_attn_bwd", tq=tq)
    dqn = mm_nt(dq_pre, w["mla_w_q_b"], name="mla_q_proj_dx", out_dtype=F32, tn=Q_LORA, tk=4096)
    g_qb = mm_tn(qn, dq_pre, name="mla_q_proj_dw", tm=Q_LORA, tn=2048)
    dkvn = mm_nt(dkv, w["mla_w_kv_b"], name="mla_kv_proj_dx", out_dtype=F32, tn=KV_LORA, tk=4096)
    grads["mla_w_kv_b"] = mm_tn(kvn, dkv, name="mla_kv_proj_dw", tm=KV_LORA, tk=4096, by_column_block=True)
    dsmall, d_qnorm, d_kvnorm = latent_bwd(proj, w["mla_q_norm"], w["mla_kv_norm"], dqn, dkvn, dkr,
                                           t_cc, t_sa, t_sb, name="mla_latent_bwd")
    dx1, d_mnorm = mm_nt_norm_bwd(dz2, w["mla_w_in"][:, P_Z:], (dsmall, w["mla_w_in"]), x1, w["mla_norm"], dx2,
                                  name="mla_in_proj_dx")
    g_in_a = mm_tn(h1, dsmall, name="mla_in_proj_dw_a", tn=P_SMALL, tk=4096)
    g_in_b = mm_tn(h1, dz2, name="mla_in_proj_dw_b", tk=4096)

    g_in = jnp.concatenate([g_in_a, g_in_b], axis=1)
    grads["mla_w_in"] = jnp.concatenate([g_in[:, P_Q:P_Z], g_in[:, P_KV:P_KV + KV_LORA],
                                         g_in[:, P_KR:P_KR + QK_ROPE], g_in[:, P_Z:]], axis=1)
    grads["mla_w_q_b"] = g_qb.reshape(Q_LORA, N_HEADS, HEAD_PAD)[:, :, :QK_NOPE + QK_ROPE].reshape(Q_LORA, -1)
    core_idx = cc_idx.reshape(1).astype(jnp.int32)
    gp_mla = _pack_mla_grads(grads)

    grads["pool_w_out"], sib = mm_tn(y1, dx1, name="pool_out_proj_dw", host=swap_halves(gp_mla))
    pre = add_halves(gp_mla, sib, core_idx, name="mla_grad_add_halves", rows=MLA_R // 2)
    dmm, dz1, d_scale, got = pool_out_dx_gate(dx1, w["pool_w_out"], pd, w["pool_w_group"], uz, sc_pool,
                                              name="pool_out_proj_dx", host=exchange_chips(pre))
    tot = sum_chips(got, core_idx, name="mla_grad_sum_chips", rows=MLA_R // 2)
    dpd, red_mla = gmm_nt(dmm, w["pool_w_group"], name="pool_group_mix_dx", host=join_halves(tot))
    grads["pool_w_group"] = gmm_tn(pd, dmm, 4, name="pool_group_mix_dw", tk=4096)
    du = pool_prep_bwd(dpd, name="pool_window_bwd")
    g_pin_u = mm_tn(h0, du, name="pool_in_proj_dw_u", tk=4096, by_column_block=True)
    g_pin_z = mm_tn(h0, dz1, name="pool_in_proj_dw_z", tk=4096, by_column_block=True)
    grads["pool_w_in"] = jnp.concatenate([g_pin_u, g_pin_z], axis=0)

    gp_pool = _pack_pool_grads(grads)
    dh0, sib = mm_nt(du, w["pool_w_in"], name="pool_in_proj_dx_u", out_dtype=F32, host=swap_halves(gp_pool))
    pre = add_halves(gp_pool, sib, core_idx, name="pool_grad_add_halves", rows=POOL_R // 2)
    grad_x, d_pnorm, got = mm_nt_norm_bwd(dz1, w["pool_w_in"], dh0, x0, g_pool, dx1, name="pool_in_proj_dx_z",
                                          b_col=POOL_WIDTH, host=exchange_chips(pre))
    tot = sum_chips(got, core_idx, name="pool_grad_sum_chips", rows=POOL_R // 2)
    red_pool = run_exchange(join_halves(tot), name="pool_grad_join_halves")[0]
    red_parts = _split_rows(red_pool, POOL_ROWS, 0) + _split_rows(red_mla, MLA_ROWS, 0)

    sv = jnp.concatenate([d_pnorm.reshape(-1), d_scale.reshape(-1), d_final.reshape(-1), d_mnorm.reshape(-1),
                          d_qnorm.reshape(-1), d_kvnorm.reshape(-1), loss_part[0, :1],
                          jnp.zeros((SV_ROWS * SV_COLS - SV_OFF["loss"] - 1,), F32)]).reshape(SV_ROWS, SV_COLS)
    sv_all = all_gather8(sv, name="gather_small_grads", own_half=False)
    sv_sum = sum_devices(sv_all, name="sum_small_grads").reshape(-1)
    loss = sv_sum[SV_OFF["loss"]]

    def sv_take(key, n):
        return lax.slice_in_dim(sv_sum, SV_OFF[key], SV_OFF[key] + n)

    out_g, out_d, out_m, out_v = {}, {}, {}, {}
    for name, part in zip(big_names, red_parts):
        shp = big_w[name].shape
        g2 = part.reshape(shp)
        two_d = (-1, shp[-1])
        d_, m_, v_ = adamw(big_w[name].reshape(two_d), g2.reshape(two_d), big_m[name].reshape(two_d),
                           big_v[name].reshape(two_d), name="adamw_" + name)
        out_g[name], out_d[name], out_m[name], out_v[name] = g2, d_.reshape(shp), m_.reshape(shp), v_.reshape(shp)

    small = [
        ("pool_norm", pool_norm, m_pool_norm, v_pool_norm, sv_take("pool_norm", 1024)),
        ("pool_scale", pool_scale, m_pool_scale, v_pool_scale, sv_take("pool_scale", 2048)),
        ("final_norm", final_norm, m_final_norm, v_final_norm, sv_take("final_norm", 1024)),
        ("mla_norm", mla_norm, m_mla_norm, v_mla_norm,
         lax.dynamic_slice_in_dim(sv_take("mla_norm", 1024), chip * 256, 256)),
        ("mla_q_norm", mla_q_norm, m_mla_q_norm, v_mla_q_norm,
         lax.dynamic_slice_in_dim(sv_take("q_norm", 384), chip * 96, 96)),
        ("mla_kv_norm", mla_kv_norm, m_mla_kv_norm, v_mla_kv_norm,
         lax.dynamic_slice_in_dim(sv_take("kv_norm", 256), chip * 64, 64)),
    ]
    sw = jnp.concatenate([t[1].reshape(-1) for t in small] + [jnp.zeros((96,), F32)]).reshape(1, -1)
    sm = jnp.concatenate([t[2].reshape(-1) for t in small] + [jnp.zeros((96,), F32)]).reshape(1, -1)
    s_v = jnp.concatenate([t[3].reshape(-1) for t in small] + [jnp.ones((96,), F32)]).reshape(1, -1)
    sg = jnp.concatenate([t[4].reshape(-1) for t in small] + [jnp.zeros((96,), F32)]).reshape(1, -1)
    sd_, sm_, sv_ = adamw(sw, sg, sm, s_v, name="adamw_vectors")
    off = 0
    for name, wt, _, _, gvec in small:
        n = gvec.shape[0]
        shp = wt.shape
        out_g[name] = gvec.reshape(shp)
        out_d[name] = sd_[0, off:off + n].reshape(shp)
        out_m[name] = sm_[0, off:off + n].reshape(shp)
        out_v[name] = sv_[0, off:off + n].reshape(shp)
        off += n

    order = ("pool_norm", "pool_w_in", "pool_w_group", "pool_scale", "pool_w_out", "mla_norm", "mla_w_in",
             "mla_q_norm", "mla_w_q_b", "mla_kv_norm", "mla_w_kv_b", "mla_w_out", "final_norm")
    return (loss, grad_x.reshape(x.shape), *[out_g[n] for n in order], *[out_d[n] for n in order],
            *[out_m[n] for n in order], *[out_v[n] for n in order])
```

```python
import functools
from typing import Callable, NamedTuple

import jax
import jax.numpy as jnp
from jax import lax
from jax.experimental import pallas as pl
from jax.experimental.pallas import tpu as pltpu

F32 = jnp.float32
BF = jnp.bfloat16
MESH = pl.DeviceIdType.MESH

D_MODEL = 1024
POOL_WIDTH = 2048
POOL_WINDOWS = (2, 4, 8, 16)
POOL_GROUP = 512
HALO = 16
N_HEADS = 16
QK_NOPE = 128
QK_ROPE = 64
V_DIM = 128
HEAD_PAD = 256
Q_LORA = 384
KV_LORA = 256
MLA_WIDTH = 2048
ROPE_THETA = 10000.0
EPS = 1e-6
SCALE = (QK_NOPE + QK_ROPE) ** -0.5
SCALE_LOG2E = SCALE * 1.4426950408889634
NEG = -1e30

P_KV, P_KR, P_Q, P_Z = 0, 256, 384, 768
P_SMALL = 768
P_WIDTH = 2816

ADAM_LR = 0.001
ADAM_B1 = 0.9
ADAM_B2 = 0.999
ADAM_EPS = 1e-08
ADAM_WD = 0.01
ADAM_STEP = 10

NN = (((1,), (0,)), ((), ()))
NT = (((1,), (1,)), ((), ()))
TN = (((0,), (0,)), ((), ()))

POOL_ROWS = (1024, 256, 512)
MLA_ROWS = (688, 288, 256, 512)
PACK_PAD = 16
POOL_R = sum(POOL_ROWS)
MLA_R = sum(MLA_ROWS) + PACK_PAD
PACK_C = 1024
SV_OFF = dict(pool_norm=0, pool_scale=1024, final_norm=3072, mla_norm=4096, q_norm=5120, kv_norm=5504, loss=5760)
SV_ROWS, SV_COLS = 8, 768

VMEM_LIMIT = 56 * 1024 * 1024


def _params(n_axes, vmem=None):
    return pltpu.CompilerParams(dimension_semantics=("arbitrary",) * n_axes,
                                vmem_limit_bytes=VMEM_LIMIT if vmem is None else vmem)


def _sigmoid(z):
    return 1.0 / (1.0 + jnp.exp(-z))


class Exchange(NamedTuple):
    operands: tuple
    out_shapes: tuple
    aliases: dict
    n_sems: int
    start: Callable
    wait: Callable


HBM_SPEC = pl.BlockSpec(memory_space=pl.ANY)


def _exchange_scratch(ex):
    return [pltpu.SemaphoreType.DMA((ex.n_sems,)), pltpu.SemaphoreType.DMA((ex.n_sems,)), pltpu.SemaphoreType.DMA]


def run_exchange(ex, *, name):
    n_in, n_out = len(ex.operands), len(ex.out_shapes)

    def body(*refs):
        args = (refs[:n_in], refs[n_in:n_in + n_out]) + tuple(refs[n_in + n_out:])
        ex.start(*args)
        ex.wait(*args)

    return pl.pallas_call(
        body, name=name, out_shape=list(ex.out_shapes), in_specs=[HBM_SPEC] * n_in,
        out_specs=[HBM_SPEC] * n_out, scratch_shapes=_exchange_scratch(ex),
        input_output_aliases=dict(ex.aliases))(*ex.operands)


def _call(core, *, name, grid, in_specs, out_specs, out_shape, args, scratch=(), host=None):
    in_specs, out_specs, out_shape = list(in_specs), list(out_specs), list(out_shape)
    params = _params(len(grid))
    if host is None:
        return pl.pallas_call(core, name=name, grid=grid, in_specs=in_specs, out_specs=out_specs,
                              out_shape=out_shape, scratch_shapes=list(scratch), compiler_params=params)(*args)
    n_in, n_out = len(in_specs), len(out_specs)
    n_hin, n_hout = len(host.operands), len(host.out_shapes)

    def body(*refs):
        ins, refs = refs[:n_in], refs[n_in:]
        h_in, refs = refs[:n_hin], refs[n_hin:]
        outs, refs = refs[:n_out], refs[n_out:]
        h_out, refs = refs[:n_hout], refs[n_hout:]
        own_scratch, sems = refs[:-3], refs[-3:]
        ids = [pl.program_id(ax) for ax in range(len(grid))]
        first = functools.reduce(jnp.logical_and, [i == 0 for i in ids])
        last = functools.reduce(jnp.logical_and, [i == n - 1 for i, n in zip(ids, grid)])

        @pl.when(first)
        def _():
            host.start(h_in, h_out, *sems)

        core(*ins, *outs, *own_scratch)

        @pl.when(last)
        def _():
            host.wait(h_in, h_out, *sems)

    return pl.pallas_call(
        body, name=name, grid=grid, in_specs=in_specs + [HBM_SPEC] * n_hin,
        out_specs=out_specs + [HBM_SPEC] * n_hout, out_shape=out_shape + list(host.out_shapes),
        scratch_shapes=list(scratch) + _exchange_scratch(host),
        input_output_aliases={n_in + i: n_out + o for i, o in host.aliases.items()},
        compiler_params=params)(*args, *host.operands)


def _mm(a, b, *, dims, grid, a_spec, b_spec, o_spec, out_shape, out_dtype, acc_shape, name,
        add=None, add_spec=None, host=None):
    nk = grid[-1]
    kax = len(grid) - 1

    def body(*refs):
        if add is None:
            a_ref, b_ref, o_ref = refs[:3]
            add_ref = None
            rest = refs[3:]
        else:
            a_ref, b_ref, add_ref, o_ref = refs[:4]
            rest = refs[4:]
        part = lax.dot_general(a_ref[...].astype(BF), b_ref[...].astype(BF), dims,
                               preferred_element_type=F32)

        def finish(r):
            if add_ref is not None:
                r = r + add_ref[...]
            o_ref[...] = r.astype(o_ref.dtype)

        if nk == 1:
            finish(part)
        else:
            acc = rest[0]
            k = pl.program_id(kax)

            @pl.when(k == 0)
            def _():
                acc[...] = part

            @pl.when(k > 0)
            def _():
                acc[...] += part

            @pl.when(k == nk - 1)
            def _():
                finish(acc[...])

    in_specs = [a_spec, b_spec]
    args = [a, b]
    if add is not None:
        in_specs.append(add_spec)
        args.append(add)
    out = _call(body, name=name, grid=grid, in_specs=in_specs, out_specs=[o_spec],
                out_shape=[jax.ShapeDtypeStruct(out_shape, out_dtype)], args=args,
                scratch=[] if nk == 1 else [pltpu.VMEM(acc_shape, F32)], host=host)
    return out[0] if host is None else out


def _pick(n, t):
    t = min(n, t)
    assert n % t == 0, (n, t)
    return t


def mm_nn(a, b, *, name, out_dtype, add=None, tm=1024, tn=1024, tk=2048, host=None):
    m = a.shape[0]
    kk, n = b.shape
    tm, tn, tk = _pick(m, tm), _pick(n, tn), _pick(kk, tk)
    return _mm(a, b, dims=NN, grid=(m // tm, n // tn, kk // tk),
               a_spec=pl.BlockSpec((tm, tk), lambda i, j, k: (i, k)),
               b_spec=pl.BlockSpec((tk, tn), lambda i, j, k: (k, j)),
               o_spec=pl.BlockSpec((tm, tn), lambda i, j, k: (i, j)),
               add=add, add_spec=pl.BlockSpec((tm, tn), lambda i, j, k: (i, j)),
               out_shape=(m, n), out_dtype=out_dtype, acc_shape=(tm, tn), name=name, host=host)


def mm_nt(a, b, *, name, out_dtype, b_col=0, add=None, tm=1024, tn=1024, tk=2048, host=None):
    m, kk = a.shape
    n = b.shape[0]
    tm, tn, tk = _pick(m, tm), _pick(n, tn), _pick(kk, tk)
    assert b_col % tk == 0
    ko = b_col // tk
    return _mm(a, b, dims=NT, grid=(m // tm, n // tn, kk // tk),
               a_spec=pl.BlockSpec((tm, tk), lambda i, j, k: (i, k)),
               b_spec=pl.BlockSpec((tn, tk), lambda i, j, k: (j, ko + k)),
               o_spec=pl.BlockSpec((tm, tn), lambda i, j, k: (i, j)),
               add=add, add_spec=pl.BlockSpec((tm, tn), lambda i, j, k: (i, j)),
               out_shape=(m, n), out_dtype=out_dtype, acc_shape=(tm, tn), name=name, host=host)


def mm_tn(a, b, *, name, tm=1024, tn=1024, tk=2048, host=None, by_column_block=False):
    s, m = a.shape
    n = b.shape[1]
    tm, tn, tk = _pick(m, tm), _pick(n, tn), _pick(s, tk)
    if by_column_block:
        out_shape, o_spec = (n // tn, m, tn), pl.BlockSpec((None, tm, tn), lambda i, j, k: (j, i, 0))
    else:
        out_shape, o_spec = (m, n), pl.BlockSpec((tm, tn), lambda i, j, k: (i, j))
    return _mm(a, b, dims=TN, grid=(m // tm, n // tn, s // tk),
               a_spec=pl.BlockSpec((tk, tm), lambda i, j, k: (k, i)),
               b_spec=pl.BlockSpec((tk, tn), lambda i, j, k: (k, j)),
               o_spec=o_spec, out_shape=out_shape, out_dtype=F32, acc_shape=(tm, tn), name=name, host=host)


def gmm_nt(a, w, *, name, tm=1024, host=None):
    s = a.shape[0]
    g, kk, n = w.shape
    tm = _pick(s, tm)
    return _mm(a, w, dims=NT, grid=(s // tm, g, 1),
               a_spec=pl.BlockSpec((tm, n), lambda i, gi, k: (i, gi)),
               b_spec=pl.BlockSpec((None, kk, n), lambda i, gi, k: (gi, 0, 0)),
               o_spec=pl.BlockSpec((tm, kk), lambda i, gi, k: (i, gi)),
               out_shape=(s, g * kk), out_dtype=F32, acc_shape=(tm, kk), name=name, host=host)


def gmm_tn(a, b, g, *, name, tk=2048):
    s = a.shape[0]
    kk, n = a.shape[1] // g, b.shape[1] // g
    tk = _pick(s, tk)
    return _mm(a, b, dims=TN, grid=(g, s // tk),
               a_spec=pl.BlockSpec((tk, kk), lambda gi, k: (k, gi)),
               b_spec=pl.BlockSpec((tk, n), lambda gi, k: (k, gi)),
               o_spec=pl.BlockSpec((None, kk, n), lambda gi, k: (gi, 0, 0)),
               out_shape=(g, kk, n), out_dtype=F32, acc_shape=(kk, n), name=name)


def _rms(xv, gv):
    inv = lax.rsqrt(jnp.mean(xv * xv, axis=-1, keepdims=True) + EPS)
    return (xv * inv) * gv


def _rms_bwd(xv, gv, dh):
    inv = lax.rsqrt(jnp.mean(xv * xv, axis=-1, keepdims=True) + EPS)
    xhat = xv * inv
    dxhat = dh * gv
    dx = inv * (dxhat - xhat * jnp.mean(dxhat * xhat, axis=-1, keepdims=True))
    return dx, jnp.sum(dh * xhat, axis=0, keepdims=True)


def norm_fwd(x, g, *, name, t=1024):
    s, width = x.shape
    t = _pick(s, t)

    def body(x_ref, g_ref, o_ref):
        o_ref[...] = _rms(x_ref[...], g_ref[...]).astype(o_ref.dtype)

    row = pl.BlockSpec((t, width), lambda i: (i, 0))
    return pl.pallas_call(
        body, name=name, grid=(s // t,), in_specs=[row, pl.BlockSpec((1, width), lambda i: (0, 0))],
        out_specs=row, out_shape=jax.ShapeDtypeStruct((s, width), BF), compiler_params=_params(1))(x, g)


def _accumulate(ref, part):
    @pl.when(pl.program_id(0) == 0)
    def _():
        ref[...] = part

    @pl.when(pl.program_id(0) > 0)
    def _():
        ref[...] += part


def mm_nt_norm_bwd(a, b, other, x, g, res, *, name, b_col=0, tm=512, host=None):
    s, kk = a.shape
    d = b.shape[0]
    tm = _pick(s, tm)
    assert b_col % kk == 0
    pair = isinstance(other, tuple)

    def body(a_ref, b_ref, *refs):
        dh = lax.dot_general(a_ref[...].astype(BF), b_ref[...].astype(BF), NT, preferred_element_type=F32)
        if pair:
            a2_ref, b2_ref, x_ref, g_ref, res_ref, dx_ref, dg_ref = refs
            dh = dh + lax.dot_general(a2_ref[...].astype(BF), b2_ref[...].astype(BF), NT, preferred_element_type=F32)
        else:
            add_ref, x_ref, g_ref, res_ref, dx_ref, dg_ref = refs
            dh = dh + add_ref[...]
        dx, dg = _rms_bwd(x_ref[...], g_ref[...], dh)
        _accumulate(dg_ref, dg)
        dx_ref[...] = dx + res_ref[...]

    row = pl.BlockSpec((tm, d), lambda i: (i, 0))
    vec = pl.BlockSpec((1, d), lambda i: (0, 0))
    if pair:
        k2 = other[0].shape[1]
        other_specs = [pl.BlockSpec((tm, k2), lambda i: (i, 0)), pl.BlockSpec((d, k2), lambda i: (0, 0))]
        other_args = list(other)
    else:
        other_specs, other_args = [row], [other]
    return _call(
        body, name=name, grid=(s // tm,),
        in_specs=[pl.BlockSpec((tm, kk), lambda i: (i, 0)), pl.BlockSpec((d, kk), lambda i: (0, b_col // kk)),
                  *other_specs, row, vec, row],
        out_specs=[row, vec],
        out_shape=[jax.ShapeDtypeStruct((s, d), F32), jax.ShapeDtypeStruct((1, d), F32)],
        args=[a, b, *other_args, x, g, res], host=host)


def mm_nn_loss(a, b, add, gf, tgt, *, name, tm=512):
    s, kk = a.shape
    d = b.shape[1]
    tm = _pick(s, tm)

    def body(a_ref, b_ref, add_ref, g_ref, t_ref, dx_ref, dg_ref, loss_ref):
        xv = jnp.dot(a_ref[...].astype(BF), b_ref[...].astype(BF), preferred_element_type=F32) + add_ref[...]
        inv = lax.rsqrt(jnp.mean(xv * xv, axis=-1, keepdims=True) + EPS)
        xhat = xv * inv
        gv = g_ref[...]
        diff = xhat * gv - t_ref[...]
        row_err = jnp.mean(diff * diff, axis=-1, keepdims=True)
        _accumulate(loss_ref, jnp.broadcast_to(0.5 * jnp.sum(row_err, axis=0, keepdims=True), (1, 128)))
        dout = diff * (1.0 / d)
        _accumulate(dg_ref, jnp.sum(dout * xhat, axis=0, keepdims=True))
        dxhat = dout * gv
        dx_ref[...] = inv * (dxhat - xhat * jnp.mean(dxhat * xhat, axis=-1, keepdims=True))

    row = pl.BlockSpec((tm, d), lambda i: (i, 0))
    vec = pl.BlockSpec((1, d), lambda i: (0, 0))
    return _call(
        body, name=name, grid=(s // tm,),
        in_specs=[pl.BlockSpec((tm, kk), lambda i: (i, 0)), pl.BlockSpec((kk, d), lambda i: (0, 0)), row, vec, row],
        out_specs=[row, vec, pl.BlockSpec((1, 128), lambda i: (0, 0))],
        out_shape=[jax.ShapeDtypeStruct((s, d), F32), jax.ShapeDtypeStruct((1, d), F32),
                   jax.ShapeDtypeStruct((1, 128), F32)],
        args=[a, b, add, gf, tgt])


ROW_CHUNK = 56


def pool_prep(uz, *, name, t=256):
    s = uz.shape[0]
    t = _pick(s, t)
    hb = t // HALO

    lead = 2 * HALO
    live = t + lead - 8
    assert live % ROW_CHUNK == 0

    def body(u_ref, halo_ref, o_ref, buf_a, buf_b):
        i = pl.program_id(0)
        buf_a[pl.ds(lead, t), :] = u_ref[...]
        buf_a[pl.ds(0, HALO), :] = jnp.zeros((HALO, POOL_WIDTH), F32)
        buf_b[pl.ds(0, 8), :] = jnp.zeros((8, POOL_WIDTH), F32)

        @pl.when(i == 0)
        def _():
            buf_a[pl.ds(HALO, HALO), :] = jnp.zeros((HALO, POOL_WIDTH), F32)

        @pl.when(i > 0)
        def _():
            buf_a[pl.ds(HALO, HALO), :] = halo_ref[...]

        pos = i * t + lax.broadcasted_iota(jnp.int32, (t, POOL_GROUP), 0)
        for g, w in enumerate(POOL_WINDOWS):
            cols = pl.ds(g * POOL_GROUP, POOL_GROUP)
            src, dst, shift = buf_a, buf_b, 1
            while shift < w:
                for r0 in range(8, 8 + live, ROW_CHUNK):
                    dst[pl.ds(r0, ROW_CHUNK), cols] = (src[pl.ds(r0, ROW_CHUNK), cols]
                                                       + src[pl.ds(r0 - shift, ROW_CHUNK), cols])
                src, dst, shift = dst, src, 2 * shift
            cnt = jnp.minimum(pos + 1, w).astype(F32)
            o_ref[:, cols] = (src[pl.ds(lead, t), cols] / cnt - u_ref[:, cols]).astype(o_ref.dtype)

    return pl.pallas_call(
        body, name=name, grid=(s // t,),
        in_specs=[pl.BlockSpec((t, POOL_WIDTH), lambda i: (i, 0)),
                  pl.BlockSpec((HALO, POOL_WIDTH), lambda i: (jnp.maximum(i * hb - 1, 0), 0))],
        out_specs=pl.BlockSpec((t, POOL_WIDTH), lambda i: (i, 0)),
        out_shape=jax.ShapeDtypeStruct((s, POOL_WIDTH), BF),
        scratch_shapes=[pltpu.VMEM((t + lead, POOL_WIDTH), F32), pltpu.VMEM((t + lead, POOL_WIDTH), F32)],
        compiler_params=_params(1))(uz, uz)


def pool_prep_bwd(dpd, *, name, t=256):
    s = dpd.shape[0]
    t = _pick(s, t)
    hb = t // HALO
    n = s // t

    tail = 2 * HALO
    live = t + tail - 8
    assert live % ROW_CHUNK == 0

    def body(d_ref, halo_ref, o_ref, buf_a, buf_b):
        i = pl.program_id(0)
        buf_a[pl.ds(t + HALO, HALO), :] = jnp.zeros((HALO, POOL_WIDTH), F32)
        buf_b[pl.ds(live, 8), :] = jnp.zeros((8, POOL_WIDTH), F32)
        pos = i * t + lax.broadcasted_iota(jnp.int32, (t, POOL_GROUP), 0)
        for g, w in enumerate(POOL_WINDOWS):
            cols = pl.ds(g * POOL_GROUP, POOL_GROUP)
            cnt = jnp.minimum(pos + 1, w).astype(F32)
            buf_a[pl.ds(0, t), cols] = d_ref[:, cols] / cnt

            @pl.when(i < n - 1)
            def _():
                buf_a[pl.ds(t, HALO), cols] = halo_ref[:, cols] / float(w)

            @pl.when(i == n - 1)
            def _():
                buf_a[pl.ds(t, HALO), cols] = jnp.zeros((HALO, POOL_GROUP), F32)

        for g, w in enumerate(POOL_WINDOWS):
            cols = pl.ds(g * POOL_GROUP, POOL_GROUP)
            src, dst, shift = buf_a, buf_b, 1
            while shift < w:
                for r0 in range(0, live, ROW_CHUNK):
                    dst[pl.ds(r0, ROW_CHUNK), cols] = (src[pl.ds(r0, ROW_CHUNK), cols]
                                                       + src[pl.ds(r0 + shift, ROW_CHUNK), cols])
                src, dst, shift = dst, src, 2 * shift
            o_ref[:, cols] = (src[pl.ds(0, t), cols] - d_ref[:, cols]).astype(o_ref.dtype)

    return pl.pallas_call(
        body, name=name, grid=(n,),
        in_specs=[pl.BlockSpec((t, POOL_WIDTH), lambda i: (i, 0)),
                  pl.BlockSpec((HALO, POOL_WIDTH), lambda i: (jnp.minimum((i + 1) * hb, n * hb - 1), 0))],
        out_specs=pl.BlockSpec((t, POOL_WIDTH), lambda i: (i, 0)),
        out_shape=jax.ShapeDtypeStruct((s, POOL_WIDTH), BF),
        scratch_shapes=[pltpu.VMEM((t + tail, POOL_WIDTH), F32), pltpu.VMEM((t + tail, POOL_WIDTH), F32)],
        compiler_params=_params(1))(dpd, dpd)


CHUNK = 512


def _chunks(width, step=CHUNK):
    return [slice(c, c + step) for c in range(0, width, step)]


def pool_mix_gate(pd, wg, uz, scale, *, name, tm=1024):
    s = pd.shape[0]
    g = wg.shape[0]
    tm = _pick(s, tm)

    def body(a_ref, w_ref, z_ref, sc_ref, y_ref):
        mm = jnp.dot(a_ref[...], w_ref[...], preferred_element_type=F32)
        z = z_ref[...]
        y_ref[...] = ((mm * sc_ref[...]) * (z * _sigmoid(z))).astype(y_ref.dtype)

    blk = pl.BlockSpec((tm, POOL_GROUP), lambda i, gi: (i, gi))
    return pl.pallas_call(
        body, name=name, grid=(s // tm, g),
        in_specs=[blk, pl.BlockSpec((None, POOL_GROUP, POOL_GROUP), lambda i, gi: (gi, 0, 0)),
                  pl.BlockSpec((tm, POOL_GROUP), lambda i, gi: (i, g + gi)),
                  pl.BlockSpec((1, POOL_GROUP), lambda i, gi: (0, gi))],
        out_specs=blk, out_shape=jax.ShapeDtypeStruct((s, POOL_WIDTH), BF),
        compiler_params=_params(2))(pd, wg, uz, scale)


def pool_out_dx_gate(dx, w_out, pd, wg, uz, scale, *, name, tm=512, host=None):
    s, d = dx.shape
    tm = _pick(s, tm)
    assert CHUNK == POOL_GROUP

    def body(dx_ref, w_ref, pd_ref, wg_ref, z_ref, sc_ref, dmm_ref, dz_ref, dsc_ref):
        dxv = dx_ref[...].astype(BF)
        parts = []
        for g, c in enumerate(_chunks(POOL_WIDTH)):
            dyv = lax.dot_general(dxv, w_ref[c, :], NT, preferred_element_type=F32)
            z = z_ref[:, c]
            sig = _sigmoid(z)
            mmv = jnp.dot(pd_ref[:, c], wg_ref[g], preferred_element_type=F32)
            scv = sc_ref[:, c]
            dmixed = dyv * (z * sig)
            dmm_ref[:, c] = (dmixed * scv).astype(dmm_ref.dtype)
            dz_ref[:, c] = (dyv * (mmv * scv) * (sig * (1.0 + z * (1.0 - sig)))).astype(dz_ref.dtype)
            parts.append(jnp.sum(dmixed * mmv, axis=0, keepdims=True))

        @pl.when(pl.program_id(0) == 0)
        def _():
            for c, part in zip(_chunks(POOL_WIDTH), parts):
                dsc_ref[:, c] = part

        @pl.when(pl.program_id(0) > 0)
        def _():
            for c, part in zip(_chunks(POOL_WIDTH), parts):
                dsc_ref[:, c] += part

    blk = pl.BlockSpec((tm, POOL_WIDTH), lambda i: (i, 0))
    vec = pl.BlockSpec((1, POOL_WIDTH), lambda i: (0, 0))
    return _call(
        body, name=name, grid=(s // tm,),
        in_specs=[pl.BlockSpec((tm, d), lambda i: (i, 0)), pl.BlockSpec((POOL_WIDTH, d), lambda i: (0, 0)),
                  blk, pl.BlockSpec(wg.shape, lambda i: (0, 0, 0)),
                  pl.BlockSpec((tm, POOL_WIDTH), lambda i: (i, 1)), vec],
        out_specs=[blk, blk, vec],
        out_shape=[jax.ShapeDtypeStruct((s, POOL_WIDTH), BF), jax.ShapeDtypeStruct((s, POOL_WIDTH), BF),
                   jax.ShapeDtypeStruct((1, POOL_WIDTH), F32)],
        args=[dx, w_out, pd, wg, uz, scale], host=host)


def _rope(a, cc, sa, sb):
    return a * cc + pltpu.roll(a, 96, 1) * sa + pltpu.roll(a, 32, 1) * sb


def _unrope(d, cc, sa, sb):
    return d * cc + pltpu.roll(d * sa, 32, 1) + pltpu.roll(d * sb, 96, 1)


def q_proj_rope(qn, wq, cc, sa, sb, *, name, tm=1024, heads=8):
    s, kk = qn.shape
    tm = _pick(s, tm)
    tn = heads * HEAD_PAD

    def body(a_ref, b_ref, cc_ref, sa_ref, sb_ref, o_ref):
        q = jnp.dot(a_ref[...], b_ref[...], preferred_element_type=F32)
        for h in range(heads):
            nope = slice(h * HEAD_PAD, h * HEAD_PAD + QK_NOPE)
            rope = slice(h * HEAD_PAD + QK_NOPE, (h + 1) * HEAD_PAD)
            o_ref[:, nope] = q[:, nope].astype(o_ref.dtype)
            o_ref[:, rope] = _rope(q[:, rope], cc_ref[...], sa_ref[...], sb_ref[...]).astype(o_ref.dtype)

    tab = pl.BlockSpec((tm, 128), lambda i, j: (i, 0))
    return pl.pallas_call(
        body, name=name, grid=(s // tm, N_HEADS // heads),
        in_specs=[pl.BlockSpec((tm, kk), lambda i, j: (i, 0)), pl.BlockSpec((kk, tn), lambda i, j: (0, j)),
                  tab, tab, tab],
        out_specs=pl.BlockSpec((tm, tn), lambda i, j: (i, j)),
        out_shape=jax.ShapeDtypeStruct((s, N_HEADS * HEAD_PAD), BF), compiler_params=_params(2))(qn, wq, cc, sa, sb)


LAT_KV = slice(P_KV, P_KV + KV_LORA)
LAT_KR = slice(P_KR, P_KR + 128)
LAT_Q = slice(P_Q, P_Q + Q_LORA)


def latent_fwd(proj, g_q, g_kv, cc, sa, sb, *, name, t=1024):
    s = proj.shape[0]
    t = _pick(s, t)

    def body(p_ref, gq_ref, gkv_ref, cc_ref, sa_ref, sb_ref, qn_ref, kvn_ref, kr_ref):
        qn_ref[...] = _rms(p_ref[:, LAT_Q], gq_ref[...]).astype(qn_ref.dtype)
        kvn_ref[...] = _rms(p_ref[:, LAT_KV], gkv_ref[...]).astype(kvn_ref.dtype)
        kr_ref[...] = _rope(p_ref[:, LAT_KR], cc_ref[...], sa_ref[...], sb_ref[...]).astype(kr_ref.dtype)

    tab = pl.BlockSpec((t, 128), lambda i: (i, 0))
    return pl.pallas_call(
        body, name=name, grid=(s // t,),
        in_specs=[pl.BlockSpec((t, P_SMALL), lambda i: (i, 0)), pl.BlockSpec((1, Q_LORA), lambda i: (0, 0)),
                  pl.BlockSpec((1, KV_LORA), lambda i: (0, 0)), tab, tab, tab],
        out_specs=[pl.BlockSpec((t, Q_LORA), lambda i: (i, 0)), pl.BlockSpec((t, KV_LORA), lambda i: (i, 0)), tab],
        out_shape=[jax.ShapeDtypeStruct((s, Q_LORA), BF), jax.ShapeDtypeStruct((s, KV_LORA), BF),
                   jax.ShapeDtypeStruct((s, 128), BF)],
        compiler_params=_params(1))(proj, g_q, g_kv, cc, sa, sb)


def latent_bwd(proj, g_q, g_kv, dqn, dkvn, dkr, cc, sa, sb, *, name, t=1024):
    s = proj.shape[0]
    t = _pick(s, t)

    def body(p_ref, gq_ref, gkv_ref, dqn_ref, dkvn_ref, dkr_ref, cc_ref, sa_ref, sb_ref, d_ref, dgq_ref, dgkv_ref):
        dq, dgq = _rms_bwd(p_ref[:, LAT_Q], gq_ref[...], dqn_ref[...])
        dkv, dgkv = _rms_bwd(p_ref[:, LAT_KV], gkv_ref[...], dkvn_ref[...])
        d_ref[:, LAT_Q] = dq.astype(d_ref.dtype)
        d_ref[:, LAT_KV] = dkv.astype(d_ref.dtype)
        d_ref[:, LAT_KR] = _unrope(dkr_ref[...], cc_ref[...], sa_ref[...], sb_ref[...]).astype(d_ref.dtype)
        _accumulate(dgq_ref, dgq)
        _accumulate(dgkv_ref, dgkv)

    tab = pl.BlockSpec((t, 128), lambda i: (i, 0))
    small = pl.BlockSpec((t, P_SMALL), lambda i: (i, 0))
    gq = pl.BlockSpec((1, Q_LORA), lambda i: (0, 0))
    gkv = pl.BlockSpec((1, KV_LORA), lambda i: (0, 0))
    return pl.pallas_call(
        body, name=name, grid=(s // t,),
        in_specs=[small, gq, gkv, pl.BlockSpec((t, Q_LORA), lambda i: (i, 0)),
                  pl.BlockSpec((t, KV_LORA), lambda i: (i, 0)), tab, tab, tab, tab],
        out_specs=[small, gq, gkv],
        out_shape=[jax.ShapeDtypeStruct((s, P_SMALL), BF), jax.ShapeDtypeStruct((1, Q_LORA), F32),
                   jax.ShapeDtypeStruct((1, KV_LORA), F32)],
        compiler_params=_params(1))(proj, g_q, g_kv, dqn, dkvn, dkr, cc, sa, sb)


def mla_out_dx_gate(dx, w_out, o, proj, *, name, tq):
    s, d = dx.shape
    nq = s // tq
    t = min(512, tq)
    per = tq // t

    def body(dx_ref, w_ref, o_ref, p_ref, do_ref, dz_ref, dl_ref):
        dxv = dx_ref[...].astype(BF)
        lane = lax.broadcasted_iota(jnp.int32, (t, 128), 1)
        deltas = jnp.zeros((t, 128), F32)
        for c in _chunks(MLA_WIDTH):
            dy_c = lax.dot_general(dxv, w_ref[c, :], NT, preferred_element_type=F32)
            for h in range(c.start // V_DIM, c.stop // V_DIM):
                hc = slice(h * V_DIM, (h + 1) * V_DIM)
                z = p_ref[:, slice(P_Z + hc.start, P_Z + hc.stop)]
                sig = _sigmoid(z)
                dyv = dy_c[:, hc.start - c.start:hc.stop - c.start]
                ov = o_ref[:, hc]
                dov = dyv * (z * sig)
                do_ref[:, hc] = dov.astype(do_ref.dtype)
                dz_ref[:, hc] = (dyv * ov * (sig * (1.0 + z * (1.0 - sig)))).astype(dz_ref.dtype)
                deltas = jnp.where(lane == h, jnp.sum(dov * ov, axis=-1, keepdims=True), deltas)
        rows = deltas.T
        for h in range(N_HEADS):
            dl_ref[h] = jnp.broadcast_to(rows[h:h + 1, :], (8, t))

    blk = pl.BlockSpec((t, MLA_WIDTH), lambda i: (i, 0))
    return pl.pallas_call(
        body, name=name, grid=(s // t,),
        in_specs=[pl.BlockSpec((t, d), lambda i: (i, 0)), pl.BlockSpec((MLA_WIDTH, d), lambda i: (0, 0)),
                  blk, pl.BlockSpec((t, P_WIDTH), lambda i: (i, 0))],
        out_specs=[blk, blk, pl.BlockSpec((N_HEADS, None, 8, t), lambda i: (0, i // per, 0, i % per))],
        out_shape=[jax.ShapeDtypeStruct((s, MLA_WIDTH), BF), jax.ShapeDtypeStruct((s, MLA_WIDTH), BF),
                   jax.ShapeDtypeStruct((N_HEADS, nq, 8, tq), F32)],
        compiler_params=_params(1))(dx, w_out, o, proj)


FWD_GROUPS = (4, 3, 2, 1)
BWD_GROUPS = (2, 1)


def _for_groups(first, count, groups, fn):
    lead = groups[-1]
    for g in groups[:-1][::-1]:
        lead = jnp.where(count >= g, g, lead)
    for g in groups:
        @pl.when(lead == g)
        def _(g=g):
            fn(first, g, True)
    first = first + lead
    count = count - lead
    for g in groups:
        n = count // g

        def one(p, carry, g=g, first=first):
            fn(first + p * g, g, False)
            return carry

        lax.fori_loop(0, n, one, 0)
        first = first + n * g
        count = count - n * g


def attn_fwd(qr, kv, krr, proj, *, name, tq):
    s = qr.shape[0]
    nq = s // tq
    z_blk = P_Z // V_DIM

    def body(kn_ref, v_ref, kr_ref, q_ref, z_ref, o_ref, y_ref, lse_ref, acc_sc, m_sc):
        j = pl.program_id(1)

        @pl.when(j == 0)
        def _():
            acc_sc[...] = jnp.zeros((nq, 2 * V_DIM, tq), F32)
            m_sc[...] = jnp.full((nq, 8, tq), NEG, F32)

        k = jnp.concatenate([kn_ref[...], kr_ref[...]], axis=1)
        vxt = jnp.concatenate([v_ref[...].astype(F32).T.astype(BF), jnp.ones((V_DIM, tq), BF)], axis=0)

        def update(i, n_tiles, masked):
            rows = pl.ds(pl.multiple_of(i * tq, tq), n_tiles * tq)
            st = lax.dot_general(k, q_ref[rows, :], NT, preferred_element_type=F32) * SCALE_LOG2E
            if masked:
                krow = lax.broadcasted_iota(jnp.int32, (tq, n_tiles * tq), 0)
                qcol = lax.broadcasted_iota(jnp.int32, (tq, n_tiles * tq), 1)
                st = jnp.where(qcol >= krow, st, NEG)
            m_prev = jnp.concatenate([m_sc[i + n, pl.ds(0, 1), :] for n in range(n_tiles)], axis=1)
            m_new = jnp.maximum(m_prev, jnp.max(st, axis=0, keepdims=True))
            alpha = jnp.exp2(m_prev - m_new)
            pt = jnp.exp2(st - m_new).astype(BF)
            pv_t = jnp.dot(vxt, pt, preferred_element_type=F32)
            for n in range(n_tiles):
                cols = slice(n * tq, (n + 1) * tq)
                acc_sc[i + n] = alpha[:, cols] * acc_sc[i + n] + pv_t[:, cols]
                m_sc[i + n, pl.ds(0, 1), :] = m_new[:, cols]

        _for_groups(j, nq - j, FWD_GROUPS, update)
        l = acc_sc[j, V_DIM:, :]
        o = (acc_sc[j, :V_DIM, :] / l).T
        o_ref[...] = o
        z = z_ref[...]
        y_ref[...] = (o * (z * _sigmoid(z))).astype(y_ref.dtype)
        lse_ref[...] = m_sc[j, pl.ds(0, 1), :] + jnp.log2(l[:8, :])

    tile = pl.BlockSpec((tq, V_DIM), lambda h, j: (j, h))
    return pl.pallas_call(
        body, name=name, grid=(N_HEADS, nq),
        in_specs=[pl.BlockSpec((tq, QK_NOPE), lambda h, j: (j, 2 * h)),
                  pl.BlockSpec((tq, V_DIM), lambda h, j: (j, 2 * h + 1)),
                  pl.BlockSpec((tq, 128), lambda h, j: (j, 0)),
                  pl.BlockSpec((s, HEAD_PAD), lambda h, j: (0, h)),
                  pl.BlockSpec((tq, V_DIM), lambda h, j: (j, z_blk + h))],
        out_specs=[tile, tile, pl.BlockSpec((None, None, 8, tq), lambda h, j: (h, j, 0, 0))],
        out_shape=[jax.ShapeDtypeStruct((s, N_HEADS * V_DIM), F32),
                   jax.ShapeDtypeStruct((s, N_HEADS * V_DIM), BF),
                   jax.ShapeDtypeStruct((N_HEADS, nq, 8, tq), F32)],
        scratch_shapes=[pltpu.VMEM((nq, 2 * V_DIM, tq), F32), pltpu.VMEM((nq, 8, tq), F32)],
        compiler_params=_params(2))(kv, kv, krr, qr, proj)


def attn_bwd(qr, kv, krr, do, lse, delta, cc, sa, sb, *, name, tq):
    s = qr.shape[0]
    nq = s // tq

    def body(kn_ref, v_ref, kr_ref, q_ref, do_ref, lse_ref, dl_ref, cc_ref, sa_ref, sb_ref,
             dkv_ref, dkr_ref, dq_ref, dq_sc, dk_sc, dv_sc):
        h = pl.program_id(0)
        j = pl.program_id(1)

        @pl.when(j == 0)
        def _():
            dq_sc[...] = jnp.zeros((s, HEAD_PAD), F32)

        dk_sc[...] = jnp.zeros((tq, HEAD_PAD), F32)
        dv_sc[...] = jnp.zeros((tq, V_DIM), F32)
        k = jnp.concatenate([kn_ref[...], kr_ref[...]], axis=1)
        v = v_ref[...]

        def step(i, n_tiles, masked):
            r0 = pl.multiple_of(i * tq, tq)
            rows = pl.ds(r0, n_tiles * tq)
            q = q_ref[rows, :]
            dov = do_ref[rows, :]
            lse_row = jnp.concatenate([lse_ref[i + n, pl.ds(0, 1), :] for n in range(n_tiles)], axis=1)
            dl_row = jnp.concatenate([dl_ref[i + n, pl.ds(0, 1), :] for n in range(n_tiles)], axis=1)
            st = lax.dot_general(k, q, NT, preferred_element_type=F32) * SCALE_LOG2E
            if masked:
                krow = lax.broadcasted_iota(jnp.int32, (tq, n_tiles * tq), 0)
                qcol = lax.broadcasted_iota(jnp.int32, (tq, n_tiles * tq), 1)
                st = jnp.where(qcol >= krow, st, NEG)
            pt = jnp.exp2(st - lse_row)
            dpt = lax.dot_general(v, dov, NT, preferred_element_type=F32)
            dst = (pt * (dpt - dl_row)).astype(BF)
            dv_sc[...] += jnp.dot(pt.astype(BF), dov, preferred_element_type=F32)
            dk_sc[...] += jnp.dot(dst, q, preferred_element_type=F32)
            dq_sc[rows, :] += lax.dot_general(dst, k, TN, preferred_element_type=F32)

        _for_groups(j, nq - j, BWD_GROUPS, step)
        dkv_ref[:, :QK_NOPE] = (dk_sc[:, :QK_NOPE] * SCALE).astype(dkv_ref.dtype)
        dkv_ref[:, QK_NOPE:] = dv_sc[...].astype(dkv_ref.dtype)
        mine = pl.ds(pl.multiple_of(j * tq, tq), tq)
        dkr = dk_sc[:, QK_NOPE:] * SCALE

        @pl.when(h == 0)
        def _():
            dkr_ref[mine, :] = dkr

        @pl.when(h > 0)
        def _():
            dkr_ref[mine, :] += dkr

        dq_ref[:, :QK_NOPE] = (dq_sc[mine, :QK_NOPE] * SCALE).astype(dq_ref.dtype)
        dq_ref[:, QK_NOPE:] = _unrope(dq_sc[mine, QK_NOPE:] * SCALE, cc_ref[...], sa_ref[...],
                                      sb_ref[...]).astype(dq_ref.dtype)

    rows = pl.BlockSpec((None, nq, 8, tq), lambda h, j: (h, 0, 0, 0))
    tab = pl.BlockSpec((tq, 128), lambda h, j: (j, 0))
    return pl.pallas_call(
        body, name=name, grid=(N_HEADS, nq),
        in_specs=[pl.BlockSpec((tq, QK_NOPE), lambda h, j: (j, 2 * h)),
                  pl.BlockSpec((tq, V_DIM), lambda h, j: (j, 2 * h + 1)), tab,
                  pl.BlockSpec((s, HEAD_PAD), lambda h, j: (0, h)),
                  pl.BlockSpec((s, V_DIM), lambda h, j: (0, h)), rows, rows, tab, tab, tab],
        out_specs=[pl.BlockSpec((tq, 256), lambda h, j: (j, h)),
                   pl.BlockSpec((s, 128), lambda h, j: (0, 0)),
                   pl.BlockSpec((tq, HEAD_PAD), lambda h, j: (j, h))],
        out_shape=[jax.ShapeDtypeStruct((s, N_HEADS * 256), BF),
                   jax.ShapeDtypeStruct((s, 128), F32),
                   jax.ShapeDtypeStruct((s, N_HEADS * HEAD_PAD), BF)],
        scratch_shapes=[pltpu.VMEM((s, HEAD_PAD), F32), pltpu.VMEM((tq, HEAD_PAD), F32),
                        pltpu.VMEM((tq, V_DIM), F32)],
        compiler_params=_params(2))(kv, kv, krr, qr, do, lse, delta, cc, sa, sb)


def adamw(w, g, m, v, *, name, t=512):
    r, c = w.shape
    t = r if r % t else t
    c1 = 1.0 - ADAM_B1 ** ADAM_STEP
    c2 = 1.0 - ADAM_B2 ** ADAM_STEP

    def body(w_ref, g_ref, m_ref, v_ref, d_ref, nm_ref, nv_ref):
        gv = g_ref[...]
        nm = ADAM_B1 * m_ref[...] + (1.0 - ADAM_B1) * gv
        nv = ADAM_B2 * v_ref[...] + (1.0 - ADAM_B2) * (gv * gv)
        nm_ref[...] = nm
        nv_ref[...] = nv
        d_ref[...] = -ADAM_LR * ((nm / c1) / (jnp.sqrt(nv / c2) + ADAM_EPS) + ADAM_WD * w_ref[...])

    blk = pl.BlockSpec((t, c), lambda i: (i, 0))
    return pl.pallas_call(
        body, name=name, grid=(r // t,), in_specs=[blk] * 4, out_specs=[blk] * 3,
        out_shape=[jax.ShapeDtypeStruct((r, c), F32)] * 3, compiler_params=_params(1))(w, g, m, v)


def sum_devices(parts, *, name):
    def body(p_ref, o_ref):
        acc = p_ref[pl.ds(0, SV_ROWS), :]
        for d in range(1, 8):
            acc = acc + p_ref[pl.ds(d * SV_ROWS, SV_ROWS), :]
        o_ref[...] = acc

    return pl.pallas_call(body, name=name, out_shape=jax.ShapeDtypeStruct((SV_ROWS, SV_COLS), F32))(parts)


def add_halves(g, rb, c_idx, *, name, rows):
    nq, r2, cc = rb.shape
    nb = r2 // rows

    def body(c_ref, g_ref, r_ref, o_ref):
        o_ref[...] = (g_ref[...] + r_ref[...]).astype(o_ref.dtype)

    grid_spec = pltpu.PrefetchScalarGridSpec(
        num_scalar_prefetch=1, grid=(nq, nb),
        in_specs=[pl.BlockSpec((None, rows, cc), lambda q, i, c: (q, c[0] * nb + i, 0)),
                  pl.BlockSpec((None, rows, cc), lambda q, i, c: (q, i, 0))],
        out_specs=pl.BlockSpec((None, rows, cc), lambda q, i, c: (q, i, 0)))
    return pl.pallas_call(body, name=name, grid_spec=grid_spec,
                          out_shape=jax.ShapeDtypeStruct((nq, r2, cc), BF),
                          compiler_params=_params(2))(c_idx, g, rb)


def sum_chips(rc, c_idx, *, name, rows):
    nq, r2, cc = rc.shape
    nb = r2 // rows

    def body(c_ref, r_ref, o_ref):
        parts = [r_ref[q].astype(F32) for q in range(4)]
        o_ref[...] = ((parts[0] + parts[1]) + parts[2]) + parts[3]

    grid_spec = pltpu.PrefetchScalarGridSpec(
        num_scalar_prefetch=1, grid=(nb,),
        in_specs=[pl.BlockSpec((nq, rows, cc), lambda i, c: (0, i, 0))],
        out_specs=pl.BlockSpec((rows, cc), lambda i, c: (c[0] * nb + i, 0)))
    return pl.pallas_call(body, name=name, grid_spec=grid_spec,
                          out_shape=jax.ShapeDtypeStruct((2 * r2, cc), F32),
                          compiler_params=_params(1))(c_idx, rc)


def _place():
    return lax.axis_index("x"), lax.axis_index("y"), lax.axis_index("c")


def all_gather8(xs, *, name, own_half):
    m = xs.shape[0] // 2 if own_half else xs.shape[0]
    n = xs.shape[1]

    def body(x_ref, out_ref, send_sems, recv_sems, local_sem):
        x, y, c = _place()
        me, sibling = (x, y, c), (x, y, 1 - c)
        chips = [(1 - x, y), (x, 1 - y), (1 - x, 1 - y)]
        src_own = x_ref.at[pl.ds(c * m, m), :] if own_half else x_ref

        def rows(px, py, pc):
            return out_ref.at[pl.ds((4 * px + 2 * py + pc) * m, m), :]

        def copy(k, block, to, src=None):
            return pltpu.make_async_remote_copy(
                src_ref=rows(*block) if src is None else src, dst_ref=rows(*block),
                send_sem=send_sems.at[k], recv_sem=recv_sems.at[k], device_id=to, device_id_type=MESH)

        mine = pltpu.make_async_copy(src_own, rows(*me), local_sem)
        mine.start()
        first = [copy(0, me, sibling, src=src_own)]
        first += [copy(1 + j, me, (*chip, c), src=src_own) for j, chip in enumerate(chips)]
        for cp in first:
            cp.start()
        passed = [copy(4 + j, (*chip, c), sibling) for j, chip in enumerate(chips)]
        for j, chip in enumerate(chips):
            copy(1 + j, (*chip, c), me).wait_recv()
            passed[j].start()
        copy(0, sibling, me).wait_recv()
        for j, chip in enumerate(chips):
            copy(4 + j, (*chip, 1 - c), me).wait_recv()
        for cp in first + passed:
            cp.wait_send()
        mine.wait()

    return pl.pallas_call(
        body, name=name, out_shape=jax.ShapeDtypeStruct((8 * m, n), xs.dtype),
        in_specs=[pl.BlockSpec(memory_space=pl.ANY)], out_specs=pl.BlockSpec(memory_space=pl.ANY),
        scratch_shapes=[pltpu.SemaphoreType.DMA((7,)), pltpu.SemaphoreType.DMA((7,)), pltpu.SemaphoreType.DMA],
    )(xs)


def _other_chips():
    x, y, c = _place()
    return [(1 - x, y), (x, 1 - y), (1 - x, 1 - y)]


def _remote(src, dst, send_sems, recv_sems, k, to):
    return pltpu.make_async_remote_copy(src_ref=src, dst_ref=dst, send_sem=send_sems.at[k], recv_sem=recv_sems.at[k],
                                        device_id=to, device_id_type=MESH)


def gather_ici(xs):
    r, cc = xs.shape
    m = r // 2

    def copies(ins, outs, ss, rs, landing):
        x, y, c = _place()
        half = pl.ds(c * m, m)
        return [_remote(ins[0].at[half, :], outs[0].at[(2 * cx + cy) if landing else (2 * x + y), half, :],
                        ss, rs, j, (cx, cy, c)) for j, (cx, cy) in enumerate(_other_chips())]

    def start(ins, outs, ss, rs, ls):
        for cp in copies(ins, outs, ss, rs, False):
            cp.start()

    def wait(ins, outs, ss, rs, ls):
        for cp in copies(ins, outs, ss, rs, True):
            cp.wait_recv()
        for cp in copies(ins, outs, ss, rs, False):
            cp.wait_send()

    return Exchange((xs,), (jax.ShapeDtypeStruct((4, r, cc), xs.dtype),), {}, 3, start, wait)


def gather_forward(buf):
    m = buf.shape[1] // 2

    def copies(outs, ss, rs, landing):
        x, y, c = _place()
        half = pl.ds(((1 - c) if landing else c) * m, m)
        return [_remote(outs[0].at[2 * cx + cy, half, :], outs[0].at[2 * cx + cy, half, :], ss, rs, j, (x, y, 1 - c))
                for j, (cx, cy) in enumerate(_other_chips())]

    def start(ins, outs, ss, rs, ls):
        for cp in copies(outs, ss, rs, False):
            cp.start()

    def wait(ins, outs, ss, rs, ls):
        for cp in copies(outs, ss, rs, True):
            cp.wait_recv()
        for cp in copies(outs, ss, rs, False):
            cp.wait_send()

    return Exchange((buf,), (jax.ShapeDtypeStruct(buf.shape, buf.dtype),), {0: 0}, 3, start, wait)


def swap_halves(g):
    nq, r, cc = g.shape
    r2 = r // 2

    def copy(ins, outs, ss, rs):
        x, y, c = _place()
        return _remote(ins[0].at[:, pl.ds((1 - c) * r2, r2), :], outs[0], ss, rs, 0, (x, y, 1 - c))

    def start(ins, outs, ss, rs, ls):
        copy(ins, outs, ss, rs).start()

    def wait(ins, outs, ss, rs, ls):
        copy(ins, outs, ss, rs).wait()

    return Exchange((g,), (jax.ShapeDtypeStruct((nq, r2, cc), g.dtype),), {}, 1, start, wait)


def exchange_chips(p):
    def own(ins, outs, ls):
        x, y, c = _place()
        return pltpu.make_async_copy(ins[0].at[2 * x + y], outs[0].at[2 * x + y], ls)

    def copies(ins, outs, ss, rs, landing):
        x, y, c = _place()
        return [_remote(ins[0].at[2 * cx + cy], outs[0].at[(2 * cx + cy) if landing else (2 * x + y)],
                        ss, rs, j, (cx, cy, c)) for j, (cx, cy) in enumerate(_other_chips())]

    def start(ins, outs, ss, rs, ls):
        own(ins, outs, ls).start()
        for cp in copies(ins, outs, ss, rs, False):
            cp.start()

    def wait(ins, outs, ss, rs, ls):
        for cp in copies(ins, outs, ss, rs, True):
            cp.wait_recv()
        for cp in copies(ins, outs, ss, rs, False):
            cp.wait_send()
        own(ins, outs, ls).wait()

    return Exchange((p,), (jax.ShapeDtypeStruct(p.shape, p.dtype),), {}, 3, start, wait)


def join_halves(tot):
    r2 = tot.shape[0] // 2

    def copy(outs, ss, rs, landing):
        x, y, c = _place()
        half = outs[0].at[pl.ds(((1 - c) if landing else c) * r2, r2), :]
        return _remote(half, half, ss, rs, 0, (x, y, 1 - c))

    def start(ins, outs, ss, rs, ls):
        copy(outs, ss, rs, False).start()

    def wait(ins, outs, ss, rs, ls):
        copy(outs, ss, rs, True).wait_recv()
        copy(outs, ss, rs, False).wait_send()

    return Exchange((tot,), (jax.ShapeDtypeStruct(tot.shape, tot.dtype),), {0: 0}, 1, start, wait)


def _pack_shard(blocks, small_vec=None):
    parts = [w.reshape(-1, PACK_C).astype(BF) for w in blocks]
    if small_vec is not None:
        srow = lax.bitcast_convert_type(small_vec, BF).reshape(1, PACK_C)
        parts.append(jnp.pad(srow, ((0, PACK_PAD - 1), (0, 0))))
    return jnp.concatenate(parts, axis=0)


def _split_rows(a, rows, axis):
    out, off = [], 0
    for n in rows:
        out.append(lax.slice_in_dim(a, off, off + n, axis=axis))
        off += n
    return out


def _unpack_pool(gw):
    p_in, p_grp, p_out = _split_rows(gw, POOL_ROWS, 1)
    return dict(
        pool_w_in=p_in.reshape(4, D_MODEL, 1024).transpose(1, 0, 2).reshape(D_MODEL, 2 * POOL_WIDTH),
        pool_w_group=p_grp.reshape(4, 4, 128, POOL_GROUP).transpose(1, 0, 2, 3).reshape(4, POOL_GROUP, POOL_GROUP),
        pool_w_out=p_out.reshape(POOL_WIDTH, D_MODEL))


def _unpack_mla(gw):
    m_in, m_qb, m_kvb, m_out, small = _split_rows(gw, MLA_ROWS + (PACK_PAD,), 1)
    w = {}
    win = m_in.reshape(4, D_MODEL, 688).transpose(1, 0, 2).reshape(D_MODEL, 2752)
    w["mla_w_in"] = jnp.concatenate(
        [win[:, 384:640], win[:, 640:704], jnp.zeros((D_MODEL, 64), BF), win[:, 0:384], win[:, 704:]], axis=1)
    wq = m_qb.reshape(4, Q_LORA, 768).transpose(1, 0, 2).reshape(Q_LORA, N_HEADS, QK_NOPE + QK_ROPE)
    w["mla_w_q_b"] = jnp.pad(wq, ((0, 0), (0, 0), (0, HEAD_PAD - QK_NOPE - QK_ROPE))).reshape(Q_LORA, N_HEADS * HEAD_PAD)
    w["mla_w_kv_b"] = m_kvb.reshape(4, KV_LORA, 1024).transpose(1, 0, 2).reshape(KV_LORA, 4096)
    w["mla_w_out"] = m_out.reshape(MLA_WIDTH, D_MODEL)
    small = lax.bitcast_convert_type(small[:, 0, :].reshape(4, 512, 2), F32)
    w["mla_norm"] = small[:, :256].reshape(1, D_MODEL)
    w["mla_q_norm"] = small[:, 256:352].reshape(1, Q_LORA)
    w["mla_kv_norm"] = small[:, 352:416].reshape(1, KV_LORA)
    return w


def _pack_pool_grads(g):
    return jnp.concatenate([
        g["pool_w_in"],
        g["pool_w_group"].reshape(4, 4, 128, POOL_GROUP).transpose(1, 0, 2, 3).reshape(4, 256, PACK_C),
        g["pool_w_out"].reshape(4, 512, PACK_C)], axis=1)


def _pack_mla_grads(g):
    return jnp.concatenate([
        g["mla_w_in"].reshape(D_MODEL, 4, 688).transpose(1, 0, 2).reshape(4, 688, PACK_C),
        g["mla_w_q_b"].reshape(Q_LORA, 4, 768).transpose(1, 0, 2).reshape(4, 288, PACK_C),
        g["mla_w_kv_b"],
        g["mla_w_out"].reshape(4, 512, PACK_C),
        jnp.zeros((4, PACK_PAD, PACK_C), F32)], axis=1)


def kernel(x, positions, pool_norm, pool_w_in, pool_w_group, pool_scale, pool_w_out, mla_norm, mla_w_in, mla_q_norm, mla_w_q_b, mla_kv_norm, mla_w_kv_b, mla_w_out, final_norm, loss_target, m_pool_norm, m_pool_w_in, m_pool_w_group, m_pool_scale, m_pool_w_out, m_mla_norm, m_mla_w_in, m_mla_q_norm, m_mla_w_q_b, m_mla_kv_norm, m_mla_w_kv_b, m_mla_w_out, m_final_norm, v_pool_norm, v_pool_w_in, v_pool_w_group, v_pool_scale, v_pool_w_out, v_mla_norm, v_mla_w_in, v_mla_q_norm, v_mla_w_q_b, v_mla_kv_norm, v_mla_w_kv_b, v_mla_w_out, v_final_norm):
    s = x.shape[1]
    tq = min(1024, s)
    x0 = x.reshape(s, D_MODEL)
    tgt = loss_target.reshape(s, D_MODEL)
    cx, cy, cc_idx = _place()
    chip = 2 * cx + cy

    big_names = ("pool_w_in", "pool_w_group", "pool_w_out", "mla_w_in", "mla_w_q_b", "mla_w_kv_b", "mla_w_out")
    big_w = dict(zip(big_names, (pool_w_in, pool_w_group, pool_w_out, mla_w_in, mla_w_q_b, mla_w_kv_b, mla_w_out)))
    big_m = dict(zip(big_names, (m_pool_w_in, m_pool_w_group, m_pool_w_out, m_mla_w_in, m_mla_w_q_b, m_mla_w_kv_b, m_mla_w_out)))
    big_v = dict(zip(big_names, (v_pool_w_in, v_pool_w_group, v_pool_w_out, v_mla_w_in, v_mla_w_q_b, v_mla_w_kv_b, v_mla_w_out)))

    small_vec = jnp.concatenate([mla_norm.reshape(-1), mla_q_norm.reshape(-1), mla_kv_norm.reshape(-1),
                                 jnp.zeros((96,), F32)])
    pool_packed = _pack_shard([big_w[n] for n in big_names[:3]])
    mla_packed = _pack_shard([big_w[n] for n in big_names[3:]], small_vec)
    w = _unpack_pool(all_gather8(pool_packed, name="gather_pool_weights", own_half=True).reshape(4, POOL_R, PACK_C))
    g_pool = pool_norm.reshape(1, D_MODEL)
    g_final = final_norm.reshape(1, D_MODEL)
    sc_pool = pool_scale.reshape(1, POOL_WIDTH)

    inv_freq = 1.0 / (ROPE_THETA ** (jnp.arange(0, QK_ROPE, 2, dtype=F32) / QK_ROPE))
    ang = positions.reshape(s).astype(F32)[:, None] * inv_freq
    cos, sin = jnp.cos(ang), jnp.sin(ang)
    z32, z64, z96 = (jnp.zeros((s, n), F32) for n in (32, 64, 96))
    t_cc = jnp.concatenate([cos, cos, z64], axis=1)
    t_sa = jnp.concatenate([-sin, z96], axis=1)
    t_sb = jnp.concatenate([z32, sin, z64], axis=1)

    h0 = norm_fwd(x0, g_pool, name="pool_norm_fwd")
    uz, mla_land = mm_nn(h0, w["pool_w_in"], name="pool_in_proj", out_dtype=F32, host=gather_ici(mla_packed))
    pd = pool_prep(uz, name="pool_window")
    y1 = pool_mix_gate(pd, w["pool_w_group"], uz, sc_pool, name="pool_group_mix")
    x1, mla_land = mm_nn(y1, w["pool_w_out"], name="pool_out_proj", out_dtype=F32, add=x0,
                         host=gather_forward(mla_land))
    w.update(_unpack_mla(lax.dynamic_update_slice_in_dim(mla_land, mla_packed[None], chip, axis=0)))

    h1 = norm_fwd(x1, w["mla_norm"], name="mla_norm_fwd")
    proj = mm_nn(h1, w["mla_w_in"], name="mla_in_proj", out_dtype=F32, tn=P_WIDTH // 2)
    qn, kvn, krr = latent_fwd(proj, w["mla_q_norm"], w["mla_kv_norm"], t_cc, t_sa, t_sb, name="mla_latent_fwd")
    qr = q_proj_rope(qn, w["mla_w_q_b"], t_cc, t_sa, t_sb, name="mla_q_proj")
    kv = mm_nn(kvn, w["mla_w_kv_b"], name="mla_kv_proj", out_dtype=BF, tn=2048)
    o, y2, lse = attn_fwd(qr, kv, krr, proj, name="mla_attn_fwd", tq=tq)
    dx2, d_final, loss_part = mm_nn_loss(y2, w["mla_w_out"], x1, g_final, tgt, name="mla_out_proj_loss")

    grads = {}
    grads["mla_w_out"] = mm_tn(y2, dx2, name="mla_out_proj_dw")
    do, dz2, delta = mla_out_dx_gate(dx2, w["mla_w_out"], o, proj, name="mla_out_proj_dx", tq=tq)
    dkv, dkr, dq_pre = attn_bwd(qr, kv, krr, do, lse, delta, t_cc, t_sa, t_sb, name="mla_attn_bwd", tq=tq)
    dqn = mm_nt(dq_pre, w["mla_w_q_b"], name="mla_q_proj_dx", out_dtype=F32, tn=Q_LORA, tk=4096)
    g_qb = mm_tn(qn, dq_pre, name="mla_q_proj_dw", tm=Q_LORA, tn=2048)
    dkvn = mm_nt(dkv, w["mla_w_kv_b"], name="mla_kv_proj_dx", out_dtype=F32, tn=KV_LORA, tk=4096)
    grads["mla_w_kv_b"] = mm_tn(kvn, dkv, name="mla_kv_proj_dw", tm=KV_LORA, tk=4096, by_column_block=True)
    dsmall, d_qnorm, d_kvnorm = latent_bwd(proj, w["mla_q_norm"], w["mla_kv_norm"], dqn, dkvn, dkr,
                                           t_cc, t_sa, t_sb, name="mla_latent_bwd")
    dx1, d_mnorm = mm_nt_norm_bwd(dz2, w["mla_w_in"][:, P_Z:], (dsmall, w["mla_w_in"]), x1, w["mla_norm"], dx2,
                                  name="mla_in_proj_dx")
    g_in_a = mm_tn(h1, dsmall, name="mla_in_proj_dw_a", tn=P_SMALL, tk=4096)
    g_in_b = mm_tn(h1, dz2, name="mla_in_proj_dw_b", tk=4096)

    g_in = jnp.concatenate([g_in_a, g_in_b], axis=1)
    grads["mla_w_in"] = jnp.concatenate([g_in[:, P_Q:P_Z], g_in[:, P_KV:P_KV + KV_LORA],
                                         g_in[:, P_KR:P_KR + QK_ROPE], g_in[:, P_Z:]], axis=1)
    grads["mla_w_q_b"] = g_qb.reshape(Q_LORA, N_HEADS, HEAD_PAD)[:, :, :QK_NOPE + QK_ROPE].reshape(Q_LORA, -1)
    core_idx = cc_idx.reshape(1).astype(jnp.int32)
    gp_mla = _pack_mla_grads(grads)

    grads["pool_w_out"], sib = mm_tn(y1, dx1, name="pool_out_proj_dw", host=swap_halves(gp_mla))
    pre = add_halves(gp_mla, sib, core_idx, name="mla_grad_add_halves", rows=MLA_R // 2)
    dmm, dz1, d_scale, got = pool_out_dx_gate(dx1, w["pool_w_out"], pd, w["pool_w_group"], uz, sc_pool,
                                              name="pool_out_proj_dx", host=exchange_chips(pre))
    tot = sum_chips(got, core_idx, name="mla_grad_sum_chips", rows=MLA_R // 2)
    dpd, red_mla = gmm_nt(dmm, w["pool_w_group"], name="pool_group_mix_dx", host=join_halves(tot))
    grads["pool_w_group"] = gmm_tn(pd, dmm, 4, name="pool_group_mix_dw", tk=4096)
    du = pool_prep_bwd(dpd, name="pool_window_bwd")
    g_pin_u = mm_tn(h0, du, name="pool_in_proj_dw_u", tk=4096, by_column_block=True)
    g_pin_z = mm_tn(h0, dz1, name="pool_in_proj_dw_z", tk=4096, by_column_block=True)
    grads["pool_w_in"] = jnp.concatenate([g_pin_u, g_pin_z], axis=0)

    gp_pool = _pack_pool_grads(grads)
    dh0, sib = mm_nt(du, w["pool_w_in"], name="pool_in_proj_dx_u", out_dtype=F32, host=swap_halves(gp_pool))
    pre = add_halves(gp_pool, sib, core_idx, name="pool_grad_add_halves", rows=POOL_R // 2)
    grad_x, d_pnorm, got = mm_nt_norm_bwd(dz1, w["pool_w_in"], dh0, x0, g_pool, dx1, name="pool_in_proj_dx_z",
                                          b_col=POOL_WIDTH, host=exchange_chips(pre))
    tot = sum_chips(got, core_idx, name="pool_grad_sum_chips", rows=POOL_R // 2)
    red_pool = run_exchange(join_halves(tot), name="pool_grad_join_halves")[0]
    red_parts = _split_rows(red_pool, POOL_ROWS, 0) + _split_rows(red_mla, MLA_ROWS, 0)

    sv = jnp.concatenate([d_pnorm.reshape(-1), d_scale.reshape(-1), d_final.reshape(-1), d_mnorm.reshape(-1),
                          d_qnorm.reshape(-1), d_kvnorm.reshape(-1), loss_part[0, :1],
                          jnp.zeros((SV_ROWS * SV_COLS - SV_OFF["loss"] - 1,), F32)]).reshape(SV_ROWS, SV_COLS)
    sv_all = all_gather8(sv, name="gather_small_grads", own_half=False)
    sv_sum = sum_devices(sv_all, name="sum_small_grads").reshape(-1)
    loss = sv_sum[SV_OFF["loss"]]

    def sv_take(key, n):
        return lax.slice_in_dim(sv_sum, SV_OFF[key], SV_OFF[key] + n)

    out_g, out_d, out_m, out_v = {}, {}, {}, {}
    for name, part in zip(big_names, red_parts):
        shp = big_w[name].shape
        g2 = part.reshape(shp)
        two_d = (-1, shp[-1])
        d_, m_, v_ = adamw(big_w[name].reshape(two_d), g2.reshape(two_d), big_m[name].reshape(two_d),
                           big_v[name].reshape(two_d), name="adamw_" + name)
        out_g[name], out_d[name], out_m[name], out_v[name] = g2, d_.reshape(shp), m_.reshape(shp), v_.reshape(shp)

    small = [
        ("pool_norm", pool_norm, m_pool_norm, v_pool_norm, sv_take("pool_norm", 1024)),
        ("pool_scale", pool_scale, m_pool_scale, v_pool_scale, sv_take("pool_scale", 2048)),
        ("final_norm", final_norm, m_final_norm, v_final_norm, sv_take("final_norm", 1024)),
        ("mla_norm", mla_norm, m_mla_norm, v_mla_norm,
         lax.dynamic_slice_in_dim(sv_take("mla_norm", 1024), chip * 256, 256)),
        ("mla_q_norm", mla_q_norm, m_mla_q_norm, v_mla_q_norm,
         lax.dynamic_slice_in_dim(sv_take("q_norm", 384), chip * 96, 96)),
        ("mla_kv_norm", mla_kv_norm, m_mla_kv_norm, v_mla_kv_norm,
         lax.dynamic_slice_in_dim(sv_take("kv_norm", 256), chip * 64, 64)),
    ]
    sw = jnp.concatenate([t[1].reshape(-1) for t in small] + [jnp.zeros((96,), F32)]).reshape(1, -1)
    sm = jnp.concatenate([t[2].reshape(-1) for t in small] + [jnp.zeros((96,), F32)]).reshape(1, -1)
    s_v = jnp.concatenate([t[3].reshape(-1) for t in small] + [jnp.ones((96,), F32)]).reshape(1, -1)
    sg = jnp.concatenate([t[4].reshape(-1) for t in small] + [jnp.zeros((96,), F32)]).reshape(1, -1)
    sd_, sm_, sv_ = adamw(sw, sg, sm, s_v, name="adamw_vectors")
    off = 0
    for name, wt, _, _, gvec in small:
        n = gvec.shape[0]
        shp = wt.shape
        out_g[name] = gvec.reshape(shp)
        out_d[name] = sd_[0, off:off + n].reshape(shp)
        out_m[name] = sm_[0, off:off + n].reshape(shp)
        out_v[name] = sv_[0, off:off + n].reshape(shp)
        off += n

    order = ("pool_norm", "pool_w_in", "pool_w_group", "pool_scale", "pool_w_out", "mla_norm", "mla_w_in",
             "mla_q_norm", "mla_w_q_b", "mla_kv_norm", "mla_w_kv_b", "mla_w_out", "final_norm")
    return (loss, grad_x.reshape(x.shape), *[out_g[n] for n in order], *[out_d[n] for n in order],
            *[out_m[n] for n in order], *[out_v[n] for n in order])
```
